```python
import math
import jax, jax.numpy as jnp
from jax import lax
import numpy as np

D_MODEL = 1024
BATCH = 2
SEQ = 8192
DEPTH = 1

MEM_LEN = 256
ROPE_THETA = 10000.0
RMS_EPS = 1e-6
Q_BLOCK = 128

MLA_HEADS = 8
MLA_NOPE = 64
MLA_ROPE = 32
MLA_V = 64
MLA_Q_RANK = 256
MLA_KV_RANK = 128

SWA_HEADS = 8
SWA_KV_HEADS = 2
SWA_HEAD_DIM = 64
SWA_WINDOW = 128

XA_HEADS = 4
XA_HEAD_DIM = 128

N_EXPERTS = 32
TOP_K = 4
D_EXPERT = 1024
SWIGLU_ALPHA = 1.702
SWIGLU_LIMIT = 7.0
MOE_BLOCK = 128

N_BRANCHES = 3

COL_MLA_Q = MLA_Q_RANK
COL_MLA_KV = MLA_KV_RANK
COL_MLA_KR = MLA_ROPE
COL_SWA_Q = SWA_HEADS * SWA_HEAD_DIM
COL_SWA_K = SWA_KV_HEADS * SWA_HEAD_DIM
COL_SWA_V = SWA_KV_HEADS * SWA_HEAD_DIM
COL_XA_Q = XA_HEADS * XA_HEAD_DIM
COL_GATES = N_BRANCHES * D_MODEL
IN_SPLITS = (
    COL_MLA_Q,
    COL_MLA_Q + COL_MLA_KV,
    COL_MLA_Q + COL_MLA_KV + COL_MLA_KR,
    COL_MLA_Q + COL_MLA_KV + COL_MLA_KR + COL_SWA_Q,
    COL_MLA_Q + COL_MLA_KV + COL_MLA_KR + COL_SWA_Q + COL_SWA_K,
    COL_MLA_Q + COL_MLA_KV + COL_MLA_KR + COL_SWA_Q + COL_SWA_K + COL_SWA_V,
    COL_MLA_Q + COL_MLA_KV + COL_MLA_KR + COL_SWA_Q + COL_SWA_K + COL_SWA_V + COL_XA_Q,
)
D_IN_PROJ = COL_MLA_Q + COL_MLA_KV + COL_MLA_KR + COL_SWA_Q + COL_SWA_K + COL_SWA_V + COL_XA_Q + COL_GATES

kernel_name = "hybrid_mla_swa_xattn_moe_layer"


def rms_norm(x, g):
    xf = x.astype(jnp.float32)
    y = xf * lax.rsqrt(jnp.mean(xf * xf, axis=-1, keepdims=True) + RMS_EPS)
    return (y * g.astype(jnp.float32)).astype(x.dtype)


def rope(x, positions):
    d = x.shape[-1]
    inv_freq = ROPE_THETA ** (-jnp.arange(0, d, 2, dtype=jnp.float32) / d)
    ang = positions.astype(jnp.float32)[..., None] * inv_freq
    cos = jnp.cos(ang)[:, :, None, :]
    sin = jnp.sin(ang)[:, :, None, :]
    xf = x.astype(jnp.float32)
    x1, x2 = xf[..., : d // 2], xf[..., d // 2:]
    out = jnp.concatenate([x1 * cos - x2 * sin, x2 * cos + x1 * sin], axis=-1)
    return out.astype(x.dtype)


def mla_attention(q, k, v):
    b, s, h, dq = q.shape
    dv = v.shape[-1]
    nb = s // Q_BLOCK
    scale = 1.0 / math.sqrt(dq)
    qb = q.reshape(b, nb, Q_BLOCK, h, dq).transpose(1, 0, 2, 3, 4)
    kpos = jnp.arange(s)

    def block(args):
        q_blk, start = args
        sc = jnp.einsum('bqhd,bkhd->bhqk', q_blk, k).astype(jnp.float32) * scale
        qpos = start + jnp.arange(Q_BLOCK)
        sc = jnp.where(kpos[None, :] <= qpos[:, None], sc, -jnp.inf)
        p = jax.nn.softmax(sc, axis=-1).astype(v.dtype)
        return jnp.einsum('bhqk,bkhd->bqhd', p, v)

    o = lax.map(block, (qb, jnp.arange(nb) * Q_BLOCK))
    return o.transpose(1, 0, 2, 3, 4).reshape(b, s, h * dv)


def swa_attention(q, k, v, sinks):
    b, s, hq, dh = q.shape
    hkv = k.shape[2]
    grp = hq // hkv
    w = SWA_WINDOW
    nb = s // w
    scale = 1.0 / math.sqrt(dh)
    qb = q.reshape(b, nb, w, hkv, grp, dh)
    kb = k.reshape(b, nb, w, hkv, dh)
    vb = v.reshape(b, nb, w, hkv, dh)
    zpad = jnp.zeros_like(kb[:, :1])
    kk = jnp.concatenate([jnp.concatenate([zpad, kb[:, :-1]], axis=1), kb], axis=2)
    vv = jnp.concatenate([jnp.concatenate([zpad, vb[:, :-1]], axis=1), vb], axis=2)
    sc = jnp.einsum('bnqhgd,bnkhd->bnhgqk', qb, kk).astype(jnp.float32) * scale
    qi = jnp.arange(w)[:, None]
    kj = jnp.arange(2 * w)[None, :]
    diff = qi + w - kj
    band = (diff >= 0) & (diff < w)
    blk_ok = (jnp.arange(nb)[:, None, None] > 0) | (kj >= w)[None]
    mask = band[None] & blk_ok
    sc = jnp.where(mask[None, :, None, None], sc, -jnp.inf)
    sink = jnp.broadcast_to(sinks.astype(jnp.float32).reshape(1, 1, hkv, grp, 1, 1),
                            sc.shape[:-1] + (1,))
    p = jax.nn.softmax(jnp.concatenate([sc, sink], axis=-1), axis=-1)[..., :-1]
    o = jnp.einsum('bnhgqk,bnkhd->bnqhgd', p.astype(v.dtype), vv)
    return o.reshape(b, s, hq * dh)


def cross_attention(q, km, vm):
    scale = 1.0 / math.sqrt(q.shape[-1])
    sc = jnp.einsum('bshd,bmhd->bhsm', q, km).astype(jnp.float32) * scale
    p = jax.nn.softmax(sc, axis=-1).astype(vm.dtype)
    o = jnp.einsum('bhsm,bmhd->bshd', p, vm)
    return o.reshape(q.shape[0], q.shape[1], -1)


def moe_ffn(xt, w_router, b_router, w_gate_up, b_gate_up, w_down, b_down):
    t, d = xt.shape
    tk = t * TOP_K
    logits = (xt @ w_router + b_router).astype(jnp.float32)
    top_vals, top_idx = lax.top_k(logits, TOP_K)
    gates = jax.nn.softmax(top_vals, axis=-1).astype(xt.dtype)
    flat_e = top_idx.reshape(-1)
    order = jnp.argsort(flat_e)
    sorted_e = flat_e[order]
    counts = jax.ops.segment_sum(jnp.ones((tk,), jnp.int32), flat_e, num_segments=N_EXPERTS)
    padded = ((counts + MOE_BLOCK - 1) // MOE_BLOCK) * MOE_BLOCK
    padded_end = jnp.cumsum(padded)
    padded_start = padded_end - padded
    group_start = jnp.cumsum(counts) - counts
    rank = jnp.arange(tk) - group_start[sorted_e]
    dest = padded_start[sorted_e] + rank
    n_blocks = -(-tk // MOE_BLOCK) + N_EXPERTS
    n_rows = n_blocks * MOE_BLOCK
    token_of_row = jnp.full((n_rows,), t, jnp.int32).at[dest].set((order // TOP_K).astype(jnp.int32))
    weight_of_row = jnp.zeros((n_rows,), xt.dtype).at[dest].set(gates.reshape(-1)[order])
    block_expert = jnp.clip(jnp.searchsorted(padded_end, jnp.arange(n_blocks) * MOE_BLOCK, side='right'),
                            0, N_EXPERTS - 1)
    xpad = jnp.concatenate([xt, jnp.zeros((1, d), xt.dtype)], axis=0)

    def expert_block(args):
        rows, e = args
        xb = xpad[rows]
        gu = xb @ w_gate_up[e] + b_gate_up[e]
        x_glu = jnp.minimum(gu[:, :D_EXPERT], SWIGLU_LIMIT)
        x_lin = jnp.clip(gu[:, D_EXPERT:], -SWIGLU_LIMIT, SWIGLU_LIMIT)
        hdn = x_glu * jax.nn.sigmoid(SWIGLU_ALPHA * x_glu) * (x_lin + 1.0)
        return hdn @ w_down[e] + b_down[e]

    y = lax.map(expert_block, (token_of_row.reshape(n_blocks, MOE_BLOCK), block_expert))
    y = y.reshape(n_rows, d) * weight_of_row[:, None]
    out = jnp.zeros((t + 1, d), xt.dtype).at[token_of_row].add(y)
    return out[:t]


def setup_inputs(seed: int = 0) -> dict:
    key = jax.random.key(seed)
    ks = jax.random.split(key, 32)
    f32 = jnp.float32

    def w(k, shape, fan_in):
        return jax.random.normal(k, shape, f32) * (fan_in ** -0.5)

    def gain(k, shape):
        return 1.0 + 0.05 * jax.random.normal(k, shape, f32)

    def bias(k, shape):
        return 0.01 * jax.random.normal(k, shape, f32)

    L = DEPTH
    return {
        "x": jax.random.normal(ks[0], (BATCH, SEQ, D_MODEL), f32),
        "mem": jax.random.normal(ks[1], (BATCH, MEM_LEN, D_MODEL), f32),
        "positions": jnp.broadcast_to(jnp.arange(SEQ, dtype=jnp.int32), (BATCH, SEQ)),
        "g_mix": gain(ks[2], (L, D_MODEL)),
        "w_in": w(ks[3], (L, D_MODEL, D_IN_PROJ), D_MODEL),
        "g_mla_q": gain(ks[4], (L, MLA_Q_RANK)),
        "w_mla_uq": w(ks[5], (L, MLA_Q_RANK, MLA_HEADS * (MLA_NOPE + MLA_ROPE)), MLA_Q_RANK),
        "g_mla_kv": gain(ks[6], (L, MLA_KV_RANK)),
        "w_mla_ukv": w(ks[7], (L, MLA_KV_RANK, MLA_HEADS * (MLA_NOPE + MLA_V)), MLA_KV_RANK),
        "w_mla_o": w(ks[8], (L, MLA_HEADS * MLA_V, D_MODEL), MLA_HEADS * MLA_V),
        "swa_sinks": 0.5 * jax.random.normal(ks[9], (L, SWA_HEADS), f32),
        "w_swa_o": w(ks[10], (L, SWA_HEADS * SWA_HEAD_DIM, D_MODEL), SWA_HEADS * SWA_HEAD_DIM),
        "g_mem": gain(ks[11], (L, D_MODEL)),
        "w_mem_kv": w(ks[12], (L, D_MODEL, 2 * XA_HEADS * XA_HEAD_DIM), D_MODEL),
        "w_xa_o": w(ks[13], (L, XA_HEADS * XA_HEAD_DIM, D_MODEL), XA_HEADS * XA_HEAD_DIM),
        "b_gate": bias(ks[14], (L, COL_GATES)),
        "w_out": w(ks[15], (L, D_MODEL, D_MODEL), D_MODEL),
        "g_ffn": gain(ks[16], (L, D_MODEL)),
        "w_router": w(ks[17], (L, D_MODEL, N_EXPERTS), D_MODEL),
        "b_router": bias(ks[18], (L, N_EXPERTS)),
        "w_gate_up": w(ks[19], (L, N_EXPERTS, D_MODEL, 2 * D_EXPERT), D_MODEL),
        "b_gate_up": bias(ks[20], (L, N_EXPERTS, 2 * D_EXPERT)),
        "w_down": w(ks[21], (L, N_EXPERTS, D_EXPERT, D_MODEL), D_EXPERT),
        "b_down": bias(ks[22], (L, N_EXPERTS, D_MODEL)),
        "g_final": gain(ks[23], (D_MODEL,)),
    }


def reference(x, mem, positions, g_mix, w_in, g_mla_q, w_mla_uq, g_mla_kv, w_mla_ukv, w_mla_o,
              swa_sinks, w_swa_o, g_mem, w_mem_kv, w_xa_o, b_gate, w_out, g_ffn,
              w_router, b_router, w_gate_up, b_gate_up, w_down, b_down, g_final):
    b, s, d = x.shape
    h = x
    for l in range(DEPTH):
        xn = rms_norm(h, g_mix[l])
        proj = xn @ w_in[l]
        cq, ckv, kr, qs, kswa, vswa, qx, gate_logits = jnp.split(proj, IN_SPLITS, axis=-1)

        q_mla = (rms_norm(cq, g_mla_q[l]) @ w_mla_uq[l]).reshape(b, s, MLA_HEADS, MLA_NOPE + MLA_ROPE)
        q_mla = jnp.concatenate([q_mla[..., :MLA_NOPE], rope(q_mla[..., MLA_NOPE:], positions)], axis=-1)
        kv_mla = (rms_norm(ckv, g_mla_kv[l]) @ w_mla_ukv[l]).reshape(b, s, MLA_HEADS, MLA_NOPE + MLA_V)
        k_rope = jnp.broadcast_to(rope(kr[:, :, None, :], positions), (b, s, MLA_HEADS, MLA_ROPE))
        k_mla = jnp.concatenate([kv_mla[..., :MLA_NOPE], k_rope], axis=-1)
        v_mla = kv_mla[..., MLA_NOPE:]
        out_mla = mla_attention(q_mla, k_mla, v_mla) @ w_mla_o[l]

        q_swa = rope(qs.reshape(b, s, SWA_HEADS, SWA_HEAD_DIM), positions)
        k_swa = rope(kswa.reshape(b, s, SWA_KV_HEADS, SWA_HEAD_DIM), positions)
        v_swa = vswa.reshape(b, s, SWA_KV_HEADS, SWA_HEAD_DIM)
        out_swa = swa_attention(q_swa, k_swa, v_swa, swa_sinks[l]) @ w_swa_o[l]

        kv_mem = rms_norm(mem, g_mem[l]) @ w_mem_kv[l]
        m = mem.shape[1]
        k_mem = kv_mem[..., : XA_HEADS * XA_HEAD_DIM].reshape(b, m, XA_HEADS, XA_HEAD_DIM)
        v_mem = kv_mem[..., XA_HEADS * XA_HEAD_DIM:].reshape(b, m, XA_HEADS, XA_HEAD_DIM)
        out_xa = cross_attention(qx.reshape(b, s, XA_HEADS, XA_HEAD_DIM), k_mem, v_mem) @ w_xa_o[l]

        gates = jax.nn.sigmoid((gate_logits + b_gate[l]).astype(jnp.float32)).astype(x.dtype)
        gates = gates.reshape(b, s, N_BRANCHES, d)
        merged = gates[:, :, 0] * out_mla + gates[:, :, 1] * out_swa + gates[:, :, 2] * out_xa
        h = h + merged @ w_out[l]

        hn = rms_norm(h, g_ffn[l]).reshape(b * s, d)
        h = h + moe_ffn(hn, w_router[l], b_router[l], w_gate_up[l], b_gate_up[l],
                        w_down[l], b_down[l]).reshape(b, s, d)
    return rms_norm(h, g_final)
```

```python
import functools
import math

import jax
import jax.numpy as jnp
from jax import lax
from jax.experimental import pallas as pl
from jax.experimental.pallas import tpu as pltpu

F32 = jnp.float32
BF16 = jnp.bfloat16
U32 = jnp.uint32
I32 = jnp.int32

LANES = 128
ROPE_THETA = 10000.0
RMS_EPS = 1e-6
LOG2E = 1.4426950408889634

MLA_HEADS = 8
MLA_NOPE = 64
MLA_ROPE = 32
MLA_V = 64
MLA_Q_RANK = 256
MLA_KV_RANK = 128
SWA_HEADS = 8
SWA_KV_HEADS = 2
SWA_HEAD_DIM = 64
SWA_WINDOW = 128
XA_HEADS = 4
XA_HEAD_DIM = 128
N_EXPERTS = 32
TOP_K = 4
SWIGLU_ALPHA = 1.702
SWIGLU_LIMIT = 7.0
N_BRANCHES = 3

NEG_BIG = -1e30

PROJ_TM = 256
MLA_TQ = 512
SWA_TS = 512
MERGE_TM = 256
FFN_TM = 256
DISPATCH_TM = 256
COMBINE_TM = 256

VMEM_LIMIT = 56 * 1024 * 1024


def _rms(x, g):
    return x * lax.rsqrt(jnp.mean(x * x, axis=-1, keepdims=True) + RMS_EPS) * g


def _rot_cols(w, d):
    k, n = w.shape
    w3 = w.reshape(k, n // d, d)
    return jnp.concatenate([-w3[..., d // 2:], w3[..., : d // 2]], axis=-1).reshape(k, n)


_A0, _A1 = 0, 640
_B0, _B1 = 640, 2048
_C0, _C1 = 2048, 2560
_D0, _D1 = 2560, 5632


def _proj_kernel(x_ref, cq_ref, sq_ref, c64_ref, s64_ref, gmix_ref, win_ref, gq_ref, wqa_ref, wqb_ref,
                 gkv_ref, wk_ref, wv_ref, bgate_ref,
                 qm_ref, km_ref, vm_ref, qs_ref, ksa_ref, ksb_ref, vsa_ref, vsb_ref, qx_ref, gt_ref):
    x = x_ref[...]
    xn = _rms(x, gmix_ref[...]).astype(BF16)
    cosq = cq_ref[...]
    sinq = sq_ref[...]
    cos64 = c64_ref[...]
    sin64 = s64_ref[...]

    xa = jnp.dot(xn, win_ref[:, _A0:_A1], preferred_element_type=F32)
    cqn = _rms(xa[:, 0:256], gq_ref[...]).astype(BF16)
    qa = jnp.dot(cqn, wqa_ref[...], preferred_element_type=F32)
    qb = jnp.dot(cqn, wqb_ref[...], preferred_element_type=F32)
    q_scale = LOG2E / math.sqrt(MLA_NOPE + MLA_ROPE)
    ckvn = _rms(xa[:, 256:384], gkv_ref[...]).astype(BF16)
    ka = jnp.dot(ckvn, wk_ref[...], preferred_element_type=F32)
    krope = xa[:, 384:512] * cosq + xa[:, 512:640] * sinq
    for h in range(MLA_HEADS):
        sl = slice(h * LANES, (h + 1) * LANES)
        qm_ref[:, sl] = ((qa[:, sl] * cosq + qb[:, sl] * sinq) * q_scale).astype(BF16)
        km_ref[:, sl] = (ka[:, sl] + krope).astype(BF16)
    vm_ref[...] = jnp.dot(ckvn, wv_ref[...], preferred_element_type=F32).astype(BF16)

    xb = jnp.dot(xn, win_ref[:, _B0:_B1], preferred_element_type=F32)
    s_scale = LOG2E / math.sqrt(SWA_HEAD_DIM)
    for p in range(SWA_HEADS // 2):
        sl = slice(p * LANES, (p + 1) * LANES)
        sr = slice(512 + p * LANES, 512 + (p + 1) * LANES)
        qs_ref[:, sl] = ((xb[:, sl] * cos64 + xb[:, sr] * sin64) * s_scale).astype(BF16)
    ks = xb[:, 1024:1152] * cos64 + xb[:, 1152:1280] * sin64
    ksa_ref[...] = ks.astype(BF16)
    ksb_ref[...] = pltpu.roll(ks, 64, axis=1).astype(BF16)
    vs = xb[:, 1280:1408]
    vsa_ref[...] = vs.astype(BF16)
    vsb_ref[...] = pltpu.roll(vs, 64, axis=1).astype(BF16)

    xc = jnp.dot(xn, win_ref[:, _C0:_C1], preferred_element_type=F32)
    qx_ref[...] = (xc * (LOG2E / math.sqrt(XA_HEAD_DIM))).astype(BF16)

    xd = jnp.dot(xn, win_ref[:, _D0:_D1], preferred_element_type=F32) + bgate_ref[...]
    gt_ref[...] = jax.nn.sigmoid(xd).astype(BF16)


def _proj_call(x2, cosq, sinq, cos64, sin64, g_mix, w_in_aug, g_q, wq_a, wq_b, g_kv, wk_aug, wv, b_gate):
    t, d = x2.shape
    tm = PROJ_TM
    row = lambda n: pl.BlockSpec((tm, n), lambda i: (i, 0))
    full = lambda a: pl.BlockSpec(a.shape, lambda i: (0,) * a.ndim)
    out_cols = [1024, 1024, 512, 512, 128, 128, 128, 128, 512, 3072]
    return pl.pallas_call(
        _proj_kernel,
        grid=(t // tm,),
        in_specs=[row(d), row(128), row(128), row(128), row(128), full(g_mix),
                  pl.BlockSpec(w_in_aug.shape, lambda i: (0, 0), pipeline_mode=pl.Buffered(1)),
                  full(g_q), full(wq_a), full(wq_b), full(g_kv), full(wk_aug), full(wv), full(b_gate)],
        out_specs=[row(n) for n in out_cols],
        out_shape=[jax.ShapeDtypeStruct((t, n), BF16) for n in out_cols],
        compiler_params=pltpu.CompilerParams(dimension_semantics=("arbitrary",), vmem_limit_bytes=VMEM_LIMIT),
        name="proj",
    )(x2, cosq, sinq, cos64, sin64, g_mix, w_in_aug, g_q, wq_a, wq_b, g_kv, wk_aug, wv, b_gate)


def _mla_kernel(q_ref, k_ref, v_ref, o_ref, m_ref, l_ref, acc_ref, *, tq):
    i = pl.program_id(2)
    m_ref[...] = jnp.full(m_ref.shape, NEG_BIG, F32)
    l_ref[...] = jnp.zeros(l_ref.shape, F32)
    acc_ref[...] = jnp.zeros(acc_ref.shape, F32)

    def step(j, masked):
        k0 = pl.multiple_of(j * tq, tq)
        v = v_ref[0, pl.ds(k0, tq), :]
        for hh in range(2):
            sl = slice(hh * LANES, (hh + 1) * LANES)
            q = q_ref[0, :, sl]
            k = k_ref[0, pl.ds(k0, tq), sl]
            s = lax.dot_general(q, k, (((1,), (1,)), ((), ())), preferred_element_type=F32)
            if masked:
                r = lax.broadcasted_iota(I32, (tq, tq), 0)
                c = lax.broadcasted_iota(I32, (tq, tq), 1)
                s = jnp.where(r >= c, s, NEG_BIG)
            m_old = m_ref[hh]
            m_new = jnp.maximum(m_old, jnp.max(s, axis=1, keepdims=True))
            alpha = jnp.exp2(m_old - m_new)
            p = jnp.exp2(s - m_new)
            l_ref[hh] = alpha * l_ref[hh] + jnp.sum(p, axis=1, keepdims=True)
            acc_ref[hh] = alpha * acc_ref[hh] + jnp.dot(p.astype(BF16), v, preferred_element_type=F32)
            m_ref[hh] = m_new

    def body(j, carry):
        step(j, False)
        return carry

    lax.fori_loop(0, i, body, 0)
    step(i, True)
    o0 = acc_ref[0] / l_ref[0]
    o1 = acc_ref[1] / l_ref[1]
    lane = lax.broadcasted_iota(I32, o0.shape, 1)
    o_ref[0] = jnp.where(lane < MLA_V, o0, o1).astype(BF16)


def _mla_call(q, k, v):
    b, s, _ = q.shape
    tq = min(MLA_TQ, s)
    n_pairs = MLA_HEADS // 2
    return pl.pallas_call(
        functools.partial(_mla_kernel, tq=tq),
        grid=(b, n_pairs, s // tq),
        in_specs=[pl.BlockSpec((1, tq, 2 * LANES), lambda bi, hp, i: (bi, i, hp)),
                  pl.BlockSpec((1, s, 2 * LANES), lambda bi, hp, i: (bi, 0, hp)),
                  pl.BlockSpec((1, s, LANES), lambda bi, hp, i: (bi, 0, hp))],
        out_specs=pl.BlockSpec((1, tq, LANES), lambda bi, hp, i: (bi, i, hp)),
        out_shape=jax.ShapeDtypeStruct((b, s, n_pairs * LANES), BF16),
        scratch_shapes=[pltpu.VMEM((2, tq, 1), F32), pltpu.VMEM((2, tq, 1), F32),
                        pltpu.VMEM((2, tq, LANES), F32)],
        compiler_params=pltpu.CompilerParams(dimension_semantics=("arbitrary",) * 3, vmem_limit_bytes=VMEM_LIMIT),
        name="mla_attn",
    )(q, k, v)


def _swa_kernel(sink_ref, q_ref, ka_ref, kb_ref, va_ref, vb_ref, kah_ref, kbh_ref, vah_ref, vbh_ref, o_ref, *, ts):
    w = SWA_WINDOW
    i = pl.program_id(1)
    ka = jnp.concatenate([kah_ref[0], ka_ref[0]], axis=0)
    kb = jnp.concatenate([kbh_ref[0], kb_ref[0]], axis=0)
    va = jnp.concatenate([vah_ref[0], va_ref[0]], axis=0)
    vb = jnp.concatenate([vbh_ref[0], vb_ref[0]], axis=0)
    lane_k = lax.broadcasted_iota(I32, (2 * w, LANES), 1)
    low = lane_k < SWA_HEAD_DIM
    qi = lax.broadcasted_iota(I32, (2 * w, 2 * w), 0) % w
    kj = lax.broadcasted_iota(I32, (2 * w, 2 * w), 1)
    diff = qi + w - kj
    band = (diff >= 0) & (diff < w)
    lane_o = lax.broadcasted_iota(I32, (w, LANES), 1)
    row2 = lax.broadcasted_iota(I32, (2 * w, 1), 0)
    zero = jnp.zeros((), BF16)
    stacks = ((0, ka, True, va), (1, kb, False, vb), (4, kb, True, vb), (5, ka, False, va))
    for n in range(ts // w):
        mask = band & ((i * (ts // w) + n > 0) | (kj >= w))
        res = []
        for h0, ksrc, keep_low, vsrc in stacks:
            p0 = h0 // 2
            q = jnp.concatenate([q_ref[0, n * w:(n + 1) * w, p0 * LANES:(p0 + 1) * LANES],
                                 q_ref[0, n * w:(n + 1) * w, (p0 + 1) * LANES:(p0 + 2) * LANES]], axis=0)
            kwin = ksrc[n * w:n * w + 2 * w]
            kwin = jnp.where(low if keep_low else ~low, kwin, zero)
            vwin = vsrc[n * w:n * w + 2 * w]
            s = lax.dot_general(q, kwin, (((1,), (1,)), ((), ())), preferred_element_type=F32)
            s = jnp.where(mask, s, NEG_BIG)
            sink = jnp.where(row2 < w, sink_ref[h0], sink_ref[h0 + 2]) * LOG2E
            m = jnp.maximum(jnp.max(s, axis=1, keepdims=True), sink)
            p = jnp.exp2(s - m)
            den = jnp.sum(p, axis=1, keepdims=True) + jnp.exp2(sink - m)
            o = jnp.dot(p.astype(BF16), vwin, preferred_element_type=F32) / den
            res.append(o)
        o02, o13, o46, o57 = res
        sel = lane_o < SWA_HEAD_DIM
        rows = slice(n * w, (n + 1) * w)
        o_ref[0, rows, 0 * LANES:1 * LANES] = jnp.where(sel, o02[:w], o13[:w]).astype(BF16)
        o_ref[0, rows, 1 * LANES:2 * LANES] = jnp.where(sel, o02[w:], o13[w:]).astype(BF16)
        o_ref[0, rows, 2 * LANES:3 * LANES] = jnp.where(sel, o46[:w], o57[:w]).astype(BF16)
        o_ref[0, rows, 3 * LANES:4 * LANES] = jnp.where(sel, o46[w:], o57[w:]).astype(BF16)


def _swa_call(sinks, q, ksa, ksb, vsa, vsb):
    b, s, _ = q.shape
    ts = min(SWA_TS, s)
    w = SWA_WINDOW
    r = ts // w
    main = pl.BlockSpec((1, ts, LANES), lambda bi, i: (bi, i, 0))
    halo = pl.BlockSpec((1, w, LANES), lambda bi, i: (bi, jnp.maximum(i * r - 1, 0), 0))
    return pl.pallas_call(
        functools.partial(_swa_kernel, ts=ts),
        grid=(b, s // ts),
        in_specs=[pl.BlockSpec(memory_space=pltpu.SMEM),
                  pl.BlockSpec((1, ts, 4 * LANES), lambda bi, i: (bi, i, 0)),
                  main, main, main, main, halo, halo, halo, halo],
        out_specs=pl.BlockSpec((1, ts, 4 * LANES), lambda bi, i: (bi, i, 0)),
        out_shape=jax.ShapeDtypeStruct((b, s, 4 * LANES), BF16),
        compiler_params=pltpu.CompilerParams(dimension_semantics=("arbitrary",) * 2, vmem_limit_bytes=VMEM_LIMIT),
        name="swa_attn",
    )(sinks, q, ksa, ksb, vsa, vsb, ksa, ksb, vsa, vsb)


def _memkv_kernel(mem_ref, g_ref, w_ref, o_ref):
    mn = _rms(mem_ref[...], g_ref[...]).astype(BF16)
    o_ref[...] = jnp.dot(mn, w_ref[...], preferred_element_type=F32).astype(BF16)


def _memkv_call(mem2, g_mem, w_mem_kv):
    n, d = mem2.shape
    tm = min(256, n)
    return pl.pallas_call(
        _memkv_kernel,
        grid=(n // tm,),
        in_specs=[pl.BlockSpec((tm, d), lambda i: (i, 0)),
                  pl.BlockSpec(g_mem.shape, lambda i: (0, 0)),
                  pl.BlockSpec(w_mem_kv.shape, lambda i: (0, 0))],
        out_specs=pl.BlockSpec((tm, w_mem_kv.shape[1]), lambda i: (i, 0)),
        out_shape=jax.ShapeDtypeStruct((n, w_mem_kv.shape[1]), BF16),
        compiler_params=pltpu.CompilerParams(dimension_semantics=("arbitrary",), vmem_limit_bytes=VMEM_LIMIT),
        name="mem_kv",
    )(mem2, g_mem, w_mem_kv)


def _merge_kernel(x_ref, omla_ref, oswa_ref, qx_ref, gt_ref, kvm_ref, wmo_ref, wso_ref, wxo_ref, wout_ref,
                  gffn_ref, wr_ref, br_ref,
                  h_ref, hnp_ref, idx_ref, gate_ref, rank_ref, cnt_ref, run_ref, *, tm):
    step = pl.program_id(0)

    @pl.when(step == 0)
    def _():
        run_ref[...] = jnp.zeros(run_ref.shape, F32)

    kv_cols = XA_HEADS * XA_HEAD_DIM
    oxs = []
    for hd in range(XA_HEADS):
        sl = slice(hd * LANES, (hd + 1) * LANES)
        q = qx_ref[:, sl]
        km = kvm_ref[0, :, sl]
        vm = kvm_ref[0, :, kv_cols + hd * LANES:kv_cols + (hd + 1) * LANES]
        s = lax.dot_general(q, km, (((1,), (1,)), ((), ())), preferred_element_type=F32)
        m = jnp.max(s, axis=1, keepdims=True)
        p = jnp.exp2(s - m)
        den = jnp.sum(p, axis=1, keepdims=True)
        oxs.append((jnp.dot(p.astype(BF16), vm, preferred_element_type=F32) / den).astype(BF16))
    oxa = jnp.concatenate(oxs, axis=1)

    d = x_ref.shape[1]
    merged = (gt_ref[:, 0:d].astype(F32) * jnp.dot(omla_ref[...], wmo_ref[...], preferred_element_type=F32)
              + gt_ref[:, d:2 * d].astype(F32) * jnp.dot(oswa_ref[...], wso_ref[...], preferred_element_type=F32)
              + gt_ref[:, 2 * d:3 * d].astype(F32) * jnp.dot(oxa, wxo_ref[...], preferred_element_type=F32))
    h = x_ref[...] + jnp.dot(merged.astype(BF16), wout_ref[...], preferred_element_type=F32)
    h_ref[...] = h

    hn = _rms(h, gffn_ref[...])
    bits = pltpu.bitcast(hn.astype(BF16).astype(F32), U32)
    hnp_ref[...] = (bits[:, : d // 2] >> 16) | (bits[:, d // 2:] & jnp.uint32(0xFFFF0000))

    logits = jnp.dot(hn, wr_ref[...], preferred_element_type=F32, precision=lax.Precision.HIGHEST) + br_ref[...]
    lane = lax.broadcasted_iota(I32, logits.shape, 1)
    work = logits
    vals, idxs, hots = [], [], []
    for _ in range(TOP_K):
        mx = jnp.max(work, axis=1, keepdims=True)
        ix = jnp.min(jnp.where(work == mx, lane, N_EXPERTS), axis=1, keepdims=True)
        hot = lane == ix
        work = jnp.where(hot, -jnp.inf, work)
        vals.append(mx)
        idxs.append(ix)
        hots.append(hot)
    es = [jnp.exp(v - vals[0]) for v in vals]
    den = es[0] + es[1] + es[2] + es[3]
    sel = (hots[0] | hots[1] | hots[2] | hots[3])
    r = lax.broadcasted_iota(I32, (tm, tm), 0)
    c = lax.broadcasted_iota(I32, (tm, tm), 1)
    tri = (r > c).astype(BF16)
    prefix = jnp.dot(tri, sel.astype(BF16), preferred_element_type=F32) + run_ref[...]
    for k in range(TOP_K):
        idx_ref[:, k:k + 1] = idxs[k]
        gate_ref[:, k:k + 1] = es[k] / den
        rank_ref[:, k:k + 1] = jnp.sum(jnp.where(hots[k], prefix, 0.0), axis=1, keepdims=True).astype(I32)
    run_ref[...] = run_ref[...] + jnp.sum(sel.astype(F32), axis=0, keepdims=True)
    cnt_ref[...] = run_ref[...].astype(I32)


def _merge_call(x2, omla, oswa, qx, gates, kvm, wmo, wso, wxo, wout, g_ffn, w_router, b_router, seq):
    t, d = x2.shape
    tm = MERGE_TM
    per_b = seq // tm
    row = lambda n: pl.BlockSpec((tm, n), lambda i: (i, 0))
    full = lambda a: pl.BlockSpec(a.shape, lambda i: (0,) * a.ndim)
    return pl.pallas_call(
        functools.partial(_merge_kernel, tm=tm),
        grid=(t // tm,),
        in_specs=[row(d), row(512), row(512), row(512), row(3 * d),
                  pl.BlockSpec((1,) + kvm.shape[1:], lambda i: (i // per_b, 0, 0)),
                  full(wmo), full(wso), full(wxo), full(wout), full(g_ffn), full(w_router), full(b_router)],
        out_specs=[row(d), row(d // 2), row(TOP_K), row(TOP_K), row(TOP_K),
                   pl.BlockSpec((1, N_EXPERTS), lambda i: (0, 0))],
        out_shape=[jax.ShapeDtypeStruct((t, d), F32), jax.ShapeDtypeStruct((t, d // 2), U32),
                   jax.ShapeDtypeStruct((t, TOP_K), I32), jax.ShapeDtypeStruct((t, TOP_K), F32),
                   jax.ShapeDtypeStruct((t, TOP_K), I32), jax.ShapeDtypeStruct((1, N_EXPERTS), I32)],
        scratch_shapes=[pltpu.VMEM((1, N_EXPERTS), F32)],
        compiler_params=pltpu.CompilerParams(dimension_semantics=("arbitrary",), vmem_limit_bytes=VMEM_LIMIT),
        name="merge_router",
    )(x2, omla, oswa, qx, gates, kvm, wmo, wso, wxo, wout, g_ffn, w_router, b_router)


def _dispatch_kernel(dest_ref, hn_ref, xs_in_ref, xs_ref, sem, *, tm):
    del xs_in_ref
    i = pl.program_id(0)
    n_steps = pl.num_programs(0)
    base = i * tm

    def issue(tt, carry):
        tok = base + tt
        for k in range(TOP_K):
            dst = dest_ref[tok * TOP_K + k]
            pltpu.make_async_copy(hn_ref.at[pl.ds(tok, 1)], xs_ref.at[pl.ds(dst, 1)], sem).start()
        return carry

    lax.fori_loop(0, tm, issue, 0)

    def drain():
        pltpu.make_async_copy(hn_ref.at[pl.ds(0, tm * TOP_K)], xs_ref.at[pl.ds(0, tm * TOP_K)], sem).wait()

    @pl.when(i > 0)
    def _():
        drain()

    @pl.when(i == n_steps - 1)
    def _():
        drain()


def _dispatch_call(dest, hnp, xs_init):
    t, c = hnp.shape
    tm = DISPATCH_TM
    return pl.pallas_call(
        functools.partial(_dispatch_kernel, tm=tm),
        grid_spec=pltpu.PrefetchScalarGridSpec(
            num_scalar_prefetch=1,
            grid=(t // tm,),
            in_specs=[pl.BlockSpec(memory_space=pl.ANY), pl.BlockSpec(memory_space=pl.ANY)],
            out_specs=pl.BlockSpec(memory_space=pl.ANY),
            scratch_shapes=[pltpu.SemaphoreType.DMA(())]),
        out_shape=jax.ShapeDtypeStruct(xs_init.shape, U32),
        input_output_aliases={2: 0},
        compiler_params=pltpu.CompilerParams(dimension_semantics=("arbitrary",), has_side_effects=True),
        name="moe_dispatch",
    )(dest, hnp, xs_init)


def _unpack_lo(w):
    return pltpu.bitcast(w << 16, F32)


def _unpack_hi(w):
    return pltpu.bitcast(w & jnp.uint32(0xFFFF0000), F32)


def _ffn_kernel(te_ref, xs_ref, wgu_ref, bgu_ref, wd_ref, bd_ref, y_ref, wgu_bf, wd_bf):
    i = pl.program_id(0)
    n_used = te_ref[pl.num_programs(0)]
    e = te_ref[i]
    e_prev = te_ref[jnp.maximum(i - 1, 0)]

    @pl.when((i < n_used) & ((i == 0) | (e != e_prev)))
    def _():
        wgu_bf[...] = wgu_ref[0].astype(BF16)
        wd_bf[...] = wd_ref[0].astype(BF16)

    @pl.when(i < n_used)
    def _():
        w = xs_ref[...]
        half = w.shape[1]
        x_lo = _unpack_lo(w).astype(BF16)
        x_hi = _unpack_hi(w).astype(BF16)
        gu = (jnp.dot(x_lo, wgu_bf[0:half, :], preferred_element_type=F32)
              + jnp.dot(x_hi, wgu_bf[half:, :], preferred_element_type=F32) + bgu_ref[0])
        de = gu.shape[1] // 2
        x_glu = jnp.minimum(gu[:, :de], SWIGLU_LIMIT)
        x_lin = jnp.clip(gu[:, de:], -SWIGLU_LIMIT, SWIGLU_LIMIT)
        hdn = x_glu * jax.nn.sigmoid(SWIGLU_ALPHA * x_glu) * (x_lin + 1.0)
        y = jnp.dot(hdn.astype(BF16), wd_bf[...], preferred_element_type=F32) + bd_ref[0]
        bits = pltpu.bitcast(y.astype(BF16).astype(F32), U32)
        y_ref[...] = (bits[:, :half] >> 16) | (bits[:, half:] & jnp.uint32(0xFFFF0000))

    @pl.when(i >= n_used)
    def _():
        y_ref[...] = jnp.zeros(y_ref.shape, U32)


def _ffn_call(tile_expert, xs, w_gate_up, b_gate_up, w_down, b_down):
    r, half = xs.shape
    tm = FFN_TM
    ne, d, de2 = w_gate_up.shape
    return pl.pallas_call(
        _ffn_kernel,
        grid_spec=pltpu.PrefetchScalarGridSpec(
            num_scalar_prefetch=1,
            grid=(r // tm,),
            in_specs=[pl.BlockSpec((tm, half), lambda i, te: (i, 0)),
                      pl.BlockSpec((1, d, de2), lambda i, te: (te[i], 0, 0)),
                      pl.BlockSpec((1, 1, de2), lambda i, te: (te[i], 0, 0)),
                      pl.BlockSpec((1, de2 // 2, d), lambda i, te: (te[i], 0, 0)),
                      pl.BlockSpec((1, 1, d), lambda i, te: (te[i], 0, 0))],
            out_specs=pl.BlockSpec((tm, half), lambda i, te: (i, 0)),
            scratch_shapes=[pltpu.VMEM((d, de2), BF16), pltpu.VMEM((de2 // 2, d), BF16)]),
        out_shape=jax.ShapeDtypeStruct((r, half), U32),
        compiler_params=pltpu.CompilerParams(dimension_semantics=("arbitrary",), vmem_limit_bytes=VMEM_LIMIT),
        name="moe_ffn",
    )(tile_expert, xs, w_gate_up, b_gate_up, w_down, b_down)


def _combine_kernel(dest_ref, h_ref, gate_ref, gfin_ref, y_ref, o_ref, ybuf, sem, *, tm):
    i = pl.program_id(0)
    base = i * tm

    def issue(tt, carry):
        tok = base + tt
        for k in range(TOP_K):
            src = dest_ref[tok * TOP_K + k]
            pltpu.make_async_copy(y_ref.at[pl.ds(src, 1)], ybuf.at[k, pl.ds(tt, 1)], sem).start()
        return carry

    lax.fori_loop(0, tm, issue, 0)
    for k in range(TOP_K):
        pltpu.make_async_copy(y_ref.at[pl.ds(0, tm)], ybuf.at[k], sem).wait()

    half = ybuf.shape[2]
    lo = h_ref[:, :half]
    hi = h_ref[:, half:]
    for k in range(TOP_K):
        g = gate_ref[:, k:k + 1]
        w = ybuf[k]
        lo = lo + g * _unpack_lo(w)
        hi = hi + g * _unpack_hi(w)
    ms = (jnp.sum(lo * lo, axis=1, keepdims=True) + jnp.sum(hi * hi, axis=1, keepdims=True)) / (2 * half)
    inv = lax.rsqrt(ms + RMS_EPS)
    o_ref[:, :half] = lo * inv * gfin_ref[:, :half]
    o_ref[:, half:] = hi * inv * gfin_ref[:, half:]


def _combine_call(dest, h, gate, g_final, y):
    t, d = h.shape
    tm = COMBINE_TM
    return pl.pallas_call(
        functools.partial(_combine_kernel, tm=tm),
        grid_spec=pltpu.PrefetchScalarGridSpec(
            num_scalar_prefetch=1,
            grid=(t // tm,),
            in_specs=[pl.BlockSpec((tm, d), lambda i, ds: (i, 0)),
                      pl.BlockSpec((tm, TOP_K), lambda i, ds: (i, 0)),
                      pl.BlockSpec((1, d), lambda i, ds: (0, 0)),
                      pl.BlockSpec(memory_space=pl.ANY)],
            out_specs=pl.BlockSpec((tm, d), lambda i, ds: (i, 0)),
            scratch_shapes=[pltpu.VMEM((TOP_K, tm, d // 2), U32), pltpu.SemaphoreType.DMA(())]),
        out_shape=jax.ShapeDtypeStruct((t, d), F32),
        compiler_params=pltpu.CompilerParams(dimension_semantics=("arbitrary",), vmem_limit_bytes=VMEM_LIMIT),
        name="moe_combine",
    )(dest, h, gate, g_final, y)


def _rope_tables(positions):
    pos = positions.astype(F32).reshape(-1, 1)

    def cs(dh):
        inv_freq = ROPE_THETA ** (-jnp.arange(0, dh, 2, dtype=F32) / dh)
        ang = pos * inv_freq
        return jnp.cos(ang), jnp.sin(ang)

    c16, s16 = cs(MLA_ROPE)
    c32, s32 = cs(SWA_HEAD_DIM)
    t = pos.shape[0]
    cosq = jnp.concatenate([jnp.ones((t, MLA_NOPE), F32), c16, c16, jnp.zeros((t, 32), F32)], axis=1)
    sinq = jnp.concatenate([jnp.zeros((t, MLA_NOPE), F32), s16, s16, jnp.zeros((t, 32), F32)], axis=1)
    cos64 = jnp.concatenate([c32, c32, c32, c32], axis=1)
    sin64 = jnp.concatenate([s32, s32, s32, s32], axis=1)
    return cosq, sinq, cos64, sin64


def _prep_weights(w_in, w_mla_uq, w_mla_ukv):
    d = w_in.shape[0]
    c0 = MLA_Q_RANK
    c1 = c0 + MLA_KV_RANK
    c2 = c1 + MLA_ROPE
    c3 = c2 + SWA_HEADS * SWA_HEAD_DIM
    c4 = c3 + SWA_KV_HEADS * SWA_HEAD_DIM
    c5 = c4 + SWA_KV_HEADS * SWA_HEAD_DIM
    c6 = c5 + XA_HEADS * XA_HEAD_DIM
    z = lambda n: jnp.zeros((d, n), w_in.dtype)
    kr = w_in[:, c1:c2]
    qs = w_in[:, c2:c3]
    ks = w_in[:, c3:c4]
    w_in_aug = jnp.concatenate([
        w_in[:, :c1],
        z(MLA_NOPE), kr, z(32),
        z(MLA_NOPE), _rot_cols(kr, MLA_ROPE), z(32),
        qs, _rot_cols(qs, SWA_HEAD_DIM),
        ks, _rot_cols(ks, SWA_HEAD_DIM),
        w_in[:, c4:c5],
        w_in[:, c5:c6],
        w_in[:, c6:],
    ], axis=1).astype(BF16)

    r = w_mla_uq.shape[0]
    wq = w_mla_uq.reshape(r, MLA_HEADS, MLA_NOPE + MLA_ROPE)
    wq_nope, wq_rope = wq[..., :MLA_NOPE], wq[..., MLA_NOPE:]
    wq_rot = jnp.concatenate([-wq_rope[..., MLA_ROPE // 2:], wq_rope[..., : MLA_ROPE // 2]], axis=-1)
    zq = jnp.zeros((r, MLA_HEADS, 32), w_mla_uq.dtype)
    wq_a = jnp.concatenate([wq_nope, wq_rope, zq], axis=-1).reshape(r, MLA_HEADS * LANES).astype(BF16)
    wq_b = jnp.concatenate([jnp.zeros_like(wq_nope), wq_rot, zq], axis=-1).reshape(r, MLA_HEADS * LANES).astype(BF16)

    rk = w_mla_ukv.shape[0]
    wkv = w_mla_ukv.reshape(rk, MLA_HEADS, MLA_NOPE + MLA_V)
    wk_aug = jnp.concatenate([wkv[..., :MLA_NOPE], jnp.zeros((rk, MLA_HEADS, LANES - MLA_NOPE), w_mla_ukv.dtype)],
                             axis=-1).reshape(rk, MLA_HEADS * LANES).astype(BF16)
    wv = wkv[..., MLA_NOPE:].reshape(rk, MLA_HEADS * MLA_V).astype(BF16)
    return w_in_aug, wq_a, wq_b, wk_aug, wv


def kernel(x, mem, positions, g_mix, w_in, g_mla_q, w_mla_uq, g_mla_kv, w_mla_ukv, w_mla_o, swa_sinks, w_swa_o,
           g_mem, w_mem_kv, w_xa_o, b_gate, w_out, g_ffn, w_router, b_router, w_gate_up, b_gate_up, w_down,
           b_down, g_final):
    b, s, d = x.shape
    t = b * s
    depth = g_mix.shape[0]
    h = x.reshape(t, d)
    cosq, sinq, cos64, sin64 = _rope_tables(positions)
    for l in range(depth):
        w_in_aug, wq_a, wq_b, wk_aug, wv = _prep_weights(w_in[l], w_mla_uq[l], w_mla_ukv[l])
        (qm, km, vm, qs, ksa, ksb, vsa, vsb, qx, gates) = _proj_call(
            h, cosq, sinq, cos64, sin64, g_mix[l][None], w_in_aug, g_mla_q[l][None], wq_a, wq_b,
            g_mla_kv[l][None], wk_aug, wv, b_gate[l][None])
        r3 = lambda a: a.reshape(b, s, a.shape[1])
        omla = _mla_call(r3(qm), r3(km), r3(vm)).reshape(t, -1)
        oswa = _swa_call(swa_sinks[l], r3(qs), r3(ksa), r3(ksb), r3(vsa), r3(vsb)).reshape(t, -1)
        m = mem.shape[1]
        kvm = _memkv_call(mem.reshape(b * m, d), g_mem[l][None], w_mem_kv[l].astype(BF16)).reshape(b, m, -1)
        h_mid, hnp, idx, gate, rank, counts = _merge_call(
            h, omla, oswa, qx, gates, kvm, w_mla_o[l].astype(BF16), w_swa_o[l].astype(BF16),
            w_xa_o[l].astype(BF16), w_out[l].astype(BF16), g_ffn[l][None], w_router[l], b_router[l][None], s)

        counts = counts[0]
        padded = ((counts + FFN_TM - 1) // FFN_TM) * FFN_TM
        padded_end = jnp.cumsum(padded)
        offsets = padded_end - padded
        dest = (offsets[idx] + rank).reshape(-1).astype(I32)
        n_tiles = (t * TOP_K) // FFN_TM + N_EXPERTS
        tile_start = jnp.arange(n_tiles, dtype=I32) * FFN_TM
        n_used = (padded_end[-1] // FFN_TM).astype(I32)
        tile_expert = jnp.clip(jnp.searchsorted(padded_end, tile_start, side="right"), 0, N_EXPERTS - 1).astype(I32)
        last_e = tile_expert[jnp.maximum(n_used - 1, 0)]
        tile_expert = jnp.where(tile_start // FFN_TM < n_used, tile_expert, last_e)
        te = jnp.concatenate([tile_expert, n_used[None]])

        xs = _dispatch_call(dest, hnp, jnp.zeros((n_tiles * FFN_TM, d // 2), U32))
        y = _ffn_call(te, xs, w_gate_up[l], b_gate_up[l][:, None, :], w_down[l], b_down[l][:, None, :])
        if l == depth - 1:
            gfin = g_final[None]
            out = _combine_call(dest, h_mid, gate, gfin, y)
        else:
            raise NotImplementedError("depth > 1 needs a combine without the final norm")
        h = out
    return h.reshape(b, s, d)
```

```python
import functools
import math

import jax
import jax.numpy as jnp
from jax import lax
from jax.experimental import pallas as pl
from jax.experimental.pallas import tpu as pltpu

F32 = jnp.float32
BF16 = jnp.bfloat16
U32 = jnp.uint32
I32 = jnp.int32

LANES = 128
ROPE_THETA = 10000.0
RMS_EPS = 1e-6
LOG2E = 1.4426950408889634

MLA_HEADS = 8
MLA_NOPE = 64
MLA_ROPE = 32
MLA_V = 64
MLA_Q_RANK = 256
MLA_KV_RANK = 128
SWA_HEADS = 8
SWA_KV_HEADS = 2
SWA_HEAD_DIM = 64
SWA_WINDOW = 128
XA_HEADS = 4
XA_HEAD_DIM = 128
N_EXPERTS = 32
TOP_K = 4
SWIGLU_ALPHA = 1.702
SWIGLU_LIMIT = 7.0
N_BRANCHES = 3

NEG_BIG = -1e30

PROJ_TM = 256
MLA_TQ = 512
SWA_TS = 512
MERGE_TM = 256
FFN_TM = 256
DISPATCH_TM = 256
COMBINE_TM = 256

VMEM_LIMIT = 56 * 1024 * 1024


def _rms(x, g):
    return x * lax.rsqrt(jnp.mean(x * x, axis=-1, keepdims=True) + RMS_EPS) * g


def _rot_cols(w, d):
    k, n = w.shape
    w3 = w.reshape(k, n // d, d)
    return jnp.concatenate([-w3[..., d // 2:], w3[..., : d // 2]], axis=-1).reshape(k, n)


_A0, _A1 = 0, 640
_B0, _B1 = 640, 2048
_C0, _C1 = 2048, 2560
_D0, _D1 = 2560, 5632


def _proj_kernel(x_ref, cq_ref, sq_ref, c64_ref, s64_ref, gmix_ref, win_ref, gq_ref, wqa_ref, wqb_ref,
                 gkv_ref, wk_ref, wv_ref, bgate_ref,
                 qm_ref, km_ref, vm_ref, qs_ref, ksa_ref, ksb_ref, vsa_ref, vsb_ref, qx_ref, gt_ref):
    x = x_ref[...]
    xn = _rms(x, gmix_ref[...]).astype(BF16)
    cosq = cq_ref[...]
    sinq = sq_ref[...]
    cos64 = c64_ref[...]
    sin64 = s64_ref[...]

    xa = jnp.dot(xn, win_ref[:, _A0:_A1], preferred_element_type=F32)
    cqn = _rms(xa[:, 0:256], gq_ref[...]).astype(BF16)
    qa = jnp.dot(cqn, wqa_ref[...], preferred_element_type=F32)
    qb = jnp.dot(cqn, wqb_ref[...], preferred_element_type=F32)
    q_scale = LOG2E / math.sqrt(MLA_NOPE + MLA_ROPE)
    ckvn = _rms(xa[:, 256:384], gkv_ref[...]).astype(BF16)
    ka = jnp.dot(ckvn, wk_ref[...], preferred_element_type=F32)
    krope = xa[:, 384:512] * cosq + xa[:, 512:640] * sinq
    for h in range(MLA_HEADS):
        sl = slice(h * LANES, (h + 1) * LANES)
        qm_ref[:, sl] = ((qa[:, sl] * cosq + qb[:, sl] * sinq) * q_scale).astype(BF16)
        km_ref[:, sl] = (ka[:, sl] + krope).astype(BF16)
    vm_ref[...] = lax.dot_general(wv_ref[...], ckvn, (((1,), (1,)), ((), ())),
                                  preferred_element_type=F32).astype(BF16)

    xb = jnp.dot(xn, win_ref[:, _B0:_B1], preferred_element_type=F32)
    s_scale = LOG2E / math.sqrt(SWA_HEAD_DIM)
    for p in range(SWA_HEADS // 2):
        sl = slice(p * LANES, (p + 1) * LANES)
        sr = slice(512 + p * LANES, 512 + (p + 1) * LANES)
        qs_ref[:, sl] = ((xb[:, sl] * cos64 + xb[:, sr] * sin64) * s_scale).astype(BF16)
    ks = xb[:, 1024:1152] * cos64 + xb[:, 1152:1280] * sin64
    ksa_ref[...] = ks.astype(BF16)
    ksb_ref[...] = pltpu.roll(ks, 64, axis=1).astype(BF16)
    vs = xb[:, 1280:1408]
    vsa_ref[...] = vs.astype(BF16)
    vsb_ref[...] = pltpu.roll(vs, 64, axis=1).astype(BF16)

    xc = jnp.dot(xn, win_ref[:, _C0:_C1], preferred_element_type=F32)
    qx_ref[...] = (xc * (LOG2E / math.sqrt(XA_HEAD_DIM))).astype(BF16)

    xd = jnp.dot(xn, win_ref[:, _D0:_D1], preferred_element_type=F32) + bgate_ref[...]
    gt_ref[...] = jax.nn.sigmoid(xd).astype(BF16)


def _proj_call(x2, cosq, sinq, cos64, sin64, g_mix, w_in_aug, g_q, wq_a, wq_b, g_kv, wk_aug, wv, b_gate):
    t, d = x2.shape
    tm = PROJ_TM
    row = lambda n: pl.BlockSpec((tm, n), lambda i: (i, 0))
    full = lambda a: pl.BlockSpec(a.shape, lambda i: (0,) * a.ndim)
    out_cols = [1024, 1024, 512, 128, 128, 128, 128, 512, 3072]
    out_specs = [row(n) for n in out_cols]
    out_shape = [jax.ShapeDtypeStruct((t, n), BF16) for n in out_cols]
    vt_rows = wv.shape[0]
    out_specs.insert(2, pl.BlockSpec((vt_rows, tm), lambda i: (0, i)))
    out_shape.insert(2, jax.ShapeDtypeStruct((vt_rows, t), BF16))
    return pl.pallas_call(
        _proj_kernel,
        grid=(t // tm,),
        in_specs=[row(d), row(128), row(128), row(128), row(128), full(g_mix),
                  pl.BlockSpec(w_in_aug.shape, lambda i: (0, 0), pipeline_mode=pl.Buffered(1)),
                  full(g_q), full(wq_a), full(wq_b), full(g_kv), full(wk_aug), full(wv), full(b_gate)],
        out_specs=out_specs,
        out_shape=out_shape,
        compiler_params=pltpu.CompilerParams(dimension_semantics=("arbitrary",), vmem_limit_bytes=VMEM_LIMIT),
        name="proj",
    )(x2, cosq, sinq, cos64, sin64, g_mix, w_in_aug, g_q, wq_a, wq_b, g_kv, wk_aug, wv, b_gate)


def _mla_kernel(q_ref, k_ref, vt_ref, o_ref, sa_ref, sb_ref, m_ref, l_ref, acc_ref, *, tq):
    i = pl.program_id(2)
    m_ref[...] = jnp.full(m_ref.shape, NEG_BIG, F32)
    l_ref[...] = jnp.zeros(l_ref.shape, F32)
    acc_ref[...] = jnp.zeros(acc_ref.shape, F32)

    def scores(j, s_ref):
        k0 = pl.multiple_of(j * tq, tq)
        for hh in range(2):
            sl = slice(hh * LANES, (hh + 1) * LANES)
            s_ref[hh] = lax.dot_general(k_ref[0, pl.ds(k0, tq), sl], q_ref[0, :, sl], (((1,), (1,)), ((), ())),
                                        preferred_element_type=F32)

    def update(j, s_ref, masked):
        k0 = pl.multiple_of(j * tq, tq)
        for hh in range(2):
            vt = vt_ref[hh * MLA_V:(hh + 1) * MLA_V, pl.ds(k0, tq)]
            st = s_ref[hh]
            if masked:
                kj = lax.broadcasted_iota(I32, (tq, tq), 0)
                qi = lax.broadcasted_iota(I32, (tq, tq), 1)
                st = jnp.where(kj <= qi, st, NEG_BIG)
            m_old = m_ref[hh]
            m_new = jnp.maximum(m_old, jnp.max(st, axis=0, keepdims=True))
            alpha = jnp.exp2(m_old - m_new)
            pt = jnp.exp2(st - m_new)
            l_ref[hh] = alpha * l_ref[hh] + jnp.sum(pt, axis=0, keepdims=True)
            acc_ref[hh] = alpha * acc_ref[hh] + jnp.dot(vt, pt.astype(BF16), preferred_element_type=F32)
            m_ref[hh] = m_new

    scores(0, sa_ref)

    def body(jj, carry):
        scores(2 * jj + 1, sb_ref)
        update(2 * jj, sa_ref, False)
        scores(2 * jj + 2, sa_ref)
        update(2 * jj + 1, sb_ref, False)
        return carry

    lax.fori_loop(0, i // 2, body, 0)

    @pl.when(i % 2 == 0)
    def _():
        update(i, sa_ref, True)

    @pl.when(i % 2 == 1)
    def _():
        scores(i, sb_ref)
        update(i - 1, sa_ref, False)
        update(i, sb_ref, True)

    ot = jnp.concatenate([acc_ref[0] / l_ref[0], acc_ref[1] / l_ref[1]], axis=0)
    o_ref[0] = ot.T.astype(BF16)


def _mla_call(q, k, vt, seq):
    b, s, _ = q.shape
    assert s == seq
    tq = min(MLA_TQ, s)
    n_pairs = MLA_HEADS // 2
    return pl.pallas_call(
        functools.partial(_mla_kernel, tq=tq),
        grid=(b, n_pairs, s // tq),
        in_specs=[pl.BlockSpec((1, tq, 2 * LANES), lambda bi, hp, i: (bi, i, hp)),
                  pl.BlockSpec((1, s, 2 * LANES), lambda bi, hp, i: (bi, 0, hp)),
                  pl.BlockSpec((2 * MLA_V, s), lambda bi, hp, i: (hp, bi))],
        out_specs=pl.BlockSpec((1, tq, LANES), lambda bi, hp, i: (bi, i, hp)),
        out_shape=jax.ShapeDtypeStruct((b, s, n_pairs * LANES), BF16),
        scratch_shapes=[pltpu.VMEM((2, tq, tq), F32), pltpu.VMEM((2, tq, tq), F32),
                        pltpu.VMEM((2, 1, tq), F32), pltpu.VMEM((2, 1, tq), F32),
                        pltpu.VMEM((2, MLA_V, tq), F32)],
        compiler_params=pltpu.CompilerParams(dimension_semantics=("arbitrary",) * 3, vmem_limit_bytes=VMEM_LIMIT),
        name="mla_attn",
    )(q, k, vt)


def _swa_kernel(sink_ref, q_ref, ka_ref, kb_ref, va_ref, vb_ref, kah_ref, kbh_ref, vah_ref, vbh_ref, o_ref, *, ts):
    w = SWA_WINDOW
    i = pl.program_id(1)
    ka = jnp.concatenate([kah_ref[0], ka_ref[0]], axis=0)
    kb = jnp.concatenate([kbh_ref[0], kb_ref[0]], axis=0)
    va = jnp.concatenate([vah_ref[0], va_ref[0]], axis=0)
    vb = jnp.concatenate([vbh_ref[0], vb_ref[0]], axis=0)
    lane_k = lax.broadcasted_iota(I32, (2 * w, LANES), 1)
    low = lane_k < SWA_HEAD_DIM
    qi = lax.broadcasted_iota(I32, (2 * w, 2 * w), 0) % w
    kj = lax.broadcasted_iota(I32, (2 * w, 2 * w), 1)
    diff = qi + w - kj
    band = (diff >= 0) & (diff < w)
    lane_o = lax.broadcasted_iota(I32, (w, LANES), 1)
    row2 = lax.broadcasted_iota(I32, (2 * w, 1), 0)
    zero = jnp.zeros((), BF16)
    stacks = ((0, ka, True, va), (1, kb, False, vb), (4, kb, True, vb), (5, ka, False, va))
    for n in range(ts // w):
        mask = band & ((i * (ts // w) + n > 0) | (kj >= w))
        res = []
        for h0, ksrc, keep_low, vsrc in stacks:
            p0 = h0 // 2
            q = jnp.concatenate([q_ref[0, n * w:(n + 1) * w, p0 * LANES:(p0 + 1) * LANES],
                                 q_ref[0, n * w:(n + 1) * w, (p0 + 1) * LANES:(p0 + 2) * LANES]], axis=0)
            kwin = ksrc[n * w:n * w + 2 * w]
            kwin = jnp.where(low if keep_low else ~low, kwin, zero)
            vwin = vsrc[n * w:n * w + 2 * w]
            s = lax.dot_general(q, kwin, (((1,), (1,)), ((), ())), preferred_element_type=F32)
            s = jnp.where(mask, s, NEG_BIG)
            sink = jnp.where(row2 < w, sink_ref[h0], sink_ref[h0 + 2]) * LOG2E
            m = jnp.maximum(jnp.max(s, axis=1, keepdims=True), sink)
            p = jnp.exp2(s - m)
            den = jnp.sum(p, axis=1, keepdims=True) + jnp.exp2(sink - m)
            o = jnp.dot(p.astype(BF16), vwin, preferred_element_type=F32) / den
            res.append(o)
        o02, o13, o46, o57 = res
        sel = lane_o < SWA_HEAD_DIM
        rows = slice(n * w, (n + 1) * w)
        o_ref[0, rows, 0 * LANES:1 * LANES] = jnp.where(sel, o02[:w], o13[:w]).astype(BF16)
        o_ref[0, rows, 1 * LANES:2 * LANES] = jnp.where(sel, o02[w:], o13[w:]).astype(BF16)
        o_ref[0, rows, 2 * LANES:3 * LANES] = jnp.where(sel, o46[:w], o57[:w]).astype(BF16)
        o_ref[0, rows, 3 * LANES:4 * LANES] = jnp.where(sel, o46[w:], o57[w:]).astype(BF16)


def _swa_call(sinks, q, ksa, ksb, vsa, vsb):
    b, s, _ = q.shape
    ts = min(SWA_TS, s)
    w = SWA_WINDOW
    r = ts // w
    main = pl.BlockSpec((1, ts, LANES), lambda bi, i: (bi, i, 0))
    halo = pl.BlockSpec((1, w, LANES), lambda bi, i: (bi, jnp.maximum(i * r - 1, 0), 0))
    return pl.pallas_call(
        functools.partial(_swa_kernel, ts=ts),
        grid=(b, s // ts),
        in_specs=[pl.BlockSpec(memory_space=pltpu.SMEM),
                  pl.BlockSpec((1, ts, 4 * LANES), lambda bi, i: (bi, i, 0)),
                  main, main, main, main, halo, halo, halo, halo],
        out_specs=pl.BlockSpec((1, ts, 4 * LANES), lambda bi, i: (bi, i, 0)),
        out_shape=jax.ShapeDtypeStruct((b, s, 4 * LANES), BF16),
        compiler_params=pltpu.CompilerParams(dimension_semantics=("arbitrary",) * 2, vmem_limit_bytes=VMEM_LIMIT),
        name="swa_attn",
    )(sinks, q, ksa, ksb, vsa, vsb, ksa, ksb, vsa, vsb)


def _memkv_kernel(mem_ref, g_ref, w_ref, o_ref):
    mn = _rms(mem_ref[...], g_ref[...]).astype(BF16)
    o_ref[...] = jnp.dot(mn, w_ref[...], preferred_element_type=F32).astype(BF16)


def _memkv_call(mem2, g_mem, w_mem_kv):
    n, d = mem2.shape
    tm = min(256, n)
    return pl.pallas_call(
        _memkv_kernel,
        grid=(n // tm,),
        in_specs=[pl.BlockSpec((tm, d), lambda i: (i, 0)),
                  pl.BlockSpec(g_mem.shape, lambda i: (0, 0)),
                  pl.BlockSpec(w_mem_kv.shape, lambda i: (0, 0))],
        out_specs=pl.BlockSpec((tm, w_mem_kv.shape[1]), lambda i: (i, 0)),
        out_shape=jax.ShapeDtypeStruct((n, w_mem_kv.shape[1]), BF16),
        compiler_params=pltpu.CompilerParams(dimension_semantics=("arbitrary",), vmem_limit_bytes=VMEM_LIMIT),
        name="mem_kv",
    )(mem2, g_mem, w_mem_kv)


def _merge_kernel(x_ref, omla_ref, oswa_ref, qx_ref, gt_ref, kvm_ref, wmo_ref, wso_ref, wxo_ref, wout_ref,
                  gffn_ref, wr_ref, br_ref,
                  h_ref, hnp_ref, idx_ref, gate_ref, rank_ref, cnt_ref, run_ref, *, tm):
    step = pl.program_id(0)

    @pl.when(step == 0)
    def _():
        run_ref[...] = jnp.zeros(run_ref.shape, F32)

    kv_cols = XA_HEADS * XA_HEAD_DIM
    oxs = []
    for hd in range(XA_HEADS):
        sl = slice(hd * LANES, (hd + 1) * LANES)
        q = qx_ref[:, sl]
        km = kvm_ref[0, :, sl]
        vm = kvm_ref[0, :, kv_cols + hd * LANES:kv_cols + (hd + 1) * LANES]
        s = lax.dot_general(q, km, (((1,), (1,)), ((), ())), preferred_element_type=F32)
        m = jnp.max(s, axis=1, keepdims=True)
        p = jnp.exp2(s - m)
        den = jnp.sum(p, axis=1, keepdims=True)
        oxs.append((jnp.dot(p.astype(BF16), vm, preferred_element_type=F32) / den).astype(BF16))
    oxa = jnp.concatenate(oxs, axis=1)

    d = x_ref.shape[1]
    merged = (gt_ref[:, 0:d].astype(F32) * jnp.dot(omla_ref[...], wmo_ref[...], preferred_element_type=F32)
              + gt_ref[:, d:2 * d].astype(F32) * jnp.dot(oswa_ref[...], wso_ref[...], preferred_element_type=F32)
              + gt_ref[:, 2 * d:3 * d].astype(F32) * jnp.dot(oxa, wxo_ref[...], preferred_element_type=F32))
    h = x_ref[...] + jnp.dot(merged.astype(BF16), wout_ref[...], preferred_element_type=F32)
    h_ref[...] = h

    hn = _rms(h, gffn_ref[...])
    bits = pltpu.bitcast(hn.astype(BF16).astype(F32), U32)
    hnp_ref[...] = (bits[:, : d // 2] >> 16) | (bits[:, d // 2:] & jnp.uint32(0xFFFF0000))

    logits = jnp.dot(hn, wr_ref[...], preferred_element_type=F32, precision=lax.Precision.HIGHEST) + br_ref[...]
    lane = lax.broadcasted_iota(I32, logits.shape, 1)
    work = logits
    vals, idxs, hots = [], [], []
    for _ in range(TOP_K):
        mx = jnp.max(work, axis=1, keepdims=True)
        ix = jnp.min(jnp.where(work == mx, lane, N_EXPERTS), axis=1, keepdims=True)
        hot = lane == ix
        work = jnp.where(hot, -jnp.inf, work)
        vals.append(mx)
        idxs.append(ix)
        hots.append(hot)
    es = [jnp.exp(v - vals[0]) for v in vals]
    den = es[0] + es[1] + es[2] + es[3]
    sel = (hots[0] | hots[1] | hots[2] | hots[3])
    r = lax.broadcasted_iota(I32, (tm, tm), 0)
    c = lax.broadcasted_iota(I32, (tm, tm), 1)
    tri = (r > c).astype(BF16)
    prefix = jnp.dot(tri, sel.astype(BF16), preferred_element_type=F32) + run_ref[...]
    for k in range(TOP_K):
        idx_ref[:, k:k + 1] = idxs[k]
        gate_ref[:, k:k + 1] = es[k] / den
        rank_ref[:, k:k + 1] = jnp.sum(jnp.where(hots[k], prefix, 0.0), axis=1, keepdims=True).astype(I32)
    run_ref[...] = run_ref[...] + jnp.sum(sel.astype(F32), axis=0, keepdims=True)
    cnt_ref[...] = run_ref[...].astype(I32)


def _merge_call(x2, omla, oswa, qx, gates, kvm, wmo, wso, wxo, wout, g_ffn, w_router, b_router, seq):
    t, d = x2.shape
    tm = MERGE_TM
    per_b = seq // tm
    row = lambda n: pl.BlockSpec((tm, n), lambda i: (i, 0))
    full = lambda a: pl.BlockSpec(a.shape, lambda i: (0,) * a.ndim)
    return pl.pallas_call(
        functools.partial(_merge_kernel, tm=tm),
        grid=(t // tm,),
        in_specs=[row(d), row(512), row(512), row(512), row(3 * d),
                  pl.BlockSpec((1,) + kvm.shape[1:], lambda i: (i // per_b, 0, 0)),
                  full(wmo), full(wso), full(wxo), full(wout), full(g_ffn), full(w_router), full(b_router)],
        out_specs=[row(d), row(d // 2), row(TOP_K), row(TOP_K), row(TOP_K),
                   pl.BlockSpec((1, N_EXPERTS), lambda i: (0, 0))],
        out_shape=[jax.ShapeDtypeStruct((t, d), F32), jax.ShapeDtypeStruct((t, d // 2), U32),
                   jax.ShapeDtypeStruct((t, TOP_K), I32), jax.ShapeDtypeStruct((t, TOP_K), F32),
                   jax.ShapeDtypeStruct((t, TOP_K), I32), jax.ShapeDtypeStruct((1, N_EXPERTS), I32)],
        scratch_shapes=[pltpu.VMEM((1, N_EXPERTS), F32)],
        compiler_params=pltpu.CompilerParams(dimension_semantics=("arbitrary",), vmem_limit_bytes=VMEM_LIMIT),
        name="merge_router",
    )(x2, omla, oswa, qx, gates, kvm, wmo, wso, wxo, wout, g_ffn, w_router, b_router)


def _dispatch_kernel(dest_ref, hn_ref, xs_in_ref, xs_ref, sem, *, tm):
    del xs_in_ref
    base = pl.program_id(0) * tm

    def issue(tt, carry):
        for k in range(TOP_K):
            dst = dest_ref[(base + tt) * TOP_K + k]
            pltpu.make_async_copy(hn_ref.at[pl.ds(tt, 1)], xs_ref.at[pl.ds(dst, 1)], sem).start()
        return carry

    lax.fori_loop(0, tm, issue, 0)
    for _ in range(TOP_K):
        pltpu.make_async_copy(hn_ref, xs_ref.at[pl.ds(0, tm)], sem).wait()


def _dispatch_call(dest, hnp, xs_init):
    t, c = hnp.shape
    tm = DISPATCH_TM
    return pl.pallas_call(
        functools.partial(_dispatch_kernel, tm=tm),
        grid_spec=pltpu.PrefetchScalarGridSpec(
            num_scalar_prefetch=1,
            grid=(t // tm,),
            in_specs=[pl.BlockSpec((tm, c), lambda i, ds: (i, 0)), pl.BlockSpec(memory_space=pl.ANY)],
            out_specs=pl.BlockSpec(memory_space=pl.ANY),
            scratch_shapes=[pltpu.SemaphoreType.DMA(())]),
        out_shape=jax.ShapeDtypeStruct(xs_init.shape, U32),
        input_output_aliases={2: 0},
        compiler_params=pltpu.CompilerParams(dimension_semantics=("arbitrary",), has_side_effects=True),
        name="moe_dispatch",
    )(dest, hnp, xs_init)


def _unpack_lo(w):
    return pltpu.bitcast(w << 16, F32)


def _unpack_hi(w):
    return pltpu.bitcast(w & jnp.uint32(0xFFFF0000), F32)


def _ffn_kernel(te_ref, xs_ref, wgu_ref, bgu_ref, wd_ref, bd_ref, y_ref, wgu_bf, wd_bf):
    i = pl.program_id(0)
    n_used = te_ref[pl.num_programs(0)]
    e = te_ref[i]
    e_prev = te_ref[jnp.maximum(i - 1, 0)]

    @pl.when((i < n_used) & ((i == 0) | (e != e_prev)))
    def _():
        wgu_bf[...] = wgu_ref[0].astype(BF16)
        wd_bf[...] = wd_ref[0].astype(BF16)

    @pl.when(i < n_used)
    def _():
        w = xs_ref[...]
        half = w.shape[1]
        x_lo = _unpack_lo(w).astype(BF16)
        x_hi = _unpack_hi(w).astype(BF16)
        gu = (jnp.dot(x_lo, wgu_bf[0:half, :], preferred_element_type=F32)
              + jnp.dot(x_hi, wgu_bf[half:, :], preferred_element_type=F32) + bgu_ref[0])
        de = gu.shape[1] // 2
        x_glu = jnp.minimum(gu[:, :de], SWIGLU_LIMIT)
        x_lin = jnp.clip(gu[:, de:], -SWIGLU_LIMIT, SWIGLU_LIMIT)
        hdn = x_glu * jax.nn.sigmoid(SWIGLU_ALPHA * x_glu) * (x_lin + 1.0)
        y = jnp.dot(hdn.astype(BF16), wd_bf[...], preferred_element_type=F32) + bd_ref[0]
        bits = pltpu.bitcast(y.astype(BF16).astype(F32), U32)
        y_ref[...] = (bits[:, :half] >> 16) | (bits[:, half:] & jnp.uint32(0xFFFF0000))

    @pl.when(i >= n_used)
    def _():
        y_ref[...] = jnp.zeros(y_ref.shape, U32)


def _ffn_call(tile_expert, xs, w_gate_up, b_gate_up, w_down, b_down):
    r, half = xs.shape
    tm = FFN_TM
    ne, d, de2 = w_gate_up.shape
    return pl.pallas_call(
        _ffn_kernel,
        grid_spec=pltpu.PrefetchScalarGridSpec(
            num_scalar_prefetch=1,
            grid=(r // tm,),
            in_specs=[pl.BlockSpec((tm, half), lambda i, te: (i, 0)),
                      pl.BlockSpec((1, d, de2), lambda i, te: (te[i], 0, 0)),
                      pl.BlockSpec((1, 1, de2), lambda i, te: (te[i], 0, 0)),
                      pl.BlockSpec((1, de2 // 2, d), lambda i, te: (te[i], 0, 0)),
                      pl.BlockSpec((1, 1, d), lambda i, te: (te[i], 0, 0))],
            out_specs=pl.BlockSpec((tm, half), lambda i, te: (i, 0)),
            scratch_shapes=[pltpu.VMEM((d, de2), BF16), pltpu.VMEM((de2 // 2, d), BF16)]),
        out_shape=jax.ShapeDtypeStruct((r, half), U32),
        compiler_params=pltpu.CompilerParams(dimension_semantics=("arbitrary",), vmem_limit_bytes=VMEM_LIMIT),
        name="moe_ffn",
    )(tile_expert, xs, w_gate_up, b_gate_up, w_down, b_down)


def _combine_kernel(dest_ref, h_ref, gate_ref, gfin_ref, y_ref, o_ref, ybuf, sem, *, tm):
    i = pl.program_id(0)
    base = i * tm

    def issue(tt, carry):
        tok = base + tt
        for k in range(TOP_K):
            src = dest_ref[tok * TOP_K + k]
            pltpu.make_async_copy(y_ref.at[pl.ds(src, 1)], ybuf.at[k, pl.ds(tt, 1)], sem).start()
        return carry

    lax.fori_loop(0, tm, issue, 0)
    for k in range(TOP_K):
        pltpu.make_async_copy(y_ref.at[pl.ds(0, tm)], ybuf.at[k], sem).wait()

    half = ybuf.shape[2]
    lo = h_ref[:, :half]
    hi = h_ref[:, half:]
    for k in range(TOP_K):
        g = gate_ref[:, k:k + 1]
        w = ybuf[k]
        lo = lo + g * _unpack_lo(w)
        hi = hi + g * _unpack_hi(w)
    ms = (jnp.sum(lo * lo, axis=1, keepdims=True) + jnp.sum(hi * hi, axis=1, keepdims=True)) / (2 * half)
    inv = lax.rsqrt(ms + RMS_EPS)
    o_ref[:, :half] = lo * inv * gfin_ref[:, :half]
    o_ref[:, half:] = hi * inv * gfin_ref[:, half:]


def _combine_call(dest, h, gate, g_final, y):
    t, d = h.shape
    tm = COMBINE_TM
    return pl.pallas_call(
        functools.partial(_combine_kernel, tm=tm),
        grid_spec=pltpu.PrefetchScalarGridSpec(
            num_scalar_prefetch=1,
            grid=(t // tm,),
            in_specs=[pl.BlockSpec((tm, d), lambda i, ds: (i, 0)),
                      pl.BlockSpec((tm, TOP_K), lambda i, ds: (i, 0)),
                      pl.BlockSpec((1, d), lambda i, ds: (0, 0)),
                      pl.BlockSpec(memory_space=pl.ANY)],
            out_specs=pl.BlockSpec((tm, d), lambda i, ds: (i, 0)),
            scratch_shapes=[pltpu.VMEM((TOP_K, tm, d // 2), U32), pltpu.SemaphoreType.DMA(())]),
        out_shape=jax.ShapeDtypeStruct((t, d), F32),
        compiler_params=pltpu.CompilerParams(dimension_semantics=("arbitrary",), vmem_limit_bytes=VMEM_LIMIT),
        name="moe_combine",
    )(dest, h, gate, g_final, y)


def _rope_tables(positions):
    pos = positions.astype(F32).reshape(-1, 1)

    def cs(dh):
        inv_freq = ROPE_THETA ** (-jnp.arange(0, dh, 2, dtype=F32) / dh)
        ang = pos * inv_freq
        return jnp.cos(ang), jnp.sin(ang)

    c16, s16 = cs(MLA_ROPE)
    c32, s32 = cs(SWA_HEAD_DIM)
    t = pos.shape[0]
    cosq = jnp.concatenate([jnp.ones((t, MLA_NOPE), F32), c16, c16, jnp.zeros((t, 32), F32)], axis=1)
    sinq = jnp.concatenate([jnp.zeros((t, MLA_NOPE), F32), s16, s16, jnp.zeros((t, 32), F32)], axis=1)
    cos64 = jnp.concatenate([c32, c32, c32, c32], axis=1)
    sin64 = jnp.concatenate([s32, s32, s32, s32], axis=1)
    return cosq, sinq, cos64, sin64


def _prep_weights(w_in, w_mla_uq, w_mla_ukv):
    d = w_in.shape[0]
    c0 = MLA_Q_RANK
    c1 = c0 + MLA_KV_RANK
    c2 = c1 + MLA_ROPE
    c3 = c2 + SWA_HEADS * SWA_HEAD_DIM
    c4 = c3 + SWA_KV_HEADS * SWA_HEAD_DIM
    c5 = c4 + SWA_KV_HEADS * SWA_HEAD_DIM
    c6 = c5 + XA_HEADS * XA_HEAD_DIM
    z = lambda n: jnp.zeros((d, n), w_in.dtype)
    kr = w_in[:, c1:c2]
    qs = w_in[:, c2:c3]
    ks = w_in[:, c3:c4]
    w_in_aug = jnp.concatenate([
        w_in[:, :c1],
        z(MLA_NOPE), kr, z(32),
        z(MLA_NOPE), _rot_cols(kr, MLA_ROPE), z(32),
        qs, _rot_cols(qs, SWA_HEAD_DIM),
        ks, _rot_cols(ks, SWA_HEAD_DIM),
        w_in[:, c4:c5],
        w_in[:, c5:c6],
        w_in[:, c6:],
    ], axis=1).astype(BF16)

    r = w_mla_uq.shape[0]
    wq = w_mla_uq.reshape(r, MLA_HEADS, MLA_NOPE + MLA_ROPE)
    wq_nope, wq_rope = wq[..., :MLA_NOPE], wq[..., MLA_NOPE:]
    wq_rot = jnp.concatenate([-wq_rope[..., MLA_ROPE // 2:], wq_rope[..., : MLA_ROPE // 2]], axis=-1)
    zq = jnp.zeros((r, MLA_HEADS, 32), w_mla_uq.dtype)
    wq_a = jnp.concatenate([wq_nope, wq_rope, zq], axis=-1).reshape(r, MLA_HEADS * LANES).astype(BF16)
    wq_b = jnp.concatenate([jnp.zeros_like(wq_nope), wq_rot, zq], axis=-1).reshape(r, MLA_HEADS * LANES).astype(BF16)

    rk = w_mla_ukv.shape[0]
    wkv = w_mla_ukv.reshape(rk, MLA_HEADS, MLA_NOPE + MLA_V)
    wk_aug = jnp.concatenate([wkv[..., :MLA_NOPE], jnp.zeros((rk, MLA_HEADS, LANES - MLA_NOPE), w_mla_ukv.dtype)],
                             axis=-1).reshape(rk, MLA_HEADS * LANES).astype(BF16)
    wv_t = wkv[..., MLA_NOPE:].reshape(rk, MLA_HEADS * MLA_V).T.astype(BF16)
    return w_in_aug, wq_a, wq_b, wk_aug, wv_t


def kernel(x, mem, positions, g_mix, w_in, g_mla_q, w_mla_uq, g_mla_kv, w_mla_ukv, w_mla_o, swa_sinks, w_swa_o,
           g_mem, w_mem_kv, w_xa_o, b_gate, w_out, g_ffn, w_router, b_router, w_gate_up, b_gate_up, w_down,
           b_down, g_final):
    b, s, d = x.shape
    t = b * s
    depth = g_mix.shape[0]
    h = x.reshape(t, d)
    cosq, sinq, cos64, sin64 = _rope_tables(positions)
    for l in range(depth):
        w_in_aug, wq_a, wq_b, wk_aug, wv_t = _prep_weights(w_in[l], w_mla_uq[l], w_mla_ukv[l])
        (qm, km, vmt, qs, ksa, ksb, vsa, vsb, qx, gates) = _proj_call(
            h, cosq, sinq, cos64, sin64, g_mix[l][None], w_in_aug, g_mla_q[l][None], wq_a, wq_b,
            g_mla_kv[l][None], wk_aug, wv_t, b_gate[l][None])
        r3 = lambda a: a.reshape(b, s, a.shape[1])
        omla = _mla_call(r3(qm), r3(km), vmt, s).reshape(t, -1)
        oswa = _swa_call(swa_sinks[l], r3(qs), r3(ksa), r3(ksb), r3(vsa), r3(vsb)).reshape(t, -1)
        m = mem.shape[1]
        kvm = _memkv_call(mem.reshape(b * m, d), g_mem[l][None], w_mem_kv[l].astype(BF16)).reshape(b, m, -1)
        h_mid, hnp, idx, gate, rank, counts = _merge_call(
            h, omla, oswa, qx, gates, kvm, w_mla_o[l].astype(BF16), w_swa_o[l].astype(BF16),
            w_xa_o[l].astype(BF16), w_out[l].astype(BF16), g_ffn[l][None], w_router[l], b_router[l][None], s)

        counts = counts[0]
        padded = ((counts + FFN_TM - 1) // FFN_TM) * FFN_TM
        padded_end = jnp.cumsum(padded)
        offsets = padded_end - padded
        experts = jnp.arange(N_EXPERTS, dtype=I32)
        dest = (jnp.sum(jnp.where(idx[..., None] == experts, offsets, 0), axis=-1) + rank).reshape(-1).astype(I32)
        n_tiles = (t * TOP_K) // FFN_TM + N_EXPERTS
        n_used = (padded_end[-1] // FFN_TM).astype(I32)
        tile_start = jnp.minimum(jnp.arange(n_tiles, dtype=I32), jnp.maximum(n_used - 1, 0)) * FFN_TM
        tile_expert = jnp.sum((padded_end[None, :] <= tile_start[:, None]).astype(I32), axis=1)
        te = jnp.concatenate([jnp.minimum(tile_expert, N_EXPERTS - 1), n_used[None]])

        xs = _dispatch_call(dest, hnp, jnp.zeros((n_tiles * FFN_TM, d // 2), U32))
        y = _ffn_call(te, xs, w_gate_up[l], b_gate_up[l][:, None, :], w_down[l], b_down[l][:, None, :])
        if l == depth - 1:
            gfin = g_final[None]
            out = _combine_call(dest, h_mid, gate, gfin, y)
        else:
            raise NotImplementedError("depth > 1 needs a combine without the final norm")
        h = out
    return h.reshape(b, s, d)
```

```python
import functools
import math

import jax
import jax.numpy as jnp
from jax import lax
from jax.experimental import pallas as pl
from jax.experimental.pallas import tpu as pltpu

F32 = jnp.float32
BF16 = jnp.bfloat16
U32 = jnp.uint32
I32 = jnp.int32

LANES = 128
ROPE_THETA = 10000.0
RMS_EPS = 1e-6
LOG2E = 1.4426950408889634

MLA_HEADS = 8
MLA_NOPE = 64
MLA_ROPE = 32
MLA_V = 64
MLA_Q_RANK = 256
MLA_KV_RANK = 128
SWA_HEADS = 8
SWA_KV_HEADS = 2
SWA_HEAD_DIM = 64
SWA_WINDOW = 128
XA_HEADS = 4
XA_HEAD_DIM = 128
N_EXPERTS = 32
TOP_K = 4
SWIGLU_ALPHA = 1.702
SWIGLU_LIMIT = 7.0
N_BRANCHES = 3

NEG_BIG = -1e30

PROJ_TM = 256
MLA_TQ = 512
SWA_TS = 512
MERGE_TM = 512
MERGE_SUB = 256
FFN_TM = 256
DISPATCH_TM = 256
COMBINE_TM = 256

VMEM_LIMIT = 56 * 1024 * 1024


def _rms(x, g):
    return x * lax.rsqrt(jnp.mean(x * x, axis=-1, keepdims=True) + RMS_EPS) * g


_A0, _A1 = 0, 640
_B0, _B1 = 640, 2048
_C0, _C1 = 2048, 2560
_D0, _D1 = 2560, 5632


def _proj_kernel(x_ref, cq_ref, sq_ref, c64_ref, s64_ref, gmix_ref, win_ref, gq_ref, wqa_ref, wqb_ref,
                 gkv_ref, wk_ref, wv_ref, bgate_ref,
                 qm_ref, km_ref, vm_ref, qs_ref, ksa_ref, ksb_ref, vsa_ref, vsb_ref, qx_ref, gt_ref):
    x = x_ref[...]
    xn = _rms(x, gmix_ref[...]).astype(BF16)
    cosq = cq_ref[...]
    sinq = sq_ref[...]
    cos64 = c64_ref[...]
    sin64 = s64_ref[...]

    xa = jnp.dot(xn, win_ref[:, _A0:_A1], preferred_element_type=F32)
    cqn = _rms(xa[:, 0:256], gq_ref[...]).astype(BF16)
    qa = jnp.dot(cqn, wqa_ref[...], preferred_element_type=F32)
    qb = jnp.dot(cqn, wqb_ref[...], preferred_element_type=F32)
    q_scale = LOG2E / math.sqrt(MLA_NOPE + MLA_ROPE)
    ckvn = _rms(xa[:, 256:384], gkv_ref[...]).astype(BF16)
    ka = jnp.dot(ckvn, wk_ref[...], preferred_element_type=F32)
    krope = xa[:, 384:512] * cosq + xa[:, 512:640] * sinq
    for h in range(MLA_HEADS):
        sl = slice(h * LANES, (h + 1) * LANES)
        qm_ref[:, sl] = ((qa[:, sl] * cosq + qb[:, sl] * sinq) * q_scale).astype(BF16)
        km_ref[:, sl] = (ka[:, sl] + krope).astype(BF16)
    vm_ref[...] = lax.dot_general(wv_ref[...], ckvn, (((1,), (1,)), ((), ())),
                                  preferred_element_type=F32).astype(BF16)

    xb = jnp.dot(xn, win_ref[:, _B0:_B1], preferred_element_type=F32)
    s_scale = LOG2E / math.sqrt(SWA_HEAD_DIM)
    for p in range(SWA_HEADS // 2):
        sl = slice(p * LANES, (p + 1) * LANES)
        sr = slice(512 + p * LANES, 512 + (p + 1) * LANES)
        qs_ref[:, sl] = ((xb[:, sl] * cos64 + xb[:, sr] * sin64) * s_scale).astype(BF16)
    ks = xb[:, 1024:1152] * cos64 + xb[:, 1152:1280] * sin64
    ksa_ref[...] = ks.astype(BF16)
    ksb_ref[...] = pltpu.roll(ks, 64, axis=1).astype(BF16)
    vs = xb[:, 1280:1408]
    vsa_ref[...] = vs.astype(BF16)
    vsb_ref[...] = pltpu.roll(vs, 64, axis=1).astype(BF16)

    xc = jnp.dot(xn, win_ref[:, _C0:_C1], preferred_element_type=F32)
    qx_ref[...] = (xc * (LOG2E / math.sqrt(XA_HEAD_DIM))).astype(BF16)

    xd = jnp.dot(xn, win_ref[:, _D0:_D1], preferred_element_type=F32) + bgate_ref[...]
    gt_ref[...] = jax.nn.sigmoid(xd).astype(BF16)


def _proj_call(x2, cosq, sinq, cos64, sin64, g_mix, w_in_aug, g_q, wq_a, wq_b, g_kv, wk_aug, wv, b_gate):
    t, d = x2.shape
    tm = PROJ_TM
    row = lambda n: pl.BlockSpec((tm, n), lambda i: (i, 0))
    full = lambda a: pl.BlockSpec(a.shape, lambda i: (0,) * a.ndim)
    out_cols = [1024, 1024, 512, 128, 128, 128, 128, 512, 3072]
    out_specs = [row(n) for n in out_cols]
    out_shape = [jax.ShapeDtypeStruct((t, n), BF16) for n in out_cols]
    vt_rows = wv.shape[0]
    out_specs.insert(2, pl.BlockSpec((vt_rows, tm), lambda i: (0, i)))
    out_shape.insert(2, jax.ShapeDtypeStruct((vt_rows, t), BF16))
    return pl.pallas_call(
        _proj_kernel,
        grid=(t // tm,),
        in_specs=[row(d), row(128), row(128), row(128), row(128), full(g_mix),
                  pl.BlockSpec(w_in_aug.shape, lambda i: (0, 0), pipeline_mode=pl.Buffered(1)),
                  full(g_q), full(wq_a), full(wq_b), full(g_kv), full(wk_aug), full(wv), full(b_gate)],
        out_specs=out_specs,
        out_shape=out_shape,
        compiler_params=pltpu.CompilerParams(dimension_semantics=("arbitrary",), vmem_limit_bytes=VMEM_LIMIT),
        name="proj",
    )(x2, cosq, sinq, cos64, sin64, g_mix, w_in_aug, g_q, wq_a, wq_b, g_kv, wk_aug, wv, b_gate)


def _mla_kernel(q_ref, k_ref, vt_ref, o_ref, sa_ref, sb_ref, m_ref, l_ref, acc_ref, *, tq):
    i = pl.program_id(2)
    m_ref[...] = jnp.full(m_ref.shape, NEG_BIG, F32)
    l_ref[...] = jnp.zeros(l_ref.shape, F32)
    acc_ref[...] = jnp.zeros(acc_ref.shape, F32)

    def scores(j, s_ref):
        k0 = pl.multiple_of(j * tq, tq)
        for hh in range(2):
            sl = slice(hh * LANES, (hh + 1) * LANES)
            s_ref[hh] = lax.dot_general(k_ref[0, pl.ds(k0, tq), sl], q_ref[0, :, sl], (((1,), (1,)), ((), ())),
                                        preferred_element_type=F32)

    def update(j, s_ref, masked):
        k0 = pl.multiple_of(j * tq, tq)
        for hh in range(2):
            vt = vt_ref[hh * MLA_V:(hh + 1) * MLA_V, pl.ds(k0, tq)]
            st = s_ref[hh]
            if masked:
                kj = lax.broadcasted_iota(I32, (tq, tq), 0)
                qi = lax.broadcasted_iota(I32, (tq, tq), 1)
                st = jnp.where(kj <= qi, st, NEG_BIG)
            m_old = m_ref[hh]
            m_new = jnp.maximum(m_old, jnp.max(st, axis=0, keepdims=True))
            alpha = jnp.exp2(m_old - m_new)
            pt = jnp.exp2(st - m_new)
            l_ref[hh] = alpha * l_ref[hh] + jnp.sum(pt, axis=0, keepdims=True)
            acc_ref[hh] = alpha * acc_ref[hh] + jnp.dot(vt, pt.astype(BF16), preferred_element_type=F32)
            m_ref[hh] = m_new

    scores(0, sa_ref)

    def body(jj, carry):
        scores(2 * jj + 1, sb_ref)
        update(2 * jj, sa_ref, False)
        scores(2 * jj + 2, sa_ref)
        update(2 * jj + 1, sb_ref, False)
        return carry

    lax.fori_loop(0, i // 2, body, 0)

    @pl.when(i % 2 == 0)
    def _():
        update(i, sa_ref, True)

    @pl.when(i % 2 == 1)
    def _():
        scores(i, sb_ref)
        update(i - 1, sa_ref, False)
        update(i, sb_ref, True)

    ot = jnp.concatenate([acc_ref[0] / l_ref[0], acc_ref[1] / l_ref[1]], axis=0)
    o_ref[0] = ot.T.astype(BF16)


def _mla_call(q, k, vt, seq):
    b, s, _ = q.shape
    assert s == seq
    tq = min(MLA_TQ, s)
    n_pairs = MLA_HEADS // 2
    return pl.pallas_call(
        functools.partial(_mla_kernel, tq=tq),
        grid=(b, n_pairs, s // tq),
        in_specs=[pl.BlockSpec((1, tq, 2 * LANES), lambda bi, hp, i: (bi, i, hp)),
                  pl.BlockSpec((1, s, 2 * LANES), lambda bi, hp, i: (bi, 0, hp)),
                  pl.BlockSpec((2 * MLA_V, s), lambda bi, hp, i: (hp, bi))],
        out_specs=pl.BlockSpec((1, tq, LANES), lambda bi, hp, i: (bi, i, hp)),
        out_shape=jax.ShapeDtypeStruct((b, s, n_pairs * LANES), BF16),
        scratch_shapes=[pltpu.VMEM((2, tq, tq), F32), pltpu.VMEM((2, tq, tq), F32),
                        pltpu.VMEM((2, 1, tq), F32), pltpu.VMEM((2, 1, tq), F32),
                        pltpu.VMEM((2, MLA_V, tq), F32)],
        compiler_params=pltpu.CompilerParams(dimension_semantics=("arbitrary",) * 3, vmem_limit_bytes=VMEM_LIMIT),
        name="mla_attn",
    )(q, k, vt)


def _swa_kernel(sink_ref, q_ref, ka_ref, kb_ref, va_ref, vb_ref, kah_ref, kbh_ref, vah_ref, vbh_ref, o_ref, *, ts):
    w = SWA_WINDOW
    i = pl.program_id(1)
    ka = jnp.concatenate([kah_ref[0], ka_ref[0]], axis=0)
    kb = jnp.concatenate([kbh_ref[0], kb_ref[0]], axis=0)
    va = jnp.concatenate([vah_ref[0], va_ref[0]], axis=0)
    vb = jnp.concatenate([vbh_ref[0], vb_ref[0]], axis=0)
    lane_k = lax.broadcasted_iota(I32, (2 * w, LANES), 1)
    low = lane_k < SWA_HEAD_DIM
    qi = lax.broadcasted_iota(I32, (2 * w, 2 * w), 0) % w
    kj = lax.broadcasted_iota(I32, (2 * w, 2 * w), 1)
    diff = qi + w - kj
    band = (diff >= 0) & (diff < w)
    lane_o = lax.broadcasted_iota(I32, (w, LANES), 1)
    row2 = lax.broadcasted_iota(I32, (2 * w, 1), 0)
    zero = jnp.zeros((), BF16)
    stacks = ((0, ka, True, va), (1, kb, False, vb), (4, kb, True, vb), (5, ka, False, va))
    for n in range(ts // w):
        mask = band & ((i * (ts // w) + n > 0) | (kj >= w))
        res = []
        for h0, ksrc, keep_low, vsrc in stacks:
            p0 = h0 // 2
            q = jnp.concatenate([q_ref[0, n * w:(n + 1) * w, p0 * LANES:(p0 + 1) * LANES],
                                 q_ref[0, n * w:(n + 1) * w, (p0 + 1) * LANES:(p0 + 2) * LANES]], axis=0)
            kwin = ksrc[n * w:n * w + 2 * w]
            kwin = jnp.where(low if keep_low else ~low, kwin, zero)
            vwin = vsrc[n * w:n * w + 2 * w]
            s = lax.dot_general(q, kwin, (((1,), (1,)), ((), ())), preferred_element_type=F32)
            s = jnp.where(mask, s, NEG_BIG)
            sink = jnp.where(row2 < w, sink_ref[h0], sink_ref[h0 + 2]) * LOG2E
            m = jnp.maximum(jnp.max(s, axis=1, keepdims=True), sink)
            p = jnp.exp2(s - m)
            den = jnp.sum(p, axis=1, keepdims=True) + jnp.exp2(sink - m)
            o = jnp.dot(p.astype(BF16), vwin, preferred_element_type=F32) / den
            res.append(o)
        o02, o13, o46, o57 = res
        sel = lane_o < SWA_HEAD_DIM
        rows = slice(n * w, (n + 1) * w)
        o_ref[0, rows, 0 * LANES:1 * LANES] = jnp.where(sel, o02[:w], o13[:w]).astype(BF16)
        o_ref[0, rows, 1 * LANES:2 * LANES] = jnp.where(sel, o02[w:], o13[w:]).astype(BF16)
        o_ref[0, rows, 2 * LANES:3 * LANES] = jnp.where(sel, o46[:w], o57[:w]).astype(BF16)
        o_ref[0, rows, 3 * LANES:4 * LANES] = jnp.where(sel, o46[w:], o57[w:]).astype(BF16)


def _swa_call(sinks, q, ksa, ksb, vsa, vsb):
    b, s, _ = q.shape
    ts = min(SWA_TS, s)
    w = SWA_WINDOW
    r = ts // w
    main = pl.BlockSpec((1, ts, LANES), lambda bi, i: (bi, i, 0))
    halo = pl.BlockSpec((1, w, LANES), lambda bi, i: (bi, jnp.maximum(i * r - 1, 0), 0))
    return pl.pallas_call(
        functools.partial(_swa_kernel, ts=ts),
        grid=(b, s // ts),
        in_specs=[pl.BlockSpec(memory_space=pltpu.SMEM),
                  pl.BlockSpec((1, ts, 4 * LANES), lambda bi, i: (bi, i, 0)),
                  main, main, main, main, halo, halo, halo, halo],
        out_specs=pl.BlockSpec((1, ts, 4 * LANES), lambda bi, i: (bi, i, 0)),
        out_shape=jax.ShapeDtypeStruct((b, s, 4 * LANES), BF16),
        compiler_params=pltpu.CompilerParams(dimension_semantics=("arbitrary",) * 2, vmem_limit_bytes=VMEM_LIMIT),
        name="swa_attn",
    )(sinks, q, ksa, ksb, vsa, vsb, ksa, ksb, vsa, vsb)


def _memkv_kernel(mem_ref, g_ref, w_ref, o_ref):
    mn = _rms(mem_ref[...], g_ref[...]).astype(BF16)
    o_ref[...] = jnp.dot(mn, w_ref[...], preferred_element_type=F32).astype(BF16)


def _memkv_call(mem2, g_mem, w_mem_kv):
    n, d = mem2.shape
    tm = min(256, n)
    return pl.pallas_call(
        _memkv_kernel,
        grid=(n // tm,),
        in_specs=[pl.BlockSpec((tm, d), lambda i: (i, 0)),
                  pl.BlockSpec(g_mem.shape, lambda i: (0, 0)),
                  pl.BlockSpec(w_mem_kv.shape, lambda i: (0, 0))],
        out_specs=pl.BlockSpec((tm, w_mem_kv.shape[1]), lambda i: (i, 0)),
        out_shape=jax.ShapeDtypeStruct((n, w_mem_kv.shape[1]), BF16),
        compiler_params=pltpu.CompilerParams(dimension_semantics=("arbitrary",), vmem_limit_bytes=VMEM_LIMIT),
        name="mem_kv",
    )(mem2, g_mem, w_mem_kv)


def _merge_kernel(x_ref, omla_ref, oswa_ref, qx_ref, gt_ref, kvm_ref, wmo_ref, wso_ref, wxo_ref, wout_ref,
                  gffn_ref, wr_ref, br_ref,
                  h_ref, hnp_ref, idx_ref, gate_ref, rank_ref, cnt_ref, run_ref, *, tm, sub):
    @pl.when(pl.program_id(0) == 0)
    def _():
        run_ref[...] = jnp.zeros(run_ref.shape, F32)

    d = x_ref.shape[1]
    kv_cols = XA_HEADS * XA_HEAD_DIM
    erow = lax.broadcasted_iota(I32, (N_EXPERTS, sub), 0)
    tri_t = (lax.broadcasted_iota(I32, (sub, sub), 0) < lax.broadcasted_iota(I32, (sub, sub), 1)).astype(BF16)
    nt = (((1,), (1,)), ((), ()))
    run = run_ref[...]
    for hf in range(tm // sub):
        rows = slice(hf * sub, (hf + 1) * sub)

        oxs = []
        for hd in range(XA_HEADS):
            sl = slice(hd * LANES, (hd + 1) * LANES)
            km = kvm_ref[0, :, sl]
            vm = kvm_ref[0, :, kv_cols + hd * LANES:kv_cols + (hd + 1) * LANES]
            s = lax.dot_general(qx_ref[rows, sl], km, nt, preferred_element_type=F32)
            p = jnp.exp2(s - jnp.max(s, axis=1, keepdims=True))
            den = jnp.sum(p, axis=1, keepdims=True)
            oxs.append((jnp.dot(p.astype(BF16), vm, preferred_element_type=F32) / den).astype(BF16))
        oxa = jnp.concatenate(oxs, axis=1)

        merged = (gt_ref[rows, 0:d].astype(F32) * jnp.dot(omla_ref[rows, :], wmo_ref[...], preferred_element_type=F32)
                  + gt_ref[rows, d:2 * d].astype(F32) * jnp.dot(oswa_ref[rows, :], wso_ref[...],
                                                                 preferred_element_type=F32)
                  + gt_ref[rows, 2 * d:3 * d].astype(F32) * jnp.dot(oxa, wxo_ref[...], preferred_element_type=F32))
        h = x_ref[rows, :] + jnp.dot(merged.astype(BF16), wout_ref[...], preferred_element_type=F32)
        h_ref[rows, :] = h

        hn = _rms(h, gffn_ref[...])
        hn_hi = hn.astype(BF16)
        hn_hi32 = hn_hi.astype(F32)
        hn_lo = (hn - hn_hi32).astype(BF16)
        bits = pltpu.bitcast(hn_hi32, U32)
        hnp_ref[rows, :] = (bits[:, : d // 2] >> 16) | (bits[:, d // 2:] & jnp.uint32(0xFFFF0000))

        part = lax.dot_general(wr_ref[...], hn_hi, nt, preferred_element_type=F32)
        logits_t = (part[:N_EXPERTS] + part[N_EXPERTS:]
                    + lax.dot_general(wr_ref[0:N_EXPERTS, :], hn_lo, nt, preferred_element_type=F32) + br_ref[...])

        work = logits_t
        vals, idxs, hots = [], [], []
        for _ in range(TOP_K):
            mx = jnp.max(work, axis=0, keepdims=True)
            ix = jnp.min(jnp.where(work == mx, erow, N_EXPERTS), axis=0, keepdims=True)
            hot = erow == ix
            work = jnp.where(hot, -jnp.inf, work)
            vals.append(mx)
            idxs.append(ix)
            hots.append(hot)
        es = [jnp.exp(v - vals[0]) for v in vals]
        den = es[0] + es[1] + es[2] + es[3]
        sel_t = (hots[0] | hots[1] | hots[2] | hots[3])
        prefix_t = jnp.dot(sel_t.astype(BF16), tri_t, preferred_element_type=F32) + run
        for k in range(TOP_K):
            idx_ref[k:k + 1, rows] = idxs[k]
            gate_ref[k:k + 1, rows] = es[k] / den
            rank_ref[k:k + 1, rows] = jnp.sum(jnp.where(hots[k], prefix_t, 0.0), axis=0, keepdims=True).astype(I32)
        run = run + jnp.sum(sel_t.astype(F32), axis=1, keepdims=True)
    run_ref[...] = run
    cnt_ref[...] = run.astype(I32)


def _merge_call(x2, omla, oswa, qx, gates, kvm, wmo, wso, wxo, wout, g_ffn, wr_split, b_router_col, seq):
    t, d = x2.shape
    tm = MERGE_TM
    per_b = seq // tm
    row = lambda n: pl.BlockSpec((tm, n), lambda i: (i, 0))
    col = lambda: pl.BlockSpec((TOP_K, tm), lambda i: (0, i))
    full = lambda a: pl.BlockSpec(a.shape, lambda i: (0,) * a.ndim)
    return pl.pallas_call(
        functools.partial(_merge_kernel, tm=tm, sub=MERGE_SUB),
        grid=(t // tm,),
        in_specs=[row(d), row(512), row(512), row(512), row(3 * d),
                  pl.BlockSpec((1,) + kvm.shape[1:], lambda i: (i // per_b, 0, 0)),
                  full(wmo), full(wso), full(wxo), full(wout), full(g_ffn), full(wr_split), full(b_router_col)],
        out_specs=[row(d), row(d // 2), col(), col(), col(),
                   pl.BlockSpec((N_EXPERTS, 1), lambda i: (0, 0))],
        out_shape=[jax.ShapeDtypeStruct((t, d), F32), jax.ShapeDtypeStruct((t, d // 2), U32),
                   jax.ShapeDtypeStruct((TOP_K, t), I32), jax.ShapeDtypeStruct((TOP_K, t), F32),
                   jax.ShapeDtypeStruct((TOP_K, t), I32), jax.ShapeDtypeStruct((N_EXPERTS, 1), I32)],
        scratch_shapes=[pltpu.VMEM((N_EXPERTS, 1), F32)],
        compiler_params=pltpu.CompilerParams(dimension_semantics=("arbitrary",), vmem_limit_bytes=VMEM_LIMIT),
        name="merge_router",
    )(x2, omla, oswa, qx, gates, kvm, wmo, wso, wxo, wout, g_ffn, wr_split, b_router_col)


def _dispatch_kernel(dest_ref, ztile_ref, hn_ref, xs_ref, zbuf, sem, zsem, *, tm, n_tok):
    base = pl.program_id(0) * tm

    @pl.when(pl.program_id(0) == 0)
    def _():
        zbuf[...] = jnp.zeros(zbuf.shape, U32)

        def zcopy(e):
            start = pl.multiple_of(ztile_ref[e], FFN_TM)
            return pltpu.make_async_copy(zbuf, xs_ref.at[pl.ds(start, zbuf.shape[0])], zsem)

        for e in range(N_EXPERTS):
            @pl.when(ztile_ref[e] >= 0)
            def _():
                zcopy(e).start()
        for e in range(N_EXPERTS):
            @pl.when(ztile_ref[e] >= 0)
            def _():
                zcopy(e).wait()

        def unused(i):
            start = pl.multiple_of(i * FFN_TM, FFN_TM)
            return pltpu.make_async_copy(zbuf, xs_ref.at[pl.ds(start, zbuf.shape[0])], zsem)

        n_tiles = xs_ref.shape[0] // FFN_TM
        lax.fori_loop(ztile_ref[N_EXPERTS], n_tiles, lambda i, c: (unused(i).start(), c)[1], 0)
        lax.fori_loop(ztile_ref[N_EXPERTS], n_tiles, lambda i, c: (unused(i).wait(), c)[1], 0)

    def issue(tt, carry):
        for k in range(TOP_K):
            dst = dest_ref[k * n_tok + base + tt]
            pltpu.make_async_copy(hn_ref.at[pl.ds(tt, 1)], xs_ref.at[pl.ds(dst, 1)], sem).start()
        return carry

    lax.fori_loop(0, tm, issue, 0)
    for _ in range(TOP_K):
        pltpu.make_async_copy(hn_ref, xs_ref.at[pl.ds(0, tm)], sem).wait()


def _dispatch_call(dest, ztile, hnp, n_rows):
    t, c = hnp.shape
    tm = DISPATCH_TM
    return pl.pallas_call(
        functools.partial(_dispatch_kernel, tm=tm, n_tok=t),
        grid_spec=pltpu.PrefetchScalarGridSpec(
            num_scalar_prefetch=2,
            grid=(t // tm,),
            in_specs=[pl.BlockSpec((tm, c), lambda i, ds, zt: (i, 0))],
            out_specs=pl.BlockSpec(memory_space=pl.ANY),
            scratch_shapes=[pltpu.VMEM((FFN_TM, c), U32), pltpu.SemaphoreType.DMA(()),
                            pltpu.SemaphoreType.DMA(())]),
        out_shape=jax.ShapeDtypeStruct((n_rows, c), U32),
        compiler_params=pltpu.CompilerParams(dimension_semantics=("arbitrary",), has_side_effects=True),
        name="moe_dispatch",
    )(dest, ztile, hnp)


def _unpack_lo(w):
    return pltpu.bitcast(w << 16, F32)


def _unpack_hi(w):
    return pltpu.bitcast(w & jnp.uint32(0xFFFF0000), F32)


def _ffn_kernel(te_ref, xs_ref, wgu_ref, bgu_ref, wd_ref, bd_ref, y_ref, wgu_bf, wd_bf):
    i = pl.program_id(0)
    n_used = te_ref[pl.num_programs(0)]
    e = te_ref[i]
    e_prev = te_ref[jnp.maximum(i - 1, 0)]

    @pl.when((i < n_used) & ((i == 0) | (e != e_prev)))
    def _():
        wgu_bf[...] = wgu_ref[0].astype(BF16)
        wd_bf[...] = wd_ref[0].astype(BF16)

    @pl.when(i < n_used)
    def _():
        w = xs_ref[...]
        half = w.shape[1]
        x_lo = _unpack_lo(w).astype(BF16)
        x_hi = _unpack_hi(w).astype(BF16)
        gu = (jnp.dot(x_lo, wgu_bf[0:half, :], preferred_element_type=F32)
              + jnp.dot(x_hi, wgu_bf[half:, :], preferred_element_type=F32) + bgu_ref[0])
        de = gu.shape[1] // 2
        x_glu = jnp.minimum(gu[:, :de], SWIGLU_LIMIT)
        x_lin = jnp.clip(gu[:, de:], -SWIGLU_LIMIT, SWIGLU_LIMIT)
        hdn = x_glu * jax.nn.sigmoid(SWIGLU_ALPHA * x_glu) * (x_lin + 1.0)
        y = jnp.dot(hdn.astype(BF16), wd_bf[...], preferred_element_type=F32) + bd_ref[0]
        bits = pltpu.bitcast(y.astype(BF16).astype(F32), U32)
        y_ref[...] = (bits[:, :half] >> 16) | (bits[:, half:] & jnp.uint32(0xFFFF0000))

    @pl.when(i >= n_used)
    def _():
        y_ref[...] = jnp.zeros(y_ref.shape, U32)


def _ffn_call(tile_expert, xs, w_gate_up, b_gate_up, w_down, b_down):
    r, half = xs.shape
    tm = FFN_TM
    ne, d, de2 = w_gate_up.shape
    return pl.pallas_call(
        _ffn_kernel,
        grid_spec=pltpu.PrefetchScalarGridSpec(
            num_scalar_prefetch=1,
            grid=(r // tm,),
            in_specs=[pl.BlockSpec((tm, half), lambda i, te: (jnp.minimum(i, jnp.maximum(te[r // tm] - 1, 0)), 0)),
                      pl.BlockSpec((1, d, de2), lambda i, te: (te[i], 0, 0)),
                      pl.BlockSpec((1, 1, de2), lambda i, te: (te[i], 0, 0)),
                      pl.BlockSpec((1, de2 // 2, d), lambda i, te: (te[i], 0, 0)),
                      pl.BlockSpec((1, 1, d), lambda i, te: (te[i], 0, 0))],
            out_specs=pl.BlockSpec((tm, half), lambda i, te: (i, 0)),
            scratch_shapes=[pltpu.VMEM((d, de2), BF16), pltpu.VMEM((de2 // 2, d), BF16)]),
        out_shape=jax.ShapeDtypeStruct((r, half), U32),
        compiler_params=pltpu.CompilerParams(dimension_semantics=("arbitrary",), vmem_limit_bytes=VMEM_LIMIT),
        name="moe_ffn",
    )(tile_expert, xs, w_gate_up, b_gate_up, w_down, b_down)


def _combine_kernel(dest_ref, h_ref, gate_ref, gfin_ref, y_ref, o_ref, ybuf, sem, *, tm):
    i = pl.program_id(0)
    base = i * tm
    n_tok = pl.num_programs(0) * tm

    def issue(tt, carry):
        tok = base + tt
        for k in range(TOP_K):
            src = dest_ref[k * n_tok + tok]
            pltpu.make_async_copy(y_ref.at[pl.ds(src, 1)], ybuf.at[k, pl.ds(tt, 1)], sem).start()
        return carry

    lax.fori_loop(0, tm, issue, 0)
    for k in range(TOP_K):
        pltpu.make_async_copy(y_ref.at[pl.ds(0, tm)], ybuf.at[k], sem).wait()

    half = ybuf.shape[2]
    lo = h_ref[:, :half]
    hi = h_ref[:, half:]
    for k in range(TOP_K):
        g = gate_ref[:, k:k + 1]
        w = ybuf[k]
        lo = lo + g * _unpack_lo(w)
        hi = hi + g * _unpack_hi(w)
    ms = (jnp.sum(lo * lo, axis=1, keepdims=True) + jnp.sum(hi * hi, axis=1, keepdims=True)) / (2 * half)
    inv = lax.rsqrt(ms + RMS_EPS)
    o_ref[:, :half] = lo * inv * gfin_ref[:, :half]
    o_ref[:, half:] = hi * inv * gfin_ref[:, half:]


def _combine_call(dest, h, gate, g_final, y):
    t, d = h.shape
    tm = COMBINE_TM
    return pl.pallas_call(
        functools.partial(_combine_kernel, tm=tm),
        grid_spec=pltpu.PrefetchScalarGridSpec(
            num_scalar_prefetch=1,
            grid=(t // tm,),
            in_specs=[pl.BlockSpec((tm, d), lambda i, ds: (i, 0)),
                      pl.BlockSpec((tm, TOP_K), lambda i, ds: (i, 0)),
                      pl.BlockSpec((1, d), lambda i, ds: (0, 0)),
                      pl.BlockSpec(memory_space=pl.ANY)],
            out_specs=pl.BlockSpec((tm, d), lambda i, ds: (i, 0)),
            scratch_shapes=[pltpu.VMEM((TOP_K, tm, d // 2), U32), pltpu.SemaphoreType.DMA(())]),
        out_shape=jax.ShapeDtypeStruct((t, d), F32),
        compiler_params=pltpu.CompilerParams(dimension_semantics=("arbitrary",), vmem_limit_bytes=VMEM_LIMIT),
        name="moe_combine",
    )(dest, h, gate, g_final, y)


def _rope_tables(positions):
    pos = positions.astype(F32).reshape(-1, 1)

    def cs(dh):
        inv_freq = ROPE_THETA ** (-jnp.arange(0, dh, 2, dtype=F32) / dh)
        ang = pos * inv_freq
        return jnp.cos(ang), jnp.sin(ang)

    c16, s16, c32, s32 = lax.optimization_barrier(cs(MLA_ROPE) + cs(SWA_HEAD_DIM))
    t = pos.shape[0]
    cosq = jnp.concatenate([jnp.ones((t, MLA_NOPE), F32), c16, c16, jnp.zeros((t, 32), F32)], axis=1)
    sinq = jnp.concatenate([jnp.zeros((t, MLA_NOPE), F32), s16, s16, jnp.zeros((t, 32), F32)], axis=1)
    cos64 = jnp.concatenate([c32, c32, c32, c32], axis=1)
    sin64 = jnp.concatenate([s32, s32, s32, s32], axis=1)
    return cosq, sinq, cos64, sin64


def _winprep_kernel(w_ref, o_ref):
    w = w_ref[...]
    rb = w.shape[0]
    c1 = MLA_Q_RANK + MLA_KV_RANK
    tail = w[:, c1 + MLA_ROPE:]
    lane = lax.broadcasted_iota(I32, (rb, LANES), 1)
    in_rope = (lane >= MLA_NOPE) & (lane < MLA_NOPE + MLA_ROPE)
    kr_p = jnp.where(in_rope, pltpu.roll(w[:, c1:c1 + LANES], MLA_NOPE, axis=1), 0.0)
    h16 = MLA_ROPE // 2
    kr_rot = jnp.where(in_rope & (lane < MLA_NOPE + h16), -pltpu.roll(kr_p, LANES - h16, axis=1),
                       jnp.where(in_rope, pltpu.roll(kr_p, h16, axis=1), 0.0))

    def rot64(a):
        n = a.shape[1]
        first = (lax.broadcasted_iota(I32, a.shape, 1) % SWA_HEAD_DIM) < SWA_HEAD_DIM // 2
        return jnp.where(first, -pltpu.roll(a, n - SWA_HEAD_DIM // 2, axis=1), pltpu.roll(a, SWA_HEAD_DIM // 2, axis=1))

    nq = SWA_HEADS * SWA_HEAD_DIM
    nk = SWA_KV_HEADS * SWA_HEAD_DIM
    qs = tail[:, :nq]
    ks = tail[:, nq:nq + nk]
    pieces = [w[:, :c1], kr_p, kr_rot, qs, rot64(qs), ks, rot64(ks), tail[:, nq + nk:]]
    off = 0
    for pc in pieces:
        o_ref[:, off:off + pc.shape[1]] = pc.astype(BF16)
        off += pc.shape[1]


def _winprep_call(w_in):
    d, n = w_in.shape
    rb = 128
    return pl.pallas_call(
        _winprep_kernel,
        grid=(d // rb,),
        in_specs=[pl.BlockSpec((rb, n), lambda i: (i, 0))],
        out_specs=pl.BlockSpec((rb, _D1), lambda i: (i, 0)),
        out_shape=jax.ShapeDtypeStruct((d, _D1), BF16),
        compiler_params=pltpu.CompilerParams(dimension_semantics=("arbitrary",), vmem_limit_bytes=VMEM_LIMIT),
        name="w_in_prep",
    )(w_in)


def _prep_weights(w_in, w_mla_uq, w_mla_ukv):
    w_in_aug = _winprep_call(w_in)

    r = w_mla_uq.shape[0]
    wq = w_mla_uq.reshape(r, MLA_HEADS, MLA_NOPE + MLA_ROPE)
    wq_nope, wq_rope = wq[..., :MLA_NOPE], wq[..., MLA_NOPE:]
    wq_rot = jnp.concatenate([-wq_rope[..., MLA_ROPE // 2:], wq_rope[..., : MLA_ROPE // 2]], axis=-1)
    zq = jnp.zeros((r, MLA_HEADS, 32), w_mla_uq.dtype)
    wq_a = jnp.concatenate([wq_nope, wq_rope, zq], axis=-1).reshape(r, MLA_HEADS * LANES).astype(BF16)
    wq_b = jnp.concatenate([jnp.zeros_like(wq_nope), wq_rot, zq], axis=-1).reshape(r, MLA_HEADS * LANES).astype(BF16)

    rk = w_mla_ukv.shape[0]
    wkv = w_mla_ukv.reshape(rk, MLA_HEADS, MLA_NOPE + MLA_V)
    wk_aug = jnp.concatenate([wkv[..., :MLA_NOPE], jnp.zeros((rk, MLA_HEADS, LANES - MLA_NOPE), w_mla_ukv.dtype)],
                             axis=-1).reshape(rk, MLA_HEADS * LANES).astype(BF16)
    wv_t = wkv[..., MLA_NOPE:].reshape(rk, MLA_HEADS * MLA_V).T.astype(BF16)
    return w_in_aug, wq_a, wq_b, wk_aug, wv_t


def kernel(x, mem, positions, g_mix, w_in, g_mla_q, w_mla_uq, g_mla_kv, w_mla_ukv, w_mla_o, swa_sinks, w_swa_o,
           g_mem, w_mem_kv, w_xa_o, b_gate, w_out, g_ffn, w_router, b_router, w_gate_up, b_gate_up, w_down,
           b_down, g_final):
    b, s, d = x.shape
    t = b * s
    depth = g_mix.shape[0]
    h = x.reshape(t, d)
    cosq, sinq, cos64, sin64 = _rope_tables(positions)
    for l in range(depth):
        w_in_aug, wq_a, wq_b, wk_aug, wv_t = _prep_weights(w_in[l], w_mla_uq[l], w_mla_ukv[l])
        (qm, km, vmt, qs, ksa, ksb, vsa, vsb, qx, gates) = _proj_call(
            h, cosq, sinq, cos64, sin64, g_mix[l][None], w_in_aug, g_mla_q[l][None], wq_a, wq_b,
            g_mla_kv[l][None], wk_aug, wv_t, b_gate[l][None])
        r3 = lambda a: a.reshape(b, s, a.shape[1])
        omla = _mla_call(r3(qm), r3(km), vmt, s).reshape(t, -1)
        oswa = _swa_call(swa_sinks[l], r3(qs), r3(ksa), r3(ksb), r3(vsa), r3(vsb)).reshape(t, -1)
        m = mem.shape[1]
        kvm = _memkv_call(mem.reshape(b * m, d), g_mem[l][None], w_mem_kv[l].astype(BF16)).reshape(b, m, -1)
        wr_t = w_router[l].T
        wr_hi = wr_t.astype(BF16)
        wr_split = jnp.concatenate([wr_hi, (wr_t - wr_hi.astype(F32)).astype(BF16)], axis=0)
        h_mid, hnp, idx, gate, rank, counts = _merge_call(
            h, omla, oswa, qx, gates, kvm, w_mla_o[l].astype(BF16), w_swa_o[l].astype(BF16),
            w_xa_o[l].astype(BF16), w_out[l].astype(BF16), g_ffn[l][None], wr_split, b_router[l][:, None], s)

        counts = counts[:, 0]
        padded = ((counts + FFN_TM - 1) // FFN_TM) * FFN_TM
        padded_end = jnp.cumsum(padded)
        offsets = padded_end - padded
        experts = jnp.arange(N_EXPERTS, dtype=I32)
        dest = (jnp.sum(jnp.where(idx[..., None] == experts, offsets, 0), axis=-1) + rank).reshape(-1).astype(I32)
        n_tiles = (t * TOP_K) // FFN_TM + N_EXPERTS
        n_used = (padded_end[-1] // FFN_TM).astype(I32)
        ztile = jnp.concatenate([jnp.where(counts % FFN_TM != 0, padded_end - FFN_TM, -1).astype(I32), n_used[None]])
        tile_start = jnp.minimum(jnp.arange(n_tiles, dtype=I32), jnp.maximum(n_used - 1, 0)) * FFN_TM
        tile_expert = jnp.sum((padded_end[None, :] <= tile_start[:, None]).astype(I32), axis=1)
        te = jnp.concatenate([jnp.minimum(tile_expert, N_EXPERTS - 1), n_used[None]])

        xs = _dispatch_call(dest, ztile, hnp, n_tiles * FFN_TM)
        y = _ffn_call(te, xs, w_gate_up[l], b_gate_up[l][:, None, :], w_down[l], b_down[l][:, None, :])
        if l == depth - 1:
            gfin = g_final[None]
            out = _combine_call(dest, h_mid, gate.T, gfin, y)
        else:
            raise NotImplementedError("depth > 1 needs a combine without the final norm")
        h = out
    return h.reshape(b, s, d)
```

```python
import functools
import math

import jax
import jax.numpy as jnp
from jax import lax
from jax.experimental import pallas as pl
from jax.experimental.pallas import tpu as pltpu

F32 = jnp.float32
BF16 = jnp.bfloat16
U32 = jnp.uint32
I32 = jnp.int32

LANES = 128
ROPE_THETA = 10000.0
RMS_EPS = 1e-6
LOG2E = 1.4426950408889634

MLA_HEADS = 8
MLA_NOPE = 64
MLA_ROPE = 32
MLA_V = 64
MLA_Q_RANK = 256
MLA_KV_RANK = 128
SWA_HEADS = 8
SWA_KV_HEADS = 2
SWA_HEAD_DIM = 64
SWA_WINDOW = 128
XA_HEADS = 4
XA_HEAD_DIM = 128
N_EXPERTS = 32
TOP_K = 4
SWIGLU_ALPHA = 1.702
SWIGLU_LIMIT = 7.0
N_BRANCHES = 3

NEG_BIG = -1e30

PROJ_TM = 256
MLA_TQ = 512
SWA_TS = 512
MERGE_TM = 512
MERGE_SUB = 256
FFN_TM = 256
DISPATCH_TM = 256
COMBINE_TM = 256

VMEM_LIMIT = 56 * 1024 * 1024


def _rms(x, g):
    return x * lax.rsqrt(jnp.mean(x * x, axis=-1, keepdims=True) + RMS_EPS) * g


_A0, _A1 = 0, 512
_B0, _B1 = 512, 1280
_C0, _C1 = 1280, 1792
_D0, _D1 = 1792, 4864


def _rotate_half(x, d, lo, hi):
    n = x.shape[1]
    half = (hi - lo) // 2
    lane = lax.broadcasted_iota(I32, x.shape, 1) % d
    up = pltpu.roll(x, n - half, axis=1)
    dn = pltpu.roll(x, half, axis=1)
    return jnp.where((lane >= lo) & (lane < lo + half), -up, jnp.where((lane >= lo + half) & (lane < hi), dn, 0.0))


def _proj_kernel(x_ref, pos_ref, fq_ref, f64_ref, gmix_ref, win_ref, gq_ref, wq_ref,
                 gkv_ref, wk_ref, wv_ref, bgate_ref,
                 qm_ref, km_ref, vm_ref, qs_ref, ksa_ref, ksb_ref, vsa_ref, vsb_ref, qx_ref, gt_ref):
    x = x_ref[...]
    xn = _rms(x, gmix_ref[...]).astype(BF16)
    tm = x.shape[0]
    pos = pos_ref[...]
    ang16 = fq_ref[...] * pos
    ang32 = f64_ref[...] * pos
    c16, s16, c32, s32 = jnp.cos(ang16), jnp.sin(ang16), jnp.cos(ang32), jnp.sin(ang32)
    one = jnp.ones((MLA_NOPE, tm), F32)
    zero = jnp.zeros((MLA_NOPE, tm), F32)
    pad = LANES - MLA_NOPE - MLA_ROPE
    cosq = jnp.concatenate([one, c16, c16, one[:pad]], axis=0).T
    sinq = jnp.concatenate([zero, s16, s16, zero[:pad]], axis=0).T
    cos64 = jnp.concatenate([c32, c32, c32, c32], axis=0).T
    sin64 = jnp.concatenate([s32, s32, s32, s32], axis=0).T
    rope_lo, rope_hi = MLA_NOPE, MLA_NOPE + MLA_ROPE

    xa = jnp.dot(xn, win_ref[:, _A0:_A1], preferred_element_type=F32)
    cqn = _rms(xa[:, 0:256], gq_ref[...]).astype(BF16)
    qa = jnp.dot(cqn, wq_ref[...], preferred_element_type=F32)
    qb = _rotate_half(qa, LANES, rope_lo, rope_hi)
    q_scale = LOG2E / math.sqrt(MLA_NOPE + MLA_ROPE)
    ckvn = _rms(xa[:, 256:384], gkv_ref[...]).astype(BF16)
    ka = jnp.dot(ckvn, wk_ref[...], preferred_element_type=F32)
    kr = xa[:, 384:512]
    krope = kr * cosq + _rotate_half(kr, LANES, rope_lo, rope_hi) * sinq
    for h in range(MLA_HEADS):
        sl = slice(h * LANES, (h + 1) * LANES)
        qm_ref[:, sl] = ((qa[:, sl] * cosq + qb[:, sl] * sinq) * q_scale).astype(BF16)
        km_ref[:, sl] = (ka[:, sl] + krope).astype(BF16)
    vm_ref[...] = lax.dot_general(wv_ref[...], ckvn, (((1,), (1,)), ((), ())),
                                  preferred_element_type=F32).astype(BF16)

    xb = jnp.dot(xn, win_ref[:, _B0:_B1], preferred_element_type=F32)
    s_scale = LOG2E / math.sqrt(SWA_HEAD_DIM)
    nq = SWA_HEADS * SWA_HEAD_DIM
    qs = xb[:, :nq]
    qs_rot = _rotate_half(qs, SWA_HEAD_DIM, 0, SWA_HEAD_DIM)
    for p in range(SWA_HEADS // 2):
        sl = slice(p * LANES, (p + 1) * LANES)
        qs_ref[:, sl] = ((qs[:, sl] * cos64 + qs_rot[:, sl] * sin64) * s_scale).astype(BF16)
    ks = xb[:, nq:nq + LANES]
    ks = ks * cos64 + _rotate_half(ks, SWA_HEAD_DIM, 0, SWA_HEAD_DIM) * sin64
    ksa_ref[...] = ks.astype(BF16)
    ksb_ref[...] = pltpu.roll(ks, 64, axis=1).astype(BF16)
    vs = xb[:, nq + LANES:nq + 2 * LANES]
    vsa_ref[...] = vs.astype(BF16)
    vsb_ref[...] = pltpu.roll(vs, 64, axis=1).astype(BF16)

    xc = jnp.dot(xn, win_ref[:, _C0:_C1], preferred_element_type=F32)
    qx_ref[...] = (xc * (LOG2E / math.sqrt(XA_HEAD_DIM))).astype(BF16)

    xd = jnp.dot(xn, win_ref[:, _D0:_D1], preferred_element_type=F32) + bgate_ref[...]
    gt_ref[...] = jax.nn.sigmoid(xd).astype(BF16)


def _proj_call(x2, pos, fq, f64, g_mix, w_in_al, g_q, wq, g_kv, wk_aug, wv, b_gate):
    t, d = x2.shape
    tm = PROJ_TM
    row = lambda n: pl.BlockSpec((tm, n), lambda i: (i, 0))
    full = lambda a: pl.BlockSpec(a.shape, lambda i: (0,) * a.ndim)
    out_cols = [1024, 1024, 512, 128, 128, 128, 128, 512, 3072]
    out_specs = [row(n) for n in out_cols]
    out_shape = [jax.ShapeDtypeStruct((t, n), BF16) for n in out_cols]
    vt_rows = wv.shape[0]
    out_specs.insert(2, pl.BlockSpec((vt_rows, tm), lambda i: (0, i)))
    out_shape.insert(2, jax.ShapeDtypeStruct((vt_rows, t), BF16))
    return pl.pallas_call(
        _proj_kernel,
        grid=(t // tm,),
        in_specs=[row(d), pl.BlockSpec((1, tm), lambda i: (0, i)), full(fq), full(f64), full(g_mix),
                  pl.BlockSpec(w_in_al.shape, lambda i: (0, 0), pipeline_mode=pl.Buffered(1)),
                  full(g_q), full(wq), full(g_kv), full(wk_aug), full(wv), full(b_gate)],
        out_specs=out_specs,
        out_shape=out_shape,
        compiler_params=pltpu.CompilerParams(dimension_semantics=("arbitrary",), vmem_limit_bytes=VMEM_LIMIT),
        name="proj",
    )(x2, pos, fq, f64, g_mix, w_in_al, g_q, wq, g_kv, wk_aug, wv, b_gate)


def _mla_kernel(q_ref, k_ref, vt_ref, o_ref, sa_ref, sb_ref, m_ref, l_ref, acc_ref, *, tq):
    i = pl.program_id(2)
    m_ref[...] = jnp.full(m_ref.shape, NEG_BIG, F32)
    l_ref[...] = jnp.zeros(l_ref.shape, F32)
    acc_ref[...] = jnp.zeros(acc_ref.shape, F32)

    def scores(j, s_ref):
        k0 = pl.multiple_of(j * tq, tq)
        for hh in range(2):
            sl = slice(hh * LANES, (hh + 1) * LANES)
            s_ref[hh] = lax.dot_general(k_ref[0, pl.ds(k0, tq), sl], q_ref[0, :, sl], (((1,), (1,)), ((), ())),
                                        preferred_element_type=F32)

    def update(j, s_ref, masked):
        k0 = pl.multiple_of(j * tq, tq)
        for hh in range(2):
            vt = vt_ref[hh * MLA_V:(hh + 1) * MLA_V, pl.ds(k0, tq)]
            st = s_ref[hh]
            if masked:
                kj = lax.broadcasted_iota(I32, (tq, tq), 0)
                qi = lax.broadcasted_iota(I32, (tq, tq), 1)
                st = jnp.where(kj <= qi, st, NEG_BIG)
            m_old = m_ref[hh]
            m_new = jnp.maximum(m_old, jnp.max(st, axis=0, keepdims=True))
            alpha = jnp.exp2(m_old - m_new)
            pt = jnp.exp2(st - m_new)
            l_ref[hh] = alpha * l_ref[hh] + jnp.sum(pt, axis=0, keepdims=True)
            acc_ref[hh] = alpha * acc_ref[hh] + jnp.dot(vt, pt.astype(BF16), preferred_element_type=F32)
            m_ref[hh] = m_new

    scores(0, sa_ref)

    def body(jj, carry):
        scores(2 * jj + 1, sb_ref)
        update(2 * jj, sa_ref, False)
        scores(2 * jj + 2, sa_ref)
        update(2 * jj + 1, sb_ref, False)
        return carry

    lax.fori_loop(0, i // 2, body, 0)

    @pl.when(i % 2 == 0)
    def _():
        update(i, sa_ref, True)

    @pl.when(i % 2 == 1)
    def _():
        scores(i, sb_ref)
        update(i - 1, sa_ref, False)
        update(i, sb_ref, True)

    ot = jnp.concatenate([acc_ref[0] / l_ref[0], acc_ref[1] / l_ref[1]], axis=0)
    o_ref[0] = ot.T.astype(BF16)


def _mla_call(q, k, vt, seq):
    b, s, _ = q.shape
    assert s == seq
    tq = min(MLA_TQ, s)
    n_pairs = MLA_HEADS // 2
    return pl.pallas_call(
        functools.partial(_mla_kernel, tq=tq),
        grid=(b, n_pairs, s // tq),
        in_specs=[pl.BlockSpec((1, tq, 2 * LANES), lambda bi, hp, i: (bi, i, hp)),
                  pl.BlockSpec((1, s, 2 * LANES), lambda bi, hp, i: (bi, 0, hp)),
                  pl.BlockSpec((2 * MLA_V, s), lambda bi, hp, i: (hp, bi))],
        out_specs=pl.BlockSpec((1, tq, LANES), lambda bi, hp, i: (bi, i, hp)),
        out_shape=jax.ShapeDtypeStruct((b, s, n_pairs * LANES), BF16),
        scratch_shapes=[pltpu.VMEM((2, tq, tq), F32), pltpu.VMEM((2, tq, tq), F32),
                        pltpu.VMEM((2, 1, tq), F32), pltpu.VMEM((2, 1, tq), F32),
                        pltpu.VMEM((2, MLA_V, tq), F32)],
        compiler_params=pltpu.CompilerParams(dimension_semantics=("arbitrary",) * 3, vmem_limit_bytes=VMEM_LIMIT),
        name="mla_attn",
    )(q, k, vt)


def _swa_kernel(sink_ref, q_ref, ka_ref, kb_ref, va_ref, vb_ref, kah_ref, kbh_ref, vah_ref, vbh_ref, o_ref, *, ts):
    w = SWA_WINDOW
    i = pl.program_id(1)
    ka = jnp.concatenate([kah_ref[0], ka_ref[0]], axis=0)
    kb = jnp.concatenate([kbh_ref[0], kb_ref[0]], axis=0)
    va = jnp.concatenate([vah_ref[0], va_ref[0]], axis=0)
    vb = jnp.concatenate([vbh_ref[0], vb_ref[0]], axis=0)
    lane_k = lax.broadcasted_iota(I32, (2 * w, LANES), 1)
    low = lane_k < SWA_HEAD_DIM
    qi = lax.broadcasted_iota(I32, (2 * w, 2 * w), 0) % w
    kj = lax.broadcasted_iota(I32, (2 * w, 2 * w), 1)
    diff = qi + w - kj
    band = (diff >= 0) & (diff < w)
    lane_o = lax.broadcasted_iota(I32, (w, LANES), 1)
    row2 = lax.broadcasted_iota(I32, (2 * w, 1), 0)
    zero = jnp.zeros((), BF16)
    stacks = ((0, ka, True, va), (1, kb, False, vb), (4, kb, True, vb), (5, ka, False, va))
    for n in range(ts // w):
        mask = band & ((i * (ts // w) + n > 0) | (kj >= w))
        res = []
        for h0, ksrc, keep_low, vsrc in stacks:
            p0 = h0 // 2
            q = jnp.concatenate([q_ref[0, n * w:(n + 1) * w, p0 * LANES:(p0 + 1) * LANES],
                                 q_ref[0, n * w:(n + 1) * w, (p0 + 1) * LANES:(p0 + 2) * LANES]], axis=0)
            kwin = ksrc[n * w:n * w + 2 * w]
            kwin = jnp.where(low if keep_low else ~low, kwin, zero)
            vwin = vsrc[n * w:n * w + 2 * w]
            s = lax.dot_general(q, kwin, (((1,), (1,)), ((), ())), preferred_element_type=F32)
            s = jnp.where(mask, s, NEG_BIG)
            sink = jnp.where(row2 < w, sink_ref[h0], sink_ref[h0 + 2]) * LOG2E
            m = jnp.maximum(jnp.max(s, axis=1, keepdims=True), sink)
            p = jnp.exp2(s - m)
            den = jnp.sum(p, axis=1, keepdims=True) + jnp.exp2(sink - m)
            o = jnp.dot(p.astype(BF16), vwin, preferred_element_type=F32) / den
            res.append(o)
        o02, o13, o46, o57 = res
        sel = lane_o < SWA_HEAD_DIM
        rows = slice(n * w, (n + 1) * w)
        o_ref[0, rows, 0 * LANES:1 * LANES] = jnp.where(sel, o02[:w], o13[:w]).astype(BF16)
        o_ref[0, rows, 1 * LANES:2 * LANES] = jnp.where(sel, o02[w:], o13[w:]).astype(BF16)
        o_ref[0, rows, 2 * LANES:3 * LANES] = jnp.where(sel, o46[:w], o57[:w]).astype(BF16)
        o_ref[0, rows, 3 * LANES:4 * LANES] = jnp.where(sel, o46[w:], o57[w:]).astype(BF16)


def _swa_call(sinks, q, ksa, ksb, vsa, vsb):
    b, s, _ = q.shape
    ts = min(SWA_TS, s)
    w = SWA_WINDOW
    r = ts // w
    main = pl.BlockSpec((1, ts, LANES), lambda bi, i: (bi, i, 0))
    halo = pl.BlockSpec((1, w, LANES), lambda bi, i: (bi, jnp.maximum(i * r - 1, 0), 0))
    return pl.pallas_call(
        functools.partial(_swa_kernel, ts=ts),
        grid=(b, s // ts),
        in_specs=[pl.BlockSpec(memory_space=pltpu.SMEM),
                  pl.BlockSpec((1, ts, 4 * LANES), lambda bi, i: (bi, i, 0)),
                  main, main, main, main, halo, halo, halo, halo],
        out_specs=pl.BlockSpec((1, ts, 4 * LANES), lambda bi, i: (bi, i, 0)),
        out_shape=jax.ShapeDtypeStruct((b, s, 4 * LANES), BF16),
        compiler_params=pltpu.CompilerParams(dimension_semantics=("arbitrary",) * 2, vmem_limit_bytes=VMEM_LIMIT),
        name="swa_attn",
    )(sinks, q, ksa, ksb, vsa, vsb, ksa, ksb, vsa, vsb)


def _memkv_kernel(mem_ref, g_ref, w_ref, o_ref):
    mn = _rms(mem_ref[...], g_ref[...]).astype(BF16)
    o_ref[...] = jnp.dot(mn, w_ref[...], preferred_element_type=F32).astype(BF16)


def _memkv_call(mem2, g_mem, w_mem_kv):
    n, d = mem2.shape
    tm = min(256, n)
    return pl.pallas_call(
        _memkv_kernel,
        grid=(n // tm,),
        in_specs=[pl.BlockSpec((tm, d), lambda i: (i, 0)),
                  pl.BlockSpec(g_mem.shape, lambda i: (0, 0)),
                  pl.BlockSpec(w_mem_kv.shape, lambda i: (0, 0))],
        out_specs=pl.BlockSpec((tm, w_mem_kv.shape[1]), lambda i: (i, 0)),
        out_shape=jax.ShapeDtypeStruct((n, w_mem_kv.shape[1]), BF16),
        compiler_params=pltpu.CompilerParams(dimension_semantics=("arbitrary",), vmem_limit_bytes=VMEM_LIMIT),
        name="mem_kv",
    )(mem2, g_mem, w_mem_kv)


def _merge_kernel(x_ref, omla_ref, oswa_ref, qx_ref, gt_ref, kvm_ref, wmo_ref, wso_ref, wxo_ref, wout_ref,
                  gffn_ref, wr_ref, br_ref,
                  h_ref, hnp_ref, idx_ref, gate_ref, rank_ref, cnt_ref, run_ref, *, tm, sub):
    @pl.when(pl.program_id(0) == 0)
    def _():
        run_ref[...] = jnp.zeros(run_ref.shape, F32)

    d = x_ref.shape[1]
    kv_cols = XA_HEADS * XA_HEAD_DIM
    erow = lax.broadcasted_iota(I32, (N_EXPERTS, sub), 0)
    tri_t = (lax.broadcasted_iota(I32, (sub, sub), 0) < lax.broadcasted_iota(I32, (sub, sub), 1)).astype(BF16)
    nt = (((1,), (1,)), ((), ()))
    run = run_ref[...]
    for hf in range(tm // sub):
        rows = slice(hf * sub, (hf + 1) * sub)

        oxs = []
        for hd in range(XA_HEADS):
            sl = slice(hd * LANES, (hd + 1) * LANES)
            km = kvm_ref[0, :, sl]
            vm = kvm_ref[0, :, kv_cols + hd * LANES:kv_cols + (hd + 1) * LANES]
            s = lax.dot_general(qx_ref[rows, sl], km, nt, preferred_element_type=F32)
            p = jnp.exp2(s - jnp.max(s, axis=1, keepdims=True))
            den = jnp.sum(p, axis=1, keepdims=True)
            oxs.append((jnp.dot(p.astype(BF16), vm, preferred_element_type=F32) / den).astype(BF16))
        oxa = jnp.concatenate(oxs, axis=1)

        merged = (gt_ref[rows, 0:d].astype(F32) * jnp.dot(omla_ref[rows, :], wmo_ref[...], preferred_element_type=F32)
                  + gt_ref[rows, d:2 * d].astype(F32) * jnp.dot(oswa_ref[rows, :], wso_ref[...],
                                                                 preferred_element_type=F32)
                  + gt_ref[rows, 2 * d:3 * d].astype(F32) * jnp.dot(oxa, wxo_ref[...], preferred_element_type=F32))
        h = x_ref[rows, :] + jnp.dot(merged.astype(BF16), wout_ref[...], preferred_element_type=F32)
        h_ref[rows, :] = h

        hn = _rms(h, gffn_ref[...])
        hn_hi = hn.astype(BF16)
        hn_hi32 = hn_hi.astype(F32)
        hn_lo = (hn - hn_hi32).astype(BF16)
        bits = pltpu.bitcast(hn_hi32, U32)
        hnp_ref[rows, :] = (bits[:, : d // 2] >> 16) | (bits[:, d // 2:] & jnp.uint32(0xFFFF0000))

        part = lax.dot_general(wr_ref[...], hn_hi, nt, preferred_element_type=F32)
        logits_t = (part[:N_EXPERTS] + part[N_EXPERTS:]
                    + lax.dot_general(wr_ref[0:N_EXPERTS, :], hn_lo, nt, preferred_element_type=F32) + br_ref[...])

        work = logits_t
        vals, idxs, hots = [], [], []
        for _ in range(TOP_K):
            mx = jnp.max(work, axis=0, keepdims=True)
            ix = jnp.min(jnp.where(work == mx, erow, N_EXPERTS), axis=0, keepdims=True)
            hot = erow == ix
            work = jnp.where(hot, -jnp.inf, work)
            vals.append(mx)
            idxs.append(ix)
            hots.append(hot)
        es = [jnp.exp(v - vals[0]) for v in vals]
        den = es[0] + es[1] + es[2] + es[3]
        sel_t = (hots[0] | hots[1] | hots[2] | hots[3])
        prefix_t = jnp.dot(sel_t.astype(BF16), tri_t, preferred_element_type=F32) + run
        for k in range(TOP_K):
            idx_ref[k:k + 1, rows] = idxs[k]
            gate_ref[k:k + 1, rows] = es[k] / den
            rank_ref[k:k + 1, rows] = jnp.sum(jnp.where(hots[k], prefix_t, 0.0), axis=0, keepdims=True).astype(I32)
        run = run + jnp.sum(sel_t.astype(F32), axis=1, keepdims=True)
    run_ref[...] = run
    cnt_ref[...] = run.astype(I32)


def _merge_call(x2, omla, oswa, qx, gates, kvm, wmo, wso, wxo, wout, g_ffn, wr_split, b_router_col, seq):
    t, d = x2.shape
    tm = MERGE_TM
    per_b = seq // tm
    row = lambda n: pl.BlockSpec((tm, n), lambda i: (i, 0))
    col = lambda: pl.BlockSpec((TOP_K, tm), lambda i: (0, i))
    full = lambda a: pl.BlockSpec(a.shape, lambda i: (0,) * a.ndim)
    return pl.pallas_call(
        functools.partial(_merge_kernel, tm=tm, sub=MERGE_SUB),
        grid=(t // tm,),
        in_specs=[row(d), row(512), row(512), row(512), row(3 * d),
                  pl.BlockSpec((1,) + kvm.shape[1:], lambda i: (i // per_b, 0, 0)),
                  full(wmo), full(wso), full(wxo), full(wout), full(g_ffn), full(wr_split), full(b_router_col)],
        out_specs=[row(d), row(d // 2), col(), col(), col(),
                   pl.BlockSpec((N_EXPERTS, 1), lambda i: (0, 0))],
        out_shape=[jax.ShapeDtypeStruct((t, d), F32), jax.ShapeDtypeStruct((t, d // 2), U32),
                   jax.ShapeDtypeStruct((TOP_K, t), I32), jax.ShapeDtypeStruct((TOP_K, t), F32),
                   jax.ShapeDtypeStruct((TOP_K, t), I32), jax.ShapeDtypeStruct((N_EXPERTS, 1), I32)],
        scratch_shapes=[pltpu.VMEM((N_EXPERTS, 1), F32)],
        compiler_params=pltpu.CompilerParams(dimension_semantics=("arbitrary",), vmem_limit_bytes=VMEM_LIMIT),
        name="merge_router",
    )(x2, omla, oswa, qx, gates, kvm, wmo, wso, wxo, wout, g_ffn, wr_split, b_router_col)


def _dispatch_kernel(dest_ref, ztile_ref, hn_ref, xs_ref, zbuf, sem, zsem, *, tm, n_tok):
    base = pl.program_id(0) * tm

    @pl.when(pl.program_id(0) == 0)
    def _():
        zbuf[...] = jnp.zeros(zbuf.shape, U32)

        def zcopy(e):
            start = pl.multiple_of(ztile_ref[e], FFN_TM)
            return pltpu.make_async_copy(zbuf, xs_ref.at[pl.ds(start, zbuf.shape[0])], zsem)

        for e in range(N_EXPERTS):
            @pl.when(ztile_ref[e] >= 0)
            def _():
                zcopy(e).start()
        for e in range(N_EXPERTS):
            @pl.when(ztile_ref[e] >= 0)
            def _():
                zcopy(e).wait()

        def unused(i):
            start = pl.multiple_of(i * FFN_TM, FFN_TM)
            return pltpu.make_async_copy(zbuf, xs_ref.at[pl.ds(start, zbuf.shape[0])], zsem)

        n_tiles = xs_ref.shape[0] // FFN_TM
        lax.fori_loop(ztile_ref[N_EXPERTS], n_tiles, lambda i, c: (unused(i).start(), c)[1], 0)
        lax.fori_loop(ztile_ref[N_EXPERTS], n_tiles, lambda i, c: (unused(i).wait(), c)[1], 0)

    def issue(tt, carry):
        for k in range(TOP_K):
            dst = dest_ref[k * n_tok + base + tt]
            pltpu.make_async_copy(hn_ref.at[pl.ds(tt, 1)], xs_ref.at[pl.ds(dst, 1)], sem).start()
        return carry

    lax.fori_loop(0, tm, issue, 0)
    for _ in range(TOP_K):
        pltpu.make_async_copy(hn_ref, xs_ref.at[pl.ds(0, tm)], sem).wait()


def _dispatch_call(dest, ztile, hnp, n_rows):
    t, c = hnp.shape
    tm = DISPATCH_TM
    return pl.pallas_call(
        functools.partial(_dispatch_kernel, tm=tm, n_tok=t),
        grid_spec=pltpu.PrefetchScalarGridSpec(
            num_scalar_prefetch=2,
            grid=(t // tm,),
            in_specs=[pl.BlockSpec((tm, c), lambda i, ds, zt: (i, 0))],
            out_specs=pl.BlockSpec(memory_space=pl.ANY),
            scratch_shapes=[pltpu.VMEM((FFN_TM, c), U32), pltpu.SemaphoreType.DMA(()),
                            pltpu.SemaphoreType.DMA(())]),
        out_shape=jax.ShapeDtypeStruct((n_rows, c), U32),
        compiler_params=pltpu.CompilerParams(dimension_semantics=("arbitrary",), has_side_effects=True),
        name="moe_dispatch",
    )(dest, ztile, hnp)


def _unpack_lo(w):
    return pltpu.bitcast(w << 16, F32)


def _unpack_hi(w):
    return pltpu.bitcast(w & jnp.uint32(0xFFFF0000), F32)


def _ffn_kernel(te_ref, xs_ref, wgu_ref, bgu_ref, wd_ref, bd_ref, y_ref, wgu_bf, wd_bf):
    i = pl.program_id(0)
    n_used = te_ref[pl.num_programs(0)]
    e = te_ref[i]
    e_prev = te_ref[jnp.maximum(i - 1, 0)]

    @pl.when((i < n_used) & ((i == 0) | (e != e_prev)))
    def _():
        wgu_bf[...] = wgu_ref[0].astype(BF16)
        wd_bf[...] = wd_ref[0].astype(BF16)

    @pl.when(i < n_used)
    def _():
        w = xs_ref[...]
        half = w.shape[1]
        x_lo = _unpack_lo(w).astype(BF16)
        x_hi = _unpack_hi(w).astype(BF16)
        gu = (jnp.dot(x_lo, wgu_bf[0:half, :], preferred_element_type=F32)
              + jnp.dot(x_hi, wgu_bf[half:, :], preferred_element_type=F32) + bgu_ref[0])
        de = gu.shape[1] // 2
        x_glu = jnp.minimum(gu[:, :de], SWIGLU_LIMIT)
        x_lin = jnp.clip(gu[:, de:], -SWIGLU_LIMIT, SWIGLU_LIMIT)
        hdn = x_glu * jax.nn.sigmoid(SWIGLU_ALPHA * x_glu) * (x_lin + 1.0)
        y = jnp.dot(hdn.astype(BF16), wd_bf[...], preferred_element_type=F32) + bd_ref[0]
        bits = pltpu.bitcast(y.astype(BF16).astype(F32), U32)
        y_ref[...] = (bits[:, :half] >> 16) | (bits[:, half:] & jnp.uint32(0xFFFF0000))

    @pl.when(i >= n_used)
    def _():
        y_ref[...] = jnp.zeros(y_ref.shape, U32)


def _ffn_call(tile_expert, xs, w_gate_up, b_gate_up, w_down, b_down):
    r, half = xs.shape
    tm = FFN_TM
    ne, d, de2 = w_gate_up.shape
    return pl.pallas_call(
        _ffn_kernel,
        grid_spec=pltpu.PrefetchScalarGridSpec(
            num_scalar_prefetch=1,
            grid=(r // tm,),
            in_specs=[pl.BlockSpec((tm, half), lambda i, te: (jnp.minimum(i, jnp.maximum(te[r // tm] - 1, 0)), 0)),
                      pl.BlockSpec((1, d, de2), lambda i, te: (te[i], 0, 0)),
                      pl.BlockSpec((1, 1, de2), lambda i, te: (te[i], 0, 0)),
                      pl.BlockSpec((1, de2 // 2, d), lambda i, te: (te[i], 0, 0)),
                      pl.BlockSpec((1, 1, d), lambda i, te: (te[i], 0, 0))],
            out_specs=pl.BlockSpec((tm, half), lambda i, te: (i, 0)),
            scratch_shapes=[pltpu.VMEM((d, de2), BF16), pltpu.VMEM((de2 // 2, d), BF16)]),
        out_shape=jax.ShapeDtypeStruct((r, half), U32),
        compiler_params=pltpu.CompilerParams(dimension_semantics=("arbitrary",), vmem_limit_bytes=VMEM_LIMIT),
        name="moe_ffn",
    )(tile_expert, xs, w_gate_up, b_gate_up, w_down, b_down)


def _combine_kernel(dest_ref, h_ref, gate_ref, gfin_ref, y_ref, o_ref, ybuf, sem, *, tm):
    i = pl.program_id(0)
    base = i * tm
    n_tok = pl.num_programs(0) * tm

    def issue(tt, carry):
        tok = base + tt
        for k in range(TOP_K):
            src = dest_ref[k * n_tok + tok]
            pltpu.make_async_copy(y_ref.at[pl.ds(src, 1)], ybuf.at[k, pl.ds(tt, 1)], sem).start()
        return carry

    lax.fori_loop(0, tm, issue, 0)
    for k in range(TOP_K):
        pltpu.make_async_copy(y_ref.at[pl.ds(0, tm)], ybuf.at[k], sem).wait()

    half = ybuf.shape[2]
    lo = h_ref[:, :half]
    hi = h_ref[:, half:]
    for k in range(TOP_K):
        g = gate_ref[:, k:k + 1]
        w = ybuf[k]
        lo = lo + g * _unpack_lo(w)
        hi = hi + g * _unpack_hi(w)
    ms = (jnp.sum(lo * lo, axis=1, keepdims=True) + jnp.sum(hi * hi, axis=1, keepdims=True)) / (2 * half)
    inv = lax.rsqrt(ms + RMS_EPS)
    o_ref[:, :half] = lo * inv * gfin_ref[:, :half]
    o_ref[:, half:] = hi * inv * gfin_ref[:, half:]


def _combine_call(dest, h, gate, g_final, y):
    t, d = h.shape
    tm = COMBINE_TM
    return pl.pallas_call(
        functools.partial(_combine_kernel, tm=tm),
        grid_spec=pltpu.PrefetchScalarGridSpec(
            num_scalar_prefetch=1,
            grid=(t // tm,),
            in_specs=[pl.BlockSpec((tm, d), lambda i, ds: (i, 0)),
                      pl.BlockSpec((tm, TOP_K), lambda i, ds: (i, 0)),
                      pl.BlockSpec((1, d), lambda i, ds: (0, 0)),
                      pl.BlockSpec(memory_space=pl.ANY)],
            out_specs=pl.BlockSpec((tm, d), lambda i, ds: (i, 0)),
            scratch_shapes=[pltpu.VMEM((TOP_K, tm, d // 2), U32), pltpu.SemaphoreType.DMA(())]),
        out_shape=jax.ShapeDtypeStruct((t, d), F32),
        compiler_params=pltpu.CompilerParams(dimension_semantics=("arbitrary",), vmem_limit_bytes=VMEM_LIMIT),
        name="moe_combine",
    )(dest, h, gate, g_final, y)


def _rope_freqs():
    def inv_freq(dh):
        return (ROPE_THETA ** (-jnp.arange(0, dh, 2, dtype=F32) / dh))[:, None]

    return inv_freq(MLA_ROPE), inv_freq(SWA_HEAD_DIM)


def _winprep_kernel(w_ref, o_ref):
    w = w_ref[...]
    rb = w.shape[0]
    c1 = MLA_Q_RANK + MLA_KV_RANK
    tail = w[:, c1 + MLA_ROPE:]
    lane = lax.broadcasted_iota(I32, (rb, LANES), 1)
    in_rope = (lane >= MLA_NOPE) & (lane < MLA_NOPE + MLA_ROPE)
    kr_p = jnp.where(in_rope, pltpu.roll(w[:, c1:c1 + LANES], MLA_NOPE, axis=1), 0.0)
    pieces = [w[:, :c1], kr_p, tail]
    off = 0
    for pc in pieces:
        o_ref[:, off:off + pc.shape[1]] = pc.astype(BF16)
        off += pc.shape[1]


def _winprep_call(w_in):
    d, n = w_in.shape
    rb = 128
    return pl.pallas_call(
        _winprep_kernel,
        grid=(d // rb,),
        in_specs=[pl.BlockSpec((rb, n), lambda i: (i, 0))],
        out_specs=pl.BlockSpec((rb, _D1), lambda i: (i, 0)),
        out_shape=jax.ShapeDtypeStruct((d, _D1), BF16),
        compiler_params=pltpu.CompilerParams(dimension_semantics=("arbitrary",), vmem_limit_bytes=VMEM_LIMIT),
        name="w_in_prep",
    )(w_in)


def _prep_weights(w_in, w_mla_uq, w_mla_ukv):
    w_in_al = _winprep_call(w_in)

    r = w_mla_uq.shape[0]
    wq = w_mla_uq.reshape(r, MLA_HEADS, MLA_NOPE + MLA_ROPE)
    zq = jnp.zeros((r, MLA_HEADS, LANES - MLA_NOPE - MLA_ROPE), w_mla_uq.dtype)
    wq_pad = jnp.concatenate([wq, zq], axis=-1).reshape(r, MLA_HEADS * LANES).astype(BF16)

    rk = w_mla_ukv.shape[0]
    wkv = w_mla_ukv.reshape(rk, MLA_HEADS, MLA_NOPE + MLA_V)
    wk_aug = jnp.concatenate([wkv[..., :MLA_NOPE], jnp.zeros((rk, MLA_HEADS, LANES - MLA_NOPE), w_mla_ukv.dtype)],
                             axis=-1).reshape(rk, MLA_HEADS * LANES).astype(BF16)
    wv_t = wkv[..., MLA_NOPE:].reshape(rk, MLA_HEADS * MLA_V).T.astype(BF16)
    return w_in_al, wq_pad, wk_aug, wv_t


def kernel(x, mem, positions, g_mix, w_in, g_mla_q, w_mla_uq, g_mla_kv, w_mla_ukv, w_mla_o, swa_sinks, w_swa_o,
           g_mem, w_mem_kv, w_xa_o, b_gate, w_out, g_ffn, w_router, b_router, w_gate_up, b_gate_up, w_down,
           b_down, g_final):
    b, s, d = x.shape
    t = b * s
    depth = g_mix.shape[0]
    h = x.reshape(t, d)
    pos = positions.astype(F32).reshape(1, t)
    fq, f64 = _rope_freqs()
    for l in range(depth):
        w_in_al, wq_pad, wk_aug, wv_t = _prep_weights(w_in[l], w_mla_uq[l], w_mla_ukv[l])
        (qm, km, vmt, qs, ksa, ksb, vsa, vsb, qx, gates) = _proj_call(
            h, pos, fq, f64, g_mix[l][None], w_in_al, g_mla_q[l][None], wq_pad,
            g_mla_kv[l][None], wk_aug, wv_t, b_gate[l][None])
        r3 = lambda a: a.reshape(b, s, a.shape[1])
        omla = _mla_call(r3(qm), r3(km), vmt, s).reshape(t, -1)
        oswa = _swa_call(swa_sinks[l], r3(qs), r3(ksa), r3(ksb), r3(vsa), r3(vsb)).reshape(t, -1)
        m = mem.shape[1]
        kvm = _memkv_call(mem.reshape(b * m, d), g_mem[l][None], w_mem_kv[l].astype(BF16)).reshape(b, m, -1)
        wr_t = w_router[l].T
        wr_hi = wr_t.astype(BF16)
        wr_split = jnp.concatenate([wr_hi, (wr_t - wr_hi.astype(F32)).astype(BF16)], axis=0)
        h_mid, hnp, idx, gate, rank, counts = _merge_call(
            h, omla, oswa, qx, gates, kvm, w_mla_o[l].astype(BF16), w_swa_o[l].astype(BF16),
            w_xa_o[l].astype(BF16), w_out[l].astype(BF16), g_ffn[l][None], wr_split, b_router[l][:, None], s)

        counts = counts[:, 0]
        padded = ((counts + FFN_TM - 1) // FFN_TM) * FFN_TM
        padded_end = jnp.cumsum(padded)
        offsets = padded_end - padded
        experts = jnp.arange(N_EXPERTS, dtype=I32)
        dest = (jnp.sum(jnp.where(idx[..., None] == experts, offsets, 0), axis=-1) + rank).reshape(-1).astype(I32)
        n_tiles = (t * TOP_K) // FFN_TM + N_EXPERTS
        n_used = (padded_end[-1] // FFN_TM).astype(I32)
        ztile = jnp.concatenate([jnp.where(counts % FFN_TM != 0, padded_end - FFN_TM, -1).astype(I32), n_used[None]])
        tile_start = jnp.minimum(jnp.arange(n_tiles, dtype=I32), jnp.maximum(n_used - 1, 0)) * FFN_TM
        tile_expert = jnp.sum((padded_end[None, :] <= tile_start[:, None]).astype(I32), axis=1)
        te = jnp.concatenate([jnp.minimum(tile_expert, N_EXPERTS - 1), n_used[None]])

        xs = _dispatch_call(dest, ztile, hnp, n_tiles * FFN_TM)
        y = _ffn_call(te, xs, w_gate_up[l], b_gate_up[l][:, None, :], w_down[l], b_down[l][:, None, :])
        if l == depth - 1:
            gfin = g_final[None]
            out = _combine_call(dest, h_mid, gate.T, gfin, y)
        else:
            raise NotImplementedError("depth > 1 needs a combine without the final norm")
        h = out
    return h.reshape(b, s, d)
```

```python
import functools
import math

import jax
import jax.numpy as jnp
from jax import lax
from jax.experimental import pallas as pl
from jax.experimental.pallas import tpu as pltpu

F32 = jnp.float32
BF16 = jnp.bfloat16
U32 = jnp.uint32
I32 = jnp.int32

LANES = 128
ROPE_THETA = 10000.0
RMS_EPS = 1e-6
LOG2E = 1.4426950408889634

MLA_HEADS = 8
MLA_NOPE = 64
MLA_ROPE = 32
MLA_V = 64
MLA_Q_RANK = 256
MLA_KV_RANK = 128
SWA_HEADS = 8
SWA_KV_HEADS = 2
SWA_HEAD_DIM = 64
SWA_WINDOW = 128
XA_HEADS = 4
XA_HEAD_DIM = 128
N_EXPERTS = 32
TOP_K = 4
SWIGLU_ALPHA = 1.702
SWIGLU_LIMIT = 7.0
N_BRANCHES = 3

NEG_BIG = -1e30

PROJ_TM = 256
MLA_TQ = 512
SWA_TS = 512
MERGE_TM = 512
MERGE_SUB = 256
FFN_TM = 256
DISPATCH_TM = 256
COMBINE_TM = 256

VMEM_LIMIT = 56 * 1024 * 1024


def _rms(x, g):
    return x * lax.rsqrt(jnp.mean(x * x, axis=-1, keepdims=True) + RMS_EPS) * g


_A0, _A1 = 0, 512
_B0, _B1 = 512, 1280
_C0, _C1 = 1280, 1792
_D0, _D1 = 1792, 4864


def _rotate_half(x, d, lo, hi):
    n = x.shape[1]
    half = (hi - lo) // 2
    lane = lax.broadcasted_iota(I32, x.shape, 1) % d
    up = pltpu.roll(x, n - half, axis=1)
    dn = pltpu.roll(x, half, axis=1)
    return jnp.where((lane >= lo) & (lane < lo + half), -up, jnp.where((lane >= lo + half) & (lane < hi), dn, 0.0))


def _proj_kernel(x_ref, pos_ref, fq_ref, f64_ref, gmix_ref, win_ref, gq_ref, wq_ref,
                 gkv_ref, wk_ref, wv_ref, bgate_ref,
                 qm_ref, km_ref, vm_ref, qs_ref, ksa_ref, ksb_ref, vsa_ref, vsb_ref, qx_ref, gt_ref):
    x = x_ref[...]
    xn = _rms(x, gmix_ref[...]).astype(BF16)
    tm = x.shape[0]
    pos = pos_ref[...]
    ang16 = fq_ref[...] * pos
    ang32 = f64_ref[...] * pos
    c16, s16, c32, s32 = jnp.cos(ang16), jnp.sin(ang16), jnp.cos(ang32), jnp.sin(ang32)
    one = jnp.ones((MLA_NOPE, tm), F32)
    zero = jnp.zeros((MLA_NOPE, tm), F32)
    pad = LANES - MLA_NOPE - MLA_ROPE
    cosq = jnp.concatenate([one, c16, c16, one[:pad]], axis=0).T
    sinq = jnp.concatenate([zero, s16, s16, zero[:pad]], axis=0).T
    cos64 = jnp.concatenate([c32, c32, c32, c32], axis=0).T
    sin64 = jnp.concatenate([s32, s32, s32, s32], axis=0).T
    rope_lo, rope_hi = MLA_NOPE, MLA_NOPE + MLA_ROPE

    xa = jnp.dot(xn, win_ref[:, _A0:_A1], preferred_element_type=F32)
    cqn = _rms(xa[:, 0:256], gq_ref[...]).astype(BF16)
    qa = jnp.dot(cqn, wq_ref[...], preferred_element_type=F32)
    qb = _rotate_half(qa, LANES, rope_lo, rope_hi)
    q_scale = LOG2E / math.sqrt(MLA_NOPE + MLA_ROPE)
    ckvn = _rms(xa[:, 256:384], gkv_ref[...]).astype(BF16)
    ka = jnp.dot(ckvn, wk_ref[...], preferred_element_type=F32)
    kr = xa[:, 384:512]
    krope = kr * cosq + _rotate_half(kr, LANES, rope_lo, rope_hi) * sinq
    for h in range(MLA_HEADS):
        sl = slice(h * LANES, (h + 1) * LANES)
        qm_ref[:, sl] = ((qa[:, sl] * cosq + qb[:, sl] * sinq) * q_scale).astype(BF16)
        km_ref[:, sl] = (ka[:, sl] + krope).astype(BF16)
    vm_ref[...] = lax.dot_general(wv_ref[...], ckvn, (((1,), (1,)), ((), ())),
                                  preferred_element_type=F32).astype(BF16)

    xb = jnp.dot(xn, win_ref[:, _B0:_B1], preferred_element_type=F32)
    s_scale = LOG2E / math.sqrt(SWA_HEAD_DIM)
    nq = SWA_HEADS * SWA_HEAD_DIM
    qs = xb[:, :nq]
    qs_rot = _rotate_half(qs, SWA_HEAD_DIM, 0, SWA_HEAD_DIM)
    for p in range(SWA_HEADS // 2):
        sl = slice(p * LANES, (p + 1) * LANES)
        qs_ref[:, sl] = ((qs[:, sl] * cos64 + qs_rot[:, sl] * sin64) * s_scale).astype(BF16)
    ks = xb[:, nq:nq + LANES]
    ks = ks * cos64 + _rotate_half(ks, SWA_HEAD_DIM, 0, SWA_HEAD_DIM) * sin64
    ksa_ref[...] = ks.astype(BF16)
    ksb_ref[...] = pltpu.roll(ks, 64, axis=1).astype(BF16)
    vs = xb[:, nq + LANES:nq + 2 * LANES]
    vsa_ref[...] = vs.astype(BF16)
    vsb_ref[...] = pltpu.roll(vs, 64, axis=1).astype(BF16)

    xc = jnp.dot(xn, win_ref[:, _C0:_C1], preferred_element_type=F32)
    qx_ref[...] = (xc * (LOG2E / math.sqrt(XA_HEAD_DIM))).astype(BF16)

    xd = jnp.dot(xn, win_ref[:, _D0:_D1], preferred_element_type=F32) + bgate_ref[...]
    gt_ref[...] = jax.nn.sigmoid(xd).astype(BF16)


def _proj_call(x2, pos, fq, f64, g_mix, w_in_al, g_q, wq, g_kv, wk_aug, wv, b_gate):
    t, d = x2.shape
    tm = PROJ_TM
    row = lambda n: pl.BlockSpec((tm, n), lambda i: (i, 0))
    full = lambda a: pl.BlockSpec(a.shape, lambda i: (0,) * a.ndim)
    out_cols = [1024, 1024, 512, 128, 128, 128, 128, 512, 3072]
    out_specs = [row(n) for n in out_cols]
    out_shape = [jax.ShapeDtypeStruct((t, n), BF16) for n in out_cols]
    vt_rows = wv.shape[0]
    out_specs.insert(2, pl.BlockSpec((vt_rows, tm), lambda i: (0, i)))
    out_shape.insert(2, jax.ShapeDtypeStruct((vt_rows, t), BF16))
    return pl.pallas_call(
        _proj_kernel,
        grid=(t // tm,),
        in_specs=[row(d), pl.BlockSpec((1, tm), lambda i: (0, i)), full(fq), full(f64), full(g_mix),
                  pl.BlockSpec(w_in_al.shape, lambda i: (0, 0), pipeline_mode=pl.Buffered(1)),
                  full(g_q), full(wq), full(g_kv), full(wk_aug), full(wv), full(b_gate)],
        out_specs=out_specs,
        out_shape=out_shape,
        compiler_params=pltpu.CompilerParams(dimension_semantics=("arbitrary",), vmem_limit_bytes=VMEM_LIMIT),
        name="proj",
    )(x2, pos, fq, f64, g_mix, w_in_al, g_q, wq, g_kv, wk_aug, wv, b_gate)


def _mla_kernel(q_ref, k_ref, vt_ref, o_ref, sa_ref, sb_ref, m_ref, l_ref, acc_ref, *, tq):
    i = pl.program_id(2)
    m_ref[...] = jnp.full(m_ref.shape, NEG_BIG, F32)
    l_ref[...] = jnp.zeros(l_ref.shape, F32)
    acc_ref[...] = jnp.zeros(acc_ref.shape, F32)

    def scores(j, s_ref):
        k0 = pl.multiple_of(j * tq, tq)
        for hh in range(2):
            sl = slice(hh * LANES, (hh + 1) * LANES)
            s_ref[hh] = lax.dot_general(k_ref[0, pl.ds(k0, tq), sl], q_ref[0, :, sl], (((1,), (1,)), ((), ())),
                                        preferred_element_type=F32)

    def update(j, s_ref, masked):
        k0 = pl.multiple_of(j * tq, tq)
        for hh in range(2):
            vt = vt_ref[hh * MLA_V:(hh + 1) * MLA_V, pl.ds(k0, tq)]
            st = s_ref[hh]
            if masked:
                kj = lax.broadcasted_iota(I32, (tq, tq), 0)
                qi = lax.broadcasted_iota(I32, (tq, tq), 1)
                st = jnp.where(kj <= qi, st, NEG_BIG)
            m_old = m_ref[hh]
            m_new = jnp.maximum(m_old, jnp.max(st, axis=0, keepdims=True))
            alpha = jnp.exp2(m_old - m_new)
            pt = jnp.exp2(st - m_new)
            l_ref[hh] = alpha * l_ref[hh] + jnp.sum(pt, axis=0, keepdims=True)
            acc_ref[hh] = alpha * acc_ref[hh] + jnp.dot(vt, pt.astype(BF16), preferred_element_type=F32)
            m_ref[hh] = m_new

    scores(0, sa_ref)

    def body(jj, carry):
        scores(2 * jj + 1, sb_ref)
        update(2 * jj, sa_ref, False)
        scores(2 * jj + 2, sa_ref)
        update(2 * jj + 1, sb_ref, False)
        return carry

    lax.fori_loop(0, i // 2, body, 0)

    @pl.when(i % 2 == 0)
    def _():
        update(i, sa_ref, True)

    @pl.when(i % 2 == 1)
    def _():
        scores(i, sb_ref)
        update(i - 1, sa_ref, False)
        update(i, sb_ref, True)

    ot = jnp.concatenate([acc_ref[0] / l_ref[0], acc_ref[1] / l_ref[1]], axis=0)
    o_ref[0] = ot.T.astype(BF16)


def _mla_call(q, k, vt, seq):
    b, s, _ = q.shape
    assert s == seq
    tq = min(MLA_TQ, s)
    n_pairs = MLA_HEADS // 2
    return pl.pallas_call(
        functools.partial(_mla_kernel, tq=tq),
        grid=(b, n_pairs, s // tq),
        in_specs=[pl.BlockSpec((1, tq, 2 * LANES), lambda bi, hp, i: (bi, i, hp)),
                  pl.BlockSpec((1, s, 2 * LANES), lambda bi, hp, i: (bi, 0, hp)),
                  pl.BlockSpec((2 * MLA_V, s), lambda bi, hp, i: (hp, bi))],
        out_specs=pl.BlockSpec((1, tq, LANES), lambda bi, hp, i: (bi, i, hp)),
        out_shape=jax.ShapeDtypeStruct((b, s, n_pairs * LANES), BF16),
        scratch_shapes=[pltpu.VMEM((2, tq, tq), F32), pltpu.VMEM((2, tq, tq), F32),
                        pltpu.VMEM((2, 1, tq), F32), pltpu.VMEM((2, 1, tq), F32),
                        pltpu.VMEM((2, MLA_V, tq), F32)],
        compiler_params=pltpu.CompilerParams(dimension_semantics=("arbitrary",) * 3, vmem_limit_bytes=VMEM_LIMIT),
        name="mla_attn",
    )(q, k, vt)


def _swa_kernel(sink_ref, q_ref, ka_ref, kb_ref, va_ref, vb_ref, kah_ref, kbh_ref, vah_ref, vbh_ref, o_ref, *, ts):
    w = SWA_WINDOW
    i = pl.program_id(1)
    ka = jnp.concatenate([kah_ref[0], ka_ref[0]], axis=0)
    kb = jnp.concatenate([kbh_ref[0], kb_ref[0]], axis=0)
    va = jnp.concatenate([vah_ref[0], va_ref[0]], axis=0)
    vb = jnp.concatenate([vbh_ref[0], vb_ref[0]], axis=0)
    lane_k = lax.broadcasted_iota(I32, (2 * w, LANES), 1)
    low = lane_k < SWA_HEAD_DIM
    qi = lax.broadcasted_iota(I32, (2 * w, 2 * w), 0) % w
    kj = lax.broadcasted_iota(I32, (2 * w, 2 * w), 1)
    diff = qi + w - kj
    band = (diff >= 0) & (diff < w)
    lane_o = lax.broadcasted_iota(I32, (w, LANES), 1)
    row2 = lax.broadcasted_iota(I32, (2 * w, 1), 0)
    zero = jnp.zeros((), BF16)
    stacks = ((0, ka, True, va), (1, kb, False, vb), (4, kb, True, vb), (5, ka, False, va))
    for n in range(ts // w):
        mask = band & ((i * (ts // w) + n > 0) | (kj >= w))
        res = []
        for h0, ksrc, keep_low, vsrc in stacks:
            p0 = h0 // 2
            q = jnp.concatenate([q_ref[0, n * w:(n + 1) * w, p0 * LANES:(p0 + 1) * LANES],
                                 q_ref[0, n * w:(n + 1) * w, (p0 + 1) * LANES:(p0 + 2) * LANES]], axis=0)
            kwin = ksrc[n * w:n * w + 2 * w]
            kwin = jnp.where(low if keep_low else ~low, kwin, zero)
            vwin = vsrc[n * w:n * w + 2 * w]
            s = lax.dot_general(q, kwin, (((1,), (1,)), ((), ())), preferred_element_type=F32)
            s = jnp.where(mask, s, NEG_BIG)
            sink = jnp.where(row2 < w, sink_ref[h0], sink_ref[h0 + 2]) * LOG2E
            m = jnp.maximum(jnp.max(s, axis=1, keepdims=True), sink)
            p = jnp.exp2(s - m)
            den = jnp.sum(p, axis=1, keepdims=True) + jnp.exp2(sink - m)
            o = jnp.dot(p.astype(BF16), vwin, preferred_element_type=F32) / den
            res.append(o)
        o02, o13, o46, o57 = res
        sel = lane_o < SWA_HEAD_DIM
        rows = slice(n * w, (n + 1) * w)
        o_ref[0, rows, 0 * LANES:1 * LANES] = jnp.where(sel, o02[:w], o13[:w]).astype(BF16)
        o_ref[0, rows, 1 * LANES:2 * LANES] = jnp.where(sel, o02[w:], o13[w:]).astype(BF16)
        o_ref[0, rows, 2 * LANES:3 * LANES] = jnp.where(sel, o46[:w], o57[:w]).astype(BF16)
        o_ref[0, rows, 3 * LANES:4 * LANES] = jnp.where(sel, o46[w:], o57[w:]).astype(BF16)


def _swa_call(sinks, q, ksa, ksb, vsa, vsb):
    b, s, _ = q.shape
    ts = min(SWA_TS, s)
    w = SWA_WINDOW
    r = ts // w
    main = pl.BlockSpec((1, ts, LANES), lambda bi, i: (bi, i, 0))
    halo = pl.BlockSpec((1, w, LANES), lambda bi, i: (bi, jnp.maximum(i * r - 1, 0), 0))
    return pl.pallas_call(
        functools.partial(_swa_kernel, ts=ts),
        grid=(b, s // ts),
        in_specs=[pl.BlockSpec(memory_space=pltpu.SMEM),
                  pl.BlockSpec((1, ts, 4 * LANES), lambda bi, i: (bi, i, 0)),
                  main, main, main, main, halo, halo, halo, halo],
        out_specs=pl.BlockSpec((1, ts, 4 * LANES), lambda bi, i: (bi, i, 0)),
        out_shape=jax.ShapeDtypeStruct((b, s, 4 * LANES), BF16),
        compiler_params=pltpu.CompilerParams(dimension_semantics=("arbitrary",) * 2, vmem_limit_bytes=VMEM_LIMIT),
        name="swa_attn",
    )(sinks, q, ksa, ksb, vsa, vsb, ksa, ksb, vsa, vsb)


def _memkv_kernel(mem_ref, g_ref, w_ref, o_ref):
    mn = _rms(mem_ref[...], g_ref[...]).astype(BF16)
    o_ref[...] = jnp.dot(mn, w_ref[...], preferred_element_type=F32).astype(BF16)


def _memkv_call(mem2, g_mem, w_mem_kv):
    n, d = mem2.shape
    tm = min(256, n)
    return pl.pallas_call(
        _memkv_kernel,
        grid=(n // tm,),
        in_specs=[pl.BlockSpec((tm, d), lambda i: (i, 0)),
                  pl.BlockSpec(g_mem.shape, lambda i: (0, 0)),
                  pl.BlockSpec(w_mem_kv.shape, lambda i: (0, 0))],
        out_specs=pl.BlockSpec((tm, w_mem_kv.shape[1]), lambda i: (i, 0)),
        out_shape=jax.ShapeDtypeStruct((n, w_mem_kv.shape[1]), BF16),
        compiler_params=pltpu.CompilerParams(dimension_semantics=("arbitrary",), vmem_limit_bytes=VMEM_LIMIT),
        name="mem_kv",
    )(mem2, g_mem, w_mem_kv)


def _merge_kernel(x_ref, omla_ref, oswa_ref, qx_ref, gt_ref, kvm_ref, wmo_ref, wso_ref, wxo_ref, wout_ref,
                  gffn_ref, wr_ref, br_ref,
                  h_ref, hnp_ref, idx_ref, gate_ref, rank_ref, cnt_ref, run_ref, *, tm, sub):
    @pl.when(pl.program_id(0) == 0)
    def _():
        run_ref[...] = jnp.zeros(run_ref.shape, F32)

    d = x_ref.shape[1]
    kv_cols = XA_HEADS * XA_HEAD_DIM
    erow = lax.broadcasted_iota(I32, (N_EXPERTS, sub), 0)
    tri_t = (lax.broadcasted_iota(I32, (sub, sub), 0) < lax.broadcasted_iota(I32, (sub, sub), 1)).astype(BF16)
    nt = (((1,), (1,)), ((), ()))
    run = run_ref[...]
    for hf in range(tm // sub):
        rows = slice(hf * sub, (hf + 1) * sub)

        oxs = []
        for hd in range(XA_HEADS):
            sl = slice(hd * LANES, (hd + 1) * LANES)
            km = kvm_ref[0, :, sl]
            vm = kvm_ref[0, :, kv_cols + hd * LANES:kv_cols + (hd + 1) * LANES]
            s = lax.dot_general(qx_ref[rows, sl], km, nt, preferred_element_type=F32)
            p = jnp.exp2(s - jnp.max(s, axis=1, keepdims=True))
            den = jnp.sum(p, axis=1, keepdims=True)
            oxs.append((jnp.dot(p.astype(BF16), vm, preferred_element_type=F32) / den).astype(BF16))
        oxa = jnp.concatenate(oxs, axis=1)

        merged = (gt_ref[rows, 0:d].astype(F32) * jnp.dot(omla_ref[rows, :], wmo_ref[...], preferred_element_type=F32)
                  + gt_ref[rows, d:2 * d].astype(F32) * jnp.dot(oswa_ref[rows, :], wso_ref[...],
                                                                 preferred_element_type=F32)
                  + gt_ref[rows, 2 * d:3 * d].astype(F32) * jnp.dot(oxa, wxo_ref[...], preferred_element_type=F32))
        h = x_ref[rows, :] + jnp.dot(merged.astype(BF16), wout_ref[...], preferred_element_type=F32)
        h_ref[rows, :] = h

        hn = _rms(h, gffn_ref[...])
        hn_hi = hn.astype(BF16)
        hn_hi32 = hn_hi.astype(F32)
        hn_lo = (hn - hn_hi32).astype(BF16)
        bits = pltpu.bitcast(hn_hi32, U32)
        hnp_ref[rows, :] = (bits[:, : d // 2] >> 16) | (bits[:, d // 2:] & jnp.uint32(0xFFFF0000))

        part = lax.dot_general(wr_ref[...], hn_hi, nt, preferred_element_type=F32)
        logits_t = (part[:N_EXPERTS] + part[N_EXPERTS:]
                    + lax.dot_general(wr_ref[0:N_EXPERTS, :], hn_lo, nt, preferred_element_type=F32) + br_ref[...])

        work = logits_t
        vals, idxs, hots = [], [], []
        for _ in range(TOP_K):
            mx = jnp.max(work, axis=0, keepdims=True)
            ix = jnp.min(jnp.where(work == mx, erow, N_EXPERTS), axis=0, keepdims=True)
            hot = erow == ix
            work = jnp.where(hot, -jnp.inf, work)
            vals.append(mx)
            idxs.append(ix)
            hots.append(hot)
        es = [jnp.exp(v - vals[0]) for v in vals]
        den = es[0] + es[1] + es[2] + es[3]
        sel_t = (hots[0] | hots[1] | hots[2] | hots[3])
        prefix_t = jnp.dot(sel_t.astype(BF16), tri_t, preferred_element_type=F32) + run
        for k in range(TOP_K):
            idx_ref[k:k + 1, rows] = idxs[k]
            gate_ref[k:k + 1, rows] = es[k] / den
            rank_ref[k:k + 1, rows] = jnp.sum(jnp.where(hots[k], prefix_t, 0.0), axis=0, keepdims=True).astype(I32)
        run = run + jnp.sum(sel_t.astype(F32), axis=1, keepdims=True)
    run_ref[...] = run
    cnt_ref[...] = run.astype(I32)


def _merge_call(x2, omla, oswa, qx, gates, kvm, wmo, wso, wxo, wout, g_ffn, wr_split, b_router_col, seq):
    t, d = x2.shape
    tm = MERGE_TM
    per_b = seq // tm
    row = lambda n: pl.BlockSpec((tm, n), lambda i: (i, 0))
    col = lambda: pl.BlockSpec((TOP_K, tm), lambda i: (0, i))
    full = lambda a: pl.BlockSpec(a.shape, lambda i: (0,) * a.ndim)
    return pl.pallas_call(
        functools.partial(_merge_kernel, tm=tm, sub=MERGE_SUB),
        grid=(t // tm,),
        in_specs=[row(d), row(512), row(512), row(512), row(3 * d),
                  pl.BlockSpec((1,) + kvm.shape[1:], lambda i: (i // per_b, 0, 0)),
                  full(wmo), full(wso), full(wxo), full(wout), full(g_ffn), full(wr_split), full(b_router_col)],
        out_specs=[row(d), row(d // 2), col(), col(), col(),
                   pl.BlockSpec((N_EXPERTS, 1), lambda i: (0, 0))],
        out_shape=[jax.ShapeDtypeStruct((t, d), F32), jax.ShapeDtypeStruct((t, d // 2), U32),
                   jax.ShapeDtypeStruct((TOP_K, t), I32), jax.ShapeDtypeStruct((TOP_K, t), F32),
                   jax.ShapeDtypeStruct((TOP_K, t), I32), jax.ShapeDtypeStruct((N_EXPERTS, 1), I32)],
        scratch_shapes=[pltpu.VMEM((N_EXPERTS, 1), F32)],
        compiler_params=pltpu.CompilerParams(dimension_semantics=("arbitrary",), vmem_limit_bytes=VMEM_LIMIT),
        name="merge_router",
    )(x2, omla, oswa, qx, gates, kvm, wmo, wso, wxo, wout, g_ffn, wr_split, b_router_col)


def _dispatch_kernel(dest_ref, ztile_ref, hn_ref, xs_ref, zbuf, sem, zsem, *, tm, n_tok):
    base = pl.program_id(0) * tm

    @pl.when(pl.program_id(0) == 0)
    def _():
        zbuf[...] = jnp.zeros(zbuf.shape, U32)

        def zcopy(e):
            start = pl.multiple_of(ztile_ref[e], FFN_TM)
            return pltpu.make_async_copy(zbuf, xs_ref.at[pl.ds(start, zbuf.shape[0])], zsem)

        for e in range(N_EXPERTS):
            @pl.when(ztile_ref[e] >= 0)
            def _():
                zcopy(e).start()
        for e in range(N_EXPERTS):
            @pl.when(ztile_ref[e] >= 0)
            def _():
                zcopy(e).wait()

        def unused(i):
            start = pl.multiple_of(i * FFN_TM, FFN_TM)
            return pltpu.make_async_copy(zbuf, xs_ref.at[pl.ds(start, zbuf.shape[0])], zsem)

        n_tiles = xs_ref.shape[0] // FFN_TM
        lax.fori_loop(ztile_ref[N_EXPERTS], n_tiles, lambda i, c: (unused(i).start(), c)[1], 0)
        lax.fori_loop(ztile_ref[N_EXPERTS], n_tiles, lambda i, c: (unused(i).wait(), c)[1], 0)

    def issue(tt, carry):
        for k in range(TOP_K):
            dst = dest_ref[k * n_tok + base + tt]
            pltpu.make_async_copy(hn_ref.at[pl.ds(tt, 1)], xs_ref.at[pl.ds(dst, 1)], sem).start()
        return carry

    lax.fori_loop(0, tm, issue, 0)
    for _ in range(TOP_K):
        pltpu.make_async_copy(hn_ref, xs_ref.at[pl.ds(0, tm)], sem).wait()


def _dispatch_call(dest, ztile, hnp, n_rows):
    t, c = hnp.shape
    tm = DISPATCH_TM
    return pl.pallas_call(
        functools.partial(_dispatch_kernel, tm=tm, n_tok=t),
        grid_spec=pltpu.PrefetchScalarGridSpec(
            num_scalar_prefetch=2,
            grid=(t // tm,),
            in_specs=[pl.BlockSpec((tm, c), lambda i, ds, zt: (i, 0))],
            out_specs=pl.BlockSpec(memory_space=pl.ANY),
            scratch_shapes=[pltpu.VMEM((FFN_TM, c), U32), pltpu.SemaphoreType.DMA(()),
                            pltpu.SemaphoreType.DMA(())]),
        out_shape=jax.ShapeDtypeStruct((n_rows, c), U32),
        compiler_params=pltpu.CompilerParams(dimension_semantics=("arbitrary",), has_side_effects=True),
        name="moe_dispatch",
    )(dest, ztile, hnp)


def _unpack_lo(w):
    return pltpu.bitcast(w << 16, F32)


def _unpack_hi(w):
    return pltpu.bitcast(w & jnp.uint32(0xFFFF0000), F32)


def _ffn_kernel(te_ref, xs_ref, wgu_ref, bgu_ref, wd_ref, bd_ref, y_ref, wgu_bf, wd_bf):
    i = pl.program_id(0)
    n_used = te_ref[pl.num_programs(0)]
    e = te_ref[i]
    e_prev = te_ref[jnp.maximum(i - 1, 0)]

    @pl.when((i < n_used) & ((i == 0) | (e != e_prev)))
    def _():
        wgu_bf[...] = wgu_ref[0].astype(BF16)
        wd_bf[...] = wd_ref[0].astype(BF16)

    @pl.when(i < n_used)
    def _():
        w = xs_ref[...]
        half = w.shape[1]
        x_lo = _unpack_lo(w).astype(BF16)
        x_hi = _unpack_hi(w).astype(BF16)
        gu = (jnp.dot(x_lo, wgu_bf[0:half, :], preferred_element_type=F32)
              + jnp.dot(x_hi, wgu_bf[half:, :], preferred_element_type=F32) + bgu_ref[0])
        de = gu.shape[1] // 2
        x_glu = jnp.minimum(gu[:, :de], SWIGLU_LIMIT)
        x_lin = jnp.clip(gu[:, de:], -SWIGLU_LIMIT, SWIGLU_LIMIT)
        hdn = x_glu * jax.nn.sigmoid(SWIGLU_ALPHA * x_glu) * (x_lin + 1.0)
        y = jnp.dot(hdn.astype(BF16), wd_bf[...], preferred_element_type=F32) + bd_ref[0]
        bits = pltpu.bitcast(y.astype(BF16).astype(F32), U32)
        y_ref[...] = (bits[:, :half] >> 16) | (bits[:, half:] & jnp.uint32(0xFFFF0000))

    @pl.when(i >= n_used)
    def _():
        y_ref[...] = jnp.zeros(y_ref.shape, U32)


def _ffn_call(tile_expert, xs, w_gate_up, b_gate_up, w_down, b_down):
    r, half = xs.shape
    tm = FFN_TM
    ne, d, de2 = w_gate_up.shape
    return pl.pallas_call(
        _ffn_kernel,
        grid_spec=pltpu.PrefetchScalarGridSpec(
            num_scalar_prefetch=1,
            grid=(r // tm,),
            in_specs=[pl.BlockSpec((tm, half), lambda i, te: (jnp.minimum(i, jnp.maximum(te[r // tm] - 1, 0)), 0)),
                      pl.BlockSpec((1, d, de2), lambda i, te: (te[i], 0, 0)),
                      pl.BlockSpec((1, 1, de2), lambda i, te: (te[i], 0, 0)),
                      pl.BlockSpec((1, de2 // 2, d), lambda i, te: (te[i], 0, 0)),
                      pl.BlockSpec((1, 1, d), lambda i, te: (te[i], 0, 0))],
            out_specs=pl.BlockSpec((tm, half), lambda i, te: (i, 0)),
            scratch_shapes=[pltpu.VMEM((d, de2), BF16), pltpu.VMEM((de2 // 2, d), BF16)]),
        out_shape=jax.ShapeDtypeStruct((r, half), U32),
        compiler_params=pltpu.CompilerParams(dimension_semantics=("arbitrary",), vmem_limit_bytes=VMEM_LIMIT),
        name="moe_ffn",
    )(tile_expert, xs, w_gate_up, b_gate_up, w_down, b_down)


def _combine_kernel(dest_ref, h_ref, gate_ref, gfin_ref, y_ref, o_ref, ybuf, sem, *, tm):
    i = pl.program_id(0)
    base = i * tm
    n_tok = pl.num_programs(0) * tm

    def issue(tt, carry):
        tok = base + tt
        for k in range(TOP_K):
            src = dest_ref[k * n_tok + tok]
            pltpu.make_async_copy(y_ref.at[pl.ds(src, 1)], ybuf.at[k, pl.ds(tt, 1)], sem).start()
        return carry

    lax.fori_loop(0, tm, issue, 0)
    for k in range(TOP_K):
        pltpu.make_async_copy(y_ref.at[pl.ds(0, tm)], ybuf.at[k], sem).wait()

    half = ybuf.shape[2]
    lo = h_ref[:, :half]
    hi = h_ref[:, half:]
    for k in range(TOP_K):
        g = gate_ref[:, k:k + 1]
        w = ybuf[k]
        lo = lo + g * _unpack_lo(w)
        hi = hi + g * _unpack_hi(w)
    ms = (jnp.sum(lo * lo, axis=1, keepdims=True) + jnp.sum(hi * hi, axis=1, keepdims=True)) / (2 * half)
    inv = lax.rsqrt(ms + RMS_EPS)
    o_ref[:, :half] = lo * inv * gfin_ref[:, :half]
    o_ref[:, half:] = hi * inv * gfin_ref[:, half:]


def _combine_call(dest, h, gate, g_final, y):
    t, d = h.shape
    tm = COMBINE_TM
    return pl.pallas_call(
        functools.partial(_combine_kernel, tm=tm),
        grid_spec=pltpu.PrefetchScalarGridSpec(
            num_scalar_prefetch=1,
            grid=(t // tm,),
            in_specs=[pl.BlockSpec((tm, d), lambda i, ds: (i, 0)),
                      pl.BlockSpec((tm, TOP_K), lambda i, ds: (i, 0)),
                      pl.BlockSpec((1, d), lambda i, ds: (0, 0)),
                      pl.BlockSpec(memory_space=pl.ANY)],
            out_specs=pl.BlockSpec((tm, d), lambda i, ds: (i, 0)),
            scratch_shapes=[pltpu.VMEM((TOP_K, tm, d // 2), U32), pltpu.SemaphoreType.DMA(())]),
        out_shape=jax.ShapeDtypeStruct((t, d), F32),
        compiler_params=pltpu.CompilerParams(dimension_semantics=("arbitrary",), vmem_limit_bytes=VMEM_LIMIT),
        name="moe_combine",
    )(dest, h, gate, g_final, y)


PAIR_LEAD = 2
TOP_K_BITS = 2
assert 1 << TOP_K_BITS == TOP_K


def _ffn_fused_kernel(te_ref, pair_ref, hn_hbm, wgu_ref, bgu_ref, wd_ref, bd_ref, yk_hbm,
                      xs_buf, y_buf, wgu_bf, wd_bf, gsem, ssem, *, tm, n_tok):
    i = pl.program_id(0)
    n_steps = pl.num_programs(0)
    n_used = te_ref[n_steps]
    half = xs_buf.shape[2]
    trash0 = TOP_K * n_tok

    def pair_of(tile, j):
        return pair_ref[(tile + PAIR_LEAD) * tm + j]

    def gather(tile, j, slot):
        tok = lax.shift_right_logical(jnp.maximum(pair_of(tile, j), 0), TOP_K_BITS)
        return pltpu.make_async_copy(hn_hbm.at[pl.ds(tok, 1)], xs_buf.at[slot, pl.ds(j, 1)], gsem.at[slot])

    def scatter(tile, j, slot):
        pair = pair_of(tile, j)
        real = (pair & (TOP_K - 1)) * n_tok + lax.shift_right_logical(pair, TOP_K_BITS)
        dst = jnp.where(pair >= 0, real, trash0 + slot * tm + j)
        return pltpu.make_async_copy(y_buf.at[slot, pl.ds(j, 1)], yk_hbm.at[pl.ds(dst, 1)], ssem.at[slot])

    def wait_gather(slot):
        pltpu.make_async_copy(hn_hbm.at[pl.ds(0, tm)], xs_buf.at[slot], gsem.at[slot]).wait()

    def wait_scatter(slot):
        pltpu.make_async_copy(y_buf.at[slot], yk_hbm.at[pl.ds(0, tm)], ssem.at[slot]).wait()

    def loop_rows(fn):
        lax.fori_loop(0, tm, lambda j, c: (fn(j), c)[1], 0)

    @pl.when(i == 0)
    def _():
        y_buf[...] = jnp.zeros(y_buf.shape, U32)
        loop_rows(lambda j: gather(0, j, 0).start())
        loop_rows(lambda j: scatter(-2, j, 0).start())

    e = te_ref[i]
    e_prev = te_ref[jnp.maximum(i - 1, 0)]

    @pl.when((i < n_used) & ((i == 0) | (e != e_prev)))
    def _():
        wgu_bf[...] = wgu_ref[0].astype(BF16)
        wd_bf[...] = wd_ref[0].astype(BF16)

    def compute_step(p):
        q = 1 - p
        wait_gather(p)
        w = xs_buf[p]
        x_lo = _unpack_lo(w).astype(BF16)
        x_hi = _unpack_hi(w).astype(BF16)
        for j in range(tm):
            gather(i + 1, j, q).start()
            scatter(i - 1, j, q).start()
        gu = (jnp.dot(x_lo, wgu_bf[0:half, :], preferred_element_type=F32)
              + jnp.dot(x_hi, wgu_bf[half:, :], preferred_element_type=F32) + bgu_ref[0])
        de = gu.shape[1] // 2
        x_glu = jnp.minimum(gu[:, :de], SWIGLU_LIMIT)
        x_lin = jnp.clip(gu[:, de:], -SWIGLU_LIMIT, SWIGLU_LIMIT)
        hdn = x_glu * jax.nn.sigmoid(SWIGLU_ALPHA * x_glu) * (x_lin + 1.0)
        y = jnp.dot(hdn.astype(BF16), wd_bf[...], preferred_element_type=F32) + bd_ref[0]
        bits = pltpu.bitcast(y.astype(BF16).astype(F32), U32)
        wait_scatter(p)
        y_buf[p] = (bits[:, :half] >> 16) | (bits[:, half:] & jnp.uint32(0xFFFF0000))

    for p in range(2):
        @pl.when((i < n_used) & (i % 2 == p))
        def _():
            compute_step(p)

    @pl.when(i == n_used)
    def _():
        s = i % 2
        wait_gather(s)
        loop_rows(lambda j: scatter(i - 1, j, 1 - s).start())
        wait_scatter(s)
        wait_scatter(1 - s)

    @pl.when((i == n_steps - 1) & (i < n_used))
    def _():
        s = i % 2
        wait_gather(1 - s)
        loop_rows(lambda j: scatter(i, j, s).start())
        wait_scatter(1 - s)
        wait_scatter(s)


def _ffn_fused_call(tile_expert, pair_table, hnp, w_gate_up, b_gate_up, w_down, b_down, n_tiles):
    t, half = hnp.shape
    tm = FFN_TM
    ne, d, de2 = w_gate_up.shape
    return pl.pallas_call(
        functools.partial(_ffn_fused_kernel, tm=tm, n_tok=t),
        grid_spec=pltpu.PrefetchScalarGridSpec(
            num_scalar_prefetch=2,
            grid=(n_tiles,),
            in_specs=[pl.BlockSpec(memory_space=pl.ANY),
                      pl.BlockSpec((1, d, de2), lambda i, te, pr: (te[i], 0, 0)),
                      pl.BlockSpec((1, 1, de2), lambda i, te, pr: (te[i], 0, 0)),
                      pl.BlockSpec((1, de2 // 2, d), lambda i, te, pr: (te[i], 0, 0)),
                      pl.BlockSpec((1, 1, d), lambda i, te, pr: (te[i], 0, 0))],
            out_specs=pl.BlockSpec(memory_space=pl.ANY),
            scratch_shapes=[pltpu.VMEM((2, tm, half), U32), pltpu.VMEM((2, tm, half), U32),
                            pltpu.VMEM((d, de2), BF16), pltpu.VMEM((de2 // 2, d), BF16),
                            pltpu.SemaphoreType.DMA((2,)), pltpu.SemaphoreType.DMA((2,))]),
        out_shape=jax.ShapeDtypeStruct((TOP_K * t + 2 * tm, half), U32),
        compiler_params=pltpu.CompilerParams(dimension_semantics=("arbitrary",), vmem_limit_bytes=VMEM_LIMIT,
                                             has_side_effects=True),
        name="moe_ffn",
    )(tile_expert, pair_table, hnp, w_gate_up, b_gate_up, w_down, b_down)


def _combine_dense_kernel(h_ref, gate_ref, gfin_ref, y0_ref, y1_ref, y2_ref, y3_ref, o_ref):
    half = y0_ref.shape[1]
    lo = h_ref[:, :half]
    hi = h_ref[:, half:]
    for k, y_ref in enumerate((y0_ref, y1_ref, y2_ref, y3_ref)):
        g = gate_ref[:, k:k + 1]
        w = y_ref[...]
        lo = lo + g * _unpack_lo(w)
        hi = hi + g * _unpack_hi(w)
    ms = (jnp.sum(lo * lo, axis=1, keepdims=True) + jnp.sum(hi * hi, axis=1, keepdims=True)) / (2 * half)
    inv = lax.rsqrt(ms + RMS_EPS)
    o_ref[:, :half] = lo * inv * gfin_ref[:, :half]
    o_ref[:, half:] = hi * inv * gfin_ref[:, half:]


def _combine_dense_call(h, gate, g_final, yk):
    t, d = h.shape
    tm = COMBINE_TM
    per_k = t // tm
    yspec = lambda k: pl.BlockSpec((tm, d // 2), lambda i: (k * per_k + i, 0))
    return pl.pallas_call(
        _combine_dense_kernel,
        grid=(t // tm,),
        in_specs=[pl.BlockSpec((tm, d), lambda i: (i, 0)),
                  pl.BlockSpec((tm, TOP_K), lambda i: (i, 0)),
                  pl.BlockSpec((1, d), lambda i: (0, 0)),
                  yspec(0), yspec(1), yspec(2), yspec(3)],
        out_specs=pl.BlockSpec((tm, d), lambda i: (i, 0)),
        out_shape=jax.ShapeDtypeStruct((t, d), F32),
        compiler_params=pltpu.CompilerParams(dimension_semantics=("arbitrary",), vmem_limit_bytes=VMEM_LIMIT),
        name="moe_combine",
    )(h, gate, g_final, yk, yk, yk, yk)


def _rope_freqs():
    def inv_freq(dh):
        return (ROPE_THETA ** (-jnp.arange(0, dh, 2, dtype=F32) / dh))[:, None]

    return inv_freq(MLA_ROPE), inv_freq(SWA_HEAD_DIM)


def _winprep_kernel(w_ref, o_ref):
    w = w_ref[...]
    rb = w.shape[0]
    c1 = MLA_Q_RANK + MLA_KV_RANK
    tail = w[:, c1 + MLA_ROPE:]
    lane = lax.broadcasted_iota(I32, (rb, LANES), 1)
    in_rope = (lane >= MLA_NOPE) & (lane < MLA_NOPE + MLA_ROPE)
    kr_p = jnp.where(in_rope, pltpu.roll(w[:, c1:c1 + LANES], MLA_NOPE, axis=1), 0.0)
    pieces = [w[:, :c1], kr_p, tail]
    off = 0
    for pc in pieces:
        o_ref[:, off:off + pc.shape[1]] = pc.astype(BF16)
        off += pc.shape[1]


def _winprep_call(w_in):
    d, n = w_in.shape
    rb = 128
    return pl.pallas_call(
        _winprep_kernel,
        grid=(d // rb,),
        in_specs=[pl.BlockSpec((rb, n), lambda i: (i, 0))],
        out_specs=pl.BlockSpec((rb, _D1), lambda i: (i, 0)),
        out_shape=jax.ShapeDtypeStruct((d, _D1), BF16),
        compiler_params=pltpu.CompilerParams(dimension_semantics=("arbitrary",), vmem_limit_bytes=VMEM_LIMIT),
        name="w_in_prep",
    )(w_in)


def _prep_weights(w_in, w_mla_uq, w_mla_ukv):
    w_in_al = _winprep_call(w_in)

    r = w_mla_uq.shape[0]
    wq = w_mla_uq.reshape(r, MLA_HEADS, MLA_NOPE + MLA_ROPE)
    zq = jnp.zeros((r, MLA_HEADS, LANES - MLA_NOPE - MLA_ROPE), w_mla_uq.dtype)
    wq_pad = jnp.concatenate([wq, zq], axis=-1).reshape(r, MLA_HEADS * LANES).astype(BF16)

    rk = w_mla_ukv.shape[0]
    wkv = w_mla_ukv.reshape(rk, MLA_HEADS, MLA_NOPE + MLA_V)
    wk_aug = jnp.concatenate([wkv[..., :MLA_NOPE], jnp.zeros((rk, MLA_HEADS, LANES - MLA_NOPE), w_mla_ukv.dtype)],
                             axis=-1).reshape(rk, MLA_HEADS * LANES).astype(BF16)
    wv_t = wkv[..., MLA_NOPE:].reshape(rk, MLA_HEADS * MLA_V).T.astype(BF16)
    return w_in_al, wq_pad, wk_aug, wv_t


def kernel(x, mem, positions, g_mix, w_in, g_mla_q, w_mla_uq, g_mla_kv, w_mla_ukv, w_mla_o, swa_sinks, w_swa_o,
           g_mem, w_mem_kv, w_xa_o, b_gate, w_out, g_ffn, w_router, b_router, w_gate_up, b_gate_up, w_down,
           b_down, g_final):
    b, s, d = x.shape
    t = b * s
    depth = g_mix.shape[0]
    h = x.reshape(t, d)
    pos = positions.astype(F32).reshape(1, t)
    fq, f64 = _rope_freqs()
    for l in range(depth):
        w_in_al, wq_pad, wk_aug, wv_t = _prep_weights(w_in[l], w_mla_uq[l], w_mla_ukv[l])
        (qm, km, vmt, qs, ksa, ksb, vsa, vsb, qx, gates) = _proj_call(
            h, pos, fq, f64, g_mix[l][None], w_in_al, g_mla_q[l][None], wq_pad,
            g_mla_kv[l][None], wk_aug, wv_t, b_gate[l][None])
        r3 = lambda a: a.reshape(b, s, a.shape[1])
        omla = _mla_call(r3(qm), r3(km), vmt, s).reshape(t, -1)
        oswa = _swa_call(swa_sinks[l], r3(qs), r3(ksa), r3(ksb), r3(vsa), r3(vsb)).reshape(t, -1)
        m = mem.shape[1]
        kvm = _memkv_call(mem.reshape(b * m, d), g_mem[l][None], w_mem_kv[l].astype(BF16)).reshape(b, m, -1)
        wr_t = w_router[l].T
        wr_hi = wr_t.astype(BF16)
        wr_split = jnp.concatenate([wr_hi, (wr_t - wr_hi.astype(F32)).astype(BF16)], axis=0)
        h_mid, hnp, idx, gate, rank, counts = _merge_call(
            h, omla, oswa, qx, gates, kvm, w_mla_o[l].astype(BF16), w_swa_o[l].astype(BF16),
            w_xa_o[l].astype(BF16), w_out[l].astype(BF16), g_ffn[l][None], wr_split, b_router[l][:, None], s)

        counts = counts[:, 0]
        padded = ((counts + FFN_TM - 1) // FFN_TM) * FFN_TM
        padded_end = jnp.cumsum(padded)
        offsets = padded_end - padded
        experts = jnp.arange(N_EXPERTS, dtype=I32)
        dest = (jnp.sum(jnp.where(idx[..., None] == experts, offsets, 0), axis=-1) + rank).reshape(-1).astype(I32)
        n_tiles = (t * TOP_K) // FFN_TM + N_EXPERTS
        n_used = (padded_end[-1] // FFN_TM).astype(I32)
        ztile = jnp.concatenate([jnp.where(counts % FFN_TM != 0, padded_end - FFN_TM, -1).astype(I32), n_used[None]])
        tile_start = jnp.minimum(jnp.arange(n_tiles, dtype=I32), jnp.maximum(n_used - 1, 0)) * FFN_TM
        tile_expert = jnp.sum((padded_end[None, :] <= tile_start[:, None]).astype(I32), axis=1)
        te = jnp.concatenate([jnp.minimum(tile_expert, N_EXPERTS - 1), n_used[None]])

        code = (jnp.arange(t, dtype=I32)[None, :] * TOP_K + jnp.arange(TOP_K, dtype=I32)[:, None]).reshape(-1)
        pair_of_row = jnp.full((n_tiles * FFN_TM,), -1, I32).at[dest].set(code, unique_indices=True)
        pad_tile = jnp.full((FFN_TM,), -1, I32)
        pair_table = jnp.concatenate([pad_tile] * PAIR_LEAD + [pair_of_row, pad_tile])
        yk = _ffn_fused_call(te, pair_table, hnp, w_gate_up[l], b_gate_up[l][:, None, :], w_down[l],
                             b_down[l][:, None, :], n_tiles)
        if l == depth - 1:
            gfin = g_final[None]
            out = _combine_dense_call(h_mid, gate.T, gfin, yk)
        else:
            raise NotImplementedError("depth > 1 needs a combine without the final norm")
        h = out
    return h.reshape(b, s, d)
```

```python
import functools
import math

import jax
import jax.numpy as jnp
from jax import lax
from jax.experimental import pallas as pl
from jax.experimental.pallas import tpu as pltpu

F32 = jnp.float32
BF16 = jnp.bfloat16
U32 = jnp.uint32
I32 = jnp.int32

LANES = 128
ROPE_THETA = 10000.0
RMS_EPS = 1e-6
LOG2E = 1.4426950408889634

MLA_HEADS = 8
MLA_NOPE = 64
MLA_ROPE = 32
MLA_V = 64
MLA_Q_RANK = 256
MLA_KV_RANK = 128
SWA_HEADS = 8
SWA_KV_HEADS = 2
SWA_HEAD_DIM = 64
SWA_WINDOW = 128
XA_HEADS = 4
XA_HEAD_DIM = 128
N_EXPERTS = 32
TOP_K = 4
SWIGLU_ALPHA = 1.702
SWIGLU_LIMIT = 7.0
N_BRANCHES = 3

NEG_BIG = -1e30

PROJ_TM = 256
MLA_TQ = 512
SWA_TS = 512
MERGE_TM = 512
MERGE_SUB = 256
FFN_TM = 256
DISPATCH_TM = 256
COMBINE_TM = 256

VMEM_LIMIT = 56 * 1024 * 1024


def _rms(x, g):
    return x * lax.rsqrt(jnp.mean(x * x, axis=-1, keepdims=True) + RMS_EPS) * g


_A0, _A1 = 0, 512
_B0, _B1 = 512, 1280
_C0, _C1 = 1280, 1792
_D0, _D1 = 1792, 4864


def _rotate_half(x, d, lo, hi):
    n = x.shape[1]
    half = (hi - lo) // 2
    lane = lax.broadcasted_iota(I32, x.shape, 1) % d
    up = pltpu.roll(x, n - half, axis=1)
    dn = pltpu.roll(x, half, axis=1)
    return jnp.where((lane >= lo) & (lane < lo + half), -up, jnp.where((lane >= lo + half) & (lane < hi), dn, 0.0))


def _proj_kernel(x_ref, pos_ref, fq_ref, f64_ref, gmix_ref, win_ref, gq_ref, wq_ref,
                 gkv_ref, wk_ref, wv_ref, bgate_ref,
                 qm_ref, km_ref, vm_ref, qs_ref, ksa_ref, ksb_ref, vsa_ref, vsb_ref, qx_ref, gt_ref):
    x = x_ref[...]
    xn = _rms(x, gmix_ref[...]).astype(BF16)
    tm = x.shape[0]
    pos = pos_ref[...]
    ang16 = fq_ref[...] * pos
    ang32 = f64_ref[...] * pos
    c16, s16, c32, s32 = jnp.cos(ang16), jnp.sin(ang16), jnp.cos(ang32), jnp.sin(ang32)
    one = jnp.ones((MLA_NOPE, tm), F32)
    zero = jnp.zeros((MLA_NOPE, tm), F32)
    pad = LANES - MLA_NOPE - MLA_ROPE
    cosq = jnp.concatenate([one, c16, c16, one[:pad]], axis=0).T
    sinq = jnp.concatenate([zero, s16, s16, zero[:pad]], axis=0).T
    cos64 = jnp.concatenate([c32, c32, c32, c32], axis=0).T
    sin64 = jnp.concatenate([s32, s32, s32, s32], axis=0).T
    rope_lo, rope_hi = MLA_NOPE, MLA_NOPE + MLA_ROPE

    xa = jnp.dot(xn, win_ref[:, _A0:_A1], preferred_element_type=F32)
    cqn = _rms(xa[:, 0:256], gq_ref[...]).astype(BF16)
    qa = jnp.dot(cqn, wq_ref[...], preferred_element_type=F32)
    qb = _rotate_half(qa, LANES, rope_lo, rope_hi)
    q_scale = LOG2E / math.sqrt(MLA_NOPE + MLA_ROPE)
    ckvn = _rms(xa[:, 256:384], gkv_ref[...]).astype(BF16)
    ka = jnp.dot(ckvn, wk_ref[...], preferred_element_type=F32)
    kr = xa[:, 384:512]
    krope = kr * cosq + _rotate_half(kr, LANES, rope_lo, rope_hi) * sinq
    for h in range(MLA_HEADS):
        sl = slice(h * LANES, (h + 1) * LANES)
        qm_ref[:, sl] = ((qa[:, sl] * cosq + qb[:, sl] * sinq) * q_scale).astype(BF16)
        km_ref[:, sl] = (ka[:, sl] + krope).astype(BF16)
    vm_ref[...] = lax.dot_general(wv_ref[...], ckvn, (((1,), (1,)), ((), ())),
                                  preferred_element_type=F32).astype(BF16)

    xb = jnp.dot(xn, win_ref[:, _B0:_B1], preferred_element_type=F32)
    s_scale = LOG2E / math.sqrt(SWA_HEAD_DIM)
    nq = SWA_HEADS * SWA_HEAD_DIM
    qs = xb[:, :nq]
    qs_rot = _rotate_half(qs, SWA_HEAD_DIM, 0, SWA_HEAD_DIM)
    for p in range(SWA_HEADS // 2):
        sl = slice(p * LANES, (p + 1) * LANES)
        qs_ref[:, sl] = ((qs[:, sl] * cos64 + qs_rot[:, sl] * sin64) * s_scale).astype(BF16)
    ks = xb[:, nq:nq + LANES]
    ks = ks * cos64 + _rotate_half(ks, SWA_HEAD_DIM, 0, SWA_HEAD_DIM) * sin64
    ksa_ref[...] = ks.astype(BF16)
    ksb_ref[...] = pltpu.roll(ks, 64, axis=1).astype(BF16)
    vs = xb[:, nq + LANES:nq + 2 * LANES]
    vsa_ref[...] = vs.astype(BF16)
    vsb_ref[...] = pltpu.roll(vs, 64, axis=1).astype(BF16)

    xc = jnp.dot(xn, win_ref[:, _C0:_C1], preferred_element_type=F32)
    qx_ref[...] = (xc * (LOG2E / math.sqrt(XA_HEAD_DIM))).astype(BF16)

    xd = jnp.dot(xn, win_ref[:, _D0:_D1], preferred_element_type=F32) + bgate_ref[...]
    gt_ref[...] = jax.nn.sigmoid(xd).astype(BF16)


def _proj_call(x2, pos, fq, f64, g_mix, w_in_al, g_q, wq, g_kv, wk_aug, wv, b_gate):
    t, d = x2.shape
    tm = PROJ_TM
    row = lambda n: pl.BlockSpec((tm, n), lambda i: (i, 0))
    full = lambda a: pl.BlockSpec(a.shape, lambda i: (0,) * a.ndim)
    out_cols = [1024, 1024, 512, 128, 128, 128, 128, 512, 3072]
    out_specs = [row(n) for n in out_cols]
    out_shape = [jax.ShapeDtypeStruct((t, n), BF16) for n in out_cols]
    vt_rows = wv.shape[0]
    out_specs.insert(2, pl.BlockSpec((vt_rows, tm), lambda i: (0, i)))
    out_shape.insert(2, jax.ShapeDtypeStruct((vt_rows, t), BF16))
    return pl.pallas_call(
        _proj_kernel,
        grid=(t // tm,),
        in_specs=[row(d), pl.BlockSpec((1, tm), lambda i: (0, i)), full(fq), full(f64), full(g_mix),
                  pl.BlockSpec(w_in_al.shape, lambda i: (0, 0), pipeline_mode=pl.Buffered(1)),
                  full(g_q), full(wq), full(g_kv), full(wk_aug), full(wv), full(b_gate)],
        out_specs=out_specs,
        out_shape=out_shape,
        compiler_params=pltpu.CompilerParams(dimension_semantics=("arbitrary",), vmem_limit_bytes=VMEM_LIMIT),
        name="proj",
    )(x2, pos, fq, f64, g_mix, w_in_al, g_q, wq, g_kv, wk_aug, wv, b_gate)


def _mla_kernel(q_ref, k_ref, vt_ref, o_ref, sa_ref, sb_ref, m_ref, l_ref, acc_ref, *, tq):
    i = pl.program_id(2)
    m_ref[...] = jnp.full(m_ref.shape, NEG_BIG, F32)
    l_ref[...] = jnp.zeros(l_ref.shape, F32)
    acc_ref[...] = jnp.zeros(acc_ref.shape, F32)

    def scores(j, s_ref):
        k0 = pl.multiple_of(j * tq, tq)
        for hh in range(2):
            sl = slice(hh * LANES, (hh + 1) * LANES)
            s_ref[hh] = lax.dot_general(k_ref[0, pl.ds(k0, tq), sl], q_ref[0, :, sl], (((1,), (1,)), ((), ())),
                                        preferred_element_type=F32)

    def update(j, s_ref, masked):
        k0 = pl.multiple_of(j * tq, tq)
        for hh in range(2):
            vt = vt_ref[hh * MLA_V:(hh + 1) * MLA_V, pl.ds(k0, tq)]
            st = s_ref[hh]
            if masked:
                kj = lax.broadcasted_iota(I32, (tq, tq), 0)
                qi = lax.broadcasted_iota(I32, (tq, tq), 1)
                st = jnp.where(kj <= qi, st, NEG_BIG)
            m_old = m_ref[hh]
            m_new = jnp.maximum(m_old, jnp.max(st, axis=0, keepdims=True))
            alpha = jnp.exp2(m_old - m_new)
            pt = jnp.exp2(st - m_new)
            l_ref[hh] = alpha * l_ref[hh] + jnp.sum(pt, axis=0, keepdims=True)
            acc_ref[hh] = alpha * acc_ref[hh] + jnp.dot(vt, pt.astype(BF16), preferred_element_type=F32)
            m_ref[hh] = m_new

    scores(0, sa_ref)

    def body(jj, carry):
        scores(2 * jj + 1, sb_ref)
        update(2 * jj, sa_ref, False)
        scores(2 * jj + 2, sa_ref)
        update(2 * jj + 1, sb_ref, False)
        return carry

    lax.fori_loop(0, i // 2, body, 0)

    @pl.when(i % 2 == 0)
    def _():
        update(i, sa_ref, True)

    @pl.when(i % 2 == 1)
    def _():
        scores(i, sb_ref)
        update(i - 1, sa_ref, False)
        update(i, sb_ref, True)

    ot = jnp.concatenate([acc_ref[0] / l_ref[0], acc_ref[1] / l_ref[1]], axis=0)
    o_ref[0] = ot.T.astype(BF16)


def _mla_call(q, k, vt, seq):
    b, s, _ = q.shape
    assert s == seq
    tq = min(MLA_TQ, s)
    n_pairs = MLA_HEADS // 2
    return pl.pallas_call(
        functools.partial(_mla_kernel, tq=tq),
        grid=(b, n_pairs, s // tq),
        in_specs=[pl.BlockSpec((1, tq, 2 * LANES), lambda bi, hp, i: (bi, i, hp)),
                  pl.BlockSpec((1, s, 2 * LANES), lambda bi, hp, i: (bi, 0, hp)),
                  pl.BlockSpec((2 * MLA_V, s), lambda bi, hp, i: (hp, bi))],
        out_specs=pl.BlockSpec((1, tq, LANES), lambda bi, hp, i: (bi, i, hp)),
        out_shape=jax.ShapeDtypeStruct((b, s, n_pairs * LANES), BF16),
        scratch_shapes=[pltpu.VMEM((2, tq, tq), F32), pltpu.VMEM((2, tq, tq), F32),
                        pltpu.VMEM((2, 1, tq), F32), pltpu.VMEM((2, 1, tq), F32),
                        pltpu.VMEM((2, MLA_V, tq), F32)],
        compiler_params=pltpu.CompilerParams(dimension_semantics=("arbitrary",) * 3, vmem_limit_bytes=VMEM_LIMIT),
        name="mla_attn",
    )(q, k, vt)


def _swa_kernel(sink_ref, q_ref, ka_ref, kb_ref, va_ref, vb_ref, kah_ref, kbh_ref, vah_ref, vbh_ref, o_ref, *, ts):
    w = SWA_WINDOW
    i = pl.program_id(1)
    ka = jnp.concatenate([kah_ref[0], ka_ref[0]], axis=0)
    kb = jnp.concatenate([kbh_ref[0], kb_ref[0]], axis=0)
    va = jnp.concatenate([vah_ref[0], va_ref[0]], axis=0)
    vb = jnp.concatenate([vbh_ref[0], vb_ref[0]], axis=0)
    lane_k = lax.broadcasted_iota(I32, (2 * w, LANES), 1)
    low = lane_k < SWA_HEAD_DIM
    qi = lax.broadcasted_iota(I32, (2 * w, 2 * w), 0) % w
    kj = lax.broadcasted_iota(I32, (2 * w, 2 * w), 1)
    diff = qi + w - kj
    band = (diff >= 0) & (diff < w)
    lane_o = lax.broadcasted_iota(I32, (w, LANES), 1)
    row2 = lax.broadcasted_iota(I32, (2 * w, 1), 0)
    zero = jnp.zeros((), BF16)
    stacks = ((0, ka, True, va), (1, kb, False, vb), (4, kb, True, vb), (5, ka, False, va))
    for n in range(ts // w):
        mask = band & ((i * (ts // w) + n > 0) | (kj >= w))
        res = []
        for h0, ksrc, keep_low, vsrc in stacks:
            p0 = h0 // 2
            q = jnp.concatenate([q_ref[0, n * w:(n + 1) * w, p0 * LANES:(p0 + 1) * LANES],
                                 q_ref[0, n * w:(n + 1) * w, (p0 + 1) * LANES:(p0 + 2) * LANES]], axis=0)
            kwin = ksrc[n * w:n * w + 2 * w]
            kwin = jnp.where(low if keep_low else ~low, kwin, zero)
            vwin = vsrc[n * w:n * w + 2 * w]
            s = lax.dot_general(q, kwin, (((1,), (1,)), ((), ())), preferred_element_type=F32)
            s = jnp.where(mask, s, NEG_BIG)
            sink = jnp.where(row2 < w, sink_ref[h0], sink_ref[h0 + 2]) * LOG2E
            m = jnp.maximum(jnp.max(s, axis=1, keepdims=True), sink)
            p = jnp.exp2(s - m)
            den = jnp.sum(p, axis=1, keepdims=True) + jnp.exp2(sink - m)
            o = jnp.dot(p.astype(BF16), vwin, preferred_element_type=F32) / den
            res.append(o)
        o02, o13, o46, o57 = res
        sel = lane_o < SWA_HEAD_DIM
        rows = slice(n * w, (n + 1) * w)
        o_ref[0, rows, 0 * LANES:1 * LANES] = jnp.where(sel, o02[:w], o13[:w]).astype(BF16)
        o_ref[0, rows, 1 * LANES:2 * LANES] = jnp.where(sel, o02[w:], o13[w:]).astype(BF16)
        o_ref[0, rows, 2 * LANES:3 * LANES] = jnp.where(sel, o46[:w], o57[:w]).astype(BF16)
        o_ref[0, rows, 3 * LANES:4 * LANES] = jnp.where(sel, o46[w:], o57[w:]).astype(BF16)


def _swa_call(sinks, q, ksa, ksb, vsa, vsb):
    b, s, _ = q.shape
    ts = min(SWA_TS, s)
    w = SWA_WINDOW
    r = ts // w
    main = pl.BlockSpec((1, ts, LANES), lambda bi, i: (bi, i, 0))
    halo = pl.BlockSpec((1, w, LANES), lambda bi, i: (bi, jnp.maximum(i * r - 1, 0), 0))
    return pl.pallas_call(
        functools.partial(_swa_kernel, ts=ts),
        grid=(b, s // ts),
        in_specs=[pl.BlockSpec(memory_space=pltpu.SMEM),
                  pl.BlockSpec((1, ts, 4 * LANES), lambda bi, i: (bi, i, 0)),
                  main, main, main, main, halo, halo, halo, halo],
        out_specs=pl.BlockSpec((1, ts, 4 * LANES), lambda bi, i: (bi, i, 0)),
        out_shape=jax.ShapeDtypeStruct((b, s, 4 * LANES), BF16),
        compiler_params=pltpu.CompilerParams(dimension_semantics=("arbitrary",) * 2, vmem_limit_bytes=VMEM_LIMIT),
        name="swa_attn",
    )(sinks, q, ksa, ksb, vsa, vsb, ksa, ksb, vsa, vsb)


def _memkv_kernel(mem_ref, g_ref, w_ref, o_ref):
    mn = _rms(mem_ref[...], g_ref[...]).astype(BF16)
    o_ref[...] = jnp.dot(mn, w_ref[...], preferred_element_type=F32).astype(BF16)


def _memkv_call(mem2, g_mem, w_mem_kv):
    n, d = mem2.shape
    tm = min(256, n)
    return pl.pallas_call(
        _memkv_kernel,
        grid=(n // tm,),
        in_specs=[pl.BlockSpec((tm, d), lambda i: (i, 0)),
                  pl.BlockSpec(g_mem.shape, lambda i: (0, 0)),
                  pl.BlockSpec(w_mem_kv.shape, lambda i: (0, 0))],
        out_specs=pl.BlockSpec((tm, w_mem_kv.shape[1]), lambda i: (i, 0)),
        out_shape=jax.ShapeDtypeStruct((n, w_mem_kv.shape[1]), BF16),
        compiler_params=pltpu.CompilerParams(dimension_semantics=("arbitrary",), vmem_limit_bytes=VMEM_LIMIT),
        name="mem_kv",
    )(mem2, g_mem, w_mem_kv)


def _merge_kernel(x_ref, omla_ref, oswa_ref, qx_ref, gt_ref, kvm_ref, wmo_ref, wso_ref, wxo_ref, wout_ref,
                  gffn_ref, wr_ref, br_ref,
                  h_ref, hnp_ref, idx_ref, gate_ref, rank_ref, cnt_ref, run_ref, *, tm, sub):
    @pl.when(pl.program_id(0) == 0)
    def _():
        run_ref[...] = jnp.zeros(run_ref.shape, F32)

    d = x_ref.shape[1]
    kv_cols = XA_HEADS * XA_HEAD_DIM
    erow = lax.broadcasted_iota(I32, (N_EXPERTS, sub), 0)
    tri_t = (lax.broadcasted_iota(I32, (sub, sub), 0) < lax.broadcasted_iota(I32, (sub, sub), 1)).astype(BF16)
    nt = (((1,), (1,)), ((), ()))
    run = run_ref[...]
    for hf in range(tm // sub):
        rows = slice(hf * sub, (hf + 1) * sub)

        oxs = []
        for hd in range(XA_HEADS):
            sl = slice(hd * LANES, (hd + 1) * LANES)
            km = kvm_ref[0, :, sl]
            vm = kvm_ref[0, :, kv_cols + hd * LANES:kv_cols + (hd + 1) * LANES]
            s = lax.dot_general(qx_ref[rows, sl], km, nt, preferred_element_type=F32)
            p = jnp.exp2(s - jnp.max(s, axis=1, keepdims=True))
            den = jnp.sum(p, axis=1, keepdims=True)
            oxs.append((jnp.dot(p.astype(BF16), vm, preferred_element_type=F32) / den).astype(BF16))
        oxa = jnp.concatenate(oxs, axis=1)

        merged = (gt_ref[rows, 0:d].astype(F32) * jnp.dot(omla_ref[rows, :], wmo_ref[...], preferred_element_type=F32)
                  + gt_ref[rows, d:2 * d].astype(F32) * jnp.dot(oswa_ref[rows, :], wso_ref[...],
                                                                 preferred_element_type=F32)
                  + gt_ref[rows, 2 * d:3 * d].astype(F32) * jnp.dot(oxa, wxo_ref[...], preferred_element_type=F32))
        h = x_ref[rows, :] + jnp.dot(merged.astype(BF16), wout_ref[...], preferred_element_type=F32)
        h_ref[rows, :] = h

        hn = _rms(h, gffn_ref[...])
        hn_hi = hn.astype(BF16)
        hn_hi32 = hn_hi.astype(F32)
        hn_lo = (hn - hn_hi32).astype(BF16)
        bits = pltpu.bitcast(hn_hi32, U32)
        hnp_ref[rows, :] = (bits[:, : d // 2] >> 16) | (bits[:, d // 2:] & jnp.uint32(0xFFFF0000))

        part = lax.dot_general(wr_ref[...], hn_hi, nt, preferred_element_type=F32)
        logits_t = (part[:N_EXPERTS] + part[N_EXPERTS:]
                    + lax.dot_general(wr_ref[0:N_EXPERTS, :], hn_lo, nt, preferred_element_type=F32) + br_ref[...])

        work = logits_t
        vals, idxs, hots = [], [], []
        for _ in range(TOP_K):
            mx = jnp.max(work, axis=0, keepdims=True)
            ix = jnp.min(jnp.where(work == mx, erow, N_EXPERTS), axis=0, keepdims=True)
            hot = erow == ix
            work = jnp.where(hot, -jnp.inf, work)
            vals.append(mx)
            idxs.append(ix)
            hots.append(hot)
        es = [jnp.exp(v - vals[0]) for v in vals]
        den = es[0] + es[1] + es[2] + es[3]
        sel_t = (hots[0] | hots[1] | hots[2] | hots[3])
        prefix_t = jnp.dot(sel_t.astype(BF16), tri_t, preferred_element_type=F32) + run
        for k in range(TOP_K):
            idx_ref[k:k + 1, rows] = idxs[k]
            gate_ref[k:k + 1, rows] = es[k] / den
            rank_ref[k:k + 1, rows] = jnp.sum(jnp.where(hots[k], prefix_t, 0.0), axis=0, keepdims=True).astype(I32)
        run = run + jnp.sum(sel_t.astype(F32), axis=1, keepdims=True)
    run_ref[...] = run
    cnt_ref[...] = run.astype(I32)


def _merge_call(x2, omla, oswa, qx, gates, kvm, wmo, wso, wxo, wout, g_ffn, wr_split, b_router_col, seq):
    t, d = x2.shape
    tm = MERGE_TM
    per_b = seq // tm
    row = lambda n: pl.BlockSpec((tm, n), lambda i: (i, 0))
    col = lambda: pl.BlockSpec((TOP_K, tm), lambda i: (0, i))
    full = lambda a: pl.BlockSpec(a.shape, lambda i: (0,) * a.ndim)
    return pl.pallas_call(
        functools.partial(_merge_kernel, tm=tm, sub=MERGE_SUB),
        grid=(t // tm,),
        in_specs=[row(d), row(512), row(512), row(512), row(3 * d),
                  pl.BlockSpec((1,) + kvm.shape[1:], lambda i: (i // per_b, 0, 0)),
                  full(wmo), full(wso), full(wxo), full(wout), full(g_ffn), full(wr_split), full(b_router_col)],
        out_specs=[row(d), row(d // 2), col(), col(), col(),
                   pl.BlockSpec((N_EXPERTS, 1), lambda i: (0, 0))],
        out_shape=[jax.ShapeDtypeStruct((t, d), F32), jax.ShapeDtypeStruct((t, d // 2), U32),
                   jax.ShapeDtypeStruct((TOP_K, t), I32), jax.ShapeDtypeStruct((TOP_K, t), F32),
                   jax.ShapeDtypeStruct((TOP_K, t), I32), jax.ShapeDtypeStruct((N_EXPERTS, 1), I32)],
        scratch_shapes=[pltpu.VMEM((N_EXPERTS, 1), F32)],
        compiler_params=pltpu.CompilerParams(dimension_semantics=("arbitrary",), vmem_limit_bytes=VMEM_LIMIT),
        name="merge_router",
    )(x2, omla, oswa, qx, gates, kvm, wmo, wso, wxo, wout, g_ffn, wr_split, b_router_col)


def _dispatch_kernel(dest_ref, ztile_ref, hn_ref, xs_ref, zbuf, sem, zsem, *, tm, n_tok):
    base = pl.program_id(0) * tm

    @pl.when(pl.program_id(0) == 0)
    def _():
        zbuf[...] = jnp.zeros(zbuf.shape, U32)

        def zcopy(e):
            start = pl.multiple_of(ztile_ref[e], FFN_TM)
            return pltpu.make_async_copy(zbuf, xs_ref.at[pl.ds(start, zbuf.shape[0])], zsem)

        for e in range(N_EXPERTS):
            @pl.when(ztile_ref[e] >= 0)
            def _():
                zcopy(e).start()
        for e in range(N_EXPERTS):
            @pl.when(ztile_ref[e] >= 0)
            def _():
                zcopy(e).wait()

        def unused(i):
            start = pl.multiple_of(i * FFN_TM, FFN_TM)
            return pltpu.make_async_copy(zbuf, xs_ref.at[pl.ds(start, zbuf.shape[0])], zsem)

        n_tiles = xs_ref.shape[0] // FFN_TM
        lax.fori_loop(ztile_ref[N_EXPERTS], n_tiles, lambda i, c: (unused(i).start(), c)[1], 0)
        lax.fori_loop(ztile_ref[N_EXPERTS], n_tiles, lambda i, c: (unused(i).wait(), c)[1], 0)

    def issue(tt, carry):
        for k in range(TOP_K):
            dst = dest_ref[k * n_tok + base + tt]
            pltpu.make_async_copy(hn_ref.at[pl.ds(tt, 1)], xs_ref.at[pl.ds(dst, 1)], sem).start(priority=k % 2)
        return carry

    lax.fori_loop(0, tm, issue, 0)
    for _ in range(TOP_K):
        pltpu.make_async_copy(hn_ref, xs_ref.at[pl.ds(0, tm)], sem).wait()


def _dispatch_call(dest, ztile, hnp, n_rows):
    t, c = hnp.shape
    tm = DISPATCH_TM
    return pl.pallas_call(
        functools.partial(_dispatch_kernel, tm=tm, n_tok=t),
        grid_spec=pltpu.PrefetchScalarGridSpec(
            num_scalar_prefetch=2,
            grid=(t // tm,),
            in_specs=[pl.BlockSpec((tm, c), lambda i, ds, zt: (i, 0))],
            out_specs=pl.BlockSpec(memory_space=pl.ANY),
            scratch_shapes=[pltpu.VMEM((FFN_TM, c), U32), pltpu.SemaphoreType.DMA(()),
                            pltpu.SemaphoreType.DMA(())]),
        out_shape=jax.ShapeDtypeStruct((n_rows, c), U32),
        compiler_params=pltpu.CompilerParams(dimension_semantics=("arbitrary",), has_side_effects=True),
        name="moe_dispatch",
    )(dest, ztile, hnp)


def _unpack_lo(w):
    return pltpu.bitcast(w << 16, F32)


def _unpack_hi(w):
    return pltpu.bitcast(w & jnp.uint32(0xFFFF0000), F32)


def _ffn_kernel(te_ref, xs_ref, wgu_ref, bgu_ref, wd_ref, bd_ref, y_ref, wgu_bf, wd_bf):
    i = pl.program_id(0)
    n_used = te_ref[pl.num_programs(0)]
    e = te_ref[i]
    e_prev = te_ref[jnp.maximum(i - 1, 0)]

    @pl.when((i < n_used) & ((i == 0) | (e != e_prev)))
    def _():
        wgu_bf[...] = wgu_ref[0].astype(BF16)
        wd_bf[...] = wd_ref[0].astype(BF16)

    @pl.when(i < n_used)
    def _():
        w = xs_ref[...]
        half = w.shape[1]
        x_lo = _unpack_lo(w).astype(BF16)
        x_hi = _unpack_hi(w).astype(BF16)
        gu = (jnp.dot(x_lo, wgu_bf[0:half, :], preferred_element_type=F32)
              + jnp.dot(x_hi, wgu_bf[half:, :], preferred_element_type=F32) + bgu_ref[0])
        de = gu.shape[1] // 2
        x_glu = jnp.minimum(gu[:, :de], SWIGLU_LIMIT)
        x_lin = jnp.clip(gu[:, de:], -SWIGLU_LIMIT, SWIGLU_LIMIT)
        hdn = x_glu * jax.nn.sigmoid(SWIGLU_ALPHA * x_glu) * (x_lin + 1.0)
        y = jnp.dot(hdn.astype(BF16), wd_bf[...], preferred_element_type=F32) + bd_ref[0]
        bits = pltpu.bitcast(y.astype(BF16).astype(F32), U32)
        y_ref[...] = (bits[:, :half] >> 16) | (bits[:, half:] & jnp.uint32(0xFFFF0000))

    @pl.when(i >= n_used)
    def _():
        y_ref[...] = jnp.zeros(y_ref.shape, U32)


def _ffn_call(tile_expert, xs, w_gate_up, b_gate_up, w_down, b_down):
    r, half = xs.shape
    tm = FFN_TM
    ne, d, de2 = w_gate_up.shape
    return pl.pallas_call(
        _ffn_kernel,
        grid_spec=pltpu.PrefetchScalarGridSpec(
            num_scalar_prefetch=1,
            grid=(r // tm,),
            in_specs=[pl.BlockSpec((tm, half), lambda i, te: (jnp.minimum(i, jnp.maximum(te[r // tm] - 1, 0)), 0)),
                      pl.BlockSpec((1, d, de2), lambda i, te: (te[i], 0, 0)),
                      pl.BlockSpec((1, 1, de2), lambda i, te: (te[i], 0, 0)),
                      pl.BlockSpec((1, de2 // 2, d), lambda i, te: (te[i], 0, 0)),
                      pl.BlockSpec((1, 1, d), lambda i, te: (te[i], 0, 0))],
            out_specs=pl.BlockSpec((tm, half), lambda i, te: (i, 0)),
            scratch_shapes=[pltpu.VMEM((d, de2), BF16), pltpu.VMEM((de2 // 2, d), BF16)]),
        out_shape=jax.ShapeDtypeStruct((r, half), U32),
        compiler_params=pltpu.CompilerParams(dimension_semantics=("arbitrary",), vmem_limit_bytes=VMEM_LIMIT),
        name="moe_ffn",
    )(tile_expert, xs, w_gate_up, b_gate_up, w_down, b_down)


def _combine_kernel(dest_ref, h_ref, gate_ref, gfin_ref, y_ref, o_ref, ybuf, sem, *, tm):
    i = pl.program_id(0)
    base = i * tm
    n_tok = pl.num_programs(0) * tm

    def issue(tt, carry):
        tok = base + tt
        for k in range(TOP_K):
            src = dest_ref[k * n_tok + tok]
            pltpu.make_async_copy(y_ref.at[pl.ds(src, 1)], ybuf.at[k, pl.ds(tt, 1)], sem).start(priority=k % 2)
        return carry

    lax.fori_loop(0, tm, issue, 0)
    for k in range(TOP_K):
        pltpu.make_async_copy(y_ref.at[pl.ds(0, tm)], ybuf.at[k], sem).wait()

    half = ybuf.shape[2]
    lo = h_ref[:, :half]
    hi = h_ref[:, half:]
    for k in range(TOP_K):
        g = gate_ref[:, k:k + 1]
        w = ybuf[k]
        lo = lo + g * _unpack_lo(w)
        hi = hi + g * _unpack_hi(w)
    ms = (jnp.sum(lo * lo, axis=1, keepdims=True) + jnp.sum(hi * hi, axis=1, keepdims=True)) / (2 * half)
    inv = lax.rsqrt(ms + RMS_EPS)
    o_ref[:, :half] = lo * inv * gfin_ref[:, :half]
    o_ref[:, half:] = hi * inv * gfin_ref[:, half:]


def _combine_call(dest, h, gate, g_final, y):
    t, d = h.shape
    tm = COMBINE_TM
    return pl.pallas_call(
        functools.partial(_combine_kernel, tm=tm),
        grid_spec=pltpu.PrefetchScalarGridSpec(
            num_scalar_prefetch=1,
            grid=(t // tm,),
            in_specs=[pl.BlockSpec((tm, d), lambda i, ds: (i, 0)),
                      pl.BlockSpec((tm, TOP_K), lambda i, ds: (i, 0)),
                      pl.BlockSpec((1, d), lambda i, ds: (0, 0)),
                      pl.BlockSpec(memory_space=pl.ANY)],
            out_specs=pl.BlockSpec((tm, d), lambda i, ds: (i, 0)),
            scratch_shapes=[pltpu.VMEM((TOP_K, tm, d // 2), U32), pltpu.SemaphoreType.DMA(())]),
        out_shape=jax.ShapeDtypeStruct((t, d), F32),
        compiler_params=pltpu.CompilerParams(dimension_semantics=("arbitrary",), vmem_limit_bytes=VMEM_LIMIT),
        name="moe_combine",
    )(dest, h, gate, g_final, y)


PAIR_LEAD = 2
TOP_K_BITS = 2
assert 1 << TOP_K_BITS == TOP_K


def _ffn_fused_kernel(te_ref, pair_ref, hn_hbm, wgu_ref, bgu_ref, wd_ref, bd_ref, yk_hbm,
                      xs_buf, y_buf, wgu_bf, wd_bf, gsem, ssem, *, tm, n_tok):
    i = pl.program_id(0)
    n_steps = pl.num_programs(0)
    n_used = te_ref[n_steps]
    half = xs_buf.shape[2]
    trash0 = TOP_K * n_tok

    def pair_of(tile, j):
        return pair_ref[(tile + PAIR_LEAD) * tm + j]

    def gather(tile, j, slot):
        tok = lax.shift_right_logical(jnp.maximum(pair_of(tile, j), 0), TOP_K_BITS)
        return pltpu.make_async_copy(hn_hbm.at[pl.ds(tok, 1)], xs_buf.at[slot, pl.ds(j, 1)], gsem.at[slot])

    def scatter(tile, j, slot):
        pair = pair_of(tile, j)
        real = (pair & (TOP_K - 1)) * n_tok + lax.shift_right_logical(pair, TOP_K_BITS)
        dst = jnp.where(pair >= 0, real, trash0 + slot * tm + j)
        return pltpu.make_async_copy(y_buf.at[slot, pl.ds(j, 1)], yk_hbm.at[pl.ds(dst, 1)], ssem.at[slot])

    def wait_gather(slot):
        pltpu.make_async_copy(hn_hbm.at[pl.ds(0, tm)], xs_buf.at[slot], gsem.at[slot]).wait()

    def wait_scatter(slot):
        pltpu.make_async_copy(y_buf.at[slot], yk_hbm.at[pl.ds(0, tm)], ssem.at[slot]).wait()

    def loop_rows(fn):
        lax.fori_loop(0, tm, lambda j, c: (fn(j), c)[1], 0)

    @pl.when(i == 0)
    def _():
        y_buf[...] = jnp.zeros(y_buf.shape, U32)
        loop_rows(lambda j: gather(0, j, 0).start())
        loop_rows(lambda j: scatter(-2, j, 0).start())

    e = te_ref[i]
    e_prev = te_ref[jnp.maximum(i - 1, 0)]

    @pl.when((i < n_used) & ((i == 0) | (e != e_prev)))
    def _():
        wgu_bf[...] = wgu_ref[0].astype(BF16)
        wd_bf[...] = wd_ref[0].astype(BF16)

    def compute_step(p):
        q = 1 - p
        wait_gather(p)
        w = xs_buf[p]
        x_lo = _unpack_lo(w).astype(BF16)
        x_hi = _unpack_hi(w).astype(BF16)
        for j in range(tm):
            gather(i + 1, j, q).start()
            scatter(i - 1, j, q).start()
        gu = (jnp.dot(x_lo, wgu_bf[0:half, :], preferred_element_type=F32)
              + jnp.dot(x_hi, wgu_bf[half:, :], preferred_element_type=F32) + bgu_ref[0])
        de = gu.shape[1] // 2
        x_glu = jnp.minimum(gu[:, :de], SWIGLU_LIMIT)
        x_lin = jnp.clip(gu[:, de:], -SWIGLU_LIMIT, SWIGLU_LIMIT)
        hdn = x_glu * jax.nn.sigmoid(SWIGLU_ALPHA * x_glu) * (x_lin + 1.0)
        y = jnp.dot(hdn.astype(BF16), wd_bf[...], preferred_element_type=F32) + bd_ref[0]
        bits = pltpu.bitcast(y.astype(BF16).astype(F32), U32)
        wait_scatter(p)
        y_buf[p] = (bits[:, :half] >> 16) | (bits[:, half:] & jnp.uint32(0xFFFF0000))

    for p in range(2):
        @pl.when((i < n_used) & (i % 2 == p))
        def _():
            compute_step(p)

    @pl.when(i == n_used)
    def _():
        s = i % 2
        wait_gather(s)
        loop_rows(lambda j: scatter(i - 1, j, 1 - s).start())
        wait_scatter(s)
        wait_scatter(1 - s)

    @pl.when((i == n_steps - 1) & (i < n_used))
    def _():
        s = i % 2
        wait_gather(1 - s)
        loop_rows(lambda j: scatter(i, j, s).start())
        wait_scatter(1 - s)
        wait_scatter(s)


def _ffn_fused_call(tile_expert, pair_table, hnp, w_gate_up, b_gate_up, w_down, b_down, n_tiles):
    t, half = hnp.shape
    tm = FFN_TM
    ne, d, de2 = w_gate_up.shape
    return pl.pallas_call(
        functools.partial(_ffn_fused_kernel, tm=tm, n_tok=t),
        grid_spec=pltpu.PrefetchScalarGridSpec(
            num_scalar_prefetch=2,
            grid=(n_tiles,),
            in_specs=[pl.BlockSpec(memory_space=pl.ANY),
                      pl.BlockSpec((1, d, de2), lambda i, te, pr: (te[i], 0, 0)),
                      pl.BlockSpec((1, 1, de2), lambda i, te, pr: (te[i], 0, 0)),
                      pl.BlockSpec((1, de2 // 2, d), lambda i, te, pr: (te[i], 0, 0)),
                      pl.BlockSpec((1, 1, d), lambda i, te, pr: (te[i], 0, 0))],
            out_specs=pl.BlockSpec(memory_space=pl.ANY),
            scratch_shapes=[pltpu.VMEM((2, tm, half), U32), pltpu.VMEM((2, tm, half), U32),
                            pltpu.VMEM((d, de2), BF16), pltpu.VMEM((de2 // 2, d), BF16),
                            pltpu.SemaphoreType.DMA((2,)), pltpu.SemaphoreType.DMA((2,))]),
        out_shape=jax.ShapeDtypeStruct((TOP_K * t + 2 * tm, half), U32),
        compiler_params=pltpu.CompilerParams(dimension_semantics=("arbitrary",), vmem_limit_bytes=VMEM_LIMIT,
                                             has_side_effects=True),
        name="moe_ffn",
    )(tile_expert, pair_table, hnp, w_gate_up, b_gate_up, w_down, b_down)


def _combine_dense_kernel(h_ref, gate_ref, gfin_ref, y0_ref, y1_ref, y2_ref, y3_ref, o_ref):
    half = y0_ref.shape[1]
    lo = h_ref[:, :half]
    hi = h_ref[:, half:]
    for k, y_ref in enumerate((y0_ref, y1_ref, y2_ref, y3_ref)):
        g = gate_ref[:, k:k + 1]
        w = y_ref[...]
        lo = lo + g * _unpack_lo(w)
        hi = hi + g * _unpack_hi(w)
    ms = (jnp.sum(lo * lo, axis=1, keepdims=True) + jnp.sum(hi * hi, axis=1, keepdims=True)) / (2 * half)
    inv = lax.rsqrt(ms + RMS_EPS)
    o_ref[:, :half] = lo * inv * gfin_ref[:, :half]
    o_ref[:, half:] = hi * inv * gfin_ref[:, half:]


def _combine_dense_call(h, gate, g_final, yk):
    t, d = h.shape
    tm = COMBINE_TM
    per_k = t // tm
    yspec = lambda k: pl.BlockSpec((tm, d // 2), lambda i: (k * per_k + i, 0))
    return pl.pallas_call(
        _combine_dense_kernel,
        grid=(t // tm,),
        in_specs=[pl.BlockSpec((tm, d), lambda i: (i, 0)),
                  pl.BlockSpec((tm, TOP_K), lambda i: (i, 0)),
                  pl.BlockSpec((1, d), lambda i: (0, 0)),
                  yspec(0), yspec(1), yspec(2), yspec(3)],
        out_specs=pl.BlockSpec((tm, d), lambda i: (i, 0)),
        out_shape=jax.ShapeDtypeStruct((t, d), F32),
        compiler_params=pltpu.CompilerParams(dimension_semantics=("arbitrary",), vmem_limit_bytes=VMEM_LIMIT),
        name="moe_combine",
    )(h, gate, g_final, yk, yk, yk, yk)


def _rope_freqs():
    def inv_freq(dh):
        return (ROPE_THETA ** (-jnp.arange(0, dh, 2, dtype=F32) / dh))[:, None]

    return inv_freq(MLA_ROPE), inv_freq(SWA_HEAD_DIM)


def _winprep_kernel(w_ref, o_ref):
    w = w_ref[...]
    rb = w.shape[0]
    c1 = MLA_Q_RANK + MLA_KV_RANK
    tail = w[:, c1 + MLA_ROPE:]
    lane = lax.broadcasted_iota(I32, (rb, LANES), 1)
    in_rope = (lane >= MLA_NOPE) & (lane < MLA_NOPE + MLA_ROPE)
    kr_p = jnp.where(in_rope, pltpu.roll(w[:, c1:c1 + LANES], MLA_NOPE, axis=1), 0.0)
    pieces = [w[:, :c1], kr_p, tail]
    off = 0
    for pc in pieces:
        o_ref[:, off:off + pc.shape[1]] = pc.astype(BF16)
        off += pc.shape[1]


def _winprep_call(w_in):
    d, n = w_in.shape
    rb = 128
    return pl.pallas_call(
        _winprep_kernel,
        grid=(d // rb,),
        in_specs=[pl.BlockSpec((rb, n), lambda i: (i, 0))],
        out_specs=pl.BlockSpec((rb, _D1), lambda i: (i, 0)),
        out_shape=jax.ShapeDtypeStruct((d, _D1), BF16),
        compiler_params=pltpu.CompilerParams(dimension_semantics=("arbitrary",), vmem_limit_bytes=VMEM_LIMIT),
        name="w_in_prep",
    )(w_in)


def _prep_weights(w_in, w_mla_uq, w_mla_ukv):
    w_in_al = _winprep_call(w_in)

    r = w_mla_uq.shape[0]
    wq = w_mla_uq.reshape(r, MLA_HEADS, MLA_NOPE + MLA_ROPE)
    zq = jnp.zeros((r, MLA_HEADS, LANES - MLA_NOPE - MLA_ROPE), w_mla_uq.dtype)
    wq_pad = jnp.concatenate([wq, zq], axis=-1).reshape(r, MLA_HEADS * LANES).astype(BF16)

    rk = w_mla_ukv.shape[0]
    wkv = w_mla_ukv.reshape(rk, MLA_HEADS, MLA_NOPE + MLA_V)
    wk_aug = jnp.concatenate([wkv[..., :MLA_NOPE], jnp.zeros((rk, MLA_HEADS, LANES - MLA_NOPE), w_mla_ukv.dtype)],
                             axis=-1).reshape(rk, MLA_HEADS * LANES).astype(BF16)
    wv_t = wkv[..., MLA_NOPE:].reshape(rk, MLA_HEADS * MLA_V).T.astype(BF16)
    return w_in_al, wq_pad, wk_aug, wv_t


def kernel(x, mem, positions, g_mix, w_in, g_mla_q, w_mla_uq, g_mla_kv, w_mla_ukv, w_mla_o, swa_sinks, w_swa_o,
           g_mem, w_mem_kv, w_xa_o, b_gate, w_out, g_ffn, w_router, b_router, w_gate_up, b_gate_up, w_down,
           b_down, g_final):
    b, s, d = x.shape
    t = b * s
    depth = g_mix.shape[0]
    h = x.reshape(t, d)
    pos = positions.astype(F32).reshape(1, t)
    fq, f64 = _rope_freqs()
    for l in range(depth):
        w_in_al, wq_pad, wk_aug, wv_t = _prep_weights(w_in[l], w_mla_uq[l], w_mla_ukv[l])
        (qm, km, vmt, qs, ksa, ksb, vsa, vsb, qx, gates) = _proj_call(
            h, pos, fq, f64, g_mix[l][None], w_in_al, g_mla_q[l][None], wq_pad,
            g_mla_kv[l][None], wk_aug, wv_t, b_gate[l][None])
        r3 = lambda a: a.reshape(b, s, a.shape[1])
        omla = _mla_call(r3(qm), r3(km), vmt, s).reshape(t, -1)
        oswa = _swa_call(swa_sinks[l], r3(qs), r3(ksa), r3(ksb), r3(vsa), r3(vsb)).reshape(t, -1)
        m = mem.shape[1]
        kvm = _memkv_call(mem.reshape(b * m, d), g_mem[l][None], w_mem_kv[l].astype(BF16)).reshape(b, m, -1)
        wr_t = w_router[l].T
        wr_hi = wr_t.astype(BF16)
        wr_split = jnp.concatenate([wr_hi, (wr_t - wr_hi.astype(F32)).astype(BF16)], axis=0)
        h_mid, hnp, idx, gate, rank, counts = _merge_call(
            h, omla, oswa, qx, gates, kvm, w_mla_o[l].astype(BF16), w_swa_o[l].astype(BF16),
            w_xa_o[l].astype(BF16), w_out[l].astype(BF16), g_ffn[l][None], wr_split, b_router[l][:, None], s)

        counts = counts[:, 0]
        padded = ((counts + FFN_TM - 1) // FFN_TM) * FFN_TM
        padded_end = jnp.cumsum(padded)
        offsets = padded_end - padded
        experts = jnp.arange(N_EXPERTS, dtype=I32)
        dest = (jnp.sum(jnp.where(idx[..., None] == experts, offsets, 0), axis=-1) + rank).reshape(-1).astype(I32)
        n_tiles = (t * TOP_K) // FFN_TM + N_EXPERTS
        n_used = (padded_end[-1] // FFN_TM).astype(I32)
        ztile = jnp.concatenate([jnp.where(counts % FFN_TM != 0, padded_end - FFN_TM, -1).astype(I32), n_used[None]])
        tile_start = jnp.minimum(jnp.arange(n_tiles, dtype=I32), jnp.maximum(n_used - 1, 0)) * FFN_TM
        tile_expert = jnp.sum((padded_end[None, :] <= tile_start[:, None]).astype(I32), axis=1)
        te = jnp.concatenate([jnp.minimum(tile_expert, N_EXPERTS - 1), n_used[None]])

        xs = _dispatch_call(dest, ztile, hnp, n_tiles * FFN_TM)
        y = _ffn_call(te, xs, w_gate_up[l], b_gate_up[l][:, None, :], w_down[l], b_down[l][:, None, :])
        if l == depth - 1:
            gfin = g_final[None]
            out = _combine_call(dest, h_mid, gate.T, gfin, y)
        else:
            raise NotImplementedError("depth > 1 needs a combine without the final norm")
        h = out
    return h.reshape(b, s, d)
```

```python
import functools
import math

import jax
import jax.numpy as jnp
from jax import lax
from jax.experimental import pallas as pl
from jax.experimental.pallas import tpu as pltpu

F32 = jnp.float32
BF16 = jnp.bfloat16
U32 = jnp.uint32
I32 = jnp.int32

LANES = 128
ROPE_THETA = 10000.0
RMS_EPS = 1e-6
LOG2E = 1.4426950408889634

MLA_HEADS = 8
MLA_NOPE = 64
MLA_ROPE = 32
MLA_V = 64
MLA_Q_RANK = 256
MLA_KV_RANK = 128
SWA_HEADS = 8
SWA_KV_HEADS = 2
SWA_HEAD_DIM = 64
SWA_WINDOW = 128
XA_HEADS = 4
XA_HEAD_DIM = 128
N_EXPERTS = 32
TOP_K = 4
SWIGLU_ALPHA = 1.702
SWIGLU_LIMIT = 7.0
N_BRANCHES = 3

NEG_BIG = -1e30

PROJ_TM = 256
MLA_TQ = 512
SWA_TS = 512
MERGE_TM = 512
MERGE_SUB = 256
FFN_TM = 512
DISPATCH_TM = 256
COMBINE_TM = 256

VMEM_LIMIT = 56 * 1024 * 1024


def _rms(x, g):
    return x * lax.rsqrt(jnp.mean(x * x, axis=-1, keepdims=True) + RMS_EPS) * g


_A0, _A1 = 0, 512
_B0, _B1 = 512, 1280
_C0, _C1 = 1280, 1792
_D0, _D1 = 1792, 4864


def _rotate_half(x, d, lo, hi):
    n = x.shape[1]
    half = (hi - lo) // 2
    lane = lax.broadcasted_iota(I32, x.shape, 1) % d
    up = pltpu.roll(x, n - half, axis=1)
    dn = pltpu.roll(x, half, axis=1)
    return jnp.where((lane >= lo) & (lane < lo + half), -up, jnp.where((lane >= lo + half) & (lane < hi), dn, 0.0))


def _proj_kernel(x_ref, pos_ref, fq_ref, f64_ref, gmix_ref, win_ref, gq_ref, wq_ref,
                 gkv_ref, wk_ref, wv_ref, bgate_ref,
                 qm_ref, km_ref, vm_ref, qs_ref, ksa_ref, ksb_ref, vsa_ref, vsb_ref, qx_ref, gt_ref):
    x = x_ref[...]
    xn = _rms(x, gmix_ref[...]).astype(BF16)
    tm = x.shape[0]
    pos = pos_ref[...]
    ang16 = fq_ref[...] * pos
    ang32 = f64_ref[...] * pos
    c16, s16, c32, s32 = jnp.cos(ang16), jnp.sin(ang16), jnp.cos(ang32), jnp.sin(ang32)
    one = jnp.ones((MLA_NOPE, tm), F32)
    zero = jnp.zeros((MLA_NOPE, tm), F32)
    pad = LANES - MLA_NOPE - MLA_ROPE
    cosq = jnp.concatenate([one, c16, c16, one[:pad]], axis=0).T
    sinq = jnp.concatenate([zero, s16, s16, zero[:pad]], axis=0).T
    cos64 = jnp.concatenate([c32, c32, c32, c32], axis=0).T
    sin64 = jnp.concatenate([s32, s32, s32, s32], axis=0).T
    rope_lo, rope_hi = MLA_NOPE, MLA_NOPE + MLA_ROPE

    xa = jnp.dot(xn, win_ref[:, _A0:_A1], preferred_element_type=F32)
    cqn = _rms(xa[:, 0:256], gq_ref[...]).astype(BF16)
    qa = jnp.dot(cqn, wq_ref[...], preferred_element_type=F32)
    qb = _rotate_half(qa, LANES, rope_lo, rope_hi)
    q_scale = LOG2E / math.sqrt(MLA_NOPE + MLA_ROPE)
    ckvn = _rms(xa[:, 256:384], gkv_ref[...]).astype(BF16)
    ka = jnp.dot(ckvn, wk_ref[...], preferred_element_type=F32)
    kr = xa[:, 384:512]
    krope = kr * cosq + _rotate_half(kr, LANES, rope_lo, rope_hi) * sinq
    for h in range(MLA_HEADS):
        sl = slice(h * LANES, (h + 1) * LANES)
        qm_ref[:, sl] = ((qa[:, sl] * cosq + qb[:, sl] * sinq) * q_scale).astype(BF16)
        km_ref[:, sl] = (ka[:, sl] + krope).astype(BF16)
    vm_ref[...] = lax.dot_general(wv_ref[...], ckvn, (((1,), (1,)), ((), ())),
                                  preferred_element_type=F32).astype(BF16)

    xb = jnp.dot(xn, win_ref[:, _B0:_B1], preferred_element_type=F32)
    s_scale = LOG2E / math.sqrt(SWA_HEAD_DIM)
    nq = SWA_HEADS * SWA_HEAD_DIM
    qs = xb[:, :nq]
    qs_rot = _rotate_half(qs, SWA_HEAD_DIM, 0, SWA_HEAD_DIM)
    for p in range(SWA_HEADS // 2):
        sl = slice(p * LANES, (p + 1) * LANES)
        qs_ref[:, sl] = ((qs[:, sl] * cos64 + qs_rot[:, sl] * sin64) * s_scale).astype(BF16)
    ks = xb[:, nq:nq + LANES]
    ks = ks * cos64 + _rotate_half(ks, SWA_HEAD_DIM, 0, SWA_HEAD_DIM) * sin64
    ksa_ref[...] = ks.astype(BF16)
    ksb_ref[...] = pltpu.roll(ks, 64, axis=1).astype(BF16)
    vs = xb[:, nq + LANES:nq + 2 * LANES]
    vsa_ref[...] = vs.astype(BF16)
    vsb_ref[...] = pltpu.roll(vs, 64, axis=1).astype(BF16)

    xc = jnp.dot(xn, win_ref[:, _C0:_C1], preferred_element_type=F32)
    qx_ref[...] = (xc * (LOG2E / math.sqrt(XA_HEAD_DIM))).astype(BF16)

    xd = jnp.dot(xn, win_ref[:, _D0:_D1], preferred_element_type=F32) + bgate_ref[...]
    gt_ref[...] = jax.nn.sigmoid(xd).astype(BF16)


def _proj_call(x2, pos, fq, f64, g_mix, w_in_al, g_q, wq, g_kv, wk_aug, wv, b_gate):
    t, d = x2.shape
    tm = PROJ_TM
    row = lambda n: pl.BlockSpec((tm, n), lambda i: (i, 0))
    full = lambda a: pl.BlockSpec(a.shape, lambda i: (0,) * a.ndim)
    out_cols = [1024, 1024, 512, 128, 128, 128, 128, 512, 3072]
    out_specs = [row(n) for n in out_cols]
    out_shape = [jax.ShapeDtypeStruct((t, n), BF16) for n in out_cols]
    vt_rows = wv.shape[0]
    out_specs.insert(2, pl.BlockSpec((vt_rows, tm), lambda i: (0, i)))
    out_shape.insert(2, jax.ShapeDtypeStruct((vt_rows, t), BF16))
    return pl.pallas_call(
        _proj_kernel,
        grid=(t // tm,),
        in_specs=[row(d), pl.BlockSpec((1, tm), lambda i: (0, i)), full(fq), full(f64), full(g_mix),
                  pl.BlockSpec(w_in_al.shape, lambda i: (0, 0), pipeline_mode=pl.Buffered(1)),
                  full(g_q), full(wq), full(g_kv), full(wk_aug), full(wv), full(b_gate)],
        out_specs=out_specs,
        out_shape=out_shape,
        compiler_params=pltpu.CompilerParams(dimension_semantics=("arbitrary",), vmem_limit_bytes=VMEM_LIMIT),
        name="proj",
    )(x2, pos, fq, f64, g_mix, w_in_al, g_q, wq, g_kv, wk_aug, wv, b_gate)


def _mla_kernel(q_ref, k_ref, vt_ref, o_ref, sa_ref, sb_ref, m_ref, l_ref, acc_ref, *, tq):
    i = pl.program_id(2)
    m_ref[...] = jnp.full(m_ref.shape, NEG_BIG, F32)
    l_ref[...] = jnp.zeros(l_ref.shape, F32)
    acc_ref[...] = jnp.zeros(acc_ref.shape, F32)

    def scores(j, s_ref):
        k0 = pl.multiple_of(j * tq, tq)
        for hh in range(2):
            sl = slice(hh * LANES, (hh + 1) * LANES)
            s_ref[hh] = lax.dot_general(k_ref[0, pl.ds(k0, tq), sl], q_ref[0, :, sl], (((1,), (1,)), ((), ())),
                                        preferred_element_type=F32)

    def update(j, s_ref, masked):
        k0 = pl.multiple_of(j * tq, tq)
        for hh in range(2):
            vt = vt_ref[hh * MLA_V:(hh + 1) * MLA_V, pl.ds(k0, tq)]
            st = s_ref[hh]
            if masked:
                kj = lax.broadcasted_iota(I32, (tq, tq), 0)
                qi = lax.broadcasted_iota(I32, (tq, tq), 1)
                st = jnp.where(kj <= qi, st, NEG_BIG)
            m_old = m_ref[hh]
            m_new = jnp.maximum(m_old, jnp.max(st, axis=0, keepdims=True))
            alpha = jnp.exp2(m_old - m_new)
            pt = jnp.exp2(st - m_new)
            l_ref[hh] = alpha * l_ref[hh] + jnp.sum(pt, axis=0, keepdims=True)
            acc_ref[hh] = alpha * acc_ref[hh] + jnp.dot(vt, pt.astype(BF16), preferred_element_type=F32)
            m_ref[hh] = m_new

    scores(0, sa_ref)

    def body(jj, carry):
        scores(2 * jj + 1, sb_ref)
        update(2 * jj, sa_ref, False)
        scores(2 * jj + 2, sa_ref)
        update(2 * jj + 1, sb_ref, False)
        return carry

    lax.fori_loop(0, i // 2, body, 0)

    @pl.when(i % 2 == 0)
    def _():
        update(i, sa_ref, True)

    @pl.when(i % 2 == 1)
    def _():
        scores(i, sb_ref)
        update(i - 1, sa_ref, False)
        update(i, sb_ref, True)

    ot = jnp.concatenate([acc_ref[0] / l_ref[0], acc_ref[1] / l_ref[1]], axis=0)
    o_ref[0] = ot.T.astype(BF16)


def _mla_call(q, k, vt, seq):
    b, s, _ = q.shape
    assert s == seq
    tq = min(MLA_TQ, s)
    n_pairs = MLA_HEADS // 2
    return pl.pallas_call(
        functools.partial(_mla_kernel, tq=tq),
        grid=(b, n_pairs, s // tq),
        in_specs=[pl.BlockSpec((1, tq, 2 * LANES), lambda bi, hp, i: (bi, i, hp)),
                  pl.BlockSpec((1, s, 2 * LANES), lambda bi, hp, i: (bi, 0, hp)),
                  pl.BlockSpec((2 * MLA_V, s), lambda bi, hp, i: (hp, bi))],
        out_specs=pl.BlockSpec((1, tq, LANES), lambda bi, hp, i: (bi, i, hp)),
        out_shape=jax.ShapeDtypeStruct((b, s, n_pairs * LANES), BF16),
        scratch_shapes=[pltpu.VMEM((2, tq, tq), F32), pltpu.VMEM((2, tq, tq), F32),
                        pltpu.VMEM((2, 1, tq), F32), pltpu.VMEM((2, 1, tq), F32),
                        pltpu.VMEM((2, MLA_V, tq), F32)],
        compiler_params=pltpu.CompilerParams(dimension_semantics=("arbitrary",) * 3, vmem_limit_bytes=VMEM_LIMIT),
        name="mla_attn",
    )(q, k, vt)


def _swa_kernel(sink_ref, q_ref, ka_ref, kb_ref, va_ref, vb_ref, kah_ref, kbh_ref, vah_ref, vbh_ref, o_ref, *, ts):
    w = SWA_WINDOW
    i = pl.program_id(1)
    ka = jnp.concatenate([kah_ref[0], ka_ref[0]], axis=0)
    kb = jnp.concatenate([kbh_ref[0], kb_ref[0]], axis=0)
    va = jnp.concatenate([vah_ref[0], va_ref[0]], axis=0)
    vb = jnp.concatenate([vbh_ref[0], vb_ref[0]], axis=0)
    lane_k = lax.broadcasted_iota(I32, (2 * w, LANES), 1)
    low = lane_k < SWA_HEAD_DIM
    qi = lax.broadcasted_iota(I32, (2 * w, 2 * w), 0) % w
    kj = lax.broadcasted_iota(I32, (2 * w, 2 * w), 1)
    diff = qi + w - kj
    band = (diff >= 0) & (diff < w)
    lane_o = lax.broadcasted_iota(I32, (w, LANES), 1)
    row2 = lax.broadcasted_iota(I32, (2 * w, 1), 0)
    zero = jnp.zeros((), BF16)
    stacks = ((0, ka, True, va), (1, kb, False, vb), (4, kb, True, vb), (5, ka, False, va))
    for n in range(ts // w):
        mask = band & ((i * (ts // w) + n > 0) | (kj >= w))
        res = []
        for h0, ksrc, keep_low, vsrc in stacks:
            p0 = h0 // 2
            q = jnp.concatenate([q_ref[0, n * w:(n + 1) * w, p0 * LANES:(p0 + 1) * LANES],
                                 q_ref[0, n * w:(n + 1) * w, (p0 + 1) * LANES:(p0 + 2) * LANES]], axis=0)
            kwin = ksrc[n * w:n * w + 2 * w]
            kwin = jnp.where(low if keep_low else ~low, kwin, zero)
            vwin = vsrc[n * w:n * w + 2 * w]
            s = lax.dot_general(q, kwin, (((1,), (1,)), ((), ())), preferred_element_type=F32)
            s = jnp.where(mask, s, NEG_BIG)
            sink = jnp.where(row2 < w, sink_ref[h0], sink_ref[h0 + 2]) * LOG2E
            m = jnp.maximum(jnp.max(s, axis=1, keepdims=True), sink)
            p = jnp.exp2(s - m)
            den = jnp.sum(p, axis=1, keepdims=True) + jnp.exp2(sink - m)
            o = jnp.dot(p.astype(BF16), vwin, preferred_element_type=F32) / den
            res.append(o)
        o02, o13, o46, o57 = res
        sel = lane_o < SWA_HEAD_DIM
        rows = slice(n * w, (n + 1) * w)
        o_ref[0, rows, 0 * LANES:1 * LANES] = jnp.where(sel, o02[:w], o13[:w]).astype(BF16)
        o_ref[0, rows, 1 * LANES:2 * LANES] = jnp.where(sel, o02[w:], o13[w:]).astype(BF16)
        o_ref[0, rows, 2 * LANES:3 * LANES] = jnp.where(sel, o46[:w], o57[:w]).astype(BF16)
        o_ref[0, rows, 3 * LANES:4 * LANES] = jnp.where(sel, o46[w:], o57[w:]).astype(BF16)


def _swa_call(sinks, q, ksa, ksb, vsa, vsb):
    b, s, _ = q.shape
    ts = min(SWA_TS, s)
    w = SWA_WINDOW
    r = ts // w
    main = pl.BlockSpec((1, ts, LANES), lambda bi, i: (bi, i, 0))
    halo = pl.BlockSpec((1, w, LANES), lambda bi, i: (bi, jnp.maximum(i * r - 1, 0), 0))
    return pl.pallas_call(
        functools.partial(_swa_kernel, ts=ts),
        grid=(b, s // ts),
        in_specs=[pl.BlockSpec(memory_space=pltpu.SMEM),
                  pl.BlockSpec((1, ts, 4 * LANES), lambda bi, i: (bi, i, 0)),
                  main, main, main, main, halo, halo, halo, halo],
        out_specs=pl.BlockSpec((1, ts, 4 * LANES), lambda bi, i: (bi, i, 0)),
        out_shape=jax.ShapeDtypeStruct((b, s, 4 * LANES), BF16),
        compiler_params=pltpu.CompilerParams(dimension_semantics=("arbitrary",) * 2, vmem_limit_bytes=VMEM_LIMIT),
        name="swa_attn",
    )(sinks, q, ksa, ksb, vsa, vsb, ksa, ksb, vsa, vsb)


def _memkv_kernel(mem_ref, g_ref, w_ref, o_ref):
    mn = _rms(mem_ref[...], g_ref[...]).astype(BF16)
    o_ref[...] = jnp.dot(mn, w_ref[...], preferred_element_type=F32).astype(BF16)


def _memkv_call(mem2, g_mem, w_mem_kv):
    n, d = mem2.shape
    tm = min(256, n)
    return pl.pallas_call(
        _memkv_kernel,
        grid=(n // tm,),
        in_specs=[pl.BlockSpec((tm, d), lambda i: (i, 0)),
                  pl.BlockSpec(g_mem.shape, lambda i: (0, 0)),
                  pl.BlockSpec(w_mem_kv.shape, lambda i: (0, 0))],
        out_specs=pl.BlockSpec((tm, w_mem_kv.shape[1]), lambda i: (i, 0)),
        out_shape=jax.ShapeDtypeStruct((n, w_mem_kv.shape[1]), BF16),
        compiler_params=pltpu.CompilerParams(dimension_semantics=("arbitrary",), vmem_limit_bytes=VMEM_LIMIT),
        name="mem_kv",
    )(mem2, g_mem, w_mem_kv)


def _merge_kernel(x_ref, omla_ref, oswa_ref, qx_ref, gt_ref, kvm_ref, wmo_ref, wso_ref, wxo_ref, wout_ref,
                  gffn_ref, wr_ref, br_ref,
                  h_ref, hnp_ref, idx_ref, gate_ref, rank_ref, cnt_ref, run_ref, *, tm, sub):
    @pl.when(pl.program_id(0) == 0)
    def _():
        run_ref[...] = jnp.zeros(run_ref.shape, F32)

    d = x_ref.shape[1]
    kv_cols = XA_HEADS * XA_HEAD_DIM
    erow = lax.broadcasted_iota(I32, (N_EXPERTS, sub), 0)
    tri_t = (lax.broadcasted_iota(I32, (sub, sub), 0) < lax.broadcasted_iota(I32, (sub, sub), 1)).astype(BF16)
    nt = (((1,), (1,)), ((), ()))
    run = run_ref[...]
    for hf in range(tm // sub):
        rows = slice(hf * sub, (hf + 1) * sub)

        oxs = []
        for hd in range(XA_HEADS):
            sl = slice(hd * LANES, (hd + 1) * LANES)
            km = kvm_ref[0, :, sl]
            vm = kvm_ref[0, :, kv_cols + hd * LANES:kv_cols + (hd + 1) * LANES]
            s = lax.dot_general(qx_ref[rows, sl], km, nt, preferred_element_type=F32)
            p = jnp.exp2(s - jnp.max(s, axis=1, keepdims=True))
            den = jnp.sum(p, axis=1, keepdims=True)
            oxs.append((jnp.dot(p.astype(BF16), vm, preferred_element_type=F32) / den).astype(BF16))
        oxa = jnp.concatenate(oxs, axis=1)

        merged = (gt_ref[rows, 0:d].astype(F32) * jnp.dot(omla_ref[rows, :], wmo_ref[...], preferred_element_type=F32)
                  + gt_ref[rows, d:2 * d].astype(F32) * jnp.dot(oswa_ref[rows, :], wso_ref[...],
                                                                 preferred_element_type=F32)
                  + gt_ref[rows, 2 * d:3 * d].astype(F32) * jnp.dot(oxa, wxo_ref[...], preferred_element_type=F32))
        h = x_ref[rows, :] + jnp.dot(merged.astype(BF16), wout_ref[...], preferred_element_type=F32)
        h_ref[rows, :] = h

        hn = _rms(h, gffn_ref[...])
        hn_hi = hn.astype(BF16)
        hn_hi32 = hn_hi.astype(F32)
        hn_lo = (hn - hn_hi32).astype(BF16)
        bits = pltpu.bitcast(hn_hi32, U32)
        hnp_ref[rows, :] = (bits[:, : d // 2] >> 16) | (bits[:, d // 2:] & jnp.uint32(0xFFFF0000))

        part = lax.dot_general(wr_ref[...], hn_hi, nt, preferred_element_type=F32)
        logits_t = (part[:N_EXPERTS] + part[N_EXPERTS:]
                    + lax.dot_general(wr_ref[0:N_EXPERTS, :], hn_lo, nt, preferred_element_type=F32) + br_ref[...])

        work = logits_t
        vals, idxs, hots = [], [], []
        for _ in range(TOP_K):
            mx = jnp.max(work, axis=0, keepdims=True)
            ix = jnp.min(jnp.where(work == mx, erow, N_EXPERTS), axis=0, keepdims=True)
            hot = erow == ix
            work = jnp.where(hot, -jnp.inf, work)
            vals.append(mx)
            idxs.append(ix)
            hots.append(hot)
        es = [jnp.exp(v - vals[0]) for v in vals]
        den = es[0] + es[1] + es[2] + es[3]
        sel_t = (hots[0] | hots[1] | hots[2] | hots[3])
        prefix_t = jnp.dot(sel_t.astype(BF16), tri_t, preferred_element_type=F32) + run
        for k in range(TOP_K):
            idx_ref[k:k + 1, rows] = idxs[k]
            gate_ref[k:k + 1, rows] = es[k] / den
            rank_ref[k:k + 1, rows] = jnp.sum(jnp.where(hots[k], prefix_t, 0.0), axis=0, keepdims=True).astype(I32)
        run = run + jnp.sum(sel_t.astype(F32), axis=1, keepdims=True)
    run_ref[...] = run
    cnt_ref[...] = run.astype(I32)


def _merge_call(x2, omla, oswa, qx, gates, kvm, wmo, wso, wxo, wout, g_ffn, wr_split, b_router_col, seq):
    t, d = x2.shape
    tm = MERGE_TM
    per_b = seq // tm
    row = lambda n: pl.BlockSpec((tm, n), lambda i: (i, 0))
    col = lambda: pl.BlockSpec((TOP_K, tm), lambda i: (0, i))
    full = lambda a: pl.BlockSpec(a.shape, lambda i: (0,) * a.ndim)
    return pl.pallas_call(
        functools.partial(_merge_kernel, tm=tm, sub=MERGE_SUB),
        grid=(t // tm,),
        in_specs=[row(d), row(512), row(512), row(512), row(3 * d),
                  pl.BlockSpec((1,) + kvm.shape[1:], lambda i: (i // per_b, 0, 0)),
                  full(wmo), full(wso), full(wxo), full(wout), full(g_ffn), full(wr_split), full(b_router_col)],
        out_specs=[row(d), row(d // 2), col(), col(), col(),
                   pl.BlockSpec((N_EXPERTS, 1), lambda i: (0, 0))],
        out_shape=[jax.ShapeDtypeStruct((t, d), F32), jax.ShapeDtypeStruct((t, d // 2), U32),
                   jax.ShapeDtypeStruct((TOP_K, t), I32), jax.ShapeDtypeStruct((TOP_K, t), F32),
                   jax.ShapeDtypeStruct((TOP_K, t), I32), jax.ShapeDtypeStruct((N_EXPERTS, 1), I32)],
        scratch_shapes=[pltpu.VMEM((N_EXPERTS, 1), F32)],
        compiler_params=pltpu.CompilerParams(dimension_semantics=("arbitrary",), vmem_limit_bytes=VMEM_LIMIT),
        name="merge_router",
    )(x2, omla, oswa, qx, gates, kvm, wmo, wso, wxo, wout, g_ffn, wr_split, b_router_col)


def _dispatch_kernel(dest_ref, ztile_ref, hn_ref, xs_ref, zbuf, sem, zsem, *, tm, n_tok):
    base = pl.program_id(0) * tm

    @pl.when(pl.program_id(0) == 0)
    def _():
        zbuf[...] = jnp.zeros(zbuf.shape, U32)

        def zcopy(e):
            start = pl.multiple_of(ztile_ref[e], FFN_TM)
            return pltpu.make_async_copy(zbuf, xs_ref.at[pl.ds(start, zbuf.shape[0])], zsem)

        for e in range(N_EXPERTS):
            @pl.when(ztile_ref[e] >= 0)
            def _():
                zcopy(e).start()
        for e in range(N_EXPERTS):
            @pl.when(ztile_ref[e] >= 0)
            def _():
                zcopy(e).wait()

        def unused(i):
            start = pl.multiple_of(i * FFN_TM, FFN_TM)
            return pltpu.make_async_copy(zbuf, xs_ref.at[pl.ds(start, zbuf.shape[0])], zsem)

        n_tiles = xs_ref.shape[0] // FFN_TM
        lax.fori_loop(ztile_ref[N_EXPERTS], n_tiles, lambda i, c: (unused(i).start(), c)[1], 0)
        lax.fori_loop(ztile_ref[N_EXPERTS], n_tiles, lambda i, c: (unused(i).wait(), c)[1], 0)

    def issue(tt, carry):
        for k in range(TOP_K):
            dst = dest_ref[k * n_tok + base + tt]
            pltpu.make_async_copy(hn_ref.at[pl.ds(tt, 1)], xs_ref.at[pl.ds(dst, 1)], sem).start(priority=k % 2)
        return carry

    lax.fori_loop(0, tm, issue, 0)
    for _ in range(TOP_K):
        pltpu.make_async_copy(hn_ref, xs_ref.at[pl.ds(0, tm)], sem).wait()


def _dispatch_call(dest, ztile, hnp, n_rows):
    t, c = hnp.shape
    tm = DISPATCH_TM
    return pl.pallas_call(
        functools.partial(_dispatch_kernel, tm=tm, n_tok=t),
        grid_spec=pltpu.PrefetchScalarGridSpec(
            num_scalar_prefetch=2,
            grid=(t // tm,),
            in_specs=[pl.BlockSpec((tm, c), lambda i, ds, zt: (i, 0))],
            out_specs=pl.BlockSpec(memory_space=pl.ANY),
            scratch_shapes=[pltpu.VMEM((FFN_TM, c), U32), pltpu.SemaphoreType.DMA(()),
                            pltpu.SemaphoreType.DMA(())]),
        out_shape=jax.ShapeDtypeStruct((n_rows, c), U32),
        compiler_params=pltpu.CompilerParams(dimension_semantics=("arbitrary",), has_side_effects=True),
        name="moe_dispatch",
    )(dest, ztile, hnp)


def _unpack_lo(w):
    return pltpu.bitcast(w << 16, F32)


def _unpack_hi(w):
    return pltpu.bitcast(w & jnp.uint32(0xFFFF0000), F32)


def _ffn_kernel(te_ref, xs_ref, wgu_ref, bgu_ref, wd_ref, bd_ref, y_ref, wgu_bf, wd_bf):
    i = pl.program_id(0)
    n_used = te_ref[pl.num_programs(0)]
    e = te_ref[i]
    e_prev = te_ref[jnp.maximum(i - 1, 0)]

    @pl.when((i < n_used) & ((i == 0) | (e != e_prev)))
    def _():
        wgu_bf[...] = wgu_ref[0].astype(BF16)
        wd_bf[...] = wd_ref[0].astype(BF16)

    @pl.when(i < n_used)
    def _():
        w = xs_ref[...]
        half = w.shape[1]
        x_lo = _unpack_lo(w).astype(BF16)
        x_hi = _unpack_hi(w).astype(BF16)
        gu = (jnp.dot(x_lo, wgu_bf[0:half, :], preferred_element_type=F32)
              + jnp.dot(x_hi, wgu_bf[half:, :], preferred_element_type=F32) + bgu_ref[0])
        de = gu.shape[1] // 2
        x_glu = jnp.minimum(gu[:, :de], SWIGLU_LIMIT)
        x_lin = jnp.clip(gu[:, de:], -SWIGLU_LIMIT, SWIGLU_LIMIT)
        hdn = x_glu * jax.nn.sigmoid(SWIGLU_ALPHA * x_glu) * (x_lin + 1.0)
        y = jnp.dot(hdn.astype(BF16), wd_bf[...], preferred_element_type=F32) + bd_ref[0]
        bits = pltpu.bitcast(y.astype(BF16).astype(F32), U32)
        y_ref[...] = (bits[:, :half] >> 16) | (bits[:, half:] & jnp.uint32(0xFFFF0000))

    @pl.when(i >= n_used)
    def _():
        y_ref[...] = jnp.zeros(y_ref.shape, U32)


def _ffn_call(tile_expert, xs, w_gate_up, b_gate_up, w_down, b_down):
    r, half = xs.shape
    tm = FFN_TM
    ne, d, de2 = w_gate_up.shape
    return pl.pallas_call(
        _ffn_kernel,
        grid_spec=pltpu.PrefetchScalarGridSpec(
            num_scalar_prefetch=1,
            grid=(r // tm,),
            in_specs=[pl.BlockSpec((tm, half), lambda i, te: (jnp.minimum(i, jnp.maximum(te[r // tm] - 1, 0)), 0)),
                      pl.BlockSpec((1, d, de2), lambda i, te: (te[i], 0, 0)),
                      pl.BlockSpec((1, 1, de2), lambda i, te: (te[i], 0, 0)),
                      pl.BlockSpec((1, de2 // 2, d), lambda i, te: (te[i], 0, 0)),
                      pl.BlockSpec((1, 1, d), lambda i, te: (te[i], 0, 0))],
            out_specs=pl.BlockSpec((tm, half), lambda i, te: (i, 0)),
            scratch_shapes=[pltpu.VMEM((d, de2), BF16), pltpu.VMEM((de2 // 2, d), BF16)]),
        out_shape=jax.ShapeDtypeStruct((r, half), U32),
        compiler_params=pltpu.CompilerParams(dimension_semantics=("arbitrary",), vmem_limit_bytes=VMEM_LIMIT),
        name="moe_ffn",
    )(tile_expert, xs, w_gate_up, b_gate_up, w_down, b_down)


def _combine_kernel(dest_ref, h_ref, gate_ref, gfin_ref, y_ref, o_ref, ybuf, sem, *, tm):
    i = pl.program_id(0)
    base = i * tm
    n_tok = pl.num_programs(0) * tm

    def issue(tt, carry):
        tok = base + tt
        for k in range(TOP_K):
            src = dest_ref[k * n_tok + tok]
            pltpu.make_async_copy(y_ref.at[pl.ds(src, 1)], ybuf.at[k, pl.ds(tt, 1)], sem).start(priority=k % 2)
        return carry

    lax.fori_loop(0, tm, issue, 0)
    for k in range(TOP_K):
        pltpu.make_async_copy(y_ref.at[pl.ds(0, tm)], ybuf.at[k], sem).wait()

    half = ybuf.shape[2]
    lo = h_ref[:, :half]
    hi = h_ref[:, half:]
    for k in range(TOP_K):
        g = gate_ref[:, k:k + 1]
        w = ybuf[k]
        lo = lo + g * _unpack_lo(w)
        hi = hi + g * _unpack_hi(w)
    ms = (jnp.sum(lo * lo, axis=1, keepdims=True) + jnp.sum(hi * hi, axis=1, keepdims=True)) / (2 * half)
    inv = lax.rsqrt(ms + RMS_EPS)
    o_ref[:, :half] = lo * inv * gfin_ref[:, :half]
    o_ref[:, half:] = hi * inv * gfin_ref[:, half:]


def _combine_call(dest, h, gate, g_final, y):
    t, d = h.shape
    tm = COMBINE_TM
    return pl.pallas_call(
        functools.partial(_combine_kernel, tm=tm),
        grid_spec=pltpu.PrefetchScalarGridSpec(
            num_scalar_prefetch=1,
            grid=(t // tm,),
            in_specs=[pl.BlockSpec((tm, d), lambda i, ds: (i, 0)),
                      pl.BlockSpec((tm, TOP_K), lambda i, ds: (i, 0)),
                      pl.BlockSpec((1, d), lambda i, ds: (0, 0)),
                      pl.BlockSpec(memory_space=pl.ANY)],
            out_specs=pl.BlockSpec((tm, d), lambda i, ds: (i, 0)),
            scratch_shapes=[pltpu.VMEM((TOP_K, tm, d // 2), U32), pltpu.SemaphoreType.DMA(())]),
        out_shape=jax.ShapeDtypeStruct((t, d), F32),
        compiler_params=pltpu.CompilerParams(dimension_semantics=("arbitrary",), vmem_limit_bytes=VMEM_LIMIT),
        name="moe_combine",
    )(dest, h, gate, g_final, y)


PAIR_LEAD = 2
TOP_K_BITS = 2
assert 1 << TOP_K_BITS == TOP_K


def _ffn_fused_kernel(te_ref, pair_ref, hn_hbm, wgu_ref, bgu_ref, wd_ref, bd_ref, yk_hbm,
                      xs_buf, y_buf, wgu_bf, wd_bf, gsem, ssem, *, tm, n_tok):
    i = pl.program_id(0)
    n_steps = pl.num_programs(0)
    n_used = te_ref[n_steps]
    half = xs_buf.shape[2]
    trash0 = TOP_K * n_tok

    def pair_of(tile, j):
        return pair_ref[(tile + PAIR_LEAD) * tm + j]

    def gather(tile, j, slot):
        tok = lax.shift_right_logical(jnp.maximum(pair_of(tile, j), 0), TOP_K_BITS)
        return pltpu.make_async_copy(hn_hbm.at[pl.ds(tok, 1)], xs_buf.at[slot, pl.ds(j, 1)], gsem.at[slot])

    def scatter(tile, j, slot):
        pair = pair_of(tile, j)
        real = (pair & (TOP_K - 1)) * n_tok + lax.shift_right_logical(pair, TOP_K_BITS)
        dst = jnp.where(pair >= 0, real, trash0 + slot * tm + j)
        return pltpu.make_async_copy(y_buf.at[slot, pl.ds(j, 1)], yk_hbm.at[pl.ds(dst, 1)], ssem.at[slot])

    def wait_gather(slot):
        pltpu.make_async_copy(hn_hbm.at[pl.ds(0, tm)], xs_buf.at[slot], gsem.at[slot]).wait()

    def wait_scatter(slot):
        pltpu.make_async_copy(y_buf.at[slot], yk_hbm.at[pl.ds(0, tm)], ssem.at[slot]).wait()

    def loop_rows(fn):
        lax.fori_loop(0, tm, lambda j, c: (fn(j), c)[1], 0)

    @pl.when(i == 0)
    def _():
        y_buf[...] = jnp.zeros(y_buf.shape, U32)
        loop_rows(lambda j: gather(0, j, 0).start())
        loop_rows(lambda j: scatter(-2, j, 0).start())

    e = te_ref[i]
    e_prev = te_ref[jnp.maximum(i - 1, 0)]

    @pl.when((i < n_used) & ((i == 0) | (e != e_prev)))
    def _():
        wgu_bf[...] = wgu_ref[0].astype(BF16)
        wd_bf[...] = wd_ref[0].astype(BF16)

    def compute_step(p):
        q = 1 - p
        wait_gather(p)
        w = xs_buf[p]
        x_lo = _unpack_lo(w).astype(BF16)
        x_hi = _unpack_hi(w).astype(BF16)
        for j in range(tm):
            gather(i + 1, j, q).start()
            scatter(i - 1, j, q).start()
        gu = (jnp.dot(x_lo, wgu_bf[0:half, :], preferred_element_type=F32)
              + jnp.dot(x_hi, wgu_bf[half:, :], preferred_element_type=F32) + bgu_ref[0])
        de = gu.shape[1] // 2
        x_glu = jnp.minimum(gu[:, :de], SWIGLU_LIMIT)
        x_lin = jnp.clip(gu[:, de:], -SWIGLU_LIMIT, SWIGLU_LIMIT)
        hdn = x_glu * jax.nn.sigmoid(SWIGLU_ALPHA * x_glu) * (x_lin + 1.0)
        y = jnp.dot(hdn.astype(BF16), wd_bf[...], preferred_element_type=F32) + bd_ref[0]
        bits = pltpu.bitcast(y.astype(BF16).astype(F32), U32)
        wait_scatter(p)
        y_buf[p] = (bits[:, :half] >> 16) | (bits[:, half:] & jnp.uint32(0xFFFF0000))

    for p in range(2):
        @pl.when((i < n_used) & (i % 2 == p))
        def _():
            compute_step(p)

    @pl.when(i == n_used)
    def _():
        s = i % 2
        wait_gather(s)
        loop_rows(lambda j: scatter(i - 1, j, 1 - s).start())
        wait_scatter(s)
        wait_scatter(1 - s)

    @pl.when((i == n_steps - 1) & (i < n_used))
    def _():
        s = i % 2
        wait_gather(1 - s)
        loop_rows(lambda j: scatter(i, j, s).start())
        wait_scatter(1 - s)
        wait_scatter(s)


def _ffn_fused_call(tile_expert, pair_table, hnp, w_gate_up, b_gate_up, w_down, b_down, n_tiles):
    t, half = hnp.shape
    tm = FFN_TM
    ne, d, de2 = w_gate_up.shape
    return pl.pallas_call(
        functools.partial(_ffn_fused_kernel, tm=tm, n_tok=t),
        grid_spec=pltpu.PrefetchScalarGridSpec(
            num_scalar_prefetch=2,
            grid=(n_tiles,),
            in_specs=[pl.BlockSpec(memory_space=pl.ANY),
                      pl.BlockSpec((1, d, de2), lambda i, te, pr: (te[i], 0, 0)),
                      pl.BlockSpec((1, 1, de2), lambda i, te, pr: (te[i], 0, 0)),
                      pl.BlockSpec((1, de2 // 2, d), lambda i, te, pr: (te[i], 0, 0)),
                      pl.BlockSpec((1, 1, d), lambda i, te, pr: (te[i], 0, 0))],
            out_specs=pl.BlockSpec(memory_space=pl.ANY),
            scratch_shapes=[pltpu.VMEM((2, tm, half), U32), pltpu.VMEM((2, tm, half), U32),
                            pltpu.VMEM((d, de2), BF16), pltpu.VMEM((de2 // 2, d), BF16),
                            pltpu.SemaphoreType.DMA((2,)), pltpu.SemaphoreType.DMA((2,))]),
        out_shape=jax.ShapeDtypeStruct((TOP_K * t + 2 * tm, half), U32),
        compiler_params=pltpu.CompilerParams(dimension_semantics=("arbitrary",), vmem_limit_bytes=VMEM_LIMIT,
                                             has_side_effects=True),
        name="moe_ffn",
    )(tile_expert, pair_table, hnp, w_gate_up, b_gate_up, w_down, b_down)


def _combine_dense_kernel(h_ref, gate_ref, gfin_ref, y0_ref, y1_ref, y2_ref, y3_ref, o_ref):
    half = y0_ref.shape[1]
    lo = h_ref[:, :half]
    hi = h_ref[:, half:]
    for k, y_ref in enumerate((y0_ref, y1_ref, y2_ref, y3_ref)):
        g = gate_ref[:, k:k + 1]
        w = y_ref[...]
        lo = lo + g * _unpack_lo(w)
        hi = hi + g * _unpack_hi(w)
    ms = (jnp.sum(lo * lo, axis=1, keepdims=True) + jnp.sum(hi * hi, axis=1, keepdims=True)) / (2 * half)
    inv = lax.rsqrt(ms + RMS_EPS)
    o_ref[:, :half] = lo * inv * gfin_ref[:, :half]
    o_ref[:, half:] = hi * inv * gfin_ref[:, half:]


def _combine_dense_call(h, gate, g_final, yk):
    t, d = h.shape
    tm = COMBINE_TM
    per_k = t // tm
    yspec = lambda k: pl.BlockSpec((tm, d // 2), lambda i: (k * per_k + i, 0))
    return pl.pallas_call(
        _combine_dense_kernel,
        grid=(t // tm,),
        in_specs=[pl.BlockSpec((tm, d), lambda i: (i, 0)),
                  pl.BlockSpec((tm, TOP_K), lambda i: (i, 0)),
                  pl.BlockSpec((1, d), lambda i: (0, 0)),
                  yspec(0), yspec(1), yspec(2), yspec(3)],
        out_specs=pl.BlockSpec((tm, d), lambda i: (i, 0)),
        out_shape=jax.ShapeDtypeStruct((t, d), F32),
        compiler_params=pltpu.CompilerParams(dimension_semantics=("arbitrary",), vmem_limit_bytes=VMEM_LIMIT),
        name="moe_combine",
    )(h, gate, g_final, yk, yk, yk, yk)


def _rope_freqs():
    def inv_freq(dh):
        return (ROPE_THETA ** (-jnp.arange(0, dh, 2, dtype=F32) / dh))[:, None]

    return inv_freq(MLA_ROPE), inv_freq(SWA_HEAD_DIM)


def _winprep_kernel(w_ref, o_ref):
    w = w_ref[...]
    rb = w.shape[0]
    c1 = MLA_Q_RANK + MLA_KV_RANK
    tail = w[:, c1 + MLA_ROPE:]
    lane = lax.broadcasted_iota(I32, (rb, LANES), 1)
    in_rope = (lane >= MLA_NOPE) & (lane < MLA_NOPE + MLA_ROPE)
    kr_p = jnp.where(in_rope, pltpu.roll(w[:, c1:c1 + LANES], MLA_NOPE, axis=1), 0.0)
    pieces = [w[:, :c1], kr_p, tail]
    off = 0
    for pc in pieces:
        o_ref[:, off:off + pc.shape[1]] = pc.astype(BF16)
        off += pc.shape[1]


def _winprep_call(w_in):
    d, n = w_in.shape
    rb = 128
    return pl.pallas_call(
        _winprep_kernel,
        grid=(d // rb,),
        in_specs=[pl.BlockSpec((rb, n), lambda i: (i, 0))],
        out_specs=pl.BlockSpec((rb, _D1), lambda i: (i, 0)),
        out_shape=jax.ShapeDtypeStruct((d, _D1), BF16),
        compiler_params=pltpu.CompilerParams(dimension_semantics=("arbitrary",), vmem_limit_bytes=VMEM_LIMIT),
        name="w_in_prep",
    )(w_in)


def _prep_weights(w_in, w_mla_uq, w_mla_ukv):
    w_in_al = _winprep_call(w_in)

    r = w_mla_uq.shape[0]
    wq = w_mla_uq.reshape(r, MLA_HEADS, MLA_NOPE + MLA_ROPE)
    zq = jnp.zeros((r, MLA_HEADS, LANES - MLA_NOPE - MLA_ROPE), w_mla_uq.dtype)
    wq_pad = jnp.concatenate([wq, zq], axis=-1).reshape(r, MLA_HEADS * LANES).astype(BF16)

    rk = w_mla_ukv.shape[0]
    wkv = w_mla_ukv.reshape(rk, MLA_HEADS, MLA_NOPE + MLA_V)
    wk_aug = jnp.concatenate([wkv[..., :MLA_NOPE], jnp.zeros((rk, MLA_HEADS, LANES - MLA_NOPE), w_mla_ukv.dtype)],
                             axis=-1).reshape(rk, MLA_HEADS * LANES).astype(BF16)
    wv_t = wkv[..., MLA_NOPE:].reshape(rk, MLA_HEADS * MLA_V).T.astype(BF16)
    return w_in_al, wq_pad, wk_aug, wv_t


def kernel(x, mem, positions, g_mix, w_in, g_mla_q, w_mla_uq, g_mla_kv, w_mla_ukv, w_mla_o, swa_sinks, w_swa_o,
           g_mem, w_mem_kv, w_xa_o, b_gate, w_out, g_ffn, w_router, b_router, w_gate_up, b_gate_up, w_down,
           b_down, g_final):
    b, s, d = x.shape
    t = b * s
    depth = g_mix.shape[0]
    h = x.reshape(t, d)
    pos = positions.astype(F32).reshape(1, t)
    fq, f64 = _rope_freqs()
    for l in range(depth):
        w_in_al, wq_pad, wk_aug, wv_t = _prep_weights(w_in[l], w_mla_uq[l], w_mla_ukv[l])
        (qm, km, vmt, qs, ksa, ksb, vsa, vsb, qx, gates) = _proj_call(
            h, pos, fq, f64, g_mix[l][None], w_in_al, g_mla_q[l][None], wq_pad,
            g_mla_kv[l][None], wk_aug, wv_t, b_gate[l][None])
        r3 = lambda a: a.reshape(b, s, a.shape[1])
        omla = _mla_call(r3(qm), r3(km), vmt, s).reshape(t, -1)
        oswa = _swa_call(swa_sinks[l], r3(qs), r3(ksa), r3(ksb), r3(vsa), r3(vsb)).reshape(t, -1)
        m = mem.shape[1]
        kvm = _memkv_call(mem.reshape(b * m, d), g_mem[l][None], w_mem_kv[l].astype(BF16)).reshape(b, m, -1)
        wr_t = w_router[l].T
        wr_hi = wr_t.astype(BF16)
        wr_split = jnp.concatenate([wr_hi, (wr_t - wr_hi.astype(F32)).astype(BF16)], axis=0)
        h_mid, hnp, idx, gate, rank, counts = _merge_call(
            h, omla, oswa, qx, gates, kvm, w_mla_o[l].astype(BF16), w_swa_o[l].astype(BF16),
            w_xa_o[l].astype(BF16), w_out[l].astype(BF16), g_ffn[l][None], wr_split, b_router[l][:, None], s)

        counts = counts[:, 0]
        padded = ((counts + FFN_TM - 1) // FFN_TM) * FFN_TM
        padded_end = jnp.cumsum(padded)
        offsets = padded_end - padded
        experts = jnp.arange(N_EXPERTS, dtype=I32)
        dest = (jnp.sum(jnp.where(idx[..., None] == experts, offsets, 0), axis=-1) + rank).reshape(-1).astype(I32)
        n_tiles = (t * TOP_K) // FFN_TM + N_EXPERTS
        n_used = (padded_end[-1] // FFN_TM).astype(I32)
        ztile = jnp.concatenate([jnp.where(counts % FFN_TM != 0, padded_end - FFN_TM, -1).astype(I32), n_used[None]])
        tile_start = jnp.minimum(jnp.arange(n_tiles, dtype=I32), jnp.maximum(n_used - 1, 0)) * FFN_TM
        tile_expert = jnp.sum((padded_end[None, :] <= tile_start[:, None]).astype(I32), axis=1)
        te = jnp.concatenate([jnp.minimum(tile_expert, N_EXPERTS - 1), n_used[None]])

        xs = _dispatch_call(dest, ztile, hnp, n_tiles * FFN_TM)
        y = _ffn_call(te, xs, w_gate_up[l], b_gate_up[l][:, None, :], w_down[l], b_down[l][:, None, :])
        if l == depth - 1:
            gfin = g_final[None]
            out = _combine_call(dest, h_mid, gate.T, gfin, y)
        else:
            raise NotImplementedError("depth > 1 needs a combine without the final norm")
        h = out
    return h.reshape(b, s, d)
```

```python
import functools
import math

import jax
import jax.numpy as jnp
from jax import lax
from jax.experimental import pallas as pl
from jax.experimental.pallas import tpu as pltpu

F32 = jnp.float32
BF16 = jnp.bfloat16
U32 = jnp.uint32
I32 = jnp.int32

LANES = 128
ROPE_THETA = 10000.0
RMS_EPS = 1e-6
LOG2E = 1.4426950408889634

MLA_HEADS = 8
MLA_NOPE = 64
MLA_ROPE = 32
MLA_V = 64
MLA_Q_RANK = 256
MLA_KV_RANK = 128
SWA_HEADS = 8
SWA_KV_HEADS = 2
SWA_HEAD_DIM = 64
SWA_WINDOW = 128
XA_HEADS = 4
XA_HEAD_DIM = 128
N_EXPERTS = 32
TOP_K = 4
SWIGLU_ALPHA = 1.702
SWIGLU_LIMIT = 7.0
N_BRANCHES = 3

NEG_BIG = -1e30

PROJ_TM = 256
MLA_TQ = 512
SWA_TS = 512
MERGE_TM = 512
MERGE_SUB = 256
FFN_TM = 512
DISPATCH_TM = 256
COMBINE_TM = 256

VMEM_LIMIT = 56 * 1024 * 1024


def _rms(x, g):
    return x * lax.rsqrt(jnp.mean(x * x, axis=-1, keepdims=True) + RMS_EPS) * g


def _store_slabs(ref, row0, value):
    rows, n, _ = ref.shape
    flat = ref.reshape(rows * n, LANES)
    for c in range(n):
        flat[pl.ds(row0 * n + c, value.shape[0], stride=n), :] = value[:, c * LANES:(c + 1) * LANES]


def _load_slabs(ref):
    rows, n, _ = ref.shape
    flat = ref.reshape(rows * n, LANES)
    return jnp.concatenate([flat[pl.ds(c, rows, stride=n), :] for c in range(n)], axis=1)


_A0, _A1 = 0, 512
_B0, _B1 = 512, 1280
_C0, _C1 = 1280, 1792
_D0, _D1 = 1792, 4864


def _rotate_half(x, d, lo, hi):
    n = x.shape[1]
    half = (hi - lo) // 2
    lane = lax.broadcasted_iota(I32, x.shape, 1) % d
    up = pltpu.roll(x, n - half, axis=1)
    dn = pltpu.roll(x, half, axis=1)
    return jnp.where((lane >= lo) & (lane < lo + half), -up, jnp.where((lane >= lo + half) & (lane < hi), dn, 0.0))


def _proj_kernel(x_ref, pos_ref, fq_ref, f64_ref, gmix_ref, win_ref, gq_ref, wq_ref,
                 gkv_ref, wk_ref, wv_ref, bgate_ref,
                 qm_ref, km_ref, vm_ref, qs_ref, ksa_ref, ksb_ref, vsa_ref, vsb_ref, qx_ref, gt_ref):
    x = x_ref[...]
    xn = _rms(x, gmix_ref[...]).astype(BF16)
    tm = x.shape[0]
    pos = pos_ref[...]
    ang16 = fq_ref[...] * pos
    ang32 = f64_ref[...] * pos
    c16, s16, c32, s32 = jnp.cos(ang16), jnp.sin(ang16), jnp.cos(ang32), jnp.sin(ang32)
    one = jnp.ones((MLA_NOPE, tm), F32)
    zero = jnp.zeros((MLA_NOPE, tm), F32)
    pad = LANES - MLA_NOPE - MLA_ROPE
    cosq = jnp.concatenate([one, c16, c16, one[:pad]], axis=0).T
    sinq = jnp.concatenate([zero, s16, s16, zero[:pad]], axis=0).T
    cos64 = jnp.concatenate([c32, c32, c32, c32], axis=0).T
    sin64 = jnp.concatenate([s32, s32, s32, s32], axis=0).T
    rope_lo, rope_hi = MLA_NOPE, MLA_NOPE + MLA_ROPE

    xa = jnp.dot(xn, win_ref[:, _A0:_A1], preferred_element_type=F32)
    cqn = _rms(xa[:, 0:256], gq_ref[...]).astype(BF16)
    qa = jnp.dot(cqn, wq_ref[...], preferred_element_type=F32)
    qb = _rotate_half(qa, LANES, rope_lo, rope_hi)
    q_scale = LOG2E / math.sqrt(MLA_NOPE + MLA_ROPE)
    ckvn = _rms(xa[:, 256:384], gkv_ref[...]).astype(BF16)
    ka = jnp.dot(ckvn, wk_ref[...], preferred_element_type=F32)
    kr = xa[:, 384:512]
    krope = kr * cosq + _rotate_half(kr, LANES, rope_lo, rope_hi) * sinq
    for h in range(MLA_HEADS):
        sl = slice(h * LANES, (h + 1) * LANES)
        qm_ref[:, sl] = ((qa[:, sl] * cosq + qb[:, sl] * sinq) * q_scale).astype(BF16)
        km_ref[:, sl] = (ka[:, sl] + krope).astype(BF16)
    vm_ref[...] = lax.dot_general(wv_ref[...], ckvn, (((1,), (1,)), ((), ())),
                                  preferred_element_type=F32).astype(BF16)

    xb = jnp.dot(xn, win_ref[:, _B0:_B1], preferred_element_type=F32)
    s_scale = LOG2E / math.sqrt(SWA_HEAD_DIM)
    nq = SWA_HEADS * SWA_HEAD_DIM
    qs = xb[:, :nq]
    qs_rot = _rotate_half(qs, SWA_HEAD_DIM, 0, SWA_HEAD_DIM)
    for p in range(SWA_HEADS // 2):
        sl = slice(p * LANES, (p + 1) * LANES)
        qs_ref[:, sl] = ((qs[:, sl] * cos64 + qs_rot[:, sl] * sin64) * s_scale).astype(BF16)
    ks = xb[:, nq:nq + LANES]
    ks = ks * cos64 + _rotate_half(ks, SWA_HEAD_DIM, 0, SWA_HEAD_DIM) * sin64
    ksa_ref[...] = ks.astype(BF16)
    ksb_ref[...] = pltpu.roll(ks, 64, axis=1).astype(BF16)
    vs = xb[:, nq + LANES:nq + 2 * LANES]
    vsa_ref[...] = vs.astype(BF16)
    vsb_ref[...] = pltpu.roll(vs, 64, axis=1).astype(BF16)

    xc = jnp.dot(xn, win_ref[:, _C0:_C1], preferred_element_type=F32)
    qx_ref[...] = (xc * (LOG2E / math.sqrt(XA_HEAD_DIM))).astype(BF16)

    xd = jnp.dot(xn, win_ref[:, _D0:_D1], preferred_element_type=F32) + bgate_ref[...]
    gt_ref[...] = jax.nn.sigmoid(xd).astype(BF16)


def _proj_call(x2, pos, fq, f64, g_mix, w_in_al, g_q, wq, g_kv, wk_aug, wv, b_gate):
    t, d = x2.shape
    tm = PROJ_TM
    row = lambda n: pl.BlockSpec((tm, n), lambda i: (i, 0))
    full = lambda a: pl.BlockSpec(a.shape, lambda i: (0,) * a.ndim)
    out_cols = [1024, 1024, 512, 128, 128, 128, 128, 512, 3072]
    out_specs = [row(n) for n in out_cols]
    out_shape = [jax.ShapeDtypeStruct((t, n), BF16) for n in out_cols]
    vt_rows = wv.shape[0]
    out_specs.insert(2, pl.BlockSpec((vt_rows, tm), lambda i: (0, i)))
    out_shape.insert(2, jax.ShapeDtypeStruct((vt_rows, t), BF16))
    return pl.pallas_call(
        _proj_kernel,
        grid=(t // tm,),
        in_specs=[row(d), pl.BlockSpec((1, tm), lambda i: (0, i)), full(fq), full(f64), full(g_mix),
                  pl.BlockSpec(w_in_al.shape, lambda i: (0, 0), pipeline_mode=pl.Buffered(1)),
                  full(g_q), full(wq), full(g_kv), full(wk_aug), full(wv), full(b_gate)],
        out_specs=out_specs,
        out_shape=out_shape,
        compiler_params=pltpu.CompilerParams(dimension_semantics=("arbitrary",), vmem_limit_bytes=VMEM_LIMIT),
        name="proj",
    )(x2, pos, fq, f64, g_mix, w_in_al, g_q, wq, g_kv, wk_aug, wv, b_gate)


def _mla_kernel(q_ref, k_ref, vt_ref, o_ref, sa_ref, sb_ref, m_ref, l_ref, acc_ref, *, tq):
    i = pl.program_id(2)
    m_ref[...] = jnp.full(m_ref.shape, NEG_BIG, F32)
    l_ref[...] = jnp.zeros(l_ref.shape, F32)
    acc_ref[...] = jnp.zeros(acc_ref.shape, F32)

    def scores(j, s_ref):
        k0 = pl.multiple_of(j * tq, tq)
        for hh in range(2):
            sl = slice(hh * LANES, (hh + 1) * LANES)
            s_ref[hh] = lax.dot_general(k_ref[0, pl.ds(k0, tq), sl], q_ref[0, :, sl], (((1,), (1,)), ((), ())),
                                        preferred_element_type=F32)

    def update(j, s_ref, masked):
        k0 = pl.multiple_of(j * tq, tq)
        for hh in range(2):
            vt = vt_ref[hh * MLA_V:(hh + 1) * MLA_V, pl.ds(k0, tq)]
            st = s_ref[hh]
            if masked:
                kj = lax.broadcasted_iota(I32, (tq, tq), 0)
                qi = lax.broadcasted_iota(I32, (tq, tq), 1)
                st = jnp.where(kj <= qi, st, NEG_BIG)
            m_old = m_ref[hh]
            m_new = jnp.maximum(m_old, jnp.max(st, axis=0, keepdims=True))
            alpha = jnp.exp2(m_old - m_new)
            pt = jnp.exp2(st - m_new)
            l_ref[hh] = alpha * l_ref[hh] + jnp.sum(pt, axis=0, keepdims=True)
            acc_ref[hh] = alpha * acc_ref[hh] + jnp.dot(vt, pt.astype(BF16), preferred_element_type=F32)
            m_ref[hh] = m_new

    scores(0, sa_ref)

    def body(jj, carry):
        scores(2 * jj + 1, sb_ref)
        update(2 * jj, sa_ref, False)
        scores(2 * jj + 2, sa_ref)
        update(2 * jj + 1, sb_ref, False)
        return carry

    lax.fori_loop(0, i // 2, body, 0)

    @pl.when(i % 2 == 0)
    def _():
        update(i, sa_ref, True)

    @pl.when(i % 2 == 1)
    def _():
        scores(i, sb_ref)
        update(i - 1, sa_ref, False)
        update(i, sb_ref, True)

    ot = jnp.concatenate([acc_ref[0] / l_ref[0], acc_ref[1] / l_ref[1]], axis=0)
    o_ref[0] = ot.T.astype(BF16)


def _mla_call(q, k, vt, seq):
    b, s, _ = q.shape
    assert s == seq
    tq = min(MLA_TQ, s)
    n_pairs = MLA_HEADS // 2
    return pl.pallas_call(
        functools.partial(_mla_kernel, tq=tq),
        grid=(b, n_pairs, s // tq),
        in_specs=[pl.BlockSpec((1, tq, 2 * LANES), lambda bi, hp, i: (bi, i, hp)),
                  pl.BlockSpec((1, s, 2 * LANES), lambda bi, hp, i: (bi, 0, hp)),
                  pl.BlockSpec((2 * MLA_V, s), lambda bi, hp, i: (hp, bi))],
        out_specs=pl.BlockSpec((1, tq, LANES), lambda bi, hp, i: (bi, i, hp)),
        out_shape=jax.ShapeDtypeStruct((b, s, n_pairs * LANES), BF16),
        scratch_shapes=[pltpu.VMEM((2, tq, tq), F32), pltpu.VMEM((2, tq, tq), F32),
                        pltpu.VMEM((2, 1, tq), F32), pltpu.VMEM((2, 1, tq), F32),
                        pltpu.VMEM((2, MLA_V, tq), F32)],
        compiler_params=pltpu.CompilerParams(dimension_semantics=("arbitrary",) * 3, vmem_limit_bytes=VMEM_LIMIT),
        name="mla_attn",
    )(q, k, vt)


def _swa_kernel(sink_ref, q_ref, ka_ref, kb_ref, va_ref, vb_ref, kah_ref, kbh_ref, vah_ref, vbh_ref, o_ref, *, ts):
    w = SWA_WINDOW
    i = pl.program_id(1)
    ka = jnp.concatenate([kah_ref[0], ka_ref[0]], axis=0)
    kb = jnp.concatenate([kbh_ref[0], kb_ref[0]], axis=0)
    va = jnp.concatenate([vah_ref[0], va_ref[0]], axis=0)
    vb = jnp.concatenate([vbh_ref[0], vb_ref[0]], axis=0)
    lane_k = lax.broadcasted_iota(I32, (2 * w, LANES), 1)
    low = lane_k < SWA_HEAD_DIM
    qi = lax.broadcasted_iota(I32, (2 * w, 2 * w), 0) % w
    kj = lax.broadcasted_iota(I32, (2 * w, 2 * w), 1)
    diff = qi + w - kj
    band = (diff >= 0) & (diff < w)
    lane_o = lax.broadcasted_iota(I32, (w, LANES), 1)
    row2 = lax.broadcasted_iota(I32, (2 * w, 1), 0)
    zero = jnp.zeros((), BF16)
    stacks = ((0, ka, True, va), (1, kb, False, vb), (4, kb, True, vb), (5, ka, False, va))
    for n in range(ts // w):
        mask = band & ((i * (ts // w) + n > 0) | (kj >= w))
        res = []
        for h0, ksrc, keep_low, vsrc in stacks:
            p0 = h0 // 2
            q = jnp.concatenate([q_ref[0, n * w:(n + 1) * w, p0 * LANES:(p0 + 1) * LANES],
                                 q_ref[0, n * w:(n + 1) * w, (p0 + 1) * LANES:(p0 + 2) * LANES]], axis=0)
            kwin = ksrc[n * w:n * w + 2 * w]
            kwin = jnp.where(low if keep_low else ~low, kwin, zero)
            vwin = vsrc[n * w:n * w + 2 * w]
            s = lax.dot_general(q, kwin, (((1,), (1,)), ((), ())), preferred_element_type=F32)
            s = jnp.where(mask, s, NEG_BIG)
            sink = jnp.where(row2 < w, sink_ref[h0], sink_ref[h0 + 2]) * LOG2E
            m = jnp.maximum(jnp.max(s, axis=1, keepdims=True), sink)
            p = jnp.exp2(s - m)
            den = jnp.sum(p, axis=1, keepdims=True) + jnp.exp2(sink - m)
            o = jnp.dot(p.astype(BF16), vwin, preferred_element_type=F32) / den
            res.append(o)
        o02, o13, o46, o57 = res
        sel = lane_o < SWA_HEAD_DIM
        rows = slice(n * w, (n + 1) * w)
        o_ref[0, rows, 0 * LANES:1 * LANES] = jnp.where(sel, o02[:w], o13[:w]).astype(BF16)
        o_ref[0, rows, 1 * LANES:2 * LANES] = jnp.where(sel, o02[w:], o13[w:]).astype(BF16)
        o_ref[0, rows, 2 * LANES:3 * LANES] = jnp.where(sel, o46[:w], o57[:w]).astype(BF16)
        o_ref[0, rows, 3 * LANES:4 * LANES] = jnp.where(sel, o46[w:], o57[w:]).astype(BF16)


def _swa_call(sinks, q, ksa, ksb, vsa, vsb):
    b, s, _ = q.shape
    ts = min(SWA_TS, s)
    w = SWA_WINDOW
    r = ts // w
    main = pl.BlockSpec((1, ts, LANES), lambda bi, i: (bi, i, 0))
    halo = pl.BlockSpec((1, w, LANES), lambda bi, i: (bi, jnp.maximum(i * r - 1, 0), 0))
    return pl.pallas_call(
        functools.partial(_swa_kernel, ts=ts),
        grid=(b, s // ts),
        in_specs=[pl.BlockSpec(memory_space=pltpu.SMEM),
                  pl.BlockSpec((1, ts, 4 * LANES), lambda bi, i: (bi, i, 0)),
                  main, main, main, main, halo, halo, halo, halo],
        out_specs=pl.BlockSpec((1, ts, 4 * LANES), lambda bi, i: (bi, i, 0)),
        out_shape=jax.ShapeDtypeStruct((b, s, 4 * LANES), BF16),
        compiler_params=pltpu.CompilerParams(dimension_semantics=("arbitrary",) * 2, vmem_limit_bytes=VMEM_LIMIT),
        name="swa_attn",
    )(sinks, q, ksa, ksb, vsa, vsb, ksa, ksb, vsa, vsb)


def _memkv_kernel(mem_ref, g_ref, w_ref, o_ref):
    mn = _rms(mem_ref[...], g_ref[...]).astype(BF16)
    o_ref[...] = jnp.dot(mn, w_ref[...], preferred_element_type=F32).astype(BF16)


def _memkv_call(mem2, g_mem, w_mem_kv):
    n, d = mem2.shape
    tm = min(256, n)
    return pl.pallas_call(
        _memkv_kernel,
        grid=(n // tm,),
        in_specs=[pl.BlockSpec((tm, d), lambda i: (i, 0)),
                  pl.BlockSpec(g_mem.shape, lambda i: (0, 0)),
                  pl.BlockSpec(w_mem_kv.shape, lambda i: (0, 0))],
        out_specs=pl.BlockSpec((tm, w_mem_kv.shape[1]), lambda i: (i, 0)),
        out_shape=jax.ShapeDtypeStruct((n, w_mem_kv.shape[1]), BF16),
        compiler_params=pltpu.CompilerParams(dimension_semantics=("arbitrary",), vmem_limit_bytes=VMEM_LIMIT),
        name="mem_kv",
    )(mem2, g_mem, w_mem_kv)


def _merge_kernel(x_ref, omla_ref, oswa_ref, qx_ref, gt_ref, kvm_ref, wmo_ref, wso_ref, wxo_ref, wout_ref,
                  gffn_ref, wr_ref, br_ref,
                  h_ref, hnp_ref, idx_ref, gate_ref, rank_ref, cnt_ref, run_ref, *, tm, sub):
    @pl.when(pl.program_id(0) == 0)
    def _():
        run_ref[...] = jnp.zeros(run_ref.shape, F32)

    d = x_ref.shape[1]
    kv_cols = XA_HEADS * XA_HEAD_DIM
    erow = lax.broadcasted_iota(I32, (N_EXPERTS, sub), 0)
    tri_t = (lax.broadcasted_iota(I32, (sub, sub), 0) < lax.broadcasted_iota(I32, (sub, sub), 1)).astype(BF16)
    nt = (((1,), (1,)), ((), ()))
    run = run_ref[...]
    for hf in range(tm // sub):
        rows = slice(hf * sub, (hf + 1) * sub)

        oxs = []
        for hd in range(XA_HEADS):
            sl = slice(hd * LANES, (hd + 1) * LANES)
            km = kvm_ref[0, :, sl]
            vm = kvm_ref[0, :, kv_cols + hd * LANES:kv_cols + (hd + 1) * LANES]
            s = lax.dot_general(qx_ref[rows, sl], km, nt, preferred_element_type=F32)
            p = jnp.exp2(s - jnp.max(s, axis=1, keepdims=True))
            den = jnp.sum(p, axis=1, keepdims=True)
            oxs.append((jnp.dot(p.astype(BF16), vm, preferred_element_type=F32) / den).astype(BF16))
        oxa = jnp.concatenate(oxs, axis=1)

        merged = (gt_ref[rows, 0:d].astype(F32) * jnp.dot(omla_ref[rows, :], wmo_ref[...], preferred_element_type=F32)
                  + gt_ref[rows, d:2 * d].astype(F32) * jnp.dot(oswa_ref[rows, :], wso_ref[...],
                                                                 preferred_element_type=F32)
                  + gt_ref[rows, 2 * d:3 * d].astype(F32) * jnp.dot(oxa, wxo_ref[...], preferred_element_type=F32))
        h = x_ref[rows, :] + jnp.dot(merged.astype(BF16), wout_ref[...], preferred_element_type=F32)
        h_ref[rows, :] = h

        hn = _rms(h, gffn_ref[...])
        hn_hi = hn.astype(BF16)
        hn_hi32 = hn_hi.astype(F32)
        hn_lo = (hn - hn_hi32).astype(BF16)
        bits = pltpu.bitcast(hn_hi32, U32)
        _store_slabs(hnp_ref, hf * sub, (bits[:, : d // 2] >> 16) | (bits[:, d // 2:] & jnp.uint32(0xFFFF0000)))

        part = lax.dot_general(wr_ref[...], hn_hi, nt, preferred_element_type=F32)
        logits_t = (part[:N_EXPERTS] + part[N_EXPERTS:]
                    + lax.dot_general(wr_ref[0:N_EXPERTS, :], hn_lo, nt, preferred_element_type=F32) + br_ref[...])

        work = logits_t
        vals, idxs, hots = [], [], []
        for _ in range(TOP_K):
            mx = jnp.max(work, axis=0, keepdims=True)
            ix = jnp.min(jnp.where(work == mx, erow, N_EXPERTS), axis=0, keepdims=True)
            hot = erow == ix
            work = jnp.where(hot, -jnp.inf, work)
            vals.append(mx)
            idxs.append(ix)
            hots.append(hot)
        es = [jnp.exp(v - vals[0]) for v in vals]
        den = es[0] + es[1] + es[2] + es[3]
        sel_t = (hots[0] | hots[1] | hots[2] | hots[3])
        prefix_t = jnp.dot(sel_t.astype(BF16), tri_t, preferred_element_type=F32) + run
        for k in range(TOP_K):
            idx_ref[k:k + 1, rows] = idxs[k]
            gate_ref[k:k + 1, rows] = es[k] / den
            rank_ref[k:k + 1, rows] = jnp.sum(jnp.where(hots[k], prefix_t, 0.0), axis=0, keepdims=True).astype(I32)
        run = run + jnp.sum(sel_t.astype(F32), axis=1, keepdims=True)
    run_ref[...] = run
    cnt_ref[...] = run.astype(I32)


def _merge_call(x2, omla, oswa, qx, gates, kvm, wmo, wso, wxo, wout, g_ffn, wr_split, b_router_col, seq):
    t, d = x2.shape
    tm = MERGE_TM
    per_b = seq // tm
    row = lambda n: pl.BlockSpec((tm, n), lambda i: (i, 0))
    col = lambda: pl.BlockSpec((TOP_K, tm), lambda i: (0, i))
    full = lambda a: pl.BlockSpec(a.shape, lambda i: (0,) * a.ndim)
    return pl.pallas_call(
        functools.partial(_merge_kernel, tm=tm, sub=MERGE_SUB),
        grid=(t // tm,),
        in_specs=[row(d), row(512), row(512), row(512), row(3 * d),
                  pl.BlockSpec((1,) + kvm.shape[1:], lambda i: (i // per_b, 0, 0)),
                  full(wmo), full(wso), full(wxo), full(wout), full(g_ffn), full(wr_split), full(b_router_col)],
        out_specs=[row(d), pl.BlockSpec((tm, d // 2 // LANES, LANES), lambda i: (i, 0, 0)), col(), col(), col(),
                   pl.BlockSpec((N_EXPERTS, 1), lambda i: (0, 0))],
        out_shape=[jax.ShapeDtypeStruct((t, d), F32), jax.ShapeDtypeStruct((t, d // 2 // LANES, LANES), U32),
                   jax.ShapeDtypeStruct((TOP_K, t), I32), jax.ShapeDtypeStruct((TOP_K, t), F32),
                   jax.ShapeDtypeStruct((TOP_K, t), I32), jax.ShapeDtypeStruct((N_EXPERTS, 1), I32)],
        scratch_shapes=[pltpu.VMEM((N_EXPERTS, 1), F32)],
        compiler_params=pltpu.CompilerParams(dimension_semantics=("arbitrary",), vmem_limit_bytes=VMEM_LIMIT),
        name="merge_router",
    )(x2, omla, oswa, qx, gates, kvm, wmo, wso, wxo, wout, g_ffn, wr_split, b_router_col)


def _dispatch_kernel(dest_ref, ztile_ref, hn_ref, xs_ref, zbuf, sem, zsem, *, tm, n_tok):
    base = pl.program_id(0) * tm

    @pl.when(pl.program_id(0) == 0)
    def _():
        zbuf[...] = jnp.zeros(zbuf.shape, U32)

        def zcopy(e):
            start = pl.multiple_of(ztile_ref[e], FFN_TM)
            return pltpu.make_async_copy(zbuf, xs_ref.at[pl.ds(start, zbuf.shape[0])], zsem)

        for e in range(N_EXPERTS):
            @pl.when(ztile_ref[e] >= 0)
            def _():
                zcopy(e).start()
        for e in range(N_EXPERTS):
            @pl.when(ztile_ref[e] >= 0)
            def _():
                zcopy(e).wait()

        def unused(i):
            start = pl.multiple_of(i * FFN_TM, FFN_TM)
            return pltpu.make_async_copy(zbuf, xs_ref.at[pl.ds(start, zbuf.shape[0])], zsem)

        n_tiles = xs_ref.shape[0] // FFN_TM
        lax.fori_loop(ztile_ref[N_EXPERTS], n_tiles, lambda i, c: (unused(i).start(), c)[1], 0)
        lax.fori_loop(ztile_ref[N_EXPERTS], n_tiles, lambda i, c: (unused(i).wait(), c)[1], 0)

    def issue(tt, carry):
        for k in range(TOP_K):
            dst = dest_ref[k * n_tok + base + tt]
            pltpu.make_async_copy(hn_ref.at[pl.ds(tt, 1)], xs_ref.at[pl.ds(dst, 1)], sem).start()
        return carry

    lax.fori_loop(0, tm, issue, 0)
    for _ in range(TOP_K):
        pltpu.make_async_copy(hn_ref, xs_ref.at[pl.ds(0, tm)], sem).wait()


def _dispatch_call(dest, ztile, hnp, n_rows):
    t, ns, c = hnp.shape
    tm = DISPATCH_TM
    return pl.pallas_call(
        functools.partial(_dispatch_kernel, tm=tm, n_tok=t),
        grid_spec=pltpu.PrefetchScalarGridSpec(
            num_scalar_prefetch=2,
            grid=(t // tm,),
            in_specs=[pl.BlockSpec((tm, ns, c), lambda i, ds, zt: (i, 0, 0))],
            out_specs=pl.BlockSpec(memory_space=pl.ANY),
            scratch_shapes=[pltpu.VMEM((FFN_TM, ns, c), U32), pltpu.SemaphoreType.DMA(()),
                            pltpu.SemaphoreType.DMA(())]),
        out_shape=jax.ShapeDtypeStruct((n_rows, ns, c), U32),
        compiler_params=pltpu.CompilerParams(dimension_semantics=("arbitrary",), has_side_effects=True),
        name="moe_dispatch",
    )(dest, ztile, hnp)


def _unpack_lo(w):
    return pltpu.bitcast(w << 16, F32)


def _unpack_hi(w):
    return pltpu.bitcast(w & jnp.uint32(0xFFFF0000), F32)


def _ffn_kernel(te_ref, xs_ref, wgu_ref, bgu_ref, wd_ref, bd_ref, y_ref, wgu_bf, wd_bf):
    i = pl.program_id(0)
    n_used = te_ref[pl.num_programs(0)]
    e = te_ref[i]
    e_prev = te_ref[jnp.maximum(i - 1, 0)]

    @pl.when((i < n_used) & ((i == 0) | (e != e_prev)))
    def _():
        wgu_bf[...] = wgu_ref[0].astype(BF16)
        wd_bf[...] = wd_ref[0].astype(BF16)

    @pl.when(i < n_used)
    def _():
        w = _load_slabs(xs_ref)
        half = w.shape[1]
        x_lo = _unpack_lo(w).astype(BF16)
        x_hi = _unpack_hi(w).astype(BF16)
        gu = (jnp.dot(x_lo, wgu_bf[0:half, :], preferred_element_type=F32)
              + jnp.dot(x_hi, wgu_bf[half:, :], preferred_element_type=F32) + bgu_ref[0])
        de = gu.shape[1] // 2
        x_glu = jnp.minimum(gu[:, :de], SWIGLU_LIMIT)
        x_lin = jnp.clip(gu[:, de:], -SWIGLU_LIMIT, SWIGLU_LIMIT)
        hdn = x_glu * jax.nn.sigmoid(SWIGLU_ALPHA * x_glu) * (x_lin + 1.0)
        y = jnp.dot(hdn.astype(BF16), wd_bf[...], preferred_element_type=F32) + bd_ref[0]
        bits = pltpu.bitcast(y.astype(BF16).astype(F32), U32)
        _store_slabs(y_ref, 0, (bits[:, :half] >> 16) | (bits[:, half:] & jnp.uint32(0xFFFF0000)))

    @pl.when(i >= n_used)
    def _():
        y_ref[...] = jnp.zeros(y_ref.shape, U32)


def _ffn_call(tile_expert, xs, w_gate_up, b_gate_up, w_down, b_down):
    r, ns, lanes = xs.shape
    tm = FFN_TM
    ne, d, de2 = w_gate_up.shape
    return pl.pallas_call(
        _ffn_kernel,
        grid_spec=pltpu.PrefetchScalarGridSpec(
            num_scalar_prefetch=1,
            grid=(r // tm,),
            in_specs=[pl.BlockSpec((tm, ns, lanes),
                                   lambda i, te: (jnp.minimum(i, jnp.maximum(te[r // tm] - 1, 0)), 0, 0)),
                      pl.BlockSpec((1, d, de2), lambda i, te: (te[i], 0, 0)),
                      pl.BlockSpec((1, 1, de2), lambda i, te: (te[i], 0, 0)),
                      pl.BlockSpec((1, de2 // 2, d), lambda i, te: (te[i], 0, 0)),
                      pl.BlockSpec((1, 1, d), lambda i, te: (te[i], 0, 0))],
            out_specs=pl.BlockSpec((tm, ns, lanes), lambda i, te: (i, 0, 0)),
            scratch_shapes=[pltpu.VMEM((d, de2), BF16), pltpu.VMEM((de2 // 2, d), BF16)]),
        out_shape=jax.ShapeDtypeStruct((r, ns, lanes), U32),
        compiler_params=pltpu.CompilerParams(dimension_semantics=("arbitrary",), vmem_limit_bytes=VMEM_LIMIT),
        name="moe_ffn",
    )(tile_expert, xs, w_gate_up, b_gate_up, w_down, b_down)


def _combine_kernel(dest_ref, h_ref, gate_ref, gfin_ref, y_ref, o_ref, ybuf, sem, *, tm):
    i = pl.program_id(0)
    base = i * tm
    n_tok = pl.num_programs(0) * tm

    def issue(tt, carry):
        tok = base + tt
        for k in range(TOP_K):
            src = dest_ref[k * n_tok + tok]
            pltpu.make_async_copy(y_ref.at[pl.ds(src, 1)], ybuf.at[k, pl.ds(tt, 1)], sem).start()
        return carry

    lax.fori_loop(0, tm, issue, 0)
    for k in range(TOP_K):
        pltpu.make_async_copy(y_ref.at[pl.ds(0, tm)], ybuf.at[k], sem).wait()

    half = ybuf.shape[2] * ybuf.shape[3]
    lo = h_ref[:, :half]
    hi = h_ref[:, half:]
    for k in range(TOP_K):
        g = gate_ref[:, k:k + 1]
        w = _load_slabs(ybuf.at[k])
        lo = lo + g * _unpack_lo(w)
        hi = hi + g * _unpack_hi(w)
    ms = (jnp.sum(lo * lo, axis=1, keepdims=True) + jnp.sum(hi * hi, axis=1, keepdims=True)) / (2 * half)
    inv = lax.rsqrt(ms + RMS_EPS)
    o_ref[:, :half] = lo * inv * gfin_ref[:, :half]
    o_ref[:, half:] = hi * inv * gfin_ref[:, half:]


def _combine_call(dest, h, gate, g_final, y):
    t, d = h.shape
    tm = COMBINE_TM
    return pl.pallas_call(
        functools.partial(_combine_kernel, tm=tm),
        grid_spec=pltpu.PrefetchScalarGridSpec(
            num_scalar_prefetch=1,
            grid=(t // tm,),
            in_specs=[pl.BlockSpec((tm, d), lambda i, ds: (i, 0)),
                      pl.BlockSpec((tm, TOP_K), lambda i, ds: (i, 0)),
                      pl.BlockSpec((1, d), lambda i, ds: (0, 0)),
                      pl.BlockSpec(memory_space=pl.ANY)],
            out_specs=pl.BlockSpec((tm, d), lambda i, ds: (i, 0)),
            scratch_shapes=[pltpu.VMEM((TOP_K, tm) + y.shape[1:], U32), pltpu.SemaphoreType.DMA(())]),
        out_shape=jax.ShapeDtypeStruct((t, d), F32),
        compiler_params=pltpu.CompilerParams(dimension_semantics=("arbitrary",), vmem_limit_bytes=VMEM_LIMIT),
        name="moe_combine",
    )(dest, h, gate, g_final, y)


PAIR_LEAD = 2
TOP_K_BITS = 2
assert 1 << TOP_K_BITS == TOP_K


def _ffn_fused_kernel(te_ref, pair_ref, hn_hbm, wgu_ref, bgu_ref, wd_ref, bd_ref, yk_hbm,
                      xs_buf, y_buf, wgu_bf, wd_bf, gsem, ssem, *, tm, n_tok):
    i = pl.program_id(0)
    n_steps = pl.num_programs(0)
    n_used = te_ref[n_steps]
    half = xs_buf.shape[2]
    trash0 = TOP_K * n_tok

    def pair_of(tile, j):
        return pair_ref[(tile + PAIR_LEAD) * tm + j]

    def gather(tile, j, slot):
        tok = lax.shift_right_logical(jnp.maximum(pair_of(tile, j), 0), TOP_K_BITS)
        return pltpu.make_async_copy(hn_hbm.at[pl.ds(tok, 1)], xs_buf.at[slot, pl.ds(j, 1)], gsem.at[slot])

    def scatter(tile, j, slot):
        pair = pair_of(tile, j)
        real = (pair & (TOP_K - 1)) * n_tok + lax.shift_right_logical(pair, TOP_K_BITS)
        dst = jnp.where(pair >= 0, real, trash0 + slot * tm + j)
        return pltpu.make_async_copy(y_buf.at[slot, pl.ds(j, 1)], yk_hbm.at[pl.ds(dst, 1)], ssem.at[slot])

    def wait_gather(slot):
        pltpu.make_async_copy(hn_hbm.at[pl.ds(0, tm)], xs_buf.at[slot], gsem.at[slot]).wait()

    def wait_scatter(slot):
        pltpu.make_async_copy(y_buf.at[slot], yk_hbm.at[pl.ds(0, tm)], ssem.at[slot]).wait()

    def loop_rows(fn):
        lax.fori_loop(0, tm, lambda j, c: (fn(j), c)[1], 0)

    @pl.when(i == 0)
    def _():
        y_buf[...] = jnp.zeros(y_buf.shape, U32)
        loop_rows(lambda j: gather(0, j, 0).start())
        loop_rows(lambda j: scatter(-2, j, 0).start())

    e = te_ref[i]
    e_prev = te_ref[jnp.maximum(i - 1, 0)]

    @pl.when((i < n_used) & ((i == 0) | (e != e_prev)))
    def _():
        wgu_bf[...] = wgu_ref[0].astype(BF16)
        wd_bf[...] = wd_ref[0].astype(BF16)

    def compute_step(p):
        q = 1 - p
        wait_gather(p)
        w = xs_buf[p]
        x_lo = _unpack_lo(w).astype(BF16)
        x_hi = _unpack_hi(w).astype(BF16)
        for j in range(tm):
            gather(i + 1, j, q).start()
            scatter(i - 1, j, q).start()
        gu = (jnp.dot(x_lo, wgu_bf[0:half, :], preferred_element_type=F32)
              + jnp.dot(x_hi, wgu_bf[half:, :], preferred_element_type=F32) + bgu_ref[0])
        de = gu.shape[1] // 2
        x_glu = jnp.minimum(gu[:, :de], SWIGLU_LIMIT)
        x_lin = jnp.clip(gu[:, de:], -SWIGLU_LIMIT, SWIGLU_LIMIT)
        hdn = x_glu * jax.nn.sigmoid(SWIGLU_ALPHA * x_glu) * (x_lin + 1.0)
        y = jnp.dot(hdn.astype(BF16), wd_bf[...], preferred_element_type=F32) + bd_ref[0]
        bits = pltpu.bitcast(y.astype(BF16).astype(F32), U32)
        wait_scatter(p)
        y_buf[p] = (bits[:, :half] >> 16) | (bits[:, half:] & jnp.uint32(0xFFFF0000))

    for p in range(2):
        @pl.when((i < n_used) & (i % 2 == p))
        def _():
            compute_step(p)

    @pl.when(i == n_used)
    def _():
        s = i % 2
        wait_gather(s)
        loop_rows(lambda j: scatter(i - 1, j, 1 - s).start())
        wait_scatter(s)
        wait_scatter(1 - s)

    @pl.when((i == n_steps - 1) & (i < n_used))
    def _():
        s = i % 2
        wait_gather(1 - s)
        loop_rows(lambda j: scatter(i, j, s).start())
        wait_scatter(1 - s)
        wait_scatter(s)


def _ffn_fused_call(tile_expert, pair_table, hnp, w_gate_up, b_gate_up, w_down, b_down, n_tiles):
    t, half = hnp.shape
    tm = FFN_TM
    ne, d, de2 = w_gate_up.shape
    return pl.pallas_call(
        functools.partial(_ffn_fused_kernel, tm=tm, n_tok=t),
        grid_spec=pltpu.PrefetchScalarGridSpec(
            num_scalar_prefetch=2,
            grid=(n_tiles,),
            in_specs=[pl.BlockSpec(memory_space=pl.ANY),
                      pl.BlockSpec((1, d, de2), lambda i, te, pr: (te[i], 0, 0)),
                      pl.BlockSpec((1, 1, de2), lambda i, te, pr: (te[i], 0, 0)),
                      pl.BlockSpec((1, de2 // 2, d), lambda i, te, pr: (te[i], 0, 0)),
                      pl.BlockSpec((1, 1, d), lambda i, te, pr: (te[i], 0, 0))],
            out_specs=pl.BlockSpec(memory_space=pl.ANY),
            scratch_shapes=[pltpu.VMEM((2, tm, half), U32), pltpu.VMEM((2, tm, half), U32),
                            pltpu.VMEM((d, de2), BF16), pltpu.VMEM((de2 // 2, d), BF16),
                            pltpu.SemaphoreType.DMA((2,)), pltpu.SemaphoreType.DMA((2,))]),
        out_shape=jax.ShapeDtypeStruct((TOP_K * t + 2 * tm, half), U32),
        compiler_params=pltpu.CompilerParams(dimension_semantics=("arbitrary",), vmem_limit_bytes=VMEM_LIMIT,
                                             has_side_effects=True),
        name="moe_ffn",
    )(tile_expert, pair_table, hnp, w_gate_up, b_gate_up, w_down, b_down)


def _combine_dense_kernel(h_ref, gate_ref, gfin_ref, y0_ref, y1_ref, y2_ref, y3_ref, o_ref):
    half = y0_ref.shape[1]
    lo = h_ref[:, :half]
    hi = h_ref[:, half:]
    for k, y_ref in enumerate((y0_ref, y1_ref, y2_ref, y3_ref)):
        g = gate_ref[:, k:k + 1]
        w = y_ref[...]
        lo = lo + g * _unpack_lo(w)
        hi = hi + g * _unpack_hi(w)
    ms = (jnp.sum(lo * lo, axis=1, keepdims=True) + jnp.sum(hi * hi, axis=1, keepdims=True)) / (2 * half)
    inv = lax.rsqrt(ms + RMS_EPS)
    o_ref[:, :half] = lo * inv * gfin_ref[:, :half]
    o_ref[:, half:] = hi * inv * gfin_ref[:, half:]


def _combine_dense_call(h, gate, g_final, yk):
    t, d = h.shape
    tm = COMBINE_TM
    per_k = t // tm
    yspec = lambda k: pl.BlockSpec((tm, d // 2), lambda i: (k * per_k + i, 0))
    return pl.pallas_call(
        _combine_dense_kernel,
        grid=(t // tm,),
        in_specs=[pl.BlockSpec((tm, d), lambda i: (i, 0)),
                  pl.BlockSpec((tm, TOP_K), lambda i: (i, 0)),
                  pl.BlockSpec((1, d), lambda i: (0, 0)),
                  yspec(0), yspec(1), yspec(2), yspec(3)],
        out_specs=pl.BlockSpec((tm, d), lambda i: (i, 0)),
        out_shape=jax.ShapeDtypeStruct((t, d), F32),
        compiler_params=pltpu.CompilerParams(dimension_semantics=("arbitrary",), vmem_limit_bytes=VMEM_LIMIT),
        name="moe_combine",
    )(h, gate, g_final, yk, yk, yk, yk)


def _rope_freqs():
    def inv_freq(dh):
        return (ROPE_THETA ** (-jnp.arange(0, dh, 2, dtype=F32) / dh))[:, None]

    return inv_freq(MLA_ROPE), inv_freq(SWA_HEAD_DIM)


def _winprep_kernel(w_ref, o_ref):
    w = w_ref[...]
    rb = w.shape[0]
    c1 = MLA_Q_RANK + MLA_KV_RANK
    tail = w[:, c1 + MLA_ROPE:]
    lane = lax.broadcasted_iota(I32, (rb, LANES), 1)
    in_rope = (lane >= MLA_NOPE) & (lane < MLA_NOPE + MLA_ROPE)
    kr_p = jnp.where(in_rope, pltpu.roll(w[:, c1:c1 + LANES], MLA_NOPE, axis=1), 0.0)
    pieces = [w[:, :c1], kr_p, tail]
    off = 0
    for pc in pieces:
        o_ref[:, off:off + pc.shape[1]] = pc.astype(BF16)
        off += pc.shape[1]


def _winprep_call(w_in):
    d, n = w_in.shape
    rb = 128
    return pl.pallas_call(
        _winprep_kernel,
        grid=(d // rb,),
        in_specs=[pl.BlockSpec((rb, n), lambda i: (i, 0))],
        out_specs=pl.BlockSpec((rb, _D1), lambda i: (i, 0)),
        out_shape=jax.ShapeDtypeStruct((d, _D1), BF16),
        compiler_params=pltpu.CompilerParams(dimension_semantics=("arbitrary",), vmem_limit_bytes=VMEM_LIMIT),
        name="w_in_prep",
    )(w_in)


def _prep_weights(w_in, w_mla_uq, w_mla_ukv):
    w_in_al = _winprep_call(w_in)

    r = w_mla_uq.shape[0]
    wq = w_mla_uq.reshape(r, MLA_HEADS, MLA_NOPE + MLA_ROPE)
    zq = jnp.zeros((r, MLA_HEADS, LANES - MLA_NOPE - MLA_ROPE), w_mla_uq.dtype)
    wq_pad = jnp.concatenate([wq, zq], axis=-1).reshape(r, MLA_HEADS * LANES).astype(BF16)

    rk = w_mla_ukv.shape[0]
    wkv = w_mla_ukv.reshape(rk, MLA_HEADS, MLA_NOPE + MLA_V)
    wk_aug = jnp.concatenate([wkv[..., :MLA_NOPE], jnp.zeros((rk, MLA_HEADS, LANES - MLA_NOPE), w_mla_ukv.dtype)],
                             axis=-1).reshape(rk, MLA_HEADS * LANES).astype(BF16)
    wv_t = wkv[..., MLA_NOPE:].reshape(rk, MLA_HEADS * MLA_V).T.astype(BF16)
    return w_in_al, wq_pad, wk_aug, wv_t


def kernel(x, mem, positions, g_mix, w_in, g_mla_q, w_mla_uq, g_mla_kv, w_mla_ukv, w_mla_o, swa_sinks, w_swa_o,
           g_mem, w_mem_kv, w_xa_o, b_gate, w_out, g_ffn, w_router, b_router, w_gate_up, b_gate_up, w_down,
           b_down, g_final):
    b, s, d = x.shape
    t = b * s
    depth = g_mix.shape[0]
    h = x.reshape(t, d)
    pos = positions.astype(F32).reshape(1, t)
    fq, f64 = _rope_freqs()
    for l in range(depth):
        w_in_al, wq_pad, wk_aug, wv_t = _prep_weights(w_in[l], w_mla_uq[l], w_mla_ukv[l])
        (qm, km, vmt, qs, ksa, ksb, vsa, vsb, qx, gates) = _proj_call(
            h, pos, fq, f64, g_mix[l][None], w_in_al, g_mla_q[l][None], wq_pad,
            g_mla_kv[l][None], wk_aug, wv_t, b_gate[l][None])
        r3 = lambda a: a.reshape(b, s, a.shape[1])
        omla = _mla_call(r3(qm), r3(km), vmt, s).reshape(t, -1)
        oswa = _swa_call(swa_sinks[l], r3(qs), r3(ksa), r3(ksb), r3(vsa), r3(vsb)).reshape(t, -1)
        m = mem.shape[1]
        kvm = _memkv_call(mem.reshape(b * m, d), g_mem[l][None], w_mem_kv[l].astype(BF16)).reshape(b, m, -1)
        wr_t = w_router[l].T
        wr_hi = wr_t.astype(BF16)
        wr_split = jnp.concatenate([wr_hi, (wr_t - wr_hi.astype(F32)).astype(BF16)], axis=0)
        h_mid, hnp, idx, gate, rank, counts = _merge_call(
            h, omla, oswa, qx, gates, kvm, w_mla_o[l].astype(BF16), w_swa_o[l].astype(BF16),
            w_xa_o[l].astype(BF16), w_out[l].astype(BF16), g_ffn[l][None], wr_split, b_router[l][:, None], s)

        counts = counts[:, 0]
        padded = ((counts + FFN_TM - 1) // FFN_TM) * FFN_TM
        padded_end = jnp.cumsum(padded)
        offsets = padded_end - padded
        experts = jnp.arange(N_EXPERTS, dtype=I32)
        dest = (jnp.sum(jnp.where(idx[..., None] == experts, offsets, 0), axis=-1) + rank).reshape(-1).astype(I32)
        n_tiles = (t * TOP_K) // FFN_TM + N_EXPERTS
        n_used = (padded_end[-1] // FFN_TM).astype(I32)
        ztile = jnp.concatenate([jnp.where(counts % FFN_TM != 0, padded_end - FFN_TM, -1).astype(I32), n_used[None]])
        tile_start = jnp.minimum(jnp.arange(n_tiles, dtype=I32), jnp.maximum(n_used - 1, 0)) * FFN_TM
        tile_expert = jnp.sum((padded_end[None, :] <= tile_start[:, None]).astype(I32), axis=1)
        te = jnp.concatenate([jnp.minimum(tile_expert, N_EXPERTS - 1), n_used[None]])

        xs = _dispatch_call(dest, ztile, hnp, n_tiles * FFN_TM)
        y = _ffn_call(te, xs, w_gate_up[l], b_gate_up[l][:, None, :], w_down[l], b_down[l][:, None, :])
        if l == depth - 1:
            gfin = g_final[None]
            out = _combine_call(dest, h_mid, gate.T, gfin, y)
        else:
            raise NotImplementedError("depth > 1 needs a combine without the final norm")
        h = out
    return h.reshape(b, s, d)
```

```python
import functools
import math

import jax
import jax.numpy as jnp
from jax import lax
from jax.experimental import pallas as pl
from jax.experimental.pallas import tpu as pltpu
from jax.experimental.pallas import tpu_sc as plsc

F32 = jnp.float32
BF16 = jnp.bfloat16
U32 = jnp.uint32
I32 = jnp.int32

LANES = 128
ROPE_THETA = 10000.0
RMS_EPS = 1e-6
LOG2E = 1.4426950408889634

MLA_HEADS = 8
MLA_NOPE = 64
MLA_ROPE = 32
MLA_V = 64
MLA_Q_RANK = 256
MLA_KV_RANK = 128
SWA_HEADS = 8
SWA_KV_HEADS = 2
SWA_HEAD_DIM = 64
SWA_WINDOW = 128
XA_HEADS = 4
XA_HEAD_DIM = 128
N_EXPERTS = 32
TOP_K = 4
SWIGLU_ALPHA = 1.702
SWIGLU_LIMIT = 7.0
N_BRANCHES = 3

NEG_BIG = -1e30

PROJ_TM = 256
MLA_TQ = 512
SWA_TS = 512
MERGE_TM = 512
MERGE_SUB = 256
FFN_TM = 512
DISPATCH_TM = 256
COMBINE_TM = 256

VMEM_LIMIT = 56 * 1024 * 1024


def _rms(x, g):
    return x * lax.rsqrt(jnp.mean(x * x, axis=-1, keepdims=True) + RMS_EPS) * g


def _store_slabs(ref, row0, value):
    rows, n, _ = ref.shape
    flat = ref.reshape(rows * n, LANES)
    for c in range(n):
        flat[pl.ds(row0 * n + c, value.shape[0], stride=n), :] = value[:, c * LANES:(c + 1) * LANES]


def _load_slabs(ref):
    rows, n, _ = ref.shape
    flat = ref.reshape(rows * n, LANES)
    return jnp.concatenate([flat[pl.ds(c, rows, stride=n), :] for c in range(n)], axis=1)


_A0, _A1 = 0, 512
_B0, _B1 = 512, 1280
_C0, _C1 = 1280, 1792
_D0, _D1 = 1792, 4864


def _rotate_half(x, d, lo, hi):
    n = x.shape[1]
    half = (hi - lo) // 2
    lane = lax.broadcasted_iota(I32, x.shape, 1) % d
    up = pltpu.roll(x, n - half, axis=1)
    dn = pltpu.roll(x, half, axis=1)
    return jnp.where((lane >= lo) & (lane < lo + half), -up, jnp.where((lane >= lo + half) & (lane < hi), dn, 0.0))


def _proj_kernel(x_ref, pos_ref, fq_ref, f64_ref, gmix_ref, win_ref, gq_ref, wq_ref,
                 gkv_ref, wk_ref, wv_ref, bgate_ref,
                 qm_ref, km_ref, vm_ref, qs_ref, ksa_ref, ksb_ref, vsa_ref, vsb_ref, qx_ref, gt_ref):
    x = x_ref[...]
    xn = _rms(x, gmix_ref[...]).astype(BF16)
    tm = x.shape[0]
    pos = pos_ref[...]
    ang16 = fq_ref[...] * pos
    ang32 = f64_ref[...] * pos
    c16, s16, c32, s32 = jnp.cos(ang16), jnp.sin(ang16), jnp.cos(ang32), jnp.sin(ang32)
    one = jnp.ones((MLA_NOPE, tm), F32)
    zero = jnp.zeros((MLA_NOPE, tm), F32)
    pad = LANES - MLA_NOPE - MLA_ROPE
    cosq = jnp.concatenate([one, c16, c16, one[:pad]], axis=0).T
    sinq = jnp.concatenate([zero, s16, s16, zero[:pad]], axis=0).T
    cos64 = jnp.concatenate([c32, c32, c32, c32], axis=0).T
    sin64 = jnp.concatenate([s32, s32, s32, s32], axis=0).T
    rope_lo, rope_hi = MLA_NOPE, MLA_NOPE + MLA_ROPE

    xa = jnp.dot(xn, win_ref[:, _A0:_A1], preferred_element_type=F32)
    cqn = _rms(xa[:, 0:256], gq_ref[...]).astype(BF16)
    qa = jnp.dot(cqn, wq_ref[...], preferred_element_type=F32)
    qb = _rotate_half(qa, LANES, rope_lo, rope_hi)
    q_scale = LOG2E / math.sqrt(MLA_NOPE + MLA_ROPE)
    ckvn = _rms(xa[:, 256:384], gkv_ref[...]).astype(BF16)
    ka = jnp.dot(ckvn, wk_ref[...], preferred_element_type=F32)
    kr = xa[:, 384:512]
    krope = kr * cosq + _rotate_half(kr, LANES, rope_lo, rope_hi) * sinq
    for h in range(MLA_HEADS):
        sl = slice(h * LANES, (h + 1) * LANES)
        qm_ref[:, sl] = ((qa[:, sl] * cosq + qb[:, sl] * sinq) * q_scale).astype(BF16)
        km_ref[:, sl] = (ka[:, sl] + krope).astype(BF16)
    vm_ref[...] = lax.dot_general(wv_ref[...], ckvn, (((1,), (1,)), ((), ())),
                                  preferred_element_type=F32).astype(BF16)

    xb = jnp.dot(xn, win_ref[:, _B0:_B1], preferred_element_type=F32)
    s_scale = LOG2E / math.sqrt(SWA_HEAD_DIM)
    nq = SWA_HEADS * SWA_HEAD_DIM
    qs = xb[:, :nq]
    qs_rot = _rotate_half(qs, SWA_HEAD_DIM, 0, SWA_HEAD_DIM)
    for p in range(SWA_HEADS // 2):
        sl = slice(p * LANES, (p + 1) * LANES)
        qs_ref[:, sl] = ((qs[:, sl] * cos64 + qs_rot[:, sl] * sin64) * s_scale).astype(BF16)
    ks = xb[:, nq:nq + LANES]
    ks = ks * cos64 + _rotate_half(ks, SWA_HEAD_DIM, 0, SWA_HEAD_DIM) * sin64
    ksa_ref[...] = ks.astype(BF16)
    ksb_ref[...] = pltpu.roll(ks, 64, axis=1).astype(BF16)
    vs = xb[:, nq + LANES:nq + 2 * LANES]
    vsa_ref[...] = vs.astype(BF16)
    vsb_ref[...] = pltpu.roll(vs, 64, axis=1).astype(BF16)

    xc = jnp.dot(xn, win_ref[:, _C0:_C1], preferred_element_type=F32)
    qx_ref[...] = (xc * (LOG2E / math.sqrt(XA_HEAD_DIM))).astype(BF16)

    xd = jnp.dot(xn, win_ref[:, _D0:_D1], preferred_element_type=F32) + bgate_ref[...]
    gt_ref[...] = jax.nn.sigmoid(xd).astype(BF16)


def _proj_call(x2, pos, fq, f64, g_mix, w_in_al, g_q, wq, g_kv, wk_aug, wv, b_gate):
    t, d = x2.shape
    tm = PROJ_TM
    row = lambda n: pl.BlockSpec((tm, n), lambda i: (i, 0))
    full = lambda a: pl.BlockSpec(a.shape, lambda i: (0,) * a.ndim)
    out_cols = [1024, 1024, 512, 128, 128, 128, 128, 512, 3072]
    out_specs = [row(n) for n in out_cols]
    out_shape = [jax.ShapeDtypeStruct((t, n), BF16) for n in out_cols]
    vt_rows = wv.shape[0]
    out_specs.insert(2, pl.BlockSpec((vt_rows, tm), lambda i: (0, i)))
    out_shape.insert(2, jax.ShapeDtypeStruct((vt_rows, t), BF16))
    return pl.pallas_call(
        _proj_kernel,
        grid=(t // tm,),
        in_specs=[row(d), pl.BlockSpec((1, tm), lambda i: (0, i)), full(fq), full(f64), full(g_mix),
                  pl.BlockSpec(w_in_al.shape, lambda i: (0, 0), pipeline_mode=pl.Buffered(1)),
                  full(g_q), full(wq), full(g_kv), full(wk_aug), full(wv), full(b_gate)],
        out_specs=out_specs,
        out_shape=out_shape,
        compiler_params=pltpu.CompilerParams(dimension_semantics=("arbitrary",), vmem_limit_bytes=VMEM_LIMIT),
        name="proj",
    )(x2, pos, fq, f64, g_mix, w_in_al, g_q, wq, g_kv, wk_aug, wv, b_gate)


def _mla_kernel(q_ref, k_ref, vt_ref, o_ref, sa_ref, sb_ref, m_ref, l_ref, acc_ref, *, tq):
    i = pl.program_id(2)
    m_ref[...] = jnp.full(m_ref.shape, NEG_BIG, F32)
    l_ref[...] = jnp.zeros(l_ref.shape, F32)
    acc_ref[...] = jnp.zeros(acc_ref.shape, F32)

    def scores(j, s_ref):
        k0 = pl.multiple_of(j * tq, tq)
        for hh in range(2):
            sl = slice(hh * LANES, (hh + 1) * LANES)
            s_ref[hh] = lax.dot_general(k_ref[0, pl.ds(k0, tq), sl], q_ref[0, :, sl], (((1,), (1,)), ((), ())),
                                        preferred_element_type=F32)

    def update(j, s_ref, masked):
        k0 = pl.multiple_of(j * tq, tq)
        for hh in range(2):
            vt = vt_ref[hh * MLA_V:(hh + 1) * MLA_V, pl.ds(k0, tq)]
            st = s_ref[hh]
            if masked:
                kj = lax.broadcasted_iota(I32, (tq, tq), 0)
                qi = lax.broadcasted_iota(I32, (tq, tq), 1)
                st = jnp.where(kj <= qi, st, NEG_BIG)
            m_old = m_ref[hh]
            m_new = jnp.maximum(m_old, jnp.max(st, axis=0, keepdims=True))
            alpha = jnp.exp2(m_old - m_new)
            pt = jnp.exp2(st - m_new)
            l_ref[hh] = alpha * l_ref[hh] + jnp.sum(pt, axis=0, keepdims=True)
            acc_ref[hh] = alpha * acc_ref[hh] + jnp.dot(vt, pt.astype(BF16), preferred_element_type=F32)
            m_ref[hh] = m_new

    scores(0, sa_ref)

    def body(jj, carry):
        scores(2 * jj + 1, sb_ref)
        update(2 * jj, sa_ref, False)
        scores(2 * jj + 2, sa_ref)
        update(2 * jj + 1, sb_ref, False)
        return carry

    lax.fori_loop(0, i // 2, body, 0)

    @pl.when(i % 2 == 0)
    def _():
        update(i, sa_ref, True)

    @pl.when(i % 2 == 1)
    def _():
        scores(i, sb_ref)
        update(i - 1, sa_ref, False)
        update(i, sb_ref, True)

    ot = jnp.concatenate([acc_ref[0] / l_ref[0], acc_ref[1] / l_ref[1]], axis=0)
    o_ref[0] = ot.T.astype(BF16)


def _mla_call(q, k, vt, seq):
    b, s, _ = q.shape
    assert s == seq
    tq = min(MLA_TQ, s)
    n_pairs = MLA_HEADS // 2
    return pl.pallas_call(
        functools.partial(_mla_kernel, tq=tq),
        grid=(b, n_pairs, s // tq),
        in_specs=[pl.BlockSpec((1, tq, 2 * LANES), lambda bi, hp, i: (bi, i, hp)),
                  pl.BlockSpec((1, s, 2 * LANES), lambda bi, hp, i: (bi, 0, hp)),
                  pl.BlockSpec((2 * MLA_V, s), lambda bi, hp, i: (hp, bi))],
        out_specs=pl.BlockSpec((1, tq, LANES), lambda bi, hp, i: (bi, i, hp)),
        out_shape=jax.ShapeDtypeStruct((b, s, n_pairs * LANES), BF16),
        scratch_shapes=[pltpu.VMEM((2, tq, tq), F32), pltpu.VMEM((2, tq, tq), F32),
                        pltpu.VMEM((2, 1, tq), F32), pltpu.VMEM((2, 1, tq), F32),
                        pltpu.VMEM((2, MLA_V, tq), F32)],
        compiler_params=pltpu.CompilerParams(dimension_semantics=("arbitrary",) * 3, vmem_limit_bytes=VMEM_LIMIT),
        name="mla_attn",
    )(q, k, vt)


def _swa_kernel(sink_ref, q_ref, ka_ref, kb_ref, va_ref, vb_ref, kah_ref, kbh_ref, vah_ref, vbh_ref, o_ref, *, ts):
    w = SWA_WINDOW
    i = pl.program_id(1)
    ka = jnp.concatenate([kah_ref[0], ka_ref[0]], axis=0)
    kb = jnp.concatenate([kbh_ref[0], kb_ref[0]], axis=0)
    va = jnp.concatenate([vah_ref[0], va_ref[0]], axis=0)
    vb = jnp.concatenate([vbh_ref[0], vb_ref[0]], axis=0)
    lane_k = lax.broadcasted_iota(I32, (2 * w, LANES), 1)
    low = lane_k < SWA_HEAD_DIM
    qi = lax.broadcasted_iota(I32, (2 * w, 2 * w), 0) % w
    kj = lax.broadcasted_iota(I32, (2 * w, 2 * w), 1)
    diff = qi + w - kj
    band = (diff >= 0) & (diff < w)
    lane_o = lax.broadcasted_iota(I32, (w, LANES), 1)
    row2 = lax.broadcasted_iota(I32, (2 * w, 1), 0)
    zero = jnp.zeros((), BF16)
    stacks = ((0, ka, True, va), (1, kb, False, vb), (4, kb, True, vb), (5, ka, False, va))
    for n in range(ts // w):
        mask = band & ((i * (ts // w) + n > 0) | (kj >= w))
        res = []
        for h0, ksrc, keep_low, vsrc in stacks:
            p0 = h0 // 2
            q = jnp.concatenate([q_ref[0, n * w:(n + 1) * w, p0 * LANES:(p0 + 1) * LANES],
                                 q_ref[0, n * w:(n + 1) * w, (p0 + 1) * LANES:(p0 + 2) * LANES]], axis=0)
            kwin = ksrc[n * w:n * w + 2 * w]
            kwin = jnp.where(low if keep_low else ~low, kwin, zero)
            vwin = vsrc[n * w:n * w + 2 * w]
            s = lax.dot_general(q, kwin, (((1,), (1,)), ((), ())), preferred_element_type=F32)
            s = jnp.where(mask, s, NEG_BIG)
            sink = jnp.where(row2 < w, sink_ref[h0], sink_ref[h0 + 2]) * LOG2E
            m = jnp.maximum(jnp.max(s, axis=1, keepdims=True), sink)
            p = jnp.exp2(s - m)
            den = jnp.sum(p, axis=1, keepdims=True) + jnp.exp2(sink - m)
            o = jnp.dot(p.astype(BF16), vwin, preferred_element_type=F32) / den
            res.append(o)
        o02, o13, o46, o57 = res
        sel = lane_o < SWA_HEAD_DIM
        rows = slice(n * w, (n + 1) * w)
        o_ref[0, rows, 0 * LANES:1 * LANES] = jnp.where(sel, o02[:w], o13[:w]).astype(BF16)
        o_ref[0, rows, 1 * LANES:2 * LANES] = jnp.where(sel, o02[w:], o13[w:]).astype(BF16)
        o_ref[0, rows, 2 * LANES:3 * LANES] = jnp.where(sel, o46[:w], o57[:w]).astype(BF16)
        o_ref[0, rows, 3 * LANES:4 * LANES] = jnp.where(sel, o46[w:], o57[w:]).astype(BF16)


def _swa_call(sinks, q, ksa, ksb, vsa, vsb):
    b, s, _ = q.shape
    ts = min(SWA_TS, s)
    w = SWA_WINDOW
    r = ts // w
    main = pl.BlockSpec((1, ts, LANES), lambda bi, i: (bi, i, 0))
    halo = pl.BlockSpec((1, w, LANES), lambda bi, i: (bi, jnp.maximum(i * r - 1, 0), 0))
    return pl.pallas_call(
        functools.partial(_swa_kernel, ts=ts),
        grid=(b, s // ts),
        in_specs=[pl.BlockSpec(memory_space=pltpu.SMEM),
                  pl.BlockSpec((1, ts, 4 * LANES), lambda bi, i: (bi, i, 0)),
                  main, main, main, main, halo, halo, halo, halo],
        out_specs=pl.BlockSpec((1, ts, 4 * LANES), lambda bi, i: (bi, i, 0)),
        out_shape=jax.ShapeDtypeStruct((b, s, 4 * LANES), BF16),
        compiler_params=pltpu.CompilerParams(dimension_semantics=("arbitrary",) * 2, vmem_limit_bytes=VMEM_LIMIT),
        name="swa_attn",
    )(sinks, q, ksa, ksb, vsa, vsb, ksa, ksb, vsa, vsb)


def _memkv_kernel(mem_ref, g_ref, w_ref, o_ref):
    mn = _rms(mem_ref[...], g_ref[...]).astype(BF16)
    o_ref[...] = jnp.dot(mn, w_ref[...], preferred_element_type=F32).astype(BF16)


def _memkv_call(mem2, g_mem, w_mem_kv):
    n, d = mem2.shape
    tm = min(256, n)
    return pl.pallas_call(
        _memkv_kernel,
        grid=(n // tm,),
        in_specs=[pl.BlockSpec((tm, d), lambda i: (i, 0)),
                  pl.BlockSpec(g_mem.shape, lambda i: (0, 0)),
                  pl.BlockSpec(w_mem_kv.shape, lambda i: (0, 0))],
        out_specs=pl.BlockSpec((tm, w_mem_kv.shape[1]), lambda i: (i, 0)),
        out_shape=jax.ShapeDtypeStruct((n, w_mem_kv.shape[1]), BF16),
        compiler_params=pltpu.CompilerParams(dimension_semantics=("arbitrary",), vmem_limit_bytes=VMEM_LIMIT),
        name="mem_kv",
    )(mem2, g_mem, w_mem_kv)


def _merge_kernel(x_ref, omla_ref, oswa_ref, qx_ref, gt_ref, kvm_ref, wmo_ref, wso_ref, wxo_ref, wout_ref,
                  gffn_ref, wr_ref, br_ref,
                  h_ref, hnp_ref, idx_ref, gate_ref, rank_ref, cnt_ref, run_ref, *, tm, sub):
    @pl.when(pl.program_id(0) == 0)
    def _():
        run_ref[...] = jnp.zeros(run_ref.shape, F32)

    d = x_ref.shape[1]
    kv_cols = XA_HEADS * XA_HEAD_DIM
    erow = lax.broadcasted_iota(I32, (N_EXPERTS, sub), 0)
    tri_t = (lax.broadcasted_iota(I32, (sub, sub), 0) < lax.broadcasted_iota(I32, (sub, sub), 1)).astype(BF16)
    nt = (((1,), (1,)), ((), ()))
    run = run_ref[...]
    for hf in range(tm // sub):
        rows = slice(hf * sub, (hf + 1) * sub)

        oxs = []
        for hd in range(XA_HEADS):
            sl = slice(hd * LANES, (hd + 1) * LANES)
            km = kvm_ref[0, :, sl]
            vm = kvm_ref[0, :, kv_cols + hd * LANES:kv_cols + (hd + 1) * LANES]
            s = lax.dot_general(qx_ref[rows, sl], km, nt, preferred_element_type=F32)
            p = jnp.exp2(s - jnp.max(s, axis=1, keepdims=True))
            den = jnp.sum(p, axis=1, keepdims=True)
            oxs.append((jnp.dot(p.astype(BF16), vm, preferred_element_type=F32) / den).astype(BF16))
        oxa = jnp.concatenate(oxs, axis=1)

        merged = (gt_ref[rows, 0:d].astype(F32) * jnp.dot(omla_ref[rows, :], wmo_ref[...], preferred_element_type=F32)
                  + gt_ref[rows, d:2 * d].astype(F32) * jnp.dot(oswa_ref[rows, :], wso_ref[...],
                                                                 preferred_element_type=F32)
                  + gt_ref[rows, 2 * d:3 * d].astype(F32) * jnp.dot(oxa, wxo_ref[...], preferred_element_type=F32))
        h = x_ref[rows, :] + jnp.dot(merged.astype(BF16), wout_ref[...], preferred_element_type=F32)
        h_ref[rows, :] = h

        hn = _rms(h, gffn_ref[...])
        hn_hi = hn.astype(BF16)
        hn_hi32 = hn_hi.astype(F32)
        hn_lo = (hn - hn_hi32).astype(BF16)
        bits = pltpu.bitcast(hn_hi32, U32)
        _store_slabs(hnp_ref, hf * sub, (bits[:, : d // 2] >> 16) | (bits[:, d // 2:] & jnp.uint32(0xFFFF0000)))

        part = lax.dot_general(wr_ref[...], hn_hi, nt, preferred_element_type=F32)
        logits_t = (part[:N_EXPERTS] + part[N_EXPERTS:]
                    + lax.dot_general(wr_ref[0:N_EXPERTS, :], hn_lo, nt, preferred_element_type=F32) + br_ref[...])

        work = logits_t
        vals, idxs, hots = [], [], []
        for _ in range(TOP_K):
            mx = jnp.max(work, axis=0, keepdims=True)
            ix = jnp.min(jnp.where(work == mx, erow, N_EXPERTS), axis=0, keepdims=True)
            hot = erow == ix
            work = jnp.where(hot, -jnp.inf, work)
            vals.append(mx)
            idxs.append(ix)
            hots.append(hot)
        es = [jnp.exp(v - vals[0]) for v in vals]
        den = es[0] + es[1] + es[2] + es[3]
        sel_t = (hots[0] | hots[1] | hots[2] | hots[3])
        prefix_t = jnp.dot(sel_t.astype(BF16), tri_t, preferred_element_type=F32) + run
        for k in range(TOP_K):
            idx_ref[k:k + 1, rows] = idxs[k]
            gate_ref[k:k + 1, rows] = es[k] / den
            rank_ref[k:k + 1, rows] = jnp.sum(jnp.where(hots[k], prefix_t, 0.0), axis=0, keepdims=True).astype(I32)
        run = run + jnp.sum(sel_t.astype(F32), axis=1, keepdims=True)
    run_ref[...] = run
    cnt_ref[...] = run.astype(I32)


def _merge_call(x2, omla, oswa, qx, gates, kvm, wmo, wso, wxo, wout, g_ffn, wr_split, b_router_col, seq):
    t, d = x2.shape
    tm = MERGE_TM
    per_b = seq // tm
    row = lambda n: pl.BlockSpec((tm, n), lambda i: (i, 0))
    col = lambda: pl.BlockSpec((TOP_K, tm), lambda i: (0, i))
    full = lambda a: pl.BlockSpec(a.shape, lambda i: (0,) * a.ndim)
    return pl.pallas_call(
        functools.partial(_merge_kernel, tm=tm, sub=MERGE_SUB),
        grid=(t // tm,),
        in_specs=[row(d), row(512), row(512), row(512), row(3 * d),
                  pl.BlockSpec((1,) + kvm.shape[1:], lambda i: (i // per_b, 0, 0)),
                  full(wmo), full(wso), full(wxo), full(wout), full(g_ffn), full(wr_split), full(b_router_col)],
        out_specs=[row(d), pl.BlockSpec((tm, d // 2 // LANES, LANES), lambda i: (i, 0, 0)), col(), col(), col(),
                   pl.BlockSpec((N_EXPERTS, 1), lambda i: (0, 0))],
        out_shape=[jax.ShapeDtypeStruct((t, d), F32), jax.ShapeDtypeStruct((t, d // 2 // LANES, LANES), U32),
                   jax.ShapeDtypeStruct((TOP_K, t), I32), jax.ShapeDtypeStruct((TOP_K, t), F32),
                   jax.ShapeDtypeStruct((TOP_K, t), I32), jax.ShapeDtypeStruct((N_EXPERTS, 1), I32)],
        scratch_shapes=[pltpu.VMEM((N_EXPERTS, 1), F32)],
        compiler_params=pltpu.CompilerParams(dimension_semantics=("arbitrary",), vmem_limit_bytes=VMEM_LIMIT),
        name="merge_router",
    )(x2, omla, oswa, qx, gates, kvm, wmo, wso, wxo, wout, g_ffn, wr_split, b_router_col)


def _dispatch_kernel(dest_ref, ztile_ref, hn_ref, xs_ref, zbuf, sem, zsem, *, tm, n_tok):
    base = pl.program_id(0) * tm

    @pl.when(pl.program_id(0) == 0)
    def _():
        zbuf[...] = jnp.zeros(zbuf.shape, U32)

        def zcopy(e):
            start = pl.multiple_of(ztile_ref[e], FFN_TM)
            return pltpu.make_async_copy(zbuf, xs_ref.at[pl.ds(start, zbuf.shape[0])], zsem)

        for e in range(N_EXPERTS):
            @pl.when(ztile_ref[e] >= 0)
            def _():
                zcopy(e).start()
        for e in range(N_EXPERTS):
            @pl.when(ztile_ref[e] >= 0)
            def _():
                zcopy(e).wait()

        def unused(i):
            start = pl.multiple_of(i * FFN_TM, FFN_TM)
            return pltpu.make_async_copy(zbuf, xs_ref.at[pl.ds(start, zbuf.shape[0])], zsem)

        n_tiles = xs_ref.shape[0] // FFN_TM
        lax.fori_loop(ztile_ref[N_EXPERTS], n_tiles, lambda i, c: (unused(i).start(), c)[1], 0)
        lax.fori_loop(ztile_ref[N_EXPERTS], n_tiles, lambda i, c: (unused(i).wait(), c)[1], 0)

    def issue(tt, carry):
        for k in range(TOP_K):
            dst = dest_ref[k * n_tok + base + tt]
            pltpu.make_async_copy(hn_ref.at[pl.ds(tt, 1)], xs_ref.at[pl.ds(dst, 1)], sem).start()
        return carry

    lax.fori_loop(0, tm, issue, 0)
    for _ in range(TOP_K):
        pltpu.make_async_copy(hn_ref, xs_ref.at[pl.ds(0, tm)], sem).wait()


def _dispatch_call(dest, ztile, hnp, n_rows):
    t, ns, c = hnp.shape
    tm = DISPATCH_TM
    return pl.pallas_call(
        functools.partial(_dispatch_kernel, tm=tm, n_tok=t),
        grid_spec=pltpu.PrefetchScalarGridSpec(
            num_scalar_prefetch=2,
            grid=(t // tm,),
            in_specs=[pl.BlockSpec((tm, ns, c), lambda i, ds, zt: (i, 0, 0))],
            out_specs=pl.BlockSpec(memory_space=pl.ANY),
            scratch_shapes=[pltpu.VMEM((FFN_TM, ns, c), U32), pltpu.SemaphoreType.DMA(()),
                            pltpu.SemaphoreType.DMA(())]),
        out_shape=jax.ShapeDtypeStruct((n_rows, ns, c), U32),
        compiler_params=pltpu.CompilerParams(dimension_semantics=("arbitrary",), has_side_effects=True),
        name="moe_dispatch",
    )(dest, ztile, hnp)


SC_CORES = 2
SC_SUBCORES = 16
SC_WORKERS = SC_CORES * SC_SUBCORES
SC_CHUNK = 64


def _sc_mesh():
    return plsc.VectorSubcoreMesh(core_axis_name="c", subcore_axis_name="s",
                                  num_cores=SC_CORES, num_subcores=SC_SUBCORES)


def _sc_index_blocks(idx):
    n = idx.shape[0]
    per_w = n // SC_WORKERS
    n_ch = per_w // SC_CHUNK
    assert per_w * SC_WORKERS == n and n_ch * SC_CHUNK == per_w
    return idx.reshape(SC_WORKERS, n_ch, SC_CHUNK), per_w, n_ch


def _sc_scatter_rows(src, idx, n_out):
    idx3, per_w, n_ch = _sc_index_blocks(idx)
    n_src = src.shape[0]
    assert n_src % per_w == 0

    @functools.partial(
        pl.kernel, mesh=_sc_mesh(),
        out_type=jax.ShapeDtypeStruct((n_out,) + src.shape[1:], src.dtype),
        scratch_types=[pltpu.VMEM((n_ch, SC_CHUNK), I32), pltpu.VMEM((SC_CHUNK,) + src.shape[1:], src.dtype),
                       pltpu.SemaphoreType.DMA],
        name="moe_dispatch_sc")
    def k(src_hbm, idx_hbm, out_hbm, idx_v, rows_v, sem):
        wid = lax.axis_index("s") * SC_CORES + lax.axis_index("c")
        base = lax.rem(wid * per_w, n_src)
        pltpu.sync_copy(idx_hbm.at[wid], idx_v)

        @pl.loop(0, n_ch)
        def _(j):
            pltpu.sync_copy(src_hbm.at[pl.ds(base + j * SC_CHUNK, SC_CHUNK)], rows_v)
            pltpu.async_copy(rows_v, out_hbm.at[idx_v.at[j]], sem).wait()

    return k(src, idx3)


def _sc_gather_rows(table, idx):
    idx3, per_w, n_ch = _sc_index_blocks(idx)

    @functools.partial(
        pl.kernel, mesh=_sc_mesh(),
        out_type=jax.ShapeDtypeStruct((idx.shape[0],) + table.shape[1:], table.dtype),
        scratch_types=[pltpu.VMEM((n_ch, SC_CHUNK), I32), pltpu.VMEM((SC_CHUNK,) + table.shape[1:], table.dtype),
                       pltpu.SemaphoreType.DMA],
        name="moe_gather_sc")
    def k(table_hbm, idx_hbm, out_hbm, idx_v, rows_v, sem):
        wid = lax.axis_index("s") * SC_CORES + lax.axis_index("c")
        base = wid * per_w
        pltpu.sync_copy(idx_hbm.at[wid], idx_v)

        @pl.loop(0, n_ch)
        def _(j):
            pltpu.async_copy(table_hbm.at[idx_v.at[j]], rows_v, sem).wait()
            pltpu.sync_copy(rows_v, out_hbm.at[pl.ds(base + j * SC_CHUNK, SC_CHUNK)])

    return k(table, idx3)


def _unpack_lo(w):
    return pltpu.bitcast(w << 16, F32)


def _unpack_hi(w):
    return pltpu.bitcast(w & jnp.uint32(0xFFFF0000), F32)


def _ffn_kernel(te_ref, xs_ref, wgu_ref, bgu_ref, wd_ref, bd_ref, y_ref, wgu_bf, wd_bf):
    i = pl.program_id(0)
    n_used = te_ref[pl.num_programs(0)]
    e = te_ref[i]
    e_prev = te_ref[jnp.maximum(i - 1, 0)]

    @pl.when((i < n_used) & ((i == 0) | (e != e_prev)))
    def _():
        wgu_bf[...] = wgu_ref[0].astype(BF16)
        wd_bf[...] = wd_ref[0].astype(BF16)

    @pl.when(i < n_used)
    def _():
        w = _load_slabs(xs_ref)
        half = w.shape[1]
        valid = te_ref[pl.num_programs(0) + 1 + i]
        w = jnp.where(lax.broadcasted_iota(I32, w.shape, 0) < valid, w, jnp.uint32(0))
        x_lo = _unpack_lo(w).astype(BF16)
        x_hi = _unpack_hi(w).astype(BF16)
        gu = (jnp.dot(x_lo, wgu_bf[0:half, :], preferred_element_type=F32)
              + jnp.dot(x_hi, wgu_bf[half:, :], preferred_element_type=F32) + bgu_ref[0])
        de = gu.shape[1] // 2
        x_glu = jnp.minimum(gu[:, :de], SWIGLU_LIMIT)
        x_lin = jnp.clip(gu[:, de:], -SWIGLU_LIMIT, SWIGLU_LIMIT)
        hdn = x_glu * jax.nn.sigmoid(SWIGLU_ALPHA * x_glu) * (x_lin + 1.0)
        y = jnp.dot(hdn.astype(BF16), wd_bf[...], preferred_element_type=F32) + bd_ref[0]
        bits = pltpu.bitcast(y.astype(BF16).astype(F32), U32)
        _store_slabs(y_ref, 0, (bits[:, :half] >> 16) | (bits[:, half:] & jnp.uint32(0xFFFF0000)))

    @pl.when(i >= n_used)
    def _():
        y_ref[...] = jnp.zeros(y_ref.shape, U32)


def _ffn_call(tile_expert, xs, w_gate_up, b_gate_up, w_down, b_down):
    r, ns, lanes = xs.shape
    tm = FFN_TM
    ne, d, de2 = w_gate_up.shape
    return pl.pallas_call(
        _ffn_kernel,
        grid_spec=pltpu.PrefetchScalarGridSpec(
            num_scalar_prefetch=1,
            grid=(r // tm,),
            in_specs=[pl.BlockSpec((tm, ns, lanes),
                                   lambda i, te: (jnp.minimum(i, jnp.maximum(te[r // tm] - 1, 0)), 0, 0)),
                      pl.BlockSpec((1, d, de2), lambda i, te: (te[i], 0, 0)),
                      pl.BlockSpec((1, 1, de2), lambda i, te: (te[i], 0, 0)),
                      pl.BlockSpec((1, de2 // 2, d), lambda i, te: (te[i], 0, 0)),
                      pl.BlockSpec((1, 1, d), lambda i, te: (te[i], 0, 0))],
            out_specs=pl.BlockSpec((tm, ns, lanes), lambda i, te: (i, 0, 0)),
            scratch_shapes=[pltpu.VMEM((d, de2), BF16), pltpu.VMEM((de2 // 2, d), BF16)]),
        out_shape=jax.ShapeDtypeStruct((r, ns, lanes), U32),
        compiler_params=pltpu.CompilerParams(dimension_semantics=("arbitrary",), vmem_limit_bytes=VMEM_LIMIT),
        name="moe_ffn",
    )(tile_expert, xs, w_gate_up, b_gate_up, w_down, b_down)


def _combine_kernel(dest_ref, h_ref, gate_ref, gfin_ref, y_ref, o_ref, ybuf, sem, *, tm):
    i = pl.program_id(0)
    base = i * tm
    n_tok = pl.num_programs(0) * tm

    def issue(tt, carry):
        tok = base + tt
        for k in range(TOP_K):
            src = dest_ref[k * n_tok + tok]
            pltpu.make_async_copy(y_ref.at[pl.ds(src, 1)], ybuf.at[k, pl.ds(tt, 1)], sem).start()
        return carry

    lax.fori_loop(0, tm, issue, 0)
    for k in range(TOP_K):
        pltpu.make_async_copy(y_ref.at[pl.ds(0, tm)], ybuf.at[k], sem).wait()

    half = ybuf.shape[2] * ybuf.shape[3]
    lo = h_ref[:, :half]
    hi = h_ref[:, half:]
    for k in range(TOP_K):
        g = gate_ref[:, k:k + 1]
        w = _load_slabs(ybuf.at[k])
        lo = lo + g * _unpack_lo(w)
        hi = hi + g * _unpack_hi(w)
    ms = (jnp.sum(lo * lo, axis=1, keepdims=True) + jnp.sum(hi * hi, axis=1, keepdims=True)) / (2 * half)
    inv = lax.rsqrt(ms + RMS_EPS)
    o_ref[:, :half] = lo * inv * gfin_ref[:, :half]
    o_ref[:, half:] = hi * inv * gfin_ref[:, half:]


def _combine_call(dest, h, gate, g_final, y):
    t, d = h.shape
    tm = COMBINE_TM
    return pl.pallas_call(
        functools.partial(_combine_kernel, tm=tm),
        grid_spec=pltpu.PrefetchScalarGridSpec(
            num_scalar_prefetch=1,
            grid=(t // tm,),
            in_specs=[pl.BlockSpec((tm, d), lambda i, ds: (i, 0)),
                      pl.BlockSpec((tm, TOP_K), lambda i, ds: (i, 0)),
                      pl.BlockSpec((1, d), lambda i, ds: (0, 0)),
                      pl.BlockSpec(memory_space=pl.ANY)],
            out_specs=pl.BlockSpec((tm, d), lambda i, ds: (i, 0)),
            scratch_shapes=[pltpu.VMEM((TOP_K, tm) + y.shape[1:], U32), pltpu.SemaphoreType.DMA(())]),
        out_shape=jax.ShapeDtypeStruct((t, d), F32),
        compiler_params=pltpu.CompilerParams(dimension_semantics=("arbitrary",), vmem_limit_bytes=VMEM_LIMIT),
        name="moe_combine",
    )(dest, h, gate, g_final, y)


PAIR_LEAD = 2
TOP_K_BITS = 2
assert 1 << TOP_K_BITS == TOP_K


def _ffn_fused_kernel(te_ref, pair_ref, hn_hbm, wgu_ref, bgu_ref, wd_ref, bd_ref, yk_hbm,
                      xs_buf, y_buf, wgu_bf, wd_bf, gsem, ssem, *, tm, n_tok):
    i = pl.program_id(0)
    n_steps = pl.num_programs(0)
    n_used = te_ref[n_steps]
    half = xs_buf.shape[2]
    trash0 = TOP_K * n_tok

    def pair_of(tile, j):
        return pair_ref[(tile + PAIR_LEAD) * tm + j]

    def gather(tile, j, slot):
        tok = lax.shift_right_logical(jnp.maximum(pair_of(tile, j), 0), TOP_K_BITS)
        return pltpu.make_async_copy(hn_hbm.at[pl.ds(tok, 1)], xs_buf.at[slot, pl.ds(j, 1)], gsem.at[slot])

    def scatter(tile, j, slot):
        pair = pair_of(tile, j)
        real = (pair & (TOP_K - 1)) * n_tok + lax.shift_right_logical(pair, TOP_K_BITS)
        dst = jnp.where(pair >= 0, real, trash0 + slot * tm + j)
        return pltpu.make_async_copy(y_buf.at[slot, pl.ds(j, 1)], yk_hbm.at[pl.ds(dst, 1)], ssem.at[slot])

    def wait_gather(slot):
        pltpu.make_async_copy(hn_hbm.at[pl.ds(0, tm)], xs_buf.at[slot], gsem.at[slot]).wait()

    def wait_scatter(slot):
        pltpu.make_async_copy(y_buf.at[slot], yk_hbm.at[pl.ds(0, tm)], ssem.at[slot]).wait()

    def loop_rows(fn):
        lax.fori_loop(0, tm, lambda j, c: (fn(j), c)[1], 0)

    @pl.when(i == 0)
    def _():
        y_buf[...] = jnp.zeros(y_buf.shape, U32)
        loop_rows(lambda j: gather(0, j, 0).start())
        loop_rows(lambda j: scatter(-2, j, 0).start())

    e = te_ref[i]
    e_prev = te_ref[jnp.maximum(i - 1, 0)]

    @pl.when((i < n_used) & ((i == 0) | (e != e_prev)))
    def _():
        wgu_bf[...] = wgu_ref[0].astype(BF16)
        wd_bf[...] = wd_ref[0].astype(BF16)

    def compute_step(p):
        q = 1 - p
        wait_gather(p)
        w = xs_buf[p]
        x_lo = _unpack_lo(w).astype(BF16)
        x_hi = _unpack_hi(w).astype(BF16)
        for j in range(tm):
            gather(i + 1, j, q).start()
            scatter(i - 1, j, q).start()
        gu = (jnp.dot(x_lo, wgu_bf[0:half, :], preferred_element_type=F32)
              + jnp.dot(x_hi, wgu_bf[half:, :], preferred_element_type=F32) + bgu_ref[0])
        de = gu.shape[1] // 2
        x_glu = jnp.minimum(gu[:, :de], SWIGLU_LIMIT)
        x_lin = jnp.clip(gu[:, de:], -SWIGLU_LIMIT, SWIGLU_LIMIT)
        hdn = x_glu * jax.nn.sigmoid(SWIGLU_ALPHA * x_glu) * (x_lin + 1.0)
        y = jnp.dot(hdn.astype(BF16), wd_bf[...], preferred_element_type=F32) + bd_ref[0]
        bits = pltpu.bitcast(y.astype(BF16).astype(F32), U32)
        wait_scatter(p)
        y_buf[p] = (bits[:, :half] >> 16) | (bits[:, half:] & jnp.uint32(0xFFFF0000))

    for p in range(2):
        @pl.when((i < n_used) & (i % 2 == p))
        def _():
            compute_step(p)

    @pl.when(i == n_used)
    def _():
        s = i % 2
        wait_gather(s)
        loop_rows(lambda j: scatter(i - 1, j, 1 - s).start())
        wait_scatter(s)
        wait_scatter(1 - s)

    @pl.when((i == n_steps - 1) & (i < n_used))
    def _():
        s = i % 2
        wait_gather(1 - s)
        loop_rows(lambda j: scatter(i, j, s).start())
        wait_scatter(1 - s)
        wait_scatter(s)


def _ffn_fused_call(tile_expert, pair_table, hnp, w_gate_up, b_gate_up, w_down, b_down, n_tiles):
    t, half = hnp.shape
    tm = FFN_TM
    ne, d, de2 = w_gate_up.shape
    return pl.pallas_call(
        functools.partial(_ffn_fused_kernel, tm=tm, n_tok=t),
        grid_spec=pltpu.PrefetchScalarGridSpec(
            num_scalar_prefetch=2,
            grid=(n_tiles,),
            in_specs=[pl.BlockSpec(memory_space=pl.ANY),
                      pl.BlockSpec((1, d, de2), lambda i, te, pr: (te[i], 0, 0)),
                      pl.BlockSpec((1, 1, de2), lambda i, te, pr: (te[i], 0, 0)),
                      pl.BlockSpec((1, de2 // 2, d), lambda i, te, pr: (te[i], 0, 0)),
                      pl.BlockSpec((1, 1, d), lambda i, te, pr: (te[i], 0, 0))],
            out_specs=pl.BlockSpec(memory_space=pl.ANY),
            scratch_shapes=[pltpu.VMEM((2, tm, half), U32), pltpu.VMEM((2, tm, half), U32),
                            pltpu.VMEM((d, de2), BF16), pltpu.VMEM((de2 // 2, d), BF16),
                            pltpu.SemaphoreType.DMA((2,)), pltpu.SemaphoreType.DMA((2,))]),
        out_shape=jax.ShapeDtypeStruct((TOP_K * t + 2 * tm, half), U32),
        compiler_params=pltpu.CompilerParams(dimension_semantics=("arbitrary",), vmem_limit_bytes=VMEM_LIMIT,
                                             has_side_effects=True),
        name="moe_ffn",
    )(tile_expert, pair_table, hnp, w_gate_up, b_gate_up, w_down, b_down)


def _combine_dense_kernel(h_ref, gate_ref, gfin_ref, y0_ref, y1_ref, y2_ref, y3_ref, o_ref):
    half = y0_ref.shape[1] * y0_ref.shape[2]
    lo = h_ref[:, :half]
    hi = h_ref[:, half:]
    for k, y_ref in enumerate((y0_ref, y1_ref, y2_ref, y3_ref)):
        g = gate_ref[:, k:k + 1]
        w = _load_slabs(y_ref)
        lo = lo + g * _unpack_lo(w)
        hi = hi + g * _unpack_hi(w)
    ms = (jnp.sum(lo * lo, axis=1, keepdims=True) + jnp.sum(hi * hi, axis=1, keepdims=True)) / (2 * half)
    inv = lax.rsqrt(ms + RMS_EPS)
    o_ref[:, :half] = lo * inv * gfin_ref[:, :half]
    o_ref[:, half:] = hi * inv * gfin_ref[:, half:]


def _combine_dense_call(h, gate, g_final, yk):
    t, d = h.shape
    tm = COMBINE_TM
    per_k = t // tm
    yspec = lambda k: pl.BlockSpec((tm,) + yk.shape[1:], lambda i: (k * per_k + i, 0, 0))
    return pl.pallas_call(
        _combine_dense_kernel,
        grid=(t // tm,),
        in_specs=[pl.BlockSpec((tm, d), lambda i: (i, 0)),
                  pl.BlockSpec((tm, TOP_K), lambda i: (i, 0)),
                  pl.BlockSpec((1, d), lambda i: (0, 0)),
                  yspec(0), yspec(1), yspec(2), yspec(3)],
        out_specs=pl.BlockSpec((tm, d), lambda i: (i, 0)),
        out_shape=jax.ShapeDtypeStruct((t, d), F32),
        compiler_params=pltpu.CompilerParams(dimension_semantics=("arbitrary",), vmem_limit_bytes=VMEM_LIMIT),
        name="moe_combine",
    )(h, gate, g_final, yk, yk, yk, yk)


def _rope_freqs():
    def inv_freq(dh):
        return (ROPE_THETA ** (-jnp.arange(0, dh, 2, dtype=F32) / dh))[:, None]

    return inv_freq(MLA_ROPE), inv_freq(SWA_HEAD_DIM)


def _winprep_kernel(w_ref, o_ref):
    w = w_ref[...]
    rb = w.shape[0]
    c1 = MLA_Q_RANK + MLA_KV_RANK
    tail = w[:, c1 + MLA_ROPE:]
    lane = lax.broadcasted_iota(I32, (rb, LANES), 1)
    in_rope = (lane >= MLA_NOPE) & (lane < MLA_NOPE + MLA_ROPE)
    kr_p = jnp.where(in_rope, pltpu.roll(w[:, c1:c1 + LANES], MLA_NOPE, axis=1), 0.0)
    pieces = [w[:, :c1], kr_p, tail]
    off = 0
    for pc in pieces:
        o_ref[:, off:off + pc.shape[1]] = pc.astype(BF16)
        off += pc.shape[1]


def _winprep_call(w_in):
    d, n = w_in.shape
    rb = 128
    return pl.pallas_call(
        _winprep_kernel,
        grid=(d // rb,),
        in_specs=[pl.BlockSpec((rb, n), lambda i: (i, 0))],
        out_specs=pl.BlockSpec((rb, _D1), lambda i: (i, 0)),
        out_shape=jax.ShapeDtypeStruct((d, _D1), BF16),
        compiler_params=pltpu.CompilerParams(dimension_semantics=("arbitrary",), vmem_limit_bytes=VMEM_LIMIT),
        name="w_in_prep",
    )(w_in)


def _prep_weights(w_in, w_mla_uq, w_mla_ukv):
    w_in_al = _winprep_call(w_in)

    r = w_mla_uq.shape[0]
    wq = w_mla_uq.reshape(r, MLA_HEADS, MLA_NOPE + MLA_ROPE)
    zq = jnp.zeros((r, MLA_HEADS, LANES - MLA_NOPE - MLA_ROPE), w_mla_uq.dtype)
    wq_pad = jnp.concatenate([wq, zq], axis=-1).reshape(r, MLA_HEADS * LANES).astype(BF16)

    rk = w_mla_ukv.shape[0]
    wkv = w_mla_ukv.reshape(rk, MLA_HEADS, MLA_NOPE + MLA_V)
    wk_aug = jnp.concatenate([wkv[..., :MLA_NOPE], jnp.zeros((rk, MLA_HEADS, LANES - MLA_NOPE), w_mla_ukv.dtype)],
                             axis=-1).reshape(rk, MLA_HEADS * LANES).astype(BF16)
    wv_t = wkv[..., MLA_NOPE:].reshape(rk, MLA_HEADS * MLA_V).T.astype(BF16)
    return w_in_al, wq_pad, wk_aug, wv_t


def kernel(x, mem, positions, g_mix, w_in, g_mla_q, w_mla_uq, g_mla_kv, w_mla_ukv, w_mla_o, swa_sinks, w_swa_o,
           g_mem, w_mem_kv, w_xa_o, b_gate, w_out, g_ffn, w_router, b_router, w_gate_up, b_gate_up, w_down,
           b_down, g_final):
    b, s, d = x.shape
    t = b * s
    depth = g_mix.shape[0]
    h = x.reshape(t, d)
    pos = positions.astype(F32).reshape(1, t)
    fq, f64 = _rope_freqs()
    for l in range(depth):
        w_in_al, wq_pad, wk_aug, wv_t = _prep_weights(w_in[l], w_mla_uq[l], w_mla_ukv[l])
        (qm, km, vmt, qs, ksa, ksb, vsa, vsb, qx, gates) = _proj_call(
            h, pos, fq, f64, g_mix[l][None], w_in_al, g_mla_q[l][None], wq_pad,
            g_mla_kv[l][None], wk_aug, wv_t, b_gate[l][None])
        r3 = lambda a: a.reshape(b, s, a.shape[1])
        omla = _mla_call(r3(qm), r3(km), vmt, s).reshape(t, -1)
        oswa = _swa_call(swa_sinks[l], r3(qs), r3(ksa), r3(ksb), r3(vsa), r3(vsb)).reshape(t, -1)
        m = mem.shape[1]
        kvm = _memkv_call(mem.reshape(b * m, d), g_mem[l][None], w_mem_kv[l].astype(BF16)).reshape(b, m, -1)
        wr_t = w_router[l].T
        wr_hi = wr_t.astype(BF16)
        wr_split = jnp.concatenate([wr_hi, (wr_t - wr_hi.astype(F32)).astype(BF16)], axis=0)
        h_mid, hnp, idx, gate, rank, counts = _merge_call(
            h, omla, oswa, qx, gates, kvm, w_mla_o[l].astype(BF16), w_swa_o[l].astype(BF16),
            w_xa_o[l].astype(BF16), w_out[l].astype(BF16), g_ffn[l][None], wr_split, b_router[l][:, None], s)

        counts = counts[:, 0]
        padded = ((counts + FFN_TM - 1) // FFN_TM) * FFN_TM
        padded_end = jnp.cumsum(padded)
        offsets = padded_end - padded
        experts = jnp.arange(N_EXPERTS, dtype=I32)
        dest = (jnp.sum(jnp.where(idx[..., None] == experts, offsets, 0), axis=-1) + rank).reshape(-1).astype(I32)
        n_tiles = (t * TOP_K) // FFN_TM + N_EXPERTS
        n_used = (padded_end[-1] // FFN_TM).astype(I32)
        ztile = jnp.concatenate([jnp.where(counts % FFN_TM != 0, padded_end - FFN_TM, -1).astype(I32), n_used[None]])
        tile_start = jnp.minimum(jnp.arange(n_tiles, dtype=I32), jnp.maximum(n_used - 1, 0)) * FFN_TM
        tile_expert = jnp.sum((padded_end[None, :] <= tile_start[:, None]).astype(I32), axis=1)
        tile_expert = jnp.minimum(tile_expert, N_EXPERTS - 1)
        tile_valid = jnp.clip(counts[tile_expert] - (tile_start - offsets[tile_expert]), 0, FFN_TM)
        te = jnp.concatenate([tile_expert, n_used[None], tile_valid.astype(I32)])

        xs = _sc_scatter_rows(hnp, dest, n_tiles * FFN_TM)
        y = _ffn_call(te, xs, w_gate_up[l], b_gate_up[l][:, None, :], w_down[l], b_down[l][:, None, :])
        yk = _sc_gather_rows(y, dest)
        if l == depth - 1:
            gfin = g_final[None]
            out = _combine_dense_call(h_mid, gate.T, gfin, yk)
        else:
            raise NotImplementedError("depth > 1 needs a combine without the final norm")
        h = out
    return h.reshape(b, s, d)
```

```python
import functools
import math

import jax
import jax.numpy as jnp
from jax import lax
from jax.experimental import pallas as pl
from jax.experimental.pallas import tpu as pltpu
from jax.experimental.pallas import tpu_sc as plsc

F32 = jnp.float32
BF16 = jnp.bfloat16
U32 = jnp.uint32
I32 = jnp.int32

LANES = 128
ROPE_THETA = 10000.0
RMS_EPS = 1e-6
LOG2E = 1.4426950408889634

MLA_HEADS = 8
MLA_NOPE = 64
MLA_ROPE = 32
MLA_V = 64
MLA_Q_RANK = 256
MLA_KV_RANK = 128
SWA_HEADS = 8
SWA_KV_HEADS = 2
SWA_HEAD_DIM = 64
SWA_WINDOW = 128
XA_HEADS = 4
XA_HEAD_DIM = 128
N_EXPERTS = 32
TOP_K = 4
SWIGLU_ALPHA = 1.702
SWIGLU_LIMIT = 7.0
N_BRANCHES = 3

NEG_BIG = -1e30

PROJ_TM = 256
MLA_TQ = 512
MLA_ONES_ROWS = 16
SWA_TS = 512
MERGE_TM = 512
MERGE_SUB = 256
FFN_TM = 512
FFN_SUB = 256
DISPATCH_TM = 256
COMBINE_TM = 256

VMEM_LIMIT = 56 * 1024 * 1024


def _rms(x, g):
    return x * lax.rsqrt(jnp.mean(x * x, axis=-1, keepdims=True) + RMS_EPS) * g


def _store_slabs(ref, row0, value):
    rows, n, _ = ref.shape
    flat = ref.reshape(rows * n, LANES)
    for c in range(n):
        flat[pl.ds(row0 * n + c, value.shape[0], stride=n), :] = value[:, c * LANES:(c + 1) * LANES]


def _load_slabs(ref, row0=0, m=None):
    rows, n, _ = ref.shape
    m = rows if m is None else m
    flat = ref.reshape(rows * n, LANES)
    return jnp.concatenate([flat[pl.ds(row0 * n + c, m, stride=n), :] for c in range(n)], axis=1)


_A0, _A1 = 0, 512
_B0, _B1 = 512, 1280
_C0, _C1 = 1280, 1792
_D0, _D1 = 1792, 4864


def _rotate_half(x, d, lo, hi):
    n = x.shape[1]
    half = (hi - lo) // 2
    lane = lax.broadcasted_iota(I32, x.shape, 1) % d
    up = pltpu.roll(x, n - half, axis=1)
    dn = pltpu.roll(x, half, axis=1)
    return jnp.where((lane >= lo) & (lane < lo + half), -up, jnp.where((lane >= lo + half) & (lane < hi), dn, 0.0))


def _proj_kernel(x_ref, pos_ref, fq_ref, f64_ref, gmix_ref, win_ref, gq_ref, wq_ref,
                 gkv_ref, wk_ref, wv_ref, bgate_ref,
                 qm_ref, km_ref, vm_ref, qs_ref, ksa_ref, ksb_ref, vsa_ref, vsb_ref, qx_ref, gt_ref):
    x = x_ref[...]
    xn = _rms(x, gmix_ref[...]).astype(BF16)
    tm = x.shape[0]
    pos = pos_ref[...]
    ang16 = fq_ref[...] * pos
    ang32 = f64_ref[...] * pos
    c16, s16, c32, s32 = jnp.cos(ang16), jnp.sin(ang16), jnp.cos(ang32), jnp.sin(ang32)
    one = jnp.ones((MLA_NOPE, tm), F32)
    zero = jnp.zeros((MLA_NOPE, tm), F32)
    pad = LANES - MLA_NOPE - MLA_ROPE
    cosq = jnp.concatenate([one, c16, c16, one[:pad]], axis=0).T
    sinq = jnp.concatenate([zero, s16, s16, zero[:pad]], axis=0).T
    cos64 = jnp.concatenate([c32, c32, c32, c32], axis=0).T
    sin64 = jnp.concatenate([s32, s32, s32, s32], axis=0).T
    rope_lo, rope_hi = MLA_NOPE, MLA_NOPE + MLA_ROPE

    xa = jnp.dot(xn, win_ref[:, _A0:_A1], preferred_element_type=F32)
    cqn = _rms(xa[:, 0:256], gq_ref[...]).astype(BF16)
    qa = jnp.dot(cqn, wq_ref[...], preferred_element_type=F32)
    qb = _rotate_half(qa, LANES, rope_lo, rope_hi)
    q_scale = LOG2E / math.sqrt(MLA_NOPE + MLA_ROPE)
    ckvn = _rms(xa[:, 256:384], gkv_ref[...]).astype(BF16)
    ka = jnp.dot(ckvn, wk_ref[...], preferred_element_type=F32)
    kr = xa[:, 384:512]
    krope = kr * cosq + _rotate_half(kr, LANES, rope_lo, rope_hi) * sinq
    for h in range(MLA_HEADS):
        sl = slice(h * LANES, (h + 1) * LANES)
        qm_ref[:, sl] = ((qa[:, sl] * cosq + qb[:, sl] * sinq) * q_scale).astype(BF16)
        km_ref[:, sl] = (ka[:, sl] + krope).astype(BF16)
    vm_ref[...] = lax.dot_general(wv_ref[...], ckvn, (((1,), (1,)), ((), ())),
                                  preferred_element_type=F32).astype(BF16)

    xb = jnp.dot(xn, win_ref[:, _B0:_B1], preferred_element_type=F32)
    s_scale = LOG2E / math.sqrt(SWA_HEAD_DIM)
    nq = SWA_HEADS * SWA_HEAD_DIM
    qs = xb[:, :nq]
    qs_rot = _rotate_half(qs, SWA_HEAD_DIM, 0, SWA_HEAD_DIM)
    for p in range(SWA_HEADS // 2):
        sl = slice(p * LANES, (p + 1) * LANES)
        qs_ref[:, sl] = ((qs[:, sl] * cos64 + qs_rot[:, sl] * sin64) * s_scale).astype(BF16)
    ks = xb[:, nq:nq + LANES]
    ks = ks * cos64 + _rotate_half(ks, SWA_HEAD_DIM, 0, SWA_HEAD_DIM) * sin64
    ksa_ref[...] = ks.astype(BF16)
    ksb_ref[...] = pltpu.roll(ks, 64, axis=1).astype(BF16)
    vs = xb[:, nq + LANES:nq + 2 * LANES]
    vsa_ref[...] = vs.astype(BF16)
    vsb_ref[...] = pltpu.roll(vs, 64, axis=1).astype(BF16)

    xc = jnp.dot(xn, win_ref[:, _C0:_C1], preferred_element_type=F32)
    qx_ref[...] = (xc * (LOG2E / math.sqrt(XA_HEAD_DIM))).astype(BF16)

    xd = jnp.dot(xn, win_ref[:, _D0:_D1], preferred_element_type=F32) + bgate_ref[...]
    gt_ref[...] = jax.nn.sigmoid(xd).astype(BF16)


def _proj_call(x2, pos, fq, f64, g_mix, w_in_al, g_q, wq, g_kv, wk_aug, wv, b_gate):
    t, d = x2.shape
    tm = PROJ_TM
    row = lambda n: pl.BlockSpec((tm, n), lambda i: (i, 0))
    full = lambda a: pl.BlockSpec(a.shape, lambda i: (0,) * a.ndim)
    out_cols = [1024, 1024, 512, 128, 128, 128, 128, 512, 3072]
    out_specs = [row(n) for n in out_cols]
    out_shape = [jax.ShapeDtypeStruct((t, n), BF16) for n in out_cols]
    vt_rows = wv.shape[0]
    out_specs.insert(2, pl.BlockSpec((vt_rows, tm), lambda i: (0, i)))
    out_shape.insert(2, jax.ShapeDtypeStruct((vt_rows, t), BF16))
    return pl.pallas_call(
        _proj_kernel,
        grid=(t // tm,),
        in_specs=[row(d), pl.BlockSpec((1, tm), lambda i: (0, i)), full(fq), full(f64), full(g_mix),
                  pl.BlockSpec(w_in_al.shape, lambda i: (0, 0), pipeline_mode=pl.Buffered(1)),
                  full(g_q), full(wq), full(g_kv), full(wk_aug), full(wv), full(b_gate)],
        out_specs=out_specs,
        out_shape=out_shape,
        compiler_params=pltpu.CompilerParams(dimension_semantics=("arbitrary",), vmem_limit_bytes=VMEM_LIMIT),
        name="proj",
    )(x2, pos, fq, f64, g_mix, w_in_al, g_q, wq, g_kv, wk_aug, wv, b_gate)


def _mla_kernel(q_ref, k_ref, vt_ref, o_ref, sa_ref, sb_ref, m_ref, acc_ref, *, tq):
    i = pl.program_id(2)
    m_ref[...] = jnp.full(m_ref.shape, NEG_BIG, F32)
    acc_ref[...] = jnp.zeros(acc_ref.shape, F32)
    ones = jnp.ones((MLA_ONES_ROWS, tq), BF16)

    def scores(j, s_ref):
        k0 = pl.multiple_of(j * tq, tq)
        for hh in range(2):
            sl = slice(hh * LANES, (hh + 1) * LANES)
            s_ref[hh] = lax.dot_general(k_ref[0, pl.ds(k0, tq), sl], q_ref[0, :, sl], (((1,), (1,)), ((), ())),
                                        preferred_element_type=F32)

    def update(j, s_ref, masked):
        k0 = pl.multiple_of(j * tq, tq)
        for hh in range(2):
            vt = jnp.concatenate([vt_ref[hh * MLA_V:(hh + 1) * MLA_V, pl.ds(k0, tq)], ones], axis=0)
            st = s_ref[hh]
            if masked:
                kj = lax.broadcasted_iota(I32, (tq, tq), 0)
                qi = lax.broadcasted_iota(I32, (tq, tq), 1)
                st = jnp.where(kj <= qi, st, NEG_BIG)
            m_old = m_ref[hh]
            m_new = jnp.maximum(m_old, jnp.max(st, axis=0, keepdims=True))
            alpha = jnp.exp2(m_old - m_new)
            pt = jnp.exp2(st - m_new)
            acc_ref[hh] = alpha * acc_ref[hh] + jnp.dot(vt, pt.astype(BF16), preferred_element_type=F32)
            m_ref[hh] = m_new

    scores(0, sa_ref)

    def body(jj, carry):
        scores(2 * jj + 1, sb_ref)
        update(2 * jj, sa_ref, False)
        scores(2 * jj + 2, sa_ref)
        update(2 * jj + 1, sb_ref, False)
        return carry

    lax.fori_loop(0, i // 2, body, 0)

    @pl.when(i % 2 == 0)
    def _():
        update(i, sa_ref, True)

    @pl.when(i % 2 == 1)
    def _():
        scores(i, sb_ref)
        update(i - 1, sa_ref, False)
        update(i, sb_ref, True)

    ot = jnp.concatenate([acc_ref[hh, :MLA_V] / acc_ref[hh, MLA_V:MLA_V + 1] for hh in range(2)],
                         axis=0)
    o_ref[0] = ot.T.astype(BF16)


def _mla_call(q, k, vt, seq):
    b, s, _ = q.shape
    assert s == seq
    tq = min(MLA_TQ, s)
    n_pairs = MLA_HEADS // 2
    return pl.pallas_call(
        functools.partial(_mla_kernel, tq=tq),
        grid=(b, n_pairs, s // tq),
        in_specs=[pl.BlockSpec((1, tq, 2 * LANES), lambda bi, hp, i: (bi, i, hp)),
                  pl.BlockSpec((1, s, 2 * LANES), lambda bi, hp, i: (bi, 0, hp)),
                  pl.BlockSpec((2 * MLA_V, s), lambda bi, hp, i: (hp, bi))],
        out_specs=pl.BlockSpec((1, tq, LANES), lambda bi, hp, i: (bi, i, hp)),
        out_shape=jax.ShapeDtypeStruct((b, s, n_pairs * LANES), BF16),
        scratch_shapes=[pltpu.VMEM((2, tq, tq), F32), pltpu.VMEM((2, tq, tq), F32),
                        pltpu.VMEM((2, 1, tq), F32),
                        pltpu.VMEM((2, MLA_V + MLA_ONES_ROWS, tq), F32)],
        compiler_params=pltpu.CompilerParams(dimension_semantics=("arbitrary",) * 3, vmem_limit_bytes=VMEM_LIMIT),
        name="mla_attn",
    )(q, k, vt)


def _swa_kernel(sink_ref, q_ref, ka_ref, kb_ref, va_ref, vb_ref, kah_ref, kbh_ref, vah_ref, vbh_ref, o_ref, *, ts):
    w = SWA_WINDOW
    i = pl.program_id(1)
    ka = jnp.concatenate([kah_ref[0], ka_ref[0]], axis=0)
    kb = jnp.concatenate([kbh_ref[0], kb_ref[0]], axis=0)
    va = jnp.concatenate([vah_ref[0], va_ref[0]], axis=0)
    vb = jnp.concatenate([vbh_ref[0], vb_ref[0]], axis=0)
    lane_k = lax.broadcasted_iota(I32, (2 * w, LANES), 1)
    low = lane_k < SWA_HEAD_DIM
    qi = lax.broadcasted_iota(I32, (2 * w, 2 * w), 0) % w
    kj = lax.broadcasted_iota(I32, (2 * w, 2 * w), 1)
    diff = qi + w - kj
    band = (diff >= 0) & (diff < w)
    lane_o = lax.broadcasted_iota(I32, (w, LANES), 1)
    row2 = lax.broadcasted_iota(I32, (2 * w, 1), 0)
    zero = jnp.zeros((), BF16)
    stacks = ((0, ka, True, va), (1, kb, False, vb), (4, kb, True, vb), (5, ka, False, va))
    for n in range(ts // w):
        mask = band & ((i * (ts // w) + n > 0) | (kj >= w))
        res = []
        for h0, ksrc, keep_low, vsrc in stacks:
            p0 = h0 // 2
            q = jnp.concatenate([q_ref[0, n * w:(n + 1) * w, p0 * LANES:(p0 + 1) * LANES],
                                 q_ref[0, n * w:(n + 1) * w, (p0 + 1) * LANES:(p0 + 2) * LANES]], axis=0)
            kwin = ksrc[n * w:n * w + 2 * w]
            kwin = jnp.where(low if keep_low else ~low, kwin, zero)
            vwin = vsrc[n * w:n * w + 2 * w]
            s = lax.dot_general(q, kwin, (((1,), (1,)), ((), ())), preferred_element_type=F32)
            s = jnp.where(mask, s, NEG_BIG)
            sink = jnp.where(row2 < w, sink_ref[h0], sink_ref[h0 + 2]) * LOG2E
            m = jnp.maximum(jnp.max(s, axis=1, keepdims=True), sink)
            p = jnp.exp2(s - m)
            den = jnp.sum(p, axis=1, keepdims=True) + jnp.exp2(sink - m)
            o = jnp.dot(p.astype(BF16), vwin, preferred_element_type=F32) / den
            res.append(o)
        o02, o13, o46, o57 = res
        sel = lane_o < SWA_HEAD_DIM
        rows = slice(n * w, (n + 1) * w)
        o_ref[0, rows, 0 * LANES:1 * LANES] = jnp.where(sel, o02[:w], o13[:w]).astype(BF16)
        o_ref[0, rows, 1 * LANES:2 * LANES] = jnp.where(sel, o02[w:], o13[w:]).astype(BF16)
        o_ref[0, rows, 2 * LANES:3 * LANES] = jnp.where(sel, o46[:w], o57[:w]).astype(BF16)
        o_ref[0, rows, 3 * LANES:4 * LANES] = jnp.where(sel, o46[w:], o57[w:]).astype(BF16)


def _swa_call(sinks, q, ksa, ksb, vsa, vsb):
    b, s, _ = q.shape
    ts = min(SWA_TS, s)
    w = SWA_WINDOW
    r = ts // w
    main = pl.BlockSpec((1, ts, LANES), lambda bi, i: (bi, i, 0))
    halo = pl.BlockSpec((1, w, LANES), lambda bi, i: (bi, jnp.maximum(i * r - 1, 0), 0))
    return pl.pallas_call(
        functools.partial(_swa_kernel, ts=ts),
        grid=(b, s // ts),
        in_specs=[pl.BlockSpec(memory_space=pltpu.SMEM),
                  pl.BlockSpec((1, ts, 4 * LANES), lambda bi, i: (bi, i, 0)),
                  main, main, main, main, halo, halo, halo, halo],
        out_specs=pl.BlockSpec((1, ts, 4 * LANES), lambda bi, i: (bi, i, 0)),
        out_shape=jax.ShapeDtypeStruct((b, s, 4 * LANES), BF16),
        compiler_params=pltpu.CompilerParams(dimension_semantics=("arbitrary",) * 2, vmem_limit_bytes=VMEM_LIMIT),
        name="swa_attn",
    )(sinks, q, ksa, ksb, vsa, vsb, ksa, ksb, vsa, vsb)


def _memkv_kernel(mem_ref, g_ref, w_ref, o_ref):
    mn = _rms(mem_ref[...], g_ref[...]).astype(BF16)
    o_ref[...] = jnp.dot(mn, w_ref[...], preferred_element_type=F32).astype(BF16)


def _memkv_call(mem2, g_mem, w_mem_kv):
    n, d = mem2.shape
    tm = min(256, n)
    return pl.pallas_call(
        _memkv_kernel,
        grid=(n // tm,),
        in_specs=[pl.BlockSpec((tm, d), lambda i: (i, 0)),
                  pl.BlockSpec(g_mem.shape, lambda i: (0, 0)),
                  pl.BlockSpec(w_mem_kv.shape, lambda i: (0, 0))],
        out_specs=pl.BlockSpec((tm, w_mem_kv.shape[1]), lambda i: (i, 0)),
        out_shape=jax.ShapeDtypeStruct((n, w_mem_kv.shape[1]), BF16),
        compiler_params=pltpu.CompilerParams(dimension_semantics=("arbitrary",), vmem_limit_bytes=VMEM_LIMIT),
        name="mem_kv",
    )(mem2, g_mem, w_mem_kv)


def _merge_kernel(x_ref, omla_ref, oswa_ref, qx_ref, gt_ref, kvm_ref, wmo_ref, wso_ref, wxo_ref, wout_ref,
                  gffn_ref, wr_ref, br_ref,
                  h_ref, hnp_ref, idx_ref, gate_ref, rank_ref, cnt_ref, run_ref, *, tm, sub):
    @pl.when(pl.program_id(0) == 0)
    def _():
        run_ref[...] = jnp.zeros(run_ref.shape, F32)

    d = x_ref.shape[1]
    kv_cols = XA_HEADS * XA_HEAD_DIM
    erow = lax.broadcasted_iota(I32, (N_EXPERTS, sub), 0)
    tri_t = (lax.broadcasted_iota(I32, (sub, sub), 0) < lax.broadcasted_iota(I32, (sub, sub), 1)).astype(BF16)
    nt = (((1,), (1,)), ((), ()))
    run = run_ref[...]
    for hf in range(tm // sub):
        rows = slice(hf * sub, (hf + 1) * sub)

        oxs = []
        for hd in range(XA_HEADS):
            sl = slice(hd * LANES, (hd + 1) * LANES)
            km = kvm_ref[0, :, sl]
            vm = kvm_ref[0, :, kv_cols + hd * LANES:kv_cols + (hd + 1) * LANES]
            s = lax.dot_general(qx_ref[rows, sl], km, nt, preferred_element_type=F32)
            p = jnp.exp2(s - jnp.max(s, axis=1, keepdims=True))
            den = jnp.sum(p, axis=1, keepdims=True)
            oxs.append((jnp.dot(p.astype(BF16), vm, preferred_element_type=F32) / den).astype(BF16))
        oxa = jnp.concatenate(oxs, axis=1)

        merged = (gt_ref[rows, 0:d].astype(F32) * jnp.dot(omla_ref[rows, :], wmo_ref[...], preferred_element_type=F32)
                  + gt_ref[rows, d:2 * d].astype(F32) * jnp.dot(oswa_ref[rows, :], wso_ref[...],
                                                                 preferred_element_type=F32)
                  + gt_ref[rows, 2 * d:3 * d].astype(F32) * jnp.dot(oxa, wxo_ref[...], preferred_element_type=F32))
        h = x_ref[rows, :] + jnp.dot(merged.astype(BF16), wout_ref[...], preferred_element_type=F32)
        h_ref[rows, :] = h

        hn = _rms(h, gffn_ref[...])
        hn_hi = hn.astype(BF16)
        hn_hi32 = hn_hi.astype(F32)
        hn_lo = (hn - hn_hi32).astype(BF16)
        bits = pltpu.bitcast(hn_hi32, U32)
        _store_slabs(hnp_ref, hf * sub, (bits[:, : d // 2] >> 16) | (bits[:, d // 2:] & jnp.uint32(0xFFFF0000)))

        part = lax.dot_general(wr_ref[...], hn_hi, nt, preferred_element_type=F32)
        logits_t = (part[:N_EXPERTS] + part[N_EXPERTS:]
                    + lax.dot_general(wr_ref[0:N_EXPERTS, :], hn_lo, nt, preferred_element_type=F32) + br_ref[...])

        work = logits_t
        vals, idxs, hots = [], [], []
        for _ in range(TOP_K):
            mx = jnp.max(work, axis=0, keepdims=True)
            ix = jnp.min(jnp.where(work == mx, erow, N_EXPERTS), axis=0, keepdims=True)
            hot = erow == ix
            work = jnp.where(hot, -jnp.inf, work)
            vals.append(mx)
            idxs.append(ix)
            hots.append(hot)
        es = [jnp.exp(v - vals[0]) for v in vals]
        den = es[0] + es[1] + es[2] + es[3]
        sel_t = (hots[0] | hots[1] | hots[2] | hots[3])
        prefix_t = jnp.dot(sel_t.astype(BF16), tri_t, preferred_element_type=F32) + run
        for k in range(TOP_K):
            idx_ref[k:k + 1, rows] = idxs[k]
            gate_ref[k:k + 1, rows] = es[k] / den
            rank_ref[k:k + 1, rows] = jnp.sum(jnp.where(hots[k], prefix_t, 0.0), axis=0, keepdims=True).astype(I32)
        run = run + jnp.sum(sel_t.astype(F32), axis=1, keepdims=True)
    run_ref[...] = run
    cnt_ref[...] = run.astype(I32)


def _merge_call(x2, omla, oswa, qx, gates, kvm, wmo, wso, wxo, wout, g_ffn, wr_split, b_router_col, seq):
    t, d = x2.shape
    tm = MERGE_TM
    per_b = seq // tm
    row = lambda n: pl.BlockSpec((tm, n), lambda i: (i, 0))
    col = lambda: pl.BlockSpec((TOP_K, tm), lambda i: (0, i))
    full = lambda a: pl.BlockSpec(a.shape, lambda i: (0,) * a.ndim)
    return pl.pallas_call(
        functools.partial(_merge_kernel, tm=tm, sub=MERGE_SUB),
        grid=(t // tm,),
        in_specs=[row(d), row(512), row(512), row(512), row(3 * d),
                  pl.BlockSpec((1,) + kvm.shape[1:], lambda i: (i // per_b, 0, 0)),
                  full(wmo), full(wso), full(wxo), full(wout), full(g_ffn), full(wr_split), full(b_router_col)],
        out_specs=[row(d), pl.BlockSpec((tm, d // 2 // LANES, LANES), lambda i: (i, 0, 0)), col(), col(), col(),
                   pl.BlockSpec((N_EXPERTS, 1), lambda i: (0, 0))],
        out_shape=[jax.ShapeDtypeStruct((t, d), F32), jax.ShapeDtypeStruct((t, d // 2 // LANES, LANES), U32),
                   jax.ShapeDtypeStruct((TOP_K, t), I32), jax.ShapeDtypeStruct((TOP_K, t), F32),
                   jax.ShapeDtypeStruct((TOP_K, t), I32), jax.ShapeDtypeStruct((N_EXPERTS, 1), I32)],
        scratch_shapes=[pltpu.VMEM((N_EXPERTS, 1), F32)],
        compiler_params=pltpu.CompilerParams(dimension_semantics=("arbitrary",), vmem_limit_bytes=VMEM_LIMIT),
        name="merge_router",
    )(x2, omla, oswa, qx, gates, kvm, wmo, wso, wxo, wout, g_ffn, wr_split, b_router_col)


def _dispatch_kernel(dest_ref, ztile_ref, hn_ref, xs_ref, zbuf, sem, zsem, *, tm, n_tok):
    base = pl.program_id(0) * tm

    @pl.when(pl.program_id(0) == 0)
    def _():
        zbuf[...] = jnp.zeros(zbuf.shape, U32)

        def zcopy(e):
            start = pl.multiple_of(ztile_ref[e], FFN_TM)
            return pltpu.make_async_copy(zbuf, xs_ref.at[pl.ds(start, zbuf.shape[0])], zsem)

        for e in range(N_EXPERTS):
            @pl.when(ztile_ref[e] >= 0)
            def _():
                zcopy(e).start()
        for e in range(N_EXPERTS):
            @pl.when(ztile_ref[e] >= 0)
            def _():
                zcopy(e).wait()

        def unused(i):
            start = pl.multiple_of(i * FFN_TM, FFN_TM)
            return pltpu.make_async_copy(zbuf, xs_ref.at[pl.ds(start, zbuf.shape[0])], zsem)

        n_tiles = xs_ref.shape[0] // FFN_TM
        lax.fori_loop(ztile_ref[N_EXPERTS], n_tiles, lambda i, c: (unused(i).start(), c)[1], 0)
        lax.fori_loop(ztile_ref[N_EXPERTS], n_tiles, lambda i, c: (unused(i).wait(), c)[1], 0)

    def issue(tt, carry):
        for k in range(TOP_K):
            dst = dest_ref[k * n_tok + base + tt]
            pltpu.make_async_copy(hn_ref.at[pl.ds(tt, 1)], xs_ref.at[pl.ds(dst, 1)], sem).start()
        return carry

    lax.fori_loop(0, tm, issue, 0)
    for _ in range(TOP_K):
        pltpu.make_async_copy(hn_ref, xs_ref.at[pl.ds(0, tm)], sem).wait()


def _dispatch_call(dest, ztile, hnp, n_rows):
    t, ns, c = hnp.shape
    tm = DISPATCH_TM
    return pl.pallas_call(
        functools.partial(_dispatch_kernel, tm=tm, n_tok=t),
        grid_spec=pltpu.PrefetchScalarGridSpec(
            num_scalar_prefetch=2,
            grid=(t // tm,),
            in_specs=[pl.BlockSpec((tm, ns, c), lambda i, ds, zt: (i, 0, 0))],
            out_specs=pl.BlockSpec(memory_space=pl.ANY),
            scratch_shapes=[pltpu.VMEM((FFN_TM, ns, c), U32), pltpu.SemaphoreType.DMA(()),
                            pltpu.SemaphoreType.DMA(())]),
        out_shape=jax.ShapeDtypeStruct((n_rows, ns, c), U32),
        compiler_params=pltpu.CompilerParams(dimension_semantics=("arbitrary",), has_side_effects=True),
        name="moe_dispatch",
    )(dest, ztile, hnp)


SC_CORES = 2
SC_SUBCORES = 16
SC_WORKERS = SC_CORES * SC_SUBCORES
SC_CHUNK = 64


def _sc_mesh():
    return plsc.VectorSubcoreMesh(core_axis_name="c", subcore_axis_name="s",
                                  num_cores=SC_CORES, num_subcores=SC_SUBCORES)


def _sc_index_blocks(idx):
    n = idx.shape[0]
    per_w = n // SC_WORKERS
    n_ch = per_w // SC_CHUNK
    assert per_w * SC_WORKERS == n and n_ch * SC_CHUNK == per_w
    return idx.reshape(SC_WORKERS, n_ch, SC_CHUNK), per_w, n_ch


def _sc_scatter_rows(src, idx, n_out):
    idx3, per_w, n_ch = _sc_index_blocks(idx)
    n_src = src.shape[0]
    assert n_src % per_w == 0

    @functools.partial(
        pl.kernel, mesh=_sc_mesh(),
        out_type=jax.ShapeDtypeStruct((n_out,) + src.shape[1:], src.dtype),
        scratch_types=[pltpu.VMEM((n_ch, SC_CHUNK), I32), pltpu.VMEM((SC_CHUNK,) + src.shape[1:], src.dtype),
                       pltpu.SemaphoreType.DMA],
        name="moe_dispatch_sc")
    def k(src_hbm, idx_hbm, out_hbm, idx_v, rows_v, sem):
        wid = lax.axis_index("s") * SC_CORES + lax.axis_index("c")
        base = lax.rem(wid * per_w, n_src)
        pltpu.sync_copy(idx_hbm.at[wid], idx_v)

        @pl.loop(0, n_ch)
        def _(j):
            pltpu.sync_copy(src_hbm.at[pl.ds(base + j * SC_CHUNK, SC_CHUNK)], rows_v)
            pltpu.async_copy(rows_v, out_hbm.at[idx_v.at[j]], sem).wait()

    return k(src, idx3)


def _sc_gather_rows(table, idx):
    idx3, per_w, n_ch = _sc_index_blocks(idx)

    @functools.partial(
        pl.kernel, mesh=_sc_mesh(),
        out_type=jax.ShapeDtypeStruct((idx.shape[0],) + table.shape[1:], table.dtype),
        scratch_types=[pltpu.VMEM((n_ch, SC_CHUNK), I32), pltpu.VMEM((SC_CHUNK,) + table.shape[1:], table.dtype),
                       pltpu.SemaphoreType.DMA],
        name="moe_gather_sc")
    def k(table_hbm, idx_hbm, out_hbm, idx_v, rows_v, sem):
        wid = lax.axis_index("s") * SC_CORES + lax.axis_index("c")
        base = wid * per_w
        pltpu.sync_copy(idx_hbm.at[wid], idx_v)

        @pl.loop(0, n_ch)
        def _(j):
            pltpu.async_copy(table_hbm.at[idx_v.at[j]], rows_v, sem).wait()
            pltpu.sync_copy(rows_v, out_hbm.at[pl.ds(base + j * SC_CHUNK, SC_CHUNK)])

    return k(table, idx3)


def _unpack_lo(w):
    return pltpu.bitcast(w << 16, F32)


def _unpack_hi(w):
    return pltpu.bitcast(w & jnp.uint32(0xFFFF0000), F32)


def _ffn_kernel(te_ref, xs_ref, wgu_ref, bgu_ref, wd_ref, bd_ref, y_ref, wgu_bf, wd_bf):
    i = pl.program_id(0)
    n_used = te_ref[pl.num_programs(0)]
    e = te_ref[i]
    e_prev = te_ref[jnp.maximum(i - 1, 0)]

    @pl.when((i < n_used) & ((i == 0) | (e != e_prev)))
    def _():
        wgu_bf[...] = wgu_ref[0].astype(BF16)
        wd_bf[...] = wd_ref[0].astype(BF16)

    @pl.when(i < n_used)
    def _():
        valid = te_ref[pl.num_programs(0) + 1 + i]
        rows, n_slab, _ = xs_ref.shape
        half = n_slab * LANES
        for sb in range(rows // FFN_SUB):
            r0 = sb * FFN_SUB
            w = _load_slabs(xs_ref, r0, FFN_SUB)
            w = jnp.where(lax.broadcasted_iota(I32, w.shape, 0) + r0 < valid, w, jnp.uint32(0))
            x_lo = _unpack_lo(w).astype(BF16)
            x_hi = _unpack_hi(w).astype(BF16)
            gu = (jnp.dot(x_lo, wgu_bf[0:half, :], preferred_element_type=F32)
                  + jnp.dot(x_hi, wgu_bf[half:, :], preferred_element_type=F32) + bgu_ref[0])
            de = gu.shape[1] // 2
            x_glu = jnp.minimum(gu[:, :de], SWIGLU_LIMIT)
            x_lin = jnp.clip(gu[:, de:], -SWIGLU_LIMIT, SWIGLU_LIMIT)
            hdn = x_glu * jax.nn.sigmoid(SWIGLU_ALPHA * x_glu) * (x_lin + 1.0)
            y = jnp.dot(hdn.astype(BF16), wd_bf[...], preferred_element_type=F32) + bd_ref[0]
            bits = pltpu.bitcast(y.astype(BF16).astype(F32), U32)
            _store_slabs(y_ref, r0, (bits[:, :half] >> 16) | (bits[:, half:] & jnp.uint32(0xFFFF0000)))

    @pl.when(i >= n_used)
    def _():
        y_ref[...] = jnp.zeros(y_ref.shape, U32)


def _ffn_call(tile_expert, xs, w_gate_up, b_gate_up, w_down, b_down):
    r, ns, lanes = xs.shape
    tm = FFN_TM
    ne, d, de2 = w_gate_up.shape
    return pl.pallas_call(
        _ffn_kernel,
        grid_spec=pltpu.PrefetchScalarGridSpec(
            num_scalar_prefetch=1,
            grid=(r // tm,),
            in_specs=[pl.BlockSpec((tm, ns, lanes),
                                   lambda i, te: (jnp.minimum(i, jnp.maximum(te[r // tm] - 1, 0)), 0, 0)),
                      pl.BlockSpec((1, d, de2), lambda i, te: (te[i], 0, 0)),
                      pl.BlockSpec((1, 1, de2), lambda i, te: (te[i], 0, 0)),
                      pl.BlockSpec((1, de2 // 2, d), lambda i, te: (te[i], 0, 0)),
                      pl.BlockSpec((1, 1, d), lambda i, te: (te[i], 0, 0))],
            out_specs=pl.BlockSpec((tm, ns, lanes), lambda i, te: (i, 0, 0)),
            scratch_shapes=[pltpu.VMEM((d, de2), BF16), pltpu.VMEM((de2 // 2, d), BF16)]),
        out_shape=jax.ShapeDtypeStruct((r, ns, lanes), U32),
        compiler_params=pltpu.CompilerParams(dimension_semantics=("arbitrary",), vmem_limit_bytes=VMEM_LIMIT),
        name="moe_ffn",
    )(tile_expert, xs, w_gate_up, b_gate_up, w_down, b_down)


def _combine_kernel(dest_ref, h_ref, gate_ref, gfin_ref, y_ref, o_ref, ybuf, sem, *, tm):
    i = pl.program_id(0)
    base = i * tm
    n_tok = pl.num_programs(0) * tm

    def issue(tt, carry):
        tok = base + tt
        for k in range(TOP_K):
            src = dest_ref[k * n_tok + tok]
            pltpu.make_async_copy(y_ref.at[pl.ds(src, 1)], ybuf.at[k, pl.ds(tt, 1)], sem).start()
        return carry

    lax.fori_loop(0, tm, issue, 0)
    for k in range(TOP_K):
        pltpu.make_async_copy(y_ref.at[pl.ds(0, tm)], ybuf.at[k], sem).wait()

    half = ybuf.shape[2] * ybuf.shape[3]
    lo = h_ref[:, :half]
    hi = h_ref[:, half:]
    for k in range(TOP_K):
        g = gate_ref[:, k:k + 1]
        w = _load_slabs(ybuf.at[k])
        lo = lo + g * _unpack_lo(w)
        hi = hi + g * _unpack_hi(w)
    ms = (jnp.sum(lo * lo, axis=1, keepdims=True) + jnp.sum(hi * hi, axis=1, keepdims=True)) / (2 * half)
    inv = lax.rsqrt(ms + RMS_EPS)
    o_ref[:, :half] = lo * inv * gfin_ref[:, :half]
    o_ref[:, half:] = hi * inv * gfin_ref[:, half:]


def _combine_call(dest, h, gate, g_final, y):
    t, d = h.shape
    tm = COMBINE_TM
    return pl.pallas_call(
        functools.partial(_combine_kernel, tm=tm),
        grid_spec=pltpu.PrefetchScalarGridSpec(
            num_scalar_prefetch=1,
            grid=(t // tm,),
            in_specs=[pl.BlockSpec((tm, d), lambda i, ds: (i, 0)),
                      pl.BlockSpec((tm, TOP_K), lambda i, ds: (i, 0)),
                      pl.BlockSpec((1, d), lambda i, ds: (0, 0)),
                      pl.BlockSpec(memory_space=pl.ANY)],
            out_specs=pl.BlockSpec((tm, d), lambda i, ds: (i, 0)),
            scratch_shapes=[pltpu.VMEM((TOP_K, tm) + y.shape[1:], U32), pltpu.SemaphoreType.DMA(())]),
        out_shape=jax.ShapeDtypeStruct((t, d), F32),
        compiler_params=pltpu.CompilerParams(dimension_semantics=("arbitrary",), vmem_limit_bytes=VMEM_LIMIT),
        name="moe_combine",
    )(dest, h, gate, g_final, y)


PAIR_LEAD = 2
TOP_K_BITS = 2
assert 1 << TOP_K_BITS == TOP_K


def _ffn_fused_kernel(te_ref, pair_ref, hn_hbm, wgu_ref, bgu_ref, wd_ref, bd_ref, yk_hbm,
                      xs_buf, y_buf, wgu_bf, wd_bf, gsem, ssem, *, tm, n_tok):
    i = pl.program_id(0)
    n_steps = pl.num_programs(0)
    n_used = te_ref[n_steps]
    half = xs_buf.shape[2]
    trash0 = TOP_K * n_tok

    def pair_of(tile, j):
        return pair_ref[(tile + PAIR_LEAD) * tm + j]

    def gather(tile, j, slot):
        tok = lax.shift_right_logical(jnp.maximum(pair_of(tile, j), 0), TOP_K_BITS)
        return pltpu.make_async_copy(hn_hbm.at[pl.ds(tok, 1)], xs_buf.at[slot, pl.ds(j, 1)], gsem.at[slot])

    def scatter(tile, j, slot):
        pair = pair_of(tile, j)
        real = (pair & (TOP_K - 1)) * n_tok + lax.shift_right_logical(pair, TOP_K_BITS)
        dst = jnp.where(pair >= 0, real, trash0 + slot * tm + j)
        return pltpu.make_async_copy(y_buf.at[slot, pl.ds(j, 1)], yk_hbm.at[pl.ds(dst, 1)], ssem.at[slot])

    def wait_gather(slot):
        pltpu.make_async_copy(hn_hbm.at[pl.ds(0, tm)], xs_buf.at[slot], gsem.at[slot]).wait()

    def wait_scatter(slot):
        pltpu.make_async_copy(y_buf.at[slot], yk_hbm.at[pl.ds(0, tm)], ssem.at[slot]).wait()

    def loop_rows(fn):
        lax.fori_loop(0, tm, lambda j, c: (fn(j), c)[1], 0)

    @pl.when(i == 0)
    def _():
        y_buf[...] = jnp.zeros(y_buf.shape, U32)
        loop_rows(lambda j: gather(0, j, 0).start())
        loop_rows(lambda j: scatter(-2, j, 0).start())

    e = te_ref[i]
    e_prev = te_ref[jnp.maximum(i - 1, 0)]

    @pl.when((i < n_used) & ((i == 0) | (e != e_prev)))
    def _():
        wgu_bf[...] = wgu_ref[0].astype(BF16)
        wd_bf[...] = wd_ref[0].astype(BF16)

    def compute_step(p):
        q = 1 - p
        wait_gather(p)
        w = xs_buf[p]
        x_lo = _unpack_lo(w).astype(BF16)
        x_hi = _unpack_hi(w).astype(BF16)
        for j in range(tm):
            gather(i + 1, j, q).start()
            scatter(i - 1, j, q).start()
        gu = (jnp.dot(x_lo, wgu_bf[0:half, :], preferred_element_type=F32)
              + jnp.dot(x_hi, wgu_bf[half:, :], preferred_element_type=F32) + bgu_ref[0])
        de = gu.shape[1] // 2
        x_glu = jnp.minimum(gu[:, :de], SWIGLU_LIMIT)
        x_lin = jnp.clip(gu[:, de:], -SWIGLU_LIMIT, SWIGLU_LIMIT)
        hdn = x_glu * jax.nn.sigmoid(SWIGLU_ALPHA * x_glu) * (x_lin + 1.0)
        y = jnp.dot(hdn.astype(BF16), wd_bf[...], preferred_element_type=F32) + bd_ref[0]
        bits = pltpu.bitcast(y.astype(BF16).astype(F32), U32)
        wait_scatter(p)
        y_buf[p] = (bits[:, :half] >> 16) | (bits[:, half:] & jnp.uint32(0xFFFF0000))

    for p in range(2):
        @pl.when((i < n_used) & (i % 2 == p))
        def _():
            compute_step(p)

    @pl.when(i == n_used)
    def _():
        s = i % 2
        wait_gather(s)
        loop_rows(lambda j: scatter(i - 1, j, 1 - s).start())
        wait_scatter(s)
        wait_scatter(1 - s)

    @pl.when((i == n_steps - 1) & (i < n_used))
    def _():
        s = i % 2
        wait_gather(1 - s)
        loop_rows(lambda j: scatter(i, j, s).start())
        wait_scatter(1 - s)
        wait_scatter(s)


def _ffn_fused_call(tile_expert, pair_table, hnp, w_gate_up, b_gate_up, w_down, b_down, n_tiles):
    t, half = hnp.shape
    tm = FFN_TM
    ne, d, de2 = w_gate_up.shape
    return pl.pallas_call(
        functools.partial(_ffn_fused_kernel, tm=tm, n_tok=t),
        grid_spec=pltpu.PrefetchScalarGridSpec(
            num_scalar_prefetch=2,
            grid=(n_tiles,),
            in_specs=[pl.BlockSpec(memory_space=pl.ANY),
                      pl.BlockSpec((1, d, de2), lambda i, te, pr: (te[i], 0, 0)),
                      pl.BlockSpec((1, 1, de2), lambda i, te, pr: (te[i], 0, 0)),
                      pl.BlockSpec((1, de2 // 2, d), lambda i, te, pr: (te[i], 0, 0)),
                      pl.BlockSpec((1, 1, d), lambda i, te, pr: (te[i], 0, 0))],
            out_specs=pl.BlockSpec(memory_space=pl.ANY),
            scratch_shapes=[pltpu.VMEM((2, tm, half), U32), pltpu.VMEM((2, tm, half), U32),
                            pltpu.VMEM((d, de2), BF16), pltpu.VMEM((de2 // 2, d), BF16),
                            pltpu.SemaphoreType.DMA((2,)), pltpu.SemaphoreType.DMA((2,))]),
        out_shape=jax.ShapeDtypeStruct((TOP_K * t + 2 * tm, half), U32),
        compiler_params=pltpu.CompilerParams(dimension_semantics=("arbitrary",), vmem_limit_bytes=VMEM_LIMIT,
                                             has_side_effects=True),
        name="moe_ffn",
    )(tile_expert, pair_table, hnp, w_gate_up, b_gate_up, w_down, b_down)


def _combine_dense_kernel(h_ref, gate_ref, gfin_ref, y0_ref, y1_ref, y2_ref, y3_ref, o_ref):
    half = y0_ref.shape[1] * y0_ref.shape[2]
    lo = h_ref[:, :half]
    hi = h_ref[:, half:]
    for k, y_ref in enumerate((y0_ref, y1_ref, y2_ref, y3_ref)):
        g = gate_ref[:, k:k + 1]
        w = _load_slabs(y_ref)
        lo = lo + g * _unpack_lo(w)
        hi = hi + g * _unpack_hi(w)
    ms = (jnp.sum(lo * lo, axis=1, keepdims=True) + jnp.sum(hi * hi, axis=1, keepdims=True)) / (2 * half)
    inv = lax.rsqrt(ms + RMS_EPS)
    o_ref[:, :half] = lo * inv * gfin_ref[:, :half]
    o_ref[:, half:] = hi * inv * gfin_ref[:, half:]


def _combine_dense_call(h, gate, g_final, yk):
    t, d = h.shape
    tm = COMBINE_TM
    per_k = t // tm
    yspec = lambda k: pl.BlockSpec((tm,) + yk.shape[1:], lambda i: (k * per_k + i, 0, 0))
    return pl.pallas_call(
        _combine_dense_kernel,
        grid=(t // tm,),
        in_specs=[pl.BlockSpec((tm, d), lambda i: (i, 0)),
                  pl.BlockSpec((tm, TOP_K), lambda i: (i, 0)),
                  pl.BlockSpec((1, d), lambda i: (0, 0)),
                  yspec(0), yspec(1), yspec(2), yspec(3)],
        out_specs=pl.BlockSpec((tm, d), lambda i: (i, 0)),
        out_shape=jax.ShapeDtypeStruct((t, d), F32),
        compiler_params=pltpu.CompilerParams(dimension_semantics=("arbitrary",), vmem_limit_bytes=VMEM_LIMIT),
        name="moe_combine",
    )(h, gate, g_final, yk, yk, yk, yk)


def _rope_freqs():
    def inv_freq(dh):
        return (ROPE_THETA ** (-jnp.arange(0, dh, 2, dtype=F32) / dh))[:, None]

    return inv_freq(MLA_ROPE), inv_freq(SWA_HEAD_DIM)


def _winprep_kernel(w_ref, o_ref):
    w = w_ref[...]
    rb = w.shape[0]
    c1 = MLA_Q_RANK + MLA_KV_RANK
    tail = w[:, c1 + MLA_ROPE:]
    lane = lax.broadcasted_iota(I32, (rb, LANES), 1)
    in_rope = (lane >= MLA_NOPE) & (lane < MLA_NOPE + MLA_ROPE)
    kr_p = jnp.where(in_rope, pltpu.roll(w[:, c1:c1 + LANES], MLA_NOPE, axis=1), 0.0)
    pieces = [w[:, :c1], kr_p, tail]
    off = 0
    for pc in pieces:
        o_ref[:, off:off + pc.shape[1]] = pc.astype(BF16)
        off += pc.shape[1]


def _winprep_call(w_in):
    d, n = w_in.shape
    rb = 128
    return pl.pallas_call(
        _winprep_kernel,
        grid=(d // rb,),
        in_specs=[pl.BlockSpec((rb, n), lambda i: (i, 0))],
        out_specs=pl.BlockSpec((rb, _D1), lambda i: (i, 0)),
        out_shape=jax.ShapeDtypeStruct((d, _D1), BF16),
        compiler_params=pltpu.CompilerParams(dimension_semantics=("arbitrary",), vmem_limit_bytes=VMEM_LIMIT),
        name="w_in_prep",
    )(w_in)


def _prep_weights(w_in, w_mla_uq, w_mla_ukv):
    w_in_al = _winprep_call(w_in)

    r = w_mla_uq.shape[0]
    wq = w_mla_uq.reshape(r, MLA_HEADS, MLA_NOPE + MLA_ROPE)
    zq = jnp.zeros((r, MLA_HEADS, LANES - MLA_NOPE - MLA_ROPE), w_mla_uq.dtype)
    wq_pad = jnp.concatenate([wq, zq], axis=-1).reshape(r, MLA_HEADS * LANES).astype(BF16)

    rk = w_mla_ukv.shape[0]
    wkv = w_mla_ukv.reshape(rk, MLA_HEADS, MLA_NOPE + MLA_V)
    wk_aug = jnp.concatenate([wkv[..., :MLA_NOPE], jnp.zeros((rk, MLA_HEADS, LANES - MLA_NOPE), w_mla_ukv.dtype)],
                             axis=-1).reshape(rk, MLA_HEADS * LANES).astype(BF16)
    wv_t = wkv[..., MLA_NOPE:].reshape(rk, MLA_HEADS * MLA_V).T.astype(BF16)
    return w_in_al, wq_pad, wk_aug, wv_t


def kernel(x, mem, positions, g_mix, w_in, g_mla_q, w_mla_uq, g_mla_kv, w_mla_ukv, w_mla_o, swa_sinks, w_swa_o,
           g_mem, w_mem_kv, w_xa_o, b_gate, w_out, g_ffn, w_router, b_router, w_gate_up, b_gate_up, w_down,
           b_down, g_final):
    b, s, d = x.shape
    t = b * s
    depth = g_mix.shape[0]
    h = x.reshape(t, d)
    pos = positions.astype(F32).reshape(1, t)
    fq, f64 = _rope_freqs()
    for l in range(depth):
        w_in_al, wq_pad, wk_aug, wv_t = _prep_weights(w_in[l], w_mla_uq[l], w_mla_ukv[l])
        (qm, km, vmt, qs, ksa, ksb, vsa, vsb, qx, gates) = _proj_call(
            h, pos, fq, f64, g_mix[l][None], w_in_al, g_mla_q[l][None], wq_pad,
            g_mla_kv[l][None], wk_aug, wv_t, b_gate[l][None])
        r3 = lambda a: a.reshape(b, s, a.shape[1])
        omla = _mla_call(r3(qm), r3(km), vmt, s).reshape(t, -1)
        oswa = _swa_call(swa_sinks[l], r3(qs), r3(ksa), r3(ksb), r3(vsa), r3(vsb)).reshape(t, -1)
        m = mem.shape[1]
        kvm = _memkv_call(mem.reshape(b * m, d), g_mem[l][None], w_mem_kv[l].astype(BF16)).reshape(b, m, -1)
        wr_t = w_router[l].T
        wr_hi = wr_t.astype(BF16)
        wr_split = jnp.concatenate([wr_hi, (wr_t - wr_hi.astype(F32)).astype(BF16)], axis=0)
        h_mid, hnp, idx, gate, rank, counts = _merge_call(
            h, omla, oswa, qx, gates, kvm, w_mla_o[l].astype(BF16), w_swa_o[l].astype(BF16),
            w_xa_o[l].astype(BF16), w_out[l].astype(BF16), g_ffn[l][None], wr_split, b_router[l][:, None], s)

        counts = counts[:, 0]
        padded = ((counts + FFN_TM - 1) // FFN_TM) * FFN_TM
        padded_end = jnp.cumsum(padded)
        offsets = padded_end - padded
        experts = jnp.arange(N_EXPERTS, dtype=I32)
        dest = (jnp.sum(jnp.where(idx[..., None] == experts, offsets, 0), axis=-1) + rank).reshape(-1).astype(I32)
        n_tiles = (t * TOP_K) // FFN_TM + N_EXPERTS
        n_used = (padded_end[-1] // FFN_TM).astype(I32)
        ztile = jnp.concatenate([jnp.where(counts % FFN_TM != 0, padded_end - FFN_TM, -1).astype(I32), n_used[None]])
        tile_start = jnp.minimum(jnp.arange(n_tiles, dtype=I32), jnp.maximum(n_used - 1, 0)) * FFN_TM
        tile_expert = jnp.sum((padded_end[None, :] <= tile_start[:, None]).astype(I32), axis=1)
        tile_expert = jnp.minimum(tile_expert, N_EXPERTS - 1)
        tile_valid = jnp.clip(counts[tile_expert] - (tile_start - offsets[tile_expert]), 0, FFN_TM)
        te = jnp.concatenate([tile_expert, n_used[None], tile_valid.astype(I32)])

        xs = _sc_scatter_rows(hnp, dest, n_tiles * FFN_TM)
        y = _ffn_call(te, xs, w_gate_up[l], b_gate_up[l][:, None, :], w_down[l], b_down[l][:, None, :])
        yk = _sc_gather_rows(y, dest)
        if l == depth - 1:
            gfin = g_final[None]
            out = _combine_dense_call(h_mid, gate.T, gfin, yk)
        else:
            raise NotImplementedError("depth > 1 needs a combine without the final norm")
        h = out
    return h.reshape(b, s, d)
```

```python
import functools
import math

import jax
import jax.numpy as jnp
from jax import lax
from jax.experimental import pallas as pl
from jax.experimental.pallas import tpu as pltpu
from jax.experimental.pallas import tpu_sc as plsc

F32 = jnp.float32
BF16 = jnp.bfloat16
U32 = jnp.uint32
I32 = jnp.int32

LANES = 128
ROPE_THETA = 10000.0
RMS_EPS = 1e-6
LOG2E = 1.4426950408889634

MLA_HEADS = 8
MLA_NOPE = 64
MLA_ROPE = 32
MLA_V = 64
MLA_Q_RANK = 256
MLA_KV_RANK = 128
SWA_HEADS = 8
SWA_KV_HEADS = 2
SWA_HEAD_DIM = 64
SWA_WINDOW = 128
XA_HEADS = 4
XA_HEAD_DIM = 128
N_EXPERTS = 32
TOP_K = 4
SWIGLU_ALPHA = 1.702
SWIGLU_LIMIT = 7.0
N_BRANCHES = 3

NEG_BIG = -1e30

PROJ_TM = 256
MLA_TQ = 512
MLA_ONES_ROWS = 16
SWA_TS = 512
MERGE_TM = 512
MERGE_SUB = 256
FFN_TM = 512
FFN_SUB = 256
DISPATCH_TM = 256
COMBINE_TM = 256

VMEM_LIMIT = 56 * 1024 * 1024


def _rms(x, g):
    return x * lax.rsqrt(jnp.mean(x * x, axis=-1, keepdims=True) + RMS_EPS) * g


def _store_slabs(ref, row0, value):
    rows, n, _ = ref.shape
    flat = ref.reshape(rows * n, LANES)
    for c in range(n):
        flat[pl.ds(row0 * n + c, value.shape[0], stride=n), :] = value[:, c * LANES:(c + 1) * LANES]


def _load_slabs(ref, row0=0, m=None):
    rows, n, _ = ref.shape
    m = rows if m is None else m
    flat = ref.reshape(rows * n, LANES)
    return jnp.concatenate([flat[pl.ds(row0 * n + c, m, stride=n), :] for c in range(n)], axis=1)


_A0, _A1 = 0, 512
_B0, _B1 = 512, 1280
_C0, _C1 = 1280, 1792
_D0, _D1 = 1792, 4864


def _rotate_half(x, d, lo, hi):
    n = x.shape[1]
    half = (hi - lo) // 2
    lane = lax.broadcasted_iota(I32, x.shape, 1) % d
    up = pltpu.roll(x, n - half, axis=1)
    dn = pltpu.roll(x, half, axis=1)
    return jnp.where((lane >= lo) & (lane < lo + half), -up, jnp.where((lane >= lo + half) & (lane < hi), dn, 0.0))


def _proj_kernel(x_ref, pos_ref, fq_ref, f64_ref, gmix_ref, win_ref, gq_ref, wq_ref,
                 gkv_ref, wk_ref, wv_ref, bgate_ref,
                 qm_ref, km_ref, vm_ref, qs_ref, ksa_ref, ksb_ref, vsa_ref, vsb_ref, qx_ref, gt_ref):
    x = x_ref[...]
    xn = _rms(x, gmix_ref[...]).astype(BF16)
    tm = x.shape[0]
    pos = pos_ref[...]
    ang16 = fq_ref[...] * pos
    ang32 = f64_ref[...] * pos
    c16, s16, c32, s32 = jnp.cos(ang16), jnp.sin(ang16), jnp.cos(ang32), jnp.sin(ang32)
    one = jnp.ones((MLA_NOPE, tm), F32)
    zero = jnp.zeros((MLA_NOPE, tm), F32)
    pad = LANES - MLA_NOPE - MLA_ROPE
    cosq = jnp.concatenate([one, c16, c16, one[:pad]], axis=0).T
    sinq = jnp.concatenate([zero, s16, s16, zero[:pad]], axis=0).T
    cos64 = jnp.concatenate([c32, c32, c32, c32], axis=0).T
    sin64 = jnp.concatenate([s32, s32, s32, s32], axis=0).T
    rope_lo, rope_hi = MLA_NOPE, MLA_NOPE + MLA_ROPE

    xa = jnp.dot(xn, win_ref[:, _A0:_A1], preferred_element_type=F32)
    cqn = _rms(xa[:, 0:256], gq_ref[...]).astype(BF16)
    qa = jnp.dot(cqn, wq_ref[...], preferred_element_type=F32)
    qb = _rotate_half(qa, LANES, rope_lo, rope_hi)
    q_scale = LOG2E / math.sqrt(MLA_NOPE + MLA_ROPE)
    ckvn = _rms(xa[:, 256:384], gkv_ref[...]).astype(BF16)
    ka = jnp.dot(ckvn, wk_ref[...], preferred_element_type=F32)
    kr = xa[:, 384:512]
    krope = kr * cosq + _rotate_half(kr, LANES, rope_lo, rope_hi) * sinq
    for h in range(MLA_HEADS):
        sl = slice(h * LANES, (h + 1) * LANES)
        qm_ref[:, sl] = ((qa[:, sl] * cosq + qb[:, sl] * sinq) * q_scale).astype(BF16)
        km_ref[:, sl] = (ka[:, sl] + krope).astype(BF16)
    vm_ref[...] = lax.dot_general(wv_ref[...], ckvn, (((1,), (1,)), ((), ())),
                                  preferred_element_type=F32).astype(BF16)

    xb = jnp.dot(xn, win_ref[:, _B0:_B1], preferred_element_type=F32)
    s_scale = LOG2E / math.sqrt(SWA_HEAD_DIM)
    nq = SWA_HEADS * SWA_HEAD_DIM
    qs = xb[:, :nq]
    qs_rot = _rotate_half(qs, SWA_HEAD_DIM, 0, SWA_HEAD_DIM)
    for p in range(SWA_HEADS // 2):
        sl = slice(p * LANES, (p + 1) * LANES)
        qs_ref[:, sl] = ((qs[:, sl] * cos64 + qs_rot[:, sl] * sin64) * s_scale).astype(BF16)
    ks = xb[:, nq:nq + LANES]
    ks = ks * cos64 + _rotate_half(ks, SWA_HEAD_DIM, 0, SWA_HEAD_DIM) * sin64
    ksa_ref[...] = ks.astype(BF16)
    ksb_ref[...] = pltpu.roll(ks, 64, axis=1).astype(BF16)
    vs = xb[:, nq + LANES:nq + 2 * LANES]
    vsa_ref[...] = vs.astype(BF16)
    vsb_ref[...] = pltpu.roll(vs, 64, axis=1).astype(BF16)

    xc = jnp.dot(xn, win_ref[:, _C0:_C1], preferred_element_type=F32)
    qx_ref[...] = (xc * (LOG2E / math.sqrt(XA_HEAD_DIM))).astype(BF16)

    xd = jnp.dot(xn, win_ref[:, _D0:_D1], preferred_element_type=F32) + bgate_ref[...]
    gt_ref[...] = jax.nn.sigmoid(xd).astype(BF16)


def _proj_call(x2, pos, fq, f64, g_mix, w_in_al, g_q, wq, g_kv, wk_aug, wv, b_gate):
    t, d = x2.shape
    tm = PROJ_TM
    row = lambda n: pl.BlockSpec((tm, n), lambda i: (i, 0))
    full = lambda a: pl.BlockSpec(a.shape, lambda i: (0,) * a.ndim)
    out_cols = [1024, 1024, 512, 128, 128, 128, 128, 512, 3072]
    out_specs = [row(n) for n in out_cols]
    out_shape = [jax.ShapeDtypeStruct((t, n), BF16) for n in out_cols]
    vt_rows = wv.shape[0]
    out_specs.insert(2, pl.BlockSpec((vt_rows, tm), lambda i: (0, i)))
    out_shape.insert(2, jax.ShapeDtypeStruct((vt_rows, t), BF16))
    return pl.pallas_call(
        _proj_kernel,
        grid=(t // tm,),
        in_specs=[row(d), pl.BlockSpec((1, tm), lambda i: (0, i)), full(fq), full(f64), full(g_mix),
                  pl.BlockSpec(w_in_al.shape, lambda i: (0, 0), pipeline_mode=pl.Buffered(1)),
                  full(g_q), full(wq), full(g_kv), full(wk_aug), full(wv), full(b_gate)],
        out_specs=out_specs,
        out_shape=out_shape,
        compiler_params=pltpu.CompilerParams(dimension_semantics=("arbitrary",), vmem_limit_bytes=VMEM_LIMIT),
        name="proj",
    )(x2, pos, fq, f64, g_mix, w_in_al, g_q, wq, g_kv, wk_aug, wv, b_gate)


def _mla_kernel(q_ref, k_ref, vt_ref, o_ref, sa_ref, sb_ref, m_ref, acc_ref, *, tq):
    i = pl.program_id(2)
    m_ref[...] = jnp.full(m_ref.shape, NEG_BIG, F32)
    acc_ref[...] = jnp.zeros(acc_ref.shape, F32)
    ones = jnp.ones((MLA_ONES_ROWS, tq), BF16)

    def scores(j, s_ref):
        k0 = pl.multiple_of(j * tq, tq)
        for hh in range(2):
            sl = slice(hh * LANES, (hh + 1) * LANES)
            s_ref[hh] = lax.dot_general(k_ref[0, pl.ds(k0, tq), sl], q_ref[0, :, sl], (((1,), (1,)), ((), ())),
                                        preferred_element_type=F32)

    def update(j, s_ref, masked):
        k0 = pl.multiple_of(j * tq, tq)
        for hh in range(2):
            vt = jnp.concatenate([vt_ref[hh * MLA_V:(hh + 1) * MLA_V, pl.ds(k0, tq)], ones], axis=0)
            st = s_ref[hh]
            if masked:
                kj = lax.broadcasted_iota(I32, (tq, tq), 0)
                qi = lax.broadcasted_iota(I32, (tq, tq), 1)
                st = jnp.where(kj <= qi, st, NEG_BIG)
            m_old = m_ref[hh]
            m_new = jnp.maximum(m_old, jnp.max(st, axis=0, keepdims=True))
            alpha = jnp.exp2(m_old - m_new)
            pt = jnp.exp2(st - m_new)
            acc_ref[hh] = alpha * acc_ref[hh] + jnp.dot(vt, pt.astype(BF16), preferred_element_type=F32)
            m_ref[hh] = m_new

    scores(0, sa_ref)

    def body(jj, carry):
        scores(2 * jj + 1, sb_ref)
        update(2 * jj, sa_ref, False)
        scores(2 * jj + 2, sa_ref)
        update(2 * jj + 1, sb_ref, False)
        return carry

    lax.fori_loop(0, i // 2, body, 0)

    @pl.when(i % 2 == 0)
    def _():
        update(i, sa_ref, True)

    @pl.when(i % 2 == 1)
    def _():
        scores(i, sb_ref)
        update(i - 1, sa_ref, False)
        update(i, sb_ref, True)

    ot = jnp.concatenate([acc_ref[hh, :MLA_V] / acc_ref[hh, MLA_V:MLA_V + 1] for hh in range(2)],
                         axis=0)
    o_ref[0] = ot.T.astype(BF16)


def _mla_call(q, k, vt, seq):
    b, s, _ = q.shape
    assert s == seq
    tq = min(MLA_TQ, s)
    n_pairs = MLA_HEADS // 2
    return pl.pallas_call(
        functools.partial(_mla_kernel, tq=tq),
        grid=(b, n_pairs, s // tq),
        in_specs=[pl.BlockSpec((1, tq, 2 * LANES), lambda bi, hp, i: (bi, i, hp)),
                  pl.BlockSpec((1, s, 2 * LANES), lambda bi, hp, i: (bi, 0, hp)),
                  pl.BlockSpec((2 * MLA_V, s), lambda bi, hp, i: (hp, bi))],
        out_specs=pl.BlockSpec((1, tq, LANES), lambda bi, hp, i: (bi, i, hp)),
        out_shape=jax.ShapeDtypeStruct((b, s, n_pairs * LANES), BF16),
        scratch_shapes=[pltpu.VMEM((2, tq, tq), F32), pltpu.VMEM((2, tq, tq), F32),
                        pltpu.VMEM((2, 1, tq), F32),
                        pltpu.VMEM((2, MLA_V + MLA_ONES_ROWS, tq), F32)],
        compiler_params=pltpu.CompilerParams(dimension_semantics=("arbitrary",) * 3, vmem_limit_bytes=VMEM_LIMIT),
        name="mla_attn",
    )(q, k, vt)


def _swa_kernel(sink_ref, q_ref, ka_ref, kb_ref, va_ref, vb_ref, kah_ref, kbh_ref, vah_ref, vbh_ref, o_ref, *, ts):
    w = SWA_WINDOW
    i = pl.program_id(1)
    ka = jnp.concatenate([kah_ref[0], ka_ref[0]], axis=0)
    kb = jnp.concatenate([kbh_ref[0], kb_ref[0]], axis=0)
    va = jnp.concatenate([vah_ref[0], va_ref[0]], axis=0)
    vb = jnp.concatenate([vbh_ref[0], vb_ref[0]], axis=0)
    lane_k = lax.broadcasted_iota(I32, (2 * w, LANES), 1)
    low = lane_k < SWA_HEAD_DIM
    qi = lax.broadcasted_iota(I32, (2 * w, 2 * w), 0) % w
    kj = lax.broadcasted_iota(I32, (2 * w, 2 * w), 1)
    diff = qi + w - kj
    band = (diff >= 0) & (diff < w)
    lane_o = lax.broadcasted_iota(I32, (w, LANES), 1)
    row2 = lax.broadcasted_iota(I32, (2 * w, 1), 0)
    zero = jnp.zeros((), BF16)
    stacks = ((0, ka, True, va), (1, kb, False, vb), (4, kb, True, vb), (5, ka, False, va))
    for n in range(ts // w):
        mask = band & ((i * (ts // w) + n > 0) | (kj >= w))
        res = []
        for h0, ksrc, keep_low, vsrc in stacks:
            p0 = h0 // 2
            q = jnp.concatenate([q_ref[0, n * w:(n + 1) * w, p0 * LANES:(p0 + 1) * LANES],
                                 q_ref[0, n * w:(n + 1) * w, (p0 + 1) * LANES:(p0 + 2) * LANES]], axis=0)
            kwin = ksrc[n * w:n * w + 2 * w]
            kwin = jnp.where(low if keep_low else ~low, kwin, zero)
            vwin = vsrc[n * w:n * w + 2 * w]
            s = lax.dot_general(q, kwin, (((1,), (1,)), ((), ())), preferred_element_type=F32)
            s = jnp.where(mask, s, NEG_BIG)
            sink = jnp.where(row2 < w, sink_ref[h0], sink_ref[h0 + 2]) * LOG2E
            m = jnp.maximum(jnp.max(s, axis=1, keepdims=True), sink)
            p = jnp.exp2(s - m)
            den = jnp.sum(p, axis=1, keepdims=True) + jnp.exp2(sink - m)
            o = jnp.dot(p.astype(BF16), vwin, preferred_element_type=F32) / den
            res.append(o)
        o02, o13, o46, o57 = res
        sel = lane_o < SWA_HEAD_DIM
        rows = slice(n * w, (n + 1) * w)
        o_ref[0, rows, 0 * LANES:1 * LANES] = jnp.where(sel, o02[:w], o13[:w]).astype(BF16)
        o_ref[0, rows, 1 * LANES:2 * LANES] = jnp.where(sel, o02[w:], o13[w:]).astype(BF16)
        o_ref[0, rows, 2 * LANES:3 * LANES] = jnp.where(sel, o46[:w], o57[:w]).astype(BF16)
        o_ref[0, rows, 3 * LANES:4 * LANES] = jnp.where(sel, o46[w:], o57[w:]).astype(BF16)


def _swa_call(sinks, q, ksa, ksb, vsa, vsb):
    b, s, _ = q.shape
    ts = min(SWA_TS, s)
    w = SWA_WINDOW
    r = ts // w
    main = pl.BlockSpec((1, ts, LANES), lambda bi, i: (bi, i, 0))
    halo = pl.BlockSpec((1, w, LANES), lambda bi, i: (bi, jnp.maximum(i * r - 1, 0), 0))
    return pl.pallas_call(
        functools.partial(_swa_kernel, ts=ts),
        grid=(b, s // ts),
        in_specs=[pl.BlockSpec(memory_space=pltpu.SMEM),
                  pl.BlockSpec((1, ts, 4 * LANES), lambda bi, i: (bi, i, 0)),
                  main, main, main, main, halo, halo, halo, halo],
        out_specs=pl.BlockSpec((1, ts, 4 * LANES), lambda bi, i: (bi, i, 0)),
        out_shape=jax.ShapeDtypeStruct((b, s, 4 * LANES), BF16),
        compiler_params=pltpu.CompilerParams(dimension_semantics=("arbitrary",) * 2, vmem_limit_bytes=VMEM_LIMIT),
        name="swa_attn",
    )(sinks, q, ksa, ksb, vsa, vsb, ksa, ksb, vsa, vsb)


def _memkv_kernel(mem_ref, g_ref, w_ref, o_ref):
    mn = _rms(mem_ref[...], g_ref[...]).astype(BF16)
    o_ref[...] = jnp.dot(mn, w_ref[...], preferred_element_type=F32).astype(BF16)


def _memkv_call(mem2, g_mem, w_mem_kv):
    n, d = mem2.shape
    tm = min(256, n)
    return pl.pallas_call(
        _memkv_kernel,
        grid=(n // tm,),
        in_specs=[pl.BlockSpec((tm, d), lambda i: (i, 0)),
                  pl.BlockSpec(g_mem.shape, lambda i: (0, 0)),
                  pl.BlockSpec(w_mem_kv.shape, lambda i: (0, 0))],
        out_specs=pl.BlockSpec((tm, w_mem_kv.shape[1]), lambda i: (i, 0)),
        out_shape=jax.ShapeDtypeStruct((n, w_mem_kv.shape[1]), BF16),
        compiler_params=pltpu.CompilerParams(dimension_semantics=("arbitrary",), vmem_limit_bytes=VMEM_LIMIT),
        name="mem_kv",
    )(mem2, g_mem, w_mem_kv)


def _merge_kernel(x_ref, omla_ref, oswa_ref, qx_ref, gt_ref, kvm_ref, wmo_ref, wso_ref, wxo_ref, wout_ref,
                  gffn_ref, wr_ref, br_ref,
                  h_ref, hnp_ref, idx_ref, gate_ref, rank_ref, cnt_ref, run_ref, *, tm, sub):
    @pl.when(pl.program_id(0) == 0)
    def _():
        run_ref[...] = jnp.zeros(run_ref.shape, F32)

    d = x_ref.shape[1]
    kv_cols = XA_HEADS * XA_HEAD_DIM
    erow = lax.broadcasted_iota(I32, (N_EXPERTS, sub), 0)
    tri_t = (lax.broadcasted_iota(I32, (sub, sub), 0) < lax.broadcasted_iota(I32, (sub, sub), 1)).astype(BF16)
    nt = (((1,), (1,)), ((), ()))
    run = run_ref[...]
    for hf in range(tm // sub):
        rows = slice(hf * sub, (hf + 1) * sub)

        oxs = []
        for hd in range(XA_HEADS):
            sl = slice(hd * LANES, (hd + 1) * LANES)
            km = kvm_ref[0, :, sl]
            vm = kvm_ref[0, :, kv_cols + hd * LANES:kv_cols + (hd + 1) * LANES]
            s = lax.dot_general(qx_ref[rows, sl], km, nt, preferred_element_type=F32)
            p = jnp.exp2(s - jnp.max(s, axis=1, keepdims=True))
            den = jnp.sum(p, axis=1, keepdims=True)
            oxs.append((jnp.dot(p.astype(BF16), vm, preferred_element_type=F32) / den).astype(BF16))
        oxa = jnp.concatenate(oxs, axis=1)

        merged = (gt_ref[rows, 0:d].astype(F32) * jnp.dot(omla_ref[rows, :], wmo_ref[...], preferred_element_type=F32)
                  + gt_ref[rows, d:2 * d].astype(F32) * jnp.dot(oswa_ref[rows, :], wso_ref[...],
                                                                 preferred_element_type=F32)
                  + gt_ref[rows, 2 * d:3 * d].astype(F32) * jnp.dot(oxa, wxo_ref[...], preferred_element_type=F32))
        h = x_ref[rows, :] + jnp.dot(merged.astype(BF16), wout_ref[...], preferred_element_type=F32)
        h_ref[rows, :] = h

        hn = _rms(h, gffn_ref[...])
        hn_hi = hn.astype(BF16)
        hn_hi32 = hn_hi.astype(F32)
        hn_lo = (hn - hn_hi32).astype(BF16)
        bits = pltpu.bitcast(hn_hi32, U32)
        _store_slabs(hnp_ref, hf * sub, (bits[:, : d // 2] >> 16) | (bits[:, d // 2:] & jnp.uint32(0xFFFF0000)))

        part = lax.dot_general(wr_ref[...], hn_hi, nt, preferred_element_type=F32)
        logits_t = (part[:N_EXPERTS] + part[N_EXPERTS:]
                    + lax.dot_general(wr_ref[0:N_EXPERTS, :], hn_lo, nt, preferred_element_type=F32) + br_ref[...])

        work = logits_t
        vals, idxs, hots = [], [], []
        for _ in range(TOP_K):
            mx = jnp.max(work, axis=0, keepdims=True)
            ix = jnp.min(jnp.where(work == mx, erow, N_EXPERTS), axis=0, keepdims=True)
            hot = erow == ix
            work = jnp.where(hot, -jnp.inf, work)
            vals.append(mx)
            idxs.append(ix)
            hots.append(hot)
        es = [jnp.exp(v - vals[0]) for v in vals]
        den = es[0] + es[1] + es[2] + es[3]
        sel_t = (hots[0] | hots[1] | hots[2] | hots[3])
        prefix_t = jnp.dot(sel_t.astype(BF16), tri_t, preferred_element_type=F32) + run
        for k in range(TOP_K):
            idx_ref[k:k + 1, rows] = idxs[k]
            gate_ref[k:k + 1, rows] = es[k] / den
            rank_ref[k:k + 1, rows] = jnp.sum(jnp.where(hots[k], prefix_t, 0.0), axis=0, keepdims=True).astype(I32)
        run = run + jnp.sum(sel_t.astype(F32), axis=1, keepdims=True)
    run_ref[...] = run
    cnt_ref[...] = run.astype(I32)


def _merge_call(x2, omla, oswa, qx, gates, kvm, wmo, wso, wxo, wout, g_ffn, wr_split, b_router_col, seq):
    t, d = x2.shape
    tm = MERGE_TM
    per_b = seq // tm
    row = lambda n: pl.BlockSpec((tm, n), lambda i: (i, 0))
    col = lambda: pl.BlockSpec((TOP_K, tm), lambda i: (0, i))
    full = lambda a: pl.BlockSpec(a.shape, lambda i: (0,) * a.ndim)
    return pl.pallas_call(
        functools.partial(_merge_kernel, tm=tm, sub=MERGE_SUB),
        grid=(t // tm,),
        in_specs=[row(d), row(512), row(512), row(512), row(3 * d),
                  pl.BlockSpec((1,) + kvm.shape[1:], lambda i: (i // per_b, 0, 0)),
                  full(wmo), full(wso), full(wxo), full(wout), full(g_ffn), full(wr_split), full(b_router_col)],
        out_specs=[row(d), pl.BlockSpec((tm, d // 2 // LANES, LANES), lambda i: (i, 0, 0)), col(), col(), col(),
                   pl.BlockSpec((N_EXPERTS, 1), lambda i: (0, 0))],
        out_shape=[jax.ShapeDtypeStruct((t, d), F32), jax.ShapeDtypeStruct((t, d // 2 // LANES, LANES), U32),
                   jax.ShapeDtypeStruct((TOP_K, t), I32), jax.ShapeDtypeStruct((TOP_K, t), F32),
                   jax.ShapeDtypeStruct((TOP_K, t), I32), jax.ShapeDtypeStruct((N_EXPERTS, 1), I32)],
        scratch_shapes=[pltpu.VMEM((N_EXPERTS, 1), F32)],
        compiler_params=pltpu.CompilerParams(dimension_semantics=("arbitrary",), vmem_limit_bytes=VMEM_LIMIT),
        name="merge_router",
    )(x2, omla, oswa, qx, gates, kvm, wmo, wso, wxo, wout, g_ffn, wr_split, b_router_col)


def _dispatch_kernel(dest_ref, ztile_ref, hn_ref, xs_ref, zbuf, sem, zsem, *, tm, n_tok):
    base = pl.program_id(0) * tm

    @pl.when(pl.program_id(0) == 0)
    def _():
        zbuf[...] = jnp.zeros(zbuf.shape, U32)

        def zcopy(e):
            start = pl.multiple_of(ztile_ref[e], FFN_TM)
            return pltpu.make_async_copy(zbuf, xs_ref.at[pl.ds(start, zbuf.shape[0])], zsem)

        for e in range(N_EXPERTS):
            @pl.when(ztile_ref[e] >= 0)
            def _():
                zcopy(e).start()
        for e in range(N_EXPERTS):
            @pl.when(ztile_ref[e] >= 0)
            def _():
                zcopy(e).wait()

        def unused(i):
            start = pl.multiple_of(i * FFN_TM, FFN_TM)
            return pltpu.make_async_copy(zbuf, xs_ref.at[pl.ds(start, zbuf.shape[0])], zsem)

        n_tiles = xs_ref.shape[0] // FFN_TM
        lax.fori_loop(ztile_ref[N_EXPERTS], n_tiles, lambda i, c: (unused(i).start(), c)[1], 0)
        lax.fori_loop(ztile_ref[N_EXPERTS], n_tiles, lambda i, c: (unused(i).wait(), c)[1], 0)

    def issue(tt, carry):
        for k in range(TOP_K):
            dst = dest_ref[k * n_tok + base + tt]
            pltpu.make_async_copy(hn_ref.at[pl.ds(tt, 1)], xs_ref.at[pl.ds(dst, 1)], sem).start()
        return carry

    lax.fori_loop(0, tm, issue, 0)
    for _ in range(TOP_K):
        pltpu.make_async_copy(hn_ref, xs_ref.at[pl.ds(0, tm)], sem).wait()


def _dispatch_call(dest, ztile, hnp, n_rows):
    t, ns, c = hnp.shape
    tm = DISPATCH_TM
    return pl.pallas_call(
        functools.partial(_dispatch_kernel, tm=tm, n_tok=t),
        grid_spec=pltpu.PrefetchScalarGridSpec(
            num_scalar_prefetch=2,
            grid=(t // tm,),
            in_specs=[pl.BlockSpec((tm, ns, c), lambda i, ds, zt: (i, 0, 0))],
            out_specs=pl.BlockSpec(memory_space=pl.ANY),
            scratch_shapes=[pltpu.VMEM((FFN_TM, ns, c), U32), pltpu.SemaphoreType.DMA(()),
                            pltpu.SemaphoreType.DMA(())]),
        out_shape=jax.ShapeDtypeStruct((n_rows, ns, c), U32),
        compiler_params=pltpu.CompilerParams(dimension_semantics=("arbitrary",), has_side_effects=True),
        name="moe_dispatch",
    )(dest, ztile, hnp)


SC_CORES = 2
SC_SUBCORES = 16
SC_WORKERS = SC_CORES * SC_SUBCORES
SC_CHUNK = 64


def _sc_mesh():
    return plsc.VectorSubcoreMesh(core_axis_name="c", subcore_axis_name="s",
                                  num_cores=SC_CORES, num_subcores=SC_SUBCORES)


def _sc_index_blocks(idx):
    n = idx.shape[0]
    per_w = n // SC_WORKERS
    n_ch = per_w // SC_CHUNK
    assert per_w * SC_WORKERS == n and n_ch * SC_CHUNK == per_w
    return idx.reshape(SC_WORKERS, n_ch, SC_CHUNK), per_w, n_ch


def _sc_scatter_rows(src, idx, n_out):
    idx3, per_w, n_ch = _sc_index_blocks(idx)
    n_src = src.shape[0]
    assert n_src % per_w == 0

    @functools.partial(
        pl.kernel, mesh=_sc_mesh(),
        out_type=jax.ShapeDtypeStruct((n_out,) + src.shape[1:], src.dtype),
        scratch_types=[pltpu.VMEM((n_ch, SC_CHUNK), I32), pltpu.VMEM((SC_CHUNK,) + src.shape[1:], src.dtype),
                       pltpu.SemaphoreType.DMA],
        name="moe_dispatch_sc")
    def k(src_hbm, idx_hbm, out_hbm, idx_v, rows_v, sem):
        wid = lax.axis_index("s") * SC_CORES + lax.axis_index("c")
        base = lax.rem(wid * per_w, n_src)
        pltpu.sync_copy(idx_hbm.at[wid], idx_v)

        @pl.loop(0, n_ch)
        def _(j):
            pltpu.sync_copy(src_hbm.at[pl.ds(base + j * SC_CHUNK, SC_CHUNK)], rows_v)
            pltpu.async_copy(rows_v, out_hbm.at[idx_v.at[j]], sem).wait()

    return k(src, idx3)


def _sc_gather_rows(table, idx):
    idx3, per_w, n_ch = _sc_index_blocks(idx)

    @functools.partial(
        pl.kernel, mesh=_sc_mesh(),
        out_type=jax.ShapeDtypeStruct((idx.shape[0],) + table.shape[1:], table.dtype),
        scratch_types=[pltpu.VMEM((n_ch, SC_CHUNK), I32), pltpu.VMEM((SC_CHUNK,) + table.shape[1:], table.dtype),
                       pltpu.SemaphoreType.DMA],
        name="moe_gather_sc")
    def k(table_hbm, idx_hbm, out_hbm, idx_v, rows_v, sem):
        wid = lax.axis_index("s") * SC_CORES + lax.axis_index("c")
        base = wid * per_w
        pltpu.sync_copy(idx_hbm.at[wid], idx_v)

        @pl.loop(0, n_ch)
        def _(j):
            pltpu.async_copy(table_hbm.at[idx_v.at[j]], rows_v, sem).wait()
            pltpu.sync_copy(rows_v, out_hbm.at[pl.ds(base + j * SC_CHUNK, SC_CHUNK)])

    return k(table, idx3)


def _unpack_lo(w):
    return pltpu.bitcast(w << 16, F32)


def _unpack_hi(w):
    return pltpu.bitcast(w & jnp.uint32(0xFFFF0000), F32)


def _ffn_kernel(te_ref, xs_ref, wgu_hbm, bgu_ref, wd_hbm, bd_ref, y_ref, wgu_f32, wd_f32, wgu_bf, wd_bf, wsem):
    i = pl.program_id(0)
    n = pl.num_programs(0)
    n_used = te_ref[n]
    e = te_ref[i]
    first = (i == 0) | (e != te_ref[jnp.maximum(i - 1, 0)])
    slot = te_ref[2 * n + 1 + i] % 2
    nxt = te_ref[3 * n + 1 + i]

    def fetch(expert, s):
        return (pltpu.make_async_copy(wgu_hbm.at[expert], wgu_f32.at[s], wsem.at[0, s]),
                pltpu.make_async_copy(wd_hbm.at[expert], wd_f32.at[s], wsem.at[1, s]))

    @pl.when(i == 0)
    def _():
        for cp in fetch(e, slot):
            cp.start()

    @pl.when((i < n_used) & first)
    def _():
        for cp in fetch(e, slot):
            cp.wait()
        wgu_bf[...] = wgu_f32[slot].astype(BF16)
        wd_bf[...] = wd_f32[slot].astype(BF16)

        @pl.when(nxt >= 0)
        def _():
            for cp in fetch(nxt, 1 - slot):
                cp.start()

    @pl.when(i < n_used)
    def _():
        w = _load_slabs(xs_ref)
        half = w.shape[1]
        valid = te_ref[n + 1 + i]
        w = jnp.where(lax.broadcasted_iota(I32, w.shape, 0) < valid, w, jnp.uint32(0))
        x_lo = _unpack_lo(w).astype(BF16)
        x_hi = _unpack_hi(w).astype(BF16)
        gu = (jnp.dot(x_lo, wgu_bf[0:half, :], preferred_element_type=F32)
              + jnp.dot(x_hi, wgu_bf[half:, :], preferred_element_type=F32) + bgu_ref[0])
        de = gu.shape[1] // 2
        x_glu = jnp.minimum(gu[:, :de], SWIGLU_LIMIT)
        x_lin = jnp.clip(gu[:, de:], -SWIGLU_LIMIT, SWIGLU_LIMIT)
        hdn = x_glu * jax.nn.sigmoid(SWIGLU_ALPHA * x_glu) * (x_lin + 1.0)
        y = jnp.dot(hdn.astype(BF16), wd_bf[...], preferred_element_type=F32) + bd_ref[0]
        bits = pltpu.bitcast(y.astype(BF16).astype(F32), U32)
        _store_slabs(y_ref, 0, (bits[:, :half] >> 16) | (bits[:, half:] & jnp.uint32(0xFFFF0000)))

    @pl.when(i >= n_used)
    def _():
        y_ref[...] = jnp.zeros(y_ref.shape, U32)


def _ffn_call(tile_table, xs, w_gate_up, b_gate_up, w_down, b_down):
    r, ns, lanes = xs.shape
    tm = FFN_TM
    ne, d, de2 = w_gate_up.shape
    return pl.pallas_call(
        _ffn_kernel,
        grid_spec=pltpu.PrefetchScalarGridSpec(
            num_scalar_prefetch=1,
            grid=(r // tm,),
            in_specs=[pl.BlockSpec((tm, ns, lanes),
                                   lambda i, te: (jnp.minimum(i, jnp.maximum(te[r // tm] - 1, 0)), 0, 0)),
                      pl.BlockSpec(memory_space=pl.ANY),
                      pl.BlockSpec((1, 1, de2), lambda i, te: (te[i], 0, 0)),
                      pl.BlockSpec(memory_space=pl.ANY),
                      pl.BlockSpec((1, 1, d), lambda i, te: (te[i], 0, 0))],
            out_specs=pl.BlockSpec((tm, ns, lanes), lambda i, te: (i, 0, 0)),
            scratch_shapes=[pltpu.VMEM((2, d, de2), F32), pltpu.VMEM((2, de2 // 2, d), F32),
                            pltpu.VMEM((d, de2), BF16), pltpu.VMEM((de2 // 2, d), BF16),
                            pltpu.SemaphoreType.DMA((2, 2))]),
        out_shape=jax.ShapeDtypeStruct((r, ns, lanes), U32),
        compiler_params=pltpu.CompilerParams(dimension_semantics=("arbitrary",), vmem_limit_bytes=VMEM_LIMIT),
        name="moe_ffn",
    )(tile_table, xs, w_gate_up, b_gate_up, w_down, b_down)


def _combine_kernel(dest_ref, h_ref, gate_ref, gfin_ref, y_ref, o_ref, ybuf, sem, *, tm):
    i = pl.program_id(0)
    base = i * tm
    n_tok = pl.num_programs(0) * tm

    def issue(tt, carry):
        tok = base + tt
        for k in range(TOP_K):
            src = dest_ref[k * n_tok + tok]
            pltpu.make_async_copy(y_ref.at[pl.ds(src, 1)], ybuf.at[k, pl.ds(tt, 1)], sem).start()
        return carry

    lax.fori_loop(0, tm, issue, 0)
    for k in range(TOP_K):
        pltpu.make_async_copy(y_ref.at[pl.ds(0, tm)], ybuf.at[k], sem).wait()

    half = ybuf.shape[2] * ybuf.shape[3]
    lo = h_ref[:, :half]
    hi = h_ref[:, half:]
    for k in range(TOP_K):
        g = gate_ref[:, k:k + 1]
        w = _load_slabs(ybuf.at[k])
        lo = lo + g * _unpack_lo(w)
        hi = hi + g * _unpack_hi(w)
    ms = (jnp.sum(lo * lo, axis=1, keepdims=True) + jnp.sum(hi * hi, axis=1, keepdims=True)) / (2 * half)
    inv = lax.rsqrt(ms + RMS_EPS)
    o_ref[:, :half] = lo * inv * gfin_ref[:, :half]
    o_ref[:, half:] = hi * inv * gfin_ref[:, half:]


def _combine_call(dest, h, gate, g_final, y):
    t, d = h.shape
    tm = COMBINE_TM
    return pl.pallas_call(
        functools.partial(_combine_kernel, tm=tm),
        grid_spec=pltpu.PrefetchScalarGridSpec(
            num_scalar_prefetch=1,
            grid=(t // tm,),
            in_specs=[pl.BlockSpec((tm, d), lambda i, ds: (i, 0)),
                      pl.BlockSpec((tm, TOP_K), lambda i, ds: (i, 0)),
                      pl.BlockSpec((1, d), lambda i, ds: (0, 0)),
                      pl.BlockSpec(memory_space=pl.ANY)],
            out_specs=pl.BlockSpec((tm, d), lambda i, ds: (i, 0)),
            scratch_shapes=[pltpu.VMEM((TOP_K, tm) + y.shape[1:], U32), pltpu.SemaphoreType.DMA(())]),
        out_shape=jax.ShapeDtypeStruct((t, d), F32),
        compiler_params=pltpu.CompilerParams(dimension_semantics=("arbitrary",), vmem_limit_bytes=VMEM_LIMIT),
        name="moe_combine",
    )(dest, h, gate, g_final, y)


PAIR_LEAD = 2
TOP_K_BITS = 2
assert 1 << TOP_K_BITS == TOP_K


def _ffn_fused_kernel(te_ref, pair_ref, hn_hbm, wgu_ref, bgu_ref, wd_ref, bd_ref, yk_hbm,
                      xs_buf, y_buf, wgu_bf, wd_bf, gsem, ssem, *, tm, n_tok):
    i = pl.program_id(0)
    n_steps = pl.num_programs(0)
    n_used = te_ref[n_steps]
    half = xs_buf.shape[2]
    trash0 = TOP_K * n_tok

    def pair_of(tile, j):
        return pair_ref[(tile + PAIR_LEAD) * tm + j]

    def gather(tile, j, slot):
        tok = lax.shift_right_logical(jnp.maximum(pair_of(tile, j), 0), TOP_K_BITS)
        return pltpu.make_async_copy(hn_hbm.at[pl.ds(tok, 1)], xs_buf.at[slot, pl.ds(j, 1)], gsem.at[slot])

    def scatter(tile, j, slot):
        pair = pair_of(tile, j)
        real = (pair & (TOP_K - 1)) * n_tok + lax.shift_right_logical(pair, TOP_K_BITS)
        dst = jnp.where(pair >= 0, real, trash0 + slot * tm + j)
        return pltpu.make_async_copy(y_buf.at[slot, pl.ds(j, 1)], yk_hbm.at[pl.ds(dst, 1)], ssem.at[slot])

    def wait_gather(slot):
        pltpu.make_async_copy(hn_hbm.at[pl.ds(0, tm)], xs_buf.at[slot], gsem.at[slot]).wait()

    def wait_scatter(slot):
        pltpu.make_async_copy(y_buf.at[slot], yk_hbm.at[pl.ds(0, tm)], ssem.at[slot]).wait()

    def loop_rows(fn):
        lax.fori_loop(0, tm, lambda j, c: (fn(j), c)[1], 0)

    @pl.when(i == 0)
    def _():
        y_buf[...] = jnp.zeros(y_buf.shape, U32)
        loop_rows(lambda j: gather(0, j, 0).start())
        loop_rows(lambda j: scatter(-2, j, 0).start())

    e = te_ref[i]
    e_prev = te_ref[jnp.maximum(i - 1, 0)]

    @pl.when((i < n_used) & ((i == 0) | (e != e_prev)))
    def _():
        wgu_bf[...] = wgu_ref[0].astype(BF16)
        wd_bf[...] = wd_ref[0].astype(BF16)

    def compute_step(p):
        q = 1 - p
        wait_gather(p)
        w = xs_buf[p]
        x_lo = _unpack_lo(w).astype(BF16)
        x_hi = _unpack_hi(w).astype(BF16)
        for j in range(tm):
            gather(i + 1, j, q).start()
            scatter(i - 1, j, q).start()
        gu = (jnp.dot(x_lo, wgu_bf[0:half, :], preferred_element_type=F32)
              + jnp.dot(x_hi, wgu_bf[half:, :], preferred_element_type=F32) + bgu_ref[0])
        de = gu.shape[1] // 2
        x_glu = jnp.minimum(gu[:, :de], SWIGLU_LIMIT)
        x_lin = jnp.clip(gu[:, de:], -SWIGLU_LIMIT, SWIGLU_LIMIT)
        hdn = x_glu * jax.nn.sigmoid(SWIGLU_ALPHA * x_glu) * (x_lin + 1.0)
        y = jnp.dot(hdn.astype(BF16), wd_bf[...], preferred_element_type=F32) + bd_ref[0]
        bits = pltpu.bitcast(y.astype(BF16).astype(F32), U32)
        wait_scatter(p)
        y_buf[p] = (bits[:, :half] >> 16) | (bits[:, half:] & jnp.uint32(0xFFFF0000))

    for p in range(2):
        @pl.when((i < n_used) & (i % 2 == p))
        def _():
            compute_step(p)

    @pl.when(i == n_used)
    def _():
        s = i % 2
        wait_gather(s)
        loop_rows(lambda j: scatter(i - 1, j, 1 - s).start())
        wait_scatter(s)
        wait_scatter(1 - s)

    @pl.when((i == n_steps - 1) & (i < n_used))
    def _():
        s = i % 2
        wait_gather(1 - s)
        loop_rows(lambda j: scatter(i, j, s).start())
        wait_scatter(1 - s)
        wait_scatter(s)


def _ffn_fused_call(tile_expert, pair_table, hnp, w_gate_up, b_gate_up, w_down, b_down, n_tiles):
    t, half = hnp.shape
    tm = FFN_TM
    ne, d, de2 = w_gate_up.shape
    return pl.pallas_call(
        functools.partial(_ffn_fused_kernel, tm=tm, n_tok=t),
        grid_spec=pltpu.PrefetchScalarGridSpec(
            num_scalar_prefetch=2,
            grid=(n_tiles,),
            in_specs=[pl.BlockSpec(memory_space=pl.ANY),
                      pl.BlockSpec((1, d, de2), lambda i, te, pr: (te[i], 0, 0)),
                      pl.BlockSpec((1, 1, de2), lambda i, te, pr: (te[i], 0, 0)),
                      pl.BlockSpec((1, de2 // 2, d), lambda i, te, pr: (te[i], 0, 0)),
                      pl.BlockSpec((1, 1, d), lambda i, te, pr: (te[i], 0, 0))],
            out_specs=pl.BlockSpec(memory_space=pl.ANY),
            scratch_shapes=[pltpu.VMEM((2, tm, half), U32), pltpu.VMEM((2, tm, half), U32),
                            pltpu.VMEM((d, de2), BF16), pltpu.VMEM((de2 // 2, d), BF16),
                            pltpu.SemaphoreType.DMA((2,)), pltpu.SemaphoreType.DMA((2,))]),
        out_shape=jax.ShapeDtypeStruct((TOP_K * t + 2 * tm, half), U32),
        compiler_params=pltpu.CompilerParams(dimension_semantics=("arbitrary",), vmem_limit_bytes=VMEM_LIMIT,
                                             has_side_effects=True),
        name="moe_ffn",
    )(tile_expert, pair_table, hnp, w_gate_up, b_gate_up, w_down, b_down)


def _combine_dense_kernel(h_ref, gate_ref, gfin_ref, y0_ref, y1_ref, y2_ref, y3_ref, o_ref):
    half = y0_ref.shape[1] * y0_ref.shape[2]
    lo = h_ref[:, :half]
    hi = h_ref[:, half:]
    for k, y_ref in enumerate((y0_ref, y1_ref, y2_ref, y3_ref)):
        g = gate_ref[:, k:k + 1]
        w = _load_slabs(y_ref)
        lo = lo + g * _unpack_lo(w)
        hi = hi + g * _unpack_hi(w)
    ms = (jnp.sum(lo * lo, axis=1, keepdims=True) + jnp.sum(hi * hi, axis=1, keepdims=True)) / (2 * half)
    inv = lax.rsqrt(ms + RMS_EPS)
    o_ref[:, :half] = lo * inv * gfin_ref[:, :half]
    o_ref[:, half:] = hi * inv * gfin_ref[:, half:]


def _combine_dense_call(h, gate, g_final, yk):
    t, d = h.shape
    tm = COMBINE_TM
    per_k = t // tm
    yspec = lambda k: pl.BlockSpec((tm,) + yk.shape[1:], lambda i: (k * per_k + i, 0, 0))
    return pl.pallas_call(
        _combine_dense_kernel,
        grid=(t // tm,),
        in_specs=[pl.BlockSpec((tm, d), lambda i: (i, 0)),
                  pl.BlockSpec((tm, TOP_K), lambda i: (i, 0)),
                  pl.BlockSpec((1, d), lambda i: (0, 0)),
                  yspec(0), yspec(1), yspec(2), yspec(3)],
        out_specs=pl.BlockSpec((tm, d), lambda i: (i, 0)),
        out_shape=jax.ShapeDtypeStruct((t, d), F32),
        compiler_params=pltpu.CompilerParams(dimension_semantics=("arbitrary",), vmem_limit_bytes=VMEM_LIMIT),
        name="moe_combine",
    )(h, gate, g_final, yk, yk, yk, yk)


def _rope_freqs():
    def inv_freq(dh):
        return (ROPE_THETA ** (-jnp.arange(0, dh, 2, dtype=F32) / dh))[:, None]

    return inv_freq(MLA_ROPE), inv_freq(SWA_HEAD_DIM)


def _winprep_kernel(w_ref, o_ref):
    w = w_ref[...]
    rb = w.shape[0]
    c1 = MLA_Q_RANK + MLA_KV_RANK
    tail = w[:, c1 + MLA_ROPE:]
    lane = lax.broadcasted_iota(I32, (rb, LANES), 1)
    in_rope = (lane >= MLA_NOPE) & (lane < MLA_NOPE + MLA_ROPE)
    kr_p = jnp.where(in_rope, pltpu.roll(w[:, c1:c1 + LANES], MLA_NOPE, axis=1), 0.0)
    pieces = [w[:, :c1], kr_p, tail]
    off = 0
    for pc in pieces:
        o_ref[:, off:off + pc.shape[1]] = pc.astype(BF16)
        off += pc.shape[1]


def _winprep_call(w_in):
    d, n = w_in.shape
    rb = 128
    return pl.pallas_call(
        _winprep_kernel,
        grid=(d // rb,),
        in_specs=[pl.BlockSpec((rb, n), lambda i: (i, 0))],
        out_specs=pl.BlockSpec((rb, _D1), lambda i: (i, 0)),
        out_shape=jax.ShapeDtypeStruct((d, _D1), BF16),
        compiler_params=pltpu.CompilerParams(dimension_semantics=("arbitrary",), vmem_limit_bytes=VMEM_LIMIT),
        name="w_in_prep",
    )(w_in)


def _prep_weights(w_in, w_mla_uq, w_mla_ukv):
    w_in_al = _winprep_call(w_in)

    r = w_mla_uq.shape[0]
    wq = w_mla_uq.reshape(r, MLA_HEADS, MLA_NOPE + MLA_ROPE)
    zq = jnp.zeros((r, MLA_HEADS, LANES - MLA_NOPE - MLA_ROPE), w_mla_uq.dtype)
    wq_pad = jnp.concatenate([wq, zq], axis=-1).reshape(r, MLA_HEADS * LANES).astype(BF16)

    rk = w_mla_ukv.shape[0]
    wkv = w_mla_ukv.reshape(rk, MLA_HEADS, MLA_NOPE + MLA_V)
    wk_aug = jnp.concatenate([wkv[..., :MLA_NOPE], jnp.zeros((rk, MLA_HEADS, LANES - MLA_NOPE), w_mla_ukv.dtype)],
                             axis=-1).reshape(rk, MLA_HEADS * LANES).astype(BF16)
    wv_t = wkv[..., MLA_NOPE:].reshape(rk, MLA_HEADS * MLA_V).T.astype(BF16)
    return w_in_al, wq_pad, wk_aug, wv_t


def kernel(x, mem, positions, g_mix, w_in, g_mla_q, w_mla_uq, g_mla_kv, w_mla_ukv, w_mla_o, swa_sinks, w_swa_o,
           g_mem, w_mem_kv, w_xa_o, b_gate, w_out, g_ffn, w_router, b_router, w_gate_up, b_gate_up, w_down,
           b_down, g_final):
    b, s, d = x.shape
    t = b * s
    depth = g_mix.shape[0]
    h = x.reshape(t, d)
    pos = positions.astype(F32).reshape(1, t)
    fq, f64 = _rope_freqs()
    for l in range(depth):
        w_in_al, wq_pad, wk_aug, wv_t = _prep_weights(w_in[l], w_mla_uq[l], w_mla_ukv[l])
        (qm, km, vmt, qs, ksa, ksb, vsa, vsb, qx, gates) = _proj_call(
            h, pos, fq, f64, g_mix[l][None], w_in_al, g_mla_q[l][None], wq_pad,
            g_mla_kv[l][None], wk_aug, wv_t, b_gate[l][None])
        r3 = lambda a: a.reshape(b, s, a.shape[1])
        omla = _mla_call(r3(qm), r3(km), vmt, s).reshape(t, -1)
        oswa = _swa_call(swa_sinks[l], r3(qs), r3(ksa), r3(ksb), r3(vsa), r3(vsb)).reshape(t, -1)
        m = mem.shape[1]
        kvm = _memkv_call(mem.reshape(b * m, d), g_mem[l][None], w_mem_kv[l].astype(BF16)).reshape(b, m, -1)
        wr_t = w_router[l].T
        wr_hi = wr_t.astype(BF16)
        wr_split = jnp.concatenate([wr_hi, (wr_t - wr_hi.astype(F32)).astype(BF16)], axis=0)
        h_mid, hnp, idx, gate, rank, counts = _merge_call(
            h, omla, oswa, qx, gates, kvm, w_mla_o[l].astype(BF16), w_swa_o[l].astype(BF16),
            w_xa_o[l].astype(BF16), w_out[l].astype(BF16), g_ffn[l][None], wr_split, b_router[l][:, None], s)

        counts = counts[:, 0]
        padded = ((counts + FFN_TM - 1) // FFN_TM) * FFN_TM
        padded_end = jnp.cumsum(padded)
        offsets = padded_end - padded
        experts = jnp.arange(N_EXPERTS, dtype=I32)
        dest = (jnp.sum(jnp.where(idx[..., None] == experts, offsets, 0), axis=-1) + rank).reshape(-1).astype(I32)
        n_tiles = (t * TOP_K) // FFN_TM + N_EXPERTS
        n_used = (padded_end[-1] // FFN_TM).astype(I32)
        ztile = jnp.concatenate([jnp.where(counts % FFN_TM != 0, padded_end - FFN_TM, -1).astype(I32), n_used[None]])
        tile_start = jnp.minimum(jnp.arange(n_tiles, dtype=I32), jnp.maximum(n_used - 1, 0)) * FFN_TM
        tile_expert = jnp.sum((padded_end[None, :] <= tile_start[:, None]).astype(I32), axis=1)
        tile_expert = jnp.minimum(tile_expert, N_EXPERTS - 1)
        tile_valid = jnp.clip(counts[tile_expert] - (tile_start - offsets[tile_expert]), 0, FFN_TM)
        nonempty = padded > 0
        tile_group = (jnp.cumsum(nonempty.astype(I32)) - 1)[tile_expert]
        later = jnp.where(nonempty[None, :] & (experts[None, :] > experts[:, None]), experts[None, :], N_EXPERTS)
        next_expert = jnp.min(later, axis=1)
        tile_next = jnp.where(next_expert < N_EXPERTS, next_expert, -1)[tile_expert]
        te = jnp.concatenate([tile_expert, n_used[None], tile_valid.astype(I32), tile_group.astype(I32),
                              tile_next.astype(I32)])

        xs = _sc_scatter_rows(hnp, dest, n_tiles * FFN_TM)
        y = _ffn_call(te, xs, w_gate_up[l], b_gate_up[l][:, None, :], w_down[l], b_down[l][:, None, :])
        yk = _sc_gather_rows(y, dest)
        if l == depth - 1:
            gfin = g_final[None]
            out = _combine_dense_call(h_mid, gate.T, gfin, yk)
        else:
            raise NotImplementedError("depth > 1 needs a combine without the final norm")
        h = out
    return h.reshape(b, s, d)
```

```python
import functools
import math

import jax
import jax.numpy as jnp
from jax import lax
from jax.experimental import pallas as pl
from jax.experimental.pallas import tpu as pltpu
from jax.experimental.pallas import tpu_sc as plsc

F32 = jnp.float32
BF16 = jnp.bfloat16
U32 = jnp.uint32
I32 = jnp.int32

LANES = 128
ROPE_THETA = 10000.0
RMS_EPS = 1e-6
LOG2E = 1.4426950408889634

MLA_HEADS = 8
MLA_NOPE = 64
MLA_ROPE = 32
MLA_V = 64
MLA_Q_RANK = 256
MLA_KV_RANK = 128
SWA_HEADS = 8
SWA_KV_HEADS = 2
SWA_HEAD_DIM = 64
SWA_WINDOW = 128
XA_HEADS = 4
XA_HEAD_DIM = 128
N_EXPERTS = 32
TOP_K = 4
SWIGLU_ALPHA = 1.702
SWIGLU_LIMIT = 7.0
N_BRANCHES = 3

NEG_BIG = -1e30

PROJ_TM = 256
MLA_TQ = 512
MLA_ONES_ROWS = 16
SWA_TS = 512
MERGE_TM = 512
MERGE_SUB = 256
FFN_TM = 512
COMBINE_TM = 256

VMEM_LIMIT = 56 * 1024 * 1024


def _rms(x, g):
    return x * lax.rsqrt(jnp.mean(x * x, axis=-1, keepdims=True) + RMS_EPS) * g


def _store_slabs(ref, row0, value):
    rows, n, _ = ref.shape
    flat = ref.reshape(rows * n, LANES)
    for c in range(n):
        flat[pl.ds(row0 * n + c, value.shape[0], stride=n), :] = value[:, c * LANES:(c + 1) * LANES]


def _load_slabs(ref, row0=0, m=None):
    rows, n, _ = ref.shape
    m = rows if m is None else m
    flat = ref.reshape(rows * n, LANES)
    return jnp.concatenate([flat[pl.ds(row0 * n + c, m, stride=n), :] for c in range(n)], axis=1)


_A0, _A1 = 0, 512
_B0, _B1 = 512, 1280
_C0, _C1 = 1280, 1792
_D0, _D1 = 1792, 4864


def _rotate_half(x, d, lo, hi):
    n = x.shape[1]
    half = (hi - lo) // 2
    lane = lax.broadcasted_iota(I32, x.shape, 1) % d
    up = pltpu.roll(x, n - half, axis=1)
    dn = pltpu.roll(x, half, axis=1)
    return jnp.where((lane >= lo) & (lane < lo + half), -up, jnp.where((lane >= lo + half) & (lane < hi), dn, 0.0))


def _proj_kernel(x_ref, pos_ref, fq_ref, f64_ref, gmix_ref, win_ref, gq_ref, wq_ref,
                 gkv_ref, wk_ref, wv_ref, bgate_ref,
                 qm_ref, km_ref, vm_ref, qs_ref, ksa_ref, ksb_ref, vsa_ref, vsb_ref, qx_ref, gt_ref):
    x = x_ref[...]
    xn = _rms(x, gmix_ref[...]).astype(BF16)
    tm = x.shape[0]
    pos = pos_ref[...]
    ang16 = fq_ref[...] * pos
    ang32 = f64_ref[...] * pos
    c16, s16, c32, s32 = jnp.cos(ang16), jnp.sin(ang16), jnp.cos(ang32), jnp.sin(ang32)
    one = jnp.ones((MLA_NOPE, tm), F32)
    zero = jnp.zeros((MLA_NOPE, tm), F32)
    pad = LANES - MLA_NOPE - MLA_ROPE
    cosq = jnp.concatenate([one, c16, c16, one[:pad]], axis=0).T
    sinq = jnp.concatenate([zero, s16, s16, zero[:pad]], axis=0).T
    cos64 = jnp.concatenate([c32, c32, c32, c32], axis=0).T
    sin64 = jnp.concatenate([s32, s32, s32, s32], axis=0).T
    rope_lo, rope_hi = MLA_NOPE, MLA_NOPE + MLA_ROPE

    xa = jnp.dot(xn, win_ref[:, _A0:_A1], preferred_element_type=F32)
    cqn = _rms(xa[:, 0:256], gq_ref[...]).astype(BF16)
    qa = jnp.dot(cqn, wq_ref[...], preferred_element_type=F32)
    qb = _rotate_half(qa, LANES, rope_lo, rope_hi)
    q_scale = LOG2E / math.sqrt(MLA_NOPE + MLA_ROPE)
    ckvn = _rms(xa[:, 256:384], gkv_ref[...]).astype(BF16)
    ka = jnp.dot(ckvn, wk_ref[...], preferred_element_type=F32)
    kr = xa[:, 384:512]
    krope = kr * cosq + _rotate_half(kr, LANES, rope_lo, rope_hi) * sinq
    for h in range(MLA_HEADS):
        sl = slice(h * LANES, (h + 1) * LANES)
        qm_ref[:, sl] = ((qa[:, sl] * cosq + qb[:, sl] * sinq) * q_scale).astype(BF16)
        km_ref[:, sl] = (ka[:, sl] + krope).astype(BF16)
    vm_ref[...] = lax.dot_general(wv_ref[...], ckvn, (((1,), (1,)), ((), ())),
                                  preferred_element_type=F32).astype(BF16)

    xb = jnp.dot(xn, win_ref[:, _B0:_B1], preferred_element_type=F32)
    s_scale = LOG2E / math.sqrt(SWA_HEAD_DIM)
    nq = SWA_HEADS * SWA_HEAD_DIM
    qs = xb[:, :nq]
    qs_rot = _rotate_half(qs, SWA_HEAD_DIM, 0, SWA_HEAD_DIM)
    for p in range(SWA_HEADS // 2):
        sl = slice(p * LANES, (p + 1) * LANES)
        qs_ref[:, sl] = ((qs[:, sl] * cos64 + qs_rot[:, sl] * sin64) * s_scale).astype(BF16)
    ks = xb[:, nq:nq + LANES]
    ks = ks * cos64 + _rotate_half(ks, SWA_HEAD_DIM, 0, SWA_HEAD_DIM) * sin64
    ksa_ref[...] = ks.astype(BF16)
    ksb_ref[...] = pltpu.roll(ks, 64, axis=1).astype(BF16)
    vs = xb[:, nq + LANES:nq + 2 * LANES]
    vsa_ref[...] = vs.astype(BF16)
    vsb_ref[...] = pltpu.roll(vs, 64, axis=1).astype(BF16)

    xc = jnp.dot(xn, win_ref[:, _C0:_C1], preferred_element_type=F32)
    qx_ref[...] = (xc * (LOG2E / math.sqrt(XA_HEAD_DIM))).astype(BF16)

    xd = jnp.dot(xn, win_ref[:, _D0:_D1], preferred_element_type=F32) + bgate_ref[...]
    gt_ref[...] = jax.nn.sigmoid(xd).astype(BF16)


def _proj_call(x2, pos, fq, f64, g_mix, w_in_al, g_q, wq, g_kv, wk_aug, wv, b_gate):
    t, d = x2.shape
    tm = PROJ_TM
    row = lambda n: pl.BlockSpec((tm, n), lambda i: (i, 0))
    full = lambda a: pl.BlockSpec(a.shape, lambda i: (0,) * a.ndim)
    out_cols = [1024, 1024, 512, 128, 128, 128, 128, 512, 3072]
    out_specs = [row(n) for n in out_cols]
    out_shape = [jax.ShapeDtypeStruct((t, n), BF16) for n in out_cols]
    vt_rows = wv.shape[0]
    out_specs.insert(2, pl.BlockSpec((vt_rows, tm), lambda i: (0, i)))
    out_shape.insert(2, jax.ShapeDtypeStruct((vt_rows, t), BF16))
    return pl.pallas_call(
        _proj_kernel,
        grid=(t // tm,),
        in_specs=[row(d), pl.BlockSpec((1, tm), lambda i: (0, i)), full(fq), full(f64), full(g_mix),
                  pl.BlockSpec(w_in_al.shape, lambda i: (0, 0), pipeline_mode=pl.Buffered(1)),
                  full(g_q), full(wq), full(g_kv), full(wk_aug), full(wv), full(b_gate)],
        out_specs=out_specs,
        out_shape=out_shape,
        compiler_params=pltpu.CompilerParams(dimension_semantics=("arbitrary",), vmem_limit_bytes=VMEM_LIMIT),
        name="proj",
    )(x2, pos, fq, f64, g_mix, w_in_al, g_q, wq, g_kv, wk_aug, wv, b_gate)


def _mla_kernel(q_ref, k_ref, vt_ref, o_ref, sa_ref, sb_ref, m_ref, acc_ref, *, tq):
    i = pl.program_id(2)
    m_ref[...] = jnp.full(m_ref.shape, NEG_BIG, F32)
    acc_ref[...] = jnp.zeros(acc_ref.shape, F32)
    ones = jnp.ones((MLA_ONES_ROWS, tq), BF16)

    def scores(j, s_ref):
        k0 = pl.multiple_of(j * tq, tq)
        for hh in range(2):
            sl = slice(hh * LANES, (hh + 1) * LANES)
            s_ref[hh] = lax.dot_general(k_ref[0, pl.ds(k0, tq), sl], q_ref[0, :, sl], (((1,), (1,)), ((), ())),
                                        preferred_element_type=F32)

    def update(j, s_ref, masked):
        k0 = pl.multiple_of(j * tq, tq)
        for hh in range(2):
            vt = jnp.concatenate([vt_ref[hh * MLA_V:(hh + 1) * MLA_V, pl.ds(k0, tq)], ones], axis=0)
            st = s_ref[hh]
            if masked:
                kj = lax.broadcasted_iota(I32, (tq, tq), 0)
                qi = lax.broadcasted_iota(I32, (tq, tq), 1)
                st = jnp.where(kj <= qi, st, NEG_BIG)
            m_old = m_ref[hh]
            m_new = jnp.maximum(m_old, jnp.max(st, axis=0, keepdims=True))
            alpha = jnp.exp2(m_old - m_new)
            pt = jnp.exp2(st - m_new)
            acc_ref[hh] = alpha * acc_ref[hh] + jnp.dot(vt, pt.astype(BF16), preferred_element_type=F32)
            m_ref[hh] = m_new

    scores(0, sa_ref)

    def body(jj, carry):
        scores(2 * jj + 1, sb_ref)
        update(2 * jj, sa_ref, False)
        scores(2 * jj + 2, sa_ref)
        update(2 * jj + 1, sb_ref, False)
        return carry

    lax.fori_loop(0, i // 2, body, 0)

    @pl.when(i % 2 == 0)
    def _():
        update(i, sa_ref, True)

    @pl.when(i % 2 == 1)
    def _():
        scores(i, sb_ref)
        update(i - 1, sa_ref, False)
        update(i, sb_ref, True)

    ot = jnp.concatenate([acc_ref[hh, :MLA_V] / acc_ref[hh, MLA_V:MLA_V + 1] for hh in range(2)],
                         axis=0)
    o_ref[0] = ot.T.astype(BF16)


def _mla_call(q, k, vt, seq):
    b, s, _ = q.shape
    assert s == seq
    tq = min(MLA_TQ, s)
    n_pairs = MLA_HEADS // 2
    return pl.pallas_call(
        functools.partial(_mla_kernel, tq=tq),
        grid=(b, n_pairs, s // tq),
        in_specs=[pl.BlockSpec((1, tq, 2 * LANES), lambda bi, hp, i: (bi, i, hp)),
                  pl.BlockSpec((1, s, 2 * LANES), lambda bi, hp, i: (bi, 0, hp)),
                  pl.BlockSpec((2 * MLA_V, s), lambda bi, hp, i: (hp, bi))],
        out_specs=pl.BlockSpec((1, tq, LANES), lambda bi, hp, i: (bi, i, hp)),
        out_shape=jax.ShapeDtypeStruct((b, s, n_pairs * LANES), BF16),
        scratch_shapes=[pltpu.VMEM((2, tq, tq), F32), pltpu.VMEM((2, tq, tq), F32),
                        pltpu.VMEM((2, 1, tq), F32),
                        pltpu.VMEM((2, MLA_V + MLA_ONES_ROWS, tq), F32)],
        compiler_params=pltpu.CompilerParams(dimension_semantics=("arbitrary",) * 3, vmem_limit_bytes=VMEM_LIMIT),
        name="mla_attn",
    )(q, k, vt)


def _swa_kernel(sink_ref, q_ref, ka_ref, kb_ref, va_ref, vb_ref, kah_ref, kbh_ref, vah_ref, vbh_ref, o_ref, *, ts):
    w = SWA_WINDOW
    i = pl.program_id(1)
    ka = jnp.concatenate([kah_ref[0], ka_ref[0]], axis=0)
    kb = jnp.concatenate([kbh_ref[0], kb_ref[0]], axis=0)
    va = jnp.concatenate([vah_ref[0], va_ref[0]], axis=0)
    vb = jnp.concatenate([vbh_ref[0], vb_ref[0]], axis=0)
    lane_k = lax.broadcasted_iota(I32, (2 * w, LANES), 1)
    low = lane_k < SWA_HEAD_DIM
    qi = lax.broadcasted_iota(I32, (2 * w, 2 * w), 0) % w
    kj = lax.broadcasted_iota(I32, (2 * w, 2 * w), 1)
    diff = qi + w - kj
    band = (diff >= 0) & (diff < w)
    lane_o = lax.broadcasted_iota(I32, (w, LANES), 1)
    row2 = lax.broadcasted_iota(I32, (2 * w, 1), 0)
    zero = jnp.zeros((), BF16)
    stacks = ((0, ka, True, va), (1, kb, False, vb), (4, kb, True, vb), (5, ka, False, va))
    for n in range(ts // w):
        mask = band & ((i * (ts // w) + n > 0) | (kj >= w))
        res = []
        for h0, ksrc, keep_low, vsrc in stacks:
            p0 = h0 // 2
            q = jnp.concatenate([q_ref[0, n * w:(n + 1) * w, p0 * LANES:(p0 + 1) * LANES],
                                 q_ref[0, n * w:(n + 1) * w, (p0 + 1) * LANES:(p0 + 2) * LANES]], axis=0)
            kwin = ksrc[n * w:n * w + 2 * w]
            kwin = jnp.where(low if keep_low else ~low, kwin, zero)
            vwin = vsrc[n * w:n * w + 2 * w]
            s = lax.dot_general(q, kwin, (((1,), (1,)), ((), ())), preferred_element_type=F32)
            s = jnp.where(mask, s, NEG_BIG)
            sink = jnp.where(row2 < w, sink_ref[h0], sink_ref[h0 + 2]) * LOG2E
            m = jnp.maximum(jnp.max(s, axis=1, keepdims=True), sink)
            p = jnp.exp2(s - m)
            den = jnp.sum(p, axis=1, keepdims=True) + jnp.exp2(sink - m)
            o = jnp.dot(p.astype(BF16), vwin, preferred_element_type=F32) / den
            res.append(o)
        o02, o13, o46, o57 = res
        sel = lane_o < SWA_HEAD_DIM
        rows = slice(n * w, (n + 1) * w)
        o_ref[0, rows, 0 * LANES:1 * LANES] = jnp.where(sel, o02[:w], o13[:w]).astype(BF16)
        o_ref[0, rows, 1 * LANES:2 * LANES] = jnp.where(sel, o02[w:], o13[w:]).astype(BF16)
        o_ref[0, rows, 2 * LANES:3 * LANES] = jnp.where(sel, o46[:w], o57[:w]).astype(BF16)
        o_ref[0, rows, 3 * LANES:4 * LANES] = jnp.where(sel, o46[w:], o57[w:]).astype(BF16)


def _swa_call(sinks, q, ksa, ksb, vsa, vsb):
    b, s, _ = q.shape
    ts = min(SWA_TS, s)
    w = SWA_WINDOW
    r = ts // w
    main = pl.BlockSpec((1, ts, LANES), lambda bi, i: (bi, i, 0))
    halo = pl.BlockSpec((1, w, LANES), lambda bi, i: (bi, jnp.maximum(i * r - 1, 0), 0))
    return pl.pallas_call(
        functools.partial(_swa_kernel, ts=ts),
        grid=(b, s // ts),
        in_specs=[pl.BlockSpec(memory_space=pltpu.SMEM),
                  pl.BlockSpec((1, ts, 4 * LANES), lambda bi, i: (bi, i, 0)),
                  main, main, main, main, halo, halo, halo, halo],
        out_specs=pl.BlockSpec((1, ts, 4 * LANES), lambda bi, i: (bi, i, 0)),
        out_shape=jax.ShapeDtypeStruct((b, s, 4 * LANES), BF16),
        compiler_params=pltpu.CompilerParams(dimension_semantics=("arbitrary",) * 2, vmem_limit_bytes=VMEM_LIMIT),
        name="swa_attn",
    )(sinks, q, ksa, ksb, vsa, vsb, ksa, ksb, vsa, vsb)


def _memkv_kernel(mem_ref, g_ref, w_ref, o_ref):
    mn = _rms(mem_ref[...], g_ref[...]).astype(BF16)
    o_ref[...] = jnp.dot(mn, w_ref[...], preferred_element_type=F32).astype(BF16)


def _memkv_call(mem2, g_mem, w_mem_kv):
    n, d = mem2.shape
    tm = min(256, n)
    return pl.pallas_call(
        _memkv_kernel,
        grid=(n // tm,),
        in_specs=[pl.BlockSpec((tm, d), lambda i: (i, 0)),
                  pl.BlockSpec(g_mem.shape, lambda i: (0, 0)),
                  pl.BlockSpec(w_mem_kv.shape, lambda i: (0, 0))],
        out_specs=pl.BlockSpec((tm, w_mem_kv.shape[1]), lambda i: (i, 0)),
        out_shape=jax.ShapeDtypeStruct((n, w_mem_kv.shape[1]), BF16),
        compiler_params=pltpu.CompilerParams(dimension_semantics=("arbitrary",), vmem_limit_bytes=VMEM_LIMIT),
        name="mem_kv",
    )(mem2, g_mem, w_mem_kv)


def _merge_kernel(x_ref, omla_ref, oswa_ref, qx_ref, gt_ref, kvm_ref, wmo_ref, wso_ref, wxo_ref, wout_ref,
                  gffn_ref, wr_ref, br_ref,
                  h_ref, hnp_ref, idx_ref, gate_ref, rank_ref, cnt_ref, run_ref, *, tm, sub):
    @pl.when(pl.program_id(0) == 0)
    def _():
        run_ref[...] = jnp.zeros(run_ref.shape, F32)

    d = x_ref.shape[1]
    kv_cols = XA_HEADS * XA_HEAD_DIM
    erow = lax.broadcasted_iota(I32, (N_EXPERTS, sub), 0)
    tri_t = (lax.broadcasted_iota(I32, (sub, sub), 0) < lax.broadcasted_iota(I32, (sub, sub), 1)).astype(BF16)
    nt = (((1,), (1,)), ((), ()))
    run = run_ref[...]
    for hf in range(tm // sub):
        rows = slice(hf * sub, (hf + 1) * sub)

        oxs = []
        for hd in range(XA_HEADS):
            sl = slice(hd * LANES, (hd + 1) * LANES)
            km = kvm_ref[0, :, sl]
            vm = kvm_ref[0, :, kv_cols + hd * LANES:kv_cols + (hd + 1) * LANES]
            s = lax.dot_general(qx_ref[rows, sl], km, nt, preferred_element_type=F32)
            p = jnp.exp2(s - jnp.max(s, axis=1, keepdims=True))
            den = jnp.sum(p, axis=1, keepdims=True)
            oxs.append((jnp.dot(p.astype(BF16), vm, preferred_element_type=F32) / den).astype(BF16))
        oxa = jnp.concatenate(oxs, axis=1)

        merged = (gt_ref[rows, 0:d].astype(F32) * jnp.dot(omla_ref[rows, :], wmo_ref[...], preferred_element_type=F32)
                  + gt_ref[rows, d:2 * d].astype(F32) * jnp.dot(oswa_ref[rows, :], wso_ref[...],
                                                                 preferred_element_type=F32)
                  + gt_ref[rows, 2 * d:3 * d].astype(F32) * jnp.dot(oxa, wxo_ref[...], preferred_element_type=F32))
        h = x_ref[rows, :] + jnp.dot(merged.astype(BF16), wout_ref[...], preferred_element_type=F32)
        h_ref[rows, :] = h

        hn = _rms(h, gffn_ref[...])
        hn_hi = hn.astype(BF16)
        hn_hi32 = hn_hi.astype(F32)
        hn_lo = (hn - hn_hi32).astype(BF16)
        bits = pltpu.bitcast(hn_hi32, U32)
        _store_slabs(hnp_ref, hf * sub, (bits[:, : d // 2] >> 16) | (bits[:, d // 2:] & jnp.uint32(0xFFFF0000)))

        part = lax.dot_general(wr_ref[...], hn_hi, nt, preferred_element_type=F32)
        logits_t = (part[:N_EXPERTS] + part[N_EXPERTS:]
                    + lax.dot_general(wr_ref[0:N_EXPERTS, :], hn_lo, nt, preferred_element_type=F32) + br_ref[...])

        work = logits_t
        vals, idxs, hots = [], [], []
        for _ in range(TOP_K):
            mx = jnp.max(work, axis=0, keepdims=True)
            ix = jnp.min(jnp.where(work == mx, erow, N_EXPERTS), axis=0, keepdims=True)
            hot = erow == ix
            work = jnp.where(hot, -jnp.inf, work)
            vals.append(mx)
            idxs.append(ix)
            hots.append(hot)
        es = [jnp.exp(v - vals[0]) for v in vals]
        den = es[0] + es[1] + es[2] + es[3]
        sel_t = (hots[0] | hots[1] | hots[2] | hots[3])
        prefix_t = jnp.dot(sel_t.astype(BF16), tri_t, preferred_element_type=F32) + run
        for k in range(TOP_K):
            idx_ref[k:k + 1, rows] = idxs[k]
            gate_ref[k:k + 1, rows] = es[k] / den
            rank_ref[k:k + 1, rows] = jnp.sum(jnp.where(hots[k], prefix_t, 0.0), axis=0, keepdims=True).astype(I32)
        run = run + jnp.sum(sel_t.astype(F32), axis=1, keepdims=True)
    run_ref[...] = run
    cnt_ref[...] = run.astype(I32)


def _merge_call(x2, omla, oswa, qx, gates, kvm, wmo, wso, wxo, wout, g_ffn, wr_split, b_router_col, seq):
    t, d = x2.shape
    tm = MERGE_TM
    per_b = seq // tm
    row = lambda n: pl.BlockSpec((tm, n), lambda i: (i, 0))
    col = lambda: pl.BlockSpec((TOP_K, tm), lambda i: (0, i))
    full = lambda a: pl.BlockSpec(a.shape, lambda i: (0,) * a.ndim)
    return pl.pallas_call(
        functools.partial(_merge_kernel, tm=tm, sub=MERGE_SUB),
        grid=(t // tm,),
        in_specs=[row(d), row(512), row(512), row(512), row(3 * d),
                  pl.BlockSpec((1,) + kvm.shape[1:], lambda i: (i // per_b, 0, 0)),
                  full(wmo), full(wso), full(wxo), full(wout), full(g_ffn), full(wr_split), full(b_router_col)],
        out_specs=[row(d), pl.BlockSpec((tm, d // 2 // LANES, LANES), lambda i: (i, 0, 0)), col(), col(), col(),
                   pl.BlockSpec((N_EXPERTS, 1), lambda i: (0, 0))],
        out_shape=[jax.ShapeDtypeStruct((t, d), F32), jax.ShapeDtypeStruct((t, d // 2 // LANES, LANES), U32),
                   jax.ShapeDtypeStruct((TOP_K, t), I32), jax.ShapeDtypeStruct((TOP_K, t), F32),
                   jax.ShapeDtypeStruct((TOP_K, t), I32), jax.ShapeDtypeStruct((N_EXPERTS, 1), I32)],
        scratch_shapes=[pltpu.VMEM((N_EXPERTS, 1), F32)],
        compiler_params=pltpu.CompilerParams(dimension_semantics=("arbitrary",), vmem_limit_bytes=VMEM_LIMIT),
        name="merge_router",
    )(x2, omla, oswa, qx, gates, kvm, wmo, wso, wxo, wout, g_ffn, wr_split, b_router_col)


SC_CORES = 2
SC_SUBCORES = 16
SC_WORKERS = SC_CORES * SC_SUBCORES
SC_CHUNK = 64


def _sc_mesh():
    return plsc.VectorSubcoreMesh(core_axis_name="c", subcore_axis_name="s",
                                  num_cores=SC_CORES, num_subcores=SC_SUBCORES)


def _sc_index_blocks(idx):
    n = idx.shape[0]
    per_w = n // SC_WORKERS
    n_ch = per_w // SC_CHUNK
    assert per_w * SC_WORKERS == n and n_ch * SC_CHUNK == per_w
    return idx.reshape(SC_WORKERS, n_ch, SC_CHUNK), per_w, n_ch


def _sc_two_buffer_loop(n_ch, load, store, bufs, load_sems, store_sems):
    assert n_ch % 2 == 0
    a, b = bufs
    la, lb = load_sems
    sa, sb = store_sems
    load(0, a, la).start()

    @pl.loop(0, n_ch, step=2)
    def _(j):
        load(j, a, la).wait()

        @pl.when(j > 0)
        def _():
            store(j - 1, b, sb).wait()

        load(j + 1, b, lb).start()
        store(j, a, sa).start()
        load(j + 1, b, lb).wait()
        store(j, a, sa).wait()

        @pl.when(j + 2 < n_ch)
        def _():
            load(j + 2, a, la).start()

        store(j + 1, b, sb).start()

    store(n_ch - 1, b, sb).wait()


def _sc_scratch(n_ch, row_shape, dtype):
    return [pltpu.VMEM((n_ch, SC_CHUNK), I32), pltpu.VMEM((SC_CHUNK,) + row_shape, dtype),
            pltpu.VMEM((SC_CHUNK,) + row_shape, dtype)] + [pltpu.SemaphoreType.DMA] * 4


def _sc_scatter_rows(src, idx, n_out):
    idx3, per_w, n_ch = _sc_index_blocks(idx)
    n_src = src.shape[0]
    assert n_src % per_w == 0

    @functools.partial(
        pl.kernel, mesh=_sc_mesh(),
        out_type=jax.ShapeDtypeStruct((n_out,) + src.shape[1:], src.dtype),
        scratch_types=_sc_scratch(n_ch, src.shape[1:], src.dtype),
        name="moe_dispatch_sc")
    def k(src_hbm, idx_hbm, out_hbm, idx_v, rows_a, rows_b, la, lb, sa, sb):
        wid = lax.axis_index("s") * SC_CORES + lax.axis_index("c")
        base = lax.rem(wid * per_w, n_src)
        pltpu.sync_copy(idx_hbm.at[wid], idx_v)

        def load(c, buf, sem):
            return pltpu.make_async_copy(src_hbm.at[pl.ds(base + c * SC_CHUNK, SC_CHUNK)], buf, sem)

        def store(c, buf, sem):
            return pltpu.make_async_copy(buf, out_hbm.at[idx_v.at[c]], sem)

        _sc_two_buffer_loop(n_ch, load, store, (rows_a, rows_b), (la, lb), (sa, sb))

    return k(src, idx3)


def _sc_gather_rows(table, idx):
    idx3, per_w, n_ch = _sc_index_blocks(idx)

    @functools.partial(
        pl.kernel, mesh=_sc_mesh(),
        out_type=jax.ShapeDtypeStruct((idx.shape[0],) + table.shape[1:], table.dtype),
        scratch_types=_sc_scratch(n_ch, table.shape[1:], table.dtype),
        name="moe_gather_sc")
    def k(table_hbm, idx_hbm, out_hbm, idx_v, rows_a, rows_b, la, lb, sa, sb):
        wid = lax.axis_index("s") * SC_CORES + lax.axis_index("c")
        base = wid * per_w
        pltpu.sync_copy(idx_hbm.at[wid], idx_v)

        def load(c, buf, sem):
            return pltpu.make_async_copy(table_hbm.at[idx_v.at[c]], buf, sem)

        def store(c, buf, sem):
            return pltpu.make_async_copy(buf, out_hbm.at[pl.ds(base + c * SC_CHUNK, SC_CHUNK)], sem)

        _sc_two_buffer_loop(n_ch, load, store, (rows_a, rows_b), (la, lb), (sa, sb))

    return k(table, idx3)


def _unpack_lo(w):
    return pltpu.bitcast(w << 16, F32)


def _unpack_hi(w):
    return pltpu.bitcast(w & jnp.uint32(0xFFFF0000), F32)


def _ffn_kernel(te_ref, xs_ref, wgu_hbm, bgu_ref, wd_hbm, bd_ref, y_ref, wgu_f32, wd_f32, wgu_bf, wd_bf, wsem):
    i = pl.program_id(0)
    n = pl.num_programs(0)
    n_used = te_ref[n]
    e = te_ref[i]
    first = (i == 0) | (e != te_ref[jnp.maximum(i - 1, 0)])
    slot = te_ref[2 * n + 1 + i] % 2
    nxt = te_ref[3 * n + 1 + i]

    def fetch(expert, s):
        return (pltpu.make_async_copy(wgu_hbm.at[expert], wgu_f32.at[s], wsem.at[0, s]),
                pltpu.make_async_copy(wd_hbm.at[expert], wd_f32.at[s], wsem.at[1, s]))

    @pl.when(i == 0)
    def _():
        for cp in fetch(e, slot):
            cp.start()

    @pl.when((i < n_used) & first)
    def _():
        for cp in fetch(e, slot):
            cp.wait()
        wgu_bf[...] = wgu_f32[slot].astype(BF16)
        wd_bf[...] = wd_f32[slot].astype(BF16)

        @pl.when(nxt >= 0)
        def _():
            for cp in fetch(nxt, 1 - slot):
                cp.start()

    @pl.when(i < n_used)
    def _():
        w = _load_slabs(xs_ref)
        half = w.shape[1]
        valid = te_ref[n + 1 + i]
        w = jnp.where(lax.broadcasted_iota(I32, w.shape, 0) < valid, w, jnp.uint32(0))
        x_lo = _unpack_lo(w).astype(BF16)
        x_hi = _unpack_hi(w).astype(BF16)
        gu = (jnp.dot(x_lo, wgu_bf[0:half, :], preferred_element_type=F32)
              + jnp.dot(x_hi, wgu_bf[half:, :], preferred_element_type=F32) + bgu_ref[0])
        de = gu.shape[1] // 2
        x_glu = jnp.minimum(gu[:, :de], SWIGLU_LIMIT)
        x_lin = jnp.clip(gu[:, de:], -SWIGLU_LIMIT, SWIGLU_LIMIT)
        hdn = x_glu * jax.nn.sigmoid(SWIGLU_ALPHA * x_glu) * (x_lin + 1.0)
        y = jnp.dot(hdn.astype(BF16), wd_bf[...], preferred_element_type=F32) + bd_ref[0]
        bits = pltpu.bitcast(y.astype(BF16).astype(F32), U32)
        _store_slabs(y_ref, 0, (bits[:, :half] >> 16) | (bits[:, half:] & jnp.uint32(0xFFFF0000)))

    @pl.when(i >= n_used)
    def _():
        y_ref[...] = jnp.zeros(y_ref.shape, U32)


def _ffn_call(tile_table, xs, w_gate_up, b_gate_up, w_down, b_down):
    r, ns, lanes = xs.shape
    tm = FFN_TM
    ne, d, de2 = w_gate_up.shape
    return pl.pallas_call(
        _ffn_kernel,
        grid_spec=pltpu.PrefetchScalarGridSpec(
            num_scalar_prefetch=1,
            grid=(r // tm,),
            in_specs=[pl.BlockSpec((tm, ns, lanes),
                                   lambda i, te: (jnp.minimum(i, jnp.maximum(te[r // tm] - 1, 0)), 0, 0)),
                      pl.BlockSpec(memory_space=pl.ANY),
                      pl.BlockSpec((1, 1, de2), lambda i, te: (te[i], 0, 0)),
                      pl.BlockSpec(memory_space=pl.ANY),
                      pl.BlockSpec((1, 1, d), lambda i, te: (te[i], 0, 0))],
            out_specs=pl.BlockSpec((tm, ns, lanes), lambda i, te: (i, 0, 0)),
            scratch_shapes=[pltpu.VMEM((2, d, de2), F32), pltpu.VMEM((2, de2 // 2, d), F32),
                            pltpu.VMEM((d, de2), BF16), pltpu.VMEM((de2 // 2, d), BF16),
                            pltpu.SemaphoreType.DMA((2, 2))]),
        out_shape=jax.ShapeDtypeStruct((r, ns, lanes), U32),
        compiler_params=pltpu.CompilerParams(dimension_semantics=("arbitrary",), vmem_limit_bytes=VMEM_LIMIT),
        name="moe_ffn",
    )(tile_table, xs, w_gate_up, b_gate_up, w_down, b_down)


def _combine_dense_kernel(h_ref, gate_ref, gfin_ref, y0_ref, y1_ref, y2_ref, y3_ref, o_ref):
    half = y0_ref.shape[1] * y0_ref.shape[2]
    lo = h_ref[:, :half]
    hi = h_ref[:, half:]
    for k, y_ref in enumerate((y0_ref, y1_ref, y2_ref, y3_ref)):
        g = gate_ref[:, k:k + 1]
        w = _load_slabs(y_ref)
        lo = lo + g * _unpack_lo(w)
        hi = hi + g * _unpack_hi(w)
    ms = (jnp.sum(lo * lo, axis=1, keepdims=True) + jnp.sum(hi * hi, axis=1, keepdims=True)) / (2 * half)
    inv = lax.rsqrt(ms + RMS_EPS)
    o_ref[:, :half] = lo * inv * gfin_ref[:, :half]
    o_ref[:, half:] = hi * inv * gfin_ref[:, half:]


def _combine_dense_call(h, gate, g_final, yk):
    t, d = h.shape
    tm = COMBINE_TM
    per_k = t // tm
    yspec = lambda k: pl.BlockSpec((tm,) + yk.shape[1:], lambda i: (k * per_k + i, 0, 0))
    return pl.pallas_call(
        _combine_dense_kernel,
        grid=(t // tm,),
        in_specs=[pl.BlockSpec((tm, d), lambda i: (i, 0)),
                  pl.BlockSpec((tm, TOP_K), lambda i: (i, 0)),
                  pl.BlockSpec((1, d), lambda i: (0, 0)),
                  yspec(0), yspec(1), yspec(2), yspec(3)],
        out_specs=pl.BlockSpec((tm, d), lambda i: (i, 0)),
        out_shape=jax.ShapeDtypeStruct((t, d), F32),
        compiler_params=pltpu.CompilerParams(dimension_semantics=("arbitrary",), vmem_limit_bytes=VMEM_LIMIT),
        name="moe_combine",
    )(h, gate, g_final, yk, yk, yk, yk)


def _rope_freqs():
    def inv_freq(dh):
        return (ROPE_THETA ** (-jnp.arange(0, dh, 2, dtype=F32) / dh))[:, None]

    return inv_freq(MLA_ROPE), inv_freq(SWA_HEAD_DIM)


def _winprep_kernel(w_ref, o_ref):
    w = w_ref[...]
    rb = w.shape[0]
    c1 = MLA_Q_RANK + MLA_KV_RANK
    tail = w[:, c1 + MLA_ROPE:]
    lane = lax.broadcasted_iota(I32, (rb, LANES), 1)
    in_rope = (lane >= MLA_NOPE) & (lane < MLA_NOPE + MLA_ROPE)
    kr_p = jnp.where(in_rope, pltpu.roll(w[:, c1:c1 + LANES], MLA_NOPE, axis=1), 0.0)
    pieces = [w[:, :c1], kr_p, tail]
    off = 0
    for pc in pieces:
        o_ref[:, off:off + pc.shape[1]] = pc.astype(BF16)
        off += pc.shape[1]


def _winprep_call(w_in):
    d, n = w_in.shape
    rb = 128
    return pl.pallas_call(
        _winprep_kernel,
        grid=(d // rb,),
        in_specs=[pl.BlockSpec((rb, n), lambda i: (i, 0))],
        out_specs=pl.BlockSpec((rb, _D1), lambda i: (i, 0)),
        out_shape=jax.ShapeDtypeStruct((d, _D1), BF16),
        compiler_params=pltpu.CompilerParams(dimension_semantics=("arbitrary",), vmem_limit_bytes=VMEM_LIMIT),
        name="w_in_prep",
    )(w_in)


def _prep_weights(w_in, w_mla_uq, w_mla_ukv):
    w_in_al = _winprep_call(w_in)

    r = w_mla_uq.shape[0]
    wq = w_mla_uq.reshape(r, MLA_HEADS, MLA_NOPE + MLA_ROPE)
    zq = jnp.zeros((r, MLA_HEADS, LANES - MLA_NOPE - MLA_ROPE), w_mla_uq.dtype)
    wq_pad = jnp.concatenate([wq, zq], axis=-1).reshape(r, MLA_HEADS * LANES).astype(BF16)

    rk = w_mla_ukv.shape[0]
    wkv = w_mla_ukv.reshape(rk, MLA_HEADS, MLA_NOPE + MLA_V)
    wk_aug = jnp.concatenate([wkv[..., :MLA_NOPE], jnp.zeros((rk, MLA_HEADS, LANES - MLA_NOPE), w_mla_ukv.dtype)],
                             axis=-1).reshape(rk, MLA_HEADS * LANES).astype(BF16)
    wv_t = wkv[..., MLA_NOPE:].reshape(rk, MLA_HEADS * MLA_V).T.astype(BF16)
    return w_in_al, wq_pad, wk_aug, wv_t


def kernel(x, mem, positions, g_mix, w_in, g_mla_q, w_mla_uq, g_mla_kv, w_mla_ukv, w_mla_o, swa_sinks, w_swa_o,
           g_mem, w_mem_kv, w_xa_o, b_gate, w_out, g_ffn, w_router, b_router, w_gate_up, b_gate_up, w_down,
           b_down, g_final):
    b, s, d = x.shape
    t = b * s
    depth = g_mix.shape[0]
    h = x.reshape(t, d)
    pos = positions.astype(F32).reshape(1, t)
    fq, f64 = _rope_freqs()
    for l in range(depth):
        w_in_al, wq_pad, wk_aug, wv_t = _prep_weights(w_in[l], w_mla_uq[l], w_mla_ukv[l])
        (qm, km, vmt, qs, ksa, ksb, vsa, vsb, qx, gates) = _proj_call(
            h, pos, fq, f64, g_mix[l][None], w_in_al, g_mla_q[l][None], wq_pad,
            g_mla_kv[l][None], wk_aug, wv_t, b_gate[l][None])
        r3 = lambda a: a.reshape(b, s, a.shape[1])
        omla = _mla_call(r3(qm), r3(km), vmt, s).reshape(t, -1)
        oswa = _swa_call(swa_sinks[l], r3(qs), r3(ksa), r3(ksb), r3(vsa), r3(vsb)).reshape(t, -1)
        m = mem.shape[1]
        kvm = _memkv_call(mem.reshape(b * m, d), g_mem[l][None], w_mem_kv[l].astype(BF16)).reshape(b, m, -1)
        wr_t = w_router[l].T
        wr_hi = wr_t.astype(BF16)
        wr_split = jnp.concatenate([wr_hi, (wr_t - wr_hi.astype(F32)).astype(BF16)], axis=0)
        h_mid, hnp, idx, gate, rank, counts = _merge_call(
            h, omla, oswa, qx, gates, kvm, w_mla_o[l].astype(BF16), w_swa_o[l].astype(BF16),
            w_xa_o[l].astype(BF16), w_out[l].astype(BF16), g_ffn[l][None], wr_split, b_router[l][:, None], s)

        counts = counts[:, 0]
        padded = ((counts + FFN_TM - 1) // FFN_TM) * FFN_TM
        padded_end = jnp.cumsum(padded)
        offsets = padded_end - padded
        experts = jnp.arange(N_EXPERTS, dtype=I32)
        dest = (jnp.sum(jnp.where(idx[..., None] == experts, offsets, 0), axis=-1) + rank).reshape(-1).astype(I32)
        n_tiles = (t * TOP_K) // FFN_TM + N_EXPERTS
        n_used = (padded_end[-1] // FFN_TM).astype(I32)
        tile_start = jnp.minimum(jnp.arange(n_tiles, dtype=I32), jnp.maximum(n_used - 1, 0)) * FFN_TM
        tile_expert = jnp.sum((padded_end[None, :] <= tile_start[:, None]).astype(I32), axis=1)
        tile_expert = jnp.minimum(tile_expert, N_EXPERTS - 1)
        tile_valid = jnp.clip(counts[tile_expert] - (tile_start - offsets[tile_expert]), 0, FFN_TM)
        nonempty = padded > 0
        tile_group = (jnp.cumsum(nonempty.astype(I32)) - 1)[tile_expert]
        later = jnp.where(nonempty[None, :] & (experts[None, :] > experts[:, None]), experts[None, :], N_EXPERTS)
        next_expert = jnp.min(later, axis=1)
        tile_next = jnp.where(next_expert < N_EXPERTS, next_expert, -1)[tile_expert]
        te = jnp.concatenate([tile_expert, n_used[None], tile_valid.astype(I32), tile_group.astype(I32),
                              tile_next.astype(I32)])

        xs = _sc_scatter_rows(hnp, dest, n_tiles * FFN_TM)
        y = _ffn_call(te, xs, w_gate_up[l], b_gate_up[l][:, None, :], w_down[l], b_down[l][:, None, :])
        yk = _sc_gather_rows(y, dest)
        if l == depth - 1:
            gfin = g_final[None]
            out = _combine_dense_call(h_mid, gate.T, gfin, yk)
        else:
            raise NotImplementedError("depth > 1 needs a combine without the final norm")
        h = out
    return h.reshape(b, s, d)
```

```python
import functools
import math

import jax
import jax.numpy as jnp
from jax import lax
from jax.experimental import pallas as pl
from jax.experimental.pallas import tpu as pltpu
from jax.experimental.pallas import tpu_sc as plsc

F32 = jnp.float32
BF16 = jnp.bfloat16
U32 = jnp.uint32
I32 = jnp.int32

LANES = 128
ROPE_THETA = 10000.0
RMS_EPS = 1e-6
LOG2E = 1.4426950408889634

MLA_HEADS = 8
MLA_NOPE = 64
MLA_ROPE = 32
MLA_V = 64
MLA_Q_RANK = 256
MLA_KV_RANK = 128
SWA_HEADS = 8
SWA_KV_HEADS = 2
SWA_HEAD_DIM = 64
SWA_WINDOW = 128
XA_HEADS = 4
XA_HEAD_DIM = 128
N_EXPERTS = 32
TOP_K = 4
SWIGLU_ALPHA = 1.702
SWIGLU_LIMIT = 7.0
N_BRANCHES = 3

NEG_BIG = -1e30

PROJ_TM = 512
PROJ_SUB = 256
MLA_TQ = 512
MLA_ONES_ROWS = 16
SWA_TS = 512
MERGE_TM = 512
MERGE_SUB = 256
FFN_TM = 512
COMBINE_TM = 256

VMEM_LIMIT = 56 * 1024 * 1024


def _rms(x, g):
    return x * lax.rsqrt(jnp.mean(x * x, axis=-1, keepdims=True) + RMS_EPS) * g


def _store_slabs(ref, row0, value):
    rows, n, _ = ref.shape
    flat = ref.reshape(rows * n, LANES)
    for c in range(n):
        flat[pl.ds(row0 * n + c, value.shape[0], stride=n), :] = value[:, c * LANES:(c + 1) * LANES]


def _load_slabs(ref, row0=0, m=None):
    rows, n, _ = ref.shape
    m = rows if m is None else m
    flat = ref.reshape(rows * n, LANES)
    return jnp.concatenate([flat[pl.ds(row0 * n + c, m, stride=n), :] for c in range(n)], axis=1)


_A0, _A1 = 0, 512
_B0, _B1 = 512, 1280
_C0, _C1 = 1280, 1792
_D0, _D1 = 1792, 4864


def _rotate_half(x, d, lo, hi):
    n = x.shape[1]
    half = (hi - lo) // 2
    lane = lax.broadcasted_iota(I32, x.shape, 1) % d
    up = pltpu.roll(x, n - half, axis=1)
    dn = pltpu.roll(x, half, axis=1)
    return jnp.where((lane >= lo) & (lane < lo + half), -up, jnp.where((lane >= lo + half) & (lane < hi), dn, 0.0))


def _proj_kernel(x_ref, pos_ref, fq_ref, f64_ref, gmix_ref, win_ref, gq_ref, wq_ref,
                 gkv_ref, wk_ref, wv_ref, bgate_ref,
                 qm_ref, km_ref, vm_ref, qs_ref, ksa_ref, ksb_ref, vsa_ref, vsb_ref, qx_ref, gt_ref):
    for hf in range(x_ref.shape[0] // PROJ_SUB):
        rows = slice(hf * PROJ_SUB, (hf + 1) * PROJ_SUB)
        _proj_rows(x_ref[rows, :], pos_ref[:, rows], fq_ref, f64_ref, gmix_ref, win_ref, gq_ref, wq_ref, gkv_ref,
                   wk_ref, wv_ref, bgate_ref,
                   [r.at[rows] for r in (qm_ref, km_ref)], vm_ref.at[:, rows],
                   [r.at[rows] for r in (qs_ref, ksa_ref, ksb_ref, vsa_ref, vsb_ref, qx_ref, gt_ref)])


def _proj_rows(x, pos, fq_ref, f64_ref, gmix_ref, win_ref, gq_ref, wq_ref, gkv_ref, wk_ref, wv_ref, bgate_ref,
               mla_refs, vm_ref, other_refs):
    qm_ref, km_ref = mla_refs
    qs_ref, ksa_ref, ksb_ref, vsa_ref, vsb_ref, qx_ref, gt_ref = other_refs
    xn = _rms(x, gmix_ref[...]).astype(BF16)
    tm = x.shape[0]
    ang16 = fq_ref[...] * pos
    ang32 = f64_ref[...] * pos
    c16, s16, c32, s32 = jnp.cos(ang16), jnp.sin(ang16), jnp.cos(ang32), jnp.sin(ang32)
    one = jnp.ones((MLA_NOPE, tm), F32)
    zero = jnp.zeros((MLA_NOPE, tm), F32)
    pad = LANES - MLA_NOPE - MLA_ROPE
    cosq = jnp.concatenate([one, c16, c16, one[:pad]], axis=0).T
    sinq = jnp.concatenate([zero, s16, s16, zero[:pad]], axis=0).T
    cos64 = jnp.concatenate([c32, c32, c32, c32], axis=0).T
    sin64 = jnp.concatenate([s32, s32, s32, s32], axis=0).T
    rope_lo, rope_hi = MLA_NOPE, MLA_NOPE + MLA_ROPE

    xa = jnp.dot(xn, win_ref[:, _A0:_A1], preferred_element_type=F32)
    cqn = _rms(xa[:, 0:256], gq_ref[...]).astype(BF16)
    qa = jnp.dot(cqn, wq_ref[...], preferred_element_type=F32)
    qb = _rotate_half(qa, LANES, rope_lo, rope_hi)
    q_scale = LOG2E / math.sqrt(MLA_NOPE + MLA_ROPE)
    ckvn = _rms(xa[:, 256:384], gkv_ref[...]).astype(BF16)
    ka = jnp.dot(ckvn, wk_ref[...], preferred_element_type=F32)
    kr = xa[:, 384:512]
    krope = kr * cosq + _rotate_half(kr, LANES, rope_lo, rope_hi) * sinq
    for h in range(MLA_HEADS):
        sl = slice(h * LANES, (h + 1) * LANES)
        qm_ref[:, sl] = ((qa[:, sl] * cosq + qb[:, sl] * sinq) * q_scale).astype(BF16)
        km_ref[:, sl] = (ka[:, sl] + krope).astype(BF16)
    vm_ref[...] = lax.dot_general(wv_ref[...], ckvn, (((1,), (1,)), ((), ())),
                                  preferred_element_type=F32).astype(BF16)

    xb = jnp.dot(xn, win_ref[:, _B0:_B1], preferred_element_type=F32)
    s_scale = LOG2E / math.sqrt(SWA_HEAD_DIM)
    nq = SWA_HEADS * SWA_HEAD_DIM
    qs = xb[:, :nq]
    qs_rot = _rotate_half(qs, SWA_HEAD_DIM, 0, SWA_HEAD_DIM)
    for p in range(SWA_HEADS // 2):
        sl = slice(p * LANES, (p + 1) * LANES)
        qs_ref[:, sl] = ((qs[:, sl] * cos64 + qs_rot[:, sl] * sin64) * s_scale).astype(BF16)
    ks = xb[:, nq:nq + LANES]
    ks = ks * cos64 + _rotate_half(ks, SWA_HEAD_DIM, 0, SWA_HEAD_DIM) * sin64
    ksa_ref[...] = ks.astype(BF16)
    ksb_ref[...] = pltpu.roll(ks, 64, axis=1).astype(BF16)
    vs = xb[:, nq + LANES:nq + 2 * LANES]
    vsa_ref[...] = vs.astype(BF16)
    vsb_ref[...] = pltpu.roll(vs, 64, axis=1).astype(BF16)

    xc = jnp.dot(xn, win_ref[:, _C0:_C1], preferred_element_type=F32)
    qx_ref[...] = (xc * (LOG2E / math.sqrt(XA_HEAD_DIM))).astype(BF16)

    xd = jnp.dot(xn, win_ref[:, _D0:_D1], preferred_element_type=F32) + bgate_ref[...]
    gt_ref[...] = jax.nn.sigmoid(xd).astype(BF16)


def _proj_call(x2, pos, fq, f64, g_mix, w_in_al, g_q, wq, g_kv, wk_aug, wv, b_gate):
    t, d = x2.shape
    tm = PROJ_TM
    row = lambda n: pl.BlockSpec((tm, n), lambda i: (i, 0))
    full = lambda a: pl.BlockSpec(a.shape, lambda i: (0,) * a.ndim)
    out_cols = [1024, 1024, 512, 128, 128, 128, 128, 512, 3072]
    out_specs = [row(n) for n in out_cols]
    out_shape = [jax.ShapeDtypeStruct((t, n), BF16) for n in out_cols]
    vt_rows = wv.shape[0]
    out_specs.insert(2, pl.BlockSpec((vt_rows, tm), lambda i: (0, i)))
    out_shape.insert(2, jax.ShapeDtypeStruct((vt_rows, t), BF16))
    return pl.pallas_call(
        _proj_kernel,
        grid=(t // tm,),
        in_specs=[row(d), pl.BlockSpec((1, tm), lambda i: (0, i)), full(fq), full(f64), full(g_mix),
                  pl.BlockSpec(w_in_al.shape, lambda i: (0, 0), pipeline_mode=pl.Buffered(1)),
                  full(g_q), full(wq), full(g_kv), full(wk_aug), full(wv), full(b_gate)],
        out_specs=out_specs,
        out_shape=out_shape,
        compiler_params=pltpu.CompilerParams(dimension_semantics=("arbitrary",), vmem_limit_bytes=VMEM_LIMIT),
        name="proj",
    )(x2, pos, fq, f64, g_mix, w_in_al, g_q, wq, g_kv, wk_aug, wv, b_gate)


def _mla_kernel(q_ref, k_ref, vt_ref, o_ref, sa_ref, sb_ref, m_ref, acc_ref, *, tq):
    i = pl.program_id(2)
    m_ref[...] = jnp.full(m_ref.shape, NEG_BIG, F32)
    acc_ref[...] = jnp.zeros(acc_ref.shape, F32)
    ones = jnp.ones((MLA_ONES_ROWS, tq), BF16)

    def scores(j, s_ref):
        k0 = pl.multiple_of(j * tq, tq)
        for hh in range(2):
            sl = slice(hh * LANES, (hh + 1) * LANES)
            s_ref[hh] = lax.dot_general(k_ref[0, pl.ds(k0, tq), sl], q_ref[0, :, sl], (((1,), (1,)), ((), ())),
                                        preferred_element_type=F32)

    def update(j, s_ref, masked):
        k0 = pl.multiple_of(j * tq, tq)
        for hh in range(2):
            vt = jnp.concatenate([vt_ref[hh * MLA_V:(hh + 1) * MLA_V, pl.ds(k0, tq)], ones], axis=0)
            st = s_ref[hh]
            if masked:
                kj = lax.broadcasted_iota(I32, (tq, tq), 0)
                qi = lax.broadcasted_iota(I32, (tq, tq), 1)
                st = jnp.where(kj <= qi, st, NEG_BIG)
            m_old = m_ref[hh]
            m_new = jnp.maximum(m_old, jnp.max(st, axis=0, keepdims=True))
            alpha = jnp.exp2(m_old - m_new)
            pt = jnp.exp2(st - m_new)
            acc_ref[hh] = alpha * acc_ref[hh] + jnp.dot(vt, pt.astype(BF16), preferred_element_type=F32)
            m_ref[hh] = m_new

    scores(0, sa_ref)

    def body(jj, carry):
        scores(2 * jj + 1, sb_ref)
        update(2 * jj, sa_ref, False)
        scores(2 * jj + 2, sa_ref)
        update(2 * jj + 1, sb_ref, False)
        return carry

    lax.fori_loop(0, i // 2, body, 0)

    @pl.when(i % 2 == 0)
    def _():
        update(i, sa_ref, True)

    @pl.when(i % 2 == 1)
    def _():
        scores(i, sb_ref)
        update(i - 1, sa_ref, False)
        update(i, sb_ref, True)

    ot = jnp.concatenate([acc_ref[hh, :MLA_V] / acc_ref[hh, MLA_V:MLA_V + 1] for hh in range(2)],
                         axis=0)
    o_ref[0] = ot.T.astype(BF16)


def _mla_call(q, k, vt, seq):
    b, s, _ = q.shape
    assert s == seq
    tq = min(MLA_TQ, s)
    n_pairs = MLA_HEADS // 2
    return pl.pallas_call(
        functools.partial(_mla_kernel, tq=tq),
        grid=(b, n_pairs, s // tq),
        in_specs=[pl.BlockSpec((1, tq, 2 * LANES), lambda bi, hp, i: (bi, i, hp)),
                  pl.BlockSpec((1, s, 2 * LANES), lambda bi, hp, i: (bi, 0, hp)),
                  pl.BlockSpec((2 * MLA_V, s), lambda bi, hp, i: (hp, bi))],
        out_specs=pl.BlockSpec((1, tq, LANES), lambda bi, hp, i: (bi, i, hp)),
        out_shape=jax.ShapeDtypeStruct((b, s, n_pairs * LANES), BF16),
        scratch_shapes=[pltpu.VMEM((2, tq, tq), F32), pltpu.VMEM((2, tq, tq), F32),
                        pltpu.VMEM((2, 1, tq), F32),
                        pltpu.VMEM((2, MLA_V + MLA_ONES_ROWS, tq), F32)],
        compiler_params=pltpu.CompilerParams(dimension_semantics=("arbitrary",) * 3, vmem_limit_bytes=VMEM_LIMIT),
        name="mla_attn",
    )(q, k, vt)


def _swa_kernel(sink_ref, q_ref, ka_ref, kb_ref, va_ref, vb_ref, kah_ref, kbh_ref, vah_ref, vbh_ref, o_ref, *, ts):
    w = SWA_WINDOW
    i = pl.program_id(1)
    ka = jnp.concatenate([kah_ref[0], ka_ref[0]], axis=0)
    kb = jnp.concatenate([kbh_ref[0], kb_ref[0]], axis=0)
    va = jnp.concatenate([vah_ref[0], va_ref[0]], axis=0)
    vb = jnp.concatenate([vbh_ref[0], vb_ref[0]], axis=0)
    lane_k = lax.broadcasted_iota(I32, (2 * w, LANES), 1)
    low = lane_k < SWA_HEAD_DIM
    qi = lax.broadcasted_iota(I32, (2 * w, 2 * w), 0) % w
    kj = lax.broadcasted_iota(I32, (2 * w, 2 * w), 1)
    diff = qi + w - kj
    band = (diff >= 0) & (diff < w)
    lane_o = lax.broadcasted_iota(I32, (w, LANES), 1)
    row2 = lax.broadcasted_iota(I32, (2 * w, 1), 0)
    zero = jnp.zeros((), BF16)
    stacks = ((0, ka, True, va), (1, kb, False, vb), (4, kb, True, vb), (5, ka, False, va))
    for n in range(ts // w):
        mask = band & ((i * (ts // w) + n > 0) | (kj >= w))
        res = []
        for h0, ksrc, keep_low, vsrc in stacks:
            p0 = h0 // 2
            q = jnp.concatenate([q_ref[0, n * w:(n + 1) * w, p0 * LANES:(p0 + 1) * LANES],
                                 q_ref[0, n * w:(n + 1) * w, (p0 + 1) * LANES:(p0 + 2) * LANES]], axis=0)
            kwin = ksrc[n * w:n * w + 2 * w]
            kwin = jnp.where(low if keep_low else ~low, kwin, zero)
            vwin = vsrc[n * w:n * w + 2 * w]
            s = lax.dot_general(q, kwin, (((1,), (1,)), ((), ())), preferred_element_type=F32)
            s = jnp.where(mask, s, NEG_BIG)
            sink = jnp.where(row2 < w, sink_ref[h0], sink_ref[h0 + 2]) * LOG2E
            m = jnp.maximum(jnp.max(s, axis=1, keepdims=True), sink)
            p = jnp.exp2(s - m)
            den = jnp.sum(p, axis=1, keepdims=True) + jnp.exp2(sink - m)
            o = jnp.dot(p.astype(BF16), vwin, preferred_element_type=F32) / den
            res.append(o)
        o02, o13, o46, o57 = res
        sel = lane_o < SWA_HEAD_DIM
        rows = slice(n * w, (n + 1) * w)
        o_ref[0, rows, 0 * LANES:1 * LANES] = jnp.where(sel, o02[:w], o13[:w]).astype(BF16)
        o_ref[0, rows, 1 * LANES:2 * LANES] = jnp.where(sel, o02[w:], o13[w:]).astype(BF16)
        o_ref[0, rows, 2 * LANES:3 * LANES] = jnp.where(sel, o46[:w], o57[:w]).astype(BF16)
        o_ref[0, rows, 3 * LANES:4 * LANES] = jnp.where(sel, o46[w:], o57[w:]).astype(BF16)


def _swa_call(sinks, q, ksa, ksb, vsa, vsb):
    b, s, _ = q.shape
    ts = min(SWA_TS, s)
    w = SWA_WINDOW
    r = ts // w
    main = pl.BlockSpec((1, ts, LANES), lambda bi, i: (bi, i, 0))
    halo = pl.BlockSpec((1, w, LANES), lambda bi, i: (bi, jnp.maximum(i * r - 1, 0), 0))
    return pl.pallas_call(
        functools.partial(_swa_kernel, ts=ts),
        grid=(b, s // ts),
        in_specs=[pl.BlockSpec(memory_space=pltpu.SMEM),
                  pl.BlockSpec((1, ts, 4 * LANES), lambda bi, i: (bi, i, 0)),
                  main, main, main, main, halo, halo, halo, halo],
        out_specs=pl.BlockSpec((1, ts, 4 * LANES), lambda bi, i: (bi, i, 0)),
        out_shape=jax.ShapeDtypeStruct((b, s, 4 * LANES), BF16),
        compiler_params=pltpu.CompilerParams(dimension_semantics=("arbitrary",) * 2, vmem_limit_bytes=VMEM_LIMIT),
        name="swa_attn",
    )(sinks, q, ksa, ksb, vsa, vsb, ksa, ksb, vsa, vsb)


def _memkv_kernel(mem_ref, g_ref, w_ref, o_ref):
    mn = _rms(mem_ref[...], g_ref[...]).astype(BF16)
    o_ref[...] = jnp.dot(mn, w_ref[...], preferred_element_type=F32).astype(BF16)


def _memkv_call(mem2, g_mem, w_mem_kv):
    n, d = mem2.shape
    tm = min(256, n)
    return pl.pallas_call(
        _memkv_kernel,
        grid=(n // tm,),
        in_specs=[pl.BlockSpec((tm, d), lambda i: (i, 0)),
                  pl.BlockSpec(g_mem.shape, lambda i: (0, 0)),
                  pl.BlockSpec(w_mem_kv.shape, lambda i: (0, 0))],
        out_specs=pl.BlockSpec((tm, w_mem_kv.shape[1]), lambda i: (i, 0)),
        out_shape=jax.ShapeDtypeStruct((n, w_mem_kv.shape[1]), BF16),
        compiler_params=pltpu.CompilerParams(dimension_semantics=("arbitrary",), vmem_limit_bytes=VMEM_LIMIT),
        name="mem_kv",
    )(mem2, g_mem, w_mem_kv)


def _merge_kernel(x_ref, omla_ref, oswa_ref, qx_ref, gt_ref, kvm_ref, wmo_ref, wso_ref, wxo_ref, wout_ref,
                  gffn_ref, wr_ref, br_ref,
                  h_ref, hnp_ref, idx_ref, gate_ref, rank_ref, cnt_ref, run_ref, *, tm, sub):
    @pl.when(pl.program_id(0) == 0)
    def _():
        run_ref[...] = jnp.zeros(run_ref.shape, F32)

    d = x_ref.shape[1]
    kv_cols = XA_HEADS * XA_HEAD_DIM
    erow = lax.broadcasted_iota(I32, (N_EXPERTS, sub), 0)
    tri_t = (lax.broadcasted_iota(I32, (sub, sub), 0) < lax.broadcasted_iota(I32, (sub, sub), 1)).astype(BF16)
    nt = (((1,), (1,)), ((), ()))
    run = run_ref[...]
    for hf in range(tm // sub):
        rows = slice(hf * sub, (hf + 1) * sub)

        oxs = []
        for hd in range(XA_HEADS):
            sl = slice(hd * LANES, (hd + 1) * LANES)
            km = kvm_ref[0, :, sl]
            vm = kvm_ref[0, :, kv_cols + hd * LANES:kv_cols + (hd + 1) * LANES]
            s = lax.dot_general(qx_ref[rows, sl], km, nt, preferred_element_type=F32)
            p = jnp.exp2(s - jnp.max(s, axis=1, keepdims=True))
            den = jnp.sum(p, axis=1, keepdims=True)
            oxs.append((jnp.dot(p.astype(BF16), vm, preferred_element_type=F32) / den).astype(BF16))
        oxa = jnp.concatenate(oxs, axis=1)

        merged = (gt_ref[rows, 0:d].astype(F32) * jnp.dot(omla_ref[rows, :], wmo_ref[...], preferred_element_type=F32)
                  + gt_ref[rows, d:2 * d].astype(F32) * jnp.dot(oswa_ref[rows, :], wso_ref[...],
                                                                 preferred_element_type=F32)
                  + gt_ref[rows, 2 * d:3 * d].astype(F32) * jnp.dot(oxa, wxo_ref[...], preferred_element_type=F32))
        h = x_ref[rows, :] + jnp.dot(merged.astype(BF16), wout_ref[...], preferred_element_type=F32)
        h_ref[rows, :] = h

        hn = _rms(h, gffn_ref[...])
        hn_hi = hn.astype(BF16)
        hn_hi32 = hn_hi.astype(F32)
        hn_lo = (hn - hn_hi32).astype(BF16)
        bits = pltpu.bitcast(hn_hi32, U32)
        _store_slabs(hnp_ref, hf * sub, (bits[:, : d // 2] >> 16) | (bits[:, d // 2:] & jnp.uint32(0xFFFF0000)))

        part = lax.dot_general(wr_ref[...], hn_hi, nt, preferred_element_type=F32)
        logits_t = (part[:N_EXPERTS] + part[N_EXPERTS:]
                    + lax.dot_general(wr_ref[0:N_EXPERTS, :], hn_lo, nt, preferred_element_type=F32) + br_ref[...])

        work = logits_t
        vals, idxs, hots = [], [], []
        for _ in range(TOP_K):
            mx = jnp.max(work, axis=0, keepdims=True)
            ix = jnp.min(jnp.where(work == mx, erow, N_EXPERTS), axis=0, keepdims=True)
            hot = erow == ix
            work = jnp.where(hot, -jnp.inf, work)
            vals.append(mx)
            idxs.append(ix)
            hots.append(hot)
        es = [jnp.exp(v - vals[0]) for v in vals]
        den = es[0] + es[1] + es[2] + es[3]
        sel_t = (hots[0] | hots[1] | hots[2] | hots[3])
        prefix_t = jnp.dot(sel_t.astype(BF16), tri_t, preferred_element_type=F32) + run
        for k in range(TOP_K):
            idx_ref[k:k + 1, rows] = idxs[k]
            gate_ref[k:k + 1, rows] = es[k] / den
            rank_ref[k:k + 1, rows] = jnp.sum(jnp.where(hots[k], prefix_t, 0.0), axis=0, keepdims=True).astype(I32)
        run = run + jnp.sum(sel_t.astype(F32), axis=1, keepdims=True)
    run_ref[...] = run
    cnt_ref[...] = run.astype(I32)


def _merge_call(x2, omla, oswa, qx, gates, kvm, wmo, wso, wxo, wout, g_ffn, wr_split, b_router_col, seq):
    t, d = x2.shape
    tm = MERGE_TM
    per_b = seq // tm
    row = lambda n: pl.BlockSpec((tm, n), lambda i: (i, 0))
    col = lambda: pl.BlockSpec((TOP_K, tm), lambda i: (0, i))
    full = lambda a: pl.BlockSpec(a.shape, lambda i: (0,) * a.ndim)
    return pl.pallas_call(
        functools.partial(_merge_kernel, tm=tm, sub=MERGE_SUB),
        grid=(t // tm,),
        in_specs=[row(d), row(512), row(512), row(512), row(3 * d),
                  pl.BlockSpec((1,) + kvm.shape[1:], lambda i: (i // per_b, 0, 0)),
                  full(wmo), full(wso), full(wxo), full(wout), full(g_ffn), full(wr_split), full(b_router_col)],
        out_specs=[row(d), pl.BlockSpec((tm, d // 2 // LANES, LANES), lambda i: (i, 0, 0)), col(), col(), col(),
                   pl.BlockSpec((N_EXPERTS, 1), lambda i: (0, 0))],
        out_shape=[jax.ShapeDtypeStruct((t, d), F32), jax.ShapeDtypeStruct((t, d // 2 // LANES, LANES), U32),
                   jax.ShapeDtypeStruct((TOP_K, t), I32), jax.ShapeDtypeStruct((TOP_K, t), F32),
                   jax.ShapeDtypeStruct((TOP_K, t), I32), jax.ShapeDtypeStruct((N_EXPERTS, 1), I32)],
        scratch_shapes=[pltpu.VMEM((N_EXPERTS, 1), F32)],
        compiler_params=pltpu.CompilerParams(dimension_semantics=("arbitrary",), vmem_limit_bytes=VMEM_LIMIT),
        name="merge_router",
    )(x2, omla, oswa, qx, gates, kvm, wmo, wso, wxo, wout, g_ffn, wr_split, b_router_col)


SC_CORES = 2
SC_SUBCORES = 16
SC_WORKERS = SC_CORES * SC_SUBCORES
SC_CHUNK = 64


def _sc_mesh():
    return plsc.VectorSubcoreMesh(core_axis_name="c", subcore_axis_name="s",
                                  num_cores=SC_CORES, num_subcores=SC_SUBCORES)


def _sc_index_blocks(idx):
    n = idx.shape[0]
    per_w = n // SC_WORKERS
    n_ch = per_w // SC_CHUNK
    assert per_w * SC_WORKERS == n and n_ch * SC_CHUNK == per_w
    return idx.reshape(SC_WORKERS, n_ch, SC_CHUNK), per_w, n_ch


def _sc_two_buffer_loop(n_ch, load, store, bufs, load_sems, store_sems):
    assert n_ch % 2 == 0
    a, b = bufs
    la, lb = load_sems
    sa, sb = store_sems
    load(0, a, la).start()

    @pl.loop(0, n_ch, step=2)
    def _(j):
        load(j, a, la).wait()

        @pl.when(j > 0)
        def _():
            store(j - 1, b, sb).wait()

        load(j + 1, b, lb).start()
        store(j, a, sa).start()
        load(j + 1, b, lb).wait()
        store(j, a, sa).wait()

        @pl.when(j + 2 < n_ch)
        def _():
            load(j + 2, a, la).start()

        store(j + 1, b, sb).start()

    store(n_ch - 1, b, sb).wait()


def _sc_scratch(n_ch, row_shape, dtype):
    return [pltpu.VMEM((n_ch, SC_CHUNK), I32), pltpu.VMEM((SC_CHUNK,) + row_shape, dtype),
            pltpu.VMEM((SC_CHUNK,) + row_shape, dtype)] + [pltpu.SemaphoreType.DMA] * 4


def _sc_scatter_rows(src, idx, n_out):
    idx3, per_w, n_ch = _sc_index_blocks(idx)
    n_src = src.shape[0]
    assert n_src % per_w == 0

    @functools.partial(
        pl.kernel, mesh=_sc_mesh(),
        out_type=jax.ShapeDtypeStruct((n_out,) + src.shape[1:], src.dtype),
        scratch_types=_sc_scratch(n_ch, src.shape[1:], src.dtype),
        name="moe_dispatch_sc")
    def k(src_hbm, idx_hbm, out_hbm, idx_v, rows_a, rows_b, la, lb, sa, sb):
        wid = lax.axis_index("s") * SC_CORES + lax.axis_index("c")
        base = lax.rem(wid * per_w, n_src)
        pltpu.sync_copy(idx_hbm.at[wid], idx_v)

        def load(c, buf, sem):
            return pltpu.make_async_copy(src_hbm.at[pl.ds(base + c * SC_CHUNK, SC_CHUNK)], buf, sem)

        def store(c, buf, sem):
            return pltpu.make_async_copy(buf, out_hbm.at[idx_v.at[c]], sem)

        _sc_two_buffer_loop(n_ch, load, store, (rows_a, rows_b), (la, lb), (sa, sb))

    return k(src, idx3)


def _sc_gather_rows(table, idx):
    idx3, per_w, n_ch = _sc_index_blocks(idx)

    @functools.partial(
        pl.kernel, mesh=_sc_mesh(),
        out_type=jax.ShapeDtypeStruct((idx.shape[0],) + table.shape[1:], table.dtype),
        scratch_types=_sc_scratch(n_ch, table.shape[1:], table.dtype),
        name="moe_gather_sc")
    def k(table_hbm, idx_hbm, out_hbm, idx_v, rows_a, rows_b, la, lb, sa, sb):
        wid = lax.axis_index("s") * SC_CORES + lax.axis_index("c")
        base = wid * per_w
        pltpu.sync_copy(idx_hbm.at[wid], idx_v)

        def load(c, buf, sem):
            return pltpu.make_async_copy(table_hbm.at[idx_v.at[c]], buf, sem)

        def store(c, buf, sem):
            return pltpu.make_async_copy(buf, out_hbm.at[pl.ds(base + c * SC_CHUNK, SC_CHUNK)], sem)

        _sc_two_buffer_loop(n_ch, load, store, (rows_a, rows_b), (la, lb), (sa, sb))

    return k(table, idx3)


def _unpack_lo(w):
    return pltpu.bitcast(w << 16, F32)


def _unpack_hi(w):
    return pltpu.bitcast(w & jnp.uint32(0xFFFF0000), F32)


def _ffn_kernel(te_ref, xs_ref, wgu_hbm, bgu_ref, wd_hbm, bd_ref, y_ref, wgu_f32, wd_f32, wgu_bf, wd_bf, wsem):
    i = pl.program_id(0)
    n = pl.num_programs(0)
    n_used = te_ref[n]
    e = te_ref[i]
    first = (i == 0) | (e != te_ref[jnp.maximum(i - 1, 0)])
    slot = te_ref[2 * n + 1 + i] % 2
    nxt = te_ref[3 * n + 1 + i]

    def fetch(expert, s):
        return (pltpu.make_async_copy(wgu_hbm.at[expert], wgu_f32.at[s], wsem.at[0, s]),
                pltpu.make_async_copy(wd_hbm.at[expert], wd_f32.at[s], wsem.at[1, s]))

    @pl.when(i == 0)
    def _():
        for cp in fetch(e, slot):
            cp.start()

    @pl.when((i < n_used) & first)
    def _():
        for cp in fetch(e, slot):
            cp.wait()
        wgu_bf[...] = wgu_f32[slot].astype(BF16)
        wd_bf[...] = wd_f32[slot].astype(BF16)

        @pl.when(nxt >= 0)
        def _():
            for cp in fetch(nxt, 1 - slot):
                cp.start()

    @pl.when(i < n_used)
    def _():
        w = _load_slabs(xs_ref)
        half = w.shape[1]
        valid = te_ref[n + 1 + i]
        w = jnp.where(lax.broadcasted_iota(I32, w.shape, 0) < valid, w, jnp.uint32(0))
        x_lo = _unpack_lo(w).astype(BF16)
        x_hi = _unpack_hi(w).astype(BF16)
        gu = (jnp.dot(x_lo, wgu_bf[0:half, :], preferred_element_type=F32)
              + jnp.dot(x_hi, wgu_bf[half:, :], preferred_element_type=F32) + bgu_ref[0])
        de = gu.shape[1] // 2
        x_glu = jnp.minimum(gu[:, :de], SWIGLU_LIMIT)
        x_lin = jnp.clip(gu[:, de:], -SWIGLU_LIMIT, SWIGLU_LIMIT)
        hdn = x_glu * jax.nn.sigmoid(SWIGLU_ALPHA * x_glu) * (x_lin + 1.0)
        y = jnp.dot(hdn.astype(BF16), wd_bf[...], preferred_element_type=F32) + bd_ref[0]
        bits = pltpu.bitcast(y.astype(BF16).astype(F32), U32)
        _store_slabs(y_ref, 0, (bits[:, :half] >> 16) | (bits[:, half:] & jnp.uint32(0xFFFF0000)))

    @pl.when(i >= n_used)
    def _():
        y_ref[...] = jnp.zeros(y_ref.shape, U32)


def _ffn_call(tile_table, xs, w_gate_up, b_gate_up, w_down, b_down):
    r, ns, lanes = xs.shape
    tm = FFN_TM
    ne, d, de2 = w_gate_up.shape
    return pl.pallas_call(
        _ffn_kernel,
        grid_spec=pltpu.PrefetchScalarGridSpec(
            num_scalar_prefetch=1,
            grid=(r // tm,),
            in_specs=[pl.BlockSpec((tm, ns, lanes),
                                   lambda i, te: (jnp.minimum(i, jnp.maximum(te[r // tm] - 1, 0)), 0, 0)),
                      pl.BlockSpec(memory_space=pl.ANY),
                      pl.BlockSpec((1, 1, de2), lambda i, te: (te[i], 0, 0)),
                      pl.BlockSpec(memory_space=pl.ANY),
                      pl.BlockSpec((1, 1, d), lambda i, te: (te[i], 0, 0))],
            out_specs=pl.BlockSpec((tm, ns, lanes), lambda i, te: (i, 0, 0)),
            scratch_shapes=[pltpu.VMEM((2, d, de2), F32), pltpu.VMEM((2, de2 // 2, d), F32),
                            pltpu.VMEM((d, de2), BF16), pltpu.VMEM((de2 // 2, d), BF16),
                            pltpu.SemaphoreType.DMA((2, 2))]),
        out_shape=jax.ShapeDtypeStruct((r, ns, lanes), U32),
        compiler_params=pltpu.CompilerParams(dimension_semantics=("arbitrary",), vmem_limit_bytes=VMEM_LIMIT),
        name="moe_ffn",
    )(tile_table, xs, w_gate_up, b_gate_up, w_down, b_down)


def _combine_dense_kernel(h_ref, gate_ref, gfin_ref, y0_ref, y1_ref, y2_ref, y3_ref, o_ref):
    half = y0_ref.shape[1] * y0_ref.shape[2]
    lo = h_ref[:, :half]
    hi = h_ref[:, half:]
    for k, y_ref in enumerate((y0_ref, y1_ref, y2_ref, y3_ref)):
        g = gate_ref[:, k:k + 1]
        w = _load_slabs(y_ref)
        lo = lo + g * _unpack_lo(w)
        hi = hi + g * _unpack_hi(w)
    ms = (jnp.sum(lo * lo, axis=1, keepdims=True) + jnp.sum(hi * hi, axis=1, keepdims=True)) / (2 * half)
    inv = lax.rsqrt(ms + RMS_EPS)
    o_ref[:, :half] = lo * inv * gfin_ref[:, :half]
    o_ref[:, half:] = hi * inv * gfin_ref[:, half:]


def _combine_dense_call(h, gate, g_final, yk):
    t, d = h.shape
    tm = COMBINE_TM
    per_k = t // tm
    yspec = lambda k: pl.BlockSpec((tm,) + yk.shape[1:], lambda i: (k * per_k + i, 0, 0))
    return pl.pallas_call(
        _combine_dense_kernel,
        grid=(t // tm,),
        in_specs=[pl.BlockSpec((tm, d), lambda i: (i, 0)),
                  pl.BlockSpec((tm, TOP_K), lambda i: (i, 0)),
                  pl.BlockSpec((1, d), lambda i: (0, 0)),
                  yspec(0), yspec(1), yspec(2), yspec(3)],
        out_specs=pl.BlockSpec((tm, d), lambda i: (i, 0)),
        out_shape=jax.ShapeDtypeStruct((t, d), F32),
        compiler_params=pltpu.CompilerParams(dimension_semantics=("arbitrary",), vmem_limit_bytes=VMEM_LIMIT),
        name="moe_combine",
    )(h, gate, g_final, yk, yk, yk, yk)


def _rope_freqs():
    def inv_freq(dh):
        return (ROPE_THETA ** (-jnp.arange(0, dh, 2, dtype=F32) / dh))[:, None]

    return inv_freq(MLA_ROPE), inv_freq(SWA_HEAD_DIM)


def _winprep_kernel(w_ref, o_ref):
    w = w_ref[...]
    rb = w.shape[0]
    c1 = MLA_Q_RANK + MLA_KV_RANK
    tail = w[:, c1 + MLA_ROPE:]
    lane = lax.broadcasted_iota(I32, (rb, LANES), 1)
    in_rope = (lane >= MLA_NOPE) & (lane < MLA_NOPE + MLA_ROPE)
    kr_p = jnp.where(in_rope, pltpu.roll(w[:, c1:c1 + LANES], MLA_NOPE, axis=1), 0.0)
    pieces = [w[:, :c1], kr_p, tail]
    off = 0
    for pc in pieces:
        o_ref[:, off:off + pc.shape[1]] = pc.astype(BF16)
        off += pc.shape[1]


def _winprep_call(w_in):
    d, n = w_in.shape
    rb = 128
    return pl.pallas_call(
        _winprep_kernel,
        grid=(d // rb,),
        in_specs=[pl.BlockSpec((rb, n), lambda i: (i, 0))],
        out_specs=pl.BlockSpec((rb, _D1), lambda i: (i, 0)),
        out_shape=jax.ShapeDtypeStruct((d, _D1), BF16),
        compiler_params=pltpu.CompilerParams(dimension_semantics=("arbitrary",), vmem_limit_bytes=VMEM_LIMIT),
        name="w_in_prep",
    )(w_in)


def _prep_weights(w_in, w_mla_uq, w_mla_ukv):
    w_in_al = _winprep_call(w_in)

    r = w_mla_uq.shape[0]
    wq = w_mla_uq.reshape(r, MLA_HEADS, MLA_NOPE + MLA_ROPE)
    zq = jnp.zeros((r, MLA_HEADS, LANES - MLA_NOPE - MLA_ROPE), w_mla_uq.dtype)
    wq_pad = jnp.concatenate([wq, zq], axis=-1).reshape(r, MLA_HEADS * LANES).astype(BF16)

    rk = w_mla_ukv.shape[0]
    wkv = w_mla_ukv.reshape(rk, MLA_HEADS, MLA_NOPE + MLA_V)
    wk_aug = jnp.concatenate([wkv[..., :MLA_NOPE], jnp.zeros((rk, MLA_HEADS, LANES - MLA_NOPE), w_mla_ukv.dtype)],
                             axis=-1).reshape(rk, MLA_HEADS * LANES).astype(BF16)
    wv_t = wkv[..., MLA_NOPE:].reshape(rk, MLA_HEADS * MLA_V).T.astype(BF16)
    return w_in_al, wq_pad, wk_aug, wv_t


def kernel(x, mem, positions, g_mix, w_in, g_mla_q, w_mla_uq, g_mla_kv, w_mla_ukv, w_mla_o, swa_sinks, w_swa_o,
           g_mem, w_mem_kv, w_xa_o, b_gate, w_out, g_ffn, w_router, b_router, w_gate_up, b_gate_up, w_down,
           b_down, g_final):
    b, s, d = x.shape
    t = b * s
    depth = g_mix.shape[0]
    h = x.reshape(t, d)
    pos = positions.astype(F32).reshape(1, t)
    fq, f64 = _rope_freqs()
    for l in range(depth):
        w_in_al, wq_pad, wk_aug, wv_t = _prep_weights(w_in[l], w_mla_uq[l], w_mla_ukv[l])
        (qm, km, vmt, qs, ksa, ksb, vsa, vsb, qx, gates) = _proj_call(
            h, pos, fq, f64, g_mix[l][None], w_in_al, g_mla_q[l][None], wq_pad,
            g_mla_kv[l][None], wk_aug, wv_t, b_gate[l][None])
        r3 = lambda a: a.reshape(b, s, a.shape[1])
        omla = _mla_call(r3(qm), r3(km), vmt, s).reshape(t, -1)
        oswa = _swa_call(swa_sinks[l], r3(qs), r3(ksa), r3(ksb), r3(vsa), r3(vsb)).reshape(t, -1)
        m = mem.shape[1]
        kvm = _memkv_call(mem.reshape(b * m, d), g_mem[l][None], w_mem_kv[l].astype(BF16)).reshape(b, m, -1)
        wr_t = w_router[l].T
        wr_hi = wr_t.astype(BF16)
        wr_split = jnp.concatenate([wr_hi, (wr_t - wr_hi.astype(F32)).astype(BF16)], axis=0)
        h_mid, hnp, idx, gate, rank, counts = _merge_call(
            h, omla, oswa, qx, gates, kvm, w_mla_o[l].astype(BF16), w_swa_o[l].astype(BF16),
            w_xa_o[l].astype(BF16), w_out[l].astype(BF16), g_ffn[l][None], wr_split, b_router[l][:, None], s)

        counts = counts[:, 0]
        padded = ((counts + FFN_TM - 1) // FFN_TM) * FFN_TM
        padded_end = jnp.cumsum(padded)
        offsets = padded_end - padded
        experts = jnp.arange(N_EXPERTS, dtype=I32)
        dest = (jnp.sum(jnp.where(idx[..., None] == experts, offsets, 0), axis=-1) + rank).reshape(-1).astype(I32)
        n_tiles = (t * TOP_K) // FFN_TM + N_EXPERTS
        n_used = (padded_end[-1] // FFN_TM).astype(I32)
        tile_start = jnp.minimum(jnp.arange(n_tiles, dtype=I32), jnp.maximum(n_used - 1, 0)) * FFN_TM
        tile_expert = jnp.sum((padded_end[None, :] <= tile_start[:, None]).astype(I32), axis=1)
        tile_expert = jnp.minimum(tile_expert, N_EXPERTS - 1)
        onehot = tile_expert[:, None] == experts[None, :]
        pick = lambda v: jnp.sum(jnp.where(onehot, v[None, :], 0), axis=1)
        tile_valid = jnp.clip(pick(counts) - (tile_start - pick(offsets)), 0, FFN_TM)
        nonempty = padded > 0
        tile_group = pick(jnp.cumsum(nonempty.astype(I32)) - 1)
        later = jnp.where(nonempty[None, :] & (experts[None, :] > experts[:, None]), experts[None, :], N_EXPERTS)
        next_expert = jnp.min(later, axis=1)
        tile_next = pick(jnp.where(next_expert < N_EXPERTS, next_expert, -1))
        te = jnp.concatenate([tile_expert, n_used[None], tile_valid.astype(I32), tile_group.astype(I32),
                              tile_next.astype(I32)])

        xs = _sc_scatter_rows(hnp, dest, n_tiles * FFN_TM)
        y = _ffn_call(te, xs, w_gate_up[l], b_gate_up[l][:, None, :], w_down[l], b_down[l][:, None, :])
        yk = _sc_gather_rows(y, dest)
        if l == depth - 1:
            gfin = g_final[None]
            out = _combine_dense_call(h_mid, gate.T, gfin, yk)
        else:
            raise NotImplementedError("depth > 1 needs a combine without the final norm")
        h = out
    return h.reshape(b, s, d)
```

```python
import functools
import math

import jax
import jax.numpy as jnp
from jax import lax
from jax.experimental import pallas as pl
from jax.experimental.pallas import tpu as pltpu
from jax.experimental.pallas import tpu_sc as plsc

F32 = jnp.float32
BF16 = jnp.bfloat16
U32 = jnp.uint32
I32 = jnp.int32

LANES = 128
ROPE_THETA = 10000.0
RMS_EPS = 1e-6
LOG2E = 1.4426950408889634

MLA_HEADS = 8
MLA_NOPE = 64
MLA_ROPE = 32
MLA_V = 64
MLA_Q_RANK = 256
MLA_KV_RANK = 128
SWA_HEADS = 8
SWA_KV_HEADS = 2
SWA_HEAD_DIM = 64
SWA_WINDOW = 128
XA_HEADS = 4
XA_HEAD_DIM = 128
N_EXPERTS = 32
TOP_K = 4
SWIGLU_ALPHA = 1.702
SWIGLU_LIMIT = 7.0
N_BRANCHES = 3

NEG_BIG = -1e30

PROJ_TM = 512
PROJ_SUB = 256
MLA_TQ = 512
MLA_ONES_ROWS = 16
SWA_TS = 512
MERGE_TM = 512
MERGE_SUB = 256
FFN_TM = 512
COMBINE_TM = 256

VMEM_LIMIT = 56 * 1024 * 1024


def _rms(x, g):
    return x * lax.rsqrt(jnp.mean(x * x, axis=-1, keepdims=True) + RMS_EPS) * g


def _store_slabs(ref, row0, value):
    rows, n, _ = ref.shape
    flat = ref.reshape(rows * n, LANES)
    for c in range(n):
        flat[pl.ds(row0 * n + c, value.shape[0], stride=n), :] = value[:, c * LANES:(c + 1) * LANES]


def _load_slabs(ref, row0=0, m=None):
    rows, n, _ = ref.shape
    m = rows if m is None else m
    flat = ref.reshape(rows * n, LANES)
    return jnp.concatenate([flat[pl.ds(row0 * n + c, m, stride=n), :] for c in range(n)], axis=1)


_A0, _A1 = 0, 512
_B0, _B1 = 512, 1280
_C0, _C1 = 1280, 1792
_D0, _D1 = 1792, 4864


def _rotate_half(x, d, lo, hi):
    n = x.shape[1]
    half = (hi - lo) // 2
    lane = lax.broadcasted_iota(I32, x.shape, 1) % d
    up = pltpu.roll(x, n - half, axis=1)
    dn = pltpu.roll(x, half, axis=1)
    return jnp.where((lane >= lo) & (lane < lo + half), -up, jnp.where((lane >= lo + half) & (lane < hi), dn, 0.0))


def _proj_kernel(x_ref, pos_ref, fq_ref, f64_ref, gmix_ref, win_ref, gq_ref, wq_ref,
                 gkv_ref, wk_ref, wv_ref, bgate_ref,
                 qm_ref, km_ref, vm_ref, qs_ref, ksa_ref, ksb_ref, vsa_ref, vsb_ref, qx_ref, gt_ref):
    for hf in range(x_ref.shape[0] // PROJ_SUB):
        rows = slice(hf * PROJ_SUB, (hf + 1) * PROJ_SUB)
        _proj_rows(x_ref[rows, :], pos_ref[:, rows], fq_ref, f64_ref, gmix_ref, win_ref, gq_ref, wq_ref, gkv_ref,
                   wk_ref, wv_ref, bgate_ref,
                   [r.at[rows] for r in (qm_ref, km_ref)], vm_ref.at[:, rows],
                   [r.at[rows] for r in (qs_ref, ksa_ref, ksb_ref, vsa_ref, vsb_ref, qx_ref, gt_ref)])


def _proj_rows(x, pos, fq_ref, f64_ref, gmix_ref, win_ref, gq_ref, wq_ref, gkv_ref, wk_ref, wv_ref, bgate_ref,
               mla_refs, vm_ref, other_refs):
    qm_ref, km_ref = mla_refs
    qs_ref, ksa_ref, ksb_ref, vsa_ref, vsb_ref, qx_ref, gt_ref = other_refs
    xn = _rms(x, gmix_ref[...]).astype(BF16)
    tm = x.shape[0]
    ang16 = fq_ref[...] * pos
    ang32 = f64_ref[...] * pos
    c16, s16, c32, s32 = jnp.cos(ang16), jnp.sin(ang16), jnp.cos(ang32), jnp.sin(ang32)
    one = jnp.ones((MLA_NOPE, tm), F32)
    zero = jnp.zeros((MLA_NOPE, tm), F32)
    pad = LANES - MLA_NOPE - MLA_ROPE
    cosq = jnp.concatenate([one, c16, c16, one[:pad]], axis=0).T
    sinq = jnp.concatenate([zero, s16, s16, zero[:pad]], axis=0).T
    cos64 = jnp.concatenate([c32, c32, c32, c32], axis=0).T
    sin64 = jnp.concatenate([s32, s32, s32, s32], axis=0).T
    rope_lo, rope_hi = MLA_NOPE, MLA_NOPE + MLA_ROPE

    xa = jnp.dot(xn, win_ref[:, _A0:_A1], preferred_element_type=F32)
    cqn = _rms(xa[:, 0:256], gq_ref[...]).astype(BF16)
    qa = jnp.dot(cqn, wq_ref[...], preferred_element_type=F32)
    qb = _rotate_half(qa, LANES, rope_lo, rope_hi)
    q_scale = LOG2E / math.sqrt(MLA_NOPE + MLA_ROPE)
    ckvn = _rms(xa[:, 256:384], gkv_ref[...]).astype(BF16)
    ka = jnp.dot(ckvn, wk_ref[...], preferred_element_type=F32)
    kr = xa[:, 384:512]
    krope = kr * cosq + _rotate_half(kr, LANES, rope_lo, rope_hi) * sinq
    for h in range(MLA_HEADS):
        sl = slice(h * LANES, (h + 1) * LANES)
        qm_ref[:, sl] = ((qa[:, sl] * cosq + qb[:, sl] * sinq) * q_scale).astype(BF16)
        km_ref[:, sl] = (ka[:, sl] + krope).astype(BF16)
    vm_ref[...] = lax.dot_general(wv_ref[...], ckvn, (((1,), (1,)), ((), ())),
                                  preferred_element_type=F32).astype(BF16)

    xb = jnp.dot(xn, win_ref[:, _B0:_B1], preferred_element_type=F32)
    s_scale = LOG2E / math.sqrt(SWA_HEAD_DIM)
    nq = SWA_HEADS * SWA_HEAD_DIM
    qs = xb[:, :nq]
    qs_rot = _rotate_half(qs, SWA_HEAD_DIM, 0, SWA_HEAD_DIM)
    for p in range(SWA_HEADS // 2):
        sl = slice(p * LANES, (p + 1) * LANES)
        qs_ref[:, sl] = ((qs[:, sl] * cos64 + qs_rot[:, sl] * sin64) * s_scale).astype(BF16)
    ks = xb[:, nq:nq + LANES]
    ks = ks * cos64 + _rotate_half(ks, SWA_HEAD_DIM, 0, SWA_HEAD_DIM) * sin64
    ksa_ref[...] = ks.astype(BF16)
    ksb_ref[...] = pltpu.roll(ks, 64, axis=1).astype(BF16)
    vs = xb[:, nq + LANES:nq + 2 * LANES]
    vsa_ref[...] = vs.astype(BF16)
    vsb_ref[...] = pltpu.roll(vs, 64, axis=1).astype(BF16)

    xc = jnp.dot(xn, win_ref[:, _C0:_C1], preferred_element_type=F32)
    qx_ref[...] = (xc * (LOG2E / math.sqrt(XA_HEAD_DIM))).astype(BF16)

    xd = jnp.dot(xn, win_ref[:, _D0:_D1], preferred_element_type=F32) + bgate_ref[...]
    gt_ref[...] = jax.nn.sigmoid(xd).astype(BF16)


def _proj_call(x2, pos, fq, f64, g_mix, w_in_al, g_q, wq, g_kv, wk_aug, wv, b_gate):
    t, d = x2.shape
    tm = PROJ_TM
    row = lambda n: pl.BlockSpec((tm, n), lambda i: (i, 0))
    full = lambda a: pl.BlockSpec(a.shape, lambda i: (0,) * a.ndim)
    out_cols = [1024, 1024, 512, 128, 128, 128, 128, 512, 3072]
    out_specs = [row(n) for n in out_cols]
    out_shape = [jax.ShapeDtypeStruct((t, n), BF16) for n in out_cols]
    vt_rows = wv.shape[0]
    out_specs.insert(2, pl.BlockSpec((vt_rows, tm), lambda i: (0, i)))
    out_shape.insert(2, jax.ShapeDtypeStruct((vt_rows, t), BF16))
    return pl.pallas_call(
        _proj_kernel,
        grid=(t // tm,),
        in_specs=[row(d), pl.BlockSpec((1, tm), lambda i: (0, i)), full(fq), full(f64), full(g_mix),
                  pl.BlockSpec(w_in_al.shape, lambda i: (0, 0), pipeline_mode=pl.Buffered(1)),
                  full(g_q), full(wq), full(g_kv), full(wk_aug), full(wv), full(b_gate)],
        out_specs=out_specs,
        out_shape=out_shape,
        compiler_params=pltpu.CompilerParams(dimension_semantics=("arbitrary",), vmem_limit_bytes=VMEM_LIMIT),
        name="proj",
    )(x2, pos, fq, f64, g_mix, w_in_al, g_q, wq, g_kv, wk_aug, wv, b_gate)


def _mla_kernel(q_ref, k_ref, vt_ref, o_ref, sa_ref, sb_ref, m_ref, acc_ref, *, tq):
    i = pl.program_id(2)
    m_ref[...] = jnp.full(m_ref.shape, NEG_BIG, F32)
    acc_ref[...] = jnp.zeros(acc_ref.shape, F32)
    ones = jnp.ones((MLA_ONES_ROWS, tq), BF16)

    def scores(j, s_ref):
        k0 = pl.multiple_of(j * tq, tq)
        for hh in range(2):
            sl = slice(hh * LANES, (hh + 1) * LANES)
            s_ref[hh, :, :tq] = lax.dot_general(k_ref[0, pl.ds(k0, tq), sl], q_ref[0, :, sl],
                                                (((1,), (1,)), ((), ())), preferred_element_type=F32)

    def update(j, s_ref, masked):
        k0 = pl.multiple_of(j * tq, tq)
        for hh in range(2):
            vt = jnp.concatenate([vt_ref[hh * MLA_V:(hh + 1) * MLA_V, pl.ds(k0, tq)], ones], axis=0)
            st = s_ref[hh, :, :tq]
            if masked:
                kj = lax.broadcasted_iota(I32, (tq, tq), 0)
                qi = lax.broadcasted_iota(I32, (tq, tq), 1)
                st = jnp.where(kj <= qi, st, NEG_BIG)
            m_old = m_ref[hh]
            m_new = jnp.maximum(m_old, jnp.max(st, axis=0, keepdims=True))
            alpha = jnp.exp2(m_old - m_new)
            pt = jnp.exp2(st - m_new)
            acc_ref[hh] = alpha * acc_ref[hh] + jnp.dot(vt, pt.astype(BF16), preferred_element_type=F32)
            m_ref[hh] = m_new

    scores(0, sa_ref)

    def body(jj, carry):
        scores(2 * jj + 1, sb_ref)
        update(2 * jj, sa_ref, False)
        scores(2 * jj + 2, sa_ref)
        update(2 * jj + 1, sb_ref, False)
        return carry

    lax.fori_loop(0, i // 2, body, 0)

    @pl.when(i % 2 == 0)
    def _():
        update(i, sa_ref, True)

    @pl.when(i % 2 == 1)
    def _():
        scores(i, sb_ref)
        update(i - 1, sa_ref, False)
        update(i, sb_ref, True)

    ot = jnp.concatenate([acc_ref[hh, :MLA_V] / acc_ref[hh, MLA_V:MLA_V + 1] for hh in range(2)],
                         axis=0)
    o_ref[0] = ot.T.astype(BF16)


def _mla_call(q, k, vt, seq):
    b, s, _ = q.shape
    assert s == seq
    tq = min(MLA_TQ, s)
    n_pairs = MLA_HEADS // 2
    return pl.pallas_call(
        functools.partial(_mla_kernel, tq=tq),
        grid=(b, n_pairs, s // tq),
        in_specs=[pl.BlockSpec((1, tq, 2 * LANES), lambda bi, hp, i: (bi, i, hp)),
                  pl.BlockSpec((1, s, 2 * LANES), lambda bi, hp, i: (bi, 0, hp)),
                  pl.BlockSpec((2 * MLA_V, s), lambda bi, hp, i: (hp, bi))],
        out_specs=pl.BlockSpec((1, tq, LANES), lambda bi, hp, i: (bi, i, hp)),
        out_shape=jax.ShapeDtypeStruct((b, s, n_pairs * LANES), BF16),
        scratch_shapes=[pltpu.VMEM((2, tq, tq + LANES), F32), pltpu.VMEM((2, tq, tq + LANES), F32),
                        pltpu.VMEM((2, 1, tq), F32),
                        pltpu.VMEM((2, MLA_V + MLA_ONES_ROWS, tq), F32)],
        compiler_params=pltpu.CompilerParams(dimension_semantics=("arbitrary",) * 3, vmem_limit_bytes=VMEM_LIMIT),
        name="mla_attn",
    )(q, k, vt)


def _swa_kernel(sink_ref, q_ref, ka_ref, kb_ref, va_ref, vb_ref, kah_ref, kbh_ref, vah_ref, vbh_ref, o_ref, *, ts):
    w = SWA_WINDOW
    i = pl.program_id(1)
    ka = jnp.concatenate([kah_ref[0], ka_ref[0]], axis=0)
    kb = jnp.concatenate([kbh_ref[0], kb_ref[0]], axis=0)
    va = jnp.concatenate([vah_ref[0], va_ref[0]], axis=0)
    vb = jnp.concatenate([vbh_ref[0], vb_ref[0]], axis=0)
    lane_k = lax.broadcasted_iota(I32, (2 * w, LANES), 1)
    low = lane_k < SWA_HEAD_DIM
    qi = lax.broadcasted_iota(I32, (2 * w, 2 * w), 0) % w
    kj = lax.broadcasted_iota(I32, (2 * w, 2 * w), 1)
    diff = qi + w - kj
    band = (diff >= 0) & (diff < w)
    lane_o = lax.broadcasted_iota(I32, (w, LANES), 1)
    row2 = lax.broadcasted_iota(I32, (2 * w, 1), 0)
    zero = jnp.zeros((), BF16)
    stacks = ((0, ka, True, va), (1, kb, False, vb), (4, kb, True, vb), (5, ka, False, va))
    for n in range(ts // w):
        mask = band & ((i * (ts // w) + n > 0) | (kj >= w))
        res = []
        for h0, ksrc, keep_low, vsrc in stacks:
            p0 = h0 // 2
            q = jnp.concatenate([q_ref[0, n * w:(n + 1) * w, p0 * LANES:(p0 + 1) * LANES],
                                 q_ref[0, n * w:(n + 1) * w, (p0 + 1) * LANES:(p0 + 2) * LANES]], axis=0)
            kwin = ksrc[n * w:n * w + 2 * w]
            kwin = jnp.where(low if keep_low else ~low, kwin, zero)
            vwin = vsrc[n * w:n * w + 2 * w]
            s = lax.dot_general(q, kwin, (((1,), (1,)), ((), ())), preferred_element_type=F32)
            s = jnp.where(mask, s, NEG_BIG)
            sink = jnp.where(row2 < w, sink_ref[h0], sink_ref[h0 + 2]) * LOG2E
            m = jnp.maximum(jnp.max(s, axis=1, keepdims=True), sink)
            p = jnp.exp2(s - m)
            den = jnp.sum(p, axis=1, keepdims=True) + jnp.exp2(sink - m)
            o = jnp.dot(p.astype(BF16), vwin, preferred_element_type=F32) / den
            res.append(o)
        o02, o13, o46, o57 = res
        sel = lane_o < SWA_HEAD_DIM
        rows = slice(n * w, (n + 1) * w)
        o_ref[0, rows, 0 * LANES:1 * LANES] = jnp.where(sel, o02[:w], o13[:w]).astype(BF16)
        o_ref[0, rows, 1 * LANES:2 * LANES] = jnp.where(sel, o02[w:], o13[w:]).astype(BF16)
        o_ref[0, rows, 2 * LANES:3 * LANES] = jnp.where(sel, o46[:w], o57[:w]).astype(BF16)
        o_ref[0, rows, 3 * LANES:4 * LANES] = jnp.where(sel, o46[w:], o57[w:]).astype(BF16)


def _swa_call(sinks, q, ksa, ksb, vsa, vsb):
    b, s, _ = q.shape
    ts = min(SWA_TS, s)
    w = SWA_WINDOW
    r = ts // w
    main = pl.BlockSpec((1, ts, LANES), lambda bi, i: (bi, i, 0))
    halo = pl.BlockSpec((1, w, LANES), lambda bi, i: (bi, jnp.maximum(i * r - 1, 0), 0))
    return pl.pallas_call(
        functools.partial(_swa_kernel, ts=ts),
        grid=(b, s // ts),
        in_specs=[pl.BlockSpec(memory_space=pltpu.SMEM),
                  pl.BlockSpec((1, ts, 4 * LANES), lambda bi, i: (bi, i, 0)),
                  main, main, main, main, halo, halo, halo, halo],
        out_specs=pl.BlockSpec((1, ts, 4 * LANES), lambda bi, i: (bi, i, 0)),
        out_shape=jax.ShapeDtypeStruct((b, s, 4 * LANES), BF16),
        compiler_params=pltpu.CompilerParams(dimension_semantics=("arbitrary",) * 2, vmem_limit_bytes=VMEM_LIMIT),
        name="swa_attn",
    )(sinks, q, ksa, ksb, vsa, vsb, ksa, ksb, vsa, vsb)


def _memkv_kernel(mem_ref, g_ref, w_ref, o_ref):
    mn = _rms(mem_ref[...], g_ref[...]).astype(BF16)
    o_ref[...] = jnp.dot(mn, w_ref[...], preferred_element_type=F32).astype(BF16)


def _memkv_call(mem2, g_mem, w_mem_kv):
    n, d = mem2.shape
    tm = min(256, n)
    return pl.pallas_call(
        _memkv_kernel,
        grid=(n // tm,),
        in_specs=[pl.BlockSpec((tm, d), lambda i: (i, 0)),
                  pl.BlockSpec(g_mem.shape, lambda i: (0, 0)),
                  pl.BlockSpec(w_mem_kv.shape, lambda i: (0, 0))],
        out_specs=pl.BlockSpec((tm, w_mem_kv.shape[1]), lambda i: (i, 0)),
        out_shape=jax.ShapeDtypeStruct((n, w_mem_kv.shape[1]), BF16),
        compiler_params=pltpu.CompilerParams(dimension_semantics=("arbitrary",), vmem_limit_bytes=VMEM_LIMIT),
        name="mem_kv",
    )(mem2, g_mem, w_mem_kv)


def _merge_kernel(x_ref, omla_ref, oswa_ref, qx_ref, gt_ref, kvm_ref, wmo_ref, wso_ref, wxo_ref, wout_ref,
                  gffn_ref, wr_ref, br_ref,
                  h_ref, hnp_ref, idx_ref, gate_ref, rank_ref, cnt_ref, run_ref, *, tm, sub):
    @pl.when(pl.program_id(0) == 0)
    def _():
        run_ref[...] = jnp.zeros(run_ref.shape, F32)

    d = x_ref.shape[1]
    kv_cols = XA_HEADS * XA_HEAD_DIM
    erow = lax.broadcasted_iota(I32, (N_EXPERTS, sub), 0)
    tri_t = (lax.broadcasted_iota(I32, (sub, sub), 0) < lax.broadcasted_iota(I32, (sub, sub), 1)).astype(BF16)
    nt = (((1,), (1,)), ((), ()))
    run = run_ref[...]
    for hf in range(tm // sub):
        rows = slice(hf * sub, (hf + 1) * sub)

        oxs = []
        for hd in range(XA_HEADS):
            sl = slice(hd * LANES, (hd + 1) * LANES)
            km = kvm_ref[0, :, sl]
            vm = kvm_ref[0, :, kv_cols + hd * LANES:kv_cols + (hd + 1) * LANES]
            s = lax.dot_general(qx_ref[rows, sl], km, nt, preferred_element_type=F32)
            p = jnp.exp2(s - jnp.max(s, axis=1, keepdims=True))
            den = jnp.sum(p, axis=1, keepdims=True)
            oxs.append((jnp.dot(p.astype(BF16), vm, preferred_element_type=F32) / den).astype(BF16))
        oxa = jnp.concatenate(oxs, axis=1)

        merged = (gt_ref[rows, 0:d].astype(F32) * jnp.dot(omla_ref[rows, :], wmo_ref[...], preferred_element_type=F32)
                  + gt_ref[rows, d:2 * d].astype(F32) * jnp.dot(oswa_ref[rows, :], wso_ref[...],
                                                                 preferred_element_type=F32)
                  + gt_ref[rows, 2 * d:3 * d].astype(F32) * jnp.dot(oxa, wxo_ref[...], preferred_element_type=F32))
        h = x_ref[rows, :] + jnp.dot(merged.astype(BF16), wout_ref[...], preferred_element_type=F32)
        h_ref[rows, :] = h

        hn = _rms(h, gffn_ref[...])
        hn_hi = hn.astype(BF16)
        hn_hi32 = hn_hi.astype(F32)
        hn_lo = (hn - hn_hi32).astype(BF16)
        bits = pltpu.bitcast(hn_hi32, U32)
        _store_slabs(hnp_ref, hf * sub, (bits[:, : d // 2] >> 16) | (bits[:, d // 2:] & jnp.uint32(0xFFFF0000)))

        part = lax.dot_general(wr_ref[...], hn_hi, nt, preferred_element_type=F32)
        logits_t = (part[:N_EXPERTS] + part[N_EXPERTS:]
                    + lax.dot_general(wr_ref[0:N_EXPERTS, :], hn_lo, nt, preferred_element_type=F32) + br_ref[...])

        work = logits_t
        vals, idxs, hots = [], [], []
        for _ in range(TOP_K):
            mx = jnp.max(work, axis=0, keepdims=True)
            ix = jnp.min(jnp.where(work == mx, erow, N_EXPERTS), axis=0, keepdims=True)
            hot = erow == ix
            work = jnp.where(hot, -jnp.inf, work)
            vals.append(mx)
            idxs.append(ix)
            hots.append(hot)
        es = [jnp.exp(v - vals[0]) for v in vals]
        den = es[0] + es[1] + es[2] + es[3]
        sel_t = (hots[0] | hots[1] | hots[2] | hots[3])
        prefix_t = jnp.dot(sel_t.astype(BF16), tri_t, preferred_element_type=F32) + run
        for k in range(TOP_K):
            idx_ref[k:k + 1, rows] = idxs[k]
            gate_ref[k:k + 1, rows] = es[k] / den
            rank_ref[k:k + 1, rows] = jnp.sum(jnp.where(hots[k], prefix_t, 0.0), axis=0, keepdims=True).astype(I32)
        run = run + jnp.sum(sel_t.astype(F32), axis=1, keepdims=True)
    run_ref[...] = run
    cnt_ref[...] = run.astype(I32)


def _merge_call(x2, omla, oswa, qx, gates, kvm, wmo, wso, wxo, wout, g_ffn, wr_split, b_router_col, seq):
    t, d = x2.shape
    tm = MERGE_TM
    per_b = seq // tm
    row = lambda n: pl.BlockSpec((tm, n), lambda i: (i, 0))
    col = lambda: pl.BlockSpec((TOP_K, tm), lambda i: (0, i))
    full = lambda a: pl.BlockSpec(a.shape, lambda i: (0,) * a.ndim)
    return pl.pallas_call(
        functools.partial(_merge_kernel, tm=tm, sub=MERGE_SUB),
        grid=(t // tm,),
        in_specs=[row(d), row(512), row(512), row(512), row(3 * d),
                  pl.BlockSpec((1,) + kvm.shape[1:], lambda i: (i // per_b, 0, 0)),
                  full(wmo), full(wso), full(wxo), full(wout), full(g_ffn), full(wr_split), full(b_router_col)],
        out_specs=[row(d), pl.BlockSpec((tm, d // 2 // LANES, LANES), lambda i: (i, 0, 0)), col(), col(), col(),
                   pl.BlockSpec((N_EXPERTS, 1), lambda i: (0, 0))],
        out_shape=[jax.ShapeDtypeStruct((t, d), F32), jax.ShapeDtypeStruct((t, d // 2 // LANES, LANES), U32),
                   jax.ShapeDtypeStruct((TOP_K, t), I32), jax.ShapeDtypeStruct((TOP_K, t), F32),
                   jax.ShapeDtypeStruct((TOP_K, t), I32), jax.ShapeDtypeStruct((N_EXPERTS, 1), I32)],
        scratch_shapes=[pltpu.VMEM((N_EXPERTS, 1), F32)],
        compiler_params=pltpu.CompilerParams(dimension_semantics=("arbitrary",), vmem_limit_bytes=VMEM_LIMIT),
        name="merge_router",
    )(x2, omla, oswa, qx, gates, kvm, wmo, wso, wxo, wout, g_ffn, wr_split, b_router_col)


SC_CORES = 2
SC_SUBCORES = 16
SC_WORKERS = SC_CORES * SC_SUBCORES
SC_CHUNK = 64


def _sc_mesh():
    return plsc.VectorSubcoreMesh(core_axis_name="c", subcore_axis_name="s",
                                  num_cores=SC_CORES, num_subcores=SC_SUBCORES)


def _sc_index_blocks(idx):
    n = idx.shape[0]
    per_w = n // SC_WORKERS
    n_ch = per_w // SC_CHUNK
    assert per_w * SC_WORKERS == n and n_ch * SC_CHUNK == per_w
    return idx.reshape(SC_WORKERS, n_ch, SC_CHUNK), per_w, n_ch


def _sc_two_buffer_loop(n_ch, load, store, bufs, load_sems, store_sems):
    assert n_ch % 2 == 0
    a, b = bufs
    la, lb = load_sems
    sa, sb = store_sems
    load(0, a, la).start()

    @pl.loop(0, n_ch, step=2)
    def _(j):
        load(j, a, la).wait()

        @pl.when(j > 0)
        def _():
            store(j - 1, b, sb).wait()

        load(j + 1, b, lb).start()
        store(j, a, sa).start()
        load(j + 1, b, lb).wait()
        store(j, a, sa).wait()

        @pl.when(j + 2 < n_ch)
        def _():
            load(j + 2, a, la).start()

        store(j + 1, b, sb).start()

    store(n_ch - 1, b, sb).wait()


def _sc_scratch(n_ch, row_shape, dtype):
    return [pltpu.VMEM((n_ch, SC_CHUNK), I32), pltpu.VMEM((SC_CHUNK,) + row_shape, dtype),
            pltpu.VMEM((SC_CHUNK,) + row_shape, dtype)] + [pltpu.SemaphoreType.DMA] * 4


def _sc_scatter_rows(src, idx, n_out):
    idx3, per_w, n_ch = _sc_index_blocks(idx)
    n_src = src.shape[0]
    assert n_src % per_w == 0

    @functools.partial(
        pl.kernel, mesh=_sc_mesh(),
        out_type=jax.ShapeDtypeStruct((n_out,) + src.shape[1:], src.dtype),
        scratch_types=_sc_scratch(n_ch, src.shape[1:], src.dtype),
        name="moe_dispatch_sc")
    def k(src_hbm, idx_hbm, out_hbm, idx_v, rows_a, rows_b, la, lb, sa, sb):
        wid = lax.axis_index("s") * SC_CORES + lax.axis_index("c")
        base = lax.rem(wid * per_w, n_src)
        pltpu.sync_copy(idx_hbm.at[wid], idx_v)

        def load(c, buf, sem):
            return pltpu.make_async_copy(src_hbm.at[pl.ds(base + c * SC_CHUNK, SC_CHUNK)], buf, sem)

        def store(c, buf, sem):
            return pltpu.make_async_copy(buf, out_hbm.at[idx_v.at[c]], sem)

        _sc_two_buffer_loop(n_ch, load, store, (rows_a, rows_b), (la, lb), (sa, sb))

    return k(src, idx3)


def _sc_gather_rows(table, idx):
    idx3, per_w, n_ch = _sc_index_blocks(idx)

    @functools.partial(
        pl.kernel, mesh=_sc_mesh(),
        out_type=jax.ShapeDtypeStruct((idx.shape[0],) + table.shape[1:], table.dtype),
        scratch_types=_sc_scratch(n_ch, table.shape[1:], table.dtype),
        name="moe_gather_sc")
    def k(table_hbm, idx_hbm, out_hbm, idx_v, rows_a, rows_b, la, lb, sa, sb):
        wid = lax.axis_index("s") * SC_CORES + lax.axis_index("c")
        base = wid * per_w
        pltpu.sync_copy(idx_hbm.at[wid], idx_v)

        def load(c, buf, sem):
            return pltpu.make_async_copy(table_hbm.at[idx_v.at[c]], buf, sem)

        def store(c, buf, sem):
            return pltpu.make_async_copy(buf, out_hbm.at[pl.ds(base + c * SC_CHUNK, SC_CHUNK)], sem)

        _sc_two_buffer_loop(n_ch, load, store, (rows_a, rows_b), (la, lb), (sa, sb))

    return k(table, idx3)


def _unpack_lo(w):
    return pltpu.bitcast(w << 16, F32)


def _unpack_hi(w):
    return pltpu.bitcast(w & jnp.uint32(0xFFFF0000), F32)


def _ffn_kernel(te_ref, xs_ref, wgu_hbm, bgu_ref, wd_hbm, bd_ref, y_ref, wgu_f32, wd_f32, wgu_bf, wd_bf, wsem):
    i = pl.program_id(0)
    n = pl.num_programs(0)
    n_used = te_ref[n]
    e = te_ref[i]
    first = (i == 0) | (e != te_ref[jnp.maximum(i - 1, 0)])
    slot = te_ref[2 * n + 1 + i] % 2
    nxt = te_ref[3 * n + 1 + i]

    def fetch(expert, s):
        return (pltpu.make_async_copy(wgu_hbm.at[expert], wgu_f32.at[s], wsem.at[0, s]),
                pltpu.make_async_copy(wd_hbm.at[expert], wd_f32.at[s], wsem.at[1, s]))

    @pl.when(i == 0)
    def _():
        for cp in fetch(e, slot):
            cp.start()

    @pl.when((i < n_used) & first)
    def _():
        for cp in fetch(e, slot):
            cp.wait()
        wgu_bf[...] = wgu_f32[slot].astype(BF16)
        wd_bf[...] = wd_f32[slot].astype(BF16)

        @pl.when(nxt >= 0)
        def _():
            for cp in fetch(nxt, 1 - slot):
                cp.start()

    @pl.when(i < n_used)
    def _():
        w = _load_slabs(xs_ref)
        half = w.shape[1]
        valid = te_ref[n + 1 + i]
        w = jnp.where(lax.broadcasted_iota(I32, w.shape, 0) < valid, w, jnp.uint32(0))
        x_lo = _unpack_lo(w).astype(BF16)
        x_hi = _unpack_hi(w).astype(BF16)
        gu = (jnp.dot(x_lo, wgu_bf[0:half, :], preferred_element_type=F32)
              + jnp.dot(x_hi, wgu_bf[half:, :], preferred_element_type=F32) + bgu_ref[0])
        de = gu.shape[1] // 2
        x_glu = jnp.minimum(gu[:, :de], SWIGLU_LIMIT)
        x_lin = jnp.clip(gu[:, de:], -SWIGLU_LIMIT, SWIGLU_LIMIT)
        hdn = x_glu * jax.nn.sigmoid(SWIGLU_ALPHA * x_glu) * (x_lin + 1.0)
        y = jnp.dot(hdn.astype(BF16), wd_bf[...], preferred_element_type=F32) + bd_ref[0]
        bits = pltpu.bitcast(y.astype(BF16).astype(F32), U32)
        _store_slabs(y_ref, 0, (bits[:, :half] >> 16) | (bits[:, half:] & jnp.uint32(0xFFFF0000)))

    @pl.when(i >= n_used)
    def _():
        y_ref[...] = jnp.zeros(y_ref.shape, U32)


def _ffn_call(tile_table, xs, w_gate_up, b_gate_up, w_down, b_down):
    r, ns, lanes = xs.shape
    tm = FFN_TM
    ne, d, de2 = w_gate_up.shape
    return pl.pallas_call(
        _ffn_kernel,
        grid_spec=pltpu.PrefetchScalarGridSpec(
            num_scalar_prefetch=1,
            grid=(r // tm,),
            in_specs=[pl.BlockSpec((tm, ns, lanes),
                                   lambda i, te: (jnp.minimum(i, jnp.maximum(te[r // tm] - 1, 0)), 0, 0)),
                      pl.BlockSpec(memory_space=pl.ANY),
                      pl.BlockSpec((1, 1, de2), lambda i, te: (te[i], 0, 0)),
                      pl.BlockSpec(memory_space=pl.ANY),
                      pl.BlockSpec((1, 1, d), lambda i, te: (te[i], 0, 0))],
            out_specs=pl.BlockSpec((tm, ns, lanes), lambda i, te: (i, 0, 0)),
            scratch_shapes=[pltpu.VMEM((2, d, de2), F32), pltpu.VMEM((2, de2 // 2, d), F32),
                            pltpu.VMEM((d, de2), BF16), pltpu.VMEM((de2 // 2, d), BF16),
                            pltpu.SemaphoreType.DMA((2, 2))]),
        out_shape=jax.ShapeDtypeStruct((r, ns, lanes), U32),
        compiler_params=pltpu.CompilerParams(dimension_semantics=("arbitrary",), vmem_limit_bytes=VMEM_LIMIT),
        name="moe_ffn",
    )(tile_table, xs, w_gate_up, b_gate_up, w_down, b_down)


def _combine_dense_kernel(h_ref, gate_ref, gfin_ref, y0_ref, y1_ref, y2_ref, y3_ref, o_ref):
    half = y0_ref.shape[1] * y0_ref.shape[2]
    lo = h_ref[:, :half]
    hi = h_ref[:, half:]
    for k, y_ref in enumerate((y0_ref, y1_ref, y2_ref, y3_ref)):
        g = gate_ref[:, k:k + 1]
        w = _load_slabs(y_ref)
        lo = lo + g * _unpack_lo(w)
        hi = hi + g * _unpack_hi(w)
    ms = (jnp.sum(lo * lo, axis=1, keepdims=True) + jnp.sum(hi * hi, axis=1, keepdims=True)) / (2 * half)
    inv = lax.rsqrt(ms + RMS_EPS)
    o_ref[:, :half] = lo * inv * gfin_ref[:, :half]
    o_ref[:, half:] = hi * inv * gfin_ref[:, half:]


def _combine_dense_call(h, gate, g_final, yk):
    t, d = h.shape
    tm = COMBINE_TM
    per_k = t // tm
    yspec = lambda k: pl.BlockSpec((tm,) + yk.shape[1:], lambda i: (k * per_k + i, 0, 0))
    return pl.pallas_call(
        _combine_dense_kernel,
        grid=(t // tm,),
        in_specs=[pl.BlockSpec((tm, d), lambda i: (i, 0)),
                  pl.BlockSpec((tm, TOP_K), lambda i: (i, 0)),
                  pl.BlockSpec((1, d), lambda i: (0, 0)),
                  yspec(0), yspec(1), yspec(2), yspec(3)],
        out_specs=pl.BlockSpec((tm, d), lambda i: (i, 0)),
        out_shape=jax.ShapeDtypeStruct((t, d), F32),
        compiler_params=pltpu.CompilerParams(dimension_semantics=("arbitrary",), vmem_limit_bytes=VMEM_LIMIT),
        name="moe_combine",
    )(h, gate, g_final, yk, yk, yk, yk)


def _rope_freqs():
    def inv_freq(dh):
        return (ROPE_THETA ** (-jnp.arange(0, dh, 2, dtype=F32) / dh))[:, None]

    return inv_freq(MLA_ROPE), inv_freq(SWA_HEAD_DIM)


def _winprep_kernel(w_ref, o_ref):
    w = w_ref[...]
    rb = w.shape[0]
    c1 = MLA_Q_RANK + MLA_KV_RANK
    tail = w[:, c1 + MLA_ROPE:]
    lane = lax.broadcasted_iota(I32, (rb, LANES), 1)
    in_rope = (lane >= MLA_NOPE) & (lane < MLA_NOPE + MLA_ROPE)
    kr_p = jnp.where(in_rope, pltpu.roll(w[:, c1:c1 + LANES], MLA_NOPE, axis=1), 0.0)
    pieces = [w[:, :c1], kr_p, tail]
    off = 0
    for pc in pieces:
        o_ref[:, off:off + pc.shape[1]] = pc.astype(BF16)
        off += pc.shape[1]


def _winprep_call(w_in, layer):
    _, d, n = w_in.shape
    rb = 128
    return pl.pallas_call(
        _winprep_kernel,
        grid=(d // rb,),
        in_specs=[pl.BlockSpec((None, rb, n), lambda i: (layer, i, 0))],
        out_specs=pl.BlockSpec((rb, _D1), lambda i: (i, 0)),
        out_shape=jax.ShapeDtypeStruct((d, _D1), BF16),
        compiler_params=pltpu.CompilerParams(dimension_semantics=("arbitrary",), vmem_limit_bytes=VMEM_LIMIT),
        name="w_in_prep",
    )(w_in)


def _prep_weights(w_in, layer, w_mla_uq, w_mla_ukv):
    w_in_al = _winprep_call(w_in, layer)

    r = w_mla_uq.shape[0]
    wq = w_mla_uq.reshape(r, MLA_HEADS, MLA_NOPE + MLA_ROPE)
    zq = jnp.zeros((r, MLA_HEADS, LANES - MLA_NOPE - MLA_ROPE), w_mla_uq.dtype)
    wq_pad = jnp.concatenate([wq, zq], axis=-1).reshape(r, MLA_HEADS * LANES).astype(BF16)

    rk = w_mla_ukv.shape[0]
    wkv = w_mla_ukv.reshape(rk, MLA_HEADS, MLA_NOPE + MLA_V)
    wk_aug = jnp.concatenate([wkv[..., :MLA_NOPE], jnp.zeros((rk, MLA_HEADS, LANES - MLA_NOPE), w_mla_ukv.dtype)],
                             axis=-1).reshape(rk, MLA_HEADS * LANES).astype(BF16)
    wv_t = wkv[..., MLA_NOPE:].reshape(rk, MLA_HEADS * MLA_V).T.astype(BF16)
    return w_in_al, wq_pad, wk_aug, wv_t


def kernel(x, mem, positions, g_mix, w_in, g_mla_q, w_mla_uq, g_mla_kv, w_mla_ukv, w_mla_o, swa_sinks, w_swa_o,
           g_mem, w_mem_kv, w_xa_o, b_gate, w_out, g_ffn, w_router, b_router, w_gate_up, b_gate_up, w_down,
           b_down, g_final):
    b, s, d = x.shape
    t = b * s
    depth = g_mix.shape[0]
    h = x.reshape(t, d)
    pos = positions.astype(F32).reshape(1, t)
    fq, f64 = _rope_freqs()
    for l in range(depth):
        w_in_al, wq_pad, wk_aug, wv_t = _prep_weights(w_in, l, w_mla_uq[l], w_mla_ukv[l])
        (qm, km, vmt, qs, ksa, ksb, vsa, vsb, qx, gates) = _proj_call(
            h, pos, fq, f64, g_mix[l][None], w_in_al, g_mla_q[l][None], wq_pad,
            g_mla_kv[l][None], wk_aug, wv_t, b_gate[l][None])
        r3 = lambda a: a.reshape(b, s, a.shape[1])
        omla = _mla_call(r3(qm), r3(km), vmt, s).reshape(t, -1)
        oswa = _swa_call(swa_sinks[l], r3(qs), r3(ksa), r3(ksb), r3(vsa), r3(vsb)).reshape(t, -1)
        m = mem.shape[1]
        kvm = _memkv_call(mem.reshape(b * m, d), g_mem[l][None], w_mem_kv[l].astype(BF16)).reshape(b, m, -1)
        wr_t = w_router[l].T
        wr_hi = wr_t.astype(BF16)
        wr_split = jnp.concatenate([wr_hi, (wr_t - wr_hi.astype(F32)).astype(BF16)], axis=0)
        h_mid, hnp, idx, gate, rank, counts = _merge_call(
            h, omla, oswa, qx, gates, kvm, w_mla_o[l].astype(BF16), w_swa_o[l].astype(BF16),
            w_xa_o[l].astype(BF16), w_out[l].astype(BF16), g_ffn[l][None], wr_split, b_router[l][:, None], s)

        counts = counts[:, 0]
        padded = ((counts + FFN_TM - 1) // FFN_TM) * FFN_TM
        padded_end = jnp.cumsum(padded)
        offsets = padded_end - padded
        experts = jnp.arange(N_EXPERTS, dtype=I32)
        dest = (jnp.sum(jnp.where(idx[..., None] == experts, offsets, 0), axis=-1) + rank).reshape(-1).astype(I32)
        n_tiles = (t * TOP_K) // FFN_TM + N_EXPERTS
        n_used = (padded_end[-1] // FFN_TM).astype(I32)
        tile_start = jnp.minimum(jnp.arange(n_tiles, dtype=I32), jnp.maximum(n_used - 1, 0)) * FFN_TM
        tile_expert = jnp.sum((padded_end[None, :] <= tile_start[:, None]).astype(I32), axis=1)
        tile_expert = jnp.minimum(tile_expert, N_EXPERTS - 1)
        onehot = tile_expert[:, None] == experts[None, :]
        pick = lambda v: jnp.sum(jnp.where(onehot, v[None, :], 0), axis=1)
        tile_valid = jnp.clip(pick(counts) - (tile_start - pick(offsets)), 0, FFN_TM)
        nonempty = padded > 0
        tile_group = pick(jnp.cumsum(nonempty.astype(I32)) - 1)
        later = jnp.where(nonempty[None, :] & (experts[None, :] > experts[:, None]), experts[None, :], N_EXPERTS)
        next_expert = jnp.min(later, axis=1)
        tile_next = pick(jnp.where(next_expert < N_EXPERTS, next_expert, -1))
        te = jnp.concatenate([tile_expert, n_used[None], tile_valid.astype(I32), tile_group.astype(I32),
                              tile_next.astype(I32)])

        xs = _sc_scatter_rows(hnp, dest, n_tiles * FFN_TM)
        y = _ffn_call(te, xs, w_gate_up[l], b_gate_up[l][:, None, :], w_down[l], b_down[l][:, None, :])
        yk = _sc_gather_rows(y, dest)
        if l == depth - 1:
            gfin = g_final[None]
            out = _combine_dense_call(h_mid, gate.T, gfin, yk)
        else:
            raise NotImplementedError("depth > 1 needs a combine without the final norm")
        h = out
    return h.reshape(b, s, d)
```

```python
import functools
import math

import jax
import jax.numpy as jnp
from jax import lax
from jax.experimental import pallas as pl
from jax.experimental.pallas import tpu as pltpu
from jax.experimental.pallas import tpu_sc as plsc

F32 = jnp.float32
BF16 = jnp.bfloat16
U32 = jnp.uint32
I32 = jnp.int32

LANES = 128
ROPE_THETA = 10000.0
RMS_EPS = 1e-6
LOG2E = 1.4426950408889634

MLA_HEADS = 8
MLA_NOPE = 64
MLA_ROPE = 32
MLA_V = 64
MLA_Q_RANK = 256
MLA_KV_RANK = 128
SWA_HEADS = 8
SWA_KV_HEADS = 2
SWA_HEAD_DIM = 64
SWA_WINDOW = 128
XA_HEADS = 4
XA_HEAD_DIM = 128
N_EXPERTS = 32
TOP_K = 4
SWIGLU_ALPHA = 1.702
SWIGLU_LIMIT = 7.0
N_BRANCHES = 3

NEG_BIG = -1e30

PROJ_TM = 512
PROJ_SUB = 256
MLA_TQ = 512
MLA_ONES_ROWS = 16
SWA_TS = 512
MERGE_TM = 512
MERGE_SUB = 256
FFN_TM = 512
COMBINE_TM = 256

VMEM_LIMIT = 56 * 1024 * 1024


def _rms(x, g):
    return x * lax.rsqrt(jnp.mean(x * x, axis=-1, keepdims=True) + RMS_EPS) * g


def _store_slabs(ref, row0, value):
    rows, n, _ = ref.shape
    flat = ref.reshape(rows * n, LANES)
    for c in range(n):
        flat[pl.ds(row0 * n + c, value.shape[0], stride=n), :] = value[:, c * LANES:(c + 1) * LANES]


def _load_slabs(ref, row0=0, m=None):
    rows, n, _ = ref.shape
    m = rows if m is None else m
    flat = ref.reshape(rows * n, LANES)
    return jnp.concatenate([flat[pl.ds(row0 * n + c, m, stride=n), :] for c in range(n)], axis=1)


_A0, _A1 = 0, 512
_B0, _B1 = 512, 1280
_C0, _C1 = 1280, 1792
_D0, _D1 = 1792, 4864


def _rotate_half(x, d, lo, hi):
    n = x.shape[1]
    half = (hi - lo) // 2
    lane = lax.broadcasted_iota(I32, x.shape, 1) % d
    up = pltpu.roll(x, n - half, axis=1)
    dn = pltpu.roll(x, half, axis=1)
    return jnp.where((lane >= lo) & (lane < lo + half), -up, jnp.where((lane >= lo + half) & (lane < hi), dn, 0.0))


def _proj_kernel(x_ref, pos_ref, fq_ref, f64_ref, gmix_ref, win_ref, gq_ref, wq_ref,
                 gkv_ref, wk_ref, wv_ref, bgate_ref,
                 qm_ref, km_ref, vm_ref, qs_ref, ksa_ref, ksb_ref, vsa_ref, vsb_ref, qx_ref, gt_ref):
    for hf in range(x_ref.shape[0] // PROJ_SUB):
        rows = slice(hf * PROJ_SUB, (hf + 1) * PROJ_SUB)
        _proj_rows(x_ref[rows, :], pos_ref[:, rows], fq_ref, f64_ref, gmix_ref, win_ref, gq_ref, wq_ref, gkv_ref,
                   wk_ref, wv_ref, bgate_ref,
                   [r.at[rows] for r in (qm_ref, km_ref)], vm_ref.at[:, rows],
                   [r.at[rows] for r in (qs_ref, ksa_ref, ksb_ref, vsa_ref, vsb_ref, qx_ref, gt_ref)])


def _proj_rows(x, pos, fq_ref, f64_ref, gmix_ref, win_ref, gq_ref, wq_ref, gkv_ref, wk_ref, wv_ref, bgate_ref,
               mla_refs, vm_ref, other_refs):
    qm_ref, km_ref = mla_refs
    qs_ref, ksa_ref, ksb_ref, vsa_ref, vsb_ref, qx_ref, gt_ref = other_refs
    xn = _rms(x, gmix_ref[...]).astype(BF16)
    tm = x.shape[0]
    ang16 = fq_ref[...] * pos
    ang32 = f64_ref[...] * pos
    c16, s16, c32, s32 = jnp.cos(ang16), jnp.sin(ang16), jnp.cos(ang32), jnp.sin(ang32)
    one = jnp.ones((MLA_NOPE, tm), F32)
    zero = jnp.zeros((MLA_NOPE, tm), F32)
    pad = LANES - MLA_NOPE - MLA_ROPE
    cosq = jnp.concatenate([one, c16, c16, one[:pad]], axis=0).T
    sinq = jnp.concatenate([zero, s16, s16, zero[:pad]], axis=0).T
    cos64 = jnp.concatenate([c32, c32, c32, c32], axis=0).T
    sin64 = jnp.concatenate([s32, s32, s32, s32], axis=0).T
    rope_lo, rope_hi = MLA_NOPE, MLA_NOPE + MLA_ROPE

    xa = jnp.dot(xn, win_ref[:, _A0:_A1], preferred_element_type=F32)
    cqn = _rms(xa[:, 0:256], gq_ref[...]).astype(BF16)
    qa = jnp.dot(cqn, wq_ref[...], preferred_element_type=F32)
    qb = _rotate_half(qa, LANES, rope_lo, rope_hi)
    q_scale = LOG2E / math.sqrt(MLA_NOPE + MLA_ROPE)
    ckvn = _rms(xa[:, 256:384], gkv_ref[...]).astype(BF16)
    ka = jnp.dot(ckvn, wk_ref[...], preferred_element_type=F32)
    kr = xa[:, 384:512]
    krope = kr * cosq + _rotate_half(kr, LANES, rope_lo, rope_hi) * sinq
    for h in range(MLA_HEADS):
        sl = slice(h * LANES, (h + 1) * LANES)
        qm_ref[:, sl] = ((qa[:, sl] * cosq + qb[:, sl] * sinq) * q_scale).astype(BF16)
        km_ref[:, sl] = (ka[:, sl] + krope).astype(BF16)
    vm_ref[...] = lax.dot_general(wv_ref[...], ckvn, (((1,), (1,)), ((), ())),
                                  preferred_element_type=F32).astype(BF16)

    xb = jnp.dot(xn, win_ref[:, _B0:_B1], preferred_element_type=F32)
    s_scale = LOG2E / math.sqrt(SWA_HEAD_DIM)
    nq = SWA_HEADS * SWA_HEAD_DIM
    qs = xb[:, :nq]
    qs_rot = _rotate_half(qs, SWA_HEAD_DIM, 0, SWA_HEAD_DIM)
    for p in range(SWA_HEADS // 2):
        sl = slice(p * LANES, (p + 1) * LANES)
        qs_ref[:, sl] = ((qs[:, sl] * cos64 + qs_rot[:, sl] * sin64) * s_scale).astype(BF16)
    ks = xb[:, nq:nq + LANES]
    ks = ks * cos64 + _rotate_half(ks, SWA_HEAD_DIM, 0, SWA_HEAD_DIM) * sin64
    ksa_ref[...] = ks.astype(BF16)
    ksb_ref[...] = pltpu.roll(ks, 64, axis=1).astype(BF16)
    vs = xb[:, nq + LANES:nq + 2 * LANES]
    vsa_ref[...] = vs.astype(BF16)
    vsb_ref[...] = pltpu.roll(vs, 64, axis=1).astype(BF16)

    xc = jnp.dot(xn, win_ref[:, _C0:_C1], preferred_element_type=F32)
    qx_ref[...] = (xc * (LOG2E / math.sqrt(XA_HEAD_DIM))).astype(BF16)

    xd = jnp.dot(xn, win_ref[:, _D0:_D1], preferred_element_type=F32) + bgate_ref[...]
    gt_ref[...] = jax.nn.sigmoid(xd).astype(BF16)


def _proj_call(x2, pos, fq, f64, g_mix, w_in_al, g_q, wq, g_kv, wk_aug, wv, b_gate):
    t, d = x2.shape
    tm = PROJ_TM
    row = lambda n: pl.BlockSpec((tm, n), lambda i: (i, 0))
    full = lambda a: pl.BlockSpec(a.shape, lambda i: (0,) * a.ndim)
    out_cols = [1024, 1024, 512, 128, 128, 128, 128, 512, 3072]
    out_specs = [row(n) for n in out_cols]
    out_shape = [jax.ShapeDtypeStruct((t, n), BF16) for n in out_cols]
    vt_rows = wv.shape[0]
    out_specs.insert(2, pl.BlockSpec((vt_rows, tm), lambda i: (0, i)))
    out_shape.insert(2, jax.ShapeDtypeStruct((vt_rows, t), BF16))
    return pl.pallas_call(
        _proj_kernel,
        grid=(t // tm,),
        in_specs=[row(d), pl.BlockSpec((1, tm), lambda i: (0, i)), full(fq), full(f64), full(g_mix),
                  pl.BlockSpec(w_in_al.shape, lambda i: (0, 0), pipeline_mode=pl.Buffered(1)),
                  full(g_q), full(wq), full(g_kv), full(wk_aug), full(wv), full(b_gate)],
        out_specs=out_specs,
        out_shape=out_shape,
        compiler_params=pltpu.CompilerParams(dimension_semantics=("arbitrary",), vmem_limit_bytes=VMEM_LIMIT),
        name="proj",
    )(x2, pos, fq, f64, g_mix, w_in_al, g_q, wq, g_kv, wk_aug, wv, b_gate)


def _mla_kernel(q_ref, k_ref, vt_ref, o_ref, sa_ref, sb_ref, ma_ref, mb_ref, m_ref, acc_ref, *, tq):
    i = pl.program_id(2)
    m_ref[...] = jnp.full(m_ref.shape, NEG_BIG, F32)
    acc_ref[...] = jnp.zeros(acc_ref.shape, F32)
    ones = jnp.ones((MLA_ONES_ROWS, tq), BF16)

    def scores(j, bufs):
        s_ref, mx_ref = bufs
        k0 = pl.multiple_of(j * tq, tq)
        for hh in range(2):
            sl = slice(hh * LANES, (hh + 1) * LANES)
            st = lax.dot_general(k_ref[0, pl.ds(k0, tq), sl], q_ref[0, :, sl], (((1,), (1,)), ((), ())),
                                 preferred_element_type=F32)
            s_ref[hh] = st
            mx_ref[hh] = jnp.max(st, axis=0, keepdims=True)

    def update(j, bufs, masked):
        s_ref, mx_ref = bufs
        k0 = pl.multiple_of(j * tq, tq)
        for hh in range(2):
            vt = jnp.concatenate([vt_ref[hh * MLA_V:(hh + 1) * MLA_V, pl.ds(k0, tq)], ones], axis=0)
            st = s_ref[hh]
            if masked:
                kj = lax.broadcasted_iota(I32, (tq, tq), 0)
                qi = lax.broadcasted_iota(I32, (tq, tq), 1)
                st = jnp.where(kj <= qi, st, NEG_BIG)
                m_tile = jnp.max(st, axis=0, keepdims=True)
            else:
                m_tile = mx_ref[hh]
            m_old = m_ref[hh]
            m_new = jnp.maximum(m_old, m_tile)
            alpha = jnp.exp2(m_old - m_new)
            pt = jnp.exp2(st - m_new)
            acc_ref[hh] = alpha * acc_ref[hh] + jnp.dot(vt, pt.astype(BF16), preferred_element_type=F32)
            m_ref[hh] = m_new

    buf_a = (sa_ref, ma_ref)
    buf_b = (sb_ref, mb_ref)
    scores(0, buf_a)

    def body(jj, carry):
        scores(2 * jj + 1, buf_b)
        update(2 * jj, buf_a, False)
        scores(2 * jj + 2, buf_a)
        update(2 * jj + 1, buf_b, False)
        return carry

    lax.fori_loop(0, i // 2, body, 0)

    @pl.when(i % 2 == 0)
    def _():
        update(i, buf_a, True)

    @pl.when(i % 2 == 1)
    def _():
        scores(i, buf_b)
        update(i - 1, buf_a, False)
        update(i, buf_b, True)

    ot = jnp.concatenate([acc_ref[hh, :MLA_V] / acc_ref[hh, MLA_V:MLA_V + 1] for hh in range(2)],
                         axis=0)
    o_ref[0] = ot.T.astype(BF16)


def _mla_call(q, k, vt, seq):
    b, s, _ = q.shape
    assert s == seq
    tq = min(MLA_TQ, s)
    n_pairs = MLA_HEADS // 2
    return pl.pallas_call(
        functools.partial(_mla_kernel, tq=tq),
        grid=(b, n_pairs, s // tq),
        in_specs=[pl.BlockSpec((1, tq, 2 * LANES), lambda bi, hp, i: (bi, i, hp)),
                  pl.BlockSpec((1, s, 2 * LANES), lambda bi, hp, i: (bi, 0, hp)),
                  pl.BlockSpec((2 * MLA_V, s), lambda bi, hp, i: (hp, bi))],
        out_specs=pl.BlockSpec((1, tq, LANES), lambda bi, hp, i: (bi, i, hp)),
        out_shape=jax.ShapeDtypeStruct((b, s, n_pairs * LANES), BF16),
        scratch_shapes=[pltpu.VMEM((2, tq, tq), F32), pltpu.VMEM((2, tq, tq), F32),
                        pltpu.VMEM((2, 1, tq), F32), pltpu.VMEM((2, 1, tq), F32),
                        pltpu.VMEM((2, 1, tq), F32),
                        pltpu.VMEM((2, MLA_V + MLA_ONES_ROWS, tq), F32)],
        compiler_params=pltpu.CompilerParams(dimension_semantics=("arbitrary",) * 3, vmem_limit_bytes=VMEM_LIMIT),
        name="mla_attn",
    )(q, k, vt)


def _swa_kernel(sink_ref, q_ref, ka_ref, kb_ref, va_ref, vb_ref, kah_ref, kbh_ref, vah_ref, vbh_ref, o_ref, *, ts):
    w = SWA_WINDOW
    i = pl.program_id(1)
    ka = jnp.concatenate([kah_ref[0], ka_ref[0]], axis=0)
    kb = jnp.concatenate([kbh_ref[0], kb_ref[0]], axis=0)
    va = jnp.concatenate([vah_ref[0], va_ref[0]], axis=0)
    vb = jnp.concatenate([vbh_ref[0], vb_ref[0]], axis=0)
    lane_k = lax.broadcasted_iota(I32, (2 * w, LANES), 1)
    low = lane_k < SWA_HEAD_DIM
    qi = lax.broadcasted_iota(I32, (2 * w, 2 * w), 0) % w
    kj = lax.broadcasted_iota(I32, (2 * w, 2 * w), 1)
    diff = qi + w - kj
    band = (diff >= 0) & (diff < w)
    lane_o = lax.broadcasted_iota(I32, (w, LANES), 1)
    row2 = lax.broadcasted_iota(I32, (2 * w, 1), 0)
    zero = jnp.zeros((), BF16)
    stacks = ((0, ka, True, va), (1, kb, False, vb), (4, kb, True, vb), (5, ka, False, va))
    for n in range(ts // w):
        mask = band & ((i * (ts // w) + n > 0) | (kj >= w))
        res = []
        for h0, ksrc, keep_low, vsrc in stacks:
            p0 = h0 // 2
            q = jnp.concatenate([q_ref[0, n * w:(n + 1) * w, p0 * LANES:(p0 + 1) * LANES],
                                 q_ref[0, n * w:(n + 1) * w, (p0 + 1) * LANES:(p0 + 2) * LANES]], axis=0)
            kwin = ksrc[n * w:n * w + 2 * w]
            kwin = jnp.where(low if keep_low else ~low, kwin, zero)
            vwin = vsrc[n * w:n * w + 2 * w]
            s = lax.dot_general(q, kwin, (((1,), (1,)), ((), ())), preferred_element_type=F32)
            s = jnp.where(mask, s, NEG_BIG)
            sink = jnp.where(row2 < w, sink_ref[h0], sink_ref[h0 + 2]) * LOG2E
            m = jnp.maximum(jnp.max(s, axis=1, keepdims=True), sink)
            p = jnp.exp2(s - m)
            den = jnp.sum(p, axis=1, keepdims=True) + jnp.exp2(sink - m)
            o = jnp.dot(p.astype(BF16), vwin, preferred_element_type=F32) / den
            res.append(o)
        o02, o13, o46, o57 = res
        sel = lane_o < SWA_HEAD_DIM
        rows = slice(n * w, (n + 1) * w)
        o_ref[0, rows, 0 * LANES:1 * LANES] = jnp.where(sel, o02[:w], o13[:w]).astype(BF16)
        o_ref[0, rows, 1 * LANES:2 * LANES] = jnp.where(sel, o02[w:], o13[w:]).astype(BF16)
        o_ref[0, rows, 2 * LANES:3 * LANES] = jnp.where(sel, o46[:w], o57[:w]).astype(BF16)
        o_ref[0, rows, 3 * LANES:4 * LANES] = jnp.where(sel, o46[w:], o57[w:]).astype(BF16)


def _swa_call(sinks, q, ksa, ksb, vsa, vsb):
    b, s, _ = q.shape
    ts = min(SWA_TS, s)
    w = SWA_WINDOW
    r = ts // w
    main = pl.BlockSpec((1, ts, LANES), lambda bi, i: (bi, i, 0))
    halo = pl.BlockSpec((1, w, LANES), lambda bi, i: (bi, jnp.maximum(i * r - 1, 0), 0))
    return pl.pallas_call(
        functools.partial(_swa_kernel, ts=ts),
        grid=(b, s // ts),
        in_specs=[pl.BlockSpec(memory_space=pltpu.SMEM),
                  pl.BlockSpec((1, ts, 4 * LANES), lambda bi, i: (bi, i, 0)),
                  main, main, main, main, halo, halo, halo, halo],
        out_specs=pl.BlockSpec((1, ts, 4 * LANES), lambda bi, i: (bi, i, 0)),
        out_shape=jax.ShapeDtypeStruct((b, s, 4 * LANES), BF16),
        compiler_params=pltpu.CompilerParams(dimension_semantics=("arbitrary",) * 2, vmem_limit_bytes=VMEM_LIMIT),
        name="swa_attn",
    )(sinks, q, ksa, ksb, vsa, vsb, ksa, ksb, vsa, vsb)


def _memkv_kernel(mem_ref, g_ref, w_ref, o_ref):
    mn = _rms(mem_ref[...], g_ref[...]).astype(BF16)
    o_ref[...] = jnp.dot(mn, w_ref[...], preferred_element_type=F32).astype(BF16)


def _memkv_call(mem2, g_mem, w_mem_kv):
    n, d = mem2.shape
    tm = min(256, n)
    return pl.pallas_call(
        _memkv_kernel,
        grid=(n // tm,),
        in_specs=[pl.BlockSpec((tm, d), lambda i: (i, 0)),
                  pl.BlockSpec(g_mem.shape, lambda i: (0, 0)),
                  pl.BlockSpec(w_mem_kv.shape, lambda i: (0, 0))],
        out_specs=pl.BlockSpec((tm, w_mem_kv.shape[1]), lambda i: (i, 0)),
        out_shape=jax.ShapeDtypeStruct((n, w_mem_kv.shape[1]), BF16),
        compiler_params=pltpu.CompilerParams(dimension_semantics=("arbitrary",), vmem_limit_bytes=VMEM_LIMIT),
        name="mem_kv",
    )(mem2, g_mem, w_mem_kv)


def _merge_kernel(x_ref, omla_ref, oswa_ref, qx_ref, gt_ref, kvm_ref, wmo_ref, wso_ref, wxo_ref, wout_ref,
                  gffn_ref, wr_ref, br_ref,
                  h_ref, hnp_ref, idx_ref, gate_ref, rank_ref, cnt_ref, run_ref, *, tm, sub):
    @pl.when(pl.program_id(0) == 0)
    def _():
        run_ref[...] = jnp.zeros(run_ref.shape, F32)

    d = x_ref.shape[1]
    kv_cols = XA_HEADS * XA_HEAD_DIM
    erow = lax.broadcasted_iota(I32, (N_EXPERTS, sub), 0)
    tri_t = (lax.broadcasted_iota(I32, (sub, sub), 0) < lax.broadcasted_iota(I32, (sub, sub), 1)).astype(BF16)
    nt = (((1,), (1,)), ((), ()))
    run = run_ref[...]
    for hf in range(tm // sub):
        rows = slice(hf * sub, (hf + 1) * sub)

        oxs = []
        for hd in range(XA_HEADS):
            sl = slice(hd * LANES, (hd + 1) * LANES)
            km = kvm_ref[0, :, sl]
            vm = kvm_ref[0, :, kv_cols + hd * LANES:kv_cols + (hd + 1) * LANES]
            s = lax.dot_general(qx_ref[rows, sl], km, nt, preferred_element_type=F32)
            p = jnp.exp2(s - jnp.max(s, axis=1, keepdims=True))
            den = jnp.sum(p, axis=1, keepdims=True)
            oxs.append((jnp.dot(p.astype(BF16), vm, preferred_element_type=F32) / den).astype(BF16))
        oxa = jnp.concatenate(oxs, axis=1)

        merged = (gt_ref[rows, 0:d].astype(F32) * jnp.dot(omla_ref[rows, :], wmo_ref[...], preferred_element_type=F32)
                  + gt_ref[rows, d:2 * d].astype(F32) * jnp.dot(oswa_ref[rows, :], wso_ref[...],
                                                                 preferred_element_type=F32)
                  + gt_ref[rows, 2 * d:3 * d].astype(F32) * jnp.dot(oxa, wxo_ref[...], preferred_element_type=F32))
        h = x_ref[rows, :] + jnp.dot(merged.astype(BF16), wout_ref[...], preferred_element_type=F32)
        h_ref[rows, :] = h

        hn = _rms(h, gffn_ref[...])
        hn_hi = hn.astype(BF16)
        hn_hi32 = hn_hi.astype(F32)
        hn_lo = (hn - hn_hi32).astype(BF16)
        bits = pltpu.bitcast(hn_hi32, U32)
        _store_slabs(hnp_ref, hf * sub, (bits[:, : d // 2] >> 16) | (bits[:, d // 2:] & jnp.uint32(0xFFFF0000)))

        part = lax.dot_general(wr_ref[...], hn_hi, nt, preferred_element_type=F32)
        logits_t = (part[:N_EXPERTS] + part[N_EXPERTS:]
                    + lax.dot_general(wr_ref[0:N_EXPERTS, :], hn_lo, nt, preferred_element_type=F32) + br_ref[...])

        work = logits_t
        vals, idxs, hots = [], [], []
        for _ in range(TOP_K):
            mx = jnp.max(work, axis=0, keepdims=True)
            ix = jnp.min(jnp.where(work == mx, erow, N_EXPERTS), axis=0, keepdims=True)
            hot = erow == ix
            work = jnp.where(hot, -jnp.inf, work)
            vals.append(mx)
            idxs.append(ix)
            hots.append(hot)
        es = [jnp.exp(v - vals[0]) for v in vals]
        den = es[0] + es[1] + es[2] + es[3]
        sel_t = (hots[0] | hots[1] | hots[2] | hots[3])
        prefix_t = jnp.dot(sel_t.astype(BF16), tri_t, preferred_element_type=F32) + run
        for k in range(TOP_K):
            idx_ref[k:k + 1, rows] = idxs[k]
            gate_ref[k:k + 1, rows] = es[k] / den
            rank_ref[k:k + 1, rows] = jnp.sum(jnp.where(hots[k], prefix_t, 0.0), axis=0, keepdims=True).astype(I32)
        run = run + jnp.sum(sel_t.astype(F32), axis=1, keepdims=True)
    run_ref[...] = run
    cnt_ref[...] = run.astype(I32)


def _merge_call(x2, omla, oswa, qx, gates, kvm, wmo, wso, wxo, wout, g_ffn, wr_split, b_router_col, seq):
    t, d = x2.shape
    tm = MERGE_TM
    per_b = seq // tm
    row = lambda n: pl.BlockSpec((tm, n), lambda i: (i, 0))
    col = lambda: pl.BlockSpec((TOP_K, tm), lambda i: (0, i))
    full = lambda a: pl.BlockSpec(a.shape, lambda i: (0,) * a.ndim)
    return pl.pallas_call(
        functools.partial(_merge_kernel, tm=tm, sub=MERGE_SUB),
        grid=(t // tm,),
        in_specs=[row(d), row(512), row(512), row(512), row(3 * d),
                  pl.BlockSpec((1,) + kvm.shape[1:], lambda i: (i // per_b, 0, 0)),
                  full(wmo), full(wso), full(wxo), full(wout), full(g_ffn), full(wr_split), full(b_router_col)],
        out_specs=[row(d), pl.BlockSpec((tm, d // 2 // LANES, LANES), lambda i: (i, 0, 0)), col(), col(), col(),
                   pl.BlockSpec((N_EXPERTS, 1), lambda i: (0, 0))],
        out_shape=[jax.ShapeDtypeStruct((t, d), F32), jax.ShapeDtypeStruct((t, d // 2 // LANES, LANES), U32),
                   jax.ShapeDtypeStruct((TOP_K, t), I32), jax.ShapeDtypeStruct((TOP_K, t), F32),
                   jax.ShapeDtypeStruct((TOP_K, t), I32), jax.ShapeDtypeStruct((N_EXPERTS, 1), I32)],
        scratch_shapes=[pltpu.VMEM((N_EXPERTS, 1), F32)],
        compiler_params=pltpu.CompilerParams(dimension_semantics=("arbitrary",), vmem_limit_bytes=VMEM_LIMIT),
        name="merge_router",
    )(x2, omla, oswa, qx, gates, kvm, wmo, wso, wxo, wout, g_ffn, wr_split, b_router_col)


SC_CORES = 2
SC_SUBCORES = 16
SC_WORKERS = SC_CORES * SC_SUBCORES
SC_CHUNK = 64


def _sc_mesh():
    return plsc.VectorSubcoreMesh(core_axis_name="c", subcore_axis_name="s",
                                  num_cores=SC_CORES, num_subcores=SC_SUBCORES)


def _sc_index_blocks(idx):
    n = idx.shape[0]
    per_w = n // SC_WORKERS
    n_ch = per_w // SC_CHUNK
    assert per_w * SC_WORKERS == n and n_ch * SC_CHUNK == per_w
    return idx.reshape(SC_WORKERS, n_ch, SC_CHUNK), per_w, n_ch


def _sc_two_buffer_loop(n_ch, load, store, bufs, load_sems, store_sems):
    assert n_ch % 2 == 0
    a, b = bufs
    la, lb = load_sems
    sa, sb = store_sems
    load(0, a, la).start()

    @pl.loop(0, n_ch, step=2)
    def _(j):
        load(j, a, la).wait()

        @pl.when(j > 0)
        def _():
            store(j - 1, b, sb).wait()

        load(j + 1, b, lb).start()
        store(j, a, sa).start()
        load(j + 1, b, lb).wait()
        store(j, a, sa).wait()

        @pl.when(j + 2 < n_ch)
        def _():
            load(j + 2, a, la).start()

        store(j + 1, b, sb).start()

    store(n_ch - 1, b, sb).wait()


def _sc_scratch(n_ch, row_shape, dtype):
    return [pltpu.VMEM((n_ch, SC_CHUNK), I32), pltpu.VMEM((SC_CHUNK,) + row_shape, dtype),
            pltpu.VMEM((SC_CHUNK,) + row_shape, dtype)] + [pltpu.SemaphoreType.DMA] * 4


def _sc_scatter_rows(src, idx, n_out):
    idx3, per_w, n_ch = _sc_index_blocks(idx)
    n_src = src.shape[0]
    assert n_src % per_w == 0

    @functools.partial(
        pl.kernel, mesh=_sc_mesh(),
        out_type=jax.ShapeDtypeStruct((n_out,) + src.shape[1:], src.dtype),
        scratch_types=_sc_scratch(n_ch, src.shape[1:], src.dtype),
        name="moe_dispatch_sc")
    def k(src_hbm, idx_hbm, out_hbm, idx_v, rows_a, rows_b, la, lb, sa, sb):
        wid = lax.axis_index("s") * SC_CORES + lax.axis_index("c")
        base = lax.rem(wid * per_w, n_src)
        pltpu.sync_copy(idx_hbm.at[wid], idx_v)

        def load(c, buf, sem):
            return pltpu.make_async_copy(src_hbm.at[pl.ds(base + c * SC_CHUNK, SC_CHUNK)], buf, sem)

        def store(c, buf, sem):
            return pltpu.make_async_copy(buf, out_hbm.at[idx_v.at[c]], sem)

        _sc_two_buffer_loop(n_ch, load, store, (rows_a, rows_b), (la, lb), (sa, sb))

    return k(src, idx3)


def _sc_gather_rows(table, idx):
    idx3, per_w, n_ch = _sc_index_blocks(idx)

    @functools.partial(
        pl.kernel, mesh=_sc_mesh(),
        out_type=jax.ShapeDtypeStruct((idx.shape[0],) + table.shape[1:], table.dtype),
        scratch_types=_sc_scratch(n_ch, table.shape[1:], table.dtype),
        name="moe_gather_sc")
    def k(table_hbm, idx_hbm, out_hbm, idx_v, rows_a, rows_b, la, lb, sa, sb):
        wid = lax.axis_index("s") * SC_CORES + lax.axis_index("c")
        base = wid * per_w
        pltpu.sync_copy(idx_hbm.at[wid], idx_v)

        def load(c, buf, sem):
            return pltpu.make_async_copy(table_hbm.at[idx_v.at[c]], buf, sem)

        def store(c, buf, sem):
            return pltpu.make_async_copy(buf, out_hbm.at[pl.ds(base + c * SC_CHUNK, SC_CHUNK)], sem)

        _sc_two_buffer_loop(n_ch, load, store, (rows_a, rows_b), (la, lb), (sa, sb))

    return k(table, idx3)


def _unpack_lo(w):
    return pltpu.bitcast(w << 16, F32)


def _unpack_hi(w):
    return pltpu.bitcast(w & jnp.uint32(0xFFFF0000), F32)


def _ffn_kernel(te_ref, xs_ref, wgu_hbm, bgu_ref, wd_hbm, bd_ref, y_ref, wgu_f32, wd_f32, wgu_bf, wd_bf, wsem):
    i = pl.program_id(0)
    n = pl.num_programs(0)
    n_used = te_ref[n]
    e = te_ref[i]
    first = (i == 0) | (e != te_ref[jnp.maximum(i - 1, 0)])
    slot = te_ref[2 * n + 1 + i] % 2
    nxt = te_ref[3 * n + 1 + i]

    def fetch(expert, s):
        return (pltpu.make_async_copy(wgu_hbm.at[expert], wgu_f32.at[s], wsem.at[0, s]),
                pltpu.make_async_copy(wd_hbm.at[expert], wd_f32.at[s], wsem.at[1, s]))

    @pl.when(i == 0)
    def _():
        for cp in fetch(e, slot):
            cp.start()

    @pl.when((i < n_used) & first)
    def _():
        for cp in fetch(e, slot):
            cp.wait()
        wgu_bf[...] = wgu_f32[slot].astype(BF16)
        wd_bf[...] = wd_f32[slot].astype(BF16)

        @pl.when(nxt >= 0)
        def _():
            for cp in fetch(nxt, 1 - slot):
                cp.start()

    @pl.when(i < n_used)
    def _():
        w = _load_slabs(xs_ref)
        half = w.shape[1]
        valid = te_ref[n + 1 + i]
        w = jnp.where(lax.broadcasted_iota(I32, w.shape, 0) < valid, w, jnp.uint32(0))
        x_lo = _unpack_lo(w).astype(BF16)
        x_hi = _unpack_hi(w).astype(BF16)
        gu = (jnp.dot(x_lo, wgu_bf[0:half, :], preferred_element_type=F32)
              + jnp.dot(x_hi, wgu_bf[half:, :], preferred_element_type=F32) + bgu_ref[0])
        de = gu.shape[1] // 2
        x_glu = jnp.minimum(gu[:, :de], SWIGLU_LIMIT)
        x_lin = jnp.clip(gu[:, de:], -SWIGLU_LIMIT, SWIGLU_LIMIT)
        hdn = x_glu * jax.nn.sigmoid(SWIGLU_ALPHA * x_glu) * (x_lin + 1.0)
        y = jnp.dot(hdn.astype(BF16), wd_bf[...], preferred_element_type=F32) + bd_ref[0]
        bits = pltpu.bitcast(y.astype(BF16).astype(F32), U32)
        _store_slabs(y_ref, 0, (bits[:, :half] >> 16) | (bits[:, half:] & jnp.uint32(0xFFFF0000)))

    @pl.when(i >= n_used)
    def _():
        y_ref[...] = jnp.zeros(y_ref.shape, U32)


def _ffn_call(tile_table, xs, w_gate_up, b_gate_up, w_down, b_down):
    r, ns, lanes = xs.shape
    tm = FFN_TM
    ne, d, de2 = w_gate_up.shape
    return pl.pallas_call(
        _ffn_kernel,
        grid_spec=pltpu.PrefetchScalarGridSpec(
            num_scalar_prefetch=1,
            grid=(r // tm,),
            in_specs=[pl.BlockSpec((tm, ns, lanes),
                                   lambda i, te: (jnp.minimum(i, jnp.maximum(te[r // tm] - 1, 0)), 0, 0)),
                      pl.BlockSpec(memory_space=pl.ANY),
                      pl.BlockSpec((1, 1, de2), lambda i, te: (te[i], 0, 0)),
                      pl.BlockSpec(memory_space=pl.ANY),
                      pl.BlockSpec((1, 1, d), lambda i, te: (te[i], 0, 0))],
            out_specs=pl.BlockSpec((tm, ns, lanes), lambda i, te: (i, 0, 0)),
            scratch_shapes=[pltpu.VMEM((2, d, de2), F32), pltpu.VMEM((2, de2 // 2, d), F32),
                            pltpu.VMEM((d, de2), BF16), pltpu.VMEM((de2 // 2, d), BF16),
                            pltpu.SemaphoreType.DMA((2, 2))]),
        out_shape=jax.ShapeDtypeStruct((r, ns, lanes), U32),
        compiler_params=pltpu.CompilerParams(dimension_semantics=("arbitrary",), vmem_limit_bytes=VMEM_LIMIT),
        name="moe_ffn",
    )(tile_table, xs, w_gate_up, b_gate_up, w_down, b_down)


def _combine_dense_kernel(h_ref, gate_ref, gfin_ref, y0_ref, y1_ref, y2_ref, y3_ref, o_ref):
    half = y0_ref.shape[1] * y0_ref.shape[2]
    lo = h_ref[:, :half]
    hi = h_ref[:, half:]
    for k, y_ref in enumerate((y0_ref, y1_ref, y2_ref, y3_ref)):
        g = gate_ref[:, k:k + 1]
        w = _load_slabs(y_ref)
        lo = lo + g * _unpack_lo(w)
        hi = hi + g * _unpack_hi(w)
    ms = (jnp.sum(lo * lo, axis=1, keepdims=True) + jnp.sum(hi * hi, axis=1, keepdims=True)) / (2 * half)
    inv = lax.rsqrt(ms + RMS_EPS)
    o_ref[:, :half] = lo * inv * gfin_ref[:, :half]
    o_ref[:, half:] = hi * inv * gfin_ref[:, half:]


def _combine_dense_call(h, gate, g_final, yk):
    t, d = h.shape
    tm = COMBINE_TM
    per_k = t // tm
    yspec = lambda k: pl.BlockSpec((tm,) + yk.shape[1:], lambda i: (k * per_k + i, 0, 0))
    return pl.pallas_call(
        _combine_dense_kernel,
        grid=(t // tm,),
        in_specs=[pl.BlockSpec((tm, d), lambda i: (i, 0)),
                  pl.BlockSpec((tm, TOP_K), lambda i: (i, 0)),
                  pl.BlockSpec((1, d), lambda i: (0, 0)),
                  yspec(0), yspec(1), yspec(2), yspec(3)],
        out_specs=pl.BlockSpec((tm, d), lambda i: (i, 0)),
        out_shape=jax.ShapeDtypeStruct((t, d), F32),
        compiler_params=pltpu.CompilerParams(dimension_semantics=("arbitrary",), vmem_limit_bytes=VMEM_LIMIT),
        name="moe_combine",
    )(h, gate, g_final, yk, yk, yk, yk)


def _rope_freqs():
    def inv_freq(dh):
        return (ROPE_THETA ** (-jnp.arange(0, dh, 2, dtype=F32) / dh))[:, None]

    return inv_freq(MLA_ROPE), inv_freq(SWA_HEAD_DIM)


def _winprep_kernel(w_ref, o_ref):
    w = w_ref[...]
    rb = w.shape[0]
    c1 = MLA_Q_RANK + MLA_KV_RANK
    tail = w[:, c1 + MLA_ROPE:]
    lane = lax.broadcasted_iota(I32, (rb, LANES), 1)
    in_rope = (lane >= MLA_NOPE) & (lane < MLA_NOPE + MLA_ROPE)
    kr_p = jnp.where(in_rope, pltpu.roll(w[:, c1:c1 + LANES], MLA_NOPE, axis=1), 0.0)
    pieces = [w[:, :c1], kr_p, tail]
    off = 0
    for pc in pieces:
        o_ref[:, off:off + pc.shape[1]] = pc.astype(BF16)
        off += pc.shape[1]


def _winprep_call(w_in, layer):
    _, d, n = w_in.shape
    rb = 128
    return pl.pallas_call(
        _winprep_kernel,
        grid=(d // rb,),
        in_specs=[pl.BlockSpec((None, rb, n), lambda i: (layer, i, 0))],
        out_specs=pl.BlockSpec((rb, _D1), lambda i: (i, 0)),
        out_shape=jax.ShapeDtypeStruct((d, _D1), BF16),
        compiler_params=pltpu.CompilerParams(dimension_semantics=("arbitrary",), vmem_limit_bytes=VMEM_LIMIT),
        name="w_in_prep",
    )(w_in)


def _prep_weights(w_in, layer, w_mla_uq, w_mla_ukv):
    w_in_al = _winprep_call(w_in, layer)

    r = w_mla_uq.shape[0]
    wq = w_mla_uq.reshape(r, MLA_HEADS, MLA_NOPE + MLA_ROPE)
    zq = jnp.zeros((r, MLA_HEADS, LANES - MLA_NOPE - MLA_ROPE), w_mla_uq.dtype)
    wq_pad = jnp.concatenate([wq, zq], axis=-1).reshape(r, MLA_HEADS * LANES).astype(BF16)

    rk = w_mla_ukv.shape[0]
    wkv = w_mla_ukv.reshape(rk, MLA_HEADS, MLA_NOPE + MLA_V)
    wk_aug = jnp.concatenate([wkv[..., :MLA_NOPE], jnp.zeros((rk, MLA_HEADS, LANES - MLA_NOPE), w_mla_ukv.dtype)],
                             axis=-1).reshape(rk, MLA_HEADS * LANES).astype(BF16)
    wv_t = wkv[..., MLA_NOPE:].reshape(rk, MLA_HEADS * MLA_V).T.astype(BF16)
    return w_in_al, wq_pad, wk_aug, wv_t


def kernel(x, mem, positions, g_mix, w_in, g_mla_q, w_mla_uq, g_mla_kv, w_mla_ukv, w_mla_o, swa_sinks, w_swa_o,
           g_mem, w_mem_kv, w_xa_o, b_gate, w_out, g_ffn, w_router, b_router, w_gate_up, b_gate_up, w_down,
           b_down, g_final):
    b, s, d = x.shape
    t = b * s
    depth = g_mix.shape[0]
    h = x.reshape(t, d)
    pos = positions.astype(F32).reshape(1, t)
    fq, f64 = _rope_freqs()
    for l in range(depth):
        w_in_al, wq_pad, wk_aug, wv_t = _prep_weights(w_in, l, w_mla_uq[l], w_mla_ukv[l])
        (qm, km, vmt, qs, ksa, ksb, vsa, vsb, qx, gates) = _proj_call(
            h, pos, fq, f64, g_mix[l][None], w_in_al, g_mla_q[l][None], wq_pad,
            g_mla_kv[l][None], wk_aug, wv_t, b_gate[l][None])
        r3 = lambda a: a.reshape(b, s, a.shape[1])
        omla = _mla_call(r3(qm), r3(km), vmt, s).reshape(t, -1)
        oswa = _swa_call(swa_sinks[l], r3(qs), r3(ksa), r3(ksb), r3(vsa), r3(vsb)).reshape(t, -1)
        m = mem.shape[1]
        kvm = _memkv_call(mem.reshape(b * m, d), g_mem[l][None], w_mem_kv[l].astype(BF16)).reshape(b, m, -1)
        wr_t = w_router[l].T
        wr_hi = wr_t.astype(BF16)
        wr_split = jnp.concatenate([wr_hi, (wr_t - wr_hi.astype(F32)).astype(BF16)], axis=0)
        h_mid, hnp, idx, gate, rank, counts = _merge_call(
            h, omla, oswa, qx, gates, kvm, w_mla_o[l].astype(BF16), w_swa_o[l].astype(BF16),
            w_xa_o[l].astype(BF16), w_out[l].astype(BF16), g_ffn[l][None], wr_split, b_router[l][:, None], s)

        counts = counts[:, 0]
        padded = ((counts + FFN_TM - 1) // FFN_TM) * FFN_TM
        padded_end = jnp.cumsum(padded)
        offsets = padded_end - padded
        experts = jnp.arange(N_EXPERTS, dtype=I32)
        dest = (jnp.sum(jnp.where(idx[..., None] == experts, offsets, 0), axis=-1) + rank).reshape(-1).astype(I32)
        n_tiles = (t * TOP_K) // FFN_TM + N_EXPERTS
        n_used = (padded_end[-1] // FFN_TM).astype(I32)
        tile_start = jnp.minimum(jnp.arange(n_tiles, dtype=I32), jnp.maximum(n_used - 1, 0)) * FFN_TM
        tile_expert = jnp.sum((padded_end[None, :] <= tile_start[:, None]).astype(I32), axis=1)
        tile_expert = jnp.minimum(tile_expert, N_EXPERTS - 1)
        onehot = tile_expert[:, None] == experts[None, :]
        pick = lambda v: jnp.sum(jnp.where(onehot, v[None, :], 0), axis=1)
        tile_valid = jnp.clip(pick(counts) - (tile_start - pick(offsets)), 0, FFN_TM)
        nonempty = padded > 0
        tile_group = pick(jnp.cumsum(nonempty.astype(I32)) - 1)
        later = jnp.where(nonempty[None, :] & (experts[None, :] > experts[:, None]), experts[None, :], N_EXPERTS)
        next_expert = jnp.min(later, axis=1)
        tile_next = pick(jnp.where(next_expert < N_EXPERTS, next_expert, -1))
        te = jnp.concatenate([tile_expert, n_used[None], tile_valid.astype(I32), tile_group.astype(I32),
                              tile_next.astype(I32)])

        xs = _sc_scatter_rows(hnp, dest, n_tiles * FFN_TM)
        y = _ffn_call(te, xs, w_gate_up[l], b_gate_up[l][:, None, :], w_down[l], b_down[l][:, None, :])
        yk = _sc_gather_rows(y, dest)
        if l == depth - 1:
            gfin = g_final[None]
            out = _combine_dense_call(h_mid, gate.T, gfin, yk)
        else:
            raise NotImplementedError("depth > 1 needs a combine without the final norm")
        h = out
    return h.reshape(b, s, d)
```

```python
import functools
import math

import jax
import jax.numpy as jnp
from jax import lax
from jax.experimental import pallas as pl
from jax.experimental.pallas import tpu as pltpu
from jax.experimental.pallas import tpu_sc as plsc

F32 = jnp.float32
BF16 = jnp.bfloat16
U32 = jnp.uint32
I32 = jnp.int32

LANES = 128
ROPE_THETA = 10000.0
RMS_EPS = 1e-6
LOG2E = 1.4426950408889634

MLA_HEADS = 8
MLA_NOPE = 64
MLA_ROPE = 32
MLA_V = 64
MLA_Q_RANK = 256
MLA_KV_RANK = 128
SWA_HEADS = 8
SWA_KV_HEADS = 2
SWA_HEAD_DIM = 64
SWA_WINDOW = 128
XA_HEADS = 4
XA_HEAD_DIM = 128
N_EXPERTS = 32
TOP_K = 4
SWIGLU_ALPHA = 1.702
SWIGLU_LIMIT = 7.0
N_BRANCHES = 3

NEG_BIG = -1e30

PROJ_TM = 512
PROJ_SUB = 256
MLA_TQ = 1024
MLA_ONES_ROWS = 16
SWA_TS = 512
MERGE_TM = 512
MERGE_SUB = 256
FFN_TM = 512
COMBINE_TM = 256

VMEM_LIMIT = 56 * 1024 * 1024


def _rms(x, g):
    return x * lax.rsqrt(jnp.mean(x * x, axis=-1, keepdims=True) + RMS_EPS) * g


def _store_slabs(ref, row0, value):
    rows, n, _ = ref.shape
    flat = ref.reshape(rows * n, LANES)
    for c in range(n):
        flat[pl.ds(row0 * n + c, value.shape[0], stride=n), :] = value[:, c * LANES:(c + 1) * LANES]


def _load_slabs(ref, row0=0, m=None):
    rows, n, _ = ref.shape
    m = rows if m is None else m
    flat = ref.reshape(rows * n, LANES)
    return jnp.concatenate([flat[pl.ds(row0 * n + c, m, stride=n), :] for c in range(n)], axis=1)


_A0, _A1 = 0, 512
_B0, _B1 = 512, 1280
_C0, _C1 = 1280, 1792
_D0, _D1 = 1792, 4864


def _rotate_half(x, d, lo, hi):
    n = x.shape[1]
    half = (hi - lo) // 2
    lane = lax.broadcasted_iota(I32, x.shape, 1) % d
    up = pltpu.roll(x, n - half, axis=1)
    dn = pltpu.roll(x, half, axis=1)
    return jnp.where((lane >= lo) & (lane < lo + half), -up, jnp.where((lane >= lo + half) & (lane < hi), dn, 0.0))


def _proj_kernel(x_ref, pos_ref, fq_ref, f64_ref, gmix_ref, win_ref, gq_ref, wq_ref,
                 gkv_ref, wk_ref, wv_ref, bgate_ref,
                 qm_ref, km_ref, vm_ref, qs_ref, ksa_ref, ksb_ref, vsa_ref, vsb_ref, qx_ref, gt_ref):
    for hf in range(x_ref.shape[0] // PROJ_SUB):
        rows = slice(hf * PROJ_SUB, (hf + 1) * PROJ_SUB)
        _proj_rows(x_ref[rows, :], pos_ref[:, rows], fq_ref, f64_ref, gmix_ref, win_ref, gq_ref, wq_ref, gkv_ref,
                   wk_ref, wv_ref, bgate_ref,
                   [r.at[rows] for r in (qm_ref, km_ref)], vm_ref.at[:, rows],
                   [r.at[rows] for r in (qs_ref, ksa_ref, ksb_ref, vsa_ref, vsb_ref, qx_ref, gt_ref)])


def _proj_rows(x, pos, fq_ref, f64_ref, gmix_ref, win_ref, gq_ref, wq_ref, gkv_ref, wk_ref, wv_ref, bgate_ref,
               mla_refs, vm_ref, other_refs):
    qm_ref, km_ref = mla_refs
    qs_ref, ksa_ref, ksb_ref, vsa_ref, vsb_ref, qx_ref, gt_ref = other_refs
    xn = _rms(x, gmix_ref[...]).astype(BF16)
    tm = x.shape[0]
    ang16 = fq_ref[...] * pos
    ang32 = f64_ref[...] * pos
    c16, s16, c32, s32 = jnp.cos(ang16), jnp.sin(ang16), jnp.cos(ang32), jnp.sin(ang32)
    one = jnp.ones((MLA_NOPE, tm), F32)
    zero = jnp.zeros((MLA_NOPE, tm), F32)
    pad = LANES - MLA_NOPE - MLA_ROPE
    cosq = jnp.concatenate([one, c16, c16, one[:pad]], axis=0).T
    sinq = jnp.concatenate([zero, s16, s16, zero[:pad]], axis=0).T
    cos64 = jnp.concatenate([c32, c32, c32, c32], axis=0).T
    sin64 = jnp.concatenate([s32, s32, s32, s32], axis=0).T
    rope_lo, rope_hi = MLA_NOPE, MLA_NOPE + MLA_ROPE

    xa = jnp.dot(xn, win_ref[:, _A0:_A1], preferred_element_type=F32)
    cqn = _rms(xa[:, 0:256], gq_ref[...]).astype(BF16)
    qa = jnp.dot(cqn, wq_ref[...], preferred_element_type=F32)
    qb = _rotate_half(qa, LANES, rope_lo, rope_hi)
    q_scale = LOG2E / math.sqrt(MLA_NOPE + MLA_ROPE)
    ckvn = _rms(xa[:, 256:384], gkv_ref[...]).astype(BF16)
    ka = jnp.dot(ckvn, wk_ref[...], preferred_element_type=F32)
    kr = xa[:, 384:512]
    krope = kr * cosq + _rotate_half(kr, LANES, rope_lo, rope_hi) * sinq
    for h in range(MLA_HEADS):
        sl = slice(h * LANES, (h + 1) * LANES)
        qm_ref[:, sl] = ((qa[:, sl] * cosq + qb[:, sl] * sinq) * q_scale).astype(BF16)
        km_ref[:, sl] = (ka[:, sl] + krope).astype(BF16)
    vm_ref[...] = lax.dot_general(wv_ref[...], ckvn, (((1,), (1,)), ((), ())),
                                  preferred_element_type=F32).astype(BF16)

    xb = jnp.dot(xn, win_ref[:, _B0:_B1], preferred_element_type=F32)
    s_scale = LOG2E / math.sqrt(SWA_HEAD_DIM)
    nq = SWA_HEADS * SWA_HEAD_DIM
    qs = xb[:, :nq]
    qs_rot = _rotate_half(qs, SWA_HEAD_DIM, 0, SWA_HEAD_DIM)
    for p in range(SWA_HEADS // 2):
        sl = slice(p * LANES, (p + 1) * LANES)
        qs_ref[:, sl] = ((qs[:, sl] * cos64 + qs_rot[:, sl] * sin64) * s_scale).astype(BF16)
    ks = xb[:, nq:nq + LANES]
    ks = ks * cos64 + _rotate_half(ks, SWA_HEAD_DIM, 0, SWA_HEAD_DIM) * sin64
    ksa_ref[...] = ks.astype(BF16)
    ksb_ref[...] = pltpu.roll(ks, 64, axis=1).astype(BF16)
    vs = xb[:, nq + LANES:nq + 2 * LANES]
    vsa_ref[...] = vs.astype(BF16)
    vsb_ref[...] = pltpu.roll(vs, 64, axis=1).astype(BF16)

    xc = jnp.dot(xn, win_ref[:, _C0:_C1], preferred_element_type=F32)
    qx_ref[...] = (xc * (LOG2E / math.sqrt(XA_HEAD_DIM))).astype(BF16)

    xd = jnp.dot(xn, win_ref[:, _D0:_D1], preferred_element_type=F32) + bgate_ref[...]
    gt_ref[...] = jax.nn.sigmoid(xd).astype(BF16)


def _proj_call(x2, pos, fq, f64, g_mix, w_in_al, g_q, wq, g_kv, wk_aug, wv, b_gate):
    t, d = x2.shape
    tm = PROJ_TM
    row = lambda n: pl.BlockSpec((tm, n), lambda i: (i, 0))
    full = lambda a: pl.BlockSpec(a.shape, lambda i: (0,) * a.ndim)
    out_cols = [1024, 1024, 512, 128, 128, 128, 128, 512, 3072]
    out_specs = [row(n) for n in out_cols]
    out_shape = [jax.ShapeDtypeStruct((t, n), BF16) for n in out_cols]
    vt_rows = wv.shape[0]
    out_specs.insert(2, pl.BlockSpec((vt_rows, tm), lambda i: (0, i)))
    out_shape.insert(2, jax.ShapeDtypeStruct((vt_rows, t), BF16))
    return pl.pallas_call(
        _proj_kernel,
        grid=(t // tm,),
        in_specs=[row(d), pl.BlockSpec((1, tm), lambda i: (0, i)), full(fq), full(f64), full(g_mix),
                  pl.BlockSpec(w_in_al.shape, lambda i: (0, 0), pipeline_mode=pl.Buffered(1)),
                  full(g_q), full(wq), full(g_kv), full(wk_aug), full(wv), full(b_gate)],
        out_specs=out_specs,
        out_shape=out_shape,
        compiler_params=pltpu.CompilerParams(dimension_semantics=("arbitrary",), vmem_limit_bytes=VMEM_LIMIT),
        name="proj",
    )(x2, pos, fq, f64, g_mix, w_in_al, g_q, wq, g_kv, wk_aug, wv, b_gate)


def _mla_kernel(q_ref, k_ref, vt_ref, o_ref, sa_ref, sb_ref, ma_ref, mb_ref, m_ref, acc_ref, *, tq):
    i = pl.program_id(2)
    m_ref[...] = jnp.full(m_ref.shape, NEG_BIG, F32)
    acc_ref[...] = jnp.zeros(acc_ref.shape, F32)
    ones = jnp.ones((MLA_ONES_ROWS, tq), BF16)

    def scores(j, bufs):
        s_ref, mx_ref = bufs
        k0 = pl.multiple_of(j * tq, tq)
        for hh in range(2):
            sl = slice(hh * LANES, (hh + 1) * LANES)
            st = lax.dot_general(k_ref[0, pl.ds(k0, tq), sl], q_ref[0, :, sl], (((1,), (1,)), ((), ())),
                                 preferred_element_type=F32)
            s_ref[hh] = st
            mx_ref[hh] = jnp.max(st, axis=0, keepdims=True)

    def update(j, bufs, masked):
        s_ref, mx_ref = bufs
        k0 = pl.multiple_of(j * tq, tq)
        for hh in range(2):
            vt = jnp.concatenate([vt_ref[hh * MLA_V:(hh + 1) * MLA_V, pl.ds(k0, tq)], ones], axis=0)
            st = s_ref[hh]
            if masked:
                kj = lax.broadcasted_iota(I32, (tq, tq), 0)
                qi = lax.broadcasted_iota(I32, (tq, tq), 1)
                st = jnp.where(kj <= qi, st, NEG_BIG)
                m_tile = jnp.max(st, axis=0, keepdims=True)
            else:
                m_tile = mx_ref[hh]
            m_old = m_ref[hh]
            m_new = jnp.maximum(m_old, m_tile)
            alpha = jnp.exp2(m_old - m_new)
            pt = jnp.exp2(st - m_new)
            acc_ref[hh] = alpha * acc_ref[hh] + jnp.dot(vt, pt.astype(BF16), preferred_element_type=F32)
            m_ref[hh] = m_new

    buf_a = (sa_ref, ma_ref)
    buf_b = (sb_ref, mb_ref)
    scores(0, buf_a)

    def body(jj, carry):
        scores(2 * jj + 1, buf_b)
        update(2 * jj, buf_a, False)
        scores(2 * jj + 2, buf_a)
        update(2 * jj + 1, buf_b, False)
        return carry

    lax.fori_loop(0, i // 2, body, 0)

    @pl.when(i % 2 == 0)
    def _():
        update(i, buf_a, True)

    @pl.when(i % 2 == 1)
    def _():
        scores(i, buf_b)
        update(i - 1, buf_a, False)
        update(i, buf_b, True)

    ot = jnp.concatenate([acc_ref[hh, :MLA_V] / acc_ref[hh, MLA_V:MLA_V + 1] for hh in range(2)],
                         axis=0)
    o_ref[0] = ot.T.astype(BF16)


def _mla_call(q, k, vt, seq):
    b, s, _ = q.shape
    assert s == seq
    tq = min(MLA_TQ, s)
    n_pairs = MLA_HEADS // 2
    return pl.pallas_call(
        functools.partial(_mla_kernel, tq=tq),
        grid=(b, n_pairs, s // tq),
        in_specs=[pl.BlockSpec((1, tq, 2 * LANES), lambda bi, hp, i: (bi, i, hp)),
                  pl.BlockSpec((1, s, 2 * LANES), lambda bi, hp, i: (bi, 0, hp)),
                  pl.BlockSpec((2 * MLA_V, s), lambda bi, hp, i: (hp, bi))],
        out_specs=pl.BlockSpec((1, tq, LANES), lambda bi, hp, i: (bi, i, hp)),
        out_shape=jax.ShapeDtypeStruct((b, s, n_pairs * LANES), BF16),
        scratch_shapes=[pltpu.VMEM((2, tq, tq), F32), pltpu.VMEM((2, tq, tq), F32),
                        pltpu.VMEM((2, 1, tq), F32), pltpu.VMEM((2, 1, tq), F32),
                        pltpu.VMEM((2, 1, tq), F32),
                        pltpu.VMEM((2, MLA_V + MLA_ONES_ROWS, tq), F32)],
        compiler_params=pltpu.CompilerParams(dimension_semantics=("arbitrary",) * 3, vmem_limit_bytes=VMEM_LIMIT),
        name="mla_attn",
    )(q, k, vt)


def _swa_kernel(sink_ref, q_ref, ka_ref, kb_ref, va_ref, vb_ref, kah_ref, kbh_ref, vah_ref, vbh_ref, o_ref, *, ts):
    w = SWA_WINDOW
    i = pl.program_id(1)
    ka = jnp.concatenate([kah_ref[0], ka_ref[0]], axis=0)
    kb = jnp.concatenate([kbh_ref[0], kb_ref[0]], axis=0)
    va = jnp.concatenate([vah_ref[0], va_ref[0]], axis=0)
    vb = jnp.concatenate([vbh_ref[0], vb_ref[0]], axis=0)
    lane_k = lax.broadcasted_iota(I32, (2 * w, LANES), 1)
    low = lane_k < SWA_HEAD_DIM
    qi = lax.broadcasted_iota(I32, (2 * w, 2 * w), 0) % w
    kj = lax.broadcasted_iota(I32, (2 * w, 2 * w), 1)
    diff = qi + w - kj
    band = (diff >= 0) & (diff < w)
    lane_o = lax.broadcasted_iota(I32, (w, LANES), 1)
    row2 = lax.broadcasted_iota(I32, (2 * w, 1), 0)
    zero = jnp.zeros((), BF16)
    stacks = ((0, ka, True, va), (1, kb, False, vb), (4, kb, True, vb), (5, ka, False, va))
    for n in range(ts // w):
        mask = band & ((i * (ts // w) + n > 0) | (kj >= w))
        res = []
        for h0, ksrc, keep_low, vsrc in stacks:
            p0 = h0 // 2
            q = jnp.concatenate([q_ref[0, n * w:(n + 1) * w, p0 * LANES:(p0 + 1) * LANES],
                                 q_ref[0, n * w:(n + 1) * w, (p0 + 1) * LANES:(p0 + 2) * LANES]], axis=0)
            kwin = ksrc[n * w:n * w + 2 * w]
            kwin = jnp.where(low if keep_low else ~low, kwin, zero)
            vwin = vsrc[n * w:n * w + 2 * w]
            s = lax.dot_general(q, kwin, (((1,), (1,)), ((), ())), preferred_element_type=F32)
            s = jnp.where(mask, s, NEG_BIG)
            sink = jnp.where(row2 < w, sink_ref[h0], sink_ref[h0 + 2]) * LOG2E
            m = jnp.maximum(jnp.max(s, axis=1, keepdims=True), sink)
            p = jnp.exp2(s - m)
            den = jnp.sum(p, axis=1, keepdims=True) + jnp.exp2(sink - m)
            o = jnp.dot(p.astype(BF16), vwin, preferred_element_type=F32) / den
            res.append(o)
        o02, o13, o46, o57 = res
        sel = lane_o < SWA_HEAD_DIM
        rows = slice(n * w, (n + 1) * w)
        o_ref[0, rows, 0 * LANES:1 * LANES] = jnp.where(sel, o02[:w], o13[:w]).astype(BF16)
        o_ref[0, rows, 1 * LANES:2 * LANES] = jnp.where(sel, o02[w:], o13[w:]).astype(BF16)
        o_ref[0, rows, 2 * LANES:3 * LANES] = jnp.where(sel, o46[:w], o57[:w]).astype(BF16)
        o_ref[0, rows, 3 * LANES:4 * LANES] = jnp.where(sel, o46[w:], o57[w:]).astype(BF16)


def _swa_call(sinks, q, ksa, ksb, vsa, vsb):
    b, s, _ = q.shape
    ts = min(SWA_TS, s)
    w = SWA_WINDOW
    r = ts // w
    main = pl.BlockSpec((1, ts, LANES), lambda bi, i: (bi, i, 0))
    halo = pl.BlockSpec((1, w, LANES), lambda bi, i: (bi, jnp.maximum(i * r - 1, 0), 0))
    return pl.pallas_call(
        functools.partial(_swa_kernel, ts=ts),
        grid=(b, s // ts),
        in_specs=[pl.BlockSpec(memory_space=pltpu.SMEM),
                  pl.BlockSpec((1, ts, 4 * LANES), lambda bi, i: (bi, i, 0)),
                  main, main, main, main, halo, halo, halo, halo],
        out_specs=pl.BlockSpec((1, ts, 4 * LANES), lambda bi, i: (bi, i, 0)),
        out_shape=jax.ShapeDtypeStruct((b, s, 4 * LANES), BF16),
        compiler_params=pltpu.CompilerParams(dimension_semantics=("arbitrary",) * 2, vmem_limit_bytes=VMEM_LIMIT),
        name="swa_attn",
    )(sinks, q, ksa, ksb, vsa, vsb, ksa, ksb, vsa, vsb)


def _memkv_kernel(mem_ref, g_ref, w_ref, o_ref):
    mn = _rms(mem_ref[...], g_ref[...]).astype(BF16)
    o_ref[...] = jnp.dot(mn, w_ref[...], preferred_element_type=F32).astype(BF16)


def _memkv_call(mem2, g_mem, w_mem_kv):
    n, d = mem2.shape
    tm = min(256, n)
    return pl.pallas_call(
        _memkv_kernel,
        grid=(n // tm,),
        in_specs=[pl.BlockSpec((tm, d), lambda i: (i, 0)),
                  pl.BlockSpec(g_mem.shape, lambda i: (0, 0)),
                  pl.BlockSpec(w_mem_kv.shape, lambda i: (0, 0))],
        out_specs=pl.BlockSpec((tm, w_mem_kv.shape[1]), lambda i: (i, 0)),
        out_shape=jax.ShapeDtypeStruct((n, w_mem_kv.shape[1]), BF16),
        compiler_params=pltpu.CompilerParams(dimension_semantics=("arbitrary",), vmem_limit_bytes=VMEM_LIMIT),
        name="mem_kv",
    )(mem2, g_mem, w_mem_kv)


def _merge_kernel(x_ref, omla_ref, oswa_ref, qx_ref, gt_ref, kvm_ref, wmo_ref, wso_ref, wxo_ref, wout_ref,
                  gffn_ref, wr_ref, br_ref,
                  h_ref, hnp_ref, idx_ref, gate_ref, rank_ref, cnt_ref, run_ref, *, tm, sub):
    @pl.when(pl.program_id(0) == 0)
    def _():
        run_ref[...] = jnp.zeros(run_ref.shape, F32)

    d = x_ref.shape[1]
    kv_cols = XA_HEADS * XA_HEAD_DIM
    erow = lax.broadcasted_iota(I32, (N_EXPERTS, sub), 0)
    tri_t = (lax.broadcasted_iota(I32, (sub, sub), 0) < lax.broadcasted_iota(I32, (sub, sub), 1)).astype(BF16)
    nt = (((1,), (1,)), ((), ()))
    run = run_ref[...]
    for hf in range(tm // sub):
        rows = slice(hf * sub, (hf + 1) * sub)

        oxs = []
        for hd in range(XA_HEADS):
            sl = slice(hd * LANES, (hd + 1) * LANES)
            km = kvm_ref[0, :, sl]
            vm = kvm_ref[0, :, kv_cols + hd * LANES:kv_cols + (hd + 1) * LANES]
            s = lax.dot_general(qx_ref[rows, sl], km, nt, preferred_element_type=F32)
            p = jnp.exp2(s - jnp.max(s, axis=1, keepdims=True))
            den = jnp.sum(p, axis=1, keepdims=True)
            oxs.append((jnp.dot(p.astype(BF16), vm, preferred_element_type=F32) / den).astype(BF16))
        oxa = jnp.concatenate(oxs, axis=1)

        merged = (gt_ref[rows, 0:d].astype(F32) * jnp.dot(omla_ref[rows, :], wmo_ref[...], preferred_element_type=F32)
                  + gt_ref[rows, d:2 * d].astype(F32) * jnp.dot(oswa_ref[rows, :], wso_ref[...],
                                                                 preferred_element_type=F32)
                  + gt_ref[rows, 2 * d:3 * d].astype(F32) * jnp.dot(oxa, wxo_ref[...], preferred_element_type=F32))
        h = x_ref[rows, :] + jnp.dot(merged.astype(BF16), wout_ref[...], preferred_element_type=F32)
        h_ref[rows, :] = h

        hn = _rms(h, gffn_ref[...])
        hn_hi = hn.astype(BF16)
        hn_hi32 = hn_hi.astype(F32)
        hn_lo = (hn - hn_hi32).astype(BF16)
        bits = pltpu.bitcast(hn_hi32, U32)
        _store_slabs(hnp_ref, hf * sub, (bits[:, : d // 2] >> 16) | (bits[:, d // 2:] & jnp.uint32(0xFFFF0000)))

        part = lax.dot_general(wr_ref[...], hn_hi, nt, preferred_element_type=F32)
        logits_t = (part[:N_EXPERTS] + part[N_EXPERTS:]
                    + lax.dot_general(wr_ref[0:N_EXPERTS, :], hn_lo, nt, preferred_element_type=F32) + br_ref[...])

        work = logits_t
        vals, idxs, hots = [], [], []
        for _ in range(TOP_K):
            mx = jnp.max(work, axis=0, keepdims=True)
            ix = jnp.min(jnp.where(work == mx, erow, N_EXPERTS), axis=0, keepdims=True)
            hot = erow == ix
            work = jnp.where(hot, -jnp.inf, work)
            vals.append(mx)
            idxs.append(ix)
            hots.append(hot)
        es = [jnp.exp(v - vals[0]) for v in vals]
        den = es[0] + es[1] + es[2] + es[3]
        sel_t = (hots[0] | hots[1] | hots[2] | hots[3])
        prefix_t = jnp.dot(sel_t.astype(BF16), tri_t, preferred_element_type=F32) + run
        for k in range(TOP_K):
            idx_ref[k:k + 1, rows] = idxs[k]
            gate_ref[k:k + 1, rows] = es[k] / den
            rank_ref[k:k + 1, rows] = jnp.sum(jnp.where(hots[k], prefix_t, 0.0), axis=0, keepdims=True).astype(I32)
        run = run + jnp.sum(sel_t.astype(F32), axis=1, keepdims=True)
    run_ref[...] = run
    cnt_ref[...] = run.astype(I32)


def _merge_call(x2, omla, oswa, qx, gates, kvm, wmo, wso, wxo, wout, g_ffn, wr_split, b_router_col, seq):
    t, d = x2.shape
    tm = MERGE_TM
    per_b = seq // tm
    row = lambda n: pl.BlockSpec((tm, n), lambda i: (i, 0))
    col = lambda: pl.BlockSpec((TOP_K, tm), lambda i: (0, i))
    full = lambda a: pl.BlockSpec(a.shape, lambda i: (0,) * a.ndim)
    return pl.pallas_call(
        functools.partial(_merge_kernel, tm=tm, sub=MERGE_SUB),
        grid=(t // tm,),
        in_specs=[row(d), row(512), row(512), row(512), row(3 * d),
                  pl.BlockSpec((1,) + kvm.shape[1:], lambda i: (i // per_b, 0, 0)),
                  full(wmo), full(wso), full(wxo), full(wout), full(g_ffn), full(wr_split), full(b_router_col)],
        out_specs=[row(d), pl.BlockSpec((tm, d // 2 // LANES, LANES), lambda i: (i, 0, 0)), col(), col(), col(),
                   pl.BlockSpec((N_EXPERTS, 1), lambda i: (0, 0))],
        out_shape=[jax.ShapeDtypeStruct((t, d), F32), jax.ShapeDtypeStruct((t, d // 2 // LANES, LANES), U32),
                   jax.ShapeDtypeStruct((TOP_K, t), I32), jax.ShapeDtypeStruct((TOP_K, t), F32),
                   jax.ShapeDtypeStruct((TOP_K, t), I32), jax.ShapeDtypeStruct((N_EXPERTS, 1), I32)],
        scratch_shapes=[pltpu.VMEM((N_EXPERTS, 1), F32)],
        compiler_params=pltpu.CompilerParams(dimension_semantics=("arbitrary",), vmem_limit_bytes=VMEM_LIMIT),
        name="merge_router",
    )(x2, omla, oswa, qx, gates, kvm, wmo, wso, wxo, wout, g_ffn, wr_split, b_router_col)


SC_CORES = 2
SC_SUBCORES = 16
SC_WORKERS = SC_CORES * SC_SUBCORES
SC_CHUNK = 64


def _sc_mesh():
    return plsc.VectorSubcoreMesh(core_axis_name="c", subcore_axis_name="s",
                                  num_cores=SC_CORES, num_subcores=SC_SUBCORES)


def _sc_index_blocks(idx):
    n = idx.shape[0]
    per_w = n // SC_WORKERS
    n_ch = per_w // SC_CHUNK
    assert per_w * SC_WORKERS == n and n_ch * SC_CHUNK == per_w
    return idx.reshape(SC_WORKERS, n_ch, SC_CHUNK), per_w, n_ch


def _sc_two_buffer_loop(n_ch, load, store, bufs, load_sems, store_sems):
    assert n_ch % 2 == 0
    a, b = bufs
    la, lb = load_sems
    sa, sb = store_sems
    load(0, a, la).start()

    @pl.loop(0, n_ch, step=2)
    def _(j):
        load(j, a, la).wait()

        @pl.when(j > 0)
        def _():
            store(j - 1, b, sb).wait()

        load(j + 1, b, lb).start()
        store(j, a, sa).start()
        load(j + 1, b, lb).wait()
        store(j, a, sa).wait()

        @pl.when(j + 2 < n_ch)
        def _():
            load(j + 2, a, la).start()

        store(j + 1, b, sb).start()

    store(n_ch - 1, b, sb).wait()


def _sc_scratch(n_ch, row_shape, dtype):
    return [pltpu.VMEM((n_ch, SC_CHUNK), I32), pltpu.VMEM((SC_CHUNK,) + row_shape, dtype),
            pltpu.VMEM((SC_CHUNK,) + row_shape, dtype)] + [pltpu.SemaphoreType.DMA] * 4


def _sc_scatter_rows(src, idx, n_out):
    idx3, per_w, n_ch = _sc_index_blocks(idx)
    n_src = src.shape[0]
    assert n_src % per_w == 0

    @functools.partial(
        pl.kernel, mesh=_sc_mesh(),
        out_type=jax.ShapeDtypeStruct((n_out,) + src.shape[1:], src.dtype),
        scratch_types=_sc_scratch(n_ch, src.shape[1:], src.dtype),
        name="moe_dispatch_sc")
    def k(src_hbm, idx_hbm, out_hbm, idx_v, rows_a, rows_b, la, lb, sa, sb):
        wid = lax.axis_index("s") * SC_CORES + lax.axis_index("c")
        base = lax.rem(wid * per_w, n_src)
        pltpu.sync_copy(idx_hbm.at[wid], idx_v)

        def load(c, buf, sem):
            return pltpu.make_async_copy(src_hbm.at[pl.ds(base + c * SC_CHUNK, SC_CHUNK)], buf, sem)

        def store(c, buf, sem):
            return pltpu.make_async_copy(buf, out_hbm.at[idx_v.at[c]], sem)

        _sc_two_buffer_loop(n_ch, load, store, (rows_a, rows_b), (la, lb), (sa, sb))

    return k(src, idx3)


def _sc_gather_rows(table, idx):
    idx3, per_w, n_ch = _sc_index_blocks(idx)

    @functools.partial(
        pl.kernel, mesh=_sc_mesh(),
        out_type=jax.ShapeDtypeStruct((idx.shape[0],) + table.shape[1:], table.dtype),
        scratch_types=_sc_scratch(n_ch, table.shape[1:], table.dtype),
        name="moe_gather_sc")
    def k(table_hbm, idx_hbm, out_hbm, idx_v, rows_a, rows_b, la, lb, sa, sb):
        wid = lax.axis_index("s") * SC_CORES + lax.axis_index("c")
        base = wid * per_w
        pltpu.sync_copy(idx_hbm.at[wid], idx_v)

        def load(c, buf, sem):
            return pltpu.make_async_copy(table_hbm.at[idx_v.at[c]], buf, sem)

        def store(c, buf, sem):
            return pltpu.make_async_copy(buf, out_hbm.at[pl.ds(base + c * SC_CHUNK, SC_CHUNK)], sem)

        _sc_two_buffer_loop(n_ch, load, store, (rows_a, rows_b), (la, lb), (sa, sb))

    return k(table, idx3)


def _unpack_lo(w):
    return pltpu.bitcast(w << 16, F32)


def _unpack_hi(w):
    return pltpu.bitcast(w & jnp.uint32(0xFFFF0000), F32)


def _ffn_kernel(te_ref, xs_ref, wgu_hbm, bgu_ref, wd_hbm, bd_ref, y_ref, wgu_f32, wd_f32, wgu_bf, wd_bf, wsem):
    i = pl.program_id(0)
    n = pl.num_programs(0)
    n_used = te_ref[n]
    e = te_ref[i]
    first = (i == 0) | (e != te_ref[jnp.maximum(i - 1, 0)])
    slot = te_ref[2 * n + 1 + i] % 2
    nxt = te_ref[3 * n + 1 + i]

    def fetch(expert, s):
        return (pltpu.make_async_copy(wgu_hbm.at[expert], wgu_f32.at[s], wsem.at[0, s]),
                pltpu.make_async_copy(wd_hbm.at[expert], wd_f32.at[s], wsem.at[1, s]))

    @pl.when(i == 0)
    def _():
        for cp in fetch(e, slot):
            cp.start()

    @pl.when((i < n_used) & first)
    def _():
        for cp in fetch(e, slot):
            cp.wait()
        wgu_bf[...] = wgu_f32[slot].astype(BF16)
        wd_bf[...] = wd_f32[slot].astype(BF16)

        @pl.when(nxt >= 0)
        def _():
            for cp in fetch(nxt, 1 - slot):
                cp.start()

    @pl.when(i < n_used)
    def _():
        w = _load_slabs(xs_ref)
        half = w.shape[1]
        valid = te_ref[n + 1 + i]
        w = jnp.where(lax.broadcasted_iota(I32, w.shape, 0) < valid, w, jnp.uint32(0))
        x_lo = _unpack_lo(w).astype(BF16)
        x_hi = _unpack_hi(w).astype(BF16)
        gu = (jnp.dot(x_lo, wgu_bf[0:half, :], preferred_element_type=F32)
              + jnp.dot(x_hi, wgu_bf[half:, :], preferred_element_type=F32) + bgu_ref[0])
        de = gu.shape[1] // 2
        x_glu = jnp.minimum(gu[:, :de], SWIGLU_LIMIT)
        x_lin = jnp.clip(gu[:, de:], -SWIGLU_LIMIT, SWIGLU_LIMIT)
        hdn = x_glu * jax.nn.sigmoid(SWIGLU_ALPHA * x_glu) * (x_lin + 1.0)
        y = jnp.dot(hdn.astype(BF16), wd_bf[...], preferred_element_type=F32) + bd_ref[0]
        bits = pltpu.bitcast(y.astype(BF16).astype(F32), U32)
        _store_slabs(y_ref, 0, (bits[:, :half] >> 16) | (bits[:, half:] & jnp.uint32(0xFFFF0000)))

    @pl.when(i >= n_used)
    def _():
        y_ref[...] = jnp.zeros(y_ref.shape, U32)


def _ffn_call(tile_table, xs, w_gate_up, b_gate_up, w_down, b_down):
    r, ns, lanes = xs.shape
    tm = FFN_TM
    ne, d, de2 = w_gate_up.shape
    return pl.pallas_call(
        _ffn_kernel,
        grid_spec=pltpu.PrefetchScalarGridSpec(
            num_scalar_prefetch=1,
            grid=(r // tm,),
            in_specs=[pl.BlockSpec((tm, ns, lanes),
                                   lambda i, te: (jnp.minimum(i, jnp.maximum(te[r // tm] - 1, 0)), 0, 0)),
                      pl.BlockSpec(memory_space=pl.ANY),
                      pl.BlockSpec((1, 1, de2), lambda i, te: (te[i], 0, 0)),
                      pl.BlockSpec(memory_space=pl.ANY),
                      pl.BlockSpec((1, 1, d), lambda i, te: (te[i], 0, 0))],
            out_specs=pl.BlockSpec((tm, ns, lanes), lambda i, te: (i, 0, 0)),
            scratch_shapes=[pltpu.VMEM((2, d, de2), F32), pltpu.VMEM((2, de2 // 2, d), F32),
                            pltpu.VMEM((d, de2), BF16), pltpu.VMEM((de2 // 2, d), BF16),
                            pltpu.SemaphoreType.DMA((2, 2))]),
        out_shape=jax.ShapeDtypeStruct((r, ns, lanes), U32),
        compiler_params=pltpu.CompilerParams(dimension_semantics=("arbitrary",), vmem_limit_bytes=VMEM_LIMIT),
        name="moe_ffn",
    )(tile_table, xs, w_gate_up, b_gate_up, w_down, b_down)


def _combine_dense_kernel(h_ref, gate_ref, gfin_ref, y0_ref, y1_ref, y2_ref, y3_ref, o_ref):
    half = y0_ref.shape[1] * y0_ref.shape[2]
    lo = h_ref[:, :half]
    hi = h_ref[:, half:]
    for k, y_ref in enumerate((y0_ref, y1_ref, y2_ref, y3_ref)):
        g = gate_ref[:, k:k + 1]
        w = _load_slabs(y_ref)
        lo = lo + g * _unpack_lo(w)
        hi = hi + g * _unpack_hi(w)
    ms = (jnp.sum(lo * lo, axis=1, keepdims=True) + jnp.sum(hi * hi, axis=1, keepdims=True)) / (2 * half)
    inv = lax.rsqrt(ms + RMS_EPS)
    o_ref[:, :half] = lo * inv * gfin_ref[:, :half]
    o_ref[:, half:] = hi * inv * gfin_ref[:, half:]


def _combine_dense_call(h, gate, g_final, yk):
    t, d = h.shape
    tm = COMBINE_TM
    per_k = t // tm
    yspec = lambda k: pl.BlockSpec((tm,) + yk.shape[1:], lambda i: (k * per_k + i, 0, 0))
    return pl.pallas_call(
        _combine_dense_kernel,
        grid=(t // tm,),
        in_specs=[pl.BlockSpec((tm, d), lambda i: (i, 0)),
                  pl.BlockSpec((tm, TOP_K), lambda i: (i, 0)),
                  pl.BlockSpec((1, d), lambda i: (0, 0)),
                  yspec(0), yspec(1), yspec(2), yspec(3)],
        out_specs=pl.BlockSpec((tm, d), lambda i: (i, 0)),
        out_shape=jax.ShapeDtypeStruct((t, d), F32),
        compiler_params=pltpu.CompilerParams(dimension_semantics=("arbitrary",), vmem_limit_bytes=VMEM_LIMIT),
        name="moe_combine",
    )(h, gate, g_final, yk, yk, yk, yk)


def _rope_freqs():
    def inv_freq(dh):
        return (ROPE_THETA ** (-jnp.arange(0, dh, 2, dtype=F32) / dh))[:, None]

    return inv_freq(MLA_ROPE), inv_freq(SWA_HEAD_DIM)


def _winprep_kernel(w_ref, o_ref):
    w = w_ref[...]
    rb = w.shape[0]
    c1 = MLA_Q_RANK + MLA_KV_RANK
    tail = w[:, c1 + MLA_ROPE:]
    lane = lax.broadcasted_iota(I32, (rb, LANES), 1)
    in_rope = (lane >= MLA_NOPE) & (lane < MLA_NOPE + MLA_ROPE)
    kr_p = jnp.where(in_rope, pltpu.roll(w[:, c1:c1 + LANES], MLA_NOPE, axis=1), 0.0)
    pieces = [w[:, :c1], kr_p, tail]
    off = 0
    for pc in pieces:
        o_ref[:, off:off + pc.shape[1]] = pc.astype(BF16)
        off += pc.shape[1]


def _winprep_call(w_in, layer):
    _, d, n = w_in.shape
    rb = 128
    return pl.pallas_call(
        _winprep_kernel,
        grid=(d // rb,),
        in_specs=[pl.BlockSpec((None, rb, n), lambda i: (layer, i, 0))],
        out_specs=pl.BlockSpec((rb, _D1), lambda i: (i, 0)),
        out_shape=jax.ShapeDtypeStruct((d, _D1), BF16),
        compiler_params=pltpu.CompilerParams(dimension_semantics=("arbitrary",), vmem_limit_bytes=VMEM_LIMIT),
        name="w_in_prep",
    )(w_in)


def _prep_weights(w_in, layer, w_mla_uq, w_mla_ukv):
    w_in_al = _winprep_call(w_in, layer)

    r = w_mla_uq.shape[0]
    wq = w_mla_uq.reshape(r, MLA_HEADS, MLA_NOPE + MLA_ROPE)
    zq = jnp.zeros((r, MLA_HEADS, LANES - MLA_NOPE - MLA_ROPE), w_mla_uq.dtype)
    wq_pad = jnp.concatenate([wq, zq], axis=-1).reshape(r, MLA_HEADS * LANES).astype(BF16)

    rk = w_mla_ukv.shape[0]
    wkv = w_mla_ukv.reshape(rk, MLA_HEADS, MLA_NOPE + MLA_V)
    wk_aug = jnp.concatenate([wkv[..., :MLA_NOPE], jnp.zeros((rk, MLA_HEADS, LANES - MLA_NOPE), w_mla_ukv.dtype)],
                             axis=-1).reshape(rk, MLA_HEADS * LANES).astype(BF16)
    wv_t = wkv[..., MLA_NOPE:].reshape(rk, MLA_HEADS * MLA_V).T.astype(BF16)
    return w_in_al, wq_pad, wk_aug, wv_t


def kernel(x, mem, positions, g_mix, w_in, g_mla_q, w_mla_uq, g_mla_kv, w_mla_ukv, w_mla_o, swa_sinks, w_swa_o,
           g_mem, w_mem_kv, w_xa_o, b_gate, w_out, g_ffn, w_router, b_router, w_gate_up, b_gate_up, w_down,
           b_down, g_final):
    b, s, d = x.shape
    t = b * s
    depth = g_mix.shape[0]
    h = x.reshape(t, d)
    pos = positions.astype(F32).reshape(1, t)
    fq, f64 = _rope_freqs()
    for l in range(depth):
        w_in_al, wq_pad, wk_aug, wv_t = _prep_weights(w_in, l, w_mla_uq[l], w_mla_ukv[l])
        (qm, km, vmt, qs, ksa, ksb, vsa, vsb, qx, gates) = _proj_call(
            h, pos, fq, f64, g_mix[l][None], w_in_al, g_mla_q[l][None], wq_pad,
            g_mla_kv[l][None], wk_aug, wv_t, b_gate[l][None])
        r3 = lambda a: a.reshape(b, s, a.shape[1])
        omla = _mla_call(r3(qm), r3(km), vmt, s).reshape(t, -1)
        oswa = _swa_call(swa_sinks[l], r3(qs), r3(ksa), r3(ksb), r3(vsa), r3(vsb)).reshape(t, -1)
        m = mem.shape[1]
        kvm = _memkv_call(mem.reshape(b * m, d), g_mem[l][None], w_mem_kv[l].astype(BF16)).reshape(b, m, -1)
        wr_t = w_router[l].T
        wr_hi = wr_t.astype(BF16)
        wr_split = jnp.concatenate([wr_hi, (wr_t - wr_hi.astype(F32)).astype(BF16)], axis=0)
        h_mid, hnp, idx, gate, rank, counts = _merge_call(
            h, omla, oswa, qx, gates, kvm, w_mla_o[l].astype(BF16), w_swa_o[l].astype(BF16),
            w_xa_o[l].astype(BF16), w_out[l].astype(BF16), g_ffn[l][None], wr_split, b_router[l][:, None], s)

        counts = counts[:, 0]
        padded = ((counts + FFN_TM - 1) // FFN_TM) * FFN_TM
        padded_end = jnp.cumsum(padded)
        offsets = padded_end - padded
        experts = jnp.arange(N_EXPERTS, dtype=I32)
        dest = (jnp.sum(jnp.where(idx[..., None] == experts, offsets, 0), axis=-1) + rank).reshape(-1).astype(I32)
        n_tiles = (t * TOP_K) // FFN_TM + N_EXPERTS
        n_used = (padded_end[-1] // FFN_TM).astype(I32)
        tile_start = jnp.minimum(jnp.arange(n_tiles, dtype=I32), jnp.maximum(n_used - 1, 0)) * FFN_TM
        tile_expert = jnp.sum((padded_end[None, :] <= tile_start[:, None]).astype(I32), axis=1)
        tile_expert = jnp.minimum(tile_expert, N_EXPERTS - 1)
        onehot = tile_expert[:, None] == experts[None, :]
        pick = lambda v: jnp.sum(jnp.where(onehot, v[None, :], 0), axis=1)
        tile_valid = jnp.clip(pick(counts) - (tile_start - pick(offsets)), 0, FFN_TM)
        nonempty = padded > 0
        tile_group = pick(jnp.cumsum(nonempty.astype(I32)) - 1)
        later = jnp.where(nonempty[None, :] & (experts[None, :] > experts[:, None]), experts[None, :], N_EXPERTS)
        next_expert = jnp.min(later, axis=1)
        tile_next = pick(jnp.where(next_expert < N_EXPERTS, next_expert, -1))
        te = jnp.concatenate([tile_expert, n_used[None], tile_valid.astype(I32), tile_group.astype(I32),
                              tile_next.astype(I32)])

        xs = _sc_scatter_rows(hnp, dest, n_tiles * FFN_TM)
        y = _ffn_call(te, xs, w_gate_up[l], b_gate_up[l][:, None, :], w_down[l], b_down[l][:, None, :])
        yk = _sc_gather_rows(y, dest)
        if l == depth - 1:
            gfin = g_final[None]
            out = _combine_dense_call(h_mid, gate.T, gfin, yk)
        else:
            raise NotImplementedError("depth > 1 needs a combine without the final norm")
        h = out
    return h.reshape(b, s, d)
```

```python
import functools
import math

import jax
import jax.numpy as jnp
from jax import lax
from jax.experimental import pallas as pl
from jax.experimental.pallas import tpu as pltpu
from jax.experimental.pallas import tpu_sc as plsc

F32 = jnp.float32
BF16 = jnp.bfloat16
U32 = jnp.uint32
I32 = jnp.int32

LANES = 128
ROPE_THETA = 10000.0
RMS_EPS = 1e-6
LOG2E = 1.4426950408889634

MLA_HEADS = 8
MLA_NOPE = 64
MLA_ROPE = 32
MLA_V = 64
MLA_Q_RANK = 256
MLA_KV_RANK = 128
SWA_HEADS = 8
SWA_KV_HEADS = 2
SWA_HEAD_DIM = 64
SWA_WINDOW = 128
XA_HEADS = 4
XA_HEAD_DIM = 128
N_EXPERTS = 32
TOP_K = 4
SWIGLU_ALPHA = 1.702
SWIGLU_LIMIT = 7.0
N_BRANCHES = 3

NEG_BIG = -1e30

PROJ_TM = 512
PROJ_SUB = 256
MLA_TQ = 1024
MLA_ONES_ROWS = 16
SWA_TS = 512
MERGE_TM = 512
MERGE_SUB = 256
FFN_TM = 1024
FFN_SUB = 256
COMBINE_TM = 256

VMEM_LIMIT = 56 * 1024 * 1024


def _rms(x, g):
    return x * lax.rsqrt(jnp.mean(x * x, axis=-1, keepdims=True) + RMS_EPS) * g


def _store_slabs(ref, row0, value):
    rows, n, _ = ref.shape
    flat = ref.reshape(rows * n, LANES)
    for c in range(n):
        flat[pl.ds(row0 * n + c, value.shape[0], stride=n), :] = value[:, c * LANES:(c + 1) * LANES]


def _load_slabs(ref, row0=0, m=None):
    rows, n, _ = ref.shape
    m = rows if m is None else m
    flat = ref.reshape(rows * n, LANES)
    return jnp.concatenate([flat[pl.ds(row0 * n + c, m, stride=n), :] for c in range(n)], axis=1)


_A0, _A1 = 0, 512
_B0, _B1 = 512, 1280
_C0, _C1 = 1280, 1792
_D0, _D1 = 1792, 4864


def _rotate_half(x, d, lo, hi):
    n = x.shape[1]
    half = (hi - lo) // 2
    lane = lax.broadcasted_iota(I32, x.shape, 1) % d
    up = pltpu.roll(x, n - half, axis=1)
    dn = pltpu.roll(x, half, axis=1)
    return jnp.where((lane >= lo) & (lane < lo + half), -up, jnp.where((lane >= lo + half) & (lane < hi), dn, 0.0))


def _proj_kernel(x_ref, pos_ref, fq_ref, f64_ref, gmix_ref, win_ref, gq_ref, wq_ref,
                 gkv_ref, wk_ref, wv_ref, bgate_ref,
                 qm_ref, km_ref, vm_ref, qs_ref, ksa_ref, ksb_ref, vsa_ref, vsb_ref, qx_ref, gt_ref):
    for hf in range(x_ref.shape[0] // PROJ_SUB):
        rows = slice(hf * PROJ_SUB, (hf + 1) * PROJ_SUB)
        _proj_rows(x_ref[rows, :], pos_ref[:, rows], fq_ref, f64_ref, gmix_ref, win_ref, gq_ref, wq_ref, gkv_ref,
                   wk_ref, wv_ref, bgate_ref,
                   [r.at[rows] for r in (qm_ref, km_ref)], vm_ref.at[:, rows],
                   [r.at[rows] for r in (qs_ref, ksa_ref, ksb_ref, vsa_ref, vsb_ref, qx_ref, gt_ref)])


def _proj_rows(x, pos, fq_ref, f64_ref, gmix_ref, win_ref, gq_ref, wq_ref, gkv_ref, wk_ref, wv_ref, bgate_ref,
               mla_refs, vm_ref, other_refs):
    qm_ref, km_ref = mla_refs
    qs_ref, ksa_ref, ksb_ref, vsa_ref, vsb_ref, qx_ref, gt_ref = other_refs
    xn = _rms(x, gmix_ref[...]).astype(BF16)
    tm = x.shape[0]
    ang16 = fq_ref[...] * pos
    ang32 = f64_ref[...] * pos
    c16, s16, c32, s32 = jnp.cos(ang16), jnp.sin(ang16), jnp.cos(ang32), jnp.sin(ang32)
    one = jnp.ones((MLA_NOPE, tm), F32)
    zero = jnp.zeros((MLA_NOPE, tm), F32)
    pad = LANES - MLA_NOPE - MLA_ROPE
    cosq = jnp.concatenate([one, c16, c16, one[:pad]], axis=0).T
    sinq = jnp.concatenate([zero, s16, s16, zero[:pad]], axis=0).T
    cos64 = jnp.concatenate([c32, c32, c32, c32], axis=0).T
    sin64 = jnp.concatenate([s32, s32, s32, s32], axis=0).T
    rope_lo, rope_hi = MLA_NOPE, MLA_NOPE + MLA_ROPE

    xa = jnp.dot(xn, win_ref[:, _A0:_A1], preferred_element_type=F32)
    cqn = _rms(xa[:, 0:256], gq_ref[...]).astype(BF16)
    qa = jnp.dot(cqn, wq_ref[...], preferred_element_type=F32)
    qb = _rotate_half(qa, LANES, rope_lo, rope_hi)
    q_scale = LOG2E / math.sqrt(MLA_NOPE + MLA_ROPE)
    ckvn = _rms(xa[:, 256:384], gkv_ref[...]).astype(BF16)
    ka = jnp.dot(ckvn, wk_ref[...], preferred_element_type=F32)
    kr = xa[:, 384:512]
    krope = kr * cosq + _rotate_half(kr, LANES, rope_lo, rope_hi) * sinq
    for h in range(MLA_HEADS):
        sl = slice(h * LANES, (h + 1) * LANES)
        qm_ref[:, sl] = ((qa[:, sl] * cosq + qb[:, sl] * sinq) * q_scale).astype(BF16)
        km_ref[:, sl] = (ka[:, sl] + krope).astype(BF16)
    vm_ref[...] = lax.dot_general(wv_ref[...], ckvn, (((1,), (1,)), ((), ())),
                                  preferred_element_type=F32).astype(BF16)

    xb = jnp.dot(xn, win_ref[:, _B0:_B1], preferred_element_type=F32)
    s_scale = LOG2E / math.sqrt(SWA_HEAD_DIM)
    nq = SWA_HEADS * SWA_HEAD_DIM
    qs = xb[:, :nq]
    qs_rot = _rotate_half(qs, SWA_HEAD_DIM, 0, SWA_HEAD_DIM)
    for p in range(SWA_HEADS // 2):
        sl = slice(p * LANES, (p + 1) * LANES)
        qs_ref[:, sl] = ((qs[:, sl] * cos64 + qs_rot[:, sl] * sin64) * s_scale).astype(BF16)
    ks = xb[:, nq:nq + LANES]
    ks = ks * cos64 + _rotate_half(ks, SWA_HEAD_DIM, 0, SWA_HEAD_DIM) * sin64
    ksa_ref[...] = ks.astype(BF16)
    ksb_ref[...] = pltpu.roll(ks, 64, axis=1).astype(BF16)
    vs = xb[:, nq + LANES:nq + 2 * LANES]
    vsa_ref[...] = vs.astype(BF16)
    vsb_ref[...] = pltpu.roll(vs, 64, axis=1).astype(BF16)

    xc = jnp.dot(xn, win_ref[:, _C0:_C1], preferred_element_type=F32)
    qx_ref[...] = (xc * (LOG2E / math.sqrt(XA_HEAD_DIM))).astype(BF16)

    xd = jnp.dot(xn, win_ref[:, _D0:_D1], preferred_element_type=F32) + bgate_ref[...]
    gt_ref[...] = jax.nn.sigmoid(xd).astype(BF16)


def _proj_call(x2, pos, fq, f64, g_mix, w_in_al, g_q, wq, g_kv, wk_aug, wv, b_gate):
    t, d = x2.shape
    tm = PROJ_TM
    row = lambda n: pl.BlockSpec((tm, n), lambda i: (i, 0))
    full = lambda a: pl.BlockSpec(a.shape, lambda i: (0,) * a.ndim)
    out_cols = [1024, 1024, 512, 128, 128, 128, 128, 512, 3072]
    out_specs = [row(n) for n in out_cols]
    out_shape = [jax.ShapeDtypeStruct((t, n), BF16) for n in out_cols]
    vt_rows = wv.shape[0]
    out_specs.insert(2, pl.BlockSpec((vt_rows, tm), lambda i: (0, i)))
    out_shape.insert(2, jax.ShapeDtypeStruct((vt_rows, t), BF16))
    return pl.pallas_call(
        _proj_kernel,
        grid=(t // tm,),
        in_specs=[row(d), pl.BlockSpec((1, tm), lambda i: (0, i)), full(fq), full(f64), full(g_mix),
                  pl.BlockSpec(w_in_al.shape, lambda i: (0, 0), pipeline_mode=pl.Buffered(1)),
                  full(g_q), full(wq), full(g_kv), full(wk_aug), full(wv), full(b_gate)],
        out_specs=out_specs,
        out_shape=out_shape,
        compiler_params=pltpu.CompilerParams(dimension_semantics=("arbitrary",), vmem_limit_bytes=VMEM_LIMIT),
        name="proj",
    )(x2, pos, fq, f64, g_mix, w_in_al, g_q, wq, g_kv, wk_aug, wv, b_gate)


def _mla_kernel(q_ref, k_ref, vt_ref, o_ref, sa_ref, sb_ref, ma_ref, mb_ref, m_ref, acc_ref, *, tq):
    i = pl.program_id(2)
    m_ref[...] = jnp.full(m_ref.shape, NEG_BIG, F32)
    acc_ref[...] = jnp.zeros(acc_ref.shape, F32)
    ones = jnp.ones((MLA_ONES_ROWS, tq), BF16)

    def scores(j, bufs):
        s_ref, mx_ref = bufs
        k0 = pl.multiple_of(j * tq, tq)
        for hh in range(2):
            sl = slice(hh * LANES, (hh + 1) * LANES)
            st = lax.dot_general(k_ref[0, pl.ds(k0, tq), sl], q_ref[0, :, sl], (((1,), (1,)), ((), ())),
                                 preferred_element_type=F32)
            s_ref[hh] = st
            mx_ref[hh] = jnp.max(st, axis=0, keepdims=True)

    def update(j, bufs, masked):
        s_ref, mx_ref = bufs
        k0 = pl.multiple_of(j * tq, tq)
        for hh in range(2):
            vt = jnp.concatenate([vt_ref[hh * MLA_V:(hh + 1) * MLA_V, pl.ds(k0, tq)], ones], axis=0)
            st = s_ref[hh]
            if masked:
                kj = lax.broadcasted_iota(I32, (tq, tq), 0)
                qi = lax.broadcasted_iota(I32, (tq, tq), 1)
                st = jnp.where(kj <= qi, st, NEG_BIG)
                m_tile = jnp.max(st, axis=0, keepdims=True)
            else:
                m_tile = mx_ref[hh]
            m_old = m_ref[hh]
            m_new = jnp.maximum(m_old, m_tile)
            alpha = jnp.exp2(m_old - m_new)
            pt = jnp.exp2(st - m_new)
            acc_ref[hh] = alpha * acc_ref[hh] + jnp.dot(vt, pt.astype(BF16), preferred_element_type=F32)
            m_ref[hh] = m_new

    buf_a = (sa_ref, ma_ref)
    buf_b = (sb_ref, mb_ref)
    scores(0, buf_a)

    def body(jj, carry):
        scores(2 * jj + 1, buf_b)
        update(2 * jj, buf_a, False)
        scores(2 * jj + 2, buf_a)
        update(2 * jj + 1, buf_b, False)
        return carry

    lax.fori_loop(0, i // 2, body, 0)

    @pl.when(i % 2 == 0)
    def _():
        update(i, buf_a, True)

    @pl.when(i % 2 == 1)
    def _():
        scores(i, buf_b)
        update(i - 1, buf_a, False)
        update(i, buf_b, True)

    ot = jnp.concatenate([acc_ref[hh, :MLA_V] / acc_ref[hh, MLA_V:MLA_V + 1] for hh in range(2)],
                         axis=0)
    o_ref[0] = ot.T.astype(BF16)


def _mla_call(q, k, vt, seq):
    b, s, _ = q.shape
    assert s == seq
    tq = min(MLA_TQ, s)
    n_pairs = MLA_HEADS // 2
    return pl.pallas_call(
        functools.partial(_mla_kernel, tq=tq),
        grid=(b, n_pairs, s // tq),
        in_specs=[pl.BlockSpec((1, tq, 2 * LANES), lambda bi, hp, i: (bi, i, hp)),
                  pl.BlockSpec((1, s, 2 * LANES), lambda bi, hp, i: (bi, 0, hp)),
                  pl.BlockSpec((2 * MLA_V, s), lambda bi, hp, i: (hp, bi))],
        out_specs=pl.BlockSpec((1, tq, LANES), lambda bi, hp, i: (bi, i, hp)),
        out_shape=jax.ShapeDtypeStruct((b, s, n_pairs * LANES), BF16),
        scratch_shapes=[pltpu.VMEM((2, tq, tq), F32), pltpu.VMEM((2, tq, tq), F32),
                        pltpu.VMEM((2, 1, tq), F32), pltpu.VMEM((2, 1, tq), F32),
                        pltpu.VMEM((2, 1, tq), F32),
                        pltpu.VMEM((2, MLA_V + MLA_ONES_ROWS, tq), F32)],
        compiler_params=pltpu.CompilerParams(dimension_semantics=("arbitrary",) * 3, vmem_limit_bytes=VMEM_LIMIT),
        name="mla_attn",
    )(q, k, vt)


def _swa_kernel(sink_ref, q_ref, ka_ref, kb_ref, va_ref, vb_ref, kah_ref, kbh_ref, vah_ref, vbh_ref, o_ref, *, ts):
    w = SWA_WINDOW
    i = pl.program_id(1)
    ka = jnp.concatenate([kah_ref[0], ka_ref[0]], axis=0)
    kb = jnp.concatenate([kbh_ref[0], kb_ref[0]], axis=0)
    va = jnp.concatenate([vah_ref[0], va_ref[0]], axis=0)
    vb = jnp.concatenate([vbh_ref[0], vb_ref[0]], axis=0)
    lane_k = lax.broadcasted_iota(I32, (2 * w, LANES), 1)
    low = lane_k < SWA_HEAD_DIM
    qi = lax.broadcasted_iota(I32, (2 * w, 2 * w), 0) % w
    kj = lax.broadcasted_iota(I32, (2 * w, 2 * w), 1)
    diff = qi + w - kj
    band = (diff >= 0) & (diff < w)
    lane_o = lax.broadcasted_iota(I32, (w, LANES), 1)
    row2 = lax.broadcasted_iota(I32, (2 * w, 1), 0)
    zero = jnp.zeros((), BF16)
    stacks = ((0, ka, True, va), (1, kb, False, vb), (4, kb, True, vb), (5, ka, False, va))
    for n in range(ts // w):
        mask = band & ((i * (ts // w) + n > 0) | (kj >= w))
        res = []
        for h0, ksrc, keep_low, vsrc in stacks:
            p0 = h0 // 2
            q = jnp.concatenate([q_ref[0, n * w:(n + 1) * w, p0 * LANES:(p0 + 1) * LANES],
                                 q_ref[0, n * w:(n + 1) * w, (p0 + 1) * LANES:(p0 + 2) * LANES]], axis=0)
            kwin = ksrc[n * w:n * w + 2 * w]
            kwin = jnp.where(low if keep_low else ~low, kwin, zero)
            vwin = vsrc[n * w:n * w + 2 * w]
            s = lax.dot_general(q, kwin, (((1,), (1,)), ((), ())), preferred_element_type=F32)
            s = jnp.where(mask, s, NEG_BIG)
            sink = jnp.where(row2 < w, sink_ref[h0], sink_ref[h0 + 2]) * LOG2E
            m = jnp.maximum(jnp.max(s, axis=1, keepdims=True), sink)
            p = jnp.exp2(s - m)
            den = jnp.sum(p, axis=1, keepdims=True) + jnp.exp2(sink - m)
            o = jnp.dot(p.astype(BF16), vwin, preferred_element_type=F32) / den
            res.append(o)
        o02, o13, o46, o57 = res
        sel = lane_o < SWA_HEAD_DIM
        rows = slice(n * w, (n + 1) * w)
        o_ref[0, rows, 0 * LANES:1 * LANES] = jnp.where(sel, o02[:w], o13[:w]).astype(BF16)
        o_ref[0, rows, 1 * LANES:2 * LANES] = jnp.where(sel, o02[w:], o13[w:]).astype(BF16)
        o_ref[0, rows, 2 * LANES:3 * LANES] = jnp.where(sel, o46[:w], o57[:w]).astype(BF16)
        o_ref[0, rows, 3 * LANES:4 * LANES] = jnp.where(sel, o46[w:], o57[w:]).astype(BF16)


def _swa_call(sinks, q, ksa, ksb, vsa, vsb):
    b, s, _ = q.shape
    ts = min(SWA_TS, s)
    w = SWA_WINDOW
    r = ts // w
    main = pl.BlockSpec((1, ts, LANES), lambda bi, i: (bi, i, 0))
    halo = pl.BlockSpec((1, w, LANES), lambda bi, i: (bi, jnp.maximum(i * r - 1, 0), 0))
    return pl.pallas_call(
        functools.partial(_swa_kernel, ts=ts),
        grid=(b, s // ts),
        in_specs=[pl.BlockSpec(memory_space=pltpu.SMEM),
                  pl.BlockSpec((1, ts, 4 * LANES), lambda bi, i: (bi, i, 0)),
                  main, main, main, main, halo, halo, halo, halo],
        out_specs=pl.BlockSpec((1, ts, 4 * LANES), lambda bi, i: (bi, i, 0)),
        out_shape=jax.ShapeDtypeStruct((b, s, 4 * LANES), BF16),
        compiler_params=pltpu.CompilerParams(dimension_semantics=("arbitrary",) * 2, vmem_limit_bytes=VMEM_LIMIT),
        name="swa_attn",
    )(sinks, q, ksa, ksb, vsa, vsb, ksa, ksb, vsa, vsb)


def _memkv_kernel(mem_ref, g_ref, w_ref, o_ref):
    mn = _rms(mem_ref[...], g_ref[...]).astype(BF16)
    o_ref[...] = jnp.dot(mn, w_ref[...], preferred_element_type=F32).astype(BF16)


def _memkv_call(mem2, g_mem, w_mem_kv):
    n, d = mem2.shape
    tm = min(256, n)
    return pl.pallas_call(
        _memkv_kernel,
        grid=(n // tm,),
        in_specs=[pl.BlockSpec((tm, d), lambda i: (i, 0)),
                  pl.BlockSpec(g_mem.shape, lambda i: (0, 0)),
                  pl.BlockSpec(w_mem_kv.shape, lambda i: (0, 0))],
        out_specs=pl.BlockSpec((tm, w_mem_kv.shape[1]), lambda i: (i, 0)),
        out_shape=jax.ShapeDtypeStruct((n, w_mem_kv.shape[1]), BF16),
        compiler_params=pltpu.CompilerParams(dimension_semantics=("arbitrary",), vmem_limit_bytes=VMEM_LIMIT),
        name="mem_kv",
    )(mem2, g_mem, w_mem_kv)


def _merge_kernel(x_ref, omla_ref, oswa_ref, qx_ref, gt_ref, kvm_ref, wmo_ref, wso_ref, wxo_ref, wout_ref,
                  gffn_ref, wr_ref, br_ref,
                  h_ref, hnp_ref, idx_ref, gate_ref, rank_ref, cnt_ref, run_ref, *, tm, sub):
    @pl.when(pl.program_id(0) == 0)
    def _():
        run_ref[...] = jnp.zeros(run_ref.shape, F32)

    d = x_ref.shape[1]
    kv_cols = XA_HEADS * XA_HEAD_DIM
    erow = lax.broadcasted_iota(I32, (N_EXPERTS, sub), 0)
    tri_t = (lax.broadcasted_iota(I32, (sub, sub), 0) < lax.broadcasted_iota(I32, (sub, sub), 1)).astype(BF16)
    nt = (((1,), (1,)), ((), ()))
    run = run_ref[...]
    for hf in range(tm // sub):
        rows = slice(hf * sub, (hf + 1) * sub)

        oxs = []
        for hd in range(XA_HEADS):
            sl = slice(hd * LANES, (hd + 1) * LANES)
            km = kvm_ref[0, :, sl]
            vm = kvm_ref[0, :, kv_cols + hd * LANES:kv_cols + (hd + 1) * LANES]
            s = lax.dot_general(qx_ref[rows, sl], km, nt, preferred_element_type=F32)
            p = jnp.exp2(s - jnp.max(s, axis=1, keepdims=True))
            den = jnp.sum(p, axis=1, keepdims=True)
            oxs.append((jnp.dot(p.astype(BF16), vm, preferred_element_type=F32) / den).astype(BF16))
        oxa = jnp.concatenate(oxs, axis=1)

        merged = (gt_ref[rows, 0:d].astype(F32) * jnp.dot(omla_ref[rows, :], wmo_ref[...], preferred_element_type=F32)
                  + gt_ref[rows, d:2 * d].astype(F32) * jnp.dot(oswa_ref[rows, :], wso_ref[...],
                                                                 preferred_element_type=F32)
                  + gt_ref[rows, 2 * d:3 * d].astype(F32) * jnp.dot(oxa, wxo_ref[...], preferred_element_type=F32))
        h = x_ref[rows, :] + jnp.dot(merged.astype(BF16), wout_ref[...], preferred_element_type=F32)
        h_ref[rows, :] = h

        hn = _rms(h, gffn_ref[...])
        hn_hi = hn.astype(BF16)
        hn_hi32 = hn_hi.astype(F32)
        hn_lo = (hn - hn_hi32).astype(BF16)
        bits = pltpu.bitcast(hn_hi32, U32)
        _store_slabs(hnp_ref, hf * sub, (bits[:, : d // 2] >> 16) | (bits[:, d // 2:] & jnp.uint32(0xFFFF0000)))

        part = lax.dot_general(wr_ref[...], hn_hi, nt, preferred_element_type=F32)
        logits_t = (part[:N_EXPERTS] + part[N_EXPERTS:]
                    + lax.dot_general(wr_ref[0:N_EXPERTS, :], hn_lo, nt, preferred_element_type=F32) + br_ref[...])

        work = logits_t
        vals, idxs, hots = [], [], []
        for _ in range(TOP_K):
            mx = jnp.max(work, axis=0, keepdims=True)
            ix = jnp.min(jnp.where(work == mx, erow, N_EXPERTS), axis=0, keepdims=True)
            hot = erow == ix
            work = jnp.where(hot, -jnp.inf, work)
            vals.append(mx)
            idxs.append(ix)
            hots.append(hot)
        es = [jnp.exp(v - vals[0]) for v in vals]
        den = es[0] + es[1] + es[2] + es[3]
        sel_t = (hots[0] | hots[1] | hots[2] | hots[3])
        prefix_t = jnp.dot(sel_t.astype(BF16), tri_t, preferred_element_type=F32) + run
        for k in range(TOP_K):
            idx_ref[k:k + 1, rows] = idxs[k]
            gate_ref[k:k + 1, rows] = es[k] / den
            rank_ref[k:k + 1, rows] = jnp.sum(jnp.where(hots[k], prefix_t, 0.0), axis=0, keepdims=True).astype(I32)
        run = run + jnp.sum(sel_t.astype(F32), axis=1, keepdims=True)
    run_ref[...] = run
    cnt_ref[...] = run.astype(I32)


def _merge_call(x2, omla, oswa, qx, gates, kvm, wmo, wso, wxo, wout, g_ffn, wr_split, b_router_col, seq):
    t, d = x2.shape
    tm = MERGE_TM
    per_b = seq // tm
    row = lambda n: pl.BlockSpec((tm, n), lambda i: (i, 0))
    col = lambda: pl.BlockSpec((TOP_K, tm), lambda i: (0, i))
    full = lambda a: pl.BlockSpec(a.shape, lambda i: (0,) * a.ndim)
    return pl.pallas_call(
        functools.partial(_merge_kernel, tm=tm, sub=MERGE_SUB),
        grid=(t // tm,),
        in_specs=[row(d), row(512), row(512), row(512), row(3 * d),
                  pl.BlockSpec((1,) + kvm.shape[1:], lambda i: (i // per_b, 0, 0)),
                  full(wmo), full(wso), full(wxo), full(wout), full(g_ffn), full(wr_split), full(b_router_col)],
        out_specs=[row(d), pl.BlockSpec((tm, d // 2 // LANES, LANES), lambda i: (i, 0, 0)), col(), col(), col(),
                   pl.BlockSpec((N_EXPERTS, 1), lambda i: (0, 0))],
        out_shape=[jax.ShapeDtypeStruct((t, d), F32), jax.ShapeDtypeStruct((t, d // 2 // LANES, LANES), U32),
                   jax.ShapeDtypeStruct((TOP_K, t), I32), jax.ShapeDtypeStruct((TOP_K, t), F32),
                   jax.ShapeDtypeStruct((TOP_K, t), I32), jax.ShapeDtypeStruct((N_EXPERTS, 1), I32)],
        scratch_shapes=[pltpu.VMEM((N_EXPERTS, 1), F32)],
        compiler_params=pltpu.CompilerParams(dimension_semantics=("arbitrary",), vmem_limit_bytes=VMEM_LIMIT),
        name="merge_router",
    )(x2, omla, oswa, qx, gates, kvm, wmo, wso, wxo, wout, g_ffn, wr_split, b_router_col)


SC_CORES = 2
SC_SUBCORES = 16
SC_WORKERS = SC_CORES * SC_SUBCORES
SC_CHUNK = 64


def _sc_mesh():
    return plsc.VectorSubcoreMesh(core_axis_name="c", subcore_axis_name="s",
                                  num_cores=SC_CORES, num_subcores=SC_SUBCORES)


def _sc_index_blocks(idx):
    n = idx.shape[0]
    per_w = n // SC_WORKERS
    n_ch = per_w // SC_CHUNK
    assert per_w * SC_WORKERS == n and n_ch * SC_CHUNK == per_w
    return idx.reshape(SC_WORKERS, n_ch, SC_CHUNK), per_w, n_ch


def _sc_two_buffer_loop(n_ch, load, store, bufs, load_sems, store_sems):
    assert n_ch % 2 == 0
    a, b = bufs
    la, lb = load_sems
    sa, sb = store_sems
    load(0, a, la).start()

    @pl.loop(0, n_ch, step=2)
    def _(j):
        load(j, a, la).wait()

        @pl.when(j > 0)
        def _():
            store(j - 1, b, sb).wait()

        load(j + 1, b, lb).start()
        store(j, a, sa).start()
        load(j + 1, b, lb).wait()
        store(j, a, sa).wait()

        @pl.when(j + 2 < n_ch)
        def _():
            load(j + 2, a, la).start()

        store(j + 1, b, sb).start()

    store(n_ch - 1, b, sb).wait()


def _sc_scratch(n_ch, row_shape, dtype):
    return [pltpu.VMEM((n_ch, SC_CHUNK), I32), pltpu.VMEM((SC_CHUNK,) + row_shape, dtype),
            pltpu.VMEM((SC_CHUNK,) + row_shape, dtype)] + [pltpu.SemaphoreType.DMA] * 4


def _sc_scatter_rows(src, idx, n_out):
    idx3, per_w, n_ch = _sc_index_blocks(idx)
    n_src = src.shape[0]
    assert n_src % per_w == 0

    @functools.partial(
        pl.kernel, mesh=_sc_mesh(),
        out_type=jax.ShapeDtypeStruct((n_out,) + src.shape[1:], src.dtype),
        scratch_types=_sc_scratch(n_ch, src.shape[1:], src.dtype),
        name="moe_dispatch_sc")
    def k(src_hbm, idx_hbm, out_hbm, idx_v, rows_a, rows_b, la, lb, sa, sb):
        wid = lax.axis_index("s") * SC_CORES + lax.axis_index("c")
        base = lax.rem(wid * per_w, n_src)
        pltpu.sync_copy(idx_hbm.at[wid], idx_v)

        def load(c, buf, sem):
            return pltpu.make_async_copy(src_hbm.at[pl.ds(base + c * SC_CHUNK, SC_CHUNK)], buf, sem)

        def store(c, buf, sem):
            return pltpu.make_async_copy(buf, out_hbm.at[idx_v.at[c]], sem)

        _sc_two_buffer_loop(n_ch, load, store, (rows_a, rows_b), (la, lb), (sa, sb))

    return k(src, idx3)


def _sc_gather_rows(table, idx):
    idx3, per_w, n_ch = _sc_index_blocks(idx)

    @functools.partial(
        pl.kernel, mesh=_sc_mesh(),
        out_type=jax.ShapeDtypeStruct((idx.shape[0],) + table.shape[1:], table.dtype),
        scratch_types=_sc_scratch(n_ch, table.shape[1:], table.dtype),
        name="moe_gather_sc")
    def k(table_hbm, idx_hbm, out_hbm, idx_v, rows_a, rows_b, la, lb, sa, sb):
        wid = lax.axis_index("s") * SC_CORES + lax.axis_index("c")
        base = wid * per_w
        pltpu.sync_copy(idx_hbm.at[wid], idx_v)

        def load(c, buf, sem):
            return pltpu.make_async_copy(table_hbm.at[idx_v.at[c]], buf, sem)

        def store(c, buf, sem):
            return pltpu.make_async_copy(buf, out_hbm.at[pl.ds(base + c * SC_CHUNK, SC_CHUNK)], sem)

        _sc_two_buffer_loop(n_ch, load, store, (rows_a, rows_b), (la, lb), (sa, sb))

    return k(table, idx3)


def _unpack_lo(w):
    return pltpu.bitcast(w << 16, F32)


def _unpack_hi(w):
    return pltpu.bitcast(w & jnp.uint32(0xFFFF0000), F32)


def _ffn_kernel(te_ref, xs_ref, wgu_hbm, bgu_ref, wd_hbm, bd_ref, y_ref, wgu_f32, wd_f32, wgu_bf, wd_bf, wsem):
    i = pl.program_id(0)
    n = pl.num_programs(0)
    n_used = te_ref[n]
    e = te_ref[i]
    first = (i == 0) | (e != te_ref[jnp.maximum(i - 1, 0)])
    slot = te_ref[2 * n + 1 + i] % 2
    nxt = te_ref[3 * n + 1 + i]

    def fetch(expert, s):
        return (pltpu.make_async_copy(wgu_hbm.at[expert], wgu_f32.at[s], wsem.at[0, s]),
                pltpu.make_async_copy(wd_hbm.at[expert], wd_f32.at[s], wsem.at[1, s]))

    @pl.when(i == 0)
    def _():
        for cp in fetch(e, slot):
            cp.start()

    @pl.when((i < n_used) & first)
    def _():
        for cp in fetch(e, slot):
            cp.wait()
        wgu_bf[...] = wgu_f32[slot].astype(BF16)
        wd_bf[...] = wd_f32[slot].astype(BF16)

        @pl.when(nxt >= 0)
        def _():
            for cp in fetch(nxt, 1 - slot):
                cp.start()

    valid = jnp.where(i < n_used, te_ref[n + 1 + i], 0)
    rows, n_slab, _ = xs_ref.shape
    half = n_slab * LANES
    for sb in range(rows // FFN_SUB):
        r0 = sb * FFN_SUB

        @pl.when(valid > r0)
        def _():
            w = _load_slabs(xs_ref, r0, FFN_SUB)
            w = jnp.where(lax.broadcasted_iota(I32, w.shape, 0) < valid - r0, w, jnp.uint32(0))
            x_lo = _unpack_lo(w).astype(BF16)
            x_hi = _unpack_hi(w).astype(BF16)
            gu = (jnp.dot(x_lo, wgu_bf[0:half, :], preferred_element_type=F32)
                  + jnp.dot(x_hi, wgu_bf[half:, :], preferred_element_type=F32) + bgu_ref[0])
            de = gu.shape[1] // 2
            x_glu = jnp.minimum(gu[:, :de], SWIGLU_LIMIT)
            x_lin = jnp.clip(gu[:, de:], -SWIGLU_LIMIT, SWIGLU_LIMIT)
            hdn = x_glu * jax.nn.sigmoid(SWIGLU_ALPHA * x_glu) * (x_lin + 1.0)
            y = jnp.dot(hdn.astype(BF16), wd_bf[...], preferred_element_type=F32) + bd_ref[0]
            bits = pltpu.bitcast(y.astype(BF16).astype(F32), U32)
            _store_slabs(y_ref, r0, (bits[:, :half] >> 16) | (bits[:, half:] & jnp.uint32(0xFFFF0000)))

        @pl.when(valid <= r0)
        def _():
            y_ref[r0:r0 + FFN_SUB] = jnp.zeros((FFN_SUB,) + y_ref.shape[1:], U32)


def _ffn_call(tile_table, xs, w_gate_up, b_gate_up, w_down, b_down):
    r, ns, lanes = xs.shape
    tm = FFN_TM
    ne, d, de2 = w_gate_up.shape
    return pl.pallas_call(
        _ffn_kernel,
        grid_spec=pltpu.PrefetchScalarGridSpec(
            num_scalar_prefetch=1,
            grid=(r // tm,),
            in_specs=[pl.BlockSpec((tm, ns, lanes),
                                   lambda i, te: (jnp.minimum(i, jnp.maximum(te[r // tm] - 1, 0)), 0, 0)),
                      pl.BlockSpec(memory_space=pl.ANY),
                      pl.BlockSpec((1, 1, de2), lambda i, te: (te[i], 0, 0)),
                      pl.BlockSpec(memory_space=pl.ANY),
                      pl.BlockSpec((1, 1, d), lambda i, te: (te[i], 0, 0))],
            out_specs=pl.BlockSpec((tm, ns, lanes), lambda i, te: (i, 0, 0)),
            scratch_shapes=[pltpu.VMEM((2, d, de2), F32), pltpu.VMEM((2, de2 // 2, d), F32),
                            pltpu.VMEM((d, de2), BF16), pltpu.VMEM((de2 // 2, d), BF16),
                            pltpu.SemaphoreType.DMA((2, 2))]),
        out_shape=jax.ShapeDtypeStruct((r, ns, lanes), U32),
        compiler_params=pltpu.CompilerParams(dimension_semantics=("arbitrary",), vmem_limit_bytes=VMEM_LIMIT),
        name="moe_ffn",
    )(tile_table, xs, w_gate_up, b_gate_up, w_down, b_down)


def _combine_dense_kernel(h_ref, gate_ref, gfin_ref, y0_ref, y1_ref, y2_ref, y3_ref, o_ref):
    half = y0_ref.shape[1] * y0_ref.shape[2]
    lo = h_ref[:, :half]
    hi = h_ref[:, half:]
    for k, y_ref in enumerate((y0_ref, y1_ref, y2_ref, y3_ref)):
        g = gate_ref[:, k:k + 1]
        w = _load_slabs(y_ref)
        lo = lo + g * _unpack_lo(w)
        hi = hi + g * _unpack_hi(w)
    ms = (jnp.sum(lo * lo, axis=1, keepdims=True) + jnp.sum(hi * hi, axis=1, keepdims=True)) / (2 * half)
    inv = lax.rsqrt(ms + RMS_EPS)
    o_ref[:, :half] = lo * inv * gfin_ref[:, :half]
    o_ref[:, half:] = hi * inv * gfin_ref[:, half:]


def _combine_dense_call(h, gate, g_final, yk):
    t, d = h.shape
    tm = COMBINE_TM
    per_k = t // tm
    yspec = lambda k: pl.BlockSpec((tm,) + yk.shape[1:], lambda i: (k * per_k + i, 0, 0))
    return pl.pallas_call(
        _combine_dense_kernel,
        grid=(t // tm,),
        in_specs=[pl.BlockSpec((tm, d), lambda i: (i, 0)),
                  pl.BlockSpec((tm, TOP_K), lambda i: (i, 0)),
                  pl.BlockSpec((1, d), lambda i: (0, 0)),
                  yspec(0), yspec(1), yspec(2), yspec(3)],
        out_specs=pl.BlockSpec((tm, d), lambda i: (i, 0)),
        out_shape=jax.ShapeDtypeStruct((t, d), F32),
        compiler_params=pltpu.CompilerParams(dimension_semantics=("arbitrary",), vmem_limit_bytes=VMEM_LIMIT),
        name="moe_combine",
    )(h, gate, g_final, yk, yk, yk, yk)


def _rope_freqs():
    def inv_freq(dh):
        return (ROPE_THETA ** (-jnp.arange(0, dh, 2, dtype=F32) / dh))[:, None]

    return inv_freq(MLA_ROPE), inv_freq(SWA_HEAD_DIM)


def _winprep_kernel(w_ref, o_ref):
    w = w_ref[...]
    rb = w.shape[0]
    c1 = MLA_Q_RANK + MLA_KV_RANK
    tail = w[:, c1 + MLA_ROPE:]
    lane = lax.broadcasted_iota(I32, (rb, LANES), 1)
    in_rope = (lane >= MLA_NOPE) & (lane < MLA_NOPE + MLA_ROPE)
    kr_p = jnp.where(in_rope, pltpu.roll(w[:, c1:c1 + LANES], MLA_NOPE, axis=1), 0.0)
    pieces = [w[:, :c1], kr_p, tail]
    off = 0
    for pc in pieces:
        o_ref[:, off:off + pc.shape[1]] = pc.astype(BF16)
        off += pc.shape[1]


def _winprep_call(w_in, layer):
    _, d, n = w_in.shape
    rb = 128
    return pl.pallas_call(
        _winprep_kernel,
        grid=(d // rb,),
        in_specs=[pl.BlockSpec((None, rb, n), lambda i: (layer, i, 0))],
        out_specs=pl.BlockSpec((rb, _D1), lambda i: (i, 0)),
        out_shape=jax.ShapeDtypeStruct((d, _D1), BF16),
        compiler_params=pltpu.CompilerParams(dimension_semantics=("arbitrary",), vmem_limit_bytes=VMEM_LIMIT),
        name="w_in_prep",
    )(w_in)


def _prep_weights(w_in, layer, w_mla_uq, w_mla_ukv):
    w_in_al = _winprep_call(w_in, layer)

    r = w_mla_uq.shape[0]
    wq = w_mla_uq.reshape(r, MLA_HEADS, MLA_NOPE + MLA_ROPE)
    zq = jnp.zeros((r, MLA_HEADS, LANES - MLA_NOPE - MLA_ROPE), w_mla_uq.dtype)
    wq_pad = jnp.concatenate([wq, zq], axis=-1).reshape(r, MLA_HEADS * LANES).astype(BF16)

    rk = w_mla_ukv.shape[0]
    wkv = w_mla_ukv.reshape(rk, MLA_HEADS, MLA_NOPE + MLA_V)
    wk_aug = jnp.concatenate([wkv[..., :MLA_NOPE], jnp.zeros((rk, MLA_HEADS, LANES - MLA_NOPE), w_mla_ukv.dtype)],
                             axis=-1).reshape(rk, MLA_HEADS * LANES).astype(BF16)
    wv_t = wkv[..., MLA_NOPE:].reshape(rk, MLA_HEADS * MLA_V).T.astype(BF16)
    return w_in_al, wq_pad, wk_aug, wv_t


def kernel(x, mem, positions, g_mix, w_in, g_mla_q, w_mla_uq, g_mla_kv, w_mla_ukv, w_mla_o, swa_sinks, w_swa_o,
           g_mem, w_mem_kv, w_xa_o, b_gate, w_out, g_ffn, w_router, b_router, w_gate_up, b_gate_up, w_down,
           b_down, g_final):
    b, s, d = x.shape
    t = b * s
    depth = g_mix.shape[0]
    h = x.reshape(t, d)
    pos = positions.astype(F32).reshape(1, t)
    fq, f64 = _rope_freqs()
    for l in range(depth):
        w_in_al, wq_pad, wk_aug, wv_t = _prep_weights(w_in, l, w_mla_uq[l], w_mla_ukv[l])
        (qm, km, vmt, qs, ksa, ksb, vsa, vsb, qx, gates) = _proj_call(
            h, pos, fq, f64, g_mix[l][None], w_in_al, g_mla_q[l][None], wq_pad,
            g_mla_kv[l][None], wk_aug, wv_t, b_gate[l][None])
        r3 = lambda a: a.reshape(b, s, a.shape[1])
        omla = _mla_call(r3(qm), r3(km), vmt, s).reshape(t, -1)
        oswa = _swa_call(swa_sinks[l], r3(qs), r3(ksa), r3(ksb), r3(vsa), r3(vsb)).reshape(t, -1)
        m = mem.shape[1]
        kvm = _memkv_call(mem.reshape(b * m, d), g_mem[l][None], w_mem_kv[l].astype(BF16)).reshape(b, m, -1)
        wr_t = w_router[l].T
        wr_hi = wr_t.astype(BF16)
        wr_split = jnp.concatenate([wr_hi, (wr_t - wr_hi.astype(F32)).astype(BF16)], axis=0)
        h_mid, hnp, idx, gate, rank, counts = _merge_call(
            h, omla, oswa, qx, gates, kvm, w_mla_o[l].astype(BF16), w_swa_o[l].astype(BF16),
            w_xa_o[l].astype(BF16), w_out[l].astype(BF16), g_ffn[l][None], wr_split, b_router[l][:, None], s)

        counts = counts[:, 0]
        padded = ((counts + FFN_TM - 1) // FFN_TM) * FFN_TM
        padded_end = jnp.cumsum(padded)
        offsets = padded_end - padded
        experts = jnp.arange(N_EXPERTS, dtype=I32)
        dest = (jnp.sum(jnp.where(idx[..., None] == experts, offsets, 0), axis=-1) + rank).reshape(-1).astype(I32)
        n_tiles = (t * TOP_K) // FFN_TM + N_EXPERTS
        n_used = (padded_end[-1] // FFN_TM).astype(I32)
        tile_start = jnp.minimum(jnp.arange(n_tiles, dtype=I32), jnp.maximum(n_used - 1, 0)) * FFN_TM
        tile_expert = jnp.sum((padded_end[None, :] <= tile_start[:, None]).astype(I32), axis=1)
        tile_expert = jnp.minimum(tile_expert, N_EXPERTS - 1)
        onehot = tile_expert[:, None] == experts[None, :]
        pick = lambda v: jnp.sum(jnp.where(onehot, v[None, :], 0), axis=1)
        tile_valid = jnp.clip(pick(counts) - (tile_start - pick(offsets)), 0, FFN_TM)
        nonempty = padded > 0
        tile_group = pick(jnp.cumsum(nonempty.astype(I32)) - 1)
        later = jnp.where(nonempty[None, :] & (experts[None, :] > experts[:, None]), experts[None, :], N_EXPERTS)
        next_expert = jnp.min(later, axis=1)
        tile_next = pick(jnp.where(next_expert < N_EXPERTS, next_expert, -1))
        te = jnp.concatenate([tile_expert, n_used[None], tile_valid.astype(I32), tile_group.astype(I32),
                              tile_next.astype(I32)])

        xs = _sc_scatter_rows(hnp, dest, n_tiles * FFN_TM)
        y = _ffn_call(te, xs, w_gate_up[l], b_gate_up[l][:, None, :], w_down[l], b_down[l][:, None, :])
        yk = _sc_gather_rows(y, dest)
        if l == depth - 1:
            gfin = g_final[None]
            out = _combine_dense_call(h_mid, gate.T, gfin, yk)
        else:
            raise NotImplementedError("depth > 1 needs a combine without the final norm")
        h = out
    return h.reshape(b, s, d)
```

```python
import functools
import math

import jax
import jax.numpy as jnp
from jax import lax
from jax.experimental import pallas as pl
from jax.experimental.pallas import tpu as pltpu
from jax.experimental.pallas import tpu_sc as plsc

F32 = jnp.float32
BF16 = jnp.bfloat16
U32 = jnp.uint32
I32 = jnp.int32

LANES = 128
ROPE_THETA = 10000.0
RMS_EPS = 1e-6
LOG2E = 1.4426950408889634

MLA_HEADS = 8
MLA_NOPE = 64
MLA_ROPE = 32
MLA_V = 64
MLA_Q_RANK = 256
MLA_KV_RANK = 128
SWA_HEADS = 8
SWA_KV_HEADS = 2
SWA_HEAD_DIM = 64
SWA_WINDOW = 128
XA_HEADS = 4
XA_HEAD_DIM = 128
N_EXPERTS = 32
TOP_K = 4
SWIGLU_ALPHA = 1.702
SWIGLU_LIMIT = 7.0
N_BRANCHES = 3

NEG_BIG = -1e30

PROJ_TM = 512
PROJ_SUB = 256
MLA_TQ = 1024
MLA_ONES_ROWS = 16
SWA_TS = 512
MERGE_TM = 512
MERGE_SUB = 256
FFN_TM = 1024
FFN_SUB = 256
COMBINE_TM = 256

VMEM_LIMIT = 56 * 1024 * 1024


def _rms(x, g):
    return x * lax.rsqrt(jnp.mean(x * x, axis=-1, keepdims=True) + RMS_EPS) * g


def _store_slabs(ref, row0, value):
    rows, n, _ = ref.shape
    flat = ref.reshape(rows * n, LANES)
    for c in range(n):
        flat[pl.ds(row0 * n + c, value.shape[0], stride=n), :] = value[:, c * LANES:(c + 1) * LANES]


def _load_slabs(ref, row0=0, m=None):
    rows, n, _ = ref.shape
    m = rows if m is None else m
    flat = ref.reshape(rows * n, LANES)
    return jnp.concatenate([flat[pl.ds(row0 * n + c, m, stride=n), :] for c in range(n)], axis=1)


_A0, _A1 = 0, 512
_B0, _B1 = 512, 1280
_C0, _C1 = 1280, 1792
_D0, _D1 = 1792, 4864


def _rotate_half(x, d, lo, hi):
    n = x.shape[1]
    half = (hi - lo) // 2
    lane = lax.broadcasted_iota(I32, x.shape, 1) % d
    up = pltpu.roll(x, n - half, axis=1)
    dn = pltpu.roll(x, half, axis=1)
    return jnp.where((lane >= lo) & (lane < lo + half), -up, jnp.where((lane >= lo + half) & (lane < hi), dn, 0.0))


def _proj_kernel(x_ref, pos_ref, fq_ref, f64_ref, gmix_ref, win_ref, gq_ref, wq_ref,
                 gkv_ref, wk_ref, wv_ref, bgate_ref,
                 qm_ref, km_ref, vm_ref, qs_ref, ksa_ref, ksb_ref, vsa_ref, vsb_ref, qx_ref, gt_ref):
    for hf in range(x_ref.shape[0] // PROJ_SUB):
        rows = slice(hf * PROJ_SUB, (hf + 1) * PROJ_SUB)
        _proj_rows(x_ref[rows, :], pos_ref[:, rows], fq_ref, f64_ref, gmix_ref, win_ref, gq_ref, wq_ref, gkv_ref,
                   wk_ref, wv_ref, bgate_ref,
                   [r.at[rows] for r in (qm_ref, km_ref)], vm_ref.at[:, rows],
                   [r.at[rows] for r in (qs_ref, ksa_ref, ksb_ref, vsa_ref, vsb_ref, qx_ref, gt_ref)])


def _proj_rows(x, pos, fq_ref, f64_ref, gmix_ref, win_ref, gq_ref, wq_ref, gkv_ref, wk_ref, wv_ref, bgate_ref,
               mla_refs, vm_ref, other_refs):
    qm_ref, km_ref = mla_refs
    qs_ref, ksa_ref, ksb_ref, vsa_ref, vsb_ref, qx_ref, gt_ref = other_refs
    xn = _rms(x, gmix_ref[...]).astype(BF16)
    tm = x.shape[0]
    ang16 = fq_ref[...] * pos
    ang32 = f64_ref[...] * pos
    c16, s16, c32, s32 = jnp.cos(ang16), jnp.sin(ang16), jnp.cos(ang32), jnp.sin(ang32)
    one = jnp.ones((MLA_NOPE, tm), F32)
    zero = jnp.zeros((MLA_NOPE, tm), F32)
    pad = LANES - MLA_NOPE - MLA_ROPE
    cosq = jnp.concatenate([one, c16, c16, one[:pad]], axis=0).T
    sinq = jnp.concatenate([zero, s16, s16, zero[:pad]], axis=0).T
    cos64 = jnp.concatenate([c32, c32, c32, c32], axis=0).T
    sin64 = jnp.concatenate([s32, s32, s32, s32], axis=0).T
    rope_lo, rope_hi = MLA_NOPE, MLA_NOPE + MLA_ROPE

    xa = jnp.dot(xn, win_ref[:, _A0:_A1], preferred_element_type=F32)
    cqn = _rms(xa[:, 0:256], gq_ref[...]).astype(BF16)
    qa = jnp.dot(cqn, wq_ref[...], preferred_element_type=F32)
    qb = _rotate_half(qa, LANES, rope_lo, rope_hi)
    q_scale = LOG2E / math.sqrt(MLA_NOPE + MLA_ROPE)
    ckvn = _rms(xa[:, 256:384], gkv_ref[...]).astype(BF16)
    ka = jnp.dot(ckvn, wk_ref[...], preferred_element_type=F32)
    kr = xa[:, 384:512]
    krope = kr * cosq + _rotate_half(kr, LANES, rope_lo, rope_hi) * sinq
    for h in range(MLA_HEADS):
        sl = slice(h * LANES, (h + 1) * LANES)
        qm_ref[:, sl] = ((qa[:, sl] * cosq + qb[:, sl] * sinq) * q_scale).astype(BF16)
        km_ref[:, sl] = (ka[:, sl] + krope).astype(BF16)
    vm_ref[...] = lax.dot_general(wv_ref[...], ckvn, (((1,), (1,)), ((), ())),
                                  preferred_element_type=F32).astype(BF16)

    xb = jnp.dot(xn, win_ref[:, _B0:_B1], preferred_element_type=F32)
    s_scale = LOG2E / math.sqrt(SWA_HEAD_DIM)
    nq = SWA_HEADS * SWA_HEAD_DIM
    qs = xb[:, :nq]
    qs_rot = _rotate_half(qs, SWA_HEAD_DIM, 0, SWA_HEAD_DIM)
    for p in range(SWA_HEADS // 2):
        sl = slice(p * LANES, (p + 1) * LANES)
        qs_ref[:, sl] = ((qs[:, sl] * cos64 + qs_rot[:, sl] * sin64) * s_scale).astype(BF16)
    ks = xb[:, nq:nq + LANES]
    ks = ks * cos64 + _rotate_half(ks, SWA_HEAD_DIM, 0, SWA_HEAD_DIM) * sin64
    ksa_ref[...] = ks.astype(BF16)
    ksb_ref[...] = pltpu.roll(ks, 64, axis=1).astype(BF16)
    vs = xb[:, nq + LANES:nq + 2 * LANES]
    vsa_ref[...] = vs.astype(BF16)
    vsb_ref[...] = pltpu.roll(vs, 64, axis=1).astype(BF16)

    xc = jnp.dot(xn, win_ref[:, _C0:_C1], preferred_element_type=F32)
    qx_ref[...] = (xc * (LOG2E / math.sqrt(XA_HEAD_DIM))).astype(BF16)

    xd = jnp.dot(xn, win_ref[:, _D0:_D1], preferred_element_type=F32) + bgate_ref[...]
    gt_ref[...] = jax.nn.sigmoid(xd).astype(BF16)


def _proj_call(x2, pos, fq, f64, g_mix, w_in_al, g_q, wq, g_kv, wk_aug, wv, b_gate):
    t, d = x2.shape
    tm = PROJ_TM
    row = lambda n: pl.BlockSpec((tm, n), lambda i: (i, 0))
    full = lambda a: pl.BlockSpec(a.shape, lambda i: (0,) * a.ndim)
    out_cols = [1024, 1024, 512, 128, 128, 128, 128, 512, 3072]
    out_specs = [row(n) for n in out_cols]
    out_shape = [jax.ShapeDtypeStruct((t, n), BF16) for n in out_cols]
    vt_rows = wv.shape[0]
    out_specs.insert(2, pl.BlockSpec((vt_rows, tm), lambda i: (0, i)))
    out_shape.insert(2, jax.ShapeDtypeStruct((vt_rows, t), BF16))
    return pl.pallas_call(
        _proj_kernel,
        grid=(t // tm,),
        in_specs=[row(d), pl.BlockSpec((1, tm), lambda i: (0, i)), full(fq), full(f64), full(g_mix),
                  pl.BlockSpec(w_in_al.shape, lambda i: (0, 0), pipeline_mode=pl.Buffered(1)),
                  full(g_q), full(wq), full(g_kv), full(wk_aug), full(wv), full(b_gate)],
        out_specs=out_specs,
        out_shape=out_shape,
        compiler_params=pltpu.CompilerParams(dimension_semantics=("arbitrary",), vmem_limit_bytes=VMEM_LIMIT),
        name="proj",
    )(x2, pos, fq, f64, g_mix, w_in_al, g_q, wq, g_kv, wk_aug, wv, b_gate)


def _mla_kernel(q_ref, k_ref, vt_ref, o_ref, sa_ref, sb_ref, ma_ref, mb_ref, m_ref, acc_ref, *, tq):
    i = pl.program_id(2)
    m_ref[...] = jnp.full(m_ref.shape, NEG_BIG, F32)
    acc_ref[...] = jnp.zeros(acc_ref.shape, F32)
    ones = jnp.ones((MLA_ONES_ROWS, tq), BF16)

    def scores(j, bufs):
        s_ref, mx_ref = bufs
        k0 = pl.multiple_of(j * tq, tq)
        for hh in range(2):
            sl = slice(hh * LANES, (hh + 1) * LANES)
            st = lax.dot_general(k_ref[0, pl.ds(k0, tq), sl], q_ref[0, :, sl], (((1,), (1,)), ((), ())),
                                 preferred_element_type=F32)
            s_ref[hh] = st
            mx_ref[hh] = jnp.max(st, axis=0, keepdims=True)

    def update(j, bufs, masked):
        s_ref, mx_ref = bufs
        k0 = pl.multiple_of(j * tq, tq)
        for hh in range(2):
            vt = jnp.concatenate([vt_ref[hh * MLA_V:(hh + 1) * MLA_V, pl.ds(k0, tq)], ones], axis=0)
            st = s_ref[hh]
            if masked:
                kj = lax.broadcasted_iota(I32, (tq, tq), 0)
                qi = lax.broadcasted_iota(I32, (tq, tq), 1)
                st = jnp.where(kj <= qi, st, NEG_BIG)
                m_tile = jnp.max(st, axis=0, keepdims=True)
            else:
                m_tile = mx_ref[hh]
            m_old = m_ref[hh]
            m_new = jnp.maximum(m_old, m_tile)
            alpha = jnp.exp2(m_old - m_new)
            pt = jnp.exp2(st - m_new)
            acc_ref[hh] = alpha * acc_ref[hh] + jnp.dot(vt, pt.astype(BF16), preferred_element_type=F32)
            m_ref[hh] = m_new

    buf_a = (sa_ref, ma_ref)
    buf_b = (sb_ref, mb_ref)
    scores(0, buf_a)

    def body(jj, carry):
        scores(2 * jj + 1, buf_b)
        update(2 * jj, buf_a, False)
        scores(2 * jj + 2, buf_a)
        update(2 * jj + 1, buf_b, False)
        return carry

    lax.fori_loop(0, i // 2, body, 0)

    @pl.when(i % 2 == 0)
    def _():
        update(i, buf_a, True)

    @pl.when(i % 2 == 1)
    def _():
        scores(i, buf_b)
        update(i - 1, buf_a, False)
        update(i, buf_b, True)

    ot = jnp.concatenate([acc_ref[hh, :MLA_V] / acc_ref[hh, MLA_V:MLA_V + 1] for hh in range(2)],
                         axis=0)
    o_ref[0] = ot.T.astype(BF16)


def _mla_call(q, k, vt, seq):
    b, s, _ = q.shape
    assert s == seq
    tq = min(MLA_TQ, s)
    n_pairs = MLA_HEADS // 2
    return pl.pallas_call(
        functools.partial(_mla_kernel, tq=tq),
        grid=(b, n_pairs, s // tq),
        in_specs=[pl.BlockSpec((1, tq, 2 * LANES), lambda bi, hp, i: (bi, i, hp)),
                  pl.BlockSpec((1, s, 2 * LANES), lambda bi, hp, i: (bi, 0, hp)),
                  pl.BlockSpec((2 * MLA_V, s), lambda bi, hp, i: (hp, bi))],
        out_specs=pl.BlockSpec((1, tq, LANES), lambda bi, hp, i: (bi, i, hp)),
        out_shape=jax.ShapeDtypeStruct((b, s, n_pairs * LANES), BF16),
        scratch_shapes=[pltpu.VMEM((2, tq, tq), F32), pltpu.VMEM((2, tq, tq), F32),
                        pltpu.VMEM((2, 1, tq), F32), pltpu.VMEM((2, 1, tq), F32),
                        pltpu.VMEM((2, 1, tq), F32),
                        pltpu.VMEM((2, MLA_V + MLA_ONES_ROWS, tq), F32)],
        compiler_params=pltpu.CompilerParams(dimension_semantics=("arbitrary",) * 3, vmem_limit_bytes=VMEM_LIMIT),
        name="mla_attn",
    )(q, k, vt)


def _swa_kernel(sink_ref, q_ref, ka_ref, kb_ref, va_ref, vb_ref, kah_ref, kbh_ref, vah_ref, vbh_ref, o_ref, *, ts):
    w = SWA_WINDOW
    i = pl.program_id(1)
    ka = jnp.concatenate([kah_ref[0], ka_ref[0]], axis=0)
    kb = jnp.concatenate([kbh_ref[0], kb_ref[0]], axis=0)
    va = jnp.concatenate([vah_ref[0], va_ref[0]], axis=0)
    vb = jnp.concatenate([vbh_ref[0], vb_ref[0]], axis=0)
    lane_k = lax.broadcasted_iota(I32, (2 * w, LANES), 1)
    low = lane_k < SWA_HEAD_DIM
    qi = lax.broadcasted_iota(I32, (2 * w, 2 * w), 0) % w
    kj = lax.broadcasted_iota(I32, (2 * w, 2 * w), 1)
    diff = qi + w - kj
    band = (diff >= 0) & (diff < w)
    lane_o = lax.broadcasted_iota(I32, (w, LANES), 1)
    row2 = lax.broadcasted_iota(I32, (2 * w, 1), 0)
    zero = jnp.zeros((), BF16)
    stacks = ((0, ka, True, va), (1, kb, False, vb), (4, kb, True, vb), (5, ka, False, va))
    for n in range(ts // w):
        mask = band & ((i * (ts // w) + n > 0) | (kj >= w))
        res = []
        for h0, ksrc, keep_low, vsrc in stacks:
            p0 = h0 // 2
            q = jnp.concatenate([q_ref[0, n * w:(n + 1) * w, p0 * LANES:(p0 + 1) * LANES],
                                 q_ref[0, n * w:(n + 1) * w, (p0 + 1) * LANES:(p0 + 2) * LANES]], axis=0)
            kwin = ksrc[n * w:n * w + 2 * w]
            kwin = jnp.where(low if keep_low else ~low, kwin, zero)
            vwin = vsrc[n * w:n * w + 2 * w]
            s = lax.dot_general(q, kwin, (((1,), (1,)), ((), ())), preferred_element_type=F32)
            s = jnp.where(mask, s, NEG_BIG)
            sink = jnp.where(row2 < w, sink_ref[h0], sink_ref[h0 + 2]) * LOG2E
            m = jnp.maximum(jnp.max(s, axis=1, keepdims=True), sink)
            p = jnp.exp2(s - m)
            den = jnp.sum(p, axis=1, keepdims=True) + jnp.exp2(sink - m)
            o = jnp.dot(p.astype(BF16), vwin, preferred_element_type=F32) / den
            res.append(o)
        o02, o13, o46, o57 = res
        sel = lane_o < SWA_HEAD_DIM
        rows = slice(n * w, (n + 1) * w)
        o_ref[0, rows, 0 * LANES:1 * LANES] = jnp.where(sel, o02[:w], o13[:w]).astype(BF16)
        o_ref[0, rows, 1 * LANES:2 * LANES] = jnp.where(sel, o02[w:], o13[w:]).astype(BF16)
        o_ref[0, rows, 2 * LANES:3 * LANES] = jnp.where(sel, o46[:w], o57[:w]).astype(BF16)
        o_ref[0, rows, 3 * LANES:4 * LANES] = jnp.where(sel, o46[w:], o57[w:]).astype(BF16)


def _swa_call(sinks, q, ksa, ksb, vsa, vsb):
    b, s, _ = q.shape
    ts = min(SWA_TS, s)
    w = SWA_WINDOW
    r = ts // w
    main = pl.BlockSpec((1, ts, LANES), lambda bi, i: (bi, i, 0))
    halo = pl.BlockSpec((1, w, LANES), lambda bi, i: (bi, jnp.maximum(i * r - 1, 0), 0))
    return pl.pallas_call(
        functools.partial(_swa_kernel, ts=ts),
        grid=(b, s // ts),
        in_specs=[pl.BlockSpec(memory_space=pltpu.SMEM),
                  pl.BlockSpec((1, ts, 4 * LANES), lambda bi, i: (bi, i, 0)),
                  main, main, main, main, halo, halo, halo, halo],
        out_specs=pl.BlockSpec((1, ts, 4 * LANES), lambda bi, i: (bi, i, 0)),
        out_shape=jax.ShapeDtypeStruct((b, s, 4 * LANES), BF16),
        compiler_params=pltpu.CompilerParams(dimension_semantics=("arbitrary",) * 2, vmem_limit_bytes=VMEM_LIMIT),
        name="swa_attn",
    )(sinks, q, ksa, ksb, vsa, vsb, ksa, ksb, vsa, vsb)


def _memkv_kernel(mem_ref, g_ref, w_ref, o_ref):
    mn = _rms(mem_ref[...], g_ref[...]).astype(BF16)
    o_ref[...] = jnp.dot(mn, w_ref[...], preferred_element_type=F32).astype(BF16)


def _memkv_call(mem2, g_mem, w_mem_kv):
    n, d = mem2.shape
    tm = min(256, n)
    return pl.pallas_call(
        _memkv_kernel,
        grid=(n // tm,),
        in_specs=[pl.BlockSpec((tm, d), lambda i: (i, 0)),
                  pl.BlockSpec(g_mem.shape, lambda i: (0, 0)),
                  pl.BlockSpec(w_mem_kv.shape, lambda i: (0, 0))],
        out_specs=pl.BlockSpec((tm, w_mem_kv.shape[1]), lambda i: (i, 0)),
        out_shape=jax.ShapeDtypeStruct((n, w_mem_kv.shape[1]), BF16),
        compiler_params=pltpu.CompilerParams(dimension_semantics=("arbitrary",), vmem_limit_bytes=VMEM_LIMIT),
        name="mem_kv",
    )(mem2, g_mem, w_mem_kv)


def _merge_kernel(x_ref, omla_ref, oswa_ref, qx_ref, gt_ref, kvm_ref, wmo_ref, wso_ref, wxo_ref, wout_ref,
                  gffn_ref, wr_ref, br_ref,
                  h_ref, hnp_ref, idx_ref, gate_ref, rank_ref, cnt_ref, run_ref, *, tm, sub):
    @pl.when(pl.program_id(0) == 0)
    def _():
        run_ref[...] = jnp.zeros(run_ref.shape, F32)

    d = x_ref.shape[1]
    kv_cols = XA_HEADS * XA_HEAD_DIM
    erow = lax.broadcasted_iota(I32, (N_EXPERTS, sub), 0)
    tri_t = (lax.broadcasted_iota(I32, (sub, sub), 0) < lax.broadcasted_iota(I32, (sub, sub), 1)).astype(BF16)
    nt = (((1,), (1,)), ((), ()))
    run = run_ref[...]
    for hf in range(tm // sub):
        rows = slice(hf * sub, (hf + 1) * sub)

        oxs = []
        for hd in range(XA_HEADS):
            sl = slice(hd * LANES, (hd + 1) * LANES)
            km = kvm_ref[0, :, sl]
            vm = kvm_ref[0, :, kv_cols + hd * LANES:kv_cols + (hd + 1) * LANES]
            s = lax.dot_general(qx_ref[rows, sl], km, nt, preferred_element_type=F32)
            p = jnp.exp2(s - jnp.max(s, axis=1, keepdims=True))
            den = jnp.sum(p, axis=1, keepdims=True)
            oxs.append((jnp.dot(p.astype(BF16), vm, preferred_element_type=F32) / den).astype(BF16))
        oxa = jnp.concatenate(oxs, axis=1)

        merged = (gt_ref[rows, 0:d].astype(F32) * jnp.dot(omla_ref[rows, :], wmo_ref[...], preferred_element_type=F32)
                  + gt_ref[rows, d:2 * d].astype(F32) * jnp.dot(oswa_ref[rows, :], wso_ref[...],
                                                                 preferred_element_type=F32)
                  + gt_ref[rows, 2 * d:3 * d].astype(F32) * jnp.dot(oxa, wxo_ref[...], preferred_element_type=F32))
        h = x_ref[rows, :] + jnp.dot(merged.astype(BF16), wout_ref[...], preferred_element_type=F32)
        h_ref[rows, :] = h

        hn = _rms(h, gffn_ref[...])
        hn_hi = hn.astype(BF16)
        hn_hi32 = hn_hi.astype(F32)
        hn_lo = (hn - hn_hi32).astype(BF16)
        bits = pltpu.bitcast(hn_hi32, U32)
        _store_slabs(hnp_ref, hf * sub, (bits[:, : d // 2] >> 16) | (bits[:, d // 2:] & jnp.uint32(0xFFFF0000)))

        part = lax.dot_general(wr_ref[...], hn_hi, nt, preferred_element_type=F32)
        logits_t = (part[:N_EXPERTS] + part[N_EXPERTS:]
                    + lax.dot_general(wr_ref[0:N_EXPERTS, :], hn_lo, nt, preferred_element_type=F32) + br_ref[...])

        work = logits_t
        vals, idxs, hots = [], [], []
        for _ in range(TOP_K):
            mx = jnp.max(work, axis=0, keepdims=True)
            ix = jnp.min(jnp.where(work == mx, erow, N_EXPERTS), axis=0, keepdims=True)
            hot = erow == ix
            work = jnp.where(hot, -jnp.inf, work)
            vals.append(mx)
            idxs.append(ix)
            hots.append(hot)
        es = [jnp.exp(v - vals[0]) for v in vals]
        den = es[0] + es[1] + es[2] + es[3]
        sel_t = (hots[0] | hots[1] | hots[2] | hots[3])
        prefix_t = jnp.dot(sel_t.astype(BF16), tri_t, preferred_element_type=F32) + run
        for k in range(TOP_K):
            idx_ref[k:k + 1, rows] = idxs[k]
            gate_ref[k:k + 1, rows] = es[k] / den
            rank_ref[k:k + 1, rows] = jnp.sum(jnp.where(hots[k], prefix_t, 0.0), axis=0, keepdims=True).astype(I32)
        run = run + jnp.sum(sel_t.astype(F32), axis=1, keepdims=True)
    run_ref[...] = run
    cnt_ref[...] = run.astype(I32)


def _merge_call(x2, omla, oswa, qx, gates, kvm, wmo, wso, wxo, wout, g_ffn, wr_split, b_router_col, seq):
    t, d = x2.shape
    tm = MERGE_TM
    per_b = seq // tm
    row = lambda n: pl.BlockSpec((tm, n), lambda i: (i, 0))
    col = lambda: pl.BlockSpec((TOP_K, tm), lambda i: (0, i))
    full = lambda a: pl.BlockSpec(a.shape, lambda i: (0,) * a.ndim)
    return pl.pallas_call(
        functools.partial(_merge_kernel, tm=tm, sub=MERGE_SUB),
        grid=(t // tm,),
        in_specs=[row(d), row(512), row(512), row(512), row(3 * d),
                  pl.BlockSpec((1,) + kvm.shape[1:], lambda i: (i // per_b, 0, 0)),
                  full(wmo), full(wso), full(wxo), full(wout), full(g_ffn), full(wr_split), full(b_router_col)],
        out_specs=[row(d), pl.BlockSpec((tm, d // 2 // LANES, LANES), lambda i: (i, 0, 0)), col(), col(), col(),
                   pl.BlockSpec((N_EXPERTS, 1), lambda i: (0, 0))],
        out_shape=[jax.ShapeDtypeStruct((t, d), F32), jax.ShapeDtypeStruct((t, d // 2 // LANES, LANES), U32),
                   jax.ShapeDtypeStruct((TOP_K, t), I32), jax.ShapeDtypeStruct((TOP_K, t), F32),
                   jax.ShapeDtypeStruct((TOP_K, t), I32), jax.ShapeDtypeStruct((N_EXPERTS, 1), I32)],
        scratch_shapes=[pltpu.VMEM((N_EXPERTS, 1), F32)],
        compiler_params=pltpu.CompilerParams(dimension_semantics=("arbitrary",), vmem_limit_bytes=VMEM_LIMIT),
        name="merge_router",
    )(x2, omla, oswa, qx, gates, kvm, wmo, wso, wxo, wout, g_ffn, wr_split, b_router_col)


SC_CORES = 2
SC_SUBCORES = 16
SC_WORKERS = SC_CORES * SC_SUBCORES
SC_CHUNK = 64


def _sc_mesh():
    return plsc.VectorSubcoreMesh(core_axis_name="c", subcore_axis_name="s",
                                  num_cores=SC_CORES, num_subcores=SC_SUBCORES)


def _sc_index_blocks(idx):
    n = idx.shape[0]
    per_w = n // SC_WORKERS
    n_ch = per_w // SC_CHUNK
    assert per_w * SC_WORKERS == n and n_ch * SC_CHUNK == per_w
    return idx.reshape(SC_WORKERS, n_ch, SC_CHUNK), per_w, n_ch


def _sc_two_buffer_loop(n_ch, load, stores, bufs, load_sems, store_sems):
    assert n_ch % 2 == 0
    a, b = bufs
    la, lb = load_sems
    sa, sb = store_sems

    def start(cps):
        for cp in cps:
            cp.start()

    def wait(cps):
        for cp in cps:
            cp.wait()

    load(0, a, la).start()

    @pl.loop(0, n_ch, step=2)
    def _(j):
        load(j, a, la).wait()

        @pl.when(j > 0)
        def _():
            wait(stores(j - 1, b, sb))

        load(j + 1, b, lb).start()
        start(stores(j, a, sa))
        load(j + 1, b, lb).wait()
        wait(stores(j, a, sa))

        @pl.when(j + 2 < n_ch)
        def _():
            load(j + 2, a, la).start()

        start(stores(j + 1, b, sb))

    wait(stores(n_ch - 1, b, sb))


def _sc_scratch(n_ch, row_shape, dtype):
    return [pltpu.VMEM((n_ch, SC_CHUNK), I32), pltpu.VMEM((SC_CHUNK,) + row_shape, dtype),
            pltpu.VMEM((SC_CHUNK,) + row_shape, dtype)] + [pltpu.SemaphoreType.DMA] * 4


def _sc_scatter_rows(src, idx, n_out):
    fan, n_src = idx.shape
    per_w = n_src // SC_WORKERS
    n_ch = per_w // SC_CHUNK
    assert per_w * SC_WORKERS == n_src and n_ch * SC_CHUNK == per_w
    idx3 = idx.reshape(fan, SC_WORKERS, n_ch, SC_CHUNK).transpose(1, 0, 2, 3).reshape(SC_WORKERS, fan * n_ch, SC_CHUNK)

    @functools.partial(
        pl.kernel, mesh=_sc_mesh(),
        out_type=jax.ShapeDtypeStruct((n_out,) + src.shape[1:], src.dtype),
        scratch_types=_sc_scratch(fan * n_ch, src.shape[1:], src.dtype),
        name="moe_dispatch_sc")
    def k(src_hbm, idx_hbm, out_hbm, idx_v, rows_a, rows_b, la, lb, sa, sb):
        wid = lax.axis_index("s") * SC_CORES + lax.axis_index("c")
        base = wid * per_w
        pltpu.sync_copy(idx_hbm.at[wid], idx_v)

        def load(c, buf, sem):
            return pltpu.make_async_copy(src_hbm.at[pl.ds(base + c * SC_CHUNK, SC_CHUNK)], buf, sem)

        def stores(c, buf, sem):
            return tuple(pltpu.make_async_copy(buf, out_hbm.at[idx_v.at[f * n_ch + c]], sem) for f in range(fan))

        _sc_two_buffer_loop(n_ch, load, stores, (rows_a, rows_b), (la, lb), (sa, sb))

    return k(src, idx3)


def _sc_gather_rows(table, idx):
    idx3, per_w, n_ch = _sc_index_blocks(idx)

    @functools.partial(
        pl.kernel, mesh=_sc_mesh(),
        out_type=jax.ShapeDtypeStruct((idx.shape[0],) + table.shape[1:], table.dtype),
        scratch_types=_sc_scratch(n_ch, table.shape[1:], table.dtype),
        name="moe_gather_sc")
    def k(table_hbm, idx_hbm, out_hbm, idx_v, rows_a, rows_b, la, lb, sa, sb):
        wid = lax.axis_index("s") * SC_CORES + lax.axis_index("c")
        base = wid * per_w
        pltpu.sync_copy(idx_hbm.at[wid], idx_v)

        def load(c, buf, sem):
            return pltpu.make_async_copy(table_hbm.at[idx_v.at[c]], buf, sem)

        def stores(c, buf, sem):
            return (pltpu.make_async_copy(buf, out_hbm.at[pl.ds(base + c * SC_CHUNK, SC_CHUNK)], sem),)

        _sc_two_buffer_loop(n_ch, load, stores, (rows_a, rows_b), (la, lb), (sa, sb))

    return k(table, idx3)


def _unpack_lo(w):
    return pltpu.bitcast(w << 16, F32)


def _unpack_hi(w):
    return pltpu.bitcast(w & jnp.uint32(0xFFFF0000), F32)


def _ffn_kernel(te_ref, xs_ref, wgu_hbm, bgu_ref, wd_hbm, bd_ref, y_ref, wgu_f32, wd_f32, wgu_bf, wd_bf, wsem):
    i = pl.program_id(0)
    n = pl.num_programs(0)
    n_used = te_ref[n]
    e = te_ref[i]
    first = (i == 0) | (e != te_ref[jnp.maximum(i - 1, 0)])
    slot = te_ref[2 * n + 1 + i] % 2
    nxt = te_ref[3 * n + 1 + i]

    def fetch(expert, s):
        return (pltpu.make_async_copy(wgu_hbm.at[expert], wgu_f32.at[s], wsem.at[0, s]),
                pltpu.make_async_copy(wd_hbm.at[expert], wd_f32.at[s], wsem.at[1, s]))

    @pl.when(i == 0)
    def _():
        for cp in fetch(e, slot):
            cp.start()

    @pl.when((i < n_used) & first)
    def _():
        for cp in fetch(e, slot):
            cp.wait()
        wgu_bf[...] = wgu_f32[slot].astype(BF16)
        wd_bf[...] = wd_f32[slot].astype(BF16)

        @pl.when(nxt >= 0)
        def _():
            for cp in fetch(nxt, 1 - slot):
                cp.start()

    valid = jnp.where(i < n_used, te_ref[n + 1 + i], 0)
    rows, n_slab, _ = xs_ref.shape
    half = n_slab * LANES
    for sb in range(rows // FFN_SUB):
        r0 = sb * FFN_SUB

        @pl.when(valid > r0)
        def _():
            w = _load_slabs(xs_ref, r0, FFN_SUB)
            w = jnp.where(lax.broadcasted_iota(I32, w.shape, 0) < valid - r0, w, jnp.uint32(0))
            x_lo = _unpack_lo(w).astype(BF16)
            x_hi = _unpack_hi(w).astype(BF16)
            gu = (jnp.dot(x_lo, wgu_bf[0:half, :], preferred_element_type=F32)
                  + jnp.dot(x_hi, wgu_bf[half:, :], preferred_element_type=F32) + bgu_ref[0])
            de = gu.shape[1] // 2
            x_glu = jnp.minimum(gu[:, :de], SWIGLU_LIMIT)
            x_lin = jnp.clip(gu[:, de:], -SWIGLU_LIMIT, SWIGLU_LIMIT)
            hdn = x_glu * jax.nn.sigmoid(SWIGLU_ALPHA * x_glu) * (x_lin + 1.0)
            y = jnp.dot(hdn.astype(BF16), wd_bf[...], preferred_element_type=F32) + bd_ref[0]
            bits = pltpu.bitcast(y.astype(BF16).astype(F32), U32)
            _store_slabs(y_ref, r0, (bits[:, :half] >> 16) | (bits[:, half:] & jnp.uint32(0xFFFF0000)))

        @pl.when(valid <= r0)
        def _():
            y_ref[r0:r0 + FFN_SUB] = jnp.zeros((FFN_SUB,) + y_ref.shape[1:], U32)


def _ffn_call(tile_table, xs, w_gate_up, b_gate_up, w_down, b_down):
    r, ns, lanes = xs.shape
    tm = FFN_TM
    ne, d, de2 = w_gate_up.shape
    return pl.pallas_call(
        _ffn_kernel,
        grid_spec=pltpu.PrefetchScalarGridSpec(
            num_scalar_prefetch=1,
            grid=(r // tm,),
            in_specs=[pl.BlockSpec((tm, ns, lanes),
                                   lambda i, te: (jnp.minimum(i, jnp.maximum(te[r // tm] - 1, 0)), 0, 0)),
                      pl.BlockSpec(memory_space=pl.ANY),
                      pl.BlockSpec((1, 1, de2), lambda i, te: (te[i], 0, 0)),
                      pl.BlockSpec(memory_space=pl.ANY),
                      pl.BlockSpec((1, 1, d), lambda i, te: (te[i], 0, 0))],
            out_specs=pl.BlockSpec((tm, ns, lanes), lambda i, te: (i, 0, 0)),
            scratch_shapes=[pltpu.VMEM((2, d, de2), F32), pltpu.VMEM((2, de2 // 2, d), F32),
                            pltpu.VMEM((d, de2), BF16), pltpu.VMEM((de2 // 2, d), BF16),
                            pltpu.SemaphoreType.DMA((2, 2))]),
        out_shape=jax.ShapeDtypeStruct((r, ns, lanes), U32),
        compiler_params=pltpu.CompilerParams(dimension_semantics=("arbitrary",), vmem_limit_bytes=VMEM_LIMIT),
        name="moe_ffn",
    )(tile_table, xs, w_gate_up, b_gate_up, w_down, b_down)


def _combine_dense_kernel(h_ref, gate_ref, gfin_ref, y0_ref, y1_ref, y2_ref, y3_ref, o_ref):
    half = y0_ref.shape[1] * y0_ref.shape[2]
    lo = h_ref[:, :half]
    hi = h_ref[:, half:]
    for k, y_ref in enumerate((y0_ref, y1_ref, y2_ref, y3_ref)):
        g = gate_ref[:, k:k + 1]
        w = _load_slabs(y_ref)
        lo = lo + g * _unpack_lo(w)
        hi = hi + g * _unpack_hi(w)
    ms = (jnp.sum(lo * lo, axis=1, keepdims=True) + jnp.sum(hi * hi, axis=1, keepdims=True)) / (2 * half)
    inv = lax.rsqrt(ms + RMS_EPS)
    o_ref[:, :half] = lo * inv * gfin_ref[:, :half]
    o_ref[:, half:] = hi * inv * gfin_ref[:, half:]


def _combine_dense_call(h, gate, g_final, yk):
    t, d = h.shape
    tm = COMBINE_TM
    per_k = t // tm
    yspec = lambda k: pl.BlockSpec((tm,) + yk.shape[1:], lambda i: (k * per_k + i, 0, 0))
    return pl.pallas_call(
        _combine_dense_kernel,
        grid=(t // tm,),
        in_specs=[pl.BlockSpec((tm, d), lambda i: (i, 0)),
                  pl.BlockSpec((tm, TOP_K), lambda i: (i, 0)),
                  pl.BlockSpec((1, d), lambda i: (0, 0)),
                  yspec(0), yspec(1), yspec(2), yspec(3)],
        out_specs=pl.BlockSpec((tm, d), lambda i: (i, 0)),
        out_shape=jax.ShapeDtypeStruct((t, d), F32),
        compiler_params=pltpu.CompilerParams(dimension_semantics=("arbitrary",), vmem_limit_bytes=VMEM_LIMIT),
        name="moe_combine",
    )(h, gate, g_final, yk, yk, yk, yk)


def _rope_freqs():
    def inv_freq(dh):
        return (ROPE_THETA ** (-jnp.arange(0, dh, 2, dtype=F32) / dh))[:, None]

    return inv_freq(MLA_ROPE), inv_freq(SWA_HEAD_DIM)


def _winprep_kernel(w_ref, o_ref):
    w = w_ref[...]
    rb = w.shape[0]
    c1 = MLA_Q_RANK + MLA_KV_RANK
    tail = w[:, c1 + MLA_ROPE:]
    lane = lax.broadcasted_iota(I32, (rb, LANES), 1)
    in_rope = (lane >= MLA_NOPE) & (lane < MLA_NOPE + MLA_ROPE)
    kr_p = jnp.where(in_rope, pltpu.roll(w[:, c1:c1 + LANES], MLA_NOPE, axis=1), 0.0)
    pieces = [w[:, :c1], kr_p, tail]
    off = 0
    for pc in pieces:
        o_ref[:, off:off + pc.shape[1]] = pc.astype(BF16)
        off += pc.shape[1]


def _winprep_call(w_in, layer):
    _, d, n = w_in.shape
    rb = 128
    return pl.pallas_call(
        _winprep_kernel,
        grid=(d // rb,),
        in_specs=[pl.BlockSpec((None, rb, n), lambda i: (layer, i, 0))],
        out_specs=pl.BlockSpec((rb, _D1), lambda i: (i, 0)),
        out_shape=jax.ShapeDtypeStruct((d, _D1), BF16),
        compiler_params=pltpu.CompilerParams(dimension_semantics=("arbitrary",), vmem_limit_bytes=VMEM_LIMIT),
        name="w_in_prep",
    )(w_in)


def _prep_weights(w_in, layer, w_mla_uq, w_mla_ukv):
    w_in_al = _winprep_call(w_in, layer)

    r = w_mla_uq.shape[0]
    wq = w_mla_uq.reshape(r, MLA_HEADS, MLA_NOPE + MLA_ROPE)
    zq = jnp.zeros((r, MLA_HEADS, LANES - MLA_NOPE - MLA_ROPE), w_mla_uq.dtype)
    wq_pad = jnp.concatenate([wq, zq], axis=-1).reshape(r, MLA_HEADS * LANES).astype(BF16)

    rk = w_mla_ukv.shape[0]
    wkv = w_mla_ukv.reshape(rk, MLA_HEADS, MLA_NOPE + MLA_V)
    wk_aug = jnp.concatenate([wkv[..., :MLA_NOPE], jnp.zeros((rk, MLA_HEADS, LANES - MLA_NOPE), w_mla_ukv.dtype)],
                             axis=-1).reshape(rk, MLA_HEADS * LANES).astype(BF16)
    wv_t = wkv[..., MLA_NOPE:].reshape(rk, MLA_HEADS * MLA_V).T.astype(BF16)
    return w_in_al, wq_pad, wk_aug, wv_t


def kernel(x, mem, positions, g_mix, w_in, g_mla_q, w_mla_uq, g_mla_kv, w_mla_ukv, w_mla_o, swa_sinks, w_swa_o,
           g_mem, w_mem_kv, w_xa_o, b_gate, w_out, g_ffn, w_router, b_router, w_gate_up, b_gate_up, w_down,
           b_down, g_final):
    b, s, d = x.shape
    t = b * s
    depth = g_mix.shape[0]
    h = x.reshape(t, d)
    pos = positions.astype(F32).reshape(1, t)
    fq, f64 = _rope_freqs()
    for l in range(depth):
        w_in_al, wq_pad, wk_aug, wv_t = _prep_weights(w_in, l, w_mla_uq[l], w_mla_ukv[l])
        (qm, km, vmt, qs, ksa, ksb, vsa, vsb, qx, gates) = _proj_call(
            h, pos, fq, f64, g_mix[l][None], w_in_al, g_mla_q[l][None], wq_pad,
            g_mla_kv[l][None], wk_aug, wv_t, b_gate[l][None])
        r3 = lambda a: a.reshape(b, s, a.shape[1])
        omla = _mla_call(r3(qm), r3(km), vmt, s).reshape(t, -1)
        oswa = _swa_call(swa_sinks[l], r3(qs), r3(ksa), r3(ksb), r3(vsa), r3(vsb)).reshape(t, -1)
        m = mem.shape[1]
        kvm = _memkv_call(mem.reshape(b * m, d), g_mem[l][None], w_mem_kv[l].astype(BF16)).reshape(b, m, -1)
        wr_t = w_router[l].T
        wr_hi = wr_t.astype(BF16)
        wr_split = jnp.concatenate([wr_hi, (wr_t - wr_hi.astype(F32)).astype(BF16)], axis=0)
        h_mid, hnp, idx, gate, rank, counts = _merge_call(
            h, omla, oswa, qx, gates, kvm, w_mla_o[l].astype(BF16), w_swa_o[l].astype(BF16),
            w_xa_o[l].astype(BF16), w_out[l].astype(BF16), g_ffn[l][None], wr_split, b_router[l][:, None], s)

        counts = counts[:, 0]
        padded = ((counts + FFN_TM - 1) // FFN_TM) * FFN_TM
        padded_end = jnp.cumsum(padded)
        offsets = padded_end - padded
        experts = jnp.arange(N_EXPERTS, dtype=I32)
        dest = (jnp.sum(jnp.where(idx[..., None] == experts, offsets, 0), axis=-1) + rank).reshape(-1).astype(I32)
        n_tiles = (t * TOP_K) // FFN_TM + N_EXPERTS
        n_used = (padded_end[-1] // FFN_TM).astype(I32)
        tile_start = jnp.minimum(jnp.arange(n_tiles, dtype=I32), jnp.maximum(n_used - 1, 0)) * FFN_TM
        tile_expert = jnp.sum((padded_end[None, :] <= tile_start[:, None]).astype(I32), axis=1)
        tile_expert = jnp.minimum(tile_expert, N_EXPERTS - 1)
        onehot = tile_expert[:, None] == experts[None, :]
        pick = lambda v: jnp.sum(jnp.where(onehot, v[None, :], 0), axis=1)
        tile_valid = jnp.clip(pick(counts) - (tile_start - pick(offsets)), 0, FFN_TM)
        nonempty = padded > 0
        tile_group = pick(jnp.cumsum(nonempty.astype(I32)) - 1)
        later = jnp.where(nonempty[None, :] & (experts[None, :] > experts[:, None]), experts[None, :], N_EXPERTS)
        next_expert = jnp.min(later, axis=1)
        tile_next = pick(jnp.where(next_expert < N_EXPERTS, next_expert, -1))
        te = jnp.concatenate([tile_expert, n_used[None], tile_valid.astype(I32), tile_group.astype(I32),
                              tile_next.astype(I32)])

        xs = _sc_scatter_rows(hnp, dest.reshape(TOP_K, t), n_tiles * FFN_TM)
        y = _ffn_call(te, xs, w_gate_up[l], b_gate_up[l][:, None, :], w_down[l], b_down[l][:, None, :])
        yk = _sc_gather_rows(y, dest)
        if l == depth - 1:
            gfin = g_final[None]
            out = _combine_dense_call(h_mid, gate.T, gfin, yk)
        else:
            raise NotImplementedError("depth > 1 needs a combine without the final norm")
        h = out
    return h.reshape(b, s, d)
```

```python
import functools
import math

import jax
import jax.numpy as jnp
from jax import lax
from jax.experimental import pallas as pl
from jax.experimental.pallas import tpu as pltpu
from jax.experimental.pallas import tpu_sc as plsc

F32 = jnp.float32
BF16 = jnp.bfloat16
U32 = jnp.uint32
I32 = jnp.int32

LANES = 128
ROPE_THETA = 10000.0
RMS_EPS = 1e-6
LOG2E = 1.4426950408889634

MLA_HEADS = 8
MLA_NOPE = 64
MLA_ROPE = 32
MLA_V = 64
MLA_Q_RANK = 256
MLA_KV_RANK = 128
SWA_HEADS = 8
SWA_KV_HEADS = 2
SWA_HEAD_DIM = 64
SWA_WINDOW = 128
XA_HEADS = 4
XA_HEAD_DIM = 128
N_EXPERTS = 32
TOP_K = 4
SWIGLU_ALPHA = 1.702
SWIGLU_LIMIT = 7.0
N_BRANCHES = 3

NEG_BIG = -1e30

PROJ_TM = 512
PROJ_SUB = 256
MLA_TQ = 1024
MLA_ONES_ROWS = 16
SWA_TS = 512
MERGE_TM = 512
MERGE_SUB = 256
FFN_TM = 1024
FFN_SUB = 256
COMBINE_TM = 256
COMBINE_PARTS = 2

VMEM_LIMIT = 56 * 1024 * 1024


def _rms(x, g):
    return x * lax.rsqrt(jnp.mean(x * x, axis=-1, keepdims=True) + RMS_EPS) * g


def _store_slabs(ref, row0, value):
    rows, n, _ = ref.shape
    flat = ref.reshape(rows * n, LANES)
    for c in range(n):
        flat[pl.ds(row0 * n + c, value.shape[0], stride=n), :] = value[:, c * LANES:(c + 1) * LANES]


def _load_slabs(ref, row0=0, m=None):
    rows, n, _ = ref.shape
    m = rows if m is None else m
    flat = ref.reshape(rows * n, LANES)
    return jnp.concatenate([flat[pl.ds(row0 * n + c, m, stride=n), :] for c in range(n)], axis=1)


_A0, _A1 = 0, 512
_B0, _B1 = 512, 1280
_C0, _C1 = 1280, 1792
_D0, _D1 = 1792, 4864


def _rotate_half(x, d, lo, hi):
    n = x.shape[1]
    half = (hi - lo) // 2
    lane = lax.broadcasted_iota(I32, x.shape, 1) % d
    up = pltpu.roll(x, n - half, axis=1)
    dn = pltpu.roll(x, half, axis=1)
    return jnp.where((lane >= lo) & (lane < lo + half), -up, jnp.where((lane >= lo + half) & (lane < hi), dn, 0.0))


def _proj_kernel(x_ref, pos_ref, fq_ref, f64_ref, gmix_ref, win_ref, gq_ref, wq_ref,
                 gkv_ref, wk_ref, wv_ref, bgate_ref,
                 qm_ref, km_ref, vm_ref, qs_ref, ksa_ref, ksb_ref, vsa_ref, vsb_ref, qx_ref, gt_ref):
    for hf in range(x_ref.shape[0] // PROJ_SUB):
        rows = slice(hf * PROJ_SUB, (hf + 1) * PROJ_SUB)
        _proj_rows(x_ref[rows, :], pos_ref[:, rows], fq_ref, f64_ref, gmix_ref, win_ref, gq_ref, wq_ref, gkv_ref,
                   wk_ref, wv_ref, bgate_ref,
                   [r.at[rows] for r in (qm_ref, km_ref)], vm_ref.at[:, rows],
                   [r.at[rows] for r in (qs_ref, ksa_ref, ksb_ref, vsa_ref, vsb_ref, qx_ref, gt_ref)])


def _proj_rows(x, pos, fq_ref, f64_ref, gmix_ref, win_ref, gq_ref, wq_ref, gkv_ref, wk_ref, wv_ref, bgate_ref,
               mla_refs, vm_ref, other_refs):
    qm_ref, km_ref = mla_refs
    qs_ref, ksa_ref, ksb_ref, vsa_ref, vsb_ref, qx_ref, gt_ref = other_refs
    xn = _rms(x, gmix_ref[...]).astype(BF16)
    tm = x.shape[0]
    ang16 = fq_ref[...] * pos
    ang32 = f64_ref[...] * pos
    c16, s16, c32, s32 = jnp.cos(ang16), jnp.sin(ang16), jnp.cos(ang32), jnp.sin(ang32)
    one = jnp.ones((MLA_NOPE, tm), F32)
    zero = jnp.zeros((MLA_NOPE, tm), F32)
    pad = LANES - MLA_NOPE - MLA_ROPE
    cosq = jnp.concatenate([one, c16, c16, one[:pad]], axis=0).T
    sinq = jnp.concatenate([zero, s16, s16, zero[:pad]], axis=0).T
    cos64 = jnp.concatenate([c32, c32, c32, c32], axis=0).T
    sin64 = jnp.concatenate([s32, s32, s32, s32], axis=0).T
    rope_lo, rope_hi = MLA_NOPE, MLA_NOPE + MLA_ROPE

    xa = jnp.dot(xn, win_ref[:, _A0:_A1], preferred_element_type=F32)
    cqn = _rms(xa[:, 0:256], gq_ref[...]).astype(BF16)
    qa = jnp.dot(cqn, wq_ref[...], preferred_element_type=F32)
    qb = _rotate_half(qa, LANES, rope_lo, rope_hi)
    q_scale = LOG2E / math.sqrt(MLA_NOPE + MLA_ROPE)
    ckvn = _rms(xa[:, 256:384], gkv_ref[...]).astype(BF16)
    ka = jnp.dot(ckvn, wk_ref[...], preferred_element_type=F32)
    kr = xa[:, 384:512]
    krope = kr * cosq + _rotate_half(kr, LANES, rope_lo, rope_hi) * sinq
    for h in range(MLA_HEADS):
        sl = slice(h * LANES, (h + 1) * LANES)
        qm_ref[:, sl] = ((qa[:, sl] * cosq + qb[:, sl] * sinq) * q_scale).astype(BF16)
        km_ref[:, sl] = (ka[:, sl] + krope).astype(BF16)
    vm_ref[...] = lax.dot_general(wv_ref[...], ckvn, (((1,), (1,)), ((), ())),
                                  preferred_element_type=F32).astype(BF16)

    xb = jnp.dot(xn, win_ref[:, _B0:_B1], preferred_element_type=F32)
    s_scale = LOG2E / math.sqrt(SWA_HEAD_DIM)
    nq = SWA_HEADS * SWA_HEAD_DIM
    qs = xb[:, :nq]
    qs_rot = _rotate_half(qs, SWA_HEAD_DIM, 0, SWA_HEAD_DIM)
    for p in range(SWA_HEADS // 2):
        sl = slice(p * LANES, (p + 1) * LANES)
        qs_ref[:, sl] = ((qs[:, sl] * cos64 + qs_rot[:, sl] * sin64) * s_scale).astype(BF16)
    ks = xb[:, nq:nq + LANES]
    ks = ks * cos64 + _rotate_half(ks, SWA_HEAD_DIM, 0, SWA_HEAD_DIM) * sin64
    ksa_ref[...] = ks.astype(BF16)
    ksb_ref[...] = pltpu.roll(ks, 64, axis=1).astype(BF16)
    vs = xb[:, nq + LANES:nq + 2 * LANES]
    vsa_ref[...] = vs.astype(BF16)
    vsb_ref[...] = pltpu.roll(vs, 64, axis=1).astype(BF16)

    xc = jnp.dot(xn, win_ref[:, _C0:_C1], preferred_element_type=F32)
    qx_ref[...] = (xc * (LOG2E / math.sqrt(XA_HEAD_DIM))).astype(BF16)

    xd = jnp.dot(xn, win_ref[:, _D0:_D1], preferred_element_type=F32) + bgate_ref[...]
    gt_ref[...] = jax.nn.sigmoid(xd).astype(BF16)


def _proj_call(x2, pos, fq, f64, g_mix, w_in_al, g_q, wq, g_kv, wk_aug, wv, b_gate):
    t, d = x2.shape
    tm = PROJ_TM
    row = lambda n: pl.BlockSpec((tm, n), lambda i: (i, 0))
    full = lambda a: pl.BlockSpec(a.shape, lambda i: (0,) * a.ndim)
    out_cols = [1024, 1024, 512, 128, 128, 128, 128, 512, 3072]
    out_specs = [row(n) for n in out_cols]
    out_shape = [jax.ShapeDtypeStruct((t, n), BF16) for n in out_cols]
    vt_rows = wv.shape[0]
    out_specs.insert(2, pl.BlockSpec((vt_rows, tm), lambda i: (0, i)))
    out_shape.insert(2, jax.ShapeDtypeStruct((vt_rows, t), BF16))
    return pl.pallas_call(
        _proj_kernel,
        grid=(t // tm,),
        in_specs=[row(d), pl.BlockSpec((1, tm), lambda i: (0, i)), full(fq), full(f64), full(g_mix),
                  pl.BlockSpec(w_in_al.shape, lambda i: (0, 0), pipeline_mode=pl.Buffered(1)),
                  full(g_q), full(wq), full(g_kv), full(wk_aug), full(wv), full(b_gate)],
        out_specs=out_specs,
        out_shape=out_shape,
        compiler_params=pltpu.CompilerParams(dimension_semantics=("arbitrary",), vmem_limit_bytes=VMEM_LIMIT),
        name="proj",
    )(x2, pos, fq, f64, g_mix, w_in_al, g_q, wq, g_kv, wk_aug, wv, b_gate)


def _mla_kernel(q_ref, k_ref, vt_ref, o_ref, sa_ref, sb_ref, ma_ref, mb_ref, m_ref, acc_ref, *, tq):
    i = pl.program_id(2)
    m_ref[...] = jnp.full(m_ref.shape, NEG_BIG, F32)
    acc_ref[...] = jnp.zeros(acc_ref.shape, F32)
    ones = jnp.ones((MLA_ONES_ROWS, tq), BF16)

    def scores(j, bufs):
        s_ref, mx_ref = bufs
        k0 = pl.multiple_of(j * tq, tq)
        for hh in range(2):
            sl = slice(hh * LANES, (hh + 1) * LANES)
            st = lax.dot_general(k_ref[0, pl.ds(k0, tq), sl], q_ref[0, :, sl], (((1,), (1,)), ((), ())),
                                 preferred_element_type=F32)
            s_ref[hh] = st
            mx_ref[hh] = jnp.max(st, axis=0, keepdims=True)

    def update(j, bufs, masked):
        s_ref, mx_ref = bufs
        k0 = pl.multiple_of(j * tq, tq)
        for hh in range(2):
            vt = jnp.concatenate([vt_ref[hh * MLA_V:(hh + 1) * MLA_V, pl.ds(k0, tq)], ones], axis=0)
            st = s_ref[hh]
            if masked:
                kj = lax.broadcasted_iota(I32, (tq, tq), 0)
                qi = lax.broadcasted_iota(I32, (tq, tq), 1)
                st = jnp.where(kj <= qi, st, NEG_BIG)
                m_tile = jnp.max(st, axis=0, keepdims=True)
            else:
                m_tile = mx_ref[hh]
            m_old = m_ref[hh]
            m_new = jnp.maximum(m_old, m_tile)
            alpha = jnp.exp2(m_old - m_new)
            pt = jnp.exp2(st - m_new)
            acc_ref[hh] = alpha * acc_ref[hh] + jnp.dot(vt, pt.astype(BF16), preferred_element_type=F32)
            m_ref[hh] = m_new

    buf_a = (sa_ref, ma_ref)
    buf_b = (sb_ref, mb_ref)
    scores(0, buf_a)

    def body(jj, carry):
        scores(2 * jj + 1, buf_b)
        update(2 * jj, buf_a, False)
        scores(2 * jj + 2, buf_a)
        update(2 * jj + 1, buf_b, False)
        return carry

    lax.fori_loop(0, i // 2, body, 0)

    @pl.when(i % 2 == 0)
    def _():
        update(i, buf_a, True)

    @pl.when(i % 2 == 1)
    def _():
        scores(i, buf_b)
        update(i - 1, buf_a, False)
        update(i, buf_b, True)

    ot = jnp.concatenate([acc_ref[hh, :MLA_V] / acc_ref[hh, MLA_V:MLA_V + 1] for hh in range(2)],
                         axis=0)
    o_ref[0] = ot.T.astype(BF16)


def _mla_call(q, k, vt, seq):
    b, s, _ = q.shape
    assert s == seq
    tq = min(MLA_TQ, s)
    n_pairs = MLA_HEADS // 2
    return pl.pallas_call(
        functools.partial(_mla_kernel, tq=tq),
        grid=(b, n_pairs, s // tq),
        in_specs=[pl.BlockSpec((1, tq, 2 * LANES), lambda bi, hp, i: (bi, i, hp)),
                  pl.BlockSpec((1, s, 2 * LANES), lambda bi, hp, i: (bi, 0, hp)),
                  pl.BlockSpec((2 * MLA_V, s), lambda bi, hp, i: (hp, bi))],
        out_specs=pl.BlockSpec((1, tq, LANES), lambda bi, hp, i: (bi, i, hp)),
        out_shape=jax.ShapeDtypeStruct((b, s, n_pairs * LANES), BF16),
        scratch_shapes=[pltpu.VMEM((2, tq, tq), F32), pltpu.VMEM((2, tq, tq), F32),
                        pltpu.VMEM((2, 1, tq), F32), pltpu.VMEM((2, 1, tq), F32),
                        pltpu.VMEM((2, 1, tq), F32),
                        pltpu.VMEM((2, MLA_V + MLA_ONES_ROWS, tq), F32)],
        compiler_params=pltpu.CompilerParams(dimension_semantics=("arbitrary",) * 3, vmem_limit_bytes=VMEM_LIMIT),
        name="mla_attn",
    )(q, k, vt)


def _swa_kernel(sink_ref, q_ref, ka_ref, kb_ref, va_ref, vb_ref, kah_ref, kbh_ref, vah_ref, vbh_ref, o_ref, *, ts):
    w = SWA_WINDOW
    i = pl.program_id(1)
    ka = jnp.concatenate([kah_ref[0], ka_ref[0]], axis=0)
    kb = jnp.concatenate([kbh_ref[0], kb_ref[0]], axis=0)
    va = jnp.concatenate([vah_ref[0], va_ref[0]], axis=0)
    vb = jnp.concatenate([vbh_ref[0], vb_ref[0]], axis=0)
    lane_k = lax.broadcasted_iota(I32, (2 * w, LANES), 1)
    low = lane_k < SWA_HEAD_DIM
    qi = lax.broadcasted_iota(I32, (2 * w, 2 * w), 0) % w
    kj = lax.broadcasted_iota(I32, (2 * w, 2 * w), 1)
    diff = qi + w - kj
    band = (diff >= 0) & (diff < w)
    lane_o = lax.broadcasted_iota(I32, (w, LANES), 1)
    row2 = lax.broadcasted_iota(I32, (2 * w, 1), 0)
    zero = jnp.zeros((), BF16)
    stacks = ((0, ka, True, va), (1, kb, False, vb), (4, kb, True, vb), (5, ka, False, va))
    for n in range(ts // w):
        mask = band & ((i * (ts // w) + n > 0) | (kj >= w))
        res = []
        for h0, ksrc, keep_low, vsrc in stacks:
            p0 = h0 // 2
            q = jnp.concatenate([q_ref[0, n * w:(n + 1) * w, p0 * LANES:(p0 + 1) * LANES],
                                 q_ref[0, n * w:(n + 1) * w, (p0 + 1) * LANES:(p0 + 2) * LANES]], axis=0)
            kwin = ksrc[n * w:n * w + 2 * w]
            kwin = jnp.where(low if keep_low else ~low, kwin, zero)
            vwin = vsrc[n * w:n * w + 2 * w]
            s = lax.dot_general(q, kwin, (((1,), (1,)), ((), ())), preferred_element_type=F32)
            s = jnp.where(mask, s, NEG_BIG)
            sink = jnp.where(row2 < w, sink_ref[h0], sink_ref[h0 + 2]) * LOG2E
            m = jnp.maximum(jnp.max(s, axis=1, keepdims=True), sink)
            p = jnp.exp2(s - m)
            den = jnp.sum(p, axis=1, keepdims=True) + jnp.exp2(sink - m)
            o = jnp.dot(p.astype(BF16), vwin, preferred_element_type=F32) / den
            res.append(o)
        o02, o13, o46, o57 = res
        sel = lane_o < SWA_HEAD_DIM
        rows = slice(n * w, (n + 1) * w)
        o_ref[0, rows, 0 * LANES:1 * LANES] = jnp.where(sel, o02[:w], o13[:w]).astype(BF16)
        o_ref[0, rows, 1 * LANES:2 * LANES] = jnp.where(sel, o02[w:], o13[w:]).astype(BF16)
        o_ref[0, rows, 2 * LANES:3 * LANES] = jnp.where(sel, o46[:w], o57[:w]).astype(BF16)
        o_ref[0, rows, 3 * LANES:4 * LANES] = jnp.where(sel, o46[w:], o57[w:]).astype(BF16)


def _swa_call(sinks, q, ksa, ksb, vsa, vsb):
    b, s, _ = q.shape
    ts = min(SWA_TS, s)
    w = SWA_WINDOW
    r = ts // w
    main = pl.BlockSpec((1, ts, LANES), lambda bi, i: (bi, i, 0))
    halo = pl.BlockSpec((1, w, LANES), lambda bi, i: (bi, jnp.maximum(i * r - 1, 0), 0))
    return pl.pallas_call(
        functools.partial(_swa_kernel, ts=ts),
        grid=(b, s // ts),
        in_specs=[pl.BlockSpec(memory_space=pltpu.SMEM),
                  pl.BlockSpec((1, ts, 4 * LANES), lambda bi, i: (bi, i, 0)),
                  main, main, main, main, halo, halo, halo, halo],
        out_specs=pl.BlockSpec((1, ts, 4 * LANES), lambda bi, i: (bi, i, 0)),
        out_shape=jax.ShapeDtypeStruct((b, s, 4 * LANES), BF16),
        compiler_params=pltpu.CompilerParams(dimension_semantics=("arbitrary",) * 2, vmem_limit_bytes=VMEM_LIMIT),
        name="swa_attn",
    )(sinks, q, ksa, ksb, vsa, vsb, ksa, ksb, vsa, vsb)


def _memkv_kernel(mem_ref, g_ref, w_ref, o_ref):
    mn = _rms(mem_ref[...], g_ref[...]).astype(BF16)
    o_ref[...] = jnp.dot(mn, w_ref[...], preferred_element_type=F32).astype(BF16)


def _memkv_call(mem2, g_mem, w_mem_kv):
    n, d = mem2.shape
    tm = min(256, n)
    return pl.pallas_call(
        _memkv_kernel,
        grid=(n // tm,),
        in_specs=[pl.BlockSpec((tm, d), lambda i: (i, 0)),
                  pl.BlockSpec(g_mem.shape, lambda i: (0, 0)),
                  pl.BlockSpec(w_mem_kv.shape, lambda i: (0, 0))],
        out_specs=pl.BlockSpec((tm, w_mem_kv.shape[1]), lambda i: (i, 0)),
        out_shape=jax.ShapeDtypeStruct((n, w_mem_kv.shape[1]), BF16),
        compiler_params=pltpu.CompilerParams(dimension_semantics=("arbitrary",), vmem_limit_bytes=VMEM_LIMIT),
        name="mem_kv",
    )(mem2, g_mem, w_mem_kv)


def _merge_kernel(x_ref, omla_ref, oswa_ref, qx_ref, gt_ref, kvm_ref, wmo_ref, wso_ref, wxo_ref, wout_ref,
                  gffn_ref, wr_ref, br_ref,
                  h_ref, hnp_ref, idx_ref, gate_ref, rank_ref, cnt_ref, run_ref, *, tm, sub):
    @pl.when(pl.program_id(0) == 0)
    def _():
        run_ref[...] = jnp.zeros(run_ref.shape, F32)

    d = x_ref.shape[1]
    kv_cols = XA_HEADS * XA_HEAD_DIM
    erow = lax.broadcasted_iota(I32, (N_EXPERTS, sub), 0)
    tri_t = (lax.broadcasted_iota(I32, (sub, sub), 0) < lax.broadcasted_iota(I32, (sub, sub), 1)).astype(BF16)
    nt = (((1,), (1,)), ((), ()))
    run = run_ref[...]
    for hf in range(tm // sub):
        rows = slice(hf * sub, (hf + 1) * sub)

        oxs = []
        for hd in range(XA_HEADS):
            sl = slice(hd * LANES, (hd + 1) * LANES)
            km = kvm_ref[0, :, sl]
            vm = kvm_ref[0, :, kv_cols + hd * LANES:kv_cols + (hd + 1) * LANES]
            s = lax.dot_general(qx_ref[rows, sl], km, nt, preferred_element_type=F32)
            p = jnp.exp2(s - jnp.max(s, axis=1, keepdims=True))
            den = jnp.sum(p, axis=1, keepdims=True)
            oxs.append((jnp.dot(p.astype(BF16), vm, preferred_element_type=F32) / den).astype(BF16))
        oxa = jnp.concatenate(oxs, axis=1)

        merged = (gt_ref[rows, 0:d].astype(F32) * jnp.dot(omla_ref[rows, :], wmo_ref[...], preferred_element_type=F32)
                  + gt_ref[rows, d:2 * d].astype(F32) * jnp.dot(oswa_ref[rows, :], wso_ref[...],
                                                                 preferred_element_type=F32)
                  + gt_ref[rows, 2 * d:3 * d].astype(F32) * jnp.dot(oxa, wxo_ref[...], preferred_element_type=F32))
        h = x_ref[rows, :] + jnp.dot(merged.astype(BF16), wout_ref[...], preferred_element_type=F32)
        h_ref[rows, :] = h

        hn = _rms(h, gffn_ref[...])
        hn_hi = hn.astype(BF16)
        hn_hi32 = hn_hi.astype(F32)
        hn_lo = (hn - hn_hi32).astype(BF16)
        bits = pltpu.bitcast(hn_hi32, U32)
        _store_slabs(hnp_ref, hf * sub, (bits[:, : d // 2] >> 16) | (bits[:, d // 2:] & jnp.uint32(0xFFFF0000)))

        part = lax.dot_general(wr_ref[...], hn_hi, nt, preferred_element_type=F32)
        logits_t = (part[:N_EXPERTS] + part[N_EXPERTS:]
                    + lax.dot_general(wr_ref[0:N_EXPERTS, :], hn_lo, nt, preferred_element_type=F32) + br_ref[...])

        work = logits_t
        vals, idxs, hots = [], [], []
        for _ in range(TOP_K):
            mx = jnp.max(work, axis=0, keepdims=True)
            ix = jnp.min(jnp.where(work == mx, erow, N_EXPERTS), axis=0, keepdims=True)
            hot = erow == ix
            work = jnp.where(hot, -jnp.inf, work)
            vals.append(mx)
            idxs.append(ix)
            hots.append(hot)
        es = [jnp.exp(v - vals[0]) for v in vals]
        den = es[0] + es[1] + es[2] + es[3]
        sel_t = (hots[0] | hots[1] | hots[2] | hots[3])
        prefix_t = jnp.dot(sel_t.astype(BF16), tri_t, preferred_element_type=F32) + run
        for k in range(TOP_K):
            idx_ref[k:k + 1, rows] = idxs[k]
            gate_ref[k:k + 1, rows] = es[k] / den
            rank_ref[k:k + 1, rows] = jnp.sum(jnp.where(hots[k], prefix_t, 0.0), axis=0, keepdims=True).astype(I32)
        run = run + jnp.sum(sel_t.astype(F32), axis=1, keepdims=True)
    run_ref[...] = run
    cnt_ref[...] = run.astype(I32)


def _merge_call(x2, omla, oswa, qx, gates, kvm, wmo, wso, wxo, wout, g_ffn, wr_split, b_router_col, seq):
    t, d = x2.shape
    tm = MERGE_TM
    per_b = seq // tm
    row = lambda n: pl.BlockSpec((tm, n), lambda i: (i, 0))
    col = lambda: pl.BlockSpec((TOP_K, tm), lambda i: (0, i))
    full = lambda a: pl.BlockSpec(a.shape, lambda i: (0,) * a.ndim)
    return pl.pallas_call(
        functools.partial(_merge_kernel, tm=tm, sub=MERGE_SUB),
        grid=(t // tm,),
        in_specs=[row(d), row(512), row(512), row(512), row(3 * d),
                  pl.BlockSpec((1,) + kvm.shape[1:], lambda i: (i // per_b, 0, 0)),
                  full(wmo), full(wso), full(wxo), full(wout), full(g_ffn), full(wr_split), full(b_router_col)],
        out_specs=[row(d), pl.BlockSpec((tm, d // 2 // LANES, LANES), lambda i: (i, 0, 0)), col(), col(), col(),
                   pl.BlockSpec((N_EXPERTS, 1), lambda i: (0, 0))],
        out_shape=[jax.ShapeDtypeStruct((t, d), F32), jax.ShapeDtypeStruct((t, d // 2 // LANES, LANES), U32),
                   jax.ShapeDtypeStruct((TOP_K, t), I32), jax.ShapeDtypeStruct((TOP_K, t), F32),
                   jax.ShapeDtypeStruct((TOP_K, t), I32), jax.ShapeDtypeStruct((N_EXPERTS, 1), I32)],
        scratch_shapes=[pltpu.VMEM((N_EXPERTS, 1), F32)],
        compiler_params=pltpu.CompilerParams(dimension_semantics=("arbitrary",), vmem_limit_bytes=VMEM_LIMIT),
        name="merge_router",
    )(x2, omla, oswa, qx, gates, kvm, wmo, wso, wxo, wout, g_ffn, wr_split, b_router_col)


SC_CORES = 2
SC_SUBCORES = 16
SC_WORKERS = SC_CORES * SC_SUBCORES
SC_CHUNK = 64


def _sc_mesh():
    return plsc.VectorSubcoreMesh(core_axis_name="c", subcore_axis_name="s",
                                  num_cores=SC_CORES, num_subcores=SC_SUBCORES)


def _sc_index_blocks(idx):
    n = idx.shape[0]
    per_w = n // SC_WORKERS
    n_ch = per_w // SC_CHUNK
    assert per_w * SC_WORKERS == n and n_ch * SC_CHUNK == per_w
    return idx.reshape(SC_WORKERS, n_ch, SC_CHUNK), per_w, n_ch


def _sc_two_buffer_loop(n_ch, load, stores, bufs, load_sems, store_sems):
    assert n_ch % 2 == 0
    a, b = bufs
    la, lb = load_sems
    sa, sb = store_sems

    def start(cps):
        for cp in cps:
            cp.start()

    def wait(cps):
        for cp in cps:
            cp.wait()

    load(0, a, la).start()

    @pl.loop(0, n_ch, step=2)
    def _(j):
        load(j, a, la).wait()

        @pl.when(j > 0)
        def _():
            wait(stores(j - 1, b, sb))

        load(j + 1, b, lb).start()
        start(stores(j, a, sa))
        load(j + 1, b, lb).wait()
        wait(stores(j, a, sa))

        @pl.when(j + 2 < n_ch)
        def _():
            load(j + 2, a, la).start()

        start(stores(j + 1, b, sb))

    wait(stores(n_ch - 1, b, sb))


def _sc_scratch(n_ch, row_shape, dtype):
    return [pltpu.VMEM((n_ch, SC_CHUNK), I32), pltpu.VMEM((SC_CHUNK,) + row_shape, dtype),
            pltpu.VMEM((SC_CHUNK,) + row_shape, dtype)] + [pltpu.SemaphoreType.DMA] * 4


def _sc_scatter_rows(src, idx, n_out):
    fan, n_src = idx.shape
    per_w = n_src // SC_WORKERS
    n_ch = per_w // SC_CHUNK
    assert per_w * SC_WORKERS == n_src and n_ch * SC_CHUNK == per_w
    idx3 = idx.reshape(fan, SC_WORKERS, n_ch, SC_CHUNK).transpose(1, 0, 2, 3).reshape(SC_WORKERS, fan * n_ch, SC_CHUNK)

    @functools.partial(
        pl.kernel, mesh=_sc_mesh(),
        out_type=jax.ShapeDtypeStruct((n_out,) + src.shape[1:], src.dtype),
        scratch_types=_sc_scratch(fan * n_ch, src.shape[1:], src.dtype),
        name="moe_dispatch_sc")
    def k(src_hbm, idx_hbm, out_hbm, idx_v, rows_a, rows_b, la, lb, sa, sb):
        wid = lax.axis_index("s") * SC_CORES + lax.axis_index("c")
        base = wid * per_w
        pltpu.sync_copy(idx_hbm.at[wid], idx_v)

        def load(c, buf, sem):
            return pltpu.make_async_copy(src_hbm.at[pl.ds(base + c * SC_CHUNK, SC_CHUNK)], buf, sem)

        def stores(c, buf, sem):
            return tuple(pltpu.make_async_copy(buf, out_hbm.at[idx_v.at[f * n_ch + c]], sem) for f in range(fan))

        _sc_two_buffer_loop(n_ch, load, stores, (rows_a, rows_b), (la, lb), (sa, sb))

    return k(src, idx3)


def _sc_gather_rows(table, idx):
    idx3, per_w, n_ch = _sc_index_blocks(idx)

    @functools.partial(
        pl.kernel, mesh=_sc_mesh(),
        out_type=jax.ShapeDtypeStruct((idx.shape[0],) + table.shape[1:], table.dtype),
        scratch_types=_sc_scratch(n_ch, table.shape[1:], table.dtype),
        name="moe_gather_sc")
    def k(table_hbm, idx_hbm, out_hbm, idx_v, rows_a, rows_b, la, lb, sa, sb):
        wid = lax.axis_index("s") * SC_CORES + lax.axis_index("c")
        base = wid * per_w
        pltpu.sync_copy(idx_hbm.at[wid], idx_v)

        def load(c, buf, sem):
            return pltpu.make_async_copy(table_hbm.at[idx_v.at[c]], buf, sem)

        def stores(c, buf, sem):
            return (pltpu.make_async_copy(buf, out_hbm.at[pl.ds(base + c * SC_CHUNK, SC_CHUNK)], sem),)

        _sc_two_buffer_loop(n_ch, load, stores, (rows_a, rows_b), (la, lb), (sa, sb))

    return k(table, idx3)


def _unpack_lo(w):
    return pltpu.bitcast(w << 16, F32)


def _unpack_hi(w):
    return pltpu.bitcast(w & jnp.uint32(0xFFFF0000), F32)


def _ffn_kernel(te_ref, xs_ref, wgu_hbm, bgu_ref, wd_hbm, bd_ref, y_ref, wgu_f32, wd_f32, wgu_bf, wd_bf, wsem):
    i = pl.program_id(0)
    n = pl.num_programs(0)
    n_used = te_ref[n]
    e = te_ref[i]
    first = (i == 0) | (e != te_ref[jnp.maximum(i - 1, 0)])
    slot = te_ref[2 * n + 1 + i] % 2
    nxt = te_ref[3 * n + 1 + i]

    def fetch(expert, s):
        return (pltpu.make_async_copy(wgu_hbm.at[expert], wgu_f32.at[s], wsem.at[0, s]),
                pltpu.make_async_copy(wd_hbm.at[expert], wd_f32.at[s], wsem.at[1, s]))

    @pl.when(i == 0)
    def _():
        for cp in fetch(e, slot):
            cp.start()

    @pl.when((i < n_used) & first)
    def _():
        for cp in fetch(e, slot):
            cp.wait()
        wgu_bf[...] = wgu_f32[slot].astype(BF16)
        wd_bf[...] = wd_f32[slot].astype(BF16)

        @pl.when(nxt >= 0)
        def _():
            for cp in fetch(nxt, 1 - slot):
                cp.start()

    valid = jnp.where(i < n_used, te_ref[n + 1 + i], 0)
    rows, n_slab, _ = xs_ref.shape
    half = n_slab * LANES
    for sb in range(rows // FFN_SUB):
        r0 = sb * FFN_SUB

        @pl.when(valid > r0)
        def _():
            w = _load_slabs(xs_ref, r0, FFN_SUB)
            w = jnp.where(lax.broadcasted_iota(I32, w.shape, 0) < valid - r0, w, jnp.uint32(0))
            x_lo = _unpack_lo(w).astype(BF16)
            x_hi = _unpack_hi(w).astype(BF16)
            gu = (jnp.dot(x_lo, wgu_bf[0:half, :], preferred_element_type=F32)
                  + jnp.dot(x_hi, wgu_bf[half:, :], preferred_element_type=F32) + bgu_ref[0])
            de = gu.shape[1] // 2
            x_glu = jnp.minimum(gu[:, :de], SWIGLU_LIMIT)
            x_lin = jnp.clip(gu[:, de:], -SWIGLU_LIMIT, SWIGLU_LIMIT)
            hdn = x_glu * jax.nn.sigmoid(SWIGLU_ALPHA * x_glu) * (x_lin + 1.0)
            y = jnp.dot(hdn.astype(BF16), wd_bf[...], preferred_element_type=F32) + bd_ref[0]
            bits = pltpu.bitcast(y.astype(BF16).astype(F32), U32)
            _store_slabs(y_ref, r0, (bits[:, :half] >> 16) | (bits[:, half:] & jnp.uint32(0xFFFF0000)))

        @pl.when(valid <= r0)
        def _():
            y_ref[r0:r0 + FFN_SUB] = jnp.zeros((FFN_SUB,) + y_ref.shape[1:], U32)


def _ffn_call(tile_table, xs, w_gate_up, b_gate_up, w_down, b_down):
    r, ns, lanes = xs.shape
    tm = FFN_TM
    ne, d, de2 = w_gate_up.shape
    return pl.pallas_call(
        _ffn_kernel,
        grid_spec=pltpu.PrefetchScalarGridSpec(
            num_scalar_prefetch=1,
            grid=(r // tm,),
            in_specs=[pl.BlockSpec((tm, ns, lanes),
                                   lambda i, te: (jnp.minimum(i, jnp.maximum(te[r // tm] - 1, 0)), 0, 0)),
                      pl.BlockSpec(memory_space=pl.ANY),
                      pl.BlockSpec((1, 1, de2), lambda i, te: (te[i], 0, 0)),
                      pl.BlockSpec(memory_space=pl.ANY),
                      pl.BlockSpec((1, 1, d), lambda i, te: (te[i], 0, 0))],
            out_specs=pl.BlockSpec((tm, ns, lanes), lambda i, te: (i, 0, 0)),
            scratch_shapes=[pltpu.VMEM((2, d, de2), F32), pltpu.VMEM((2, de2 // 2, d), F32),
                            pltpu.VMEM((d, de2), BF16), pltpu.VMEM((de2 // 2, d), BF16),
                            pltpu.SemaphoreType.DMA((2, 2))]),
        out_shape=jax.ShapeDtypeStruct((r, ns, lanes), U32),
        compiler_params=pltpu.CompilerParams(dimension_semantics=("arbitrary",), vmem_limit_bytes=VMEM_LIMIT),
        name="moe_ffn",
    )(tile_table, xs, w_gate_up, b_gate_up, w_down, b_down)


def _combine_dense_kernel(h_ref, gate_ref, gfin_ref, y0_ref, y1_ref, y2_ref, y3_ref, o_ref):
    half = y0_ref.shape[1] * y0_ref.shape[2]
    lo = h_ref[:, :half]
    hi = h_ref[:, half:]
    for k, y_ref in enumerate((y0_ref, y1_ref, y2_ref, y3_ref)):
        g = gate_ref[:, k:k + 1]
        w = _load_slabs(y_ref)
        lo = lo + g * _unpack_lo(w)
        hi = hi + g * _unpack_hi(w)
    ms = (jnp.sum(lo * lo, axis=1, keepdims=True) + jnp.sum(hi * hi, axis=1, keepdims=True)) / (2 * half)
    inv = lax.rsqrt(ms + RMS_EPS)
    o_ref[:, :half] = lo * inv * gfin_ref[:, :half]
    o_ref[:, half:] = hi * inv * gfin_ref[:, half:]


def _combine_dense_into_kernel(h_ref, gate_ref, gfin_ref, y0_ref, y1_ref, y2_ref, y3_ref, prev_ref, o_ref):
    del prev_ref
    _combine_dense_kernel(h_ref, gate_ref, gfin_ref, y0_ref, y1_ref, y2_ref, y3_ref, o_ref)


def _combine_dense_call(h, gate, g_final, yk, part, n_parts, prev_out):
    t, d = h.shape
    tm = COMBINE_TM
    steps = t // tm // n_parts
    first = part * steps
    yspec = lambda k: pl.BlockSpec((tm,) + yk.shape[1:], lambda i: (k * steps + i, 0, 0))
    in_specs = [pl.BlockSpec((tm, d), lambda i: (first + i, 0)),
                pl.BlockSpec((tm, TOP_K), lambda i: (first + i, 0)),
                pl.BlockSpec((1, d), lambda i: (0, 0)),
                yspec(0), yspec(1), yspec(2), yspec(3)]
    args = [h, gate, g_final, yk, yk, yk, yk]
    aliases = {}
    body = _combine_dense_kernel
    if prev_out is not None:
        in_specs.append(pl.BlockSpec(memory_space=pl.ANY))
        args.append(prev_out)
        aliases = {len(args) - 1: 0}
        body = _combine_dense_into_kernel
    return pl.pallas_call(
        body,
        grid=(steps,),
        in_specs=in_specs,
        out_specs=pl.BlockSpec((tm, d), lambda i: (first + i, 0)),
        out_shape=jax.ShapeDtypeStruct((t, d), F32),
        input_output_aliases=aliases,
        compiler_params=pltpu.CompilerParams(dimension_semantics=("arbitrary",), vmem_limit_bytes=VMEM_LIMIT),
        name="moe_combine",
    )(*args)


def _rope_freqs():
    def inv_freq(dh):
        return (ROPE_THETA ** (-jnp.arange(0, dh, 2, dtype=F32) / dh))[:, None]

    return inv_freq(MLA_ROPE), inv_freq(SWA_HEAD_DIM)


def _winprep_kernel(w_ref, o_ref):
    w = w_ref[...]
    rb = w.shape[0]
    c1 = MLA_Q_RANK + MLA_KV_RANK
    tail = w[:, c1 + MLA_ROPE:]
    lane = lax.broadcasted_iota(I32, (rb, LANES), 1)
    in_rope = (lane >= MLA_NOPE) & (lane < MLA_NOPE + MLA_ROPE)
    kr_p = jnp.where(in_rope, pltpu.roll(w[:, c1:c1 + LANES], MLA_NOPE, axis=1), 0.0)
    pieces = [w[:, :c1], kr_p, tail]
    off = 0
    for pc in pieces:
        o_ref[:, off:off + pc.shape[1]] = pc.astype(BF16)
        off += pc.shape[1]


def _winprep_call(w_in, layer):
    _, d, n = w_in.shape
    rb = 128
    return pl.pallas_call(
        _winprep_kernel,
        grid=(d // rb,),
        in_specs=[pl.BlockSpec((None, rb, n), lambda i: (layer, i, 0))],
        out_specs=pl.BlockSpec((rb, _D1), lambda i: (i, 0)),
        out_shape=jax.ShapeDtypeStruct((d, _D1), BF16),
        compiler_params=pltpu.CompilerParams(dimension_semantics=("arbitrary",), vmem_limit_bytes=VMEM_LIMIT),
        name="w_in_prep",
    )(w_in)


def _prep_weights(w_in, layer, w_mla_uq, w_mla_ukv):
    w_in_al = _winprep_call(w_in, layer)

    r = w_mla_uq.shape[0]
    wq = w_mla_uq.reshape(r, MLA_HEADS, MLA_NOPE + MLA_ROPE)
    zq = jnp.zeros((r, MLA_HEADS, LANES - MLA_NOPE - MLA_ROPE), w_mla_uq.dtype)
    wq_pad = jnp.concatenate([wq, zq], axis=-1).reshape(r, MLA_HEADS * LANES).astype(BF16)

    rk = w_mla_ukv.shape[0]
    wkv = w_mla_ukv.reshape(rk, MLA_HEADS, MLA_NOPE + MLA_V)
    wk_aug = jnp.concatenate([wkv[..., :MLA_NOPE], jnp.zeros((rk, MLA_HEADS, LANES - MLA_NOPE), w_mla_ukv.dtype)],
                             axis=-1).reshape(rk, MLA_HEADS * LANES).astype(BF16)
    wv_t = wkv[..., MLA_NOPE:].reshape(rk, MLA_HEADS * MLA_V).T.astype(BF16)
    return w_in_al, wq_pad, wk_aug, wv_t


def kernel(x, mem, positions, g_mix, w_in, g_mla_q, w_mla_uq, g_mla_kv, w_mla_ukv, w_mla_o, swa_sinks, w_swa_o,
           g_mem, w_mem_kv, w_xa_o, b_gate, w_out, g_ffn, w_router, b_router, w_gate_up, b_gate_up, w_down,
           b_down, g_final):
    b, s, d = x.shape
    t = b * s
    depth = g_mix.shape[0]
    h = x.reshape(t, d)
    pos = positions.astype(F32).reshape(1, t)
    fq, f64 = _rope_freqs()
    for l in range(depth):
        w_in_al, wq_pad, wk_aug, wv_t = _prep_weights(w_in, l, w_mla_uq[l], w_mla_ukv[l])
        (qm, km, vmt, qs, ksa, ksb, vsa, vsb, qx, gates) = _proj_call(
            h, pos, fq, f64, g_mix[l][None], w_in_al, g_mla_q[l][None], wq_pad,
            g_mla_kv[l][None], wk_aug, wv_t, b_gate[l][None])
        r3 = lambda a: a.reshape(b, s, a.shape[1])
        omla = _mla_call(r3(qm), r3(km), vmt, s).reshape(t, -1)
        oswa = _swa_call(swa_sinks[l], r3(qs), r3(ksa), r3(ksb), r3(vsa), r3(vsb)).reshape(t, -1)
        m = mem.shape[1]
        kvm = _memkv_call(mem.reshape(b * m, d), g_mem[l][None], w_mem_kv[l].astype(BF16)).reshape(b, m, -1)
        wr_t = w_router[l].T
        wr_hi = wr_t.astype(BF16)
        wr_split = jnp.concatenate([wr_hi, (wr_t - wr_hi.astype(F32)).astype(BF16)], axis=0)
        h_mid, hnp, idx, gate, rank, counts = _merge_call(
            h, omla, oswa, qx, gates, kvm, w_mla_o[l].astype(BF16), w_swa_o[l].astype(BF16),
            w_xa_o[l].astype(BF16), w_out[l].astype(BF16), g_ffn[l][None], wr_split, b_router[l][:, None], s)

        counts = counts[:, 0]
        padded = ((counts + FFN_TM - 1) // FFN_TM) * FFN_TM
        padded_end = jnp.cumsum(padded)
        offsets = padded_end - padded
        experts = jnp.arange(N_EXPERTS, dtype=I32)
        dest = (jnp.sum(jnp.where(idx[..., None] == experts, offsets, 0), axis=-1) + rank).reshape(-1).astype(I32)
        n_tiles = (t * TOP_K) // FFN_TM + N_EXPERTS
        n_used = (padded_end[-1] // FFN_TM).astype(I32)
        tile_start = jnp.minimum(jnp.arange(n_tiles, dtype=I32), jnp.maximum(n_used - 1, 0)) * FFN_TM
        tile_expert = jnp.sum((padded_end[None, :] <= tile_start[:, None]).astype(I32), axis=1)
        tile_expert = jnp.minimum(tile_expert, N_EXPERTS - 1)
        onehot = tile_expert[:, None] == experts[None, :]
        pick = lambda v: jnp.sum(jnp.where(onehot, v[None, :], 0), axis=1)
        tile_valid = jnp.clip(pick(counts) - (tile_start - pick(offsets)), 0, FFN_TM)
        nonempty = padded > 0
        tile_group = pick(jnp.cumsum(nonempty.astype(I32)) - 1)
        later = jnp.where(nonempty[None, :] & (experts[None, :] > experts[:, None]), experts[None, :], N_EXPERTS)
        next_expert = jnp.min(later, axis=1)
        tile_next = pick(jnp.where(next_expert < N_EXPERTS, next_expert, -1))
        te = jnp.concatenate([tile_expert, n_used[None], tile_valid.astype(I32), tile_group.astype(I32),
                              tile_next.astype(I32)])

        xs = _sc_scatter_rows(hnp, dest.reshape(TOP_K, t), n_tiles * FFN_TM)
        y = _ffn_call(te, xs, w_gate_up[l], b_gate_up[l][:, None, :], w_down[l], b_down[l][:, None, :])
        if l == depth - 1:
            gfin = g_final[None]
            gate_tk = gate.T
            dest_parts = dest.reshape(TOP_K, COMBINE_PARTS, t // COMBINE_PARTS)
            out = None
            for part in range(COMBINE_PARTS):
                yk = _sc_gather_rows(y, dest_parts[:, part].reshape(-1))
                out = _combine_dense_call(h_mid, gate_tk, gfin, yk, part, COMBINE_PARTS, out)
        else:
            raise NotImplementedError("depth > 1 needs a combine without the final norm")
        h = out
    return h.reshape(b, s, d)
```

```python
import functools
import math

import jax
import jax.numpy as jnp
from jax import lax
from jax.experimental import pallas as pl
from jax.experimental.pallas import tpu as pltpu
from jax.experimental.pallas import tpu_sc as plsc

F32 = jnp.float32
BF16 = jnp.bfloat16
U32 = jnp.uint32
I32 = jnp.int32

LANES = 128
ROPE_THETA = 10000.0
RMS_EPS = 1e-6
LOG2E = 1.4426950408889634

MLA_HEADS = 8
MLA_NOPE = 64
MLA_ROPE = 32
MLA_V = 64
MLA_Q_RANK = 256
MLA_KV_RANK = 128
SWA_HEADS = 8
SWA_KV_HEADS = 2
SWA_HEAD_DIM = 64
SWA_WINDOW = 128
XA_HEADS = 4
XA_HEAD_DIM = 128
N_EXPERTS = 32
TOP_K = 4
SWIGLU_ALPHA = 1.702
SWIGLU_LIMIT = 7.0
N_BRANCHES = 3

NEG_BIG = -1e30

PROJ_TM = 512
PROJ_SUB = 256
MLA_TQ = 1024
MLA_ONES_ROWS = 16
SWA_TS = 1024
MERGE_TM = 512
MERGE_SUB = 256
FFN_TM = 2048
FFN_SUB = 256
COMBINE_TM = 256
COMBINE_PARTS = 2

VMEM_LIMIT = 56 * 1024 * 1024


def _rms(x, g):
    return x * lax.rsqrt(jnp.mean(x * x, axis=-1, keepdims=True) + RMS_EPS) * g


def _store_slabs(ref, row0, value):
    rows, n, _ = ref.shape
    flat = ref.reshape(rows * n, LANES)
    for c in range(n):
        flat[pl.ds(row0 * n + c, value.shape[0], stride=n), :] = value[:, c * LANES:(c + 1) * LANES]


def _load_slabs(ref, row0=0, m=None):
    rows, n, _ = ref.shape
    m = rows if m is None else m
    flat = ref.reshape(rows * n, LANES)
    return jnp.concatenate([flat[pl.ds(row0 * n + c, m, stride=n), :] for c in range(n)], axis=1)


_A0, _A1 = 0, 512
_B0, _B1 = 512, 1280
_C0, _C1 = 1280, 1792
_D0, _D1 = 1792, 4864


def _rotate_half(x, d, lo, hi):
    n = x.shape[1]
    half = (hi - lo) // 2
    lane = lax.broadcasted_iota(I32, x.shape, 1) % d
    up = pltpu.roll(x, n - half, axis=1)
    dn = pltpu.roll(x, half, axis=1)
    return jnp.where((lane >= lo) & (lane < lo + half), -up, jnp.where((lane >= lo + half) & (lane < hi), dn, 0.0))


def _proj_kernel(x_ref, pos_ref, fq_ref, f64_ref, gmix_ref, win_ref, gq_ref, wq_ref,
                 gkv_ref, wk_ref, wv_ref, bgate_ref,
                 qm_ref, km_ref, vm_ref, qs_ref, ksa_ref, ksb_ref, vsa_ref, vsb_ref, qx_ref, gt_ref):
    for hf in range(x_ref.shape[0] // PROJ_SUB):
        rows = slice(hf * PROJ_SUB, (hf + 1) * PROJ_SUB)
        _proj_rows(x_ref[rows, :], pos_ref[:, rows], fq_ref, f64_ref, gmix_ref, win_ref, gq_ref, wq_ref, gkv_ref,
                   wk_ref, wv_ref, bgate_ref,
                   [r.at[rows] for r in (qm_ref, km_ref)], vm_ref.at[:, rows],
                   [r.at[rows] for r in (qs_ref, ksa_ref, ksb_ref, vsa_ref, vsb_ref, qx_ref, gt_ref)])


def _proj_rows(x, pos, fq_ref, f64_ref, gmix_ref, win_ref, gq_ref, wq_ref, gkv_ref, wk_ref, wv_ref, bgate_ref,
               mla_refs, vm_ref, other_refs):
    qm_ref, km_ref = mla_refs
    qs_ref, ksa_ref, ksb_ref, vsa_ref, vsb_ref, qx_ref, gt_ref = other_refs
    xn = _rms(x, gmix_ref[...]).astype(BF16)
    tm = x.shape[0]
    ang16 = fq_ref[...] * pos
    ang32 = f64_ref[...] * pos
    c16, s16, c32, s32 = jnp.cos(ang16), jnp.sin(ang16), jnp.cos(ang32), jnp.sin(ang32)
    one = jnp.ones((MLA_NOPE, tm), F32)
    zero = jnp.zeros((MLA_NOPE, tm), F32)
    pad = LANES - MLA_NOPE - MLA_ROPE
    cosq = jnp.concatenate([one, c16, c16, one[:pad]], axis=0).T
    sinq = jnp.concatenate([zero, s16, s16, zero[:pad]], axis=0).T
    cos64 = jnp.concatenate([c32, c32, c32, c32], axis=0).T
    sin64 = jnp.concatenate([s32, s32, s32, s32], axis=0).T
    rope_lo, rope_hi = MLA_NOPE, MLA_NOPE + MLA_ROPE

    xa = jnp.dot(xn, win_ref[:, _A0:_A1], preferred_element_type=F32)
    cqn = _rms(xa[:, 0:256], gq_ref[...]).astype(BF16)
    qa = jnp.dot(cqn, wq_ref[...], preferred_element_type=F32)
    qb = _rotate_half(qa, LANES, rope_lo, rope_hi)
    q_scale = LOG2E / math.sqrt(MLA_NOPE + MLA_ROPE)
    ckvn = _rms(xa[:, 256:384], gkv_ref[...]).astype(BF16)
    ka = jnp.dot(ckvn, wk_ref[...], preferred_element_type=F32)
    kr = xa[:, 384:512]
    krope = kr * cosq + _rotate_half(kr, LANES, rope_lo, rope_hi) * sinq
    for h in range(MLA_HEADS):
        sl = slice(h * LANES, (h + 1) * LANES)
        qm_ref[:, sl] = ((qa[:, sl] * cosq + qb[:, sl] * sinq) * q_scale).astype(BF16)
        km_ref[:, sl] = (ka[:, sl] + krope).astype(BF16)
    vm_ref[...] = lax.dot_general(wv_ref[...], ckvn, (((1,), (1,)), ((), ())),
                                  preferred_element_type=F32).astype(BF16)

    xb = jnp.dot(xn, win_ref[:, _B0:_B1], preferred_element_type=F32)
    s_scale = LOG2E / math.sqrt(SWA_HEAD_DIM)
    nq = SWA_HEADS * SWA_HEAD_DIM
    qs = xb[:, :nq]
    qs_rot = _rotate_half(qs, SWA_HEAD_DIM, 0, SWA_HEAD_DIM)
    for p in range(SWA_HEADS // 2):
        sl = slice(p * LANES, (p + 1) * LANES)
        qs_ref[:, sl] = ((qs[:, sl] * cos64 + qs_rot[:, sl] * sin64) * s_scale).astype(BF16)
    ks = xb[:, nq:nq + LANES]
    ks = ks * cos64 + _rotate_half(ks, SWA_HEAD_DIM, 0, SWA_HEAD_DIM) * sin64
    ksa_ref[...] = ks.astype(BF16)
    ksb_ref[...] = pltpu.roll(ks, 64, axis=1).astype(BF16)
    vs = xb[:, nq + LANES:nq + 2 * LANES]
    vsa_ref[...] = vs.astype(BF16)
    vsb_ref[...] = pltpu.roll(vs, 64, axis=1).astype(BF16)

    xc = jnp.dot(xn, win_ref[:, _C0:_C1], preferred_element_type=F32)
    qx_ref[...] = (xc * (LOG2E / math.sqrt(XA_HEAD_DIM))).astype(BF16)

    xd = jnp.dot(xn, win_ref[:, _D0:_D1], preferred_element_type=F32) + bgate_ref[...]
    gt_ref[...] = jax.nn.sigmoid(xd).astype(BF16)


def _proj_call(x2, pos, fq, f64, g_mix, w_in_al, g_q, wq, g_kv, wk_aug, wv, b_gate):
    t, d = x2.shape
    tm = PROJ_TM
    row = lambda n: pl.BlockSpec((tm, n), lambda i: (i, 0))
    full = lambda a: pl.BlockSpec(a.shape, lambda i: (0,) * a.ndim)
    out_cols = [1024, 1024, 512, 128, 128, 128, 128, 512, 3072]
    out_specs = [row(n) for n in out_cols]
    out_shape = [jax.ShapeDtypeStruct((t, n), BF16) for n in out_cols]
    vt_rows = wv.shape[0]
    out_specs.insert(2, pl.BlockSpec((vt_rows, tm), lambda i: (0, i)))
    out_shape.insert(2, jax.ShapeDtypeStruct((vt_rows, t), BF16))
    return pl.pallas_call(
        _proj_kernel,
        grid=(t // tm,),
        in_specs=[row(d), pl.BlockSpec((1, tm), lambda i: (0, i)), full(fq), full(f64), full(g_mix),
                  pl.BlockSpec(w_in_al.shape, lambda i: (0, 0), pipeline_mode=pl.Buffered(1)),
                  full(g_q), full(wq), full(g_kv), full(wk_aug), full(wv), full(b_gate)],
        out_specs=out_specs,
        out_shape=out_shape,
        compiler_params=pltpu.CompilerParams(dimension_semantics=("arbitrary",), vmem_limit_bytes=VMEM_LIMIT),
        name="proj",
    )(x2, pos, fq, f64, g_mix, w_in_al, g_q, wq, g_kv, wk_aug, wv, b_gate)


def _mla_kernel(q_ref, k_ref, vt_ref, o_ref, sa_ref, sb_ref, ma_ref, mb_ref, m_ref, acc_ref, *, tq):
    i = pl.program_id(2)
    m_ref[...] = jnp.full(m_ref.shape, NEG_BIG, F32)
    acc_ref[...] = jnp.zeros(acc_ref.shape, F32)
    ones = jnp.ones((MLA_ONES_ROWS, tq), BF16)

    def scores(j, bufs):
        s_ref, mx_ref = bufs
        k0 = pl.multiple_of(j * tq, tq)
        for hh in range(2):
            sl = slice(hh * LANES, (hh + 1) * LANES)
            st = lax.dot_general(k_ref[0, pl.ds(k0, tq), sl], q_ref[0, :, sl], (((1,), (1,)), ((), ())),
                                 preferred_element_type=F32)
            s_ref[hh] = st
            mx_ref[hh] = jnp.max(st, axis=0, keepdims=True)

    def update(j, bufs, masked):
        s_ref, mx_ref = bufs
        k0 = pl.multiple_of(j * tq, tq)
        hq = tq // 2
        parts = ((slice(0, hq), hq, 0), (slice(hq, tq), tq, hq)) if masked else ((slice(0, tq), tq, 0),)
        for hh in range(2):
            for cols, nk, q0 in parts:
                nq = cols.stop - cols.start
                vt = jnp.concatenate([vt_ref[hh * MLA_V:(hh + 1) * MLA_V, pl.ds(k0, nk)], ones[:, :nk]], axis=0)
                st = s_ref[hh, 0:nk, cols]
                if masked:
                    kj = lax.broadcasted_iota(I32, (nk, nq), 0)
                    qi = lax.broadcasted_iota(I32, (nk, nq), 1) + q0
                    st = jnp.where(kj <= qi, st, NEG_BIG)
                    m_tile = jnp.max(st, axis=0, keepdims=True)
                else:
                    m_tile = mx_ref[hh]
                m_old = m_ref[hh, :, cols]
                m_new = jnp.maximum(m_old, m_tile)
                alpha = jnp.exp2(m_old - m_new)
                pt = jnp.exp2(st - m_new)
                acc_ref[hh, :, cols] = (alpha * acc_ref[hh, :, cols]
                                        + jnp.dot(vt, pt.astype(BF16), preferred_element_type=F32))
                m_ref[hh, :, cols] = m_new

    buf_a = (sa_ref, ma_ref)
    buf_b = (sb_ref, mb_ref)
    scores(0, buf_a)

    def body(jj, carry):
        scores(2 * jj + 1, buf_b)
        update(2 * jj, buf_a, False)
        scores(2 * jj + 2, buf_a)
        update(2 * jj + 1, buf_b, False)
        return carry

    lax.fori_loop(0, i // 2, body, 0)

    @pl.when(i % 2 == 0)
    def _():
        update(i, buf_a, True)

    @pl.when(i % 2 == 1)
    def _():
        scores(i, buf_b)
        update(i - 1, buf_a, False)
        update(i, buf_b, True)

    ot = jnp.concatenate([acc_ref[hh, :MLA_V] / acc_ref[hh, MLA_V:MLA_V + 1] for hh in range(2)],
                         axis=0)
    o_ref[0] = ot.T.astype(BF16)


def _mla_call(q, k, vt, seq):
    b, s, _ = q.shape
    assert s == seq
    tq = min(MLA_TQ, s)
    n_pairs = MLA_HEADS // 2
    return pl.pallas_call(
        functools.partial(_mla_kernel, tq=tq),
        grid=(b, n_pairs, s // tq),
        in_specs=[pl.BlockSpec((1, tq, 2 * LANES), lambda bi, hp, i: (bi, i, hp)),
                  pl.BlockSpec((1, s, 2 * LANES), lambda bi, hp, i: (bi, 0, hp)),
                  pl.BlockSpec((2 * MLA_V, s), lambda bi, hp, i: (hp, bi))],
        out_specs=pl.BlockSpec((1, tq, LANES), lambda bi, hp, i: (bi, i, hp)),
        out_shape=jax.ShapeDtypeStruct((b, s, n_pairs * LANES), BF16),
        scratch_shapes=[pltpu.VMEM((2, tq, tq), F32), pltpu.VMEM((2, tq, tq), F32),
                        pltpu.VMEM((2, 1, tq), F32), pltpu.VMEM((2, 1, tq), F32),
                        pltpu.VMEM((2, 1, tq), F32),
                        pltpu.VMEM((2, MLA_V + MLA_ONES_ROWS, tq), F32)],
        compiler_params=pltpu.CompilerParams(dimension_semantics=("arbitrary",) * 3, vmem_limit_bytes=VMEM_LIMIT),
        name="mla_attn",
    )(q, k, vt)


def _swa_kernel(sink_ref, q_ref, ka_ref, kb_ref, va_ref, vb_ref, kah_ref, kbh_ref, vah_ref, vbh_ref, o_ref, *, ts):
    w = SWA_WINDOW
    i = pl.program_id(1)
    ka = jnp.concatenate([kah_ref[0], ka_ref[0]], axis=0)
    kb = jnp.concatenate([kbh_ref[0], kb_ref[0]], axis=0)
    va = jnp.concatenate([vah_ref[0], va_ref[0]], axis=0)
    vb = jnp.concatenate([vbh_ref[0], vb_ref[0]], axis=0)
    lane_k = lax.broadcasted_iota(I32, (2 * w, LANES), 1)
    low = lane_k < SWA_HEAD_DIM
    qi = lax.broadcasted_iota(I32, (2 * w, 2 * w), 0) % w
    kj = lax.broadcasted_iota(I32, (2 * w, 2 * w), 1)
    diff = qi + w - kj
    band = (diff >= 0) & (diff < w)
    lane_o = lax.broadcasted_iota(I32, (w, LANES), 1)
    row2 = lax.broadcasted_iota(I32, (2 * w, 1), 0)
    zero = jnp.zeros((), BF16)
    stacks = ((0, ka, True, va), (1, kb, False, vb), (4, kb, True, vb), (5, ka, False, va))
    for n in range(ts // w):
        mask = band & ((i * (ts // w) + n > 0) | (kj >= w))
        res = []
        for h0, ksrc, keep_low, vsrc in stacks:
            p0 = h0 // 2
            q = jnp.concatenate([q_ref[0, n * w:(n + 1) * w, p0 * LANES:(p0 + 1) * LANES],
                                 q_ref[0, n * w:(n + 1) * w, (p0 + 1) * LANES:(p0 + 2) * LANES]], axis=0)
            kwin = ksrc[n * w:n * w + 2 * w]
            kwin = jnp.where(low if keep_low else ~low, kwin, zero)
            vwin = vsrc[n * w:n * w + 2 * w]
            s = lax.dot_general(q, kwin, (((1,), (1,)), ((), ())), preferred_element_type=F32)
            s = jnp.where(mask, s, NEG_BIG)
            sink = jnp.where(row2 < w, sink_ref[h0], sink_ref[h0 + 2]) * LOG2E
            m = jnp.maximum(jnp.max(s, axis=1, keepdims=True), sink)
            p = jnp.exp2(s - m)
            den = jnp.sum(p, axis=1, keepdims=True) + jnp.exp2(sink - m)
            o = jnp.dot(p.astype(BF16), vwin, preferred_element_type=F32) / den
            res.append(o)
        o02, o13, o46, o57 = res
        sel = lane_o < SWA_HEAD_DIM
        rows = slice(n * w, (n + 1) * w)
        o_ref[0, rows, 0 * LANES:1 * LANES] = jnp.where(sel, o02[:w], o13[:w]).astype(BF16)
        o_ref[0, rows, 1 * LANES:2 * LANES] = jnp.where(sel, o02[w:], o13[w:]).astype(BF16)
        o_ref[0, rows, 2 * LANES:3 * LANES] = jnp.where(sel, o46[:w], o57[:w]).astype(BF16)
        o_ref[0, rows, 3 * LANES:4 * LANES] = jnp.where(sel, o46[w:], o57[w:]).astype(BF16)


def _swa_call(sinks, q, ksa, ksb, vsa, vsb):
    b, s, _ = q.shape
    ts = min(SWA_TS, s)
    w = SWA_WINDOW
    r = ts // w
    main = pl.BlockSpec((1, ts, LANES), lambda bi, i: (bi, i, 0))
    halo = pl.BlockSpec((1, w, LANES), lambda bi, i: (bi, jnp.maximum(i * r - 1, 0), 0))
    return pl.pallas_call(
        functools.partial(_swa_kernel, ts=ts),
        grid=(b, s // ts),
        in_specs=[pl.BlockSpec(memory_space=pltpu.SMEM),
                  pl.BlockSpec((1, ts, 4 * LANES), lambda bi, i: (bi, i, 0)),
                  main, main, main, main, halo, halo, halo, halo],
        out_specs=pl.BlockSpec((1, ts, 4 * LANES), lambda bi, i: (bi, i, 0)),
        out_shape=jax.ShapeDtypeStruct((b, s, 4 * LANES), BF16),
        compiler_params=pltpu.CompilerParams(dimension_semantics=("arbitrary",) * 2, vmem_limit_bytes=VMEM_LIMIT),
        name="swa_attn",
    )(sinks, q, ksa, ksb, vsa, vsb, ksa, ksb, vsa, vsb)


def _memkv_kernel(mem_ref, g_ref, w_ref, o_ref):
    mn = _rms(mem_ref[...], g_ref[...]).astype(BF16)
    o_ref[...] = jnp.dot(mn, w_ref[...], preferred_element_type=F32).astype(BF16)


def _memkv_call(mem2, g_mem, w_mem_kv):
    n, d = mem2.shape
    tm = min(256, n)
    return pl.pallas_call(
        _memkv_kernel,
        grid=(n // tm,),
        in_specs=[pl.BlockSpec((tm, d), lambda i: (i, 0)),
                  pl.BlockSpec(g_mem.shape, lambda i: (0, 0)),
                  pl.BlockSpec(w_mem_kv.shape, lambda i: (0, 0))],
        out_specs=pl.BlockSpec((tm, w_mem_kv.shape[1]), lambda i: (i, 0)),
        out_shape=jax.ShapeDtypeStruct((n, w_mem_kv.shape[1]), BF16),
        compiler_params=pltpu.CompilerParams(dimension_semantics=("arbitrary",), vmem_limit_bytes=VMEM_LIMIT),
        name="mem_kv",
    )(mem2, g_mem, w_mem_kv)


def _merge_kernel(x_ref, omla_ref, oswa_ref, qx_ref, gt_ref, kvm_ref, wmo_ref, wso_ref, wxo_ref, wout_ref,
                  gffn_ref, wr_ref, br_ref,
                  h_ref, hnp_ref, idx_ref, gate_ref, rank_ref, cnt_ref, run_ref, *, tm, sub):
    @pl.when(pl.program_id(0) == 0)
    def _():
        run_ref[...] = jnp.zeros(run_ref.shape, F32)

    d = x_ref.shape[1]
    kv_cols = XA_HEADS * XA_HEAD_DIM
    erow = lax.broadcasted_iota(I32, (N_EXPERTS, sub), 0)
    tri_t = (lax.broadcasted_iota(I32, (sub, sub), 0) < lax.broadcasted_iota(I32, (sub, sub), 1)).astype(BF16)
    nt = (((1,), (1,)), ((), ()))
    run = run_ref[...]
    for hf in range(tm // sub):
        rows = slice(hf * sub, (hf + 1) * sub)

        oxs = []
        for hd in range(XA_HEADS):
            sl = slice(hd * LANES, (hd + 1) * LANES)
            km = kvm_ref[0, :, sl]
            vm = kvm_ref[0, :, kv_cols + hd * LANES:kv_cols + (hd + 1) * LANES]
            s = lax.dot_general(qx_ref[rows, sl], km, nt, preferred_element_type=F32)
            p = jnp.exp2(s - jnp.max(s, axis=1, keepdims=True))
            den = jnp.sum(p, axis=1, keepdims=True)
            oxs.append((jnp.dot(p.astype(BF16), vm, preferred_element_type=F32) / den).astype(BF16))
        oxa = jnp.concatenate(oxs, axis=1)

        merged = (gt_ref[rows, 0:d].astype(F32) * jnp.dot(omla_ref[rows, :], wmo_ref[...], preferred_element_type=F32)
                  + gt_ref[rows, d:2 * d].astype(F32) * jnp.dot(oswa_ref[rows, :], wso_ref[...],
                                                                 preferred_element_type=F32)
                  + gt_ref[rows, 2 * d:3 * d].astype(F32) * jnp.dot(oxa, wxo_ref[...], preferred_element_type=F32))
        h = x_ref[rows, :] + jnp.dot(merged.astype(BF16), wout_ref[...], preferred_element_type=F32)
        h_ref[rows, :] = h

        hn = _rms(h, gffn_ref[...])
        hn_hi = hn.astype(BF16)
        hn_hi32 = hn_hi.astype(F32)
        hn_lo = (hn - hn_hi32).astype(BF16)
        bits = pltpu.bitcast(hn_hi32, U32)
        _store_slabs(hnp_ref, hf * sub, (bits[:, : d // 2] >> 16) | (bits[:, d // 2:] & jnp.uint32(0xFFFF0000)))

        part = lax.dot_general(wr_ref[...], hn_hi, nt, preferred_element_type=F32)
        logits_t = (part[:N_EXPERTS] + part[N_EXPERTS:]
                    + lax.dot_general(wr_ref[0:N_EXPERTS, :], hn_lo, nt, preferred_element_type=F32) + br_ref[...])

        work = logits_t
        vals, idxs, hots = [], [], []
        for _ in range(TOP_K):
            mx = jnp.max(work, axis=0, keepdims=True)
            ix = jnp.min(jnp.where(work == mx, erow, N_EXPERTS), axis=0, keepdims=True)
            hot = erow == ix
            work = jnp.where(hot, -jnp.inf, work)
            vals.append(mx)
            idxs.append(ix)
            hots.append(hot)
        es = [jnp.exp(v - vals[0]) for v in vals]
        den = es[0] + es[1] + es[2] + es[3]
        sel_t = (hots[0] | hots[1] | hots[2] | hots[3])
        prefix_t = jnp.dot(sel_t.astype(BF16), tri_t, preferred_element_type=F32) + run
        for k in range(TOP_K):
            idx_ref[k:k + 1, rows] = idxs[k]
            gate_ref[k:k + 1, rows] = es[k] / den
            rank_ref[k:k + 1, rows] = jnp.sum(jnp.where(hots[k], prefix_t, 0.0), axis=0, keepdims=True).astype(I32)
        run = run + jnp.sum(sel_t.astype(F32), axis=1, keepdims=True)
    run_ref[...] = run
    cnt_ref[...] = run.astype(I32)


def _merge_call(x2, omla, oswa, qx, gates, kvm, wmo, wso, wxo, wout, g_ffn, wr_split, b_router_col, seq):
    t, d = x2.shape
    tm = MERGE_TM
    per_b = seq // tm
    row = lambda n: pl.BlockSpec((tm, n), lambda i: (i, 0))
    col = lambda: pl.BlockSpec((TOP_K, tm), lambda i: (0, i))
    full = lambda a: pl.BlockSpec(a.shape, lambda i: (0,) * a.ndim)
    return pl.pallas_call(
        functools.partial(_merge_kernel, tm=tm, sub=MERGE_SUB),
        grid=(t // tm,),
        in_specs=[row(d), row(512), row(512), row(512), row(3 * d),
                  pl.BlockSpec((1,) + kvm.shape[1:], lambda i: (i // per_b, 0, 0)),
                  full(wmo), full(wso), full(wxo), full(wout), full(g_ffn), full(wr_split), full(b_router_col)],
        out_specs=[row(d), pl.BlockSpec((tm, d // 2 // LANES, LANES), lambda i: (i, 0, 0)), col(), col(), col(),
                   pl.BlockSpec((N_EXPERTS, 1), lambda i: (0, 0))],
        out_shape=[jax.ShapeDtypeStruct((t, d), F32), jax.ShapeDtypeStruct((t, d // 2 // LANES, LANES), U32),
                   jax.ShapeDtypeStruct((TOP_K, t), I32), jax.ShapeDtypeStruct((TOP_K, t), F32),
                   jax.ShapeDtypeStruct((TOP_K, t), I32), jax.ShapeDtypeStruct((N_EXPERTS, 1), I32)],
        scratch_shapes=[pltpu.VMEM((N_EXPERTS, 1), F32)],
        compiler_params=pltpu.CompilerParams(dimension_semantics=("arbitrary",), vmem_limit_bytes=VMEM_LIMIT),
        name="merge_router",
    )(x2, omla, oswa, qx, gates, kvm, wmo, wso, wxo, wout, g_ffn, wr_split, b_router_col)


SC_CORES = 2
SC_SUBCORES = 16
SC_WORKERS = SC_CORES * SC_SUBCORES
SC_CHUNK = 64


def _sc_mesh():
    return plsc.VectorSubcoreMesh(core_axis_name="c", subcore_axis_name="s",
                                  num_cores=SC_CORES, num_subcores=SC_SUBCORES)


def _sc_index_blocks(idx):
    n = idx.shape[0]
    per_w = n // SC_WORKERS
    n_ch = per_w // SC_CHUNK
    assert per_w * SC_WORKERS == n and n_ch * SC_CHUNK == per_w
    return idx.reshape(SC_WORKERS, n_ch, SC_CHUNK), per_w, n_ch


def _sc_two_buffer_loop(n_ch, load, stores, bufs, load_sems, store_sems):
    assert n_ch % 2 == 0
    a, b = bufs
    la, lb = load_sems
    sa, sb = store_sems

    def start(cps):
        for cp in cps:
            cp.start()

    def wait(cps):
        for cp in cps:
            cp.wait()

    load(0, a, la).start()

    @pl.loop(0, n_ch, step=2)
    def _(j):
        load(j, a, la).wait()

        @pl.when(j > 0)
        def _():
            wait(stores(j - 1, b, sb))

        load(j + 1, b, lb).start()
        start(stores(j, a, sa))
        load(j + 1, b, lb).wait()
        wait(stores(j, a, sa))

        @pl.when(j + 2 < n_ch)
        def _():
            load(j + 2, a, la).start()

        start(stores(j + 1, b, sb))

    wait(stores(n_ch - 1, b, sb))


def _sc_scratch(n_ch, row_shape, dtype):
    return [pltpu.VMEM((n_ch, SC_CHUNK), I32), pltpu.VMEM((SC_CHUNK,) + row_shape, dtype),
            pltpu.VMEM((SC_CHUNK,) + row_shape, dtype)] + [pltpu.SemaphoreType.DMA] * 4


def _sc_scatter_rows(src, idx, n_out):
    fan, n_src = idx.shape
    per_w = n_src // SC_WORKERS
    n_ch = per_w // SC_CHUNK
    assert per_w * SC_WORKERS == n_src and n_ch * SC_CHUNK == per_w
    idx3 = idx.reshape(fan, SC_WORKERS, n_ch, SC_CHUNK).transpose(1, 0, 2, 3).reshape(SC_WORKERS, fan * n_ch, SC_CHUNK)

    @functools.partial(
        pl.kernel, mesh=_sc_mesh(),
        out_type=jax.ShapeDtypeStruct((n_out,) + src.shape[1:], src.dtype),
        scratch_types=_sc_scratch(fan * n_ch, src.shape[1:], src.dtype),
        name="moe_dispatch_sc")
    def k(src_hbm, idx_hbm, out_hbm, idx_v, rows_a, rows_b, la, lb, sa, sb):
        wid = lax.axis_index("s") * SC_CORES + lax.axis_index("c")
        base = wid * per_w
        pltpu.sync_copy(idx_hbm.at[wid], idx_v)

        def load(c, buf, sem):
            return pltpu.make_async_copy(src_hbm.at[pl.ds(base + c * SC_CHUNK, SC_CHUNK)], buf, sem)

        def stores(c, buf, sem):
            return tuple(pltpu.make_async_copy(buf, out_hbm.at[idx_v.at[f * n_ch + c]], sem) for f in range(fan))

        _sc_two_buffer_loop(n_ch, load, stores, (rows_a, rows_b), (la, lb), (sa, sb))

    return k(src, idx3)


def _sc_gather_rows(table, idx):
    idx3, per_w, n_ch = _sc_index_blocks(idx)

    @functools.partial(
        pl.kernel, mesh=_sc_mesh(),
        out_type=jax.ShapeDtypeStruct((idx.shape[0],) + table.shape[1:], table.dtype),
        scratch_types=_sc_scratch(n_ch, table.shape[1:], table.dtype),
        name="moe_gather_sc")
    def k(table_hbm, idx_hbm, out_hbm, idx_v, rows_a, rows_b, la, lb, sa, sb):
        wid = lax.axis_index("s") * SC_CORES + lax.axis_index("c")
        base = wid * per_w
        pltpu.sync_copy(idx_hbm.at[wid], idx_v)

        def load(c, buf, sem):
            return pltpu.make_async_copy(table_hbm.at[idx_v.at[c]], buf, sem)

        def stores(c, buf, sem):
            return (pltpu.make_async_copy(buf, out_hbm.at[pl.ds(base + c * SC_CHUNK, SC_CHUNK)], sem),)

        _sc_two_buffer_loop(n_ch, load, stores, (rows_a, rows_b), (la, lb), (sa, sb))

    return k(table, idx3)


def _unpack_lo(w):
    return pltpu.bitcast(w << 16, F32)


def _unpack_hi(w):
    return pltpu.bitcast(w & jnp.uint32(0xFFFF0000), F32)


def _ffn_kernel(te_ref, xs_ref, wgu_hbm, bgu_ref, wd_hbm, bd_ref, y_ref, wgu_f32, wd_f32, wgu_bf, wd_bf, wsem):
    i = pl.program_id(0)
    n = pl.num_programs(0)
    n_used = te_ref[n]
    e = te_ref[i]
    first = (i == 0) | (e != te_ref[jnp.maximum(i - 1, 0)])
    slot = te_ref[2 * n + 1 + i] % 2
    nxt = te_ref[3 * n + 1 + i]

    def fetch(expert, s):
        return (pltpu.make_async_copy(wgu_hbm.at[expert], wgu_f32.at[s], wsem.at[0, s]),
                pltpu.make_async_copy(wd_hbm.at[expert], wd_f32.at[s], wsem.at[1, s]))

    @pl.when(i == 0)
    def _():
        for cp in fetch(e, slot):
            cp.start()

    @pl.when((i < n_used) & first)
    def _():
        for cp in fetch(e, slot):
            cp.wait()
        wgu_bf[...] = wgu_f32[slot].astype(BF16)
        wd_bf[...] = wd_f32[slot].astype(BF16)

        @pl.when(nxt >= 0)
        def _():
            for cp in fetch(nxt, 1 - slot):
                cp.start()

    valid = jnp.where(i < n_used, te_ref[n + 1 + i], 0)
    rows, n_slab, _ = xs_ref.shape
    half = n_slab * LANES
    for sb in range(rows // FFN_SUB):
        r0 = sb * FFN_SUB

        @pl.when(valid > r0)
        def _():
            w = _load_slabs(xs_ref, r0, FFN_SUB)
            w = jnp.where(lax.broadcasted_iota(I32, w.shape, 0) < valid - r0, w, jnp.uint32(0))
            x_lo = _unpack_lo(w).astype(BF16)
            x_hi = _unpack_hi(w).astype(BF16)
            gu = (jnp.dot(x_lo, wgu_bf[0:half, :], preferred_element_type=F32)
                  + jnp.dot(x_hi, wgu_bf[half:, :], preferred_element_type=F32) + bgu_ref[0])
            de = gu.shape[1] // 2
            x_glu = jnp.minimum(gu[:, :de], SWIGLU_LIMIT)
            x_lin = jnp.clip(gu[:, de:], -SWIGLU_LIMIT, SWIGLU_LIMIT)
            hdn = x_glu * jax.nn.sigmoid(SWIGLU_ALPHA * x_glu) * (x_lin + 1.0)
            y = jnp.dot(hdn.astype(BF16), wd_bf[...], preferred_element_type=F32) + bd_ref[0]
            bits = pltpu.bitcast(y.astype(BF16).astype(F32), U32)
            _store_slabs(y_ref, r0, (bits[:, :half] >> 16) | (bits[:, half:] & jnp.uint32(0xFFFF0000)))

        @pl.when(valid <= r0)
        def _():
            y_ref[r0:r0 + FFN_SUB] = jnp.zeros((FFN_SUB,) + y_ref.shape[1:], U32)


def _ffn_call(tile_table, xs, w_gate_up, b_gate_up, w_down, b_down):
    r, ns, lanes = xs.shape
    tm = FFN_TM
    ne, d, de2 = w_gate_up.shape
    return pl.pallas_call(
        _ffn_kernel,
        grid_spec=pltpu.PrefetchScalarGridSpec(
            num_scalar_prefetch=1,
            grid=(r // tm,),
            in_specs=[pl.BlockSpec((tm, ns, lanes),
                                   lambda i, te: (jnp.minimum(i, jnp.maximum(te[r // tm] - 1, 0)), 0, 0)),
                      pl.BlockSpec(memory_space=pl.ANY),
                      pl.BlockSpec((1, 1, de2), lambda i, te: (te[i], 0, 0)),
                      pl.BlockSpec(memory_space=pl.ANY),
                      pl.BlockSpec((1, 1, d), lambda i, te: (te[i], 0, 0))],
            out_specs=pl.BlockSpec((tm, ns, lanes), lambda i, te: (i, 0, 0)),
            scratch_shapes=[pltpu.VMEM((2, d, de2), F32), pltpu.VMEM((2, de2 // 2, d), F32),
                            pltpu.VMEM((d, de2), BF16), pltpu.VMEM((de2 // 2, d), BF16),
                            pltpu.SemaphoreType.DMA((2, 2))]),
        out_shape=jax.ShapeDtypeStruct((r, ns, lanes), U32),
        compiler_params=pltpu.CompilerParams(dimension_semantics=("arbitrary",), vmem_limit_bytes=VMEM_LIMIT),
        name="moe_ffn",
    )(tile_table, xs, w_gate_up, b_gate_up, w_down, b_down)


def _combine_dense_kernel(h_ref, gate_ref, gfin_ref, y0_ref, y1_ref, y2_ref, y3_ref, o_ref):
    half = y0_ref.shape[1] * y0_ref.shape[2]
    lo = h_ref[:, :half]
    hi = h_ref[:, half:]
    for k, y_ref in enumerate((y0_ref, y1_ref, y2_ref, y3_ref)):
        g = gate_ref[:, k:k + 1]
        w = _load_slabs(y_ref)
        lo = lo + g * _unpack_lo(w)
        hi = hi + g * _unpack_hi(w)
    ms = (jnp.sum(lo * lo, axis=1, keepdims=True) + jnp.sum(hi * hi, axis=1, keepdims=True)) / (2 * half)
    inv = lax.rsqrt(ms + RMS_EPS)
    o_ref[:, :half] = lo * inv * gfin_ref[:, :half]
    o_ref[:, half:] = hi * inv * gfin_ref[:, half:]


def _combine_dense_into_kernel(h_ref, gate_ref, gfin_ref, y0_ref, y1_ref, y2_ref, y3_ref, prev_ref, o_ref):
    del prev_ref
    _combine_dense_kernel(h_ref, gate_ref, gfin_ref, y0_ref, y1_ref, y2_ref, y3_ref, o_ref)


def _combine_dense_call(h, gate, g_final, yk, part, n_parts, prev_out):
    t, d = h.shape
    tm = COMBINE_TM
    steps = t // tm // n_parts
    first = part * steps
    yspec = lambda k: pl.BlockSpec((tm,) + yk.shape[1:], lambda i: (k * steps + i, 0, 0))
    in_specs = [pl.BlockSpec((tm, d), lambda i: (first + i, 0)),
                pl.BlockSpec((tm, TOP_K), lambda i: (first + i, 0)),
                pl.BlockSpec((1, d), lambda i: (0, 0)),
                yspec(0), yspec(1), yspec(2), yspec(3)]
    args = [h, gate, g_final, yk, yk, yk, yk]
    aliases = {}
    body = _combine_dense_kernel
    if prev_out is not None:
        in_specs.append(pl.BlockSpec(memory_space=pl.ANY))
        args.append(prev_out)
        aliases = {len(args) - 1: 0}
        body = _combine_dense_into_kernel
    return pl.pallas_call(
        body,
        grid=(steps,),
        in_specs=in_specs,
        out_specs=pl.BlockSpec((tm, d), lambda i: (first + i, 0)),
        out_shape=jax.ShapeDtypeStruct((t, d), F32),
        input_output_aliases=aliases,
        compiler_params=pltpu.CompilerParams(dimension_semantics=("arbitrary",), vmem_limit_bytes=VMEM_LIMIT),
        name="moe_combine",
    )(*args)


def _rope_freqs():
    def inv_freq(dh):
        return (ROPE_THETA ** (-jnp.arange(0, dh, 2, dtype=F32) / dh))[:, None]

    return inv_freq(MLA_ROPE), inv_freq(SWA_HEAD_DIM)


def _winprep_kernel(w_ref, o_ref):
    w = w_ref[...]
    rb = w.shape[0]
    c1 = MLA_Q_RANK + MLA_KV_RANK
    tail = w[:, c1 + MLA_ROPE:]
    lane = lax.broadcasted_iota(I32, (rb, LANES), 1)
    in_rope = (lane >= MLA_NOPE) & (lane < MLA_NOPE + MLA_ROPE)
    kr_p = jnp.where(in_rope, pltpu.roll(w[:, c1:c1 + LANES], MLA_NOPE, axis=1), 0.0)
    pieces = [w[:, :c1], kr_p, tail]
    off = 0
    for pc in pieces:
        o_ref[:, off:off + pc.shape[1]] = pc.astype(BF16)
        off += pc.shape[1]


def _winprep_call(w_in, layer):
    _, d, n = w_in.shape
    rb = 128
    return pl.pallas_call(
        _winprep_kernel,
        grid=(d // rb,),
        in_specs=[pl.BlockSpec((None, rb, n), lambda i: (layer, i, 0))],
        out_specs=pl.BlockSpec((rb, _D1), lambda i: (i, 0)),
        out_shape=jax.ShapeDtypeStruct((d, _D1), BF16),
        compiler_params=pltpu.CompilerParams(dimension_semantics=("arbitrary",), vmem_limit_bytes=VMEM_LIMIT),
        name="w_in_prep",
    )(w_in)


def _prep_weights(w_in, layer, w_mla_uq, w_mla_ukv):
    w_in_al = _winprep_call(w_in, layer)

    r = w_mla_uq.shape[0]
    wq = w_mla_uq.reshape(r, MLA_HEADS, MLA_NOPE + MLA_ROPE)
    zq = jnp.zeros((r, MLA_HEADS, LANES - MLA_NOPE - MLA_ROPE), w_mla_uq.dtype)
    wq_pad = jnp.concatenate([wq, zq], axis=-1).reshape(r, MLA_HEADS * LANES).astype(BF16)

    rk = w_mla_ukv.shape[0]
    wkv = w_mla_ukv.reshape(rk, MLA_HEADS, MLA_NOPE + MLA_V)
    wk_aug = jnp.concatenate([wkv[..., :MLA_NOPE], jnp.zeros((rk, MLA_HEADS, LANES - MLA_NOPE), w_mla_ukv.dtype)],
                             axis=-1).reshape(rk, MLA_HEADS * LANES).astype(BF16)
    wv_t = wkv[..., MLA_NOPE:].reshape(rk, MLA_HEADS * MLA_V).T.astype(BF16)
    return w_in_al, wq_pad, wk_aug, wv_t


def kernel(x, mem, positions, g_mix, w_in, g_mla_q, w_mla_uq, g_mla_kv, w_mla_ukv, w_mla_o, swa_sinks, w_swa_o,
           g_mem, w_mem_kv, w_xa_o, b_gate, w_out, g_ffn, w_router, b_router, w_gate_up, b_gate_up, w_down,
           b_down, g_final):
    b, s, d = x.shape
    t = b * s
    depth = g_mix.shape[0]
    h = x.reshape(t, d)
    pos = positions.astype(F32).reshape(1, t)
    fq, f64 = _rope_freqs()
    for l in range(depth):
        w_in_al, wq_pad, wk_aug, wv_t = _prep_weights(w_in, l, w_mla_uq[l], w_mla_ukv[l])
        (qm, km, vmt, qs, ksa, ksb, vsa, vsb, qx, gates) = _proj_call(
            h, pos, fq, f64, g_mix[l][None], w_in_al, g_mla_q[l][None], wq_pad,
            g_mla_kv[l][None], wk_aug, wv_t, b_gate[l][None])
        r3 = lambda a: a.reshape(b, s, a.shape[1])
        omla = _mla_call(r3(qm), r3(km), vmt, s).reshape(t, -1)
        oswa = _swa_call(swa_sinks[l], r3(qs), r3(ksa), r3(ksb), r3(vsa), r3(vsb)).reshape(t, -1)
        m = mem.shape[1]
        kvm = _memkv_call(mem.reshape(b * m, d), g_mem[l][None], w_mem_kv[l].astype(BF16)).reshape(b, m, -1)
        wr_t = w_router[l].T
        wr_hi = wr_t.astype(BF16)
        wr_split = jnp.concatenate([wr_hi, (wr_t - wr_hi.astype(F32)).astype(BF16)], axis=0)
        h_mid, hnp, idx, gate, rank, counts = _merge_call(
            h, omla, oswa, qx, gates, kvm, w_mla_o[l].astype(BF16), w_swa_o[l].astype(BF16),
            w_xa_o[l].astype(BF16), w_out[l].astype(BF16), g_ffn[l][None], wr_split, b_router[l][:, None], s)

        counts = counts[:, 0]
        padded = ((counts + FFN_TM - 1) // FFN_TM) * FFN_TM
        padded_end = jnp.cumsum(padded)
        offsets = padded_end - padded
        experts = jnp.arange(N_EXPERTS, dtype=I32)
        dest = (jnp.sum(jnp.where(idx[..., None] == experts, offsets, 0), axis=-1) + rank).reshape(-1).astype(I32)
        n_tiles = (t * TOP_K) // FFN_TM + N_EXPERTS
        n_used = (padded_end[-1] // FFN_TM).astype(I32)
        tile_start = jnp.minimum(jnp.arange(n_tiles, dtype=I32), jnp.maximum(n_used - 1, 0)) * FFN_TM
        tile_expert = jnp.sum((padded_end[None, :] <= tile_start[:, None]).astype(I32), axis=1)
        tile_expert = jnp.minimum(tile_expert, N_EXPERTS - 1)
        onehot = tile_expert[:, None] == experts[None, :]
        pick = lambda v: jnp.sum(jnp.where(onehot, v[None, :], 0), axis=1)
        tile_valid = jnp.clip(pick(counts) - (tile_start - pick(offsets)), 0, FFN_TM)
        nonempty = padded > 0
        tile_group = pick(jnp.cumsum(nonempty.astype(I32)) - 1)
        later = jnp.where(nonempty[None, :] & (experts[None, :] > experts[:, None]), experts[None, :], N_EXPERTS)
        next_expert = jnp.min(later, axis=1)
        tile_next = pick(jnp.where(next_expert < N_EXPERTS, next_expert, -1))
        te = jnp.concatenate([tile_expert, n_used[None], tile_valid.astype(I32), tile_group.astype(I32),
                              tile_next.astype(I32)])

        xs = _sc_scatter_rows(hnp, dest.reshape(TOP_K, t), n_tiles * FFN_TM)
        y = _ffn_call(te, xs, w_gate_up[l], b_gate_up[l][:, None, :], w_down[l], b_down[l][:, None, :])
        if l == depth - 1:
            gfin = g_final[None]
            gate_tk = gate.T
            dest_parts = dest.reshape(TOP_K, COMBINE_PARTS, t // COMBINE_PARTS)
            out = None
            for part in range(COMBINE_PARTS):
                yk = _sc_gather_rows(y, dest_parts[:, part].reshape(-1))
                out = _combine_dense_call(h_mid, gate_tk, gfin, yk, part, COMBINE_PARTS, out)
        else:
            raise NotImplementedError("depth > 1 needs a combine without the final norm")
        h = out
    return h.reshape(b, s, d)
```

```python
import functools
import math

import jax
import jax.numpy as jnp
from jax import lax
from jax.experimental import pallas as pl
from jax.experimental.pallas import tpu as pltpu
from jax.experimental.pallas import tpu_sc as plsc

F32 = jnp.float32
BF16 = jnp.bfloat16
U32 = jnp.uint32
I32 = jnp.int32

LANES = 128
ROPE_THETA = 10000.0
RMS_EPS = 1e-6
LOG2E = 1.4426950408889634

MLA_HEADS = 8
MLA_NOPE = 64
MLA_ROPE = 32
MLA_V = 64
MLA_Q_RANK = 256
MLA_KV_RANK = 128
SWA_HEADS = 8
SWA_KV_HEADS = 2
SWA_HEAD_DIM = 64
SWA_WINDOW = 128
XA_HEADS = 4
XA_HEAD_DIM = 128
N_EXPERTS = 32
TOP_K = 4
SWIGLU_ALPHA = 1.702
SWIGLU_LIMIT = 7.0
N_BRANCHES = 3

NEG_BIG = -1e30

PROJ_TM = 512
PROJ_SUB = 256
MLA_TQ = 1024
MLA_ONES_ROWS = 16
SWA_TS = 1024
MERGE_TM = 512
MERGE_SUB = 256
FFN_TM = 1024
FFN_SUB = 256
COMBINE_TM = 256
COMBINE_PARTS = 2

VMEM_LIMIT = 56 * 1024 * 1024


def _rms(x, g):
    return x * lax.rsqrt(jnp.mean(x * x, axis=-1, keepdims=True) + RMS_EPS) * g


def _store_slabs(ref, row0, value):
    rows, n, _ = ref.shape
    flat = ref.reshape(rows * n, LANES)
    for c in range(n):
        flat[pl.ds(row0 * n + c, value.shape[0], stride=n), :] = value[:, c * LANES:(c + 1) * LANES]


def _load_slabs(ref, row0=0, m=None):
    rows, n, _ = ref.shape
    m = rows if m is None else m
    flat = ref.reshape(rows * n, LANES)
    return jnp.concatenate([flat[pl.ds(row0 * n + c, m, stride=n), :] for c in range(n)], axis=1)


_A0, _A1 = 0, 512
_B0, _B1 = 512, 1280
_C0, _C1 = 1280, 1792
_D0, _D1 = 1792, 4864


def _rotate_half(x, d, lo, hi):
    n = x.shape[1]
    half = (hi - lo) // 2
    lane = lax.broadcasted_iota(I32, x.shape, 1) % d
    up = pltpu.roll(x, n - half, axis=1)
    dn = pltpu.roll(x, half, axis=1)
    return jnp.where((lane >= lo) & (lane < lo + half), -up, jnp.where((lane >= lo + half) & (lane < hi), dn, 0.0))


def _proj_kernel(x_ref, pos_ref, fq_ref, f64_ref, gmix_ref, win_ref, gq_ref, wq_ref,
                 gkv_ref, wk_ref, wv_ref, bgate_ref,
                 qm_ref, km_ref, vm_ref, qs_ref, ksa_ref, ksb_ref, vsa_ref, vsb_ref, qx_ref, gt_ref):
    for hf in range(x_ref.shape[0] // PROJ_SUB):
        rows = slice(hf * PROJ_SUB, (hf + 1) * PROJ_SUB)
        _proj_rows(x_ref[rows, :], pos_ref[:, rows], fq_ref, f64_ref, gmix_ref, win_ref, gq_ref, wq_ref, gkv_ref,
                   wk_ref, wv_ref, bgate_ref,
                   [r.at[rows] for r in (qm_ref, km_ref)], vm_ref.at[:, rows],
                   [r.at[rows] for r in (qs_ref, ksa_ref, ksb_ref, vsa_ref, vsb_ref, qx_ref, gt_ref)])


def _proj_rows(x, pos, fq_ref, f64_ref, gmix_ref, win_ref, gq_ref, wq_ref, gkv_ref, wk_ref, wv_ref, bgate_ref,
               mla_refs, vm_ref, other_refs):
    qm_ref, km_ref = mla_refs
    qs_ref, ksa_ref, ksb_ref, vsa_ref, vsb_ref, qx_ref, gt_ref = other_refs
    xn = _rms(x, gmix_ref[...]).astype(BF16)
    tm = x.shape[0]
    ang16 = fq_ref[...] * pos
    ang32 = f64_ref[...] * pos
    c16, s16, c32, s32 = jnp.cos(ang16), jnp.sin(ang16), jnp.cos(ang32), jnp.sin(ang32)
    one = jnp.ones((MLA_NOPE, tm), F32)
    zero = jnp.zeros((MLA_NOPE, tm), F32)
    pad = LANES - MLA_NOPE - MLA_ROPE
    cosq = jnp.concatenate([one, c16, c16, one[:pad]], axis=0).T
    sinq = jnp.concatenate([zero, s16, s16, zero[:pad]], axis=0).T
    cos64 = jnp.concatenate([c32, c32, c32, c32], axis=0).T
    sin64 = jnp.concatenate([s32, s32, s32, s32], axis=0).T
    rope_lo, rope_hi = MLA_NOPE, MLA_NOPE + MLA_ROPE

    xa = jnp.dot(xn, win_ref[:, _A0:_A1], preferred_element_type=F32)
    cqn = _rms(xa[:, 0:256], gq_ref[...]).astype(BF16)
    qa = jnp.dot(cqn, wq_ref[...], preferred_element_type=F32)
    qb = _rotate_half(qa, LANES, rope_lo, rope_hi)
    q_scale = LOG2E / math.sqrt(MLA_NOPE + MLA_ROPE)
    ckvn = _rms(xa[:, 256:384], gkv_ref[...]).astype(BF16)
    ka = jnp.dot(ckvn, wk_ref[...], preferred_element_type=F32)
    kr = xa[:, 384:512]
    krope = kr * cosq + _rotate_half(kr, LANES, rope_lo, rope_hi) * sinq
    for h in range(MLA_HEADS):
        sl = slice(h * LANES, (h + 1) * LANES)
        qm_ref[:, sl] = ((qa[:, sl] * cosq + qb[:, sl] * sinq) * q_scale).astype(BF16)
        km_ref[:, sl] = (ka[:, sl] + krope).astype(BF16)
    vm_ref[...] = lax.dot_general(wv_ref[...], ckvn, (((1,), (1,)), ((), ())),
                                  preferred_element_type=F32).astype(BF16)

    xb = jnp.dot(xn, win_ref[:, _B0:_B1], preferred_element_type=F32)
    s_scale = LOG2E / math.sqrt(SWA_HEAD_DIM)
    nq = SWA_HEADS * SWA_HEAD_DIM
    qs = xb[:, :nq]
    qs_rot = _rotate_half(qs, SWA_HEAD_DIM, 0, SWA_HEAD_DIM)
    for p in range(SWA_HEADS // 2):
        sl = slice(p * LANES, (p + 1) * LANES)
        qs_ref[:, sl] = ((qs[:, sl] * cos64 + qs_rot[:, sl] * sin64) * s_scale).astype(BF16)
    ks = xb[:, nq:nq + LANES]
    ks = ks * cos64 + _rotate_half(ks, SWA_HEAD_DIM, 0, SWA_HEAD_DIM) * sin64
    ksa_ref[...] = ks.astype(BF16)
    ksb_ref[...] = pltpu.roll(ks, 64, axis=1).astype(BF16)
    vs = xb[:, nq + LANES:nq + 2 * LANES]
    vsa_ref[...] = vs.astype(BF16)
    vsb_ref[...] = pltpu.roll(vs, 64, axis=1).astype(BF16)

    xc = jnp.dot(xn, win_ref[:, _C0:_C1], preferred_element_type=F32)
    qx_ref[...] = (xc * (LOG2E / math.sqrt(XA_HEAD_DIM))).astype(BF16)

    xd = jnp.dot(xn, win_ref[:, _D0:_D1], preferred_element_type=F32) + bgate_ref[...]
    gt_ref[...] = jax.nn.sigmoid(xd).astype(BF16)


def _proj_call(x2, pos, fq, f64, g_mix, w_in_al, g_q, wq, g_kv, wk_aug, wv, b_gate):
    t, d = x2.shape
    tm = PROJ_TM
    row = lambda n: pl.BlockSpec((tm, n), lambda i: (i, 0))
    full = lambda a: pl.BlockSpec(a.shape, lambda i: (0,) * a.ndim)
    out_cols = [1024, 1024, 512, 128, 128, 128, 128, 512, 3072]
    out_specs = [row(n) for n in out_cols]
    out_shape = [jax.ShapeDtypeStruct((t, n), BF16) for n in out_cols]
    vt_rows = wv.shape[0]
    out_specs.insert(2, pl.BlockSpec((vt_rows, tm), lambda i: (0, i)))
    out_shape.insert(2, jax.ShapeDtypeStruct((vt_rows, t), BF16))
    return pl.pallas_call(
        _proj_kernel,
        grid=(t // tm,),
        in_specs=[row(d), pl.BlockSpec((1, tm), lambda i: (0, i)), full(fq), full(f64), full(g_mix),
                  pl.BlockSpec(w_in_al.shape, lambda i: (0, 0), pipeline_mode=pl.Buffered(1)),
                  full(g_q), full(wq), full(g_kv), full(wk_aug), full(wv), full(b_gate)],
        out_specs=out_specs,
        out_shape=out_shape,
        compiler_params=pltpu.CompilerParams(dimension_semantics=("arbitrary",), vmem_limit_bytes=VMEM_LIMIT),
        name="proj",
    )(x2, pos, fq, f64, g_mix, w_in_al, g_q, wq, g_kv, wk_aug, wv, b_gate)


def _mla_kernel(q_ref, k_ref, vt_ref, o_ref, sa_ref, sb_ref, ma_ref, mb_ref, m_ref, acc_ref, *, tq):
    i = pl.program_id(2)
    m_ref[...] = jnp.full(m_ref.shape, NEG_BIG, F32)
    acc_ref[...] = jnp.zeros(acc_ref.shape, F32)
    ones = jnp.ones((MLA_ONES_ROWS, tq), BF16)

    def scores(j, bufs):
        s_ref, mx_ref = bufs
        k0 = pl.multiple_of(j * tq, tq)
        for hh in range(2):
            sl = slice(hh * LANES, (hh + 1) * LANES)
            st = lax.dot_general(k_ref[0, pl.ds(k0, tq), sl], q_ref[0, :, sl], (((1,), (1,)), ((), ())),
                                 preferred_element_type=F32)
            s_ref[hh] = st
            mx_ref[hh] = jnp.max(st, axis=0, keepdims=True)

    def update(j, bufs, masked):
        s_ref, mx_ref = bufs
        k0 = pl.multiple_of(j * tq, tq)
        hq = tq // 2
        parts = ((slice(0, hq), hq, 0), (slice(hq, tq), tq, hq)) if masked else ((slice(0, tq), tq, 0),)
        for hh in range(2):
            for cols, nk, q0 in parts:
                nq = cols.stop - cols.start
                vt = jnp.concatenate([vt_ref[hh * MLA_V:(hh + 1) * MLA_V, pl.ds(k0, nk)], ones[:, :nk]], axis=0)
                st = s_ref[hh, 0:nk, cols]
                if masked:
                    kj = lax.broadcasted_iota(I32, (nk, nq), 0)
                    qi = lax.broadcasted_iota(I32, (nk, nq), 1) + q0
                    st = jnp.where(kj <= qi, st, NEG_BIG)
                    m_tile = jnp.max(st, axis=0, keepdims=True)
                else:
                    m_tile = mx_ref[hh]
                m_old = m_ref[hh, :, cols]
                m_new = jnp.maximum(m_old, m_tile)
                alpha = jnp.exp2(m_old - m_new)
                pt = jnp.exp2(st - m_new)
                acc_ref[hh, :, cols] = (alpha * acc_ref[hh, :, cols]
                                        + jnp.dot(vt, pt.astype(BF16), preferred_element_type=F32))
                m_ref[hh, :, cols] = m_new

    buf_a = (sa_ref, ma_ref)
    buf_b = (sb_ref, mb_ref)
    scores(0, buf_a)

    def body(jj, carry):
        scores(2 * jj + 1, buf_b)
        update(2 * jj, buf_a, False)
        scores(2 * jj + 2, buf_a)
        update(2 * jj + 1, buf_b, False)
        return carry

    lax.fori_loop(0, i // 2, body, 0)

    @pl.when(i % 2 == 0)
    def _():
        update(i, buf_a, True)

    @pl.when(i % 2 == 1)
    def _():
        scores(i, buf_b)
        update(i - 1, buf_a, False)
        update(i, buf_b, True)

    ot = jnp.concatenate([acc_ref[hh, :MLA_V] / acc_ref[hh, MLA_V:MLA_V + 1] for hh in range(2)],
                         axis=0)
    o_ref[0] = ot.T.astype(BF16)


def _mla_call(q, k, vt, seq):
    b, s, _ = q.shape
    assert s == seq
    tq = min(MLA_TQ, s)
    n_pairs = MLA_HEADS // 2
    return pl.pallas_call(
        functools.partial(_mla_kernel, tq=tq),
        grid=(b, n_pairs, s // tq),
        in_specs=[pl.BlockSpec((1, tq, 2 * LANES), lambda bi, hp, i: (bi, i, hp)),
                  pl.BlockSpec((1, s, 2 * LANES), lambda bi, hp, i: (bi, 0, hp)),
                  pl.BlockSpec((2 * MLA_V, s), lambda bi, hp, i: (hp, bi))],
        out_specs=pl.BlockSpec((1, tq, LANES), lambda bi, hp, i: (bi, i, hp)),
        out_shape=jax.ShapeDtypeStruct((b, s, n_pairs * LANES), BF16),
        scratch_shapes=[pltpu.VMEM((2, tq, tq), F32), pltpu.VMEM((2, tq, tq), F32),
                        pltpu.VMEM((2, 1, tq), F32), pltpu.VMEM((2, 1, tq), F32),
                        pltpu.VMEM((2, 1, tq), F32),
                        pltpu.VMEM((2, MLA_V + MLA_ONES_ROWS, tq), F32)],
        compiler_params=pltpu.CompilerParams(dimension_semantics=("arbitrary",) * 3, vmem_limit_bytes=VMEM_LIMIT),
        name="mla_attn",
    )(q, k, vt)


def _swa_kernel(sink_ref, q_ref, ka_ref, kb_ref, va_ref, vb_ref, kah_ref, kbh_ref, vah_ref, vbh_ref, o_ref, *, ts):
    w = SWA_WINDOW
    i = pl.program_id(1)
    ka = jnp.concatenate([kah_ref[0], ka_ref[0]], axis=0)
    kb = jnp.concatenate([kbh_ref[0], kb_ref[0]], axis=0)
    va = jnp.concatenate([vah_ref[0], va_ref[0]], axis=0)
    vb = jnp.concatenate([vbh_ref[0], vb_ref[0]], axis=0)
    lane_k = lax.broadcasted_iota(I32, (2 * w, LANES), 1)
    low = lane_k < SWA_HEAD_DIM
    qi = lax.broadcasted_iota(I32, (2 * w, 2 * w), 0) % w
    kj = lax.broadcasted_iota(I32, (2 * w, 2 * w), 1)
    diff = qi + w - kj
    band = (diff >= 0) & (diff < w)
    lane_o = lax.broadcasted_iota(I32, (w, LANES), 1)
    row2 = lax.broadcasted_iota(I32, (2 * w, 1), 0)
    zero = jnp.zeros((), BF16)
    stacks = ((0, ka, True, va), (1, kb, False, vb), (4, kb, True, vb), (5, ka, False, va))
    for n in range(ts // w):
        mask = band & ((i * (ts // w) + n > 0) | (kj >= w))
        res = []
        for h0, ksrc, keep_low, vsrc in stacks:
            p0 = h0 // 2
            q = jnp.concatenate([q_ref[0, n * w:(n + 1) * w, p0 * LANES:(p0 + 1) * LANES],
                                 q_ref[0, n * w:(n + 1) * w, (p0 + 1) * LANES:(p0 + 2) * LANES]], axis=0)
            kwin = ksrc[n * w:n * w + 2 * w]
            kwin = jnp.where(low if keep_low else ~low, kwin, zero)
            vwin = vsrc[n * w:n * w + 2 * w]
            s = lax.dot_general(q, kwin, (((1,), (1,)), ((), ())), preferred_element_type=F32)
            s = jnp.where(mask, s, NEG_BIG)
            sink = jnp.where(row2 < w, sink_ref[h0], sink_ref[h0 + 2]) * LOG2E
            m = jnp.maximum(jnp.max(s, axis=1, keepdims=True), sink)
            p = jnp.exp2(s - m)
            den = jnp.sum(p, axis=1, keepdims=True) + jnp.exp2(sink - m)
            o = jnp.dot(p.astype(BF16), vwin, preferred_element_type=F32) / den
            res.append(o)
        o02, o13, o46, o57 = res
        sel = lane_o < SWA_HEAD_DIM
        rows = slice(n * w, (n + 1) * w)
        o_ref[0, rows, 0 * LANES:1 * LANES] = jnp.where(sel, o02[:w], o13[:w]).astype(BF16)
        o_ref[0, rows, 1 * LANES:2 * LANES] = jnp.where(sel, o02[w:], o13[w:]).astype(BF16)
        o_ref[0, rows, 2 * LANES:3 * LANES] = jnp.where(sel, o46[:w], o57[:w]).astype(BF16)
        o_ref[0, rows, 3 * LANES:4 * LANES] = jnp.where(sel, o46[w:], o57[w:]).astype(BF16)


def _swa_call(sinks, q, ksa, ksb, vsa, vsb):
    b, s, _ = q.shape
    ts = min(SWA_TS, s)
    w = SWA_WINDOW
    r = ts // w
    main = pl.BlockSpec((1, ts, LANES), lambda bi, i: (bi, i, 0))
    halo = pl.BlockSpec((1, w, LANES), lambda bi, i: (bi, jnp.maximum(i * r - 1, 0), 0))
    return pl.pallas_call(
        functools.partial(_swa_kernel, ts=ts),
        grid=(b, s // ts),
        in_specs=[pl.BlockSpec(memory_space=pltpu.SMEM),
                  pl.BlockSpec((1, ts, 4 * LANES), lambda bi, i: (bi, i, 0)),
                  main, main, main, main, halo, halo, halo, halo],
        out_specs=pl.BlockSpec((1, ts, 4 * LANES), lambda bi, i: (bi, i, 0)),
        out_shape=jax.ShapeDtypeStruct((b, s, 4 * LANES), BF16),
        compiler_params=pltpu.CompilerParams(dimension_semantics=("arbitrary",) * 2, vmem_limit_bytes=VMEM_LIMIT),
        name="swa_attn",
    )(sinks, q, ksa, ksb, vsa, vsb, ksa, ksb, vsa, vsb)


def _memkv_kernel(mem_ref, g_ref, w_ref, o_ref):
    mn = _rms(mem_ref[...], g_ref[...]).astype(BF16)
    o_ref[...] = jnp.dot(mn, w_ref[...], preferred_element_type=F32).astype(BF16)


def _memkv_call(mem2, g_mem, w_mem_kv):
    n, d = mem2.shape
    tm = min(256, n)
    return pl.pallas_call(
        _memkv_kernel,
        grid=(n // tm,),
        in_specs=[pl.BlockSpec((tm, d), lambda i: (i, 0)),
                  pl.BlockSpec(g_mem.shape, lambda i: (0, 0)),
                  pl.BlockSpec(w_mem_kv.shape, lambda i: (0, 0))],
        out_specs=pl.BlockSpec((tm, w_mem_kv.shape[1]), lambda i: (i, 0)),
        out_shape=jax.ShapeDtypeStruct((n, w_mem_kv.shape[1]), BF16),
        compiler_params=pltpu.CompilerParams(dimension_semantics=("arbitrary",), vmem_limit_bytes=VMEM_LIMIT),
        name="mem_kv",
    )(mem2, g_mem, w_mem_kv)


def _merge_kernel(x_ref, omla_ref, oswa_ref, qx_ref, gt_ref, kvm_ref, wmo_ref, wso_ref, wxo_ref, wout_ref,
                  gffn_ref, wr_ref, br_ref,
                  h_ref, hnp_ref, idx_ref, gate_ref, rank_ref, cnt_ref, run_ref, *, tm, sub):
    @pl.when(pl.program_id(0) == 0)
    def _():
        run_ref[...] = jnp.zeros(run_ref.shape, F32)

    d = x_ref.shape[1]
    kv_cols = XA_HEADS * XA_HEAD_DIM
    erow = lax.broadcasted_iota(I32, (N_EXPERTS, sub), 0)
    tri_t = (lax.broadcasted_iota(I32, (sub, sub), 0) < lax.broadcasted_iota(I32, (sub, sub), 1)).astype(BF16)
    nt = (((1,), (1,)), ((), ()))
    run = run_ref[...]
    for hf in range(tm // sub):
        rows = slice(hf * sub, (hf + 1) * sub)

        oxs = []
        for hd in range(XA_HEADS):
            sl = slice(hd * LANES, (hd + 1) * LANES)
            km = kvm_ref[0, :, sl]
            vm = kvm_ref[0, :, kv_cols + hd * LANES:kv_cols + (hd + 1) * LANES]
            s = lax.dot_general(qx_ref[rows, sl], km, nt, preferred_element_type=F32)
            p = jnp.exp2(s - jnp.max(s, axis=1, keepdims=True))
            den = jnp.sum(p, axis=1, keepdims=True)
            oxs.append((jnp.dot(p.astype(BF16), vm, preferred_element_type=F32) / den).astype(BF16))
        oxa = jnp.concatenate(oxs, axis=1)

        merged = (gt_ref[rows, 0:d].astype(F32) * jnp.dot(omla_ref[rows, :], wmo_ref[...], preferred_element_type=F32)
                  + gt_ref[rows, d:2 * d].astype(F32) * jnp.dot(oswa_ref[rows, :], wso_ref[...],
                                                                 preferred_element_type=F32)
                  + gt_ref[rows, 2 * d:3 * d].astype(F32) * jnp.dot(oxa, wxo_ref[...], preferred_element_type=F32))
        h = x_ref[rows, :] + jnp.dot(merged.astype(BF16), wout_ref[...], preferred_element_type=F32)
        h_ref[rows, :] = h

        hn = _rms(h, gffn_ref[...])
        hn_hi = hn.astype(BF16)
        hn_hi32 = hn_hi.astype(F32)
        hn_lo = (hn - hn_hi32).astype(BF16)
        bits = pltpu.bitcast(hn_hi32, U32)
        _store_slabs(hnp_ref, hf * sub, (bits[:, : d // 2] >> 16) | (bits[:, d // 2:] & jnp.uint32(0xFFFF0000)))

        part = lax.dot_general(wr_ref[...], hn_hi, nt, preferred_element_type=F32)
        logits_t = (part[:N_EXPERTS] + part[N_EXPERTS:]
                    + lax.dot_general(wr_ref[0:N_EXPERTS, :], hn_lo, nt, preferred_element_type=F32) + br_ref[...])

        work = logits_t
        vals, idxs, hots = [], [], []
        for _ in range(TOP_K):
            mx = jnp.max(work, axis=0, keepdims=True)
            ix = jnp.min(jnp.where(work == mx, erow, N_EXPERTS), axis=0, keepdims=True)
            hot = erow == ix
            work = jnp.where(hot, -jnp.inf, work)
            vals.append(mx)
            idxs.append(ix)
            hots.append(hot)
        es = [jnp.exp(v - vals[0]) for v in vals]
        den = es[0] + es[1] + es[2] + es[3]
        sel_t = (hots[0] | hots[1] | hots[2] | hots[3])
        prefix_t = jnp.dot(sel_t.astype(BF16), tri_t, preferred_element_type=F32) + run
        for k in range(TOP_K):
            idx_ref[k:k + 1, rows] = idxs[k]
            gate_ref[k:k + 1, rows] = es[k] / den
            rank_ref[k:k + 1, rows] = jnp.sum(jnp.where(hots[k], prefix_t, 0.0), axis=0, keepdims=True).astype(I32)
        run = run + jnp.sum(sel_t.astype(F32), axis=1, keepdims=True)
    run_ref[...] = run
    cnt_ref[...] = run.astype(I32)


def _merge_call(x2, omla, oswa, qx, gates, kvm, wmo, wso, wxo, wout, g_ffn, wr_split, b_router_col, seq):
    t, d = x2.shape
    tm = MERGE_TM
    per_b = seq // tm
    row = lambda n: pl.BlockSpec((tm, n), lambda i: (i, 0))
    col = lambda: pl.BlockSpec((TOP_K, tm), lambda i: (0, i))
    full = lambda a: pl.BlockSpec(a.shape, lambda i: (0,) * a.ndim)
    return pl.pallas_call(
        functools.partial(_merge_kernel, tm=tm, sub=MERGE_SUB),
        grid=(t // tm,),
        in_specs=[row(d), row(512), row(512), row(512), row(3 * d),
                  pl.BlockSpec((1,) + kvm.shape[1:], lambda i: (i // per_b, 0, 0)),
                  full(wmo), full(wso), full(wxo), full(wout), full(g_ffn), full(wr_split), full(b_router_col)],
        out_specs=[row(d), pl.BlockSpec((tm, d // 2 // LANES, LANES), lambda i: (i, 0, 0)), col(), col(), col(),
                   pl.BlockSpec((N_EXPERTS, 1), lambda i: (0, 0))],
        out_shape=[jax.ShapeDtypeStruct((t, d), F32), jax.ShapeDtypeStruct((t, d // 2 // LANES, LANES), U32),
                   jax.ShapeDtypeStruct((TOP_K, t), I32), jax.ShapeDtypeStruct((TOP_K, t), F32),
                   jax.ShapeDtypeStruct((TOP_K, t), I32), jax.ShapeDtypeStruct((N_EXPERTS, 1), I32)],
        scratch_shapes=[pltpu.VMEM((N_EXPERTS, 1), F32)],
        compiler_params=pltpu.CompilerParams(dimension_semantics=("arbitrary",), vmem_limit_bytes=VMEM_LIMIT),
        name="merge_router",
    )(x2, omla, oswa, qx, gates, kvm, wmo, wso, wxo, wout, g_ffn, wr_split, b_router_col)


SC_CORES = 2
SC_SUBCORES = 16
SC_WORKERS = SC_CORES * SC_SUBCORES
SC_CHUNK = 64


def _sc_mesh():
    return plsc.VectorSubcoreMesh(core_axis_name="c", subcore_axis_name="s",
                                  num_cores=SC_CORES, num_subcores=SC_SUBCORES)


def _sc_index_blocks(idx):
    n = idx.shape[0]
    per_w = n // SC_WORKERS
    n_ch = per_w // SC_CHUNK
    assert per_w * SC_WORKERS == n and n_ch * SC_CHUNK == per_w
    return idx.reshape(SC_WORKERS, n_ch, SC_CHUNK), per_w, n_ch


def _sc_two_buffer_loop(n_ch, load, stores, bufs, load_sems, store_sems):
    assert n_ch % 2 == 0
    a, b = bufs
    la, lb = load_sems
    sa, sb = store_sems

    def start(cps):
        for cp in cps:
            cp.start()

    def wait(cps):
        for cp in cps:
            cp.wait()

    load(0, a, la).start()

    @pl.loop(0, n_ch, step=2)
    def _(j):
        load(j, a, la).wait()

        @pl.when(j > 0)
        def _():
            wait(stores(j - 1, b, sb))

        load(j + 1, b, lb).start()
        start(stores(j, a, sa))
        load(j + 1, b, lb).wait()
        wait(stores(j, a, sa))

        @pl.when(j + 2 < n_ch)
        def _():
            load(j + 2, a, la).start()

        start(stores(j + 1, b, sb))

    wait(stores(n_ch - 1, b, sb))


def _sc_scratch(n_ch, row_shape, dtype):
    return [pltpu.VMEM((n_ch, SC_CHUNK), I32), pltpu.VMEM((SC_CHUNK,) + row_shape, dtype),
            pltpu.VMEM((SC_CHUNK,) + row_shape, dtype)] + [pltpu.SemaphoreType.DMA] * 4


def _sc_scatter_rows(src, idx, n_out):
    fan, n_src = idx.shape
    per_w = n_src // SC_WORKERS
    n_ch = per_w // SC_CHUNK
    assert per_w * SC_WORKERS == n_src and n_ch * SC_CHUNK == per_w
    idx3 = idx.reshape(fan, SC_WORKERS, n_ch, SC_CHUNK).transpose(1, 0, 2, 3).reshape(SC_WORKERS, fan * n_ch, SC_CHUNK)

    @functools.partial(
        pl.kernel, mesh=_sc_mesh(),
        out_type=jax.ShapeDtypeStruct((n_out,) + src.shape[1:], src.dtype),
        scratch_types=_sc_scratch(fan * n_ch, src.shape[1:], src.dtype),
        name="moe_dispatch_sc")
    def k(src_hbm, idx_hbm, out_hbm, idx_v, rows_a, rows_b, la, lb, sa, sb):
        wid = lax.axis_index("s") * SC_CORES + lax.axis_index("c")
        base = wid * per_w
        pltpu.sync_copy(idx_hbm.at[wid], idx_v)

        def load(c, buf, sem):
            return pltpu.make_async_copy(src_hbm.at[pl.ds(base + c * SC_CHUNK, SC_CHUNK)], buf, sem)

        def stores(c, buf, sem):
            return tuple(pltpu.make_async_copy(buf, out_hbm.at[idx_v.at[f * n_ch + c]], sem) for f in range(fan))

        _sc_two_buffer_loop(n_ch, load, stores, (rows_a, rows_b), (la, lb), (sa, sb))

    return k(src, idx3)


def _sc_gather_rows(table, idx):
    idx3, per_w, n_ch = _sc_index_blocks(idx)

    @functools.partial(
        pl.kernel, mesh=_sc_mesh(),
        out_type=jax.ShapeDtypeStruct((idx.shape[0],) + table.shape[1:], table.dtype),
        scratch_types=_sc_scratch(n_ch, table.shape[1:], table.dtype),
        name="moe_gather_sc")
    def k(table_hbm, idx_hbm, out_hbm, idx_v, rows_a, rows_b, la, lb, sa, sb):
        wid = lax.axis_index("s") * SC_CORES + lax.axis_index("c")
        base = wid * per_w
        pltpu.sync_copy(idx_hbm.at[wid], idx_v)

        def load(c, buf, sem):
            return pltpu.make_async_copy(table_hbm.at[idx_v.at[c]], buf, sem)

        def stores(c, buf, sem):
            return (pltpu.make_async_copy(buf, out_hbm.at[pl.ds(base + c * SC_CHUNK, SC_CHUNK)], sem),)

        _sc_two_buffer_loop(n_ch, load, stores, (rows_a, rows_b), (la, lb), (sa, sb))

    return k(table, idx3)


def _unpack_lo(w):
    return pltpu.bitcast(w << 16, F32)


def _unpack_hi(w):
    return pltpu.bitcast(w & jnp.uint32(0xFFFF0000), F32)


def _ffn_kernel(te_ref, xs_ref, wgu_hbm, bgu_ref, wd_hbm, bd_ref, y_ref, wgu_f32, wd_f32, wgu_bf, wd_bf, wsem):
    i = pl.program_id(0)
    n = pl.num_programs(0)
    n_used = te_ref[n]
    e = te_ref[i]
    first = (i == 0) | (e != te_ref[jnp.maximum(i - 1, 0)])
    slot = te_ref[2 * n + 1 + i] % 2
    nxt = te_ref[3 * n + 1 + i]

    def fetch(expert, s):
        return (pltpu.make_async_copy(wgu_hbm.at[expert], wgu_f32.at[s], wsem.at[0, s]),
                pltpu.make_async_copy(wd_hbm.at[expert], wd_f32.at[s], wsem.at[1, s]))

    @pl.when(i == 0)
    def _():
        for cp in fetch(e, slot):
            cp.start()

    @pl.when((i < n_used) & first)
    def _():
        for cp in fetch(e, slot):
            cp.wait()
        wgu_bf[...] = wgu_f32[slot].astype(BF16)
        wd_bf[...] = wd_f32[slot].astype(BF16)

        @pl.when(nxt >= 0)
        def _():
            for cp in fetch(nxt, 1 - slot):
                cp.start()

    valid = jnp.where(i < n_used, te_ref[n + 1 + i], 0)
    rows, n_slab, _ = xs_ref.shape
    half = n_slab * LANES
    for sb in range(rows // FFN_SUB):
        r0 = sb * FFN_SUB

        @pl.when(valid > r0)
        def _():
            w = _load_slabs(xs_ref, r0, FFN_SUB)
            w = jnp.where(lax.broadcasted_iota(I32, w.shape, 0) < valid - r0, w, jnp.uint32(0))
            x_lo = _unpack_lo(w).astype(BF16)
            x_hi = _unpack_hi(w).astype(BF16)
            gu = (jnp.dot(x_lo, wgu_bf[0:half, :], preferred_element_type=F32)
                  + jnp.dot(x_hi, wgu_bf[half:, :], preferred_element_type=F32) + bgu_ref[0])
            de = gu.shape[1] // 2
            x_glu = jnp.minimum(gu[:, :de], SWIGLU_LIMIT)
            x_lin = jnp.clip(gu[:, de:], -SWIGLU_LIMIT, SWIGLU_LIMIT)
            hdn = x_glu * jax.nn.sigmoid(SWIGLU_ALPHA * x_glu) * (x_lin + 1.0)
            y = jnp.dot(hdn.astype(BF16), wd_bf[...], preferred_element_type=F32) + bd_ref[0]
            bits = pltpu.bitcast(y.astype(BF16).astype(F32), U32)
            _store_slabs(y_ref, r0, (bits[:, :half] >> 16) | (bits[:, half:] & jnp.uint32(0xFFFF0000)))

        @pl.when(valid <= r0)
        def _():
            y_ref[r0:r0 + FFN_SUB] = jnp.zeros((FFN_SUB,) + y_ref.shape[1:], U32)


def _ffn_call(tile_table, xs, w_gate_up, b_gate_up, w_down, b_down):
    r, ns, lanes = xs.shape
    tm = FFN_TM
    ne, d, de2 = w_gate_up.shape
    return pl.pallas_call(
        _ffn_kernel,
        grid_spec=pltpu.PrefetchScalarGridSpec(
            num_scalar_prefetch=1,
            grid=(r // tm,),
            in_specs=[pl.BlockSpec((tm, ns, lanes),
                                   lambda i, te: (jnp.minimum(i, jnp.maximum(te[r // tm] - 1, 0)), 0, 0)),
                      pl.BlockSpec(memory_space=pl.ANY),
                      pl.BlockSpec((1, 1, de2), lambda i, te: (te[i], 0, 0)),
                      pl.BlockSpec(memory_space=pl.ANY),
                      pl.BlockSpec((1, 1, d), lambda i, te: (te[i], 0, 0))],
            out_specs=pl.BlockSpec((tm, ns, lanes), lambda i, te: (i, 0, 0)),
            scratch_shapes=[pltpu.VMEM((2, d, de2), F32), pltpu.VMEM((2, de2 // 2, d), F32),
                            pltpu.VMEM((d, de2), BF16), pltpu.VMEM((de2 // 2, d), BF16),
                            pltpu.SemaphoreType.DMA((2, 2))]),
        out_shape=jax.ShapeDtypeStruct((r, ns, lanes), U32),
        compiler_params=pltpu.CompilerParams(dimension_semantics=("arbitrary",), vmem_limit_bytes=VMEM_LIMIT),
        name="moe_ffn",
    )(tile_table, xs, w_gate_up, b_gate_up, w_down, b_down)


def _combine_dense_kernel(h_ref, gate_ref, gfin_ref, y0_ref, y1_ref, y2_ref, y3_ref, o_ref):
    half = y0_ref.shape[1] * y0_ref.shape[2]
    lo = h_ref[:, :half]
    hi = h_ref[:, half:]
    for k, y_ref in enumerate((y0_ref, y1_ref, y2_ref, y3_ref)):
        g = gate_ref[:, k:k + 1]
        w = _load_slabs(y_ref)
        lo = lo + g * _unpack_lo(w)
        hi = hi + g * _unpack_hi(w)
    ms = (jnp.sum(lo * lo, axis=1, keepdims=True) + jnp.sum(hi * hi, axis=1, keepdims=True)) / (2 * half)
    inv = lax.rsqrt(ms + RMS_EPS)
    o_ref[:, :half] = lo * inv * gfin_ref[:, :half]
    o_ref[:, half:] = hi * inv * gfin_ref[:, half:]


def _combine_dense_into_kernel(h_ref, gate_ref, gfin_ref, y0_ref, y1_ref, y2_ref, y3_ref, prev_ref, o_ref):
    del prev_ref
    _combine_dense_kernel(h_ref, gate_ref, gfin_ref, y0_ref, y1_ref, y2_ref, y3_ref, o_ref)


def _combine_dense_call(h, gate, g_final, yk, part, n_parts, prev_out):
    t, d = h.shape
    tm = COMBINE_TM
    steps = t // tm // n_parts
    first = part * steps
    yspec = lambda k: pl.BlockSpec((tm,) + yk.shape[1:], lambda i: (k * steps + i, 0, 0))
    in_specs = [pl.BlockSpec((tm, d), lambda i: (first + i, 0)),
                pl.BlockSpec((tm, TOP_K), lambda i: (first + i, 0)),
                pl.BlockSpec((1, d), lambda i: (0, 0)),
                yspec(0), yspec(1), yspec(2), yspec(3)]
    args = [h, gate, g_final, yk, yk, yk, yk]
    aliases = {}
    body = _combine_dense_kernel
    if prev_out is not None:
        in_specs.append(pl.BlockSpec(memory_space=pl.ANY))
        args.append(prev_out)
        aliases = {len(args) - 1: 0}
        body = _combine_dense_into_kernel
    return pl.pallas_call(
        body,
        grid=(steps,),
        in_specs=in_specs,
        out_specs=pl.BlockSpec((tm, d), lambda i: (first + i, 0)),
        out_shape=jax.ShapeDtypeStruct((t, d), F32),
        input_output_aliases=aliases,
        compiler_params=pltpu.CompilerParams(dimension_semantics=("arbitrary",), vmem_limit_bytes=VMEM_LIMIT),
        name="moe_combine",
    )(*args)


def _rope_freqs():
    def inv_freq(dh):
        return (ROPE_THETA ** (-jnp.arange(0, dh, 2, dtype=F32) / dh))[:, None]

    return inv_freq(MLA_ROPE), inv_freq(SWA_HEAD_DIM)


def _winprep_kernel(wt_ref, o_ref):
    j = pl.program_id(0)
    c1 = MLA_Q_RANK + MLA_KV_RANK
    kr_block = c1 // LANES

    @pl.when(j != kr_block)
    def _():
        row0 = jnp.where(j < kr_block, j * LANES, c1 + MLA_ROPE + (j - kr_block - 1) * LANES)
        o_ref[...] = wt_ref[pl.ds(pl.multiple_of(row0, 8), LANES), :].T.astype(BF16)

    @pl.when(j == kr_block)
    def _():
        d = wt_ref.shape[1]
        blk = jnp.concatenate([jnp.zeros((MLA_NOPE, d), F32), wt_ref[c1:c1 + MLA_ROPE, :],
                               jnp.zeros((LANES - MLA_NOPE - MLA_ROPE, d), F32)], axis=0)
        o_ref[...] = blk.T.astype(BF16)


def _winprep_call(w_in, layer):
    _, d, n = w_in.shape
    w_t = jnp.swapaxes(w_in, 1, 2)
    return pl.pallas_call(
        _winprep_kernel,
        grid=(_D1 // LANES,),
        in_specs=[pl.BlockSpec((None, n, d), lambda j: (layer, 0, 0), pipeline_mode=pl.Buffered(1))],
        out_specs=pl.BlockSpec((d, LANES), lambda j: (0, j)),
        out_shape=jax.ShapeDtypeStruct((d, _D1), BF16),
        compiler_params=pltpu.CompilerParams(dimension_semantics=("arbitrary",), vmem_limit_bytes=VMEM_LIMIT),
        name="w_in_prep",
    )(w_t)


def _prep_weights(w_in, layer, w_mla_uq, w_mla_ukv):
    w_in_al = _winprep_call(w_in, layer)

    r = w_mla_uq.shape[0]
    wq = w_mla_uq.reshape(r, MLA_HEADS, MLA_NOPE + MLA_ROPE)
    zq = jnp.zeros((r, MLA_HEADS, LANES - MLA_NOPE - MLA_ROPE), w_mla_uq.dtype)
    wq_pad = jnp.concatenate([wq, zq], axis=-1).reshape(r, MLA_HEADS * LANES).astype(BF16)

    rk = w_mla_ukv.shape[0]
    wkv = w_mla_ukv.reshape(rk, MLA_HEADS, MLA_NOPE + MLA_V)
    wk_aug = jnp.concatenate([wkv[..., :MLA_NOPE], jnp.zeros((rk, MLA_HEADS, LANES - MLA_NOPE), w_mla_ukv.dtype)],
                             axis=-1).reshape(rk, MLA_HEADS * LANES).astype(BF16)
    wv_t = wkv[..., MLA_NOPE:].reshape(rk, MLA_HEADS * MLA_V).T.astype(BF16)
    return w_in_al, wq_pad, wk_aug, wv_t


def kernel(x, mem, positions, g_mix, w_in, g_mla_q, w_mla_uq, g_mla_kv, w_mla_ukv, w_mla_o, swa_sinks, w_swa_o,
           g_mem, w_mem_kv, w_xa_o, b_gate, w_out, g_ffn, w_router, b_router, w_gate_up, b_gate_up, w_down,
           b_down, g_final):
    b, s, d = x.shape
    t = b * s
    depth = g_mix.shape[0]
    h = x.reshape(t, d)
    pos = positions.astype(F32).reshape(1, t)
    fq, f64 = _rope_freqs()
    for l in range(depth):
        w_in_al, wq_pad, wk_aug, wv_t = _prep_weights(w_in, l, w_mla_uq[l], w_mla_ukv[l])
        (qm, km, vmt, qs, ksa, ksb, vsa, vsb, qx, gates) = _proj_call(
            h, pos, fq, f64, g_mix[l][None], w_in_al, g_mla_q[l][None], wq_pad,
            g_mla_kv[l][None], wk_aug, wv_t, b_gate[l][None])
        r3 = lambda a: a.reshape(b, s, a.shape[1])
        omla = _mla_call(r3(qm), r3(km), vmt, s).reshape(t, -1)
        oswa = _swa_call(swa_sinks[l], r3(qs), r3(ksa), r3(ksb), r3(vsa), r3(vsb)).reshape(t, -1)
        m = mem.shape[1]
        kvm = _memkv_call(mem.reshape(b * m, d), g_mem[l][None], w_mem_kv[l].astype(BF16)).reshape(b, m, -1)
        wr_t = w_router[l].T
        wr_hi = wr_t.astype(BF16)
        wr_split = jnp.concatenate([wr_hi, (wr_t - wr_hi.astype(F32)).astype(BF16)], axis=0)
        h_mid, hnp, idx, gate, rank, counts = _merge_call(
            h, omla, oswa, qx, gates, kvm, w_mla_o[l].astype(BF16), w_swa_o[l].astype(BF16),
            w_xa_o[l].astype(BF16), w_out[l].astype(BF16), g_ffn[l][None], wr_split, b_router[l][:, None], s)

        counts = counts[:, 0]
        padded = ((counts + FFN_TM - 1) // FFN_TM) * FFN_TM
        padded_end = jnp.cumsum(padded)
        offsets = padded_end - padded
        experts = jnp.arange(N_EXPERTS, dtype=I32)
        dest = (jnp.sum(jnp.where(idx[..., None] == experts, offsets, 0), axis=-1) + rank).reshape(-1).astype(I32)
        n_tiles = (t * TOP_K) // FFN_TM + N_EXPERTS
        n_used = (padded_end[-1] // FFN_TM).astype(I32)
        tile_start = jnp.minimum(jnp.arange(n_tiles, dtype=I32), jnp.maximum(n_used - 1, 0)) * FFN_TM
        tile_expert = jnp.sum((padded_end[None, :] <= tile_start[:, None]).astype(I32), axis=1)
        tile_expert = jnp.minimum(tile_expert, N_EXPERTS - 1)
        onehot = tile_expert[:, None] == experts[None, :]
        pick = lambda v: jnp.sum(jnp.where(onehot, v[None, :], 0), axis=1)
        tile_valid = jnp.clip(pick(counts) - (tile_start - pick(offsets)), 0, FFN_TM)
        nonempty = padded > 0
        tile_group = pick(jnp.cumsum(nonempty.astype(I32)) - 1)
        later = jnp.where(nonempty[None, :] & (experts[None, :] > experts[:, None]), experts[None, :], N_EXPERTS)
        next_expert = jnp.min(later, axis=1)
        tile_next = pick(jnp.where(next_expert < N_EXPERTS, next_expert, -1))
        te = jnp.concatenate([tile_expert, n_used[None], tile_valid.astype(I32), tile_group.astype(I32),
                              tile_next.astype(I32)])

        xs = _sc_scatter_rows(hnp, dest.reshape(TOP_K, t), n_tiles * FFN_TM)
        y = _ffn_call(te, xs, w_gate_up[l], b_gate_up[l][:, None, :], w_down[l], b_down[l][:, None, :])
        if l == depth - 1:
            gfin = g_final[None]
            gate_tk = gate.T
            dest_parts = dest.reshape(TOP_K, COMBINE_PARTS, t // COMBINE_PARTS)
            out = None
            for part in range(COMBINE_PARTS):
                yk = _sc_gather_rows(y, dest_parts[:, part].reshape(-1))
                out = _combine_dense_call(h_mid, gate_tk, gfin, yk, part, COMBINE_PARTS, out)
        else:
            raise NotImplementedError("depth > 1 needs a combine without the final norm")
        h = out
    return h.reshape(b, s, d)
```

```python
import functools
import math

import jax
import jax.numpy as jnp
from jax import lax
from jax.experimental import pallas as pl
from jax.experimental.pallas import tpu as pltpu
from jax.experimental.pallas import tpu_sc as plsc

F32 = jnp.float32
BF16 = jnp.bfloat16
U32 = jnp.uint32
I32 = jnp.int32

LANES = 128
SUBLANES = 8
ROPE_THETA = 10000.0
RMS_EPS = 1e-6
LOG2E = 1.4426950408889634

MLA_HEADS = 8
MLA_NOPE = 64
MLA_ROPE = 32
MLA_V = 64
MLA_Q_RANK = 256
MLA_KV_RANK = 128
SWA_HEADS = 8
SWA_KV_HEADS = 2
SWA_HEAD_DIM = 64
SWA_WINDOW = 128
XA_HEADS = 4
XA_HEAD_DIM = 128
N_EXPERTS = 32
TOP_K = 4
SWIGLU_ALPHA = 1.702
SWIGLU_LIMIT = 7.0
N_BRANCHES = 3

NEG_BIG = -1e30

PROJ_TM = 512
PROJ_SUB = 256
MLA_TQ = 1024
MLA_ONES_ROWS = 16
SWA_TS = 1024
MERGE_TM = 512
MERGE_SUB = 256
FFN_TM = 1024
FFN_SUB = 256
COMBINE_TM = 256
COMBINE_PARTS = 2

VMEM_LIMIT = 56 * 1024 * 1024


def _rms(x, g):
    return x * lax.rsqrt(jnp.mean(x * x, axis=-1, keepdims=True) + RMS_EPS) * g


def _store_slabs(ref, row0, value):
    rows, n, _ = ref.shape
    flat = ref.reshape(rows * n, LANES)
    for c in range(n):
        flat[pl.ds(row0 * n + c, value.shape[0], stride=n), :] = value[:, c * LANES:(c + 1) * LANES]


def _load_slabs(ref, row0=0, m=None):
    rows, n, _ = ref.shape
    m = rows if m is None else m
    flat = ref.reshape(rows * n, LANES)
    return jnp.concatenate([flat[pl.ds(row0 * n + c, m, stride=n), :] for c in range(n)], axis=1)


_A0, _A1 = 0, 512
_B0, _B1 = 512, 1280
_C0, _C1 = 1280, 1792
_D0, _D1 = 1792, 4864


def _rotate_half(x, d, lo, hi):
    n = x.shape[1]
    half = (hi - lo) // 2
    lane = lax.broadcasted_iota(I32, x.shape, 1) % d
    up = pltpu.roll(x, n - half, axis=1)
    dn = pltpu.roll(x, half, axis=1)
    return jnp.where((lane >= lo) & (lane < lo + half), -up, jnp.where((lane >= lo + half) & (lane < hi), dn, 0.0))


def _proj_kernel(x_ref, pos_ref, fq_ref, f64_ref, gmix_ref, win_ref, gq_ref, wq_ref,
                 gkv_ref, wk_ref, wv_ref, bgate_ref,
                 qm_ref, km_ref, vm_ref, qs_ref, ksa_ref, ksb_ref, vsa_ref, vsb_ref, qx_ref, gt_ref):
    for hf in range(x_ref.shape[0] // PROJ_SUB):
        rows = slice(hf * PROJ_SUB, (hf + 1) * PROJ_SUB)
        _proj_rows(x_ref[rows, :], pos_ref[:, rows], fq_ref, f64_ref, gmix_ref, win_ref, gq_ref, wq_ref, gkv_ref,
                   wk_ref, wv_ref, bgate_ref,
                   [r.at[rows] for r in (qm_ref, km_ref)], vm_ref.at[:, rows],
                   [r.at[rows] for r in (qs_ref, ksa_ref, ksb_ref, vsa_ref, vsb_ref, qx_ref, gt_ref)])


def _proj_rows(x, pos, fq_ref, f64_ref, gmix_ref, win_ref, gq_ref, wq_ref, gkv_ref, wk_ref, wv_ref, bgate_ref,
               mla_refs, vm_ref, other_refs):
    qm_ref, km_ref = mla_refs
    qs_ref, ksa_ref, ksb_ref, vsa_ref, vsb_ref, qx_ref, gt_ref = other_refs
    xn = _rms(x, gmix_ref[...]).astype(BF16)
    tm = x.shape[0]
    ang16 = fq_ref[...] * pos
    ang32 = f64_ref[...] * pos
    c16, s16, c32, s32 = jnp.cos(ang16), jnp.sin(ang16), jnp.cos(ang32), jnp.sin(ang32)
    one = jnp.ones((MLA_NOPE, tm), F32)
    zero = jnp.zeros((MLA_NOPE, tm), F32)
    pad = LANES - MLA_NOPE - MLA_ROPE
    cosq = jnp.concatenate([one, c16, c16, one[:pad]], axis=0).T
    sinq = jnp.concatenate([zero, s16, s16, zero[:pad]], axis=0).T
    cos64 = jnp.concatenate([c32, c32, c32, c32], axis=0).T
    sin64 = jnp.concatenate([s32, s32, s32, s32], axis=0).T
    rope_lo, rope_hi = MLA_NOPE, MLA_NOPE + MLA_ROPE

    xa = jnp.dot(xn, win_ref[:, _A0:_A1], preferred_element_type=F32)
    cqn = _rms(xa[:, 0:256], gq_ref[...]).astype(BF16)
    qa = jnp.dot(cqn, wq_ref[...], preferred_element_type=F32)
    qb = _rotate_half(qa, LANES, rope_lo, rope_hi)
    q_scale = LOG2E / math.sqrt(MLA_NOPE + MLA_ROPE)
    ckvn = _rms(xa[:, 256:384], gkv_ref[...]).astype(BF16)
    ka = jnp.dot(ckvn, wk_ref[...], preferred_element_type=F32)
    kr = xa[:, 384:512]
    krope = kr * cosq + _rotate_half(kr, LANES, rope_lo, rope_hi) * sinq
    for h in range(MLA_HEADS):
        sl = slice(h * LANES, (h + 1) * LANES)
        qm_ref[:, sl] = ((qa[:, sl] * cosq + qb[:, sl] * sinq) * q_scale).astype(BF16)
        km_ref[:, sl] = (ka[:, sl] + krope).astype(BF16)
    vm_ref[...] = lax.dot_general(wv_ref[...], ckvn, (((1,), (1,)), ((), ())),
                                  preferred_element_type=F32).astype(BF16)

    xb = jnp.dot(xn, win_ref[:, _B0:_B1], preferred_element_type=F32)
    s_scale = LOG2E / math.sqrt(SWA_HEAD_DIM)
    nq = SWA_HEADS * SWA_HEAD_DIM
    qs = xb[:, :nq]
    qs_rot = _rotate_half(qs, SWA_HEAD_DIM, 0, SWA_HEAD_DIM)
    for p in range(SWA_HEADS // 2):
        sl = slice(p * LANES, (p + 1) * LANES)
        qs_ref[:, sl] = ((qs[:, sl] * cos64 + qs_rot[:, sl] * sin64) * s_scale).astype(BF16)
    ks = xb[:, nq:nq + LANES]
    ks = ks * cos64 + _rotate_half(ks, SWA_HEAD_DIM, 0, SWA_HEAD_DIM) * sin64
    ksa_ref[...] = ks.astype(BF16)
    ksb_ref[...] = pltpu.roll(ks, 64, axis=1).astype(BF16)
    vs = xb[:, nq + LANES:nq + 2 * LANES]
    vsa_ref[...] = vs.astype(BF16)
    vsb_ref[...] = pltpu.roll(vs, 64, axis=1).astype(BF16)

    xc = jnp.dot(xn, win_ref[:, _C0:_C1], preferred_element_type=F32)
    qx_ref[...] = (xc * (LOG2E / math.sqrt(XA_HEAD_DIM))).astype(BF16)

    xd = jnp.dot(xn, win_ref[:, _D0:_D1], preferred_element_type=F32) + bgate_ref[...]
    gt_ref[...] = jax.nn.sigmoid(xd).astype(BF16)


def _proj_call(x2, pos, fq, f64, g_mix, w_in_al, g_q, wq, g_kv, wk_aug, wv, b_gate):
    t, d = x2.shape
    tm = PROJ_TM
    row = lambda n: pl.BlockSpec((tm, n), lambda i: (i, 0))
    full = lambda a: pl.BlockSpec(a.shape, lambda i: (0,) * a.ndim)
    out_cols = [1024, 1024, 512, 128, 128, 128, 128, 512, 3072]
    out_specs = [row(n) for n in out_cols]
    out_shape = [jax.ShapeDtypeStruct((t, n), BF16) for n in out_cols]
    vt_rows = wv.shape[0]
    out_specs.insert(2, pl.BlockSpec((vt_rows, tm), lambda i: (0, i)))
    out_shape.insert(2, jax.ShapeDtypeStruct((vt_rows, t), BF16))
    return pl.pallas_call(
        _proj_kernel,
        grid=(t // tm,),
        in_specs=[row(d), pl.BlockSpec((1, tm), lambda i: (0, i)), full(fq), full(f64), full(g_mix),
                  pl.BlockSpec(w_in_al.shape, lambda i: (0, 0), pipeline_mode=pl.Buffered(1)),
                  full(g_q), full(wq), full(g_kv), full(wk_aug), full(wv), full(b_gate)],
        out_specs=out_specs,
        out_shape=out_shape,
        compiler_params=pltpu.CompilerParams(dimension_semantics=("arbitrary",), vmem_limit_bytes=VMEM_LIMIT),
        name="proj",
    )(x2, pos, fq, f64, g_mix, w_in_al, g_q, wq, g_kv, wk_aug, wv, b_gate)


def _mla_kernel(q_ref, k_ref, vt_ref, o_ref, sa_ref, sb_ref, ma_ref, mb_ref, m_ref, acc_ref, *, tq):
    i = pl.program_id(2)
    m_ref[...] = jnp.full(m_ref.shape, NEG_BIG, F32)
    acc_ref[...] = jnp.zeros(acc_ref.shape, F32)
    ones = jnp.ones((MLA_ONES_ROWS, tq), BF16)

    def scores(j, bufs):
        s_ref, mx_ref = bufs
        k0 = pl.multiple_of(j * tq, tq)
        for hh in range(2):
            sl = slice(hh * LANES, (hh + 1) * LANES)
            st = lax.dot_general(k_ref[0, pl.ds(k0, tq), sl], q_ref[0, :, sl], (((1,), (1,)), ((), ())),
                                 preferred_element_type=F32)
            s_ref[hh] = st
            mx_ref[hh] = jnp.max(st, axis=0, keepdims=True)

    def update(j, bufs, masked):
        s_ref, mx_ref = bufs
        k0 = pl.multiple_of(j * tq, tq)
        hq = tq // 2
        parts = ((slice(0, hq), hq, 0), (slice(hq, tq), tq, hq)) if masked else ((slice(0, tq), tq, 0),)
        for hh in range(2):
            for cols, nk, q0 in parts:
                nq = cols.stop - cols.start
                vt = jnp.concatenate([vt_ref[hh * MLA_V:(hh + 1) * MLA_V, pl.ds(k0, nk)], ones[:, :nk]], axis=0)
                st = s_ref[hh, 0:nk, cols]
                if masked:
                    kj = lax.broadcasted_iota(I32, (nk, nq), 0)
                    qi = lax.broadcasted_iota(I32, (nk, nq), 1) + q0
                    st = jnp.where(kj <= qi, st, NEG_BIG)
                    m_tile = jnp.max(st, axis=0, keepdims=True)
                else:
                    m_tile = mx_ref[hh]
                m_old = m_ref[hh, :, cols]
                m_new = jnp.maximum(m_old, m_tile)
                alpha = jnp.exp2(m_old - m_new)
                pt = jnp.exp2(st - m_new)
                acc_ref[hh, :, cols] = (alpha * acc_ref[hh, :, cols]
                                        + jnp.dot(vt, pt.astype(BF16), preferred_element_type=F32))
                m_ref[hh, :, cols] = m_new

    buf_a = (sa_ref, ma_ref)
    buf_b = (sb_ref, mb_ref)
    scores(0, buf_a)

    def body(jj, carry):
        scores(2 * jj + 1, buf_b)
        update(2 * jj, buf_a, False)
        scores(2 * jj + 2, buf_a)
        update(2 * jj + 1, buf_b, False)
        return carry

    lax.fori_loop(0, i // 2, body, 0)

    @pl.when(i % 2 == 0)
    def _():
        update(i, buf_a, True)

    @pl.when(i % 2 == 1)
    def _():
        scores(i, buf_b)
        update(i - 1, buf_a, False)
        update(i, buf_b, True)

    ot = jnp.concatenate([acc_ref[hh, :MLA_V] / acc_ref[hh, MLA_V:MLA_V + 1] for hh in range(2)],
                         axis=0)
    o_ref[0] = ot.T.astype(BF16)


def _mla_call(q, k, vt, seq):
    b, s, _ = q.shape
    assert s == seq
    tq = min(MLA_TQ, s)
    n_pairs = MLA_HEADS // 2
    return pl.pallas_call(
        functools.partial(_mla_kernel, tq=tq),
        grid=(b, n_pairs, s // tq),
        in_specs=[pl.BlockSpec((1, tq, 2 * LANES), lambda bi, hp, i: (bi, i, hp)),
                  pl.BlockSpec((1, s, 2 * LANES), lambda bi, hp, i: (bi, 0, hp)),
                  pl.BlockSpec((2 * MLA_V, s), lambda bi, hp, i: (hp, bi))],
        out_specs=pl.BlockSpec((1, tq, LANES), lambda bi, hp, i: (bi, i, hp)),
        out_shape=jax.ShapeDtypeStruct((b, s, n_pairs * LANES), BF16),
        scratch_shapes=[pltpu.VMEM((2, tq, tq), F32), pltpu.VMEM((2, tq, tq), F32),
                        pltpu.VMEM((2, 1, tq), F32), pltpu.VMEM((2, 1, tq), F32),
                        pltpu.VMEM((2, 1, tq), F32),
                        pltpu.VMEM((2, MLA_V + MLA_ONES_ROWS, tq), F32)],
        compiler_params=pltpu.CompilerParams(dimension_semantics=("arbitrary",) * 3, vmem_limit_bytes=VMEM_LIMIT),
        name="mla_attn",
    )(q, k, vt)


def _swa_kernel(sink_ref, q_ref, ka_ref, kb_ref, va_ref, vb_ref, kah_ref, kbh_ref, vah_ref, vbh_ref, o_ref, *, ts):
    w = SWA_WINDOW
    i = pl.program_id(1)
    ka = jnp.concatenate([kah_ref[0], ka_ref[0]], axis=0)
    kb = jnp.concatenate([kbh_ref[0], kb_ref[0]], axis=0)
    va = jnp.concatenate([vah_ref[0], va_ref[0]], axis=0)
    vb = jnp.concatenate([vbh_ref[0], vb_ref[0]], axis=0)
    lane_k = lax.broadcasted_iota(I32, (2 * w, LANES), 1)
    low = lane_k < SWA_HEAD_DIM
    qi = lax.broadcasted_iota(I32, (2 * w, 2 * w), 0) % w
    kj = lax.broadcasted_iota(I32, (2 * w, 2 * w), 1)
    diff = qi + w - kj
    band = (diff >= 0) & (diff < w)
    lane_o = lax.broadcasted_iota(I32, (w, LANES), 1)
    row2 = lax.broadcasted_iota(I32, (2 * w, 1), 0)
    zero = jnp.zeros((), BF16)
    stacks = ((0, ka, True, va), (1, kb, False, vb), (4, kb, True, vb), (5, ka, False, va))
    for n in range(ts // w):
        mask = band & ((i * (ts // w) + n > 0) | (kj >= w))
        res = []
        for h0, ksrc, keep_low, vsrc in stacks:
            p0 = h0 // 2
            q = jnp.concatenate([q_ref[0, n * w:(n + 1) * w, p0 * LANES:(p0 + 1) * LANES],
                                 q_ref[0, n * w:(n + 1) * w, (p0 + 1) * LANES:(p0 + 2) * LANES]], axis=0)
            kwin = ksrc[n * w:n * w + 2 * w]
            kwin = jnp.where(low if keep_low else ~low, kwin, zero)
            vwin = vsrc[n * w:n * w + 2 * w]
            s = lax.dot_general(q, kwin, (((1,), (1,)), ((), ())), preferred_element_type=F32)
            s = jnp.where(mask, s, NEG_BIG)
            sink = jnp.where(row2 < w, sink_ref[h0], sink_ref[h0 + 2]) * LOG2E
            m = jnp.maximum(jnp.max(s, axis=1, keepdims=True), sink)
            p = jnp.exp2(s - m)
            den = jnp.sum(p, axis=1, keepdims=True) + jnp.exp2(sink - m)
            o = jnp.dot(p.astype(BF16), vwin, preferred_element_type=F32) / den
            res.append(o)
        o02, o13, o46, o57 = res
        sel = lane_o < SWA_HEAD_DIM
        rows = slice(n * w, (n + 1) * w)
        o_ref[0, rows, 0 * LANES:1 * LANES] = jnp.where(sel, o02[:w], o13[:w]).astype(BF16)
        o_ref[0, rows, 1 * LANES:2 * LANES] = jnp.where(sel, o02[w:], o13[w:]).astype(BF16)
        o_ref[0, rows, 2 * LANES:3 * LANES] = jnp.where(sel, o46[:w], o57[:w]).astype(BF16)
        o_ref[0, rows, 3 * LANES:4 * LANES] = jnp.where(sel, o46[w:], o57[w:]).astype(BF16)


def _swa_call(sinks, q, ksa, ksb, vsa, vsb):
    b, s, _ = q.shape
    ts = min(SWA_TS, s)
    w = SWA_WINDOW
    r = ts // w
    main = pl.BlockSpec((1, ts, LANES), lambda bi, i: (bi, i, 0))
    halo = pl.BlockSpec((1, w, LANES), lambda bi, i: (bi, jnp.maximum(i * r - 1, 0), 0))
    return pl.pallas_call(
        functools.partial(_swa_kernel, ts=ts),
        grid=(b, s // ts),
        in_specs=[pl.BlockSpec(memory_space=pltpu.SMEM),
                  pl.BlockSpec((1, ts, 4 * LANES), lambda bi, i: (bi, i, 0)),
                  main, main, main, main, halo, halo, halo, halo],
        out_specs=pl.BlockSpec((1, ts, 4 * LANES), lambda bi, i: (bi, i, 0)),
        out_shape=jax.ShapeDtypeStruct((b, s, 4 * LANES), BF16),
        compiler_params=pltpu.CompilerParams(dimension_semantics=("arbitrary",) * 2, vmem_limit_bytes=VMEM_LIMIT),
        name="swa_attn",
    )(sinks, q, ksa, ksb, vsa, vsb, ksa, ksb, vsa, vsb)


def _memkv_kernel(mem_ref, g_ref, w_ref, o_ref):
    mn = _rms(mem_ref[...], g_ref[...]).astype(BF16)
    o_ref[...] = jnp.dot(mn, w_ref[...], preferred_element_type=F32).astype(BF16)


def _memkv_call(mem2, g_mem, w_mem_kv):
    n, d = mem2.shape
    tm = min(256, n)
    return pl.pallas_call(
        _memkv_kernel,
        grid=(n // tm,),
        in_specs=[pl.BlockSpec((tm, d), lambda i: (i, 0)),
                  pl.BlockSpec(g_mem.shape, lambda i: (0, 0)),
                  pl.BlockSpec(w_mem_kv.shape, lambda i: (0, 0))],
        out_specs=pl.BlockSpec((tm, w_mem_kv.shape[1]), lambda i: (i, 0)),
        out_shape=jax.ShapeDtypeStruct((n, w_mem_kv.shape[1]), BF16),
        compiler_params=pltpu.CompilerParams(dimension_semantics=("arbitrary",), vmem_limit_bytes=VMEM_LIMIT),
        name="mem_kv",
    )(mem2, g_mem, w_mem_kv)


def _merge_kernel(x_ref, omla_ref, oswa_ref, qx_ref, gt_ref, kvm_ref, wmo_ref, wso_ref, wxo_ref, wout_ref,
                  gffn_ref, wr_ref, br_ref,
                  h_ref, hnp_ref, idx_ref, gate_ref, rank_ref, cnt_ref, run_ref, *, tm, sub):
    @pl.when(pl.program_id(0) == 0)
    def _():
        run_ref[...] = jnp.zeros(run_ref.shape, F32)

    d = x_ref.shape[1]
    kv_cols = XA_HEADS * XA_HEAD_DIM
    erow = lax.broadcasted_iota(I32, (N_EXPERTS, sub), 0)
    tri_t = (lax.broadcasted_iota(I32, (sub, sub), 0) < lax.broadcasted_iota(I32, (sub, sub), 1)).astype(BF16)
    nt = (((1,), (1,)), ((), ()))
    run = run_ref[...]
    for hf in range(tm // sub):
        rows = slice(hf * sub, (hf + 1) * sub)

        oxs = []
        for hd in range(XA_HEADS):
            sl = slice(hd * LANES, (hd + 1) * LANES)
            km = kvm_ref[0, :, sl]
            vm = kvm_ref[0, :, kv_cols + hd * LANES:kv_cols + (hd + 1) * LANES]
            s = lax.dot_general(qx_ref[rows, sl], km, nt, preferred_element_type=F32)
            p = jnp.exp2(s - jnp.max(s, axis=1, keepdims=True))
            den = jnp.sum(p, axis=1, keepdims=True)
            oxs.append((jnp.dot(p.astype(BF16), vm, preferred_element_type=F32) / den).astype(BF16))
        oxa = jnp.concatenate(oxs, axis=1)

        merged = (gt_ref[rows, 0:d].astype(F32) * jnp.dot(omla_ref[rows, :], wmo_ref[...], preferred_element_type=F32)
                  + gt_ref[rows, d:2 * d].astype(F32) * jnp.dot(oswa_ref[rows, :], wso_ref[...],
                                                                 preferred_element_type=F32)
                  + gt_ref[rows, 2 * d:3 * d].astype(F32) * jnp.dot(oxa, wxo_ref[...], preferred_element_type=F32))
        h = x_ref[rows, :] + jnp.dot(merged.astype(BF16), wout_ref[...], preferred_element_type=F32)
        h_ref[rows, :] = h

        hn = _rms(h, gffn_ref[...])
        hn_hi = hn.astype(BF16)
        hn_hi32 = hn_hi.astype(F32)
        hn_lo = (hn - hn_hi32).astype(BF16)
        bits = pltpu.bitcast(hn_hi32, U32)
        _store_slabs(hnp_ref, hf * sub, (bits[:, : d // 2] >> 16) | (bits[:, d // 2:] & jnp.uint32(0xFFFF0000)))

        part = lax.dot_general(wr_ref[...], hn_hi, nt, preferred_element_type=F32)
        logits_t = (part[:N_EXPERTS] + part[N_EXPERTS:]
                    + lax.dot_general(wr_ref[0:N_EXPERTS, :], hn_lo, nt, preferred_element_type=F32) + br_ref[...])

        work = logits_t
        vals, idxs, hots = [], [], []
        for _ in range(TOP_K):
            mx = jnp.max(work, axis=0, keepdims=True)
            ix = jnp.min(jnp.where(work == mx, erow, N_EXPERTS), axis=0, keepdims=True)
            hot = erow == ix
            work = jnp.where(hot, -jnp.inf, work)
            vals.append(mx)
            idxs.append(ix)
            hots.append(hot)
        es = [jnp.exp(v - vals[0]) for v in vals]
        den = es[0] + es[1] + es[2] + es[3]
        sel_t = (hots[0] | hots[1] | hots[2] | hots[3])
        prefix_t = jnp.dot(sel_t.astype(BF16), tri_t, preferred_element_type=F32) + run
        for k in range(TOP_K):
            idx_ref[k:k + 1, rows] = idxs[k]
            gate_ref[k:k + 1, rows] = es[k] / den
            rank_ref[k:k + 1, rows] = jnp.sum(jnp.where(hots[k], prefix_t, 0.0), axis=0, keepdims=True).astype(I32)
        run = run + jnp.sum(sel_t.astype(F32), axis=1, keepdims=True)
    run_ref[...] = run
    cnt_ref[...] = run.astype(I32)


def _merge_call(x2, omla, oswa, qx, gates, kvm, wmo, wso, wxo, wout, g_ffn, wr_split, b_router_col, seq):
    t, d = x2.shape
    tm = MERGE_TM
    per_b = seq // tm
    row = lambda n: pl.BlockSpec((tm, n), lambda i: (i, 0))
    col = lambda: pl.BlockSpec((TOP_K, tm), lambda i: (0, i))
    full = lambda a: pl.BlockSpec(a.shape, lambda i: (0,) * a.ndim)
    return pl.pallas_call(
        functools.partial(_merge_kernel, tm=tm, sub=MERGE_SUB),
        grid=(t // tm,),
        in_specs=[row(d), row(512), row(512), row(512), row(3 * d),
                  pl.BlockSpec((1,) + kvm.shape[1:], lambda i: (i // per_b, 0, 0)),
                  full(wmo), full(wso), full(wxo), full(wout), full(g_ffn), full(wr_split), full(b_router_col)],
        out_specs=[row(d), pl.BlockSpec((tm, d // 2 // LANES, LANES), lambda i: (i, 0, 0)), col(), col(), col(),
                   pl.BlockSpec((N_EXPERTS, 1), lambda i: (0, 0))],
        out_shape=[jax.ShapeDtypeStruct((t, d), F32), jax.ShapeDtypeStruct((t, d // 2 // LANES, LANES), U32),
                   jax.ShapeDtypeStruct((TOP_K, t), I32), jax.ShapeDtypeStruct((TOP_K, t), F32),
                   jax.ShapeDtypeStruct((TOP_K, t), I32), jax.ShapeDtypeStruct((N_EXPERTS, 1), I32)],
        scratch_shapes=[pltpu.VMEM((N_EXPERTS, 1), F32)],
        compiler_params=pltpu.CompilerParams(dimension_semantics=("arbitrary",), vmem_limit_bytes=VMEM_LIMIT),
        name="merge_router",
    )(x2, omla, oswa, qx, gates, kvm, wmo, wso, wxo, wout, g_ffn, wr_split, b_router_col)


SC_CORES = 2
SC_SUBCORES = 16
SC_WORKERS = SC_CORES * SC_SUBCORES
SC_CHUNK = 64


def _sc_mesh():
    return plsc.VectorSubcoreMesh(core_axis_name="c", subcore_axis_name="s",
                                  num_cores=SC_CORES, num_subcores=SC_SUBCORES)


def _sc_index_blocks(idx):
    n = idx.shape[0]
    per_w = n // SC_WORKERS
    n_ch = per_w // SC_CHUNK
    assert per_w * SC_WORKERS == n and n_ch * SC_CHUNK == per_w
    return idx.reshape(SC_WORKERS, n_ch, SC_CHUNK), per_w, n_ch


def _sc_two_buffer_loop(n_ch, load, stores, bufs, load_sems, store_sems):
    assert n_ch % 2 == 0
    a, b = bufs
    la, lb = load_sems
    sa, sb = store_sems

    def start(cps):
        for cp in cps:
            cp.start()

    def wait(cps):
        for cp in cps:
            cp.wait()

    load(0, a, la).start()

    @pl.loop(0, n_ch, step=2)
    def _(j):
        load(j, a, la).wait()

        @pl.when(j > 0)
        def _():
            wait(stores(j - 1, b, sb))

        load(j + 1, b, lb).start()
        start(stores(j, a, sa))
        load(j + 1, b, lb).wait()
        wait(stores(j, a, sa))

        @pl.when(j + 2 < n_ch)
        def _():
            load(j + 2, a, la).start()

        start(stores(j + 1, b, sb))

    wait(stores(n_ch - 1, b, sb))


def _sc_scratch(n_ch, row_shape, dtype):
    return [pltpu.VMEM((n_ch, SC_CHUNK), I32), pltpu.VMEM((SC_CHUNK,) + row_shape, dtype),
            pltpu.VMEM((SC_CHUNK,) + row_shape, dtype)] + [pltpu.SemaphoreType.DMA] * 4


def _sc_scatter_rows(src, idx, n_out):
    fan, n_src = idx.shape
    per_w = n_src // SC_WORKERS
    n_ch = per_w // SC_CHUNK
    assert per_w * SC_WORKERS == n_src and n_ch * SC_CHUNK == per_w
    idx3 = idx.reshape(fan, SC_WORKERS, n_ch, SC_CHUNK).transpose(1, 0, 2, 3).reshape(SC_WORKERS, fan * n_ch, SC_CHUNK)

    @functools.partial(
        pl.kernel, mesh=_sc_mesh(),
        out_type=jax.ShapeDtypeStruct((n_out,) + src.shape[1:], src.dtype),
        scratch_types=_sc_scratch(fan * n_ch, src.shape[1:], src.dtype),
        name="moe_dispatch_sc")
    def k(src_hbm, idx_hbm, out_hbm, idx_v, rows_a, rows_b, la, lb, sa, sb):
        wid = lax.axis_index("s") * SC_CORES + lax.axis_index("c")
        base = wid * per_w
        pltpu.sync_copy(idx_hbm.at[wid], idx_v)

        def load(c, buf, sem):
            return pltpu.make_async_copy(src_hbm.at[pl.ds(base + c * SC_CHUNK, SC_CHUNK)], buf, sem)

        def stores(c, buf, sem):
            return tuple(pltpu.make_async_copy(buf, out_hbm.at[idx_v.at[f * n_ch + c]], sem) for f in range(fan))

        _sc_two_buffer_loop(n_ch, load, stores, (rows_a, rows_b), (la, lb), (sa, sb))

    return k(src, idx3)


def _sc_gather_rows(table, idx):
    idx3, per_w, n_ch = _sc_index_blocks(idx)

    @functools.partial(
        pl.kernel, mesh=_sc_mesh(),
        out_type=jax.ShapeDtypeStruct((idx.shape[0],) + table.shape[1:], table.dtype),
        scratch_types=_sc_scratch(n_ch, table.shape[1:], table.dtype),
        name="moe_gather_sc")
    def k(table_hbm, idx_hbm, out_hbm, idx_v, rows_a, rows_b, la, lb, sa, sb):
        wid = lax.axis_index("s") * SC_CORES + lax.axis_index("c")
        base = wid * per_w
        pltpu.sync_copy(idx_hbm.at[wid], idx_v)

        def load(c, buf, sem):
            return pltpu.make_async_copy(table_hbm.at[idx_v.at[c]], buf, sem)

        def stores(c, buf, sem):
            return (pltpu.make_async_copy(buf, out_hbm.at[pl.ds(base + c * SC_CHUNK, SC_CHUNK)], sem),)

        _sc_two_buffer_loop(n_ch, load, stores, (rows_a, rows_b), (la, lb), (sa, sb))

    return k(table, idx3)


def _unpack_lo(w):
    return pltpu.bitcast(w << 16, F32)


def _unpack_hi(w):
    return pltpu.bitcast(w & jnp.uint32(0xFFFF0000), F32)


def _ffn_kernel(te_ref, xs_ref, wgu_hbm, bgu_ref, wd_hbm, bd_ref, y_ref, wgu_f32, wd_f32, wgu_bf, wd_bf, wsem):
    i = pl.program_id(0)
    n = pl.num_programs(0)
    n_used = te_ref[n]
    e = te_ref[i]
    first = (i == 0) | (e != te_ref[jnp.maximum(i - 1, 0)])
    slot = te_ref[2 * n + 1 + i] % 2
    nxt = te_ref[3 * n + 1 + i]

    def fetch(expert, s):
        return (pltpu.make_async_copy(wgu_hbm.at[expert], wgu_f32.at[s], wsem.at[0, s]),
                pltpu.make_async_copy(wd_hbm.at[expert], wd_f32.at[s], wsem.at[1, s]))

    @pl.when(i == 0)
    def _():
        for cp in fetch(e, slot):
            cp.start()

    @pl.when((i < n_used) & first)
    def _():
        for cp in fetch(e, slot):
            cp.wait()
        wgu_bf[...] = wgu_f32[slot].astype(BF16)
        wd_bf[...] = wd_f32[slot].astype(BF16)

        @pl.when(nxt >= 0)
        def _():
            for cp in fetch(nxt, 1 - slot):
                cp.start()

    valid = jnp.where(i < n_used, te_ref[n + 1 + i], 0)
    rows, n_slab, _ = xs_ref.shape
    half = n_slab * LANES
    for sb in range(rows // FFN_SUB):
        r0 = sb * FFN_SUB

        @pl.when(valid > r0)
        def _():
            w = _load_slabs(xs_ref, r0, FFN_SUB)
            w = jnp.where(lax.broadcasted_iota(I32, w.shape, 0) < valid - r0, w, jnp.uint32(0))
            x_lo = _unpack_lo(w).astype(BF16)
            x_hi = _unpack_hi(w).astype(BF16)
            gu = (jnp.dot(x_lo, wgu_bf[0:half, :], preferred_element_type=F32)
                  + jnp.dot(x_hi, wgu_bf[half:, :], preferred_element_type=F32) + bgu_ref[0])
            de = gu.shape[1] // 2
            x_glu = jnp.minimum(gu[:, :de], SWIGLU_LIMIT)
            x_lin = jnp.clip(gu[:, de:], -SWIGLU_LIMIT, SWIGLU_LIMIT)
            hdn = x_glu * jax.nn.sigmoid(SWIGLU_ALPHA * x_glu) * (x_lin + 1.0)
            y = jnp.dot(hdn.astype(BF16), wd_bf[...], preferred_element_type=F32) + bd_ref[0]
            bits = pltpu.bitcast(y.astype(BF16).astype(F32), U32)
            _store_slabs(y_ref, r0, (bits[:, :half] >> 16) | (bits[:, half:] & jnp.uint32(0xFFFF0000)))

        @pl.when(valid <= r0)
        def _():
            y_ref[r0:r0 + FFN_SUB] = jnp.zeros((FFN_SUB,) + y_ref.shape[1:], U32)


def _ffn_call(tile_table, xs, w_gate_up, b_gate_up, w_down, b_down):
    r, ns, lanes = xs.shape
    tm = FFN_TM
    ne, d, de2 = w_gate_up.shape
    return pl.pallas_call(
        _ffn_kernel,
        grid_spec=pltpu.PrefetchScalarGridSpec(
            num_scalar_prefetch=1,
            grid=(r // tm,),
            in_specs=[pl.BlockSpec((tm, ns, lanes),
                                   lambda i, te: (jnp.minimum(i, jnp.maximum(te[r // tm] - 1, 0)), 0, 0)),
                      pl.BlockSpec(memory_space=pl.ANY),
                      pl.BlockSpec((1, 1, de2), lambda i, te: (te[i], 0, 0)),
                      pl.BlockSpec(memory_space=pl.ANY),
                      pl.BlockSpec((1, 1, d), lambda i, te: (te[i], 0, 0))],
            out_specs=pl.BlockSpec((tm, ns, lanes), lambda i, te: (i, 0, 0)),
            scratch_shapes=[pltpu.VMEM((2, d, de2), F32), pltpu.VMEM((2, de2 // 2, d), F32),
                            pltpu.VMEM((d, de2), BF16), pltpu.VMEM((de2 // 2, d), BF16),
                            pltpu.SemaphoreType.DMA((2, 2))]),
        out_shape=jax.ShapeDtypeStruct((r, ns, lanes), U32),
        compiler_params=pltpu.CompilerParams(dimension_semantics=("arbitrary",), vmem_limit_bytes=VMEM_LIMIT),
        name="moe_ffn",
    )(tile_table, xs, w_gate_up, b_gate_up, w_down, b_down)


def _combine_dense_kernel(h_ref, gate_ref, gfin_ref, y0_ref, y1_ref, y2_ref, y3_ref, o_ref):
    half = y0_ref.shape[1] * y0_ref.shape[2]
    lo = h_ref[:, :half]
    hi = h_ref[:, half:]
    gates_kt = gate_ref[...]
    gates_tk = jnp.concatenate([gates_kt, jnp.zeros((SUBLANES - TOP_K, gates_kt.shape[1]), F32)], axis=0).T
    for k, y_ref in enumerate((y0_ref, y1_ref, y2_ref, y3_ref)):
        g = gates_tk[:, k:k + 1]
        w = _load_slabs(y_ref)
        lo = lo + g * _unpack_lo(w)
        hi = hi + g * _unpack_hi(w)
    ms = (jnp.sum(lo * lo, axis=1, keepdims=True) + jnp.sum(hi * hi, axis=1, keepdims=True)) / (2 * half)
    inv = lax.rsqrt(ms + RMS_EPS)
    o_ref[:, :half] = lo * inv * gfin_ref[:, :half]
    o_ref[:, half:] = hi * inv * gfin_ref[:, half:]


def _combine_dense_into_kernel(h_ref, gate_ref, gfin_ref, y0_ref, y1_ref, y2_ref, y3_ref, prev_ref, o_ref):
    del prev_ref
    _combine_dense_kernel(h_ref, gate_ref, gfin_ref, y0_ref, y1_ref, y2_ref, y3_ref, o_ref)


def _combine_dense_call(h, gate, g_final, yk, part, n_parts, prev_out):
    t, d = h.shape
    tm = COMBINE_TM
    steps = t // tm // n_parts
    first = part * steps
    yspec = lambda k: pl.BlockSpec((tm,) + yk.shape[1:], lambda i: (k * steps + i, 0, 0))
    in_specs = [pl.BlockSpec((tm, d), lambda i: (first + i, 0)),
                pl.BlockSpec((TOP_K, tm), lambda i: (0, first + i)),
                pl.BlockSpec((1, d), lambda i: (0, 0)),
                yspec(0), yspec(1), yspec(2), yspec(3)]
    args = [h, gate, g_final, yk, yk, yk, yk]
    aliases = {}
    body = _combine_dense_kernel
    if prev_out is not None:
        in_specs.append(pl.BlockSpec(memory_space=pl.ANY))
        args.append(prev_out)
        aliases = {len(args) - 1: 0}
        body = _combine_dense_into_kernel
    return pl.pallas_call(
        body,
        grid=(steps,),
        in_specs=in_specs,
        out_specs=pl.BlockSpec((tm, d), lambda i: (first + i, 0)),
        out_shape=jax.ShapeDtypeStruct((t, d), F32),
        input_output_aliases=aliases,
        compiler_params=pltpu.CompilerParams(dimension_semantics=("arbitrary",), vmem_limit_bytes=VMEM_LIMIT),
        name="moe_combine",
    )(*args)


def _rope_freqs():
    def inv_freq(dh):
        return (ROPE_THETA ** (-jnp.arange(0, dh, 2, dtype=F32) / dh))[:, None]

    return inv_freq(MLA_ROPE), inv_freq(SWA_HEAD_DIM)


def _winprep_kernel(wt_ref, o_ref):
    j = pl.program_id(0)
    c1 = MLA_Q_RANK + MLA_KV_RANK
    kr_block = c1 // LANES

    @pl.when(j != kr_block)
    def _():
        row0 = jnp.where(j < kr_block, j * LANES, c1 + MLA_ROPE + (j - kr_block - 1) * LANES)
        o_ref[...] = wt_ref[pl.ds(pl.multiple_of(row0, SUBLANES), LANES), :].T.astype(BF16)

    @pl.when(j == kr_block)
    def _():
        d = wt_ref.shape[1]
        blk = jnp.concatenate([jnp.zeros((MLA_NOPE, d), F32), wt_ref[c1:c1 + MLA_ROPE, :],
                               jnp.zeros((LANES - MLA_NOPE - MLA_ROPE, d), F32)], axis=0)
        o_ref[...] = blk.T.astype(BF16)


def _winprep_call(w_in, layer):
    _, d, n = w_in.shape
    w_t = jnp.swapaxes(w_in, 1, 2)
    return pl.pallas_call(
        _winprep_kernel,
        grid=(_D1 // LANES,),
        in_specs=[pl.BlockSpec((None, n, d), lambda j: (layer, 0, 0), pipeline_mode=pl.Buffered(1))],
        out_specs=pl.BlockSpec((d, LANES), lambda j: (0, j)),
        out_shape=jax.ShapeDtypeStruct((d, _D1), BF16),
        compiler_params=pltpu.CompilerParams(dimension_semantics=("arbitrary",), vmem_limit_bytes=VMEM_LIMIT),
        name="w_in_prep",
    )(w_t)


def _prep_weights(w_in, layer, w_mla_uq, w_mla_ukv):
    w_in_al = _winprep_call(w_in, layer)

    r = w_mla_uq.shape[0]
    wq = w_mla_uq.reshape(r, MLA_HEADS, MLA_NOPE + MLA_ROPE)
    zq = jnp.zeros((r, MLA_HEADS, LANES - MLA_NOPE - MLA_ROPE), w_mla_uq.dtype)
    wq_pad = jnp.concatenate([wq, zq], axis=-1).reshape(r, MLA_HEADS * LANES).astype(BF16)

    rk = w_mla_ukv.shape[0]
    wkv = w_mla_ukv.reshape(rk, MLA_HEADS, MLA_NOPE + MLA_V)
    wk_aug = jnp.concatenate([wkv[..., :MLA_NOPE], jnp.zeros((rk, MLA_HEADS, LANES - MLA_NOPE), w_mla_ukv.dtype)],
                             axis=-1).reshape(rk, MLA_HEADS * LANES).astype(BF16)
    wv_t = wkv[..., MLA_NOPE:].reshape(rk, MLA_HEADS * MLA_V).T.astype(BF16)
    return w_in_al, wq_pad, wk_aug, wv_t


def kernel(x, mem, positions, g_mix, w_in, g_mla_q, w_mla_uq, g_mla_kv, w_mla_ukv, w_mla_o, swa_sinks, w_swa_o,
           g_mem, w_mem_kv, w_xa_o, b_gate, w_out, g_ffn, w_router, b_router, w_gate_up, b_gate_up, w_down,
           b_down, g_final):
    b, s, d = x.shape
    t = b * s
    depth = g_mix.shape[0]
    h = x.reshape(t, d)
    pos = positions.astype(F32).reshape(1, t)
    fq, f64 = _rope_freqs()
    for l in range(depth):
        w_in_al, wq_pad, wk_aug, wv_t = _prep_weights(w_in, l, w_mla_uq[l], w_mla_ukv[l])
        (qm, km, vmt, qs, ksa, ksb, vsa, vsb, qx, gates) = _proj_call(
            h, pos, fq, f64, g_mix[l][None], w_in_al, g_mla_q[l][None], wq_pad,
            g_mla_kv[l][None], wk_aug, wv_t, b_gate[l][None])
        r3 = lambda a: a.reshape(b, s, a.shape[1])
        omla = _mla_call(r3(qm), r3(km), vmt, s).reshape(t, -1)
        oswa = _swa_call(swa_sinks[l], r3(qs), r3(ksa), r3(ksb), r3(vsa), r3(vsb)).reshape(t, -1)
        m = mem.shape[1]
        kvm = _memkv_call(mem.reshape(b * m, d), g_mem[l][None], w_mem_kv[l].astype(BF16)).reshape(b, m, -1)
        wr_t = w_router[l].T
        wr_hi = wr_t.astype(BF16)
        wr_split = jnp.concatenate([wr_hi, (wr_t - wr_hi.astype(F32)).astype(BF16)], axis=0)
        h_mid, hnp, idx, gate, rank, counts = _merge_call(
            h, omla, oswa, qx, gates, kvm, w_mla_o[l].astype(BF16), w_swa_o[l].astype(BF16),
            w_xa_o[l].astype(BF16), w_out[l].astype(BF16), g_ffn[l][None], wr_split, b_router[l][:, None], s)

        counts = counts[:, 0]
        padded = ((counts + FFN_TM - 1) // FFN_TM) * FFN_TM
        padded_end = jnp.cumsum(padded)
        offsets = padded_end - padded
        experts = jnp.arange(N_EXPERTS, dtype=I32)
        dest = (jnp.sum(jnp.where(idx[..., None] == experts, offsets, 0), axis=-1) + rank).reshape(-1).astype(I32)
        n_tiles = (t * TOP_K) // FFN_TM + N_EXPERTS
        n_used = (padded_end[-1] // FFN_TM).astype(I32)
        tile_start = jnp.minimum(jnp.arange(n_tiles, dtype=I32), jnp.maximum(n_used - 1, 0)) * FFN_TM
        tile_expert = jnp.sum((padded_end[None, :] <= tile_start[:, None]).astype(I32), axis=1)
        tile_expert = jnp.minimum(tile_expert, N_EXPERTS - 1)
        onehot = tile_expert[:, None] == experts[None, :]
        pick = lambda v: jnp.sum(jnp.where(onehot, v[None, :], 0), axis=1)
        tile_valid = jnp.clip(pick(counts) - (tile_start - pick(offsets)), 0, FFN_TM)
        nonempty = padded > 0
        tile_group = pick(jnp.cumsum(nonempty.astype(I32)) - 1)
        later = jnp.where(nonempty[None, :] & (experts[None, :] > experts[:, None]), experts[None, :], N_EXPERTS)
        next_expert = jnp.min(later, axis=1)
        tile_next = pick(jnp.where(next_expert < N_EXPERTS, next_expert, -1))
        te = jnp.concatenate([tile_expert, n_used[None], tile_valid.astype(I32), tile_group.astype(I32),
                              tile_next.astype(I32)])

        xs = _sc_scatter_rows(hnp, dest.reshape(TOP_K, t), n_tiles * FFN_TM)
        y = _ffn_call(te, xs, w_gate_up[l], b_gate_up[l][:, None, :], w_down[l], b_down[l][:, None, :])
        if l == depth - 1:
            gfin = g_final[None]
            dest_parts = dest.reshape(TOP_K, COMBINE_PARTS, t // COMBINE_PARTS)
            out = None
            for part in range(COMBINE_PARTS):
                yk = _sc_gather_rows(y, dest_parts[:, part].reshape(-1))
                out = _combine_dense_call(h_mid, gate, gfin, yk, part, COMBINE_PARTS, out)
        else:
            raise NotImplementedError("depth > 1 needs a combine without the final norm")
        h = out
    return h.reshape(b, s, d)
```

```python
import functools
import math

import jax
import jax.numpy as jnp
from jax import lax
from jax.experimental import pallas as pl
from jax.experimental.pallas import tpu as pltpu
from jax.experimental.pallas import tpu_sc as plsc

F32 = jnp.float32
BF16 = jnp.bfloat16
U32 = jnp.uint32
I32 = jnp.int32

LANES = 128
SUBLANES = 8
ROPE_THETA = 10000.0
RMS_EPS = 1e-6
LOG2E = 1.4426950408889634

MLA_HEADS = 8
MLA_NOPE = 64
MLA_ROPE = 32
MLA_V = 64
MLA_Q_RANK = 256
MLA_KV_RANK = 128
SWA_HEADS = 8
SWA_KV_HEADS = 2
SWA_HEAD_DIM = 64
SWA_WINDOW = 128
XA_HEADS = 4
XA_HEAD_DIM = 128
N_EXPERTS = 32
TOP_K = 4
SWIGLU_ALPHA = 1.702
SWIGLU_LIMIT = 7.0
N_BRANCHES = 3

NEG_BIG = -1e30

PROJ_TM = 512
PROJ_SUB = 256
MLA_TQ = 1024
MLA_ONES_ROWS = 16
SWA_TS = 1024
MERGE_TM = 512
MERGE_SUB = 256
FFN_TM = 1024
FFN_SUB = 256
COMBINE_TM = 512
COMBINE_PARTS = 2

VMEM_LIMIT = 56 * 1024 * 1024


def _rms(x, g):
    return x * lax.rsqrt(jnp.mean(x * x, axis=-1, keepdims=True) + RMS_EPS) * g


def _store_slabs(ref, row0, value):
    rows, n, _ = ref.shape
    flat = ref.reshape(rows * n, LANES)
    for c in range(n):
        flat[pl.ds(row0 * n + c, value.shape[0], stride=n), :] = value[:, c * LANES:(c + 1) * LANES]


def _load_slabs(ref, row0=0, m=None):
    rows, n, _ = ref.shape
    m = rows if m is None else m
    flat = ref.reshape(rows * n, LANES)
    return jnp.concatenate([flat[pl.ds(row0 * n + c, m, stride=n), :] for c in range(n)], axis=1)


_A0, _A1 = 0, 512
_B0, _B1 = 512, 1280
_C0, _C1 = 1280, 1792
_D0, _D1 = 1792, 4864


def _rotate_half(x, d, lo, hi):
    n = x.shape[1]
    half = (hi - lo) // 2
    lane = lax.broadcasted_iota(I32, x.shape, 1) % d
    up = pltpu.roll(x, n - half, axis=1)
    dn = pltpu.roll(x, half, axis=1)
    return jnp.where((lane >= lo) & (lane < lo + half), -up, jnp.where((lane >= lo + half) & (lane < hi), dn, 0.0))


def _proj_kernel(x_ref, pos_ref, fq_ref, f64_ref, gmix_ref, win_ref, gq_ref, wq_ref,
                 gkv_ref, wk_ref, wv_ref, bgate_ref,
                 qm_ref, km_ref, vm_ref, qs_ref, ksa_ref, ksb_ref, vsa_ref, vsb_ref, qx_ref, gt_ref):
    for hf in range(x_ref.shape[0] // PROJ_SUB):
        rows = slice(hf * PROJ_SUB, (hf + 1) * PROJ_SUB)
        _proj_rows(x_ref[rows, :], pos_ref[:, rows], fq_ref, f64_ref, gmix_ref, win_ref, gq_ref, wq_ref, gkv_ref,
                   wk_ref, wv_ref, bgate_ref,
                   [r.at[rows] for r in (qm_ref, km_ref)], vm_ref.at[:, rows],
                   [r.at[rows] for r in (qs_ref, ksa_ref, ksb_ref, vsa_ref, vsb_ref, qx_ref, gt_ref)])


def _proj_rows(x, pos, fq_ref, f64_ref, gmix_ref, win_ref, gq_ref, wq_ref, gkv_ref, wk_ref, wv_ref, bgate_ref,
               mla_refs, vm_ref, other_refs):
    qm_ref, km_ref = mla_refs
    qs_ref, ksa_ref, ksb_ref, vsa_ref, vsb_ref, qx_ref, gt_ref = other_refs
    xn = _rms(x, gmix_ref[...]).astype(BF16)
    tm = x.shape[0]
    ang16 = fq_ref[...] * pos
    ang32 = f64_ref[...] * pos
    c16, s16, c32, s32 = jnp.cos(ang16), jnp.sin(ang16), jnp.cos(ang32), jnp.sin(ang32)
    one = jnp.ones((MLA_NOPE, tm), F32)
    zero = jnp.zeros((MLA_NOPE, tm), F32)
    pad = LANES - MLA_NOPE - MLA_ROPE
    cosq = jnp.concatenate([one, c16, c16, one[:pad]], axis=0).T
    sinq = jnp.concatenate([zero, s16, s16, zero[:pad]], axis=0).T
    cos64 = jnp.concatenate([c32, c32, c32, c32], axis=0).T
    sin64 = jnp.concatenate([s32, s32, s32, s32], axis=0).T
    rope_lo, rope_hi = MLA_NOPE, MLA_NOPE + MLA_ROPE

    xa = jnp.dot(xn, win_ref[:, _A0:_A1], preferred_element_type=F32)
    cqn = _rms(xa[:, 0:256], gq_ref[...]).astype(BF16)
    qa = jnp.dot(cqn, wq_ref[...], preferred_element_type=F32)
    qb = _rotate_half(qa, LANES, rope_lo, rope_hi)
    q_scale = LOG2E / math.sqrt(MLA_NOPE + MLA_ROPE)
    ckvn = _rms(xa[:, 256:384], gkv_ref[...]).astype(BF16)
    ka = jnp.dot(ckvn, wk_ref[...], preferred_element_type=F32)
    kr = xa[:, 384:512]
    krope = kr * cosq + _rotate_half(kr, LANES, rope_lo, rope_hi) * sinq
    for h in range(MLA_HEADS):
        sl = slice(h * LANES, (h + 1) * LANES)
        qm_ref[:, sl] = ((qa[:, sl] * cosq + qb[:, sl] * sinq) * q_scale).astype(BF16)
        km_ref[:, sl] = (ka[:, sl] + krope).astype(BF16)
    vm_ref[...] = lax.dot_general(wv_ref[...], ckvn, (((1,), (1,)), ((), ())),
                                  preferred_element_type=F32).astype(BF16)

    xb = jnp.dot(xn, win_ref[:, _B0:_B1], preferred_element_type=F32)
    s_scale = LOG2E / math.sqrt(SWA_HEAD_DIM)
    nq = SWA_HEADS * SWA_HEAD_DIM
    qs = xb[:, :nq]
    qs_rot = _rotate_half(qs, SWA_HEAD_DIM, 0, SWA_HEAD_DIM)
    for p in range(SWA_HEADS // 2):
        sl = slice(p * LANES, (p + 1) * LANES)
        qs_ref[:, sl] = ((qs[:, sl] * cos64 + qs_rot[:, sl] * sin64) * s_scale).astype(BF16)
    ks = xb[:, nq:nq + LANES]
    ks = ks * cos64 + _rotate_half(ks, SWA_HEAD_DIM, 0, SWA_HEAD_DIM) * sin64
    ksa_ref[...] = ks.astype(BF16)
    ksb_ref[...] = pltpu.roll(ks, 64, axis=1).astype(BF16)
    vs = xb[:, nq + LANES:nq + 2 * LANES]
    vsa_ref[...] = vs.astype(BF16)
    vsb_ref[...] = pltpu.roll(vs, 64, axis=1).astype(BF16)

    xc = jnp.dot(xn, win_ref[:, _C0:_C1], preferred_element_type=F32)
    qx_ref[...] = (xc * (LOG2E / math.sqrt(XA_HEAD_DIM))).astype(BF16)

    xd = jnp.dot(xn, win_ref[:, _D0:_D1], preferred_element_type=F32) + bgate_ref[...]
    gt_ref[...] = jax.nn.sigmoid(xd).astype(BF16)


def _proj_call(x2, pos, fq, f64, g_mix, w_in_al, g_q, wq, g_kv, wk_aug, wv, b_gate):
    t, d = x2.shape
    tm = PROJ_TM
    row = lambda n: pl.BlockSpec((tm, n), lambda i: (i, 0))
    full = lambda a: pl.BlockSpec(a.shape, lambda i: (0,) * a.ndim)
    out_cols = [1024, 1024, 512, 128, 128, 128, 128, 512, 3072]
    out_specs = [row(n) for n in out_cols]
    out_shape = [jax.ShapeDtypeStruct((t, n), BF16) for n in out_cols]
    vt_rows = wv.shape[0]
    out_specs.insert(2, pl.BlockSpec((vt_rows, tm), lambda i: (0, i)))
    out_shape.insert(2, jax.ShapeDtypeStruct((vt_rows, t), BF16))
    return pl.pallas_call(
        _proj_kernel,
        grid=(t // tm,),
        in_specs=[row(d), pl.BlockSpec((1, tm), lambda i: (0, i)), full(fq), full(f64), full(g_mix),
                  pl.BlockSpec(w_in_al.shape, lambda i: (0, 0), pipeline_mode=pl.Buffered(1)),
                  full(g_q), full(wq), full(g_kv), full(wk_aug), full(wv), full(b_gate)],
        out_specs=out_specs,
        out_shape=out_shape,
        compiler_params=pltpu.CompilerParams(dimension_semantics=("arbitrary",), vmem_limit_bytes=VMEM_LIMIT),
        name="proj",
    )(x2, pos, fq, f64, g_mix, w_in_al, g_q, wq, g_kv, wk_aug, wv, b_gate)


def _mla_kernel(q_ref, k_ref, vt_ref, o_ref, sa_ref, sb_ref, ma_ref, mb_ref, m_ref, acc_ref, *, tq):
    i = pl.program_id(2)
    m_ref[...] = jnp.full(m_ref.shape, NEG_BIG, F32)
    acc_ref[...] = jnp.zeros(acc_ref.shape, F32)
    ones = jnp.ones((MLA_ONES_ROWS, tq), BF16)

    def scores(j, bufs):
        s_ref, mx_ref = bufs
        k0 = pl.multiple_of(j * tq, tq)
        for hh in range(2):
            sl = slice(hh * LANES, (hh + 1) * LANES)
            st = lax.dot_general(k_ref[0, pl.ds(k0, tq), sl], q_ref[0, :, sl], (((1,), (1,)), ((), ())),
                                 preferred_element_type=F32)
            s_ref[hh] = st
            mx_ref[hh] = jnp.max(st, axis=0, keepdims=True)

    def update(j, bufs, masked):
        s_ref, mx_ref = bufs
        k0 = pl.multiple_of(j * tq, tq)
        hq = tq // 2
        parts = ((slice(0, hq), hq, 0), (slice(hq, tq), tq, hq)) if masked else ((slice(0, tq), tq, 0),)
        for hh in range(2):
            for cols, nk, q0 in parts:
                nq = cols.stop - cols.start
                vt = jnp.concatenate([vt_ref[hh * MLA_V:(hh + 1) * MLA_V, pl.ds(k0, nk)], ones[:, :nk]], axis=0)
                st = s_ref[hh, 0:nk, cols]
                if masked:
                    kj = lax.broadcasted_iota(I32, (nk, nq), 0)
                    qi = lax.broadcasted_iota(I32, (nk, nq), 1) + q0
                    st = jnp.where(kj <= qi, st, NEG_BIG)
                    m_tile = jnp.max(st, axis=0, keepdims=True)
                else:
                    m_tile = mx_ref[hh]
                m_old = m_ref[hh, :, cols]
                m_new = jnp.maximum(m_old, m_tile)
                alpha = jnp.exp2(m_old - m_new)
                pt = jnp.exp2(st - m_new)
                acc_ref[hh, :, cols] = (alpha * acc_ref[hh, :, cols]
                                        + jnp.dot(vt, pt.astype(BF16), preferred_element_type=F32))
                m_ref[hh, :, cols] = m_new

    buf_a = (sa_ref, ma_ref)
    buf_b = (sb_ref, mb_ref)
    scores(0, buf_a)

    def body(jj, carry):
        scores(2 * jj + 1, buf_b)
        update(2 * jj, buf_a, False)
        scores(2 * jj + 2, buf_a)
        update(2 * jj + 1, buf_b, False)
        return carry

    lax.fori_loop(0, i // 2, body, 0)

    @pl.when(i % 2 == 0)
    def _():
        update(i, buf_a, True)

    @pl.when(i % 2 == 1)
    def _():
        scores(i, buf_b)
        update(i - 1, buf_a, False)
        update(i, buf_b, True)

    ot = jnp.concatenate([acc_ref[hh, :MLA_V] / acc_ref[hh, MLA_V:MLA_V + 1] for hh in range(2)],
                         axis=0)
    o_ref[0] = ot.T.astype(BF16)


def _mla_call(q, k, vt, seq):
    b, s, _ = q.shape
    assert s == seq
    tq = min(MLA_TQ, s)
    n_pairs = MLA_HEADS // 2
    return pl.pallas_call(
        functools.partial(_mla_kernel, tq=tq),
        grid=(b, n_pairs, s // tq),
        in_specs=[pl.BlockSpec((1, tq, 2 * LANES), lambda bi, hp, i: (bi, i, hp)),
                  pl.BlockSpec((1, s, 2 * LANES), lambda bi, hp, i: (bi, 0, hp)),
                  pl.BlockSpec((2 * MLA_V, s), lambda bi, hp, i: (hp, bi))],
        out_specs=pl.BlockSpec((1, tq, LANES), lambda bi, hp, i: (bi, i, hp)),
        out_shape=jax.ShapeDtypeStruct((b, s, n_pairs * LANES), BF16),
        scratch_shapes=[pltpu.VMEM((2, tq, tq), F32), pltpu.VMEM((2, tq, tq), F32),
                        pltpu.VMEM((2, 1, tq), F32), pltpu.VMEM((2, 1, tq), F32),
                        pltpu.VMEM((2, 1, tq), F32),
                        pltpu.VMEM((2, MLA_V + MLA_ONES_ROWS, tq), F32)],
        compiler_params=pltpu.CompilerParams(dimension_semantics=("arbitrary",) * 3, vmem_limit_bytes=VMEM_LIMIT),
        name="mla_attn",
    )(q, k, vt)


def _swa_kernel(sink_ref, q_ref, ka_ref, kb_ref, va_ref, vb_ref, kah_ref, kbh_ref, vah_ref, vbh_ref, o_ref, *, ts):
    w = SWA_WINDOW
    i = pl.program_id(1)
    ka = jnp.concatenate([kah_ref[0], ka_ref[0]], axis=0)
    kb = jnp.concatenate([kbh_ref[0], kb_ref[0]], axis=0)
    va = jnp.concatenate([vah_ref[0], va_ref[0]], axis=0)
    vb = jnp.concatenate([vbh_ref[0], vb_ref[0]], axis=0)
    lane_k = lax.broadcasted_iota(I32, (2 * w, LANES), 1)
    low = lane_k < SWA_HEAD_DIM
    qi = lax.broadcasted_iota(I32, (2 * w, 2 * w), 0) % w
    kj = lax.broadcasted_iota(I32, (2 * w, 2 * w), 1)
    diff = qi + w - kj
    band = (diff >= 0) & (diff < w)
    lane_o = lax.broadcasted_iota(I32, (w, LANES), 1)
    row2 = lax.broadcasted_iota(I32, (2 * w, 1), 0)
    zero = jnp.zeros((), BF16)
    stacks = ((0, ka, True, va), (1, kb, False, vb), (4, kb, True, vb), (5, ka, False, va))
    for n in range(ts // w):
        mask = band & ((i * (ts // w) + n > 0) | (kj >= w))
        res = []
        for h0, ksrc, keep_low, vsrc in stacks:
            p0 = h0 // 2
            q = jnp.concatenate([q_ref[0, n * w:(n + 1) * w, p0 * LANES:(p0 + 1) * LANES],
                                 q_ref[0, n * w:(n + 1) * w, (p0 + 1) * LANES:(p0 + 2) * LANES]], axis=0)
            kwin = ksrc[n * w:n * w + 2 * w]
            kwin = jnp.where(low if keep_low else ~low, kwin, zero)
            vwin = vsrc[n * w:n * w + 2 * w]
            s = lax.dot_general(q, kwin, (((1,), (1,)), ((), ())), preferred_element_type=F32)
            s = jnp.where(mask, s, NEG_BIG)
            sink = jnp.where(row2 < w, sink_ref[h0], sink_ref[h0 + 2]) * LOG2E
            m = jnp.maximum(jnp.max(s, axis=1, keepdims=True), sink)
            p = jnp.exp2(s - m)
            den = jnp.sum(p, axis=1, keepdims=True) + jnp.exp2(sink - m)
            o = jnp.dot(p.astype(BF16), vwin, preferred_element_type=F32) / den
            res.append(o)
        o02, o13, o46, o57 = res
        sel = lane_o < SWA_HEAD_DIM
        rows = slice(n * w, (n + 1) * w)
        o_ref[0, rows, 0 * LANES:1 * LANES] = jnp.where(sel, o02[:w], o13[:w]).astype(BF16)
        o_ref[0, rows, 1 * LANES:2 * LANES] = jnp.where(sel, o02[w:], o13[w:]).astype(BF16)
        o_ref[0, rows, 2 * LANES:3 * LANES] = jnp.where(sel, o46[:w], o57[:w]).astype(BF16)
        o_ref[0, rows, 3 * LANES:4 * LANES] = jnp.where(sel, o46[w:], o57[w:]).astype(BF16)


def _swa_call(sinks, q, ksa, ksb, vsa, vsb):
    b, s, _ = q.shape
    ts = min(SWA_TS, s)
    w = SWA_WINDOW
    r = ts // w
    main = pl.BlockSpec((1, ts, LANES), lambda bi, i: (bi, i, 0))
    halo = pl.BlockSpec((1, w, LANES), lambda bi, i: (bi, jnp.maximum(i * r - 1, 0), 0))
    return pl.pallas_call(
        functools.partial(_swa_kernel, ts=ts),
        grid=(b, s // ts),
        in_specs=[pl.BlockSpec(memory_space=pltpu.SMEM),
                  pl.BlockSpec((1, ts, 4 * LANES), lambda bi, i: (bi, i, 0)),
                  main, main, main, main, halo, halo, halo, halo],
        out_specs=pl.BlockSpec((1, ts, 4 * LANES), lambda bi, i: (bi, i, 0)),
        out_shape=jax.ShapeDtypeStruct((b, s, 4 * LANES), BF16),
        compiler_params=pltpu.CompilerParams(dimension_semantics=("arbitrary",) * 2, vmem_limit_bytes=VMEM_LIMIT),
        name="swa_attn",
    )(sinks, q, ksa, ksb, vsa, vsb, ksa, ksb, vsa, vsb)


def _memkv_kernel(mem_ref, g_ref, w_ref, o_ref):
    mn = _rms(mem_ref[...], g_ref[...]).astype(BF16)
    o_ref[...] = jnp.dot(mn, w_ref[...], preferred_element_type=F32).astype(BF16)


def _memkv_call(mem2, g_mem, w_mem_kv):
    n, d = mem2.shape
    tm = min(256, n)
    return pl.pallas_call(
        _memkv_kernel,
        grid=(n // tm,),
        in_specs=[pl.BlockSpec((tm, d), lambda i: (i, 0)),
                  pl.BlockSpec(g_mem.shape, lambda i: (0, 0)),
                  pl.BlockSpec(w_mem_kv.shape, lambda i: (0, 0))],
        out_specs=pl.BlockSpec((tm, w_mem_kv.shape[1]), lambda i: (i, 0)),
        out_shape=jax.ShapeDtypeStruct((n, w_mem_kv.shape[1]), BF16),
        compiler_params=pltpu.CompilerParams(dimension_semantics=("arbitrary",), vmem_limit_bytes=VMEM_LIMIT),
        name="mem_kv",
    )(mem2, g_mem, w_mem_kv)


def _merge_kernel(x_ref, omla_ref, oswa_ref, qx_ref, gt_ref, kvm_ref, wmo_ref, wso_ref, wxo_ref, wout_ref,
                  gffn_ref, wr_ref, br_ref,
                  h_ref, hnp_ref, idx_ref, gate_ref, rank_ref, cnt_ref, run_ref, *, tm, sub):
    @pl.when(pl.program_id(0) == 0)
    def _():
        run_ref[...] = jnp.zeros(run_ref.shape, F32)

    d = x_ref.shape[1]
    kv_cols = XA_HEADS * XA_HEAD_DIM
    erow = lax.broadcasted_iota(I32, (N_EXPERTS, sub), 0)
    tri_t = (lax.broadcasted_iota(I32, (sub, sub), 0) < lax.broadcasted_iota(I32, (sub, sub), 1)).astype(BF16)
    nt = (((1,), (1,)), ((), ()))
    run = run_ref[...]
    for hf in range(tm // sub):
        rows = slice(hf * sub, (hf + 1) * sub)

        oxs = []
        for hd in range(XA_HEADS):
            sl = slice(hd * LANES, (hd + 1) * LANES)
            km = kvm_ref[0, :, sl]
            vm = kvm_ref[0, :, kv_cols + hd * LANES:kv_cols + (hd + 1) * LANES]
            s = lax.dot_general(qx_ref[rows, sl], km, nt, preferred_element_type=F32)
            p = jnp.exp2(s - jnp.max(s, axis=1, keepdims=True))
            den = jnp.sum(p, axis=1, keepdims=True)
            oxs.append((jnp.dot(p.astype(BF16), vm, preferred_element_type=F32) / den).astype(BF16))
        oxa = jnp.concatenate(oxs, axis=1)

        merged = (gt_ref[rows, 0:d].astype(F32) * jnp.dot(omla_ref[rows, :], wmo_ref[...], preferred_element_type=F32)
                  + gt_ref[rows, d:2 * d].astype(F32) * jnp.dot(oswa_ref[rows, :], wso_ref[...],
                                                                 preferred_element_type=F32)
                  + gt_ref[rows, 2 * d:3 * d].astype(F32) * jnp.dot(oxa, wxo_ref[...], preferred_element_type=F32))
        h = x_ref[rows, :] + jnp.dot(merged.astype(BF16), wout_ref[...], preferred_element_type=F32)
        h_ref[rows, :] = h

        hn = _rms(h, gffn_ref[...])
        hn_hi = hn.astype(BF16)
        hn_hi32 = hn_hi.astype(F32)
        hn_lo = (hn - hn_hi32).astype(BF16)
        bits = pltpu.bitcast(hn_hi32, U32)
        _store_slabs(hnp_ref, hf * sub, (bits[:, : d // 2] >> 16) | (bits[:, d // 2:] & jnp.uint32(0xFFFF0000)))

        part = lax.dot_general(wr_ref[...], hn_hi, nt, preferred_element_type=F32)
        logits_t = (part[:N_EXPERTS] + part[N_EXPERTS:]
                    + lax.dot_general(wr_ref[0:N_EXPERTS, :], hn_lo, nt, preferred_element_type=F32) + br_ref[...])

        work = logits_t
        vals, idxs, hots = [], [], []
        for _ in range(TOP_K):
            mx = jnp.max(work, axis=0, keepdims=True)
            ix = jnp.min(jnp.where(work == mx, erow, N_EXPERTS), axis=0, keepdims=True)
            hot = erow == ix
            work = jnp.where(hot, -jnp.inf, work)
            vals.append(mx)
            idxs.append(ix)
            hots.append(hot)
        es = [jnp.exp(v - vals[0]) for v in vals]
        den = es[0] + es[1] + es[2] + es[3]
        sel_t = (hots[0] | hots[1] | hots[2] | hots[3])
        prefix_t = jnp.dot(sel_t.astype(BF16), tri_t, preferred_element_type=F32) + run
        for k in range(TOP_K):
            idx_ref[k:k + 1, rows] = idxs[k]
            gate_ref[k:k + 1, rows] = es[k] / den
            rank_ref[k:k + 1, rows] = jnp.sum(jnp.where(hots[k], prefix_t, 0.0), axis=0, keepdims=True).astype(I32)
        run = run + jnp.sum(sel_t.astype(F32), axis=1, keepdims=True)
    run_ref[...] = run
    cnt_ref[...] = run.astype(I32)


def _merge_call(x2, omla, oswa, qx, gates, kvm, wmo, wso, wxo, wout, g_ffn, wr_split, b_router_col, seq):
    t, d = x2.shape
    tm = MERGE_TM
    per_b = seq // tm
    row = lambda n: pl.BlockSpec((tm, n), lambda i: (i, 0))
    col = lambda: pl.BlockSpec((TOP_K, tm), lambda i: (0, i))
    full = lambda a: pl.BlockSpec(a.shape, lambda i: (0,) * a.ndim)
    return pl.pallas_call(
        functools.partial(_merge_kernel, tm=tm, sub=MERGE_SUB),
        grid=(t // tm,),
        in_specs=[row(d), row(512), row(512), row(512), row(3 * d),
                  pl.BlockSpec((1,) + kvm.shape[1:], lambda i: (i // per_b, 0, 0)),
                  full(wmo), full(wso), full(wxo), full(wout), full(g_ffn), full(wr_split), full(b_router_col)],
        out_specs=[row(d), pl.BlockSpec((tm, d // 2 // LANES, LANES), lambda i: (i, 0, 0)), col(), col(), col(),
                   pl.BlockSpec((N_EXPERTS, 1), lambda i: (0, 0))],
        out_shape=[jax.ShapeDtypeStruct((t, d), F32), jax.ShapeDtypeStruct((t, d // 2 // LANES, LANES), U32),
                   jax.ShapeDtypeStruct((TOP_K, t), I32), jax.ShapeDtypeStruct((TOP_K, t), F32),
                   jax.ShapeDtypeStruct((TOP_K, t), I32), jax.ShapeDtypeStruct((N_EXPERTS, 1), I32)],
        scratch_shapes=[pltpu.VMEM((N_EXPERTS, 1), F32)],
        compiler_params=pltpu.CompilerParams(dimension_semantics=("arbitrary",), vmem_limit_bytes=VMEM_LIMIT),
        name="merge_router",
    )(x2, omla, oswa, qx, gates, kvm, wmo, wso, wxo, wout, g_ffn, wr_split, b_router_col)


SC_CORES = 2
SC_SUBCORES = 16
SC_WORKERS = SC_CORES * SC_SUBCORES
SC_CHUNK = 64


def _sc_mesh():
    return plsc.VectorSubcoreMesh(core_axis_name="c", subcore_axis_name="s",
                                  num_cores=SC_CORES, num_subcores=SC_SUBCORES)


def _sc_index_blocks(idx):
    n = idx.shape[0]
    per_w = n // SC_WORKERS
    n_ch = per_w // SC_CHUNK
    assert per_w * SC_WORKERS == n and n_ch * SC_CHUNK == per_w
    return idx.reshape(SC_WORKERS, n_ch, SC_CHUNK), per_w, n_ch


def _sc_two_buffer_loop(n_ch, load, stores, bufs, load_sems, store_sems):
    assert n_ch % 2 == 0
    a, b = bufs
    la, lb = load_sems
    sa, sb = store_sems

    def start(cps):
        for cp in cps:
            cp.start()

    def wait(cps):
        for cp in cps:
            cp.wait()

    load(0, a, la).start()

    @pl.loop(0, n_ch, step=2)
    def _(j):
        load(j, a, la).wait()

        @pl.when(j > 0)
        def _():
            wait(stores(j - 1, b, sb))

        load(j + 1, b, lb).start()
        start(stores(j, a, sa))
        load(j + 1, b, lb).wait()
        wait(stores(j, a, sa))

        @pl.when(j + 2 < n_ch)
        def _():
            load(j + 2, a, la).start()

        start(stores(j + 1, b, sb))

    wait(stores(n_ch - 1, b, sb))


def _sc_scratch(n_ch, row_shape, dtype):
    return [pltpu.VMEM((n_ch, SC_CHUNK), I32), pltpu.VMEM((SC_CHUNK,) + row_shape, dtype),
            pltpu.VMEM((SC_CHUNK,) + row_shape, dtype)] + [pltpu.SemaphoreType.DMA] * 4


def _sc_scatter_rows(src, idx, n_out):
    fan, n_src = idx.shape
    per_w = n_src // SC_WORKERS
    n_ch = per_w // SC_CHUNK
    assert per_w * SC_WORKERS == n_src and n_ch * SC_CHUNK == per_w
    idx3 = idx.reshape(fan, SC_WORKERS, n_ch, SC_CHUNK).transpose(1, 0, 2, 3).reshape(SC_WORKERS, fan * n_ch, SC_CHUNK)

    @functools.partial(
        pl.kernel, mesh=_sc_mesh(),
        out_type=jax.ShapeDtypeStruct((n_out,) + src.shape[1:], src.dtype),
        scratch_types=_sc_scratch(fan * n_ch, src.shape[1:], src.dtype),
        name="moe_dispatch_sc")
    def k(src_hbm, idx_hbm, out_hbm, idx_v, rows_a, rows_b, la, lb, sa, sb):
        wid = lax.axis_index("s") * SC_CORES + lax.axis_index("c")
        base = wid * per_w
        pltpu.sync_copy(idx_hbm.at[wid], idx_v)

        def load(c, buf, sem):
            return pltpu.make_async_copy(src_hbm.at[pl.ds(base + c * SC_CHUNK, SC_CHUNK)], buf, sem)

        def stores(c, buf, sem):
            return tuple(pltpu.make_async_copy(buf, out_hbm.at[idx_v.at[f * n_ch + c]], sem) for f in range(fan))

        _sc_two_buffer_loop(n_ch, load, stores, (rows_a, rows_b), (la, lb), (sa, sb))

    return k(src, idx3)


def _sc_gather_rows(table, idx):
    idx3, per_w, n_ch = _sc_index_blocks(idx)

    @functools.partial(
        pl.kernel, mesh=_sc_mesh(),
        out_type=jax.ShapeDtypeStruct((idx.shape[0],) + table.shape[1:], table.dtype),
        scratch_types=_sc_scratch(n_ch, table.shape[1:], table.dtype),
        name="moe_gather_sc")
    def k(table_hbm, idx_hbm, out_hbm, idx_v, rows_a, rows_b, la, lb, sa, sb):
        wid = lax.axis_index("s") * SC_CORES + lax.axis_index("c")
        base = wid * per_w
        pltpu.sync_copy(idx_hbm.at[wid], idx_v)

        def load(c, buf, sem):
            return pltpu.make_async_copy(table_hbm.at[idx_v.at[c]], buf, sem)

        def stores(c, buf, sem):
            return (pltpu.make_async_copy(buf, out_hbm.at[pl.ds(base + c * SC_CHUNK, SC_CHUNK)], sem),)

        _sc_two_buffer_loop(n_ch, load, stores, (rows_a, rows_b), (la, lb), (sa, sb))

    return k(table, idx3)


def _unpack_lo(w):
    return pltpu.bitcast(w << 16, F32)


def _unpack_hi(w):
    return pltpu.bitcast(w & jnp.uint32(0xFFFF0000), F32)


def _ffn_kernel(te_ref, xs_ref, wgu_hbm, bgu_ref, wd_hbm, bd_ref, y_ref, wgu_f32, wd_f32, wgu_bf, wd_bf, wsem):
    i = pl.program_id(0)
    n = pl.num_programs(0)
    n_used = te_ref[n]
    e = te_ref[i]
    first = (i == 0) | (e != te_ref[jnp.maximum(i - 1, 0)])
    slot = te_ref[2 * n + 1 + i] % 2
    nxt = te_ref[3 * n + 1 + i]

    def fetch(expert, s):
        return (pltpu.make_async_copy(wgu_hbm.at[expert], wgu_f32.at[s], wsem.at[0, s]),
                pltpu.make_async_copy(wd_hbm.at[expert], wd_f32.at[s], wsem.at[1, s]))

    @pl.when(i == 0)
    def _():
        for cp in fetch(e, slot):
            cp.start()

    @pl.when((i < n_used) & first)
    def _():
        for cp in fetch(e, slot):
            cp.wait()
        wgu_bf[...] = wgu_f32[slot].astype(BF16)
        wd_bf[...] = wd_f32[slot].astype(BF16)

        @pl.when(nxt >= 0)
        def _():
            for cp in fetch(nxt, 1 - slot):
                cp.start()

    valid = jnp.where(i < n_used, te_ref[n + 1 + i], 0)
    rows, n_slab, _ = xs_ref.shape
    half = n_slab * LANES
    for sb in range(rows // FFN_SUB):
        r0 = sb * FFN_SUB

        @pl.when(valid > r0)
        def _():
            w = _load_slabs(xs_ref, r0, FFN_SUB)
            w = jnp.where(lax.broadcasted_iota(I32, w.shape, 0) < valid - r0, w, jnp.uint32(0))
            x_lo = _unpack_lo(w).astype(BF16)
            x_hi = _unpack_hi(w).astype(BF16)
            gu = (jnp.dot(x_lo, wgu_bf[0:half, :], preferred_element_type=F32)
                  + jnp.dot(x_hi, wgu_bf[half:, :], preferred_element_type=F32) + bgu_ref[0])
            de = gu.shape[1] // 2
            x_glu = jnp.minimum(gu[:, :de], SWIGLU_LIMIT)
            x_lin = jnp.clip(gu[:, de:], -SWIGLU_LIMIT, SWIGLU_LIMIT)
            hdn = x_glu * jax.nn.sigmoid(SWIGLU_ALPHA * x_glu) * (x_lin + 1.0)
            y = jnp.dot(hdn.astype(BF16), wd_bf[...], preferred_element_type=F32) + bd_ref[0]
            bits = pltpu.bitcast(y.astype(BF16).astype(F32), U32)
            _store_slabs(y_ref, r0, (bits[:, :half] >> 16) | (bits[:, half:] & jnp.uint32(0xFFFF0000)))

        @pl.when(valid <= r0)
        def _():
            y_ref[r0:r0 + FFN_SUB] = jnp.zeros((FFN_SUB,) + y_ref.shape[1:], U32)


def _ffn_call(tile_table, xs, w_gate_up, b_gate_up, w_down, b_down):
    r, ns, lanes = xs.shape
    tm = FFN_TM
    ne, d, de2 = w_gate_up.shape
    return pl.pallas_call(
        _ffn_kernel,
        grid_spec=pltpu.PrefetchScalarGridSpec(
            num_scalar_prefetch=1,
            grid=(r // tm,),
            in_specs=[pl.BlockSpec((tm, ns, lanes),
                                   lambda i, te: (jnp.minimum(i, jnp.maximum(te[r // tm] - 1, 0)), 0, 0)),
                      pl.BlockSpec(memory_space=pl.ANY),
                      pl.BlockSpec((1, 1, de2), lambda i, te: (te[i], 0, 0)),
                      pl.BlockSpec(memory_space=pl.ANY),
                      pl.BlockSpec((1, 1, d), lambda i, te: (te[i], 0, 0))],
            out_specs=pl.BlockSpec((tm, ns, lanes), lambda i, te: (i, 0, 0)),
            scratch_shapes=[pltpu.VMEM((2, d, de2), F32), pltpu.VMEM((2, de2 // 2, d), F32),
                            pltpu.VMEM((d, de2), BF16), pltpu.VMEM((de2 // 2, d), BF16),
                            pltpu.SemaphoreType.DMA((2, 2))]),
        out_shape=jax.ShapeDtypeStruct((r, ns, lanes), U32),
        compiler_params=pltpu.CompilerParams(dimension_semantics=("arbitrary",), vmem_limit_bytes=VMEM_LIMIT),
        name="moe_ffn",
    )(tile_table, xs, w_gate_up, b_gate_up, w_down, b_down)


def _combine_dense_kernel(h_ref, gate_ref, gfin_ref, y0_ref, y1_ref, y2_ref, y3_ref, o_ref):
    half = y0_ref.shape[1] * y0_ref.shape[2]
    lo = h_ref[:, :half]
    hi = h_ref[:, half:]
    gates_kt = gate_ref[...]
    gates_tk = jnp.concatenate([gates_kt, jnp.zeros((SUBLANES - TOP_K, gates_kt.shape[1]), F32)], axis=0).T
    for k, y_ref in enumerate((y0_ref, y1_ref, y2_ref, y3_ref)):
        g = gates_tk[:, k:k + 1]
        w = _load_slabs(y_ref)
        lo = lo + g * _unpack_lo(w)
        hi = hi + g * _unpack_hi(w)
    ms = (jnp.sum(lo * lo, axis=1, keepdims=True) + jnp.sum(hi * hi, axis=1, keepdims=True)) / (2 * half)
    inv = lax.rsqrt(ms + RMS_EPS)
    o_ref[:, :half] = lo * inv * gfin_ref[:, :half]
    o_ref[:, half:] = hi * inv * gfin_ref[:, half:]


def _combine_dense_into_kernel(h_ref, gate_ref, gfin_ref, y0_ref, y1_ref, y2_ref, y3_ref, prev_ref, o_ref):
    del prev_ref
    _combine_dense_kernel(h_ref, gate_ref, gfin_ref, y0_ref, y1_ref, y2_ref, y3_ref, o_ref)


def _combine_dense_call(h, gate, g_final, yk, part, n_parts, prev_out):
    t, d = h.shape
    tm = COMBINE_TM
    steps = t // tm // n_parts
    first = part * steps
    yspec = lambda k: pl.BlockSpec((tm,) + yk.shape[1:], lambda i: (k * steps + i, 0, 0))
    in_specs = [pl.BlockSpec((tm, d), lambda i: (first + i, 0)),
                pl.BlockSpec((TOP_K, tm), lambda i: (0, first + i)),
                pl.BlockSpec((1, d), lambda i: (0, 0)),
                yspec(0), yspec(1), yspec(2), yspec(3)]
    args = [h, gate, g_final, yk, yk, yk, yk]
    aliases = {}
    body = _combine_dense_kernel
    if prev_out is not None:
        in_specs.append(pl.BlockSpec(memory_space=pl.ANY))
        args.append(prev_out)
        aliases = {len(args) - 1: 0}
        body = _combine_dense_into_kernel
    return pl.pallas_call(
        body,
        grid=(steps,),
        in_specs=in_specs,
        out_specs=pl.BlockSpec((tm, d), lambda i: (first + i, 0)),
        out_shape=jax.ShapeDtypeStruct((t, d), F32),
        input_output_aliases=aliases,
        compiler_params=pltpu.CompilerParams(dimension_semantics=("arbitrary",), vmem_limit_bytes=VMEM_LIMIT),
        name="moe_combine",
    )(*args)


def _rope_freqs():
    def inv_freq(dh):
        return (ROPE_THETA ** (-jnp.arange(0, dh, 2, dtype=F32) / dh))[:, None]

    return inv_freq(MLA_ROPE), inv_freq(SWA_HEAD_DIM)


def _winprep_kernel(wt_ref, o_ref):
    j = pl.program_id(0)
    c1 = MLA_Q_RANK + MLA_KV_RANK
    kr_block = c1 // LANES

    @pl.when(j != kr_block)
    def _():
        row0 = jnp.where(j < kr_block, j * LANES, c1 + MLA_ROPE + (j - kr_block - 1) * LANES)
        o_ref[...] = wt_ref[pl.ds(pl.multiple_of(row0, SUBLANES), LANES), :].T.astype(BF16)

    @pl.when(j == kr_block)
    def _():
        d = wt_ref.shape[1]
        blk = jnp.concatenate([jnp.zeros((MLA_NOPE, d), F32), wt_ref[c1:c1 + MLA_ROPE, :],
                               jnp.zeros((LANES - MLA_NOPE - MLA_ROPE, d), F32)], axis=0)
        o_ref[...] = blk.T.astype(BF16)


def _winprep_call(w_in, layer):
    _, d, n = w_in.shape
    w_t = jnp.swapaxes(w_in, 1, 2)
    return pl.pallas_call(
        _winprep_kernel,
        grid=(_D1 // LANES,),
        in_specs=[pl.BlockSpec((None, n, d), lambda j: (layer, 0, 0), pipeline_mode=pl.Buffered(1))],
        out_specs=pl.BlockSpec((d, LANES), lambda j: (0, j)),
        out_shape=jax.ShapeDtypeStruct((d, _D1), BF16),
        compiler_params=pltpu.CompilerParams(dimension_semantics=("arbitrary",), vmem_limit_bytes=VMEM_LIMIT),
        name="w_in_prep",
    )(w_t)


def _prep_weights(w_in, layer, w_mla_uq, w_mla_ukv):
    w_in_al = _winprep_call(w_in, layer)

    r = w_mla_uq.shape[0]
    wq = w_mla_uq.reshape(r, MLA_HEADS, MLA_NOPE + MLA_ROPE)
    zq = jnp.zeros((r, MLA_HEADS, LANES - MLA_NOPE - MLA_ROPE), w_mla_uq.dtype)
    wq_pad = jnp.concatenate([wq, zq], axis=-1).reshape(r, MLA_HEADS * LANES).astype(BF16)

    rk = w_mla_ukv.shape[0]
    wkv = w_mla_ukv.reshape(rk, MLA_HEADS, MLA_NOPE + MLA_V)
    wk_aug = jnp.concatenate([wkv[..., :MLA_NOPE], jnp.zeros((rk, MLA_HEADS, LANES - MLA_NOPE), w_mla_ukv.dtype)],
                             axis=-1).reshape(rk, MLA_HEADS * LANES).astype(BF16)
    wv_t = wkv[..., MLA_NOPE:].reshape(rk, MLA_HEADS * MLA_V).T.astype(BF16)
    return w_in_al, wq_pad, wk_aug, wv_t


def kernel(x, mem, positions, g_mix, w_in, g_mla_q, w_mla_uq, g_mla_kv, w_mla_ukv, w_mla_o, swa_sinks, w_swa_o,
           g_mem, w_mem_kv, w_xa_o, b_gate, w_out, g_ffn, w_router, b_router, w_gate_up, b_gate_up, w_down,
           b_down, g_final):
    b, s, d = x.shape
    t = b * s
    depth = g_mix.shape[0]
    h = x.reshape(t, d)
    pos = positions.astype(F32).reshape(1, t)
    fq, f64 = _rope_freqs()
    for l in range(depth):
        w_in_al, wq_pad, wk_aug, wv_t = _prep_weights(w_in, l, w_mla_uq[l], w_mla_ukv[l])
        (qm, km, vmt, qs, ksa, ksb, vsa, vsb, qx, gates) = _proj_call(
            h, pos, fq, f64, g_mix[l][None], w_in_al, g_mla_q[l][None], wq_pad,
            g_mla_kv[l][None], wk_aug, wv_t, b_gate[l][None])
        r3 = lambda a: a.reshape(b, s, a.shape[1])
        omla = _mla_call(r3(qm), r3(km), vmt, s).reshape(t, -1)
        oswa = _swa_call(swa_sinks[l], r3(qs), r3(ksa), r3(ksb), r3(vsa), r3(vsb)).reshape(t, -1)
        m = mem.shape[1]
        kvm = _memkv_call(mem.reshape(b * m, d), g_mem[l][None], w_mem_kv[l].astype(BF16)).reshape(b, m, -1)
        wr_t = w_router[l].T
        wr_hi = wr_t.astype(BF16)
        wr_split = jnp.concatenate([wr_hi, (wr_t - wr_hi.astype(F32)).astype(BF16)], axis=0)
        h_mid, hnp, idx, gate, rank, counts = _merge_call(
            h, omla, oswa, qx, gates, kvm, w_mla_o[l].astype(BF16), w_swa_o[l].astype(BF16),
            w_xa_o[l].astype(BF16), w_out[l].astype(BF16), g_ffn[l][None], wr_split, b_router[l][:, None], s)

        counts = counts[:, 0]
        padded = ((counts + FFN_TM - 1) // FFN_TM) * FFN_TM
        padded_end = jnp.cumsum(padded)
        offsets = padded_end - padded
        experts = jnp.arange(N_EXPERTS, dtype=I32)
        dest = (jnp.sum(jnp.where(idx[..., None] == experts, offsets, 0), axis=-1) + rank).reshape(-1).astype(I32)
        n_tiles = (t * TOP_K) // FFN_TM + N_EXPERTS
        n_used = (padded_end[-1] // FFN_TM).astype(I32)
        tile_start = jnp.minimum(jnp.arange(n_tiles, dtype=I32), jnp.maximum(n_used - 1, 0)) * FFN_TM
        tile_expert = jnp.sum((padded_end[None, :] <= tile_start[:, None]).astype(I32), axis=1)
        tile_expert = jnp.minimum(tile_expert, N_EXPERTS - 1)
        onehot = tile_expert[:, None] == experts[None, :]
        pick = lambda v: jnp.sum(jnp.where(onehot, v[None, :], 0), axis=1)
        tile_valid = jnp.clip(pick(counts) - (tile_start - pick(offsets)), 0, FFN_TM)
        nonempty = padded > 0
        tile_group = pick(jnp.cumsum(nonempty.astype(I32)) - 1)
        later = jnp.where(nonempty[None, :] & (experts[None, :] > experts[:, None]), experts[None, :], N_EXPERTS)
        next_expert = jnp.min(later, axis=1)
        tile_next = pick(jnp.where(next_expert < N_EXPERTS, next_expert, -1))
        te = jnp.concatenate([tile_expert, n_used[None], tile_valid.astype(I32), tile_group.astype(I32),
                              tile_next.astype(I32)])

        xs = _sc_scatter_rows(hnp, dest.reshape(TOP_K, t), n_tiles * FFN_TM)
        y = _ffn_call(te, xs, w_gate_up[l], b_gate_up[l][:, None, :], w_down[l], b_down[l][:, None, :])
        if l == depth - 1:
            gfin = g_final[None]
            dest_parts = dest.reshape(TOP_K, COMBINE_PARTS, t // COMBINE_PARTS)
            out = None
            for part in range(COMBINE_PARTS):
                yk = _sc_gather_rows(y, dest_parts[:, part].reshape(-1))
                out = _combine_dense_call(h_mid, gate, gfin, yk, part, COMBINE_PARTS, out)
        else:
            raise NotImplementedError("depth > 1 needs a combine without the final norm")
        h = out
    return h.reshape(b, s, d)
```

```python
import functools
import math

import jax
import jax.numpy as jnp
from jax import lax
from jax.experimental import pallas as pl
from jax.experimental.pallas import tpu as pltpu
from jax.experimental.pallas import tpu_sc as plsc

F32 = jnp.float32
BF16 = jnp.bfloat16
U32 = jnp.uint32
I32 = jnp.int32

LANES = 128
SUBLANES = 8
ROPE_THETA = 10000.0
RMS_EPS = 1e-6
LOG2E = 1.4426950408889634

MLA_HEADS = 8
MLA_NOPE = 64
MLA_ROPE = 32
MLA_V = 64
MLA_Q_RANK = 256
MLA_KV_RANK = 128
SWA_HEADS = 8
SWA_KV_HEADS = 2
SWA_HEAD_DIM = 64
SWA_WINDOW = 128
XA_HEADS = 4
XA_HEAD_DIM = 128
N_EXPERTS = 32
TOP_K = 4
SWIGLU_ALPHA = 1.702
SWIGLU_LIMIT = 7.0
N_BRANCHES = 3

NEG_BIG = -1e30

PROJ_TM = 512
PROJ_SUB = 256
MLA_TQ = 1024
MLA_ONES_ROWS = 16
SWA_TS = 2048
MERGE_TM = 512
MERGE_SUB = 256
FFN_TM = 1024
FFN_SUB = 256
COMBINE_TM = 1024
COMBINE_PARTS = 2

VMEM_LIMIT = 56 * 1024 * 1024


def _rms(x, g):
    return x * lax.rsqrt(jnp.mean(x * x, axis=-1, keepdims=True) + RMS_EPS) * g


def _store_slabs(ref, row0, value):
    rows, n, _ = ref.shape
    flat = ref.reshape(rows * n, LANES)
    for c in range(n):
        flat[pl.ds(row0 * n + c, value.shape[0], stride=n), :] = value[:, c * LANES:(c + 1) * LANES]


def _load_slabs(ref, row0=0, m=None):
    rows, n, _ = ref.shape
    m = rows if m is None else m
    flat = ref.reshape(rows * n, LANES)
    return jnp.concatenate([flat[pl.ds(row0 * n + c, m, stride=n), :] for c in range(n)], axis=1)


_A0, _A1 = 0, 512
_B0, _B1 = 512, 1280
_C0, _C1 = 1280, 1792
_D0, _D1 = 1792, 4864


def _rotate_half(x, d, lo, hi):
    n = x.shape[1]
    half = (hi - lo) // 2
    lane = lax.broadcasted_iota(I32, x.shape, 1) % d
    up = pltpu.roll(x, n - half, axis=1)
    dn = pltpu.roll(x, half, axis=1)
    return jnp.where((lane >= lo) & (lane < lo + half), -up, jnp.where((lane >= lo + half) & (lane < hi), dn, 0.0))


def _proj_kernel(x_ref, pos_ref, fq_ref, f64_ref, gmix_ref, win_ref, gq_ref, wq_ref,
                 gkv_ref, wk_ref, wv_ref, bgate_ref,
                 qm_ref, km_ref, vm_ref, qs_ref, ksa_ref, ksb_ref, vsa_ref, vsb_ref, qx_ref, gt_ref):
    for hf in range(x_ref.shape[0] // PROJ_SUB):
        rows = slice(hf * PROJ_SUB, (hf + 1) * PROJ_SUB)
        _proj_rows(x_ref[rows, :], pos_ref[:, rows], fq_ref, f64_ref, gmix_ref, win_ref, gq_ref, wq_ref, gkv_ref,
                   wk_ref, wv_ref, bgate_ref,
                   [r.at[rows] for r in (qm_ref, km_ref)], vm_ref.at[:, rows],
                   [r.at[rows] for r in (qs_ref, ksa_ref, ksb_ref, vsa_ref, vsb_ref, qx_ref, gt_ref)])


def _proj_rows(x, pos, fq_ref, f64_ref, gmix_ref, win_ref, gq_ref, wq_ref, gkv_ref, wk_ref, wv_ref, bgate_ref,
               mla_refs, vm_ref, other_refs):
    qm_ref, km_ref = mla_refs
    qs_ref, ksa_ref, ksb_ref, vsa_ref, vsb_ref, qx_ref, gt_ref = other_refs
    xn = _rms(x, gmix_ref[...]).astype(BF16)
    tm = x.shape[0]
    ang16 = fq_ref[...] * pos
    ang32 = f64_ref[...] * pos
    c16, s16, c32, s32 = jnp.cos(ang16), jnp.sin(ang16), jnp.cos(ang32), jnp.sin(ang32)
    one = jnp.ones((MLA_NOPE, tm), F32)
    zero = jnp.zeros((MLA_NOPE, tm), F32)
    pad = LANES - MLA_NOPE - MLA_ROPE
    cosq = jnp.concatenate([one, c16, c16, one[:pad]], axis=0).T
    sinq = jnp.concatenate([zero, s16, s16, zero[:pad]], axis=0).T
    cos64 = jnp.concatenate([c32, c32, c32, c32], axis=0).T
    sin64 = jnp.concatenate([s32, s32, s32, s32], axis=0).T
    rope_lo, rope_hi = MLA_NOPE, MLA_NOPE + MLA_ROPE

    xa = jnp.dot(xn, win_ref[:, _A0:_A1], preferred_element_type=F32)
    cqn = _rms(xa[:, 0:256], gq_ref[...]).astype(BF16)
    qa = jnp.dot(cqn, wq_ref[...], preferred_element_type=F32)
    qb = _rotate_half(qa, LANES, rope_lo, rope_hi)
    q_scale = LOG2E / math.sqrt(MLA_NOPE + MLA_ROPE)
    ckvn = _rms(xa[:, 256:384], gkv_ref[...]).astype(BF16)
    ka = jnp.dot(ckvn, wk_ref[...], preferred_element_type=F32)
    kr = xa[:, 384:512]
    krope = kr * cosq + _rotate_half(kr, LANES, rope_lo, rope_hi) * sinq
    for h in range(MLA_HEADS):
        sl = slice(h * LANES, (h + 1) * LANES)
        qm_ref[:, sl] = ((qa[:, sl] * cosq + qb[:, sl] * sinq) * q_scale).astype(BF16)
        km_ref[:, sl] = (ka[:, sl] + krope).astype(BF16)
    vm_ref[...] = lax.dot_general(wv_ref[...], ckvn, (((1,), (1,)), ((), ())),
                                  preferred_element_type=F32).astype(BF16)

    xb = jnp.dot(xn, win_ref[:, _B0:_B1], preferred_element_type=F32)
    s_scale = LOG2E / math.sqrt(SWA_HEAD_DIM)
    nq = SWA_HEADS * SWA_HEAD_DIM
    qs = xb[:, :nq]
    qs_rot = _rotate_half(qs, SWA_HEAD_DIM, 0, SWA_HEAD_DIM)
    for p in range(SWA_HEADS // 2):
        sl = slice(p * LANES, (p + 1) * LANES)
        qs_ref[:, sl] = ((qs[:, sl] * cos64 + qs_rot[:, sl] * sin64) * s_scale).astype(BF16)
    ks = xb[:, nq:nq + LANES]
    ks = ks * cos64 + _rotate_half(ks, SWA_HEAD_DIM, 0, SWA_HEAD_DIM) * sin64
    ksa_ref[...] = ks.astype(BF16)
    ksb_ref[...] = pltpu.roll(ks, 64, axis=1).astype(BF16)
    vs = xb[:, nq + LANES:nq + 2 * LANES]
    vsa_ref[...] = vs.astype(BF16)
    vsb_ref[...] = pltpu.roll(vs, 64, axis=1).astype(BF16)

    xc = jnp.dot(xn, win_ref[:, _C0:_C1], preferred_element_type=F32)
    qx_ref[...] = (xc * (LOG2E / math.sqrt(XA_HEAD_DIM))).astype(BF16)

    xd = jnp.dot(xn, win_ref[:, _D0:_D1], preferred_element_type=F32) + bgate_ref[...]
    gt_ref[...] = jax.nn.sigmoid(xd).astype(BF16)


def _proj_call(x2, pos, fq, f64, g_mix, w_in_al, g_q, wq, g_kv, wk_aug, wv, b_gate):
    t, d = x2.shape
    tm = PROJ_TM
    row = lambda n: pl.BlockSpec((tm, n), lambda i: (i, 0))
    full = lambda a: pl.BlockSpec(a.shape, lambda i: (0,) * a.ndim)
    out_cols = [1024, 1024, 512, 128, 128, 128, 128, 512, 3072]
    out_specs = [row(n) for n in out_cols]
    out_shape = [jax.ShapeDtypeStruct((t, n), BF16) for n in out_cols]
    vt_rows = wv.shape[0]
    out_specs.insert(2, pl.BlockSpec((vt_rows, tm), lambda i: (0, i)))
    out_shape.insert(2, jax.ShapeDtypeStruct((vt_rows, t), BF16))
    return pl.pallas_call(
        _proj_kernel,
        grid=(t // tm,),
        in_specs=[row(d), pl.BlockSpec((1, tm), lambda i: (0, i)), full(fq), full(f64), full(g_mix),
                  pl.BlockSpec(w_in_al.shape, lambda i: (0, 0), pipeline_mode=pl.Buffered(1)),
                  full(g_q), full(wq), full(g_kv), full(wk_aug), full(wv), full(b_gate)],
        out_specs=out_specs,
        out_shape=out_shape,
        compiler_params=pltpu.CompilerParams(dimension_semantics=("arbitrary",), vmem_limit_bytes=VMEM_LIMIT),
        name="proj",
    )(x2, pos, fq, f64, g_mix, w_in_al, g_q, wq, g_kv, wk_aug, wv, b_gate)


def _mla_kernel(q_ref, k_ref, vt_ref, o_ref, sa_ref, sb_ref, ma_ref, mb_ref, m_ref, acc_ref, *, tq):
    i = pl.program_id(2)
    m_ref[...] = jnp.full(m_ref.shape, NEG_BIG, F32)
    acc_ref[...] = jnp.zeros(acc_ref.shape, F32)
    ones = jnp.ones((MLA_ONES_ROWS, tq), BF16)

    def scores(j, bufs):
        s_ref, mx_ref = bufs
        k0 = pl.multiple_of(j * tq, tq)
        for hh in range(2):
            sl = slice(hh * LANES, (hh + 1) * LANES)
            st = lax.dot_general(k_ref[0, pl.ds(k0, tq), sl], q_ref[0, :, sl], (((1,), (1,)), ((), ())),
                                 preferred_element_type=F32)
            s_ref[hh] = st
            mx_ref[hh] = jnp.max(st, axis=0, keepdims=True)

    def update(j, bufs, masked):
        s_ref, mx_ref = bufs
        k0 = pl.multiple_of(j * tq, tq)
        hq = tq // 2
        parts = ((slice(0, hq), hq, 0), (slice(hq, tq), tq, hq)) if masked else ((slice(0, tq), tq, 0),)
        for hh in range(2):
            for cols, nk, q0 in parts:
                nq = cols.stop - cols.start
                vt = jnp.concatenate([vt_ref[hh * MLA_V:(hh + 1) * MLA_V, pl.ds(k0, nk)], ones[:, :nk]], axis=0)
                st = s_ref[hh, 0:nk, cols]
                if masked:
                    kj = lax.broadcasted_iota(I32, (nk, nq), 0)
                    qi = lax.broadcasted_iota(I32, (nk, nq), 1) + q0
                    st = jnp.where(kj <= qi, st, NEG_BIG)
                    m_tile = jnp.max(st, axis=0, keepdims=True)
                else:
                    m_tile = mx_ref[hh]
                m_old = m_ref[hh, :, cols]
                m_new = jnp.maximum(m_old, m_tile)
                alpha = jnp.exp2(m_old - m_new)
                pt = jnp.exp2(st - m_new)
                acc_ref[hh, :, cols] = (alpha * acc_ref[hh, :, cols]
                                        + jnp.dot(vt, pt.astype(BF16), preferred_element_type=F32))
                m_ref[hh, :, cols] = m_new

    buf_a = (sa_ref, ma_ref)
    buf_b = (sb_ref, mb_ref)
    scores(0, buf_a)

    def body(jj, carry):
        scores(2 * jj + 1, buf_b)
        update(2 * jj, buf_a, False)
        scores(2 * jj + 2, buf_a)
        update(2 * jj + 1, buf_b, False)
        return carry

    lax.fori_loop(0, i // 2, body, 0)

    @pl.when(i % 2 == 0)
    def _():
        update(i, buf_a, True)

    @pl.when(i % 2 == 1)
    def _():
        scores(i, buf_b)
        update(i - 1, buf_a, False)
        update(i, buf_b, True)

    ot = jnp.concatenate([acc_ref[hh, :MLA_V] / acc_ref[hh, MLA_V:MLA_V + 1] for hh in range(2)],
                         axis=0)
    o_ref[0] = ot.T.astype(BF16)


def _mla_call(q, k, vt, seq):
    b, s, _ = q.shape
    assert s == seq
    tq = min(MLA_TQ, s)
    n_pairs = MLA_HEADS // 2
    return pl.pallas_call(
        functools.partial(_mla_kernel, tq=tq),
        grid=(b, n_pairs, s // tq),
        in_specs=[pl.BlockSpec((1, tq, 2 * LANES), lambda bi, hp, i: (bi, i, hp)),
                  pl.BlockSpec((1, s, 2 * LANES), lambda bi, hp, i: (bi, 0, hp)),
                  pl.BlockSpec((2 * MLA_V, s), lambda bi, hp, i: (hp, bi))],
        out_specs=pl.BlockSpec((1, tq, LANES), lambda bi, hp, i: (bi, i, hp)),
        out_shape=jax.ShapeDtypeStruct((b, s, n_pairs * LANES), BF16),
        scratch_shapes=[pltpu.VMEM((2, tq, tq), F32), pltpu.VMEM((2, tq, tq), F32),
                        pltpu.VMEM((2, 1, tq), F32), pltpu.VMEM((2, 1, tq), F32),
                        pltpu.VMEM((2, 1, tq), F32),
                        pltpu.VMEM((2, MLA_V + MLA_ONES_ROWS, tq), F32)],
        compiler_params=pltpu.CompilerParams(dimension_semantics=("arbitrary",) * 3, vmem_limit_bytes=VMEM_LIMIT),
        name="mla_attn",
    )(q, k, vt)


def _swa_kernel(sink_ref, q_ref, ka_ref, kb_ref, va_ref, vb_ref, kah_ref, kbh_ref, vah_ref, vbh_ref, o_ref, *, ts):
    w = SWA_WINDOW
    i = pl.program_id(1)
    ka = jnp.concatenate([kah_ref[0], ka_ref[0]], axis=0)
    kb = jnp.concatenate([kbh_ref[0], kb_ref[0]], axis=0)
    va = jnp.concatenate([vah_ref[0], va_ref[0]], axis=0)
    vb = jnp.concatenate([vbh_ref[0], vb_ref[0]], axis=0)
    lane_k = lax.broadcasted_iota(I32, (2 * w, LANES), 1)
    low = lane_k < SWA_HEAD_DIM
    qi = lax.broadcasted_iota(I32, (2 * w, 2 * w), 0) % w
    kj = lax.broadcasted_iota(I32, (2 * w, 2 * w), 1)
    diff = qi + w - kj
    band = (diff >= 0) & (diff < w)
    lane_o = lax.broadcasted_iota(I32, (w, LANES), 1)
    row2 = lax.broadcasted_iota(I32, (2 * w, 1), 0)
    zero = jnp.zeros((), BF16)
    stacks = ((0, ka, True, va), (1, kb, False, vb), (4, kb, True, vb), (5, ka, False, va))
    for n in range(ts // w):
        mask = band & ((i * (ts // w) + n > 0) | (kj >= w))
        res = []
        for h0, ksrc, keep_low, vsrc in stacks:
            p0 = h0 // 2
            q = jnp.concatenate([q_ref[0, n * w:(n + 1) * w, p0 * LANES:(p0 + 1) * LANES],
                                 q_ref[0, n * w:(n + 1) * w, (p0 + 1) * LANES:(p0 + 2) * LANES]], axis=0)
            kwin = ksrc[n * w:n * w + 2 * w]
            kwin = jnp.where(low if keep_low else ~low, kwin, zero)
            vwin = vsrc[n * w:n * w + 2 * w]
            s = lax.dot_general(q, kwin, (((1,), (1,)), ((), ())), preferred_element_type=F32)
            s = jnp.where(mask, s, NEG_BIG)
            sink = jnp.where(row2 < w, sink_ref[h0], sink_ref[h0 + 2]) * LOG2E
            m = jnp.maximum(jnp.max(s, axis=1, keepdims=True), sink)
            p = jnp.exp2(s - m)
            den = jnp.sum(p, axis=1, keepdims=True) + jnp.exp2(sink - m)
            o = jnp.dot(p.astype(BF16), vwin, preferred_element_type=F32) / den
            res.append(o)
        o02, o13, o46, o57 = res
        sel = lane_o < SWA_HEAD_DIM
        rows = slice(n * w, (n + 1) * w)
        o_ref[0, rows, 0 * LANES:1 * LANES] = jnp.where(sel, o02[:w], o13[:w]).astype(BF16)
        o_ref[0, rows, 1 * LANES:2 * LANES] = jnp.where(sel, o02[w:], o13[w:]).astype(BF16)
        o_ref[0, rows, 2 * LANES:3 * LANES] = jnp.where(sel, o46[:w], o57[:w]).astype(BF16)
        o_ref[0, rows, 3 * LANES:4 * LANES] = jnp.where(sel, o46[w:], o57[w:]).astype(BF16)


def _swa_call(sinks, q, ksa, ksb, vsa, vsb):
    b, s, _ = q.shape
    ts = min(SWA_TS, s)
    w = SWA_WINDOW
    r = ts // w
    main = pl.BlockSpec((1, ts, LANES), lambda bi, i: (bi, i, 0))
    halo = pl.BlockSpec((1, w, LANES), lambda bi, i: (bi, jnp.maximum(i * r - 1, 0), 0))
    return pl.pallas_call(
        functools.partial(_swa_kernel, ts=ts),
        grid=(b, s // ts),
        in_specs=[pl.BlockSpec(memory_space=pltpu.SMEM),
                  pl.BlockSpec((1, ts, 4 * LANES), lambda bi, i: (bi, i, 0)),
                  main, main, main, main, halo, halo, halo, halo],
        out_specs=pl.BlockSpec((1, ts, 4 * LANES), lambda bi, i: (bi, i, 0)),
        out_shape=jax.ShapeDtypeStruct((b, s, 4 * LANES), BF16),
        compiler_params=pltpu.CompilerParams(dimension_semantics=("arbitrary",) * 2, vmem_limit_bytes=VMEM_LIMIT),
        name="swa_attn",
    )(sinks, q, ksa, ksb, vsa, vsb, ksa, ksb, vsa, vsb)


def _memkv_kernel(mem_ref, g_ref, w_ref, o_ref):
    mn = _rms(mem_ref[...], g_ref[...]).astype(BF16)
    o_ref[...] = jnp.dot(mn, w_ref[...], preferred_element_type=F32).astype(BF16)


def _memkv_call(mem2, g_mem, w_mem_kv):
    n, d = mem2.shape
    tm = min(256, n)
    return pl.pallas_call(
        _memkv_kernel,
        grid=(n // tm,),
        in_specs=[pl.BlockSpec((tm, d), lambda i: (i, 0)),
                  pl.BlockSpec(g_mem.shape, lambda i: (0, 0)),
                  pl.BlockSpec(w_mem_kv.shape, lambda i: (0, 0))],
        out_specs=pl.BlockSpec((tm, w_mem_kv.shape[1]), lambda i: (i, 0)),
        out_shape=jax.ShapeDtypeStruct((n, w_mem_kv.shape[1]), BF16),
        compiler_params=pltpu.CompilerParams(dimension_semantics=("arbitrary",), vmem_limit_bytes=VMEM_LIMIT),
        name="mem_kv",
    )(mem2, g_mem, w_mem_kv)


def _merge_kernel(x_ref, omla_ref, oswa_ref, qx_ref, gt_ref, kvm_ref, wmo_ref, wso_ref, wxo_ref, wout_ref,
                  gffn_ref, wr_ref, br_ref,
                  h_ref, hnp_ref, idx_ref, gate_ref, rank_ref, cnt_ref, run_ref, *, tm, sub):
    @pl.when(pl.program_id(0) == 0)
    def _():
        run_ref[...] = jnp.zeros(run_ref.shape, F32)

    d = x_ref.shape[1]
    kv_cols = XA_HEADS * XA_HEAD_DIM
    erow = lax.broadcasted_iota(I32, (N_EXPERTS, sub), 0)
    tri_t = (lax.broadcasted_iota(I32, (sub, sub), 0) < lax.broadcasted_iota(I32, (sub, sub), 1)).astype(BF16)
    nt = (((1,), (1,)), ((), ()))
    run = run_ref[...]
    for hf in range(tm // sub):
        rows = slice(hf * sub, (hf + 1) * sub)

        oxs = []
        for hd in range(XA_HEADS):
            sl = slice(hd * LANES, (hd + 1) * LANES)
            km = kvm_ref[0, :, sl]
            vm = kvm_ref[0, :, kv_cols + hd * LANES:kv_cols + (hd + 1) * LANES]
            s = lax.dot_general(qx_ref[rows, sl], km, nt, preferred_element_type=F32)
            p = jnp.exp2(s - jnp.max(s, axis=1, keepdims=True))
            den = jnp.sum(p, axis=1, keepdims=True)
            oxs.append((jnp.dot(p.astype(BF16), vm, preferred_element_type=F32) / den).astype(BF16))
        oxa = jnp.concatenate(oxs, axis=1)

        merged = (gt_ref[rows, 0:d].astype(F32) * jnp.dot(omla_ref[rows, :], wmo_ref[...], preferred_element_type=F32)
                  + gt_ref[rows, d:2 * d].astype(F32) * jnp.dot(oswa_ref[rows, :], wso_ref[...],
                                                                 preferred_element_type=F32)
                  + gt_ref[rows, 2 * d:3 * d].astype(F32) * jnp.dot(oxa, wxo_ref[...], preferred_element_type=F32))
        h = x_ref[rows, :] + jnp.dot(merged.astype(BF16), wout_ref[...], preferred_element_type=F32)
        h_ref[rows, :] = h

        hn = _rms(h, gffn_ref[...])
        hn_hi = hn.astype(BF16)
        hn_hi32 = hn_hi.astype(F32)
        hn_lo = (hn - hn_hi32).astype(BF16)
        bits = pltpu.bitcast(hn_hi32, U32)
        _store_slabs(hnp_ref, hf * sub, (bits[:, : d // 2] >> 16) | (bits[:, d // 2:] & jnp.uint32(0xFFFF0000)))

        part = lax.dot_general(wr_ref[...], hn_hi, nt, preferred_element_type=F32)
        logits_t = (part[:N_EXPERTS] + part[N_EXPERTS:]
                    + lax.dot_general(wr_ref[0:N_EXPERTS, :], hn_lo, nt, preferred_element_type=F32) + br_ref[...])

        work = logits_t
        vals, idxs, hots = [], [], []
        for _ in range(TOP_K):
            mx = jnp.max(work, axis=0, keepdims=True)
            ix = jnp.min(jnp.where(work == mx, erow, N_EXPERTS), axis=0, keepdims=True)
            hot = erow == ix
            work = jnp.where(hot, -jnp.inf, work)
            vals.append(mx)
            idxs.append(ix)
            hots.append(hot)
        es = [jnp.exp(v - vals[0]) for v in vals]
        den = es[0] + es[1] + es[2] + es[3]
        sel_t = (hots[0] | hots[1] | hots[2] | hots[3])
        prefix_t = jnp.dot(sel_t.astype(BF16), tri_t, preferred_element_type=F32) + run
        for k in range(TOP_K):
            idx_ref[k:k + 1, rows] = idxs[k]
            gate_ref[k:k + 1, rows] = es[k] / den
            rank_ref[k:k + 1, rows] = jnp.sum(jnp.where(hots[k], prefix_t, 0.0), axis=0, keepdims=True).astype(I32)
        run = run + jnp.sum(sel_t.astype(F32), axis=1, keepdims=True)
    run_ref[...] = run
    cnt_ref[...] = run.astype(I32)


def _merge_call(x2, omla, oswa, qx, gates, kvm, wmo, wso, wxo, wout, g_ffn, wr_split, b_router_col, seq):
    t, d = x2.shape
    tm = MERGE_TM
    per_b = seq // tm
    row = lambda n: pl.BlockSpec((tm, n), lambda i: (i, 0))
    col = lambda: pl.BlockSpec((TOP_K, tm), lambda i: (0, i))
    full = lambda a: pl.BlockSpec(a.shape, lambda i: (0,) * a.ndim)
    return pl.pallas_call(
        functools.partial(_merge_kernel, tm=tm, sub=MERGE_SUB),
        grid=(t // tm,),
        in_specs=[row(d), row(512), row(512), row(512), row(3 * d),
                  pl.BlockSpec((1,) + kvm.shape[1:], lambda i: (i // per_b, 0, 0)),
                  full(wmo), full(wso), full(wxo), full(wout), full(g_ffn), full(wr_split), full(b_router_col)],
        out_specs=[row(d), pl.BlockSpec((tm, d // 2 // LANES, LANES), lambda i: (i, 0, 0)), col(), col(), col(),
                   pl.BlockSpec((N_EXPERTS, 1), lambda i: (0, 0))],
        out_shape=[jax.ShapeDtypeStruct((t, d), F32), jax.ShapeDtypeStruct((t, d // 2 // LANES, LANES), U32),
                   jax.ShapeDtypeStruct((TOP_K, t), I32), jax.ShapeDtypeStruct((TOP_K, t), F32),
                   jax.ShapeDtypeStruct((TOP_K, t), I32), jax.ShapeDtypeStruct((N_EXPERTS, 1), I32)],
        scratch_shapes=[pltpu.VMEM((N_EXPERTS, 1), F32)],
        compiler_params=pltpu.CompilerParams(dimension_semantics=("arbitrary",), vmem_limit_bytes=VMEM_LIMIT),
        name="merge_router",
    )(x2, omla, oswa, qx, gates, kvm, wmo, wso, wxo, wout, g_ffn, wr_split, b_router_col)


SC_CORES = 2
SC_SUBCORES = 16
SC_WORKERS = SC_CORES * SC_SUBCORES
SC_CHUNK = 64


def _sc_mesh():
    return plsc.VectorSubcoreMesh(core_axis_name="c", subcore_axis_name="s",
                                  num_cores=SC_CORES, num_subcores=SC_SUBCORES)


def _sc_index_blocks(idx):
    n = idx.shape[0]
    per_w = n // SC_WORKERS
    n_ch = per_w // SC_CHUNK
    assert per_w * SC_WORKERS == n and n_ch * SC_CHUNK == per_w
    return idx.reshape(SC_WORKERS, n_ch, SC_CHUNK), per_w, n_ch


def _sc_two_buffer_loop(n_ch, load, stores, bufs, load_sems, store_sems):
    assert n_ch % 2 == 0
    a, b = bufs
    la, lb = load_sems
    sa, sb = store_sems

    def start(cps):
        for cp in cps:
            cp.start()

    def wait(cps):
        for cp in cps:
            cp.wait()

    load(0, a, la).start()

    @pl.loop(0, n_ch, step=2)
    def _(j):
        load(j, a, la).wait()

        @pl.when(j > 0)
        def _():
            wait(stores(j - 1, b, sb))

        load(j + 1, b, lb).start()
        start(stores(j, a, sa))
        load(j + 1, b, lb).wait()
        wait(stores(j, a, sa))

        @pl.when(j + 2 < n_ch)
        def _():
            load(j + 2, a, la).start()

        start(stores(j + 1, b, sb))

    wait(stores(n_ch - 1, b, sb))


def _sc_scratch(n_ch, row_shape, dtype):
    return [pltpu.VMEM((n_ch, SC_CHUNK), I32), pltpu.VMEM((SC_CHUNK,) + row_shape, dtype),
            pltpu.VMEM((SC_CHUNK,) + row_shape, dtype)] + [pltpu.SemaphoreType.DMA] * 4


def _sc_scatter_rows(src, idx, n_out):
    fan, n_src = idx.shape
    per_w = n_src // SC_WORKERS
    n_ch = per_w // SC_CHUNK
    assert per_w * SC_WORKERS == n_src and n_ch * SC_CHUNK == per_w
    idx3 = idx.reshape(fan, SC_WORKERS, n_ch, SC_CHUNK).transpose(1, 0, 2, 3).reshape(SC_WORKERS, fan * n_ch, SC_CHUNK)

    @functools.partial(
        pl.kernel, mesh=_sc_mesh(),
        out_type=jax.ShapeDtypeStruct((n_out,) + src.shape[1:], src.dtype),
        scratch_types=_sc_scratch(fan * n_ch, src.shape[1:], src.dtype),
        name="moe_dispatch_sc")
    def k(src_hbm, idx_hbm, out_hbm, idx_v, rows_a, rows_b, la, lb, sa, sb):
        wid = lax.axis_index("s") * SC_CORES + lax.axis_index("c")
        base = wid * per_w
        pltpu.sync_copy(idx_hbm.at[wid], idx_v)

        def load(c, buf, sem):
            return pltpu.make_async_copy(src_hbm.at[pl.ds(base + c * SC_CHUNK, SC_CHUNK)], buf, sem)

        def stores(c, buf, sem):
            return tuple(pltpu.make_async_copy(buf, out_hbm.at[idx_v.at[f * n_ch + c]], sem) for f in range(fan))

        _sc_two_buffer_loop(n_ch, load, stores, (rows_a, rows_b), (la, lb), (sa, sb))

    return k(src, idx3)


def _sc_gather_rows(table, idx):
    idx3, per_w, n_ch = _sc_index_blocks(idx)

    @functools.partial(
        pl.kernel, mesh=_sc_mesh(),
        out_type=jax.ShapeDtypeStruct((idx.shape[0],) + table.shape[1:], table.dtype),
        scratch_types=_sc_scratch(n_ch, table.shape[1:], table.dtype),
        name="moe_gather_sc")
    def k(table_hbm, idx_hbm, out_hbm, idx_v, rows_a, rows_b, la, lb, sa, sb):
        wid = lax.axis_index("s") * SC_CORES + lax.axis_index("c")
        base = wid * per_w
        pltpu.sync_copy(idx_hbm.at[wid], idx_v)

        def load(c, buf, sem):
            return pltpu.make_async_copy(table_hbm.at[idx_v.at[c]], buf, sem)

        def stores(c, buf, sem):
            return (pltpu.make_async_copy(buf, out_hbm.at[pl.ds(base + c * SC_CHUNK, SC_CHUNK)], sem),)

        _sc_two_buffer_loop(n_ch, load, stores, (rows_a, rows_b), (la, lb), (sa, sb))

    return k(table, idx3)


def _unpack_lo(w):
    return pltpu.bitcast(w << 16, F32)


def _unpack_hi(w):
    return pltpu.bitcast(w & jnp.uint32(0xFFFF0000), F32)


def _ffn_kernel(te_ref, xs_ref, wgu_hbm, bgu_ref, wd_hbm, bd_ref, y_ref, wgu_f32, wd_f32, wgu_bf, wd_bf, wsem):
    i = pl.program_id(0)
    n = pl.num_programs(0)
    n_used = te_ref[n]
    e = te_ref[i]
    first = (i == 0) | (e != te_ref[jnp.maximum(i - 1, 0)])
    slot = te_ref[2 * n + 1 + i] % 2
    nxt = te_ref[3 * n + 1 + i]

    def fetch(expert, s):
        return (pltpu.make_async_copy(wgu_hbm.at[expert], wgu_f32.at[s], wsem.at[0, s]),
                pltpu.make_async_copy(wd_hbm.at[expert], wd_f32.at[s], wsem.at[1, s]))

    @pl.when(i == 0)
    def _():
        for cp in fetch(e, slot):
            cp.start()

    @pl.when((i < n_used) & first)
    def _():
        for cp in fetch(e, slot):
            cp.wait()
        wgu_bf[...] = wgu_f32[slot].astype(BF16)
        wd_bf[...] = wd_f32[slot].astype(BF16)

        @pl.when(nxt >= 0)
        def _():
            for cp in fetch(nxt, 1 - slot):
                cp.start()

    valid = jnp.where(i < n_used, te_ref[n + 1 + i], 0)
    rows, n_slab, _ = xs_ref.shape
    half = n_slab * LANES
    for sb in range(rows // FFN_SUB):
        r0 = sb * FFN_SUB

        @pl.when(valid > r0)
        def _():
            w = _load_slabs(xs_ref, r0, FFN_SUB)
            w = jnp.where(lax.broadcasted_iota(I32, w.shape, 0) < valid - r0, w, jnp.uint32(0))
            x_lo = _unpack_lo(w).astype(BF16)
            x_hi = _unpack_hi(w).astype(BF16)
            gu = (jnp.dot(x_lo, wgu_bf[0:half, :], preferred_element_type=F32)
                  + jnp.dot(x_hi, wgu_bf[half:, :], preferred_element_type=F32) + bgu_ref[0])
            de = gu.shape[1] // 2
            x_glu = jnp.minimum(gu[:, :de], SWIGLU_LIMIT)
            x_lin = jnp.clip(gu[:, de:], -SWIGLU_LIMIT, SWIGLU_LIMIT)
            hdn = x_glu * jax.nn.sigmoid(SWIGLU_ALPHA * x_glu) * (x_lin + 1.0)
            y = jnp.dot(hdn.astype(BF16), wd_bf[...], preferred_element_type=F32) + bd_ref[0]
            bits = pltpu.bitcast(y.astype(BF16).astype(F32), U32)
            _store_slabs(y_ref, r0, (bits[:, :half] >> 16) | (bits[:, half:] & jnp.uint32(0xFFFF0000)))

        @pl.when(valid <= r0)
        def _():
            y_ref[r0:r0 + FFN_SUB] = jnp.zeros((FFN_SUB,) + y_ref.shape[1:], U32)


def _ffn_call(tile_table, xs, w_gate_up, b_gate_up, w_down, b_down):
    r, ns, lanes = xs.shape
    tm = FFN_TM
    ne, d, de2 = w_gate_up.shape
    return pl.pallas_call(
        _ffn_kernel,
        grid_spec=pltpu.PrefetchScalarGridSpec(
            num_scalar_prefetch=1,
            grid=(r // tm,),
            in_specs=[pl.BlockSpec((tm, ns, lanes),
                                   lambda i, te: (jnp.minimum(i, jnp.maximum(te[r // tm] - 1, 0)), 0, 0)),
                      pl.BlockSpec(memory_space=pl.ANY),
                      pl.BlockSpec((1, 1, de2), lambda i, te: (te[i], 0, 0)),
                      pl.BlockSpec(memory_space=pl.ANY),
                      pl.BlockSpec((1, 1, d), lambda i, te: (te[i], 0, 0))],
            out_specs=pl.BlockSpec((tm, ns, lanes), lambda i, te: (i, 0, 0)),
            scratch_shapes=[pltpu.VMEM((2, d, de2), F32), pltpu.VMEM((2, de2 // 2, d), F32),
                            pltpu.VMEM((d, de2), BF16), pltpu.VMEM((de2 // 2, d), BF16),
                            pltpu.SemaphoreType.DMA((2, 2))]),
        out_shape=jax.ShapeDtypeStruct((r, ns, lanes), U32),
        compiler_params=pltpu.CompilerParams(dimension_semantics=("arbitrary",), vmem_limit_bytes=VMEM_LIMIT),
        name="moe_ffn",
    )(tile_table, xs, w_gate_up, b_gate_up, w_down, b_down)


def _combine_dense_kernel(h_ref, gate_ref, gfin_ref, y0_ref, y1_ref, y2_ref, y3_ref, o_ref):
    half = y0_ref.shape[1] * y0_ref.shape[2]
    lo = h_ref[:, :half]
    hi = h_ref[:, half:]
    gates_kt = gate_ref[...]
    gates_tk = jnp.concatenate([gates_kt, jnp.zeros((SUBLANES - TOP_K, gates_kt.shape[1]), F32)], axis=0).T
    for k, y_ref in enumerate((y0_ref, y1_ref, y2_ref, y3_ref)):
        g = gates_tk[:, k:k + 1]
        w = _load_slabs(y_ref)
        lo = lo + g * _unpack_lo(w)
        hi = hi + g * _unpack_hi(w)
    ms = (jnp.sum(lo * lo, axis=1, keepdims=True) + jnp.sum(hi * hi, axis=1, keepdims=True)) / (2 * half)
    inv = lax.rsqrt(ms + RMS_EPS)
    o_ref[:, :half] = lo * inv * gfin_ref[:, :half]
    o_ref[:, half:] = hi * inv * gfin_ref[:, half:]


def _combine_dense_into_kernel(h_ref, gate_ref, gfin_ref, y0_ref, y1_ref, y2_ref, y3_ref, prev_ref, o_ref):
    del prev_ref
    _combine_dense_kernel(h_ref, gate_ref, gfin_ref, y0_ref, y1_ref, y2_ref, y3_ref, o_ref)


def _combine_dense_call(h, gate, g_final, yk, part, n_parts, prev_out):
    t, d = h.shape
    tm = COMBINE_TM
    steps = t // tm // n_parts
    first = part * steps
    yspec = lambda k: pl.BlockSpec((tm,) + yk.shape[1:], lambda i: (k * steps + i, 0, 0))
    in_specs = [pl.BlockSpec((tm, d), lambda i: (first + i, 0)),
                pl.BlockSpec((TOP_K, tm), lambda i: (0, first + i)),
                pl.BlockSpec((1, d), lambda i: (0, 0)),
                yspec(0), yspec(1), yspec(2), yspec(3)]
    args = [h, gate, g_final, yk, yk, yk, yk]
    aliases = {}
    body = _combine_dense_kernel
    if prev_out is not None:
        in_specs.append(pl.BlockSpec(memory_space=pl.ANY))
        args.append(prev_out)
        aliases = {len(args) - 1: 0}
        body = _combine_dense_into_kernel
    return pl.pallas_call(
        body,
        grid=(steps,),
        in_specs=in_specs,
        out_specs=pl.BlockSpec((tm, d), lambda i: (first + i, 0)),
        out_shape=jax.ShapeDtypeStruct((t, d), F32),
        input_output_aliases=aliases,
        compiler_params=pltpu.CompilerParams(dimension_semantics=("arbitrary",), vmem_limit_bytes=VMEM_LIMIT),
        name="moe_combine",
    )(*args)


def _rope_freqs():
    def inv_freq(dh):
        return (ROPE_THETA ** (-jnp.arange(0, dh, 2, dtype=F32) / dh))[:, None]

    return inv_freq(MLA_ROPE), inv_freq(SWA_HEAD_DIM)


def _winprep_kernel(wt_ref, o_ref):
    j = pl.program_id(0)
    c1 = MLA_Q_RANK + MLA_KV_RANK
    kr_block = c1 // LANES

    @pl.when(j != kr_block)
    def _():
        row0 = jnp.where(j < kr_block, j * LANES, c1 + MLA_ROPE + (j - kr_block - 1) * LANES)
        o_ref[...] = wt_ref[pl.ds(pl.multiple_of(row0, SUBLANES), LANES), :].T.astype(BF16)

    @pl.when(j == kr_block)
    def _():
        d = wt_ref.shape[1]
        blk = jnp.concatenate([jnp.zeros((MLA_NOPE, d), F32), wt_ref[c1:c1 + MLA_ROPE, :],
                               jnp.zeros((LANES - MLA_NOPE - MLA_ROPE, d), F32)], axis=0)
        o_ref[...] = blk.T.astype(BF16)


def _winprep_call(w_in, layer):
    _, d, n = w_in.shape
    w_t = jnp.swapaxes(w_in, 1, 2)
    return pl.pallas_call(
        _winprep_kernel,
        grid=(_D1 // LANES,),
        in_specs=[pl.BlockSpec((None, n, d), lambda j: (layer, 0, 0), pipeline_mode=pl.Buffered(1))],
        out_specs=pl.BlockSpec((d, LANES), lambda j: (0, j)),
        out_shape=jax.ShapeDtypeStruct((d, _D1), BF16),
        compiler_params=pltpu.CompilerParams(dimension_semantics=("arbitrary",), vmem_limit_bytes=VMEM_LIMIT),
        name="w_in_prep",
    )(w_t)


def _prep_weights(w_in, layer, w_mla_uq, w_mla_ukv):
    w_in_al = _winprep_call(w_in, layer)

    r = w_mla_uq.shape[0]
    wq = w_mla_uq.reshape(r, MLA_HEADS, MLA_NOPE + MLA_ROPE)
    zq = jnp.zeros((r, MLA_HEADS, LANES - MLA_NOPE - MLA_ROPE), w_mla_uq.dtype)
    wq_pad = jnp.concatenate([wq, zq], axis=-1).reshape(r, MLA_HEADS * LANES).astype(BF16)

    rk = w_mla_ukv.shape[0]
    wkv = w_mla_ukv.reshape(rk, MLA_HEADS, MLA_NOPE + MLA_V)
    wk_aug = jnp.concatenate([wkv[..., :MLA_NOPE], jnp.zeros((rk, MLA_HEADS, LANES - MLA_NOPE), w_mla_ukv.dtype)],
                             axis=-1).reshape(rk, MLA_HEADS * LANES).astype(BF16)
    wv_t = wkv[..., MLA_NOPE:].reshape(rk, MLA_HEADS * MLA_V).T.astype(BF16)
    return w_in_al, wq_pad, wk_aug, wv_t


def kernel(x, mem, positions, g_mix, w_in, g_mla_q, w_mla_uq, g_mla_kv, w_mla_ukv, w_mla_o, swa_sinks, w_swa_o,
           g_mem, w_mem_kv, w_xa_o, b_gate, w_out, g_ffn, w_router, b_router, w_gate_up, b_gate_up, w_down,
           b_down, g_final):
    b, s, d = x.shape
    t = b * s
    depth = g_mix.shape[0]
    h = x.reshape(t, d)
    pos = positions.astype(F32).reshape(1, t)
    fq, f64 = _rope_freqs()
    for l in range(depth):
        w_in_al, wq_pad, wk_aug, wv_t = _prep_weights(w_in, l, w_mla_uq[l], w_mla_ukv[l])
        (qm, km, vmt, qs, ksa, ksb, vsa, vsb, qx, gates) = _proj_call(
            h, pos, fq, f64, g_mix[l][None], w_in_al, g_mla_q[l][None], wq_pad,
            g_mla_kv[l][None], wk_aug, wv_t, b_gate[l][None])
        r3 = lambda a: a.reshape(b, s, a.shape[1])
        omla = _mla_call(r3(qm), r3(km), vmt, s).reshape(t, -1)
        oswa = _swa_call(swa_sinks[l], r3(qs), r3(ksa), r3(ksb), r3(vsa), r3(vsb)).reshape(t, -1)
        m = mem.shape[1]
        kvm = _memkv_call(mem.reshape(b * m, d), g_mem[l][None], w_mem_kv[l].astype(BF16)).reshape(b, m, -1)
        wr_t = w_router[l].T
        wr_hi = wr_t.astype(BF16)
        wr_split = jnp.concatenate([wr_hi, (wr_t - wr_hi.astype(F32)).astype(BF16)], axis=0)
        h_mid, hnp, idx, gate, rank, counts = _merge_call(
            h, omla, oswa, qx, gates, kvm, w_mla_o[l].astype(BF16), w_swa_o[l].astype(BF16),
            w_xa_o[l].astype(BF16), w_out[l].astype(BF16), g_ffn[l][None], wr_split, b_router[l][:, None], s)

        counts = counts[:, 0]
        padded = ((counts + FFN_TM - 1) // FFN_TM) * FFN_TM
        padded_end = jnp.cumsum(padded)
        offsets = padded_end - padded
        experts = jnp.arange(N_EXPERTS, dtype=I32)
        dest = (jnp.sum(jnp.where(idx[..., None] == experts, offsets, 0), axis=-1) + rank).reshape(-1).astype(I32)
        n_tiles = (t * TOP_K) // FFN_TM + N_EXPERTS
        n_used = (padded_end[-1] // FFN_TM).astype(I32)
        tile_start = jnp.minimum(jnp.arange(n_tiles, dtype=I32), jnp.maximum(n_used - 1, 0)) * FFN_TM
        tile_expert = jnp.sum((padded_end[None, :] <= tile_start[:, None]).astype(I32), axis=1)
        tile_expert = jnp.minimum(tile_expert, N_EXPERTS - 1)
        onehot = tile_expert[:, None] == experts[None, :]
        pick = lambda v: jnp.sum(jnp.where(onehot, v[None, :], 0), axis=1)
        tile_valid = jnp.clip(pick(counts) - (tile_start - pick(offsets)), 0, FFN_TM)
        nonempty = padded > 0
        tile_group = pick(jnp.cumsum(nonempty.astype(I32)) - 1)
        later = jnp.where(nonempty[None, :] & (experts[None, :] > experts[:, None]), experts[None, :], N_EXPERTS)
        next_expert = jnp.min(later, axis=1)
        tile_next = pick(jnp.where(next_expert < N_EXPERTS, next_expert, -1))
        te = jnp.concatenate([tile_expert, n_used[None], tile_valid.astype(I32), tile_group.astype(I32),
                              tile_next.astype(I32)])

        xs = _sc_scatter_rows(hnp, dest.reshape(TOP_K, t), n_tiles * FFN_TM)
        y = _ffn_call(te, xs, w_gate_up[l], b_gate_up[l][:, None, :], w_down[l], b_down[l][:, None, :])
        if l == depth - 1:
            gfin = g_final[None]
            dest_parts = dest.reshape(TOP_K, COMBINE_PARTS, t // COMBINE_PARTS)
            out = None
            for part in range(COMBINE_PARTS):
                yk = _sc_gather_rows(y, dest_parts[:, part].reshape(-1))
                out = _combine_dense_call(h_mid, gate, gfin, yk, part, COMBINE_PARTS, out)
        else:
            raise NotImplementedError("depth > 1 needs a combine without the final norm")
        h = out
    return h.reshape(b, s, d)
```
